```python
import jax, jax.numpy as jnp
from jax import lax
import numpy as np

D_MODEL = 2048
BATCH = 8
SEQ = 8192
DEPTH = 1

EPS = 1e-6
HG_HEADS = 16
HG_DK = 128
HG_DV = 128
HG_FDIM = HG_HEADS * HG_DK
HG_VDIM = HG_HEADS * HG_DV
HG_CHUNK = 64
SSM_DINNER = 2 * D_MODEL
SSM_HEADDIM = 64
SSM_HEADS = SSM_DINNER // SSM_HEADDIM
SSM_GROUPS = 8
SSM_HPG = SSM_HEADS // SSM_GROUPS
SSM_DSTATE = 128
SSM_CONV = 4
SSM_CHUNK = 256
SSM_CONV_DIM = SSM_DINNER + 2 * SSM_GROUPS * SSM_DSTATE
D_FF = 5632
FFN_CONV = 3
IN_SIZES = (HG_FDIM, HG_FDIM, HG_VDIM, HG_VDIM,
            SSM_DINNER, SSM_CONV_DIM, SSM_HEADS,
            D_MODEL, D_MODEL)
IN_TOTAL = sum(IN_SIZES)

kernel_name = "hgrn2_mamba2_gated_hybrid_block"


def rmsnorm(x, w):
    xf = x.astype(jnp.float32)
    y = xf * lax.rsqrt(jnp.mean(xf * xf, axis=-1, keepdims=True) + EPS)
    return (y * w.astype(jnp.float32)).astype(x.dtype)


def causal_dwconv(x, w, b):
    k = w.shape[0]
    c = x.shape[-1]
    y = lax.conv_general_dilated(x, w[:, None, :].astype(x.dtype), window_strides=(1,),
                                 padding=[(k - 1, 0)], dimension_numbers=('NWC', 'WIO', 'NWC'),
                                 feature_group_count=c)
    return y + b.astype(x.dtype)


def _to_chunks(a, c):
    b, t = a.shape[:2]
    n = -(-t // c)
    a = jnp.pad(a, ((0, 0), (0, n * c - t)) + ((0, 0),) * (a.ndim - 2))
    return jnp.moveaxis(a.reshape((b, n, c) + a.shape[2:]), 1, 0)


def _from_chunks(y, t):
    n, b, c = y.shape[:3]
    return jnp.moveaxis(y, 0, 1).reshape((b, n * c) + y.shape[3:])[:, :t]


def hgrn2_scan(q, k, v, logf):
    bsz, t, h, dk = q.shape
    dv = v.shape[-1]
    causal = jnp.tril(jnp.ones((HG_CHUNK, HG_CHUNK), dtype=bool))[:, :, None, None]

    def step(S, inp):
        qc, kc, vc, gc = inp
        bcum = jnp.cumsum(gc, axis=1)
        rel = jnp.exp(jnp.where(causal, bcum[:, :, None] - bcum[:, None, :], -jnp.inf))
        scores = jnp.einsum('bthk,bshk,btshk->bhts', qc, kc, rel)
        o = jnp.einsum('bhts,bshv->bthv', scores, vc)
        o = o + jnp.einsum('bthk,bhkv->bthv', qc * jnp.exp(bcum), S)
        b_last = bcum[:, -1]
        S = jnp.exp(b_last)[..., None] * S + jnp.einsum('bshk,bshv->bhkv', kc * jnp.exp(b_last[:, None] - bcum), vc)
        return S, o

    S0 = jnp.zeros((bsz, h, dk, dv), jnp.float32)
    xs = tuple(_to_chunks(a.astype(jnp.float32), HG_CHUNK) for a in (q, k, v, logf))
    _, o = lax.scan(step, S0, xs)
    return _from_chunks(o, t)


def ssd_scan(x, dA, Bm, Cm):
    bsz, t, g, r, p = x.shape
    n = Bm.shape[-1]
    causal = jnp.tril(jnp.ones((SSM_CHUNK, SSM_CHUNK), dtype=bool))[:, :, None, None]

    def step(S, inp):
        xc, ac, bc, cc = inp
        acum = jnp.cumsum(ac, axis=1)
        L = jnp.exp(jnp.where(causal, acum[:, :, None] - acum[:, None, :], -jnp.inf))
        cb = jnp.einsum('btgn,bsgn->bgts', cc, bc)
        y = jnp.einsum('bgts,btsgr,bsgrp->btgrp', cb, L, xc)
        y = y + jnp.einsum('btgn,bgrpn,btgr->btgrp', cc, S, jnp.exp(acum))
        a_last = acum[:, -1]
        S = jnp.exp(a_last)[..., None, None] * S + jnp.einsum(
            'bsgn,bsgr,bsgrp->bgrpn', bc, jnp.exp(a_last[:, None] - acum), xc)
        return S, y

    S0 = jnp.zeros((bsz, g, r, p, n), jnp.float32)
    xs = tuple(_to_chunks(a.astype(jnp.float32), SSM_CHUNK) for a in (x, dA, Bm, Cm))
    _, y = lax.scan(step, S0, xs)
    return _from_chunks(y, t)


def hgrn2_mixer(q, f_raw, i, g, lb, norm_w):
    bsz, t = q.shape[:2]
    f = lb + (1.0 - lb) * jax.nn.sigmoid(f_raw.astype(jnp.float32))
    qh = (jax.nn.silu(q.astype(jnp.float32)) * HG_DK ** -0.5).reshape(bsz, t, HG_HEADS, HG_DK)
    kh = (1.0 - f).reshape(bsz, t, HG_HEADS, HG_DK)
    lfh = jnp.log(f).reshape(bsz, t, HG_HEADS, HG_DK)
    vh = i.reshape(bsz, t, HG_HEADS, HG_DV)
    o = hgrn2_scan(qh, kh, vh, lfh)
    o = rmsnorm(o, norm_w) * jax.nn.silu(g.reshape(bsz, t, HG_HEADS, HG_DV).astype(jnp.float32))
    return o.reshape(bsz, t, HG_VDIM).astype(q.dtype)


def mamba2_mixer(z, xbc, dt_raw, conv_w, conv_b, dt_bias, A_log, D_skip, norm_w):
    bsz, t = z.shape[:2]
    xbc = jax.nn.silu(causal_dwconv(xbc, conv_w, conv_b))
    xs = xbc[..., :SSM_DINNER]
    Bm = xbc[..., SSM_DINNER:SSM_DINNER + SSM_GROUPS * SSM_DSTATE].reshape(bsz, t, SSM_GROUPS, SSM_DSTATE)
    Cm = xbc[..., SSM_DINNER + SSM_GROUPS * SSM_DSTATE:].reshape(bsz, t, SSM_GROUPS, SSM_DSTATE)
    dt = jax.nn.softplus(dt_raw.astype(jnp.float32) + dt_bias.astype(jnp.float32))
    A = -jnp.exp(A_log.astype(jnp.float32))
    xh = xs.astype(jnp.float32).reshape(bsz, t, SSM_GROUPS, SSM_HPG, SSM_HEADDIM)
    dtg = dt.reshape(bsz, t, SSM_GROUPS, SSM_HPG)
    y = ssd_scan(xh * dtg[..., None], dtg * A.reshape(SSM_GROUPS, SSM_HPG), Bm, Cm)
    y = y + D_skip.astype(jnp.float32).reshape(SSM_GROUPS, SSM_HPG, 1) * xh
    y = y.reshape(bsz, t, SSM_DINNER) * jax.nn.silu(z.astype(jnp.float32))
    y = rmsnorm(y.reshape(bsz, t, SSM_GROUPS, SSM_DINNER // SSM_GROUPS),
                norm_w.reshape(SSM_GROUPS, SSM_DINNER // SSM_GROUPS))
    return y.reshape(bsz, t, SSM_DINNER).astype(z.dtype)


def conv_ffn(h, w_up, conv_w, conv_b, w_down):
    gu = jnp.einsum('btd,df->btf', h, w_up)
    gate, up = gu[..., :D_FF], gu[..., D_FF:]
    gate = causal_dwconv(gate, conv_w, conv_b)
    return jnp.einsum('btf,fd->btd', jax.nn.gelu(gate, approximate=True) * up, w_down)


def _fwd_setup_inputs(seed: int = 0) -> dict:
    key = jax.random.key(seed)
    ks = jax.random.split(key, 24)
    nrm = lambda k, shape, s: jax.random.normal(k, shape, jnp.float32) * s
    gain = lambda k, shape: 1.0 + 0.02 * jax.random.normal(k, shape, jnp.float32)
    dt0 = jnp.exp(jax.random.uniform(ks[8], (DEPTH, SSM_HEADS), jnp.float32,
                                     np.log(1e-3), np.log(1e-1)))
    return {
        'x': jax.random.normal(ks[0], (BATCH, SEQ, D_MODEL), jnp.float32),
        'w_in': nrm(ks[1], (DEPTH, D_MODEL, IN_TOTAL), D_MODEL ** -0.5),
        'mix_pre_norm': gain(ks[2], (DEPTH, D_MODEL)),
        'mix_post_norm': gain(ks[3], (DEPTH, D_MODEL)),
        'hg_lb_table': nrm(ks[4], (DEPTH + 1, HG_FDIM), 0.5),
        'hg_out_norm': gain(ks[5], (DEPTH, HG_DV)),
        'ssm_conv_w': nrm(ks[6], (DEPTH, SSM_CONV, SSM_CONV_DIM), SSM_CONV ** -0.5),
        'ssm_conv_b': nrm(ks[7], (DEPTH, SSM_CONV_DIM), 0.01),
        'ssm_dt_bias': dt0 + jnp.log(-jnp.expm1(-dt0)),
        'ssm_A_log': jnp.log(jax.random.uniform(ks[9], (DEPTH, SSM_HEADS), jnp.float32, 1.0, 16.0)),
        'ssm_D': gain(ks[10], (DEPTH, SSM_HEADS)),
        'ssm_out_norm': gain(ks[11], (DEPTH, SSM_DINNER)),
        'w_branch_hg': nrm(ks[12], (DEPTH, HG_VDIM, D_MODEL), HG_VDIM ** -0.5),
        'w_branch_ssm': nrm(ks[13], (DEPTH, SSM_DINNER, D_MODEL), SSM_DINNER ** -0.5),
        'w_out': nrm(ks[14], (DEPTH, D_MODEL, D_MODEL), D_MODEL ** -0.5),
        'ffn_pre_norm': gain(ks[15], (DEPTH, D_MODEL)),
        'ffn_post_norm': gain(ks[16], (DEPTH, D_MODEL)),
        'ffn_w_up': nrm(ks[17], (DEPTH, D_MODEL, 2 * D_FF), D_MODEL ** -0.5),
        'ffn_conv_w': nrm(ks[18], (DEPTH, FFN_CONV, D_FF), FFN_CONV ** -0.5),
        'ffn_conv_b': nrm(ks[19], (DEPTH, D_FF), 0.01),
        'ffn_w_down': nrm(ks[20], (DEPTH, D_FF, D_MODEL), D_FF ** -0.5),
    }


def _fwd_reference(x, w_in, mix_pre_norm, mix_post_norm, hg_lb_table, hg_out_norm, ssm_conv_w, ssm_conv_b,
              ssm_dt_bias, ssm_A_log, ssm_D, ssm_out_norm, w_branch_hg, w_branch_ssm, w_out,
              ffn_pre_norm, ffn_post_norm, ffn_w_up, ffn_conv_w, ffn_conv_b, ffn_w_down):
    lower_bounds = jnp.cumsum(jax.nn.softmax(hg_lb_table.astype(jnp.float32), axis=0), axis=0)
    splits = []
    acc = 0
    for s in IN_SIZES[:-1]:
        acc += s
        splits.append(acc)
    for l in range(DEPTH):
        h = rmsnorm(x, mix_pre_norm[l])
        proj = jnp.einsum('btd,de->bte', h, w_in[l])
        q, f_raw, i, g, z, xbc, dt_raw, gate_hg, gate_ssm = jnp.split(proj, splits, axis=-1)
        y_hg = hgrn2_mixer(q, f_raw, i, g, lower_bounds[l], hg_out_norm[l])
        y_ssm = mamba2_mixer(z, xbc, dt_raw, ssm_conv_w[l], ssm_conv_b[l], ssm_dt_bias[l],
                             ssm_A_log[l], ssm_D[l], ssm_out_norm[l])
        mixed = (jax.nn.sigmoid(gate_hg) * jnp.einsum('btv,vd->btd', y_hg, w_branch_hg[l])
                 + jax.nn.sigmoid(gate_ssm) * jnp.einsum('bte,ed->btd', y_ssm, w_branch_ssm[l]))
        x = x + rmsnorm(jnp.einsum('btd,de->bte', mixed, w_out[l]), mix_post_norm[l])
        h = rmsnorm(x, ffn_pre_norm[l])
        x = x + rmsnorm(conv_ffn(h, ffn_w_up[l], ffn_conv_w[l], ffn_conv_b[l], ffn_w_down[l]), ffn_post_norm[l])
    return x


import jax as _jax
import jax.numpy as _jnp

TWIN_FORMAT = 'train_step'
FWD_PARAMS = ['x', 'w_in', 'mix_pre_norm', 'mix_post_norm', 'hg_lb_table', 'hg_out_norm', 'ssm_conv_w', 'ssm_conv_b', 'ssm_dt_bias', 'ssm_A_log', 'ssm_D', 'ssm_out_norm', 'w_branch_hg', 'w_branch_ssm', 'w_out', 'ffn_pre_norm', 'ffn_post_norm', 'ffn_w_up', 'ffn_conv_w', 'ffn_conv_b', 'ffn_w_down']
TWIN_WEIGHTS = ['w_in', 'mix_pre_norm', 'mix_post_norm', 'hg_lb_table', 'hg_out_norm', 'ssm_conv_w', 'ssm_conv_b', 'ssm_dt_bias', 'ssm_A_log', 'ssm_D', 'ssm_out_norm', 'w_branch_hg', 'w_branch_ssm', 'w_out', 'ffn_pre_norm', 'ffn_post_norm', 'ffn_w_up', 'ffn_conv_w', 'ffn_conv_b', 'ffn_w_down']
TWIN_DIFF_INPUT = 'x'
TWIN_INPUTS = ['x', 'w_in', 'mix_pre_norm', 'mix_post_norm', 'hg_lb_table', 'hg_out_norm', 'ssm_conv_w', 'ssm_conv_b', 'ssm_dt_bias', 'ssm_A_log', 'ssm_D', 'ssm_out_norm', 'w_branch_hg', 'w_branch_ssm', 'w_out', 'ffn_pre_norm', 'ffn_post_norm', 'ffn_w_up', 'ffn_conv_w', 'ffn_conv_b', 'ffn_w_down', 'loss_target', 'm_w_in', 'm_mix_pre_norm', 'm_mix_post_norm', 'm_hg_lb_table', 'm_hg_out_norm', 'm_ssm_conv_w', 'm_ssm_conv_b', 'm_ssm_dt_bias', 'm_ssm_A_log', 'm_ssm_D', 'm_ssm_out_norm', 'm_w_branch_hg', 'm_w_branch_ssm', 'm_w_out', 'm_ffn_pre_norm', 'm_ffn_post_norm', 'm_ffn_w_up', 'm_ffn_conv_w', 'm_ffn_conv_b', 'm_ffn_w_down', 'v_w_in', 'v_mix_pre_norm', 'v_mix_post_norm', 'v_hg_lb_table', 'v_hg_out_norm', 'v_ssm_conv_w', 'v_ssm_conv_b', 'v_ssm_dt_bias', 'v_ssm_A_log', 'v_ssm_D', 'v_ssm_out_norm', 'v_w_branch_hg', 'v_w_branch_ssm', 'v_w_out', 'v_ffn_pre_norm', 'v_ffn_post_norm', 'v_ffn_w_up', 'v_ffn_conv_w', 'v_ffn_conv_b', 'v_ffn_w_down']
TWIN_OUTPUTS = ['loss', 'grad_x', 'grad_w_in', 'grad_mix_pre_norm', 'grad_mix_post_norm', 'grad_hg_lb_table', 'grad_hg_out_norm', 'grad_ssm_conv_w', 'grad_ssm_conv_b', 'grad_ssm_dt_bias', 'grad_ssm_A_log', 'grad_ssm_D', 'grad_ssm_out_norm', 'grad_w_branch_hg', 'grad_w_branch_ssm', 'grad_w_out', 'grad_ffn_pre_norm', 'grad_ffn_post_norm', 'grad_ffn_w_up', 'grad_ffn_conv_w', 'grad_ffn_conv_b', 'grad_ffn_w_down', 'delta_w_in', 'delta_mix_pre_norm', 'delta_mix_post_norm', 'delta_hg_lb_table', 'delta_hg_out_norm', 'delta_ssm_conv_w', 'delta_ssm_conv_b', 'delta_ssm_dt_bias', 'delta_ssm_A_log', 'delta_ssm_D', 'delta_ssm_out_norm', 'delta_w_branch_hg', 'delta_w_branch_ssm', 'delta_w_out', 'delta_ffn_pre_norm', 'delta_ffn_post_norm', 'delta_ffn_w_up', 'delta_ffn_conv_w', 'delta_ffn_conv_b', 'delta_ffn_w_down', 'new_m_w_in', 'new_m_mix_pre_norm', 'new_m_mix_post_norm', 'new_m_hg_lb_table', 'new_m_hg_out_norm', 'new_m_ssm_conv_w', 'new_m_ssm_conv_b', 'new_m_ssm_dt_bias', 'new_m_ssm_A_log', 'new_m_ssm_D', 'new_m_ssm_out_norm', 'new_m_w_branch_hg', 'new_m_w_branch_ssm', 'new_m_w_out', 'new_m_ffn_pre_norm', 'new_m_ffn_post_norm', 'new_m_ffn_w_up', 'new_m_ffn_conv_w', 'new_m_ffn_conv_b', 'new_m_ffn_w_down', 'new_v_w_in', 'new_v_mix_pre_norm', 'new_v_mix_post_norm', 'new_v_hg_lb_table', 'new_v_hg_out_norm', 'new_v_ssm_conv_w', 'new_v_ssm_conv_b', 'new_v_ssm_dt_bias', 'new_v_ssm_A_log', 'new_v_ssm_D', 'new_v_ssm_out_norm', 'new_v_w_branch_hg', 'new_v_w_branch_ssm', 'new_v_w_out', 'new_v_ffn_pre_norm', 'new_v_ffn_post_norm', 'new_v_ffn_w_up', 'new_v_ffn_conv_w', 'new_v_ffn_conv_b', 'new_v_ffn_w_down']
TWIN_LEAF_KINDS = {'loss': 'loss', 'grad_x': 'grad_x', 'grad_w_in': 'grad_w', 'grad_mix_pre_norm': 'grad_w', 'grad_mix_post_norm': 'grad_w', 'grad_hg_lb_table': 'grad_w', 'grad_hg_out_norm': 'grad_w', 'grad_ssm_conv_w': 'grad_w', 'grad_ssm_conv_b': 'grad_w', 'grad_ssm_dt_bias': 'grad_w', 'grad_ssm_A_log': 'grad_w', 'grad_ssm_D': 'grad_w', 'grad_ssm_out_norm': 'grad_w', 'grad_w_branch_hg': 'grad_w', 'grad_w_branch_ssm': 'grad_w', 'grad_w_out': 'grad_w', 'grad_ffn_pre_norm': 'grad_w', 'grad_ffn_post_norm': 'grad_w', 'grad_ffn_w_up': 'grad_w', 'grad_ffn_conv_w': 'grad_w', 'grad_ffn_conv_b': 'grad_w', 'grad_ffn_w_down': 'grad_w', 'delta_w_in': 'delta_w', 'delta_mix_pre_norm': 'delta_w', 'delta_mix_post_norm': 'delta_w', 'delta_hg_lb_table': 'delta_w', 'delta_hg_out_norm': 'delta_w', 'delta_ssm_conv_w': 'delta_w', 'delta_ssm_conv_b': 'delta_w', 'delta_ssm_dt_bias': 'delta_w', 'delta_ssm_A_log': 'delta_w', 'delta_ssm_D': 'delta_w', 'delta_ssm_out_norm': 'delta_w', 'delta_w_branch_hg': 'delta_w', 'delta_w_branch_ssm': 'delta_w', 'delta_w_out': 'delta_w', 'delta_ffn_pre_norm': 'delta_w', 'delta_ffn_post_norm': 'delta_w', 'delta_ffn_w_up': 'delta_w', 'delta_ffn_conv_w': 'delta_w', 'delta_ffn_conv_b': 'delta_w', 'delta_ffn_w_down': 'delta_w', 'new_m_w_in': 'new_m', 'new_m_mix_pre_norm': 'new_m', 'new_m_mix_post_norm': 'new_m', 'new_m_hg_lb_table': 'new_m', 'new_m_hg_out_norm': 'new_m', 'new_m_ssm_conv_w': 'new_m', 'new_m_ssm_conv_b': 'new_m', 'new_m_ssm_dt_bias': 'new_m', 'new_m_ssm_A_log': 'new_m', 'new_m_ssm_D': 'new_m', 'new_m_ssm_out_norm': 'new_m', 'new_m_w_branch_hg': 'new_m', 'new_m_w_branch_ssm': 'new_m', 'new_m_w_out': 'new_m', 'new_m_ffn_pre_norm': 'new_m', 'new_m_ffn_post_norm': 'new_m', 'new_m_ffn_w_up': 'new_m', 'new_m_ffn_conv_w': 'new_m', 'new_m_ffn_conv_b': 'new_m', 'new_m_ffn_w_down': 'new_m', 'new_v_w_in': 'new_v', 'new_v_mix_pre_norm': 'new_v', 'new_v_mix_post_norm': 'new_v', 'new_v_hg_lb_table': 'new_v', 'new_v_hg_out_norm': 'new_v', 'new_v_ssm_conv_w': 'new_v', 'new_v_ssm_conv_b': 'new_v', 'new_v_ssm_dt_bias': 'new_v', 'new_v_ssm_A_log': 'new_v', 'new_v_ssm_D': 'new_v', 'new_v_ssm_out_norm': 'new_v', 'new_v_w_branch_hg': 'new_v', 'new_v_w_branch_ssm': 'new_v', 'new_v_w_out': 'new_v', 'new_v_ffn_pre_norm': 'new_v', 'new_v_ffn_post_norm': 'new_v', 'new_v_ffn_w_up': 'new_v', 'new_v_ffn_conv_w': 'new_v', 'new_v_ffn_conv_b': 'new_v', 'new_v_ffn_w_down': 'new_v'}


def _forward(args):
    return _fwd_reference(*[args[k] for k in FWD_PARAMS])


def _output_shape():
    def fwd():
        inp = _fwd_setup_inputs(0)
        return _fwd_reference(*[inp[k] for k in FWD_PARAMS])
    out = _jax.eval_shape(fwd)
    return out.shape, out.dtype

N_MICROBATCH = 1
ADAM_LR = 0.001
ADAM_B1 = 0.9
ADAM_B2 = 0.999
ADAM_EPS = 1e-08
ADAM_WD = 0.01
ADAM_STEP = 10
PER_EXAMPLE_BATCH_AXIS = {'x': 0, 'loss_target': 0}
SHARED_INPUTS = []
_WEIGHT_DTYPES = {'w_in': _jnp.float32, 'mix_pre_norm': _jnp.float32, 'mix_post_norm': _jnp.float32, 'hg_lb_table': _jnp.float32, 'hg_out_norm': _jnp.float32, 'ssm_conv_w': _jnp.float32, 'ssm_conv_b': _jnp.float32, 'ssm_dt_bias': _jnp.float32, 'ssm_A_log': _jnp.float32, 'ssm_D': _jnp.float32, 'ssm_out_norm': _jnp.float32, 'w_branch_hg': _jnp.float32, 'w_branch_ssm': _jnp.float32, 'w_out': _jnp.float32, 'ffn_pre_norm': _jnp.float32, 'ffn_post_norm': _jnp.float32, 'ffn_w_up': _jnp.float32, 'ffn_conv_w': _jnp.float32, 'ffn_conv_b': _jnp.float32, 'ffn_w_down': _jnp.float32}
MOMENT_SCALE = {'w_in': 1.493875e-01, 'mix_pre_norm': 5.598760e-01, 'mix_post_norm': 3.203090e+01, 'hg_lb_table': 1.453315e-02, 'hg_out_norm': 9.910577e-01, 'ssm_conv_w': 1.978698e-01, 'ssm_conv_b': 4.880209e-01, 'ssm_dt_bias': 3.917534e-01, 'ssm_A_log': 9.937692e-01, 'ssm_D': 1.081108e+00, 'ssm_out_norm': 2.762287e-01, 'w_branch_hg': 1.697275e-01, 'w_branch_ssm': 4.058304e-01, 'w_out': 4.804038e-01, 'ffn_pre_norm': 4.778159e-01, 'ffn_post_norm': 3.197639e+01, 'ffn_w_up': 2.034527e-01, 'ffn_conv_w': 2.146771e-01, 'ffn_conv_b': 4.606490e-01, 'ffn_w_down': 4.388816e-01}


def _to_microbatches(a, axis):
    t = _jnp.moveaxis(a, axis, 0)
    t = t.reshape((N_MICROBATCH, t.shape[0] // N_MICROBATCH) + t.shape[1:])
    return _jnp.moveaxis(t, 1, axis + 1)


def setup_inputs(seed: int = 0) -> dict:
    inp = _fwd_setup_inputs(seed)
    key = _jax.random.fold_in(_jax.random.key(seed), 7919)
    shape, _ = _output_shape()
    out = dict(inp)
    out["loss_target"] = _jax.random.normal(_jax.random.fold_in(key, 0), shape, _jnp.float32)
    for i, name in enumerate(TWIN_WEIGHTS):
        w = inp[name].astype(_jnp.float32)
        if MOMENT_SCALE is None:
            s = _jnp.sqrt(_jnp.mean(_jnp.square(w)) + 1e-30)
        else:
            s = MOMENT_SCALE[name]
        km, kv = _jax.random.split(_jax.random.fold_in(key, i + 1))
        out[name] = w
        out["m_" + name] = s * _jax.random.normal(km, w.shape, _jnp.float32)
        out["v_" + name] = (s * s) * _jax.random.uniform(kv, w.shape, _jnp.float32, 0.5, 1.5)
    if N_MICROBATCH > 1:
        for name, axis in PER_EXAMPLE_BATCH_AXIS.items():
            out[name] = _to_microbatches(out[name], axis)
    return {'x': out['x'], 'w_in': out['w_in'], 'mix_pre_norm': out['mix_pre_norm'], 'mix_post_norm': out['mix_post_norm'], 'hg_lb_table': out['hg_lb_table'], 'hg_out_norm': out['hg_out_norm'], 'ssm_conv_w': out['ssm_conv_w'], 'ssm_conv_b': out['ssm_conv_b'], 'ssm_dt_bias': out['ssm_dt_bias'], 'ssm_A_log': out['ssm_A_log'], 'ssm_D': out['ssm_D'], 'ssm_out_norm': out['ssm_out_norm'], 'w_branch_hg': out['w_branch_hg'], 'w_branch_ssm': out['w_branch_ssm'], 'w_out': out['w_out'], 'ffn_pre_norm': out['ffn_pre_norm'], 'ffn_post_norm': out['ffn_post_norm'], 'ffn_w_up': out['ffn_w_up'], 'ffn_conv_w': out['ffn_conv_w'], 'ffn_conv_b': out['ffn_conv_b'], 'ffn_w_down': out['ffn_w_down'], 'loss_target': out['loss_target'], 'm_w_in': out['m_w_in'], 'm_mix_pre_norm': out['m_mix_pre_norm'], 'm_mix_post_norm': out['m_mix_post_norm'], 'm_hg_lb_table': out['m_hg_lb_table'], 'm_hg_out_norm': out['m_hg_out_norm'], 'm_ssm_conv_w': out['m_ssm_conv_w'], 'm_ssm_conv_b': out['m_ssm_conv_b'], 'm_ssm_dt_bias': out['m_ssm_dt_bias'], 'm_ssm_A_log': out['m_ssm_A_log'], 'm_ssm_D': out['m_ssm_D'], 'm_ssm_out_norm': out['m_ssm_out_norm'], 'm_w_branch_hg': out['m_w_branch_hg'], 'm_w_branch_ssm': out['m_w_branch_ssm'], 'm_w_out': out['m_w_out'], 'm_ffn_pre_norm': out['m_ffn_pre_norm'], 'm_ffn_post_norm': out['m_ffn_post_norm'], 'm_ffn_w_up': out['m_ffn_w_up'], 'm_ffn_conv_w': out['m_ffn_conv_w'], 'm_ffn_conv_b': out['m_ffn_conv_b'], 'm_ffn_w_down': out['m_ffn_w_down'], 'v_w_in': out['v_w_in'], 'v_mix_pre_norm': out['v_mix_pre_norm'], 'v_mix_post_norm': out['v_mix_post_norm'], 'v_hg_lb_table': out['v_hg_lb_table'], 'v_hg_out_norm': out['v_hg_out_norm'], 'v_ssm_conv_w': out['v_ssm_conv_w'], 'v_ssm_conv_b': out['v_ssm_conv_b'], 'v_ssm_dt_bias': out['v_ssm_dt_bias'], 'v_ssm_A_log': out['v_ssm_A_log'], 'v_ssm_D': out['v_ssm_D'], 'v_ssm_out_norm': out['v_ssm_out_norm'], 'v_w_branch_hg': out['v_w_branch_hg'], 'v_w_branch_ssm': out['v_w_branch_ssm'], 'v_w_out': out['v_w_out'], 'v_ffn_pre_norm': out['v_ffn_pre_norm'], 'v_ffn_post_norm': out['v_ffn_post_norm'], 'v_ffn_w_up': out['v_ffn_w_up'], 'v_ffn_conv_w': out['v_ffn_conv_w'], 'v_ffn_conv_b': out['v_ffn_conv_b'], 'v_ffn_w_down': out['v_ffn_w_down']}


def _loss(weights, diff, rest, loss_target):
    with _jax.named_scope("forward"):
        args = {**rest, TWIN_DIFF_INPUT: diff, **{k: w.astype(_WEIGHT_DTYPES[k]) for k, w in weights.items()}}
        y = _forward(args)
    with _jax.named_scope("loss_head"):
        err = _jnp.square(y.astype(_jnp.float32) - loss_target)
        return 0.5 * _jnp.sum(_jnp.mean(err, axis=-1)) if err.ndim else 0.5 * err


def _adamw(w, g, m, v):
    m = ADAM_B1 * m + (1.0 - ADAM_B1) * g
    v = ADAM_B2 * v + (1.0 - ADAM_B2) * _jnp.square(g)
    m_hat = m / (1.0 - ADAM_B1 ** ADAM_STEP)
    v_hat = v / (1.0 - ADAM_B2 ** ADAM_STEP)
    delta = -ADAM_LR * (m_hat / (_jnp.sqrt(v_hat) + ADAM_EPS) + ADAM_WD * w)
    return delta, m, v


def reference(x, w_in, mix_pre_norm, mix_post_norm, hg_lb_table, hg_out_norm, ssm_conv_w, ssm_conv_b, ssm_dt_bias, ssm_A_log, ssm_D, ssm_out_norm, w_branch_hg, w_branch_ssm, w_out, ffn_pre_norm, ffn_post_norm, ffn_w_up, ffn_conv_w, ffn_conv_b, ffn_w_down, loss_target, m_w_in, m_mix_pre_norm, m_mix_post_norm, m_hg_lb_table, m_hg_out_norm, m_ssm_conv_w, m_ssm_conv_b, m_ssm_dt_bias, m_ssm_A_log, m_ssm_D, m_ssm_out_norm, m_w_branch_hg, m_w_branch_ssm, m_w_out, m_ffn_pre_norm, m_ffn_post_norm, m_ffn_w_up, m_ffn_conv_w, m_ffn_conv_b, m_ffn_w_down, v_w_in, v_mix_pre_norm, v_mix_post_norm, v_hg_lb_table, v_hg_out_norm, v_ssm_conv_w, v_ssm_conv_b, v_ssm_dt_bias, v_ssm_A_log, v_ssm_D, v_ssm_out_norm, v_w_branch_hg, v_w_branch_ssm, v_w_out, v_ffn_pre_norm, v_ffn_post_norm, v_ffn_w_up, v_ffn_conv_w, v_ffn_conv_b, v_ffn_w_down):
    given = dict(x=x, w_in=w_in, mix_pre_norm=mix_pre_norm, mix_post_norm=mix_post_norm, hg_lb_table=hg_lb_table, hg_out_norm=hg_out_norm, ssm_conv_w=ssm_conv_w, ssm_conv_b=ssm_conv_b, ssm_dt_bias=ssm_dt_bias, ssm_A_log=ssm_A_log, ssm_D=ssm_D, ssm_out_norm=ssm_out_norm, w_branch_hg=w_branch_hg, w_branch_ssm=w_branch_ssm, w_out=w_out, ffn_pre_norm=ffn_pre_norm, ffn_post_norm=ffn_post_norm, ffn_w_up=ffn_w_up, ffn_conv_w=ffn_conv_w, ffn_conv_b=ffn_conv_b, ffn_w_down=ffn_w_down, loss_target=loss_target, m_w_in=m_w_in, m_mix_pre_norm=m_mix_pre_norm, m_mix_post_norm=m_mix_post_norm, m_hg_lb_table=m_hg_lb_table, m_hg_out_norm=m_hg_out_norm, m_ssm_conv_w=m_ssm_conv_w, m_ssm_conv_b=m_ssm_conv_b, m_ssm_dt_bias=m_ssm_dt_bias, m_ssm_A_log=m_ssm_A_log, m_ssm_D=m_ssm_D, m_ssm_out_norm=m_ssm_out_norm, m_w_branch_hg=m_w_branch_hg, m_w_branch_ssm=m_w_branch_ssm, m_w_out=m_w_out, m_ffn_pre_norm=m_ffn_pre_norm, m_ffn_post_norm=m_ffn_post_norm, m_ffn_w_up=m_ffn_w_up, m_ffn_conv_w=m_ffn_conv_w, m_ffn_conv_b=m_ffn_conv_b, m_ffn_w_down=m_ffn_w_down, v_w_in=v_w_in, v_mix_pre_norm=v_mix_pre_norm, v_mix_post_norm=v_mix_post_norm, v_hg_lb_table=v_hg_lb_table, v_hg_out_norm=v_hg_out_norm, v_ssm_conv_w=v_ssm_conv_w, v_ssm_conv_b=v_ssm_conv_b, v_ssm_dt_bias=v_ssm_dt_bias, v_ssm_A_log=v_ssm_A_log, v_ssm_D=v_ssm_D, v_ssm_out_norm=v_ssm_out_norm, v_w_branch_hg=v_w_branch_hg, v_w_branch_ssm=v_w_branch_ssm, v_w_out=v_w_out, v_ffn_pre_norm=v_ffn_pre_norm, v_ffn_post_norm=v_ffn_post_norm, v_ffn_w_up=v_ffn_w_up, v_ffn_conv_w=v_ffn_conv_w, v_ffn_conv_b=v_ffn_conv_b, v_ffn_w_down=v_ffn_w_down)
    weights = {n: given[n] for n in TWIN_WEIGHTS}
    shared = {n: given[n] for n in SHARED_INPUTS}
    per_example = {n: given[n] for n in ['x']}
    grad_fn = _jax.value_and_grad(_loss, argnums=(0, 1))

    def one_microbatch(ex, loss_target):
        ex = dict(ex)
        diff = ex.pop(TWIN_DIFF_INPUT)
        return grad_fn(weights, diff, {**shared, **ex}, loss_target)

    if N_MICROBATCH == 1:
        loss, (grad_w, grad_x) = one_microbatch(per_example, given["loss_target"])
    else:
        def body(carry, xs):
            loss_sum, grad_sum = carry
            l_k, (gw_k, gx_k) = one_microbatch(xs[0], xs[1])
            with _jax.named_scope("update"):
                return (loss_sum + l_k, _jax.tree.map(_jnp.add, grad_sum, gw_k)), gx_k

        init = (_jnp.zeros((), _jnp.float32), _jax.tree.map(_jnp.zeros_like, weights))
        (loss, grad_w), grad_x = _jax.lax.scan(body, init, (per_example, given["loss_target"]))
    with _jax.named_scope("update"):
        delta_w, new_m, new_v = {}, {}, {}
        for n in TWIN_WEIGHTS:
            delta_w[n], new_m[n], new_v[n] = _adamw(weights[n], grad_w[n], given["m_" + n], given["v_" + n])
    return (loss, grad_x, *[grad_w[n] for n in TWIN_WEIGHTS], *[delta_w[n] for n in TWIN_WEIGHTS],
            *[new_m[n] for n in TWIN_WEIGHTS], *[new_v[n] for n in TWIN_WEIGHTS])
```

```python
import functools

import numpy as np
import jax
import jax.numpy as jnp
from jax import lax
from jax.experimental import pallas as pl
from jax.experimental.pallas import tpu as pltpu

F32, BF16 = jnp.float32, jnp.bfloat16

D_MODEL = 2048
EPS = 1e-6
HG_HEADS, HG_DK, HG_CHUNK = 16, 128, 64
HG_BLK = 4 * HG_DK
SSM_DINNER, SSM_HEADDIM, SSM_HEADS, SSM_GROUPS, SSM_DSTATE, SSM_CONV, SSM_CHUNK = 4096, 64, 64, 8, 128, 4, 256
SSM_GW = SSM_DINNER // SSM_GROUPS
SSM_HPG = SSM_HEADS // SSM_GROUPS
SSM_XBC = SSM_GW + 2 * SSM_DSTATE
SSM_BLK = SSM_XBC + 128 + SSM_GW
SSM_CONV_DIM = SSM_DINNER + 2 * SSM_GROUPS * SSM_DSTATE
D_FF, FFN_CONV = 5632, 3
FFN_GW = 512
FFN_G = D_FF // FFN_GW
IN_TOTAL = 22592
N_DEV = 8
HALO = 8
VMEM_LIMIT = 52 * 1024 * 1024
ADAM_LR, ADAM_B1, ADAM_B2, ADAM_EPS, ADAM_WD, ADAM_STEP = 0.001, 0.9, 0.999, 1e-08, 0.01, 10

_DIMS = {"nn": ((1,), (0,)), "nt": ((1,), (1,)), "tn": ((0,), (0,))}


def _mm_raw(a, b, mode):
    return lax.dot_general(a.astype(BF16), b.astype(BF16), (_DIMS[mode], ((), ())), preferred_element_type=F32)


@functools.partial(jax.custom_vjp, nondiff_argnums=(2,))
def _mm(a, b, mode):
    return _mm_raw(a, b, mode)


def _mm_fwd(a, b, mode):
    return _mm_raw(a, b, mode), (a, b)


def _mm_bwd(mode, res, dc):
    a, b = res
    if mode == "nn":
        return _mm_raw(dc, b, "nt"), _mm_raw(a, dc, "tn")
    if mode == "nt":
        return _mm_raw(dc, b, "nn"), _mm_raw(dc, a, "tn")
    return _mm_raw(b, dc, "nt"), _mm_raw(a, dc, "nn")


_mm.defvjp(_mm_fwd, _mm_bwd)


def _cmm_raw(m, x, mode):
    hi = x.astype(BF16)
    r1 = x - hi.astype(F32)
    mid = r1.astype(BF16)
    lo = (r1 - mid.astype(F32)).astype(BF16)
    dn = (_DIMS[mode], ((), ()))
    dot = lambda p: lax.dot_general(m, p, dn, preferred_element_type=F32)
    return dot(hi) + dot(mid) + dot(lo)


@jax.custom_vjp
def _cmm(m, x):
    return _cmm_raw(m, x, "nn")


def _cmm_fwd(m, x):
    return _cmm_raw(m, x, "nn"), m


def _cmm_bwd(m, dy):
    return jnp.zeros_like(m), _cmm_raw(m, dy, "tn")


_cmm.defvjp(_cmm_fwd, _cmm_bwd)


@functools.partial(jax.custom_vjp, nondiff_argnums=(1,))
def _sroll(x, s):
    return pltpu.roll(x, s, 0) if s else x


def _sroll_fwd(x, s):
    return _sroll(x, s), None


def _sroll_bwd(s, _, ct):
    return ((pltpu.roll(ct, ct.shape[0] - s, 0) if s else ct),)


_sroll.defvjp(_sroll_fwd, _sroll_bwd)


def _rms(x, w):
    return x * lax.rsqrt(jnp.mean(x * x, axis=-1, keepdims=True) + EPS) * w


def _softplus(x):
    return jnp.maximum(x, 0.0) + jnp.log(1.0 + jnp.exp(-jnp.abs(x)))


def _causal_conv(halo, x, w, b):
    k_taps = w.shape[0]
    xe = jnp.concatenate([halo, x], axis=0)
    out = b
    for k in range(k_taps):
        out = out + w[k:k + 1, :] * _sroll(xe, k_taps - 1 - k)[HALO:, :]
    return out


def _hg_consts():
    c = HG_CHUNK
    t = np.arange(c)
    blocks, rowmask, pair = [], [], []
    lates, earlies = [], []
    for m in (32, 16, 8, 4, 2, 1):
        pos = t % (2 * m)
        late = pos >= m
        mid = t - pos + m
        j = t[None, :]
        mq = late[:, None] & (j >= mid[:, None]) & (j <= t[:, None])
        mk = (~late)[:, None] & (j > t[:, None]) & (j <= mid[:, None] - 1)
        lates.append((mq, late))
        earlies.append((mk, ~late))
        parent = t // (2 * m)
        pair.append((parent[:, None] == parent[None, :]) & late[:, None] & (~late)[None, :])
    for mat, msk in lates + earlies:
        blocks.append(mat)
        rowmask.append(np.repeat(msk[:, None], HG_DK, axis=1))
    blocks.append(t[None, :] <= t[:, None])
    mall = jnp.asarray(np.concatenate(blocks, 0).astype(np.float32), BF16)
    rowmask = jnp.asarray(np.concatenate(rowmask, 0).astype(np.float32))
    pair = jnp.asarray(np.stack(pair, 0).astype(np.float32))
    eye = jnp.asarray(np.eye(c, dtype=np.float32))
    return [mall, rowmask, pair, eye]


def _hg_step(carry, xs, params, consts):
    (st,) = carry
    blk = xs[0].astype(F32)
    tab, nw = params
    mall, rowmask, pair, eye = consts
    c, dk = HG_CHUNK, HG_DK
    q_raw, f_raw, v, og = blk[:, :dk], blk[:, dk:2 * dk], blk[:, 2 * dk:3 * dk], blk[:, 3 * dk:]
    lb = jax.nn.sigmoid(tab[0:1, :] - tab[1:2, :])
    f = lb + (1.0 - lb) * jax.nn.sigmoid(f_raw)
    g = jnp.log(f)
    kk = 1.0 - f
    qh = jax.nn.silu(q_raw) * (HG_DK ** -0.5)
    sums = _cmm(mall, g)
    b = sums[12 * c:, :]
    fac = jnp.exp(sums[:12 * c, :]) * rowmask
    scores = eye * jnp.sum(qh * kk, axis=1, keepdims=True)
    for l in range(6):
        qs = qh * fac[l * c:(l + 1) * c, :]
        ks = kk * fac[(6 + l) * c:(7 + l) * c, :]
        scores = scores + pair[l] * _mm(qs, ks, "nt")
    o = _mm(scores, v, "nn") + _mm(qh * jnp.exp(b), st, "nt")
    b_last = jnp.sum(g, axis=0, keepdims=True)
    st_new = st * jnp.exp(b_last) + _mm(v, kk * jnp.exp(b_last - b), "tn")
    y = _rms(o, nw) * jax.nn.silu(og)
    return [st_new], [y]


def _ssd_consts():
    t = np.arange(SSM_CHUNK)
    tril = (t[None, :] <= t[:, None]).astype(np.float32)
    return [jnp.asarray(tril, BF16), jnp.asarray(tril)]


def _ssd_step(carry, xs, params, consts):
    st, halo = carry
    blk = xs[0].astype(F32)
    conv_w, conv_b, dtb, alog, dskip, nw = params
    tril_b, tril = consts
    c = SSM_CHUNK
    raw, dtr, z = blk[:, :SSM_XBC], blk[:, SSM_XBC:SSM_XBC + 128], blk[:, SSM_XBC + 128:]
    act = jax.nn.silu(_causal_conv(halo, raw, conv_w, conv_b))
    xh, bm, cm = act[:, :SSM_GW], act[:, SSM_GW:SSM_GW + SSM_DSTATE], act[:, SSM_GW + SSM_DSTATE:]
    dt = _softplus(dtr + dtb)
    da = dt * (-jnp.exp(alog))
    acum = _cmm(tril_b, da)
    acum_t = acum.T
    a_last = jnp.sum(da, axis=0, keepdims=True)
    cb = _mm(cm, bm, "nt")
    lane = lax.broadcasted_iota(jnp.int32, (c, 128), 1)
    row = lax.broadcasted_iota(jnp.int32, (128, 128), 0)
    first = lane < SSM_HEADDIM
    ys, st_new = [], []
    for j in range(SSM_HPG // 2):
        xp = xh[:, 128 * j:128 * (j + 1)]
        sp = st[128 * j:128 * (j + 1), :]
        r0, r1 = 2 * j, 2 * j + 1
        col = lambda a, r: jnp.broadcast_to(a[:, r:r + 1], (c, 128))
        xdt = xp * jnp.where(first, col(dt, r0), col(dt, r1))
        yj = _mm(cm, sp, "nt") * jnp.exp(jnp.where(first, col(acum, r0), col(acum, r1)))
        for r, keep in ((r0, first), (r1, ~first)):
            dec = jnp.broadcast_to(acum[:, r:r + 1], (c, c)) - jnp.broadcast_to(acum_t[r:r + 1, :], (c, c))
            m = cb * (tril * jnp.exp(jnp.minimum(dec, 0.0)))
            yj = yj + _mm(m, jnp.where(keep, xdt, 0.0), "nn")
        al0, al1 = a_last[:, r0:r0 + 1], a_last[:, r1:r1 + 1]
        wts = jnp.exp(jnp.where(first, al0 - col(acum, r0), al1 - col(acum, r1)))
        st_new.append(jnp.where(row < SSM_HEADDIM, jnp.exp(al0), jnp.exp(al1)) * sp + _mm(xdt * wts, bm, "tn"))
        ys.append(yj)
    y = jnp.concatenate(ys, axis=1) + dskip * xh
    y = _rms(y * jax.nn.silu(z), nw)
    return [jnp.concatenate(st_new, axis=0), raw[c - HALO:, :]], [y]


def _ffn_step(carry, xs, params, consts):
    (halo,) = carry
    blk = xs[0].astype(F32)
    conv_w, conv_b = params
    gate, up = blk[:, :FFN_GW], blk[:, FFN_GW:]
    a = jax.nn.gelu(_causal_conv(halo, gate, conv_w, conv_b), approximate=True) * up
    return [gate[gate.shape[0] - HALO:, :]], [a]


def _pre_step(carry, xs, params, consts):
    return [], [_rms(xs[0], params[0])]


def _mix_step(carry, xs, params, consts):
    gates, uh, us = (a.astype(F32) for a in xs)
    return [], [jax.nn.sigmoid(gates[:, :D_MODEL]) * uh + jax.nn.sigmoid(gates[:, D_MODEL:]) * us]


def _post_step(carry, xs, params, consts):
    x, v = xs
    x1 = x + _rms(v, params[0])
    return [], [x1, _rms(x1, params[1])]


def _scan_call(step, *, name, rows, chunk, nc, groups, xs, cins=(), params=(), consts=(), carries=(), ys=(), couts=(),
               accs=(), reverse=False):
    blk_rows = chunk * nc
    nb = rows // blk_rows
    n_chunks = rows // chunk
    assert nb * blk_rows == rows
    rb = (lambda i: nb - 1 - i) if reverse else (lambda i: i)
    n_x, n_ci, n_p, n_c = len(xs), len(cins), len(params), len(consts)
    n_y, n_co, n_a = len(ys), len(couts), len(accs)

    def chunk_spec(shape):
        zeros = (0,) * len(shape)
        return pl.BlockSpec((None, nc) + tuple(shape), lambda g, i: (g, rb(i)) + zeros)

    in_specs = [pl.BlockSpec((blk_rows, w), lambda g, i: (rb(i), g)) for _, w in xs]
    in_specs += [chunk_spec(a.shape[2:]) for a in cins]
    in_specs += [pl.BlockSpec((None,) + tuple(a.shape[1:]), lambda g, i: (g, 0, 0)) for a in params]
    in_specs += [pl.BlockSpec(a.shape, (lambda nd: lambda g, i: (0,) * nd)(a.ndim)) for a in consts]
    out_specs = [pl.BlockSpec((blk_rows, w), lambda g, i: (rb(i), g)) for w, _ in ys]
    out_specs += [chunk_spec(s) for s in couts]
    out_specs += [pl.BlockSpec((None, r, c), lambda g, i: (g, 0, 0)) for r, c in accs]
    out_shape = [jax.ShapeDtypeStruct((rows, groups * w), dt) for w, dt in ys]
    out_shape += [jax.ShapeDtypeStruct((groups, n_chunks) + tuple(s), F32) for s in couts]
    out_shape += [jax.ShapeDtypeStruct((groups, r, c), F32) for r, c in accs]

    def body(*refs):
        x_refs = refs[:n_x]
        ci_refs = refs[n_x:n_x + n_ci]
        p_refs = refs[n_x + n_ci:n_x + n_ci + n_p]
        c_refs = refs[n_x + n_ci + n_p:n_x + n_ci + n_p + n_c]
        o = n_x + n_ci + n_p + n_c
        y_refs = refs[o:o + n_y]
        co_refs = refs[o + n_y:o + n_y + n_co]
        a_refs = refs[o + n_y + n_co:o + n_y + n_co + n_a]
        carry_refs = refs[o + n_y + n_co + n_a:]

        @pl.when(pl.program_id(1) == 0)
        def _():
            for s in carry_refs:
                s[...] = jnp.zeros(s.shape, F32)
            for a in a_refs:
                a[...] = jnp.zeros(a.shape, F32)

        pvals = [p[...] for p in p_refs]
        cvals = [c[...] for c in c_refs]

        def one_chunk(i, _):
            c = (nc - 1 - i) if reverse else i
            r0 = c * chunk if isinstance(c, int) else pl.multiple_of(c * chunk, chunk)
            carry = [s[...] for s in carry_refs]
            xv = [x[pl.ds(r0, chunk), :] for x in x_refs]
            civ = [ci[c] for ci in ci_refs]
            new_carry, yv, cov, av = step(carry, xv, civ, pvals, cvals)
            for s, val in zip(carry_refs, new_carry):
                s[...] = val
            for y, val in zip(y_refs, yv):
                y[pl.ds(r0, chunk), :] = val.astype(y.dtype)
            for co, val in zip(co_refs, cov):
                co[c] = val
            for a, val in zip(a_refs, av):
                a[...] += val
            return 0

        if nc == 1:
            one_chunk(0, 0)
        else:
            lax.fori_loop(0, nc, one_chunk, 0)

    outs = pl.pallas_call(
        body, name=name, grid=(groups, nb), in_specs=in_specs, out_specs=out_specs, out_shape=out_shape,
        scratch_shapes=[pltpu.VMEM(tuple(s), F32) for s in carries],
        compiler_params=pltpu.CompilerParams(dimension_semantics=("arbitrary", "arbitrary"),
                                             vmem_limit_bytes=VMEM_LIMIT),
    )(*[a for a, _ in xs], *cins, *params, *consts)
    return outs[:n_y], outs[n_y:n_y + n_co], outs[n_y + n_co:]


def _stage_fwd(step, *, name, rows, chunk, nc, groups, xs, params, consts, carries, ys):
    def fstep(carry, xv, civ, pv, cv):
        new_carry, yv = step(carry, xv, pv, cv)
        return new_carry, yv, carry, []

    yv, saved, _ = _scan_call(fstep, name=name, rows=rows, chunk=chunk, nc=nc, groups=groups, xs=xs, params=params,
                              consts=consts, carries=carries, ys=ys, couts=carries)
    return yv, saved


def _stage_bwd(step, *, name, rows, chunk, nc, groups, xs, saved, params, consts, carries, dys, dxs):
    n_x = len(xs)

    def bstep(dcarry, xv_all, civ, pv, cv):
        xv, dyv = xv_all[:n_x], [d.astype(F32) for d in xv_all[n_x:]]
        _, vjp = jax.vjp(lambda c_, x_, p_: tuple(step(c_, x_, p_, cv)), list(civ), list(xv), list(pv))
        dcarry_in, dxv, dpv = vjp((list(dcarry), dyv))
        return dcarry_in, dxv, [], dpv

    dxv, _, dpv = _scan_call(bstep, name=name, rows=rows, chunk=chunk, nc=nc, groups=groups, xs=list(xs) + list(dys),
                             cins=saved, params=params, consts=consts, carries=carries,
                             ys=[(w, dt) for (_, w), dt in zip(xs, dxs)], accs=[a.shape[1:] for a in params],
                             reverse=True)
    return dxv, dpv


def _mm_params(sem):
    return pltpu.CompilerParams(dimension_semantics=sem, vmem_limit_bytes=VMEM_LIMIT)


def _matmul_nt(a, b, *, name, tm=1024, tn=512):
    m, k = a.shape
    n = b.shape[0]
    tm = min(tm, m)

    def body(a_ref, b_ref, o_ref):
        o_ref[...] = lax.dot_general(a_ref[...], b_ref[...], (_DIMS["nt"], ((), ())), preferred_element_type=F32)

    return pl.pallas_call(
        body, name=name, grid=(m // tm, n // tn),
        in_specs=[pl.BlockSpec((tm, k), lambda i, j: (i, 0)), pl.BlockSpec((tn, k), lambda i, j: (j, 0))],
        out_specs=pl.BlockSpec((tm, tn), lambda i, j: (i, j)),
        out_shape=jax.ShapeDtypeStruct((m, n), F32),
        compiler_params=_mm_params(("parallel", "arbitrary")),
    )(a, b)


def _matmul_nn(a, b, *, name, tm=1024, tk=512):
    m, k = a.shape
    n = b.shape[1]
    tm = min(tm, m)

    def body(a_ref, b_ref, o_ref):
        part = jnp.dot(a_ref[...], b_ref[...], preferred_element_type=F32)

        @pl.when(pl.program_id(1) == 0)
        def _():
            o_ref[...] = part

        @pl.when(pl.program_id(1) != 0)
        def _():
            o_ref[...] += part

    return pl.pallas_call(
        body, name=name, grid=(m // tm, k // tk),
        in_specs=[pl.BlockSpec((tm, tk), lambda i, j: (i, j)), pl.BlockSpec((tk, n), lambda i, j: (j, 0))],
        out_specs=pl.BlockSpec((tm, n), lambda i, j: (i, 0)),
        out_shape=jax.ShapeDtypeStruct((m, n), F32),
        compiler_params=_mm_params(("parallel", "arbitrary")),
    )(a, b)


def _matmul_tn(x, y, *, name, tp, tt=1024):
    t, p = x.shape
    q = y.shape[1]
    tt = min(tt, t)
    steps = t // tt

    def body(x_ref, y_ref, o_ref, acc_ref):
        part = lax.dot_general(x_ref[...], y_ref[...], (_DIMS["tn"], ((), ())), preferred_element_type=F32)

        @pl.when(pl.program_id(1) == 0)
        def _():
            acc_ref[...] = part

        @pl.when(pl.program_id(1) != 0)
        def _():
            acc_ref[...] += part

        @pl.when(pl.program_id(1) == steps - 1)
        def _():
            o_ref[...] = acc_ref[...].astype(o_ref.dtype)

    return pl.pallas_call(
        body, name=name, grid=(p // tp, steps),
        in_specs=[pl.BlockSpec((tt, tp), lambda i, j: (j, i)), pl.BlockSpec((tt, q), lambda i, j: (j, 0))],
        out_specs=pl.BlockSpec((tp, q), lambda i, j: (i, 0)),
        out_shape=jax.ShapeDtypeStruct((p, q), BF16),
        scratch_shapes=[pltpu.VMEM((tp, q), F32)],
        compiler_params=_mm_params(("parallel", "arbitrary")),
    )(x, y)


def _exchange(arrays, *, name):
    n = len(arrays)
    out_shape = [jax.ShapeDtypeStruct((N_DEV,) + tuple(a.shape[-2:]), a.dtype) for a in arrays]

    def body(*refs):
        in_refs, out_refs = refs[:n], refs[n:2 * n]
        send_sems, recv_sems, local_sems = refs[2 * n:]
        x, y, c = lax.axis_index("x"), lax.axis_index("y"), lax.axis_index("c")
        me = 4 * x + 2 * y + c
        copies = []
        for a in range(n):
            src_of = (lambda a_: lambda dev: in_refs[a_] if in_refs[a_].ndim == 2 else in_refs[a_].at[dev])(a)
            local = pltpu.make_async_copy(src_of(me), out_refs[a].at[me], local_sems.at[a])
            local.start()
            copies.append(local)
            for k in range(1, N_DEV):
                px = 1 - x if k & 4 else x
                py = 1 - y if k & 2 else y
                pc = 1 - c if k & 1 else c
                cp = pltpu.make_async_remote_copy(
                    src_ref=src_of(4 * px + 2 * py + pc), dst_ref=out_refs[a].at[me],
                    send_sem=send_sems.at[a, k - 1], recv_sem=recv_sems.at[a, k - 1],
                    device_id=(px, py, pc), device_id_type=pl.DeviceIdType.MESH)
                cp.start()
                copies.append(cp)
        for cp in copies:
            cp.wait()

    any_spec = pl.BlockSpec(memory_space=pl.ANY)
    return pl.pallas_call(
        body, name=name, in_specs=[any_spec] * n, out_specs=[any_spec] * n, out_shape=out_shape,
        scratch_shapes=[pltpu.SemaphoreType.DMA((n, N_DEV - 1)), pltpu.SemaphoreType.DMA((n, N_DEV - 1)),
                        pltpu.SemaphoreType.DMA((n,))],
        compiler_params=pltpu.CompilerParams(has_side_effects=True),
    )(*arrays)


def _sum_blocks(a, *, name, tc):
    _, r, c = a.shape

    def body(a_ref, o_ref):
        acc = a_ref[0].astype(F32)
        for i in range(1, N_DEV):
            acc = acc + a_ref[i].astype(F32)
        o_ref[...] = acc

    return pl.pallas_call(
        body, name=name, grid=(c // tc,),
        in_specs=[pl.BlockSpec((N_DEV, r, tc), lambda j: (0, 0, j))],
        out_specs=pl.BlockSpec((r, tc), lambda j: (0, j)),
        out_shape=jax.ShapeDtypeStruct((r, c), F32),
        compiler_params=pltpu.CompilerParams(dimension_semantics=("parallel",), vmem_limit_bytes=VMEM_LIMIT),
    )(a)


def _adamw(w, g, m, v, *, name, tr):
    r, c = w.shape

    def body(w_ref, g_ref, m_ref, v_ref, d_ref, mo_ref, vo_ref):
        gv = g_ref[...]
        mn = ADAM_B1 * m_ref[...] + (1.0 - ADAM_B1) * gv
        vn = ADAM_B2 * v_ref[...] + (1.0 - ADAM_B2) * jnp.square(gv)
        m_hat = mn / (1.0 - ADAM_B1 ** ADAM_STEP)
        v_hat = vn / (1.0 - ADAM_B2 ** ADAM_STEP)
        d_ref[...] = -ADAM_LR * (m_hat / (jnp.sqrt(v_hat) + ADAM_EPS) + ADAM_WD * w_ref[...])
        mo_ref[...] = mn
        vo_ref[...] = vn

    spec = pl.BlockSpec((tr, c), lambda i: (i, 0))
    return pl.pallas_call(
        body, name=name, grid=(r // tr,), in_specs=[spec] * 4, out_specs=[spec] * 3,
        out_shape=[jax.ShapeDtypeStruct((r, c), F32)] * 3,
        compiler_params=pltpu.CompilerParams(dimension_semantics=("parallel",), vmem_limit_bytes=VMEM_LIMIT),
    )(w, g, m, v)


def _in_proj_layout():
    z0, xbc0, dt0, gate0 = 8192, 12288, 18432, 18496
    hg = []
    for h in range(HG_HEADS):
        for part in range(4):
            hg.append(part * 2048 + h * HG_DK + np.arange(HG_DK))
    ssm = []
    for g in range(SSM_GROUPS):
        ssm.append(xbc0 + g * SSM_GW + np.arange(SSM_GW))
        ssm.append(xbc0 + SSM_DINNER + g * SSM_DSTATE + np.arange(SSM_DSTATE))
        ssm.append(xbc0 + SSM_DINNER + SSM_GROUPS * SSM_DSTATE + g * SSM_DSTATE + np.arange(SSM_DSTATE))
        ssm.append(np.concatenate([dt0 + g * SSM_HPG + np.arange(SSM_HPG), -np.ones(128 - SSM_HPG, np.int64)]))
        ssm.append(z0 + g * SSM_GW + np.arange(SSM_GW))
    gate = gate0 + np.arange(2 * D_MODEL)
    return np.concatenate(hg), np.concatenate(ssm), gate


def _conv_layout():
    idx = []
    for g in range(SSM_GROUPS):
        idx.append(np.concatenate([g * SSM_GW + np.arange(SSM_GW),
                                   SSM_DINNER + g * SSM_DSTATE + np.arange(SSM_DSTATE),
                                   SSM_DINNER + SSM_GROUPS * SSM_DSTATE + g * SSM_DSTATE + np.arange(SSM_DSTATE)]))
    return np.stack(idx)


def _up_layout():
    idx = []
    for g in range(FFN_G):
        idx.append(g * FFN_GW + np.arange(FFN_GW))
        idx.append(D_FF + g * FFN_GW + np.arange(FFN_GW))
    return np.concatenate(idx)


def _inverse(idx, n):
    inv = np.zeros(n, np.int64)
    pos = np.nonzero(idx >= 0)[0]
    inv[idx[pos]] = pos
    return inv


def _take_rows(a, idx):
    if (idx < 0).any():
        a = jnp.concatenate([a, jnp.zeros((1,) + a.shape[1:], a.dtype)], axis=0)
        idx = np.where(idx < 0, a.shape[0] - 1, idx)
    return jnp.take(a, jnp.asarray(idx, jnp.int32), axis=0)


_SMALL = (("mix_pre_norm", (1, 2048)), ("mix_post_norm", (1, 2048)), ("hg_lb_table", (2, 2048)), ("hg_out_norm", (1, 128)),
          ("ssm_conv_w", (4, 6144)), ("ssm_conv_b", (1, 6144)), ("ssm_dt_bias", (1, 64)), ("ssm_A_log", (1, 64)),
          ("ssm_D", (1, 64)), ("ssm_out_norm", (1, 4096)), ("ffn_pre_norm", (1, 2048)), ("ffn_post_norm", (1, 2048)),
          ("ffn_conv_w", (3, 5632)), ("ffn_conv_b", (1, 5632)), ("loss", (1, 1)))
_PACK_ROWS = 8 * (-(-sum(int(np.prod(s)) for _, s in _SMALL) // 1024))


def _pack(vals):
    flat = jnp.concatenate([vals[k].astype(F32).reshape(-1) for k, _ in _SMALL])
    return jnp.pad(flat, (0, _PACK_ROWS * 128 - flat.shape[0])).reshape(_PACK_ROWS, 128)


def _unpack(packed):
    flat, out, o = packed.reshape(-1), {}, 0
    for k, s in _SMALL:
        n = int(np.prod(s))
        out[k] = flat[o:o + n].reshape(s)
        o += n
    return out


def _local_step(x, target, w, p):
    t = x.shape[0]
    one = lambda a: a.reshape((1,) + a.shape)
    row = dict(rows=t, groups=1, consts=[], carries=[])

    (h1,), _ = _stage_fwd(_pre_step, name="pre_fwd", chunk=512, nc=1, xs=[(x, D_MODEL)], params=[one(p["mix_pre_norm"])],
                          ys=[(D_MODEL, BF16)], **row)
    proj_hg = _matmul_nt(h1, w["in_hg"], name="proj_hg")
    proj_ssm = _matmul_nt(h1, w["in_ssm"], name="proj_ssm")
    proj_gate = _matmul_nt(h1, w["in_gate"], name="proj_gate")

    hg = dict(rows=t, chunk=HG_CHUNK, nc=8, groups=HG_HEADS, xs=[(proj_hg, HG_BLK)], params=[p["hg_tab"], p["hg_nw"]],
              consts=_hg_consts(), carries=[(HG_DK, HG_DK)])
    (y_hg,), hg_saved = _stage_fwd(_hg_step, name="hg_fwd", ys=[(HG_DK, BF16)], **hg)

    ssd = dict(rows=t, chunk=SSM_CHUNK, nc=2, groups=SSM_GROUPS, xs=[(proj_ssm, SSM_BLK)],
               params=[p["conv_w"], p["conv_b"], p["dt_bias"], p["a_log"], p["d_skip"], p["ssm_nw"]],
               consts=_ssd_consts(), carries=[(4 * 128, SSM_DSTATE), (HALO, SSM_XBC)])
    (y_ssm,), ssd_saved = _stage_fwd(_ssd_step, name="ssd_fwd", ys=[(SSM_GW, BF16)], **ssd)

    u_hg = _matmul_nn(y_hg, w["branch_hg"], name="branch_hg")
    u_ssm = _matmul_nn(y_ssm, w["branch_ssm"], name="branch_ssm")
    mix = dict(chunk=256, nc=1, xs=[(proj_gate, 2 * D_MODEL), (u_hg, D_MODEL), (u_ssm, D_MODEL)], params=[], **row)
    (mixed,), _ = _stage_fwd(_mix_step, name="mix_fwd", ys=[(D_MODEL, BF16)], **mix)
    v = _matmul_nn(mixed, w["out"], name="out_proj")
    post = dict(chunk=256, nc=1, xs=[(x, D_MODEL), (v, D_MODEL)],
                params=[one(p["mix_post_norm"]), one(p["ffn_pre_norm"])], **row)
    (x1, h2), _ = _stage_fwd(_post_step, name="post_fwd", ys=[(D_MODEL, F32), (D_MODEL, BF16)], **post)
    gu = _matmul_nt(h2, w["up"], name="ffn_up")
    ffn = dict(rows=t, chunk=256, nc=2, groups=FFN_G, xs=[(gu, 2 * FFN_GW)], params=[p["ffn_conv_w"], p["ffn_conv_b"]],
               consts=[], carries=[(HALO, FFN_GW)])
    (act,), ffn_saved = _stage_fwd(_ffn_step, name="ffn_fwd", ys=[(FFN_GW, BF16)], **ffn)
    d = _matmul_nn(act, w["down"], name="ffn_down")

    def head_step(carry, xv, civ, pv, cv):
        x1_, d_, tgt = xv

        def per_row_loss(a, b, nw):
            e = a + _rms(b, nw) - tgt
            return 0.5 * jnp.mean(e * e, axis=1, keepdims=True)

        lrow, vjp = jax.vjp(per_row_loss, x1_, d_, pv[0])
        dx1_, dd_, dnw = vjp(jnp.ones_like(lrow))
        loss = jnp.broadcast_to(jnp.sum(lrow, axis=0, keepdims=True), (1, 128))
        return [], [dx1_, dd_], [], [dnw, loss]

    (dy, dd), _, (g_ffn_post, loss) = _scan_call(
        head_step, name="loss_head", chunk=256, nc=1, xs=[(x1, D_MODEL), (d, D_MODEL), (target, D_MODEL)],
        params=[one(p["ffn_post_norm"])], ys=[(D_MODEL, F32), (D_MODEL, BF16)], accs=[(1, D_MODEL), (1, 128)], **row)

    gw = {}
    gw["down"] = _matmul_tn(act, dd, name="g_down", tp=512)
    dact = _matmul_nt(dd, w["down"], name="d_act")
    (dgu,), (g_fcw, g_fcb) = _stage_bwd(_ffn_step, name="ffn_bwd", saved=ffn_saved, dys=[(dact, FFN_GW)], dxs=[BF16], **ffn)
    gw["up"] = _matmul_tn(dgu, h2, name="g_up", tp=1024)
    dh2 = _matmul_nn(dgu, w["up"], name="d_h2")
    (dx1, dv), (g_mix_post, g_ffn_pre) = _stage_bwd(_post_step, name="post_bwd", saved=[], dys=[(dy, D_MODEL), (dh2, D_MODEL)],
                                                    dxs=[F32, BF16], **post)
    gw["out"] = _matmul_tn(mixed, dv, name="g_out", tp=1024)
    dmixed = _matmul_nt(dv, w["out"], name="d_mixed")
    (dgate, du_hg, du_ssm), _ = _stage_bwd(_mix_step, name="mix_bwd", saved=[], dys=[(dmixed, D_MODEL)],
                                           dxs=[BF16, BF16, BF16], **mix)
    gw["branch_hg"] = _matmul_tn(y_hg, du_hg, name="g_branch_hg", tp=1024)
    gw["branch_ssm"] = _matmul_tn(y_ssm, du_ssm, name="g_branch_ssm", tp=1024)
    dy_hg = _matmul_nt(du_hg, w["branch_hg"], name="d_y_hg")
    dy_ssm = _matmul_nt(du_ssm, w["branch_ssm"], name="d_y_ssm")
    (dproj_ssm,), g_ssd = _stage_bwd(_ssd_step, name="ssd_bwd", saved=ssd_saved, dys=[(dy_ssm, SSM_GW)], dxs=[BF16], **ssd)
    (dproj_hg,), (g_tab, g_hg_nw) = _stage_bwd(_hg_step, name="hg_bwd", saved=hg_saved, dys=[(dy_hg, HG_DK)], dxs=[BF16], **hg)
    gw["in_hg"] = _matmul_tn(dproj_hg, h1, name="g_in_hg", tp=1024)
    gw["in_ssm"] = _matmul_tn(dproj_ssm, h1, name="g_in_ssm", tp=1024)
    gw["in_gate"] = _matmul_tn(dgate, h1, name="g_in_gate", tp=1024)
    dh_a = _matmul_nn(dproj_hg, w["in_hg"], name="d_h1_hg")
    dh_b = _matmul_nn(dproj_ssm, w["in_ssm"], name="d_h1_ssm")
    dh_c = _matmul_nn(dgate, w["in_gate"], name="d_h1_gate")

    def pre_bwd_step(carry, xv, civ, pv, cv):
        x_, da, db, dc, dres = xv
        _, vjp = jax.vjp(_rms, x_, pv[0])
        dx_, dnw = vjp(da + db + dc)
        return [], [dx_ + dres], [], [dnw]

    (grad_x,), _, (g_mix_pre,) = _scan_call(
        pre_bwd_step, name="pre_bwd", chunk=256, nc=1,
        xs=[(x, D_MODEL), (dh_a, D_MODEL), (dh_b, D_MODEL), (dh_c, D_MODEL), (dx1, D_MODEL)],
        params=[one(p["mix_pre_norm"])], ys=[(D_MODEL, F32)], accs=[(1, D_MODEL)], **row)

    gp = dict(mix_pre_norm=g_mix_pre[0], mix_post_norm=g_mix_post[0], ffn_pre_norm=g_ffn_pre[0], ffn_post_norm=g_ffn_post[0],
              hg_tab=g_tab, hg_nw=g_hg_nw, conv_w=g_ssd[0], conv_b=g_ssd[1], dt_bias=g_ssd[2], a_log=g_ssd[3],
              d_skip=g_ssd[4], ssm_nw=g_ssd[5], ffn_conv_w=g_fcw, ffn_conv_b=g_fcb, loss=loss[0, :, :1])
    return grad_x, gw, gp


def _small_to_kernel_layout(s):
    conv_idx = _conv_layout()
    pad_heads = lambda a: jnp.pad(a.reshape(SSM_GROUPS, 1, SSM_HPG), ((0, 0), (0, 0), (0, 128 - SSM_HPG)))
    return dict(
        mix_pre_norm=s["mix_pre_norm"], mix_post_norm=s["mix_post_norm"], ffn_pre_norm=s["ffn_pre_norm"],
        ffn_post_norm=s["ffn_post_norm"],
        hg_tab=s["hg_lb_table"].reshape(2, HG_HEADS, HG_DK).transpose(1, 0, 2),
        hg_nw=jnp.broadcast_to(s["hg_out_norm"].reshape(1, 1, HG_DK), (HG_HEADS, 1, HG_DK)),
        conv_w=jnp.take(s["ssm_conv_w"], jnp.asarray(conv_idx, jnp.int32), axis=1).transpose(1, 0, 2),
        conv_b=jnp.take(s["ssm_conv_b"][0], jnp.asarray(conv_idx, jnp.int32), axis=0)[:, None, :],
        dt_bias=pad_heads(s["ssm_dt_bias"]), a_log=pad_heads(s["ssm_A_log"]),
        d_skip=jnp.repeat(s["ssm_D"].reshape(SSM_HEADS), SSM_HEADDIM).reshape(SSM_GROUPS, 1, SSM_GW),
        ssm_nw=s["ssm_out_norm"].reshape(SSM_GROUPS, 1, SSM_GW),
        ffn_conv_w=s["ffn_conv_w"].reshape(FFN_CONV, FFN_G, FFN_GW).transpose(1, 0, 2),
        ffn_conv_b=s["ffn_conv_b"].reshape(FFN_G, 1, FFN_GW),
    )


def _small_from_kernel_layout(g):
    conv_inv = _inverse(_conv_layout().reshape(-1), SSM_CONV_DIM)
    heads = lambda a: a[:, 0, :SSM_HPG].reshape(1, SSM_HEADS)
    return dict(
        mix_pre_norm=g["mix_pre_norm"], mix_post_norm=g["mix_post_norm"], ffn_pre_norm=g["ffn_pre_norm"],
        ffn_post_norm=g["ffn_post_norm"],
        hg_lb_table=g["hg_tab"].transpose(1, 0, 2).reshape(2, HG_HEADS * HG_DK),
        hg_out_norm=jnp.sum(g["hg_nw"], axis=0),
        ssm_conv_w=jnp.take(g["conv_w"].transpose(1, 0, 2).reshape(SSM_CONV, -1), jnp.asarray(conv_inv, jnp.int32), axis=1),
        ssm_conv_b=jnp.take(g["conv_b"].reshape(1, -1), jnp.asarray(conv_inv, jnp.int32), axis=1),
        ssm_dt_bias=heads(g["dt_bias"]), ssm_A_log=heads(g["a_log"]),
        ssm_D=jnp.sum(g["d_skip"].reshape(SSM_HEADS, SSM_HEADDIM), axis=1).reshape(1, SSM_HEADS),
        ssm_out_norm=g["ssm_nw"].reshape(1, SSM_DINNER),
        ffn_conv_w=g["ffn_conv_w"].transpose(1, 0, 2).reshape(FFN_CONV, D_FF),
        ffn_conv_b=g["ffn_conv_b"].reshape(1, D_FF),
        loss=g["loss"],
    )


def kernel(x, w_in, mix_pre_norm, mix_post_norm, hg_lb_table, hg_out_norm, ssm_conv_w, ssm_conv_b, ssm_dt_bias, ssm_A_log, ssm_D, ssm_out_norm, w_branch_hg, w_branch_ssm, w_out, ffn_pre_norm, ffn_post_norm, ffn_w_up, ffn_conv_w, ffn_conv_b, ffn_w_down, loss_target, m_w_in, m_mix_pre_norm, m_mix_post_norm, m_hg_lb_table, m_hg_out_norm, m_ssm_conv_w, m_ssm_conv_b, m_ssm_dt_bias, m_ssm_A_log, m_ssm_D, m_ssm_out_norm, m_w_branch_hg, m_w_branch_ssm, m_w_out, m_ffn_pre_norm, m_ffn_post_norm, m_ffn_w_up, m_ffn_conv_w, m_ffn_conv_b, m_ffn_w_down, v_w_in, v_mix_pre_norm, v_mix_post_norm, v_hg_lb_table, v_hg_out_norm, v_ssm_conv_w, v_ssm_conv_b, v_ssm_dt_bias, v_ssm_A_log, v_ssm_D, v_ssm_out_norm, v_w_branch_hg, v_w_branch_ssm, v_w_out, v_ffn_pre_norm, v_ffn_post_norm, v_ffn_w_up, v_ffn_conv_w, v_ffn_conv_b, v_ffn_w_down):
    names = ["w_in", "mix_pre_norm", "mix_post_norm", "hg_lb_table", "hg_out_norm", "ssm_conv_w", "ssm_conv_b", "ssm_dt_bias",
             "ssm_A_log", "ssm_D", "ssm_out_norm", "w_branch_hg", "w_branch_ssm", "w_out", "ffn_pre_norm", "ffn_post_norm",
             "ffn_w_up", "ffn_conv_w", "ffn_conv_b", "ffn_w_down"]
    ws = dict(zip(names, (w_in, mix_pre_norm, mix_post_norm, hg_lb_table, hg_out_norm, ssm_conv_w, ssm_conv_b, ssm_dt_bias,
                          ssm_A_log, ssm_D, ssm_out_norm, w_branch_hg, w_branch_ssm, w_out, ffn_pre_norm, ffn_post_norm,
                          ffn_w_up, ffn_conv_w, ffn_conv_b, ffn_w_down)))
    ms = dict(zip(names, (m_w_in, m_mix_pre_norm, m_mix_post_norm, m_hg_lb_table, m_hg_out_norm, m_ssm_conv_w, m_ssm_conv_b,
                          m_ssm_dt_bias, m_ssm_A_log, m_ssm_D, m_ssm_out_norm, m_w_branch_hg, m_w_branch_ssm, m_w_out,
                          m_ffn_pre_norm, m_ffn_post_norm, m_ffn_w_up, m_ffn_conv_w, m_ffn_conv_b, m_ffn_w_down)))
    vs = dict(zip(names, (v_w_in, v_mix_pre_norm, v_mix_post_norm, v_hg_lb_table, v_hg_out_norm, v_ssm_conv_w, v_ssm_conv_b,
                          v_ssm_dt_bias, v_ssm_A_log, v_ssm_D, v_ssm_out_norm, v_w_branch_hg, v_w_branch_ssm, v_w_out,
                          v_ffn_pre_norm, v_ffn_post_norm, v_ffn_w_up, v_ffn_conv_w, v_ffn_conv_b, v_ffn_w_down)))
    me = 4 * lax.axis_index("x") + 2 * lax.axis_index("y") + lax.axis_index("c")

    gathered = _exchange(
        [w_in[0].T.astype(BF16), ffn_w_up[0].T.astype(BF16), w_branch_hg[0].astype(BF16), w_branch_ssm[0].astype(BF16),
         w_out[0].astype(BF16), ffn_w_down[0].astype(BF16), ssm_conv_w[0], ffn_conv_w[0]], name="gather_weights")
    in_t = gathered[0].reshape(IN_TOTAL, D_MODEL)
    up_t = gathered[1].reshape(2 * D_FF, D_MODEL)
    idx_hg, idx_ssm, idx_gate = _in_proj_layout()
    up_idx = _up_layout()
    w = dict(in_hg=_take_rows(in_t, idx_hg), in_ssm=_take_rows(in_t, idx_ssm), in_gate=_take_rows(in_t, idx_gate),
             up=_take_rows(up_t, up_idx), branch_hg=gathered[2].reshape(D_MODEL, D_MODEL),
             branch_ssm=gathered[3].reshape(SSM_DINNER, D_MODEL), out=gathered[4].reshape(D_MODEL, D_MODEL),
             down=gathered[5].reshape(D_FF, D_MODEL))
    small = {k: ws[k] for k, _ in _SMALL[:-1]}
    small["ssm_conv_w"] = gathered[6].transpose(1, 0, 2).reshape(SSM_CONV, SSM_CONV_DIM)
    small["ffn_conv_w"] = gathered[7].transpose(1, 0, 2).reshape(FFN_CONV, D_FF)
    small = {k: small[k].reshape(s) for k, s in _SMALL[:-1]}

    grad_x, gw, gp = _local_step(x[0], loss_target[0], w, _small_to_kernel_layout(small))

    in_all = jnp.concatenate([gw["in_hg"], gw["in_ssm"], gw["in_gate"]], axis=0)
    in_inv = _inverse(np.concatenate([idx_hg, idx_ssm, idx_gate]), IN_TOTAL)
    parts = [jnp.take(in_all, jnp.asarray(in_inv, jnp.int32), axis=0).reshape(N_DEV, IN_TOTAL // N_DEV, D_MODEL),
             jnp.take(gw["up"], jnp.asarray(_inverse(up_idx, 2 * D_FF), jnp.int32), axis=0).reshape(N_DEV, -1, D_MODEL),
             gw["branch_hg"].reshape(N_DEV, -1, D_MODEL), gw["branch_ssm"].reshape(N_DEV, -1, D_MODEL),
             gw["out"].reshape(N_DEV, -1, D_MODEL), gw["down"].reshape(N_DEV, -1, D_MODEL)]
    received = _exchange(parts + [_pack(_small_from_kernel_layout(gp))], name="exchange_grads")
    big_names = ["w_in", "ffn_w_up", "w_branch_hg", "w_branch_ssm", "w_out", "ffn_w_down"]
    grads = {}
    for k, r in zip(big_names, received[:6]):
        g = _sum_blocks(r, name="sum_" + k, tc=256)
        grads[k] = g.T if k in ("w_in", "ffn_w_up") else g
    small_g = _unpack(_sum_blocks(received[6], name="sum_small", tc=128))
    loss = small_g.pop("loss").reshape(())
    for k, g in small_g.items():
        if k in ("ssm_conv_w", "ffn_conv_w"):
            n = g.shape[1] // N_DEV
            g = lax.dynamic_slice_in_dim(g, me * n, n, axis=1)
        grads[k] = g

    delta, new_m, new_v = {}, {}, {}
    for k in big_names:
        delta[k], new_m[k], new_v[k] = _adamw(ws[k][0], grads[k], ms[k][0], vs[k][0], name="adamw_" + k, tr=64)
    small_names = [k for k in names if k not in big_names]
    flat = lambda d: jnp.concatenate([d[k].astype(F32).reshape(-1) for k in small_names])
    n_small = sum(int(np.prod(ws[k].shape)) for k in small_names)
    rows = 8 * (-(-n_small // 1024))
    pack2 = lambda d: jnp.pad(flat(d), (0, rows * 128 - n_small)).reshape(rows, 128)
    v_packed = jnp.pad(flat(vs), (0, rows * 128 - n_small), constant_values=1.0).reshape(rows, 128)
    packed = _adamw(pack2(ws), pack2(grads), pack2(ms), v_packed, name="adamw_small", tr=rows)
    o = 0
    for k in small_names:
        n = int(np.prod(ws[k].shape))
        delta[k], new_m[k], new_v[k] = (a.reshape(-1)[o:o + n].reshape(ws[k].shape) for a in packed)
        o += n

    full = lambda d: [d[k].reshape(ws[k].shape) for k in names]
    return (loss, grad_x[None], *full(grads), *full(delta), *full(new_m), *full(new_v))
```

```python
import functools

import numpy as np
import jax
import jax.numpy as jnp
from jax import lax
from jax.experimental import pallas as pl
from jax.experimental.pallas import tpu as pltpu

F32, BF16 = jnp.float32, jnp.bfloat16

D_MODEL = 2048
EPS = 1e-6
HG_HEADS, HG_DK, HG_CHUNK = 16, 128, 64
HG_BLK = 4 * HG_DK
SSM_DINNER, SSM_HEADDIM, SSM_HEADS, SSM_GROUPS, SSM_DSTATE, SSM_CONV, SSM_CHUNK = 4096, 64, 64, 8, 128, 4, 256
SSM_GW = SSM_DINNER // SSM_GROUPS
SSM_HPG = SSM_HEADS // SSM_GROUPS
SSM_XBC = SSM_GW + 2 * SSM_DSTATE
SSM_BLK = SSM_XBC + 128 + SSM_GW
SSM_CONV_DIM = SSM_DINNER + 2 * SSM_GROUPS * SSM_DSTATE
D_FF, FFN_CONV = 5632, 3
FFN_GW = 512
FFN_G = D_FF // FFN_GW
IN_TOTAL = 22592
N_DEV = 8
HALO = 8
VMEM_LIMIT = 52 * 1024 * 1024
ADAM_LR, ADAM_B1, ADAM_B2, ADAM_EPS, ADAM_WD, ADAM_STEP = 0.001, 0.9, 0.999, 1e-08, 0.01, 10

_DIMS = {"nn": ((1,), (0,)), "nt": ((1,), (1,)), "tn": ((0,), (0,))}


def _mm_raw(a, b, mode):
    return lax.dot_general(a.astype(BF16), b.astype(BF16), (_DIMS[mode], ((), ())), preferred_element_type=F32)


@functools.partial(jax.custom_vjp, nondiff_argnums=(2,))
def _mm(a, b, mode):
    return _mm_raw(a, b, mode)


def _mm_fwd(a, b, mode):
    return _mm_raw(a, b, mode), (a, b)


def _mm_bwd(mode, res, dc):
    a, b = res
    if mode == "nn":
        return _mm_raw(dc, b, "nt"), _mm_raw(a, dc, "tn")
    if mode == "nt":
        return _mm_raw(dc, b, "nn"), _mm_raw(dc, a, "tn")
    return _mm_raw(b, dc, "nt"), _mm_raw(a, dc, "nn")


_mm.defvjp(_mm_fwd, _mm_bwd)


def _cmm_raw(m, x, mode):
    hi = x.astype(BF16)
    r1 = x - hi.astype(F32)
    mid = r1.astype(BF16)
    lo = (r1 - mid.astype(F32)).astype(BF16)
    dn = (_DIMS[mode], ((), ()))
    dot = lambda p: lax.dot_general(m, p, dn, preferred_element_type=F32)
    return dot(hi) + dot(mid) + dot(lo)


@jax.custom_vjp
def _cmm(m, x):
    return _cmm_raw(m, x, "nn")


def _cmm_fwd(m, x):
    return _cmm_raw(m, x, "nn"), m


def _cmm_bwd(m, dy):
    return jnp.zeros_like(m), _cmm_raw(m, dy, "tn")


_cmm.defvjp(_cmm_fwd, _cmm_bwd)


@functools.partial(jax.custom_vjp, nondiff_argnums=(1,))
def _sroll(x, s):
    return pltpu.roll(x, s, 0) if s else x


def _sroll_fwd(x, s):
    return _sroll(x, s), None


def _sroll_bwd(s, _, ct):
    return ((pltpu.roll(ct, ct.shape[0] - s, 0) if s else ct),)


_sroll.defvjp(_sroll_fwd, _sroll_bwd)


def _rms(x, w):
    return x * lax.rsqrt(jnp.mean(x * x, axis=-1, keepdims=True) + EPS) * w


def _softplus(x):
    return jnp.maximum(x, 0.0) + jnp.log(1.0 + jnp.exp(-jnp.abs(x)))


def _causal_conv(halo, x, w, b):
    k_taps = w.shape[0]
    xe = jnp.concatenate([halo, x], axis=0)
    out = b
    for k in range(k_taps):
        out = out + w[k:k + 1, :] * _sroll(xe, k_taps - 1 - k)[HALO:, :]
    return out


def _hg_consts():
    c = HG_CHUNK
    t = np.arange(c)
    blocks, rowmask, pair = [], [], []
    lates, earlies = [], []
    for m in (32, 16, 8, 4, 2, 1):
        pos = t % (2 * m)
        late = pos >= m
        mid = t - pos + m
        j = t[None, :]
        mq = late[:, None] & (j >= mid[:, None]) & (j <= t[:, None])
        mk = (~late)[:, None] & (j > t[:, None]) & (j <= mid[:, None] - 1)
        lates.append((mq, late))
        earlies.append((mk, ~late))
        parent = t // (2 * m)
        pair.append((parent[:, None] == parent[None, :]) & late[:, None] & (~late)[None, :])
    for mat, msk in lates + earlies:
        blocks.append(mat)
        rowmask.append(np.repeat(msk[:, None], HG_DK, axis=1))
    blocks.append(t[None, :] <= t[:, None])
    mall = jnp.asarray(np.concatenate(blocks, 0).astype(np.float32), BF16)
    rowmask = jnp.asarray(np.concatenate(rowmask, 0).astype(np.float32))
    pair = jnp.asarray(np.stack(pair, 0).astype(np.float32))
    eye = jnp.asarray(np.eye(c, dtype=np.float32))
    return [mall, rowmask, pair, eye]


def _hg_step(carry, xs, params, consts):
    (st,) = carry
    blk = xs[0].astype(F32)
    tab, nw = params
    mall, rowmask, pair, eye = consts
    c, dk = HG_CHUNK, HG_DK
    q_raw, f_raw, v, og = blk[:, :dk], blk[:, dk:2 * dk], blk[:, 2 * dk:3 * dk], blk[:, 3 * dk:]
    lb = jax.nn.sigmoid(tab[0:1, :] - tab[1:2, :])
    f = lb + (1.0 - lb) * jax.nn.sigmoid(f_raw)
    g = jnp.log(f)
    kk = 1.0 - f
    qh = jax.nn.silu(q_raw) * (HG_DK ** -0.5)
    yield
    sums = _cmm(mall, g)
    yield
    b = sums[12 * c:, :]
    fac = jnp.exp(sums[:12 * c, :]) * rowmask
    scores = eye * jnp.sum(qh * kk, axis=1, keepdims=True)
    b_last = jnp.sum(g, axis=0, keepdims=True)
    yield
    inter = _mm(qh * jnp.exp(b), st, "nt")
    st_new = st * jnp.exp(b_last) + _mm(v, kk * jnp.exp(b_last - b), "tn")
    yield
    for l in range(6):
        qs = qh * fac[l * c:(l + 1) * c, :]
        ks = kk * fac[(6 + l) * c:(7 + l) * c, :]
        scores = scores + pair[l] * _mm(qs, ks, "nt")
        if l % 2:
            yield
    o = _mm(scores, v, "nn") + inter
    yield
    y = _rms(o, nw) * jax.nn.silu(og)
    return [st_new], [y]


def _ssd_consts():
    t = np.arange(SSM_CHUNK)
    tril = (t[None, :] <= t[:, None]).astype(np.float32)
    return [jnp.asarray(tril, BF16), jnp.asarray(tril)]


def _ssd_step(carry, xs, params, consts):
    st, halo = carry
    blk = xs[0].astype(F32)
    conv_w, conv_b, dtb, alog, dskip, nw = params
    tril_b, tril = consts
    c = SSM_CHUNK
    raw, dtr, z = blk[:, :SSM_XBC], blk[:, SSM_XBC:SSM_XBC + 128], blk[:, SSM_XBC + 128:]
    act = jax.nn.silu(_causal_conv(halo, raw, conv_w, conv_b))
    xh, bm, cm = act[:, :SSM_GW], act[:, SSM_GW:SSM_GW + SSM_DSTATE], act[:, SSM_GW + SSM_DSTATE:]
    dt = _softplus(dtr + dtb)
    da = dt * (-jnp.exp(alog))
    acum = _cmm(tril_b, da)
    acum_t = acum.T
    a_last = jnp.sum(da, axis=0, keepdims=True)
    cb_causal = _mm(cm, bm, "nt") * tril
    lane = lax.broadcasted_iota(jnp.int32, (c, 128), 1)
    row = lax.broadcasted_iota(jnp.int32, (128, 128), 0)
    first = lane < SSM_HEADDIM
    ys, st_new = [], []
    for j in range(SSM_HPG // 2):
        xp = xh[:, 128 * j:128 * (j + 1)]
        sp = st[128 * j:128 * (j + 1), :]
        r0, r1 = 2 * j, 2 * j + 1
        col = lambda a, r: jnp.broadcast_to(a[:, r:r + 1], (c, 128))
        xdt = xp * jnp.where(first, col(dt, r0), col(dt, r1))
        yj = _mm(cm, sp, "nt") * jnp.exp(jnp.where(first, col(acum, r0), col(acum, r1)))
        for r, keep in ((r0, first), (r1, ~first)):
            dec = jnp.broadcast_to(acum[:, r:r + 1], (c, c)) - jnp.broadcast_to(acum_t[r:r + 1, :], (c, c))
            m = cb_causal * jnp.exp(jnp.minimum(dec, 0.0))
            yj = yj + _mm(m, jnp.where(keep, xdt, 0.0), "nn")
        al0, al1 = a_last[:, r0:r0 + 1], a_last[:, r1:r1 + 1]
        wts = jnp.exp(jnp.where(first, al0 - col(acum, r0), al1 - col(acum, r1)))
        st_new.append(jnp.where(row < SSM_HEADDIM, jnp.exp(al0), jnp.exp(al1)) * sp + _mm(xdt * wts, bm, "tn"))
        ys.append(yj)
    y = jnp.concatenate(ys, axis=1) + dskip * xh
    y = _rms(y * jax.nn.silu(z), nw)
    return [jnp.concatenate(st_new, axis=0), raw[c - HALO:, :]], [y]


def _ffn_step(carry, xs, params, consts):
    (halo,) = carry
    blk = xs[0].astype(F32)
    conv_w, conv_b = params
    gate, up = blk[:, :FFN_GW], blk[:, FFN_GW:]
    a = jax.nn.gelu(_causal_conv(halo, gate, conv_w, conv_b), approximate=True) * up
    return [gate[gate.shape[0] - HALO:, :]], [a]


def _pre_step(carry, xs, params, consts):
    return [], [_rms(xs[0], params[0])]


def _mix_step(carry, xs, params, consts):
    gates, uh, us = (a.astype(F32) for a in xs)
    return [], [jax.nn.sigmoid(gates[:, :D_MODEL]) * uh + jax.nn.sigmoid(gates[:, D_MODEL:]) * us]


def _post_step(carry, xs, params, consts):
    x, v = xs
    x1 = x + _rms(v, params[0])
    return [], [x1, _rms(x1, params[1])]


def _scan_call(step, *, name, rows, chunk, nc, groups, xs, cins=(), params=(), consts=(), carries=(), ys=(), couts=(),
               accs=(), reverse=False, gpb=1, multi=False):
    blk_rows = chunk * nc
    nb = rows // blk_rows
    n_chunks = rows // chunk
    assert nb * blk_rows == rows and groups % gpb == 0
    rb = (lambda i: nb - 1 - i) if reverse else (lambda i: i)
    n_x, n_ci, n_p, n_c = len(xs), len(cins), len(params), len(consts)
    n_y, n_co, n_a = len(ys), len(couts), len(accs)

    def chunk_spec(shape):
        zeros = (0,) * len(shape)
        return pl.BlockSpec((gpb, nc) + tuple(shape), lambda g, i: (g, rb(i)) + zeros)

    in_specs = [pl.BlockSpec((blk_rows, gpb * w), lambda g, i: (rb(i), g)) for _, w in xs]
    in_specs += [chunk_spec(a.shape[2:]) for a in cins]
    in_specs += [pl.BlockSpec((gpb,) + tuple(a.shape[1:]), lambda g, i: (g, 0, 0)) for a in params]
    in_specs += [pl.BlockSpec(a.shape, (lambda nd: lambda g, i: (0,) * nd)(a.ndim)) for a in consts]
    out_specs = [pl.BlockSpec((blk_rows, gpb * w), lambda g, i: (rb(i), g)) for w, _ in ys]
    out_specs += [chunk_spec(s) for s in couts]
    out_specs += [pl.BlockSpec((gpb, r, c), lambda g, i: (g, 0, 0)) for r, c in accs]
    out_shape = [jax.ShapeDtypeStruct((rows, groups * w), dt) for w, dt in ys]
    out_shape += [jax.ShapeDtypeStruct((groups, n_chunks) + tuple(s), F32) for s in couts]
    out_shape += [jax.ShapeDtypeStruct((groups, r, c), F32) for r, c in accs]
    x_widths = [w for _, w in xs]
    y_widths = [w for w, _ in ys]

    def body(*refs):
        x_refs = refs[:n_x]
        ci_refs = refs[n_x:n_x + n_ci]
        p_refs = refs[n_x + n_ci:n_x + n_ci + n_p]
        c_refs = refs[n_x + n_ci + n_p:n_x + n_ci + n_p + n_c]
        o = n_x + n_ci + n_p + n_c
        y_refs = refs[o:o + n_y]
        co_refs = refs[o + n_y:o + n_y + n_co]
        a_refs = refs[o + n_y + n_co:o + n_y + n_co + n_a]
        carry_refs = refs[o + n_y + n_co + n_a:]

        @pl.when(pl.program_id(1) == 0)
        def _():
            for s in carry_refs:
                s[...] = jnp.zeros(s.shape, F32)
            for a in a_refs:
                a[...] = jnp.zeros(a.shape, F32)

        cvals = [c[...] for c in c_refs]

        def one_chunk(i, _):
            c = (nc - 1 - i) if reverse else i
            r0 = c * chunk if isinstance(c, int) else pl.multiple_of(c * chunk, chunk)
            loaded = []
            for u in range(gpb):
                carry = [s[u] for s in carry_refs]
                xv = [x[pl.ds(r0, chunk), u * w:(u + 1) * w] for x, w in zip(x_refs, x_widths)]
                civ = [ci[u, c] for ci in ci_refs]
                loaded.append((carry, xv, civ, [p[u] for p in p_refs]))
            results = step(loaded, cvals) if multi else [step(*args, cvals) for args in loaded]
            for u, (new_carry, yv, cov, av) in enumerate(results):
                for s, val in zip(carry_refs, new_carry):
                    s[u] = val
                for y, w, val in zip(y_refs, y_widths, yv):
                    y[pl.ds(r0, chunk), u * w:(u + 1) * w] = val.astype(y.dtype)
                for co, val in zip(co_refs, cov):
                    co[u, c] = val
                for a, val in zip(a_refs, av):
                    a[u] += val
            return 0

        if nc == 1:
            one_chunk(0, 0)
        else:
            lax.fori_loop(0, nc, one_chunk, 0)

    outs = pl.pallas_call(
        body, name=name, grid=(groups // gpb, nb), in_specs=in_specs, out_specs=out_specs, out_shape=out_shape,
        scratch_shapes=[pltpu.VMEM((gpb,) + tuple(s), F32) for s in carries],
        compiler_params=pltpu.CompilerParams(dimension_semantics=("arbitrary", "arbitrary"),
                                             vmem_limit_bytes=VMEM_LIMIT),
    )(*[a for a, _ in xs], *cins, *params, *consts)
    return outs[:n_y], outs[n_y:n_y + n_co], outs[n_y + n_co:]


def _run_interleaved(step, arg_tuples):
    runs = [step(*args) for args in arg_tuples]
    if not hasattr(runs[0], "send"):
        return runs
    results, live = [None] * len(runs), list(range(len(runs)))
    while live:
        for u in list(live):
            try:
                next(runs[u])
            except StopIteration as done:
                results[u] = done.value
                live.remove(u)
    return results


def _stage_fwd(step, *, name, rows, chunk, nc, groups, xs, params, consts, carries, ys, gpb=1):
    def fstep(loaded, cv):
        outs = _run_interleaved(step, [(carry, xv, pv, cv) for carry, xv, _, pv in loaded])
        return [(new_carry, yv, carry, []) for (new_carry, yv), (carry, _, _, _) in zip(outs, loaded)]

    yv, saved, _ = _scan_call(fstep, name=name, rows=rows, chunk=chunk, nc=nc, groups=groups, xs=xs, params=params,
                              consts=consts, carries=carries, ys=ys, couts=carries, gpb=gpb, multi=True)
    return yv, saved


def _stage_bwd(step, *, name, rows, chunk, nc, groups, xs, saved, params, consts, carries, dys, dxs, gpb=1):
    n_x = len(xs)

    def bstep(loaded, cv):
        civs = [list(civ) for _, _, civ, _ in loaded]
        xvs = [list(xv_all[:n_x]) for _, xv_all, _, _ in loaded]
        pvs = [list(pv) for _, _, _, pv in loaded]
        cts = [(list(dcarry), [d.astype(F32) for d in xv_all[n_x:]]) for dcarry, xv_all, _, _ in loaded]

        def fwd(civs_, xvs_, pvs_):
            outs = _run_interleaved(step, [(c_, x_, p_, cv) for c_, x_, p_ in zip(civs_, xvs_, pvs_)])
            return [(list(new_carry), list(yv)) for new_carry, yv in outs]

        _, vjp = jax.vjp(fwd, civs, xvs, pvs)
        dcivs, dxvs, dpvs = vjp(cts)
        return [(dc, dx, [], dp) for dc, dx, dp in zip(dcivs, dxvs, dpvs)]

    dxv, _, dpv = _scan_call(bstep, name=name, rows=rows, chunk=chunk, nc=nc, groups=groups, xs=list(xs) + list(dys),
                             cins=saved, params=params, consts=consts, carries=carries,
                             ys=[(w, dt) for (_, w), dt in zip(xs, dxs)], accs=[a.shape[1:] for a in params],
                             reverse=True, gpb=gpb, multi=True)
    return dxv, dpv


def _mm_params(sem):
    return pltpu.CompilerParams(dimension_semantics=sem, vmem_limit_bytes=VMEM_LIMIT)


def _matmul_nt(a, b, *, name, tm=1024, tn=512):
    m, k = a.shape
    n = b.shape[0]
    tm = min(tm, m)

    def body(a_ref, b_ref, o_ref):
        o_ref[...] = lax.dot_general(a_ref[...], b_ref[...], (_DIMS["nt"], ((), ())), preferred_element_type=F32)

    return pl.pallas_call(
        body, name=name, grid=(m // tm, n // tn),
        in_specs=[pl.BlockSpec((tm, k), lambda i, j: (i, 0)), pl.BlockSpec((tn, k), lambda i, j: (j, 0))],
        out_specs=pl.BlockSpec((tm, tn), lambda i, j: (i, j)),
        out_shape=jax.ShapeDtypeStruct((m, n), F32),
        compiler_params=_mm_params(("parallel", "arbitrary")),
    )(a, b)


def _matmul_nn(a, b, *, name):
    m, k = a.shape
    n = b.shape[1]
    tm, tn = (1024, 512) if k <= 4096 else (1024, 256) if k <= 6144 else (512, 512) if k <= 8192 else (512, 256)
    tm = min(tm, m)

    def body(a_ref, b_ref, o_ref):
        o_ref[...] = jnp.dot(a_ref[...], b_ref[...], preferred_element_type=F32)

    return pl.pallas_call(
        body, name=name, grid=(m // tm, n // tn),
        in_specs=[pl.BlockSpec((tm, k), lambda i, j: (i, 0)), pl.BlockSpec((k, tn), lambda i, j: (0, j))],
        out_specs=pl.BlockSpec((tm, tn), lambda i, j: (i, j)),
        out_shape=jax.ShapeDtypeStruct((m, n), F32),
        compiler_params=_mm_params(("parallel", "arbitrary")),
    )(a, b)


def _matmul_tn(x, y, *, name, tp=512, tq=512):
    t, p = x.shape
    q = y.shape[1]

    def body(x_ref, y_ref, o_ref):
        o_ref[...] = lax.dot_general(x_ref[...], y_ref[...], (_DIMS["tn"], ((), ())),
                                     preferred_element_type=F32).astype(o_ref.dtype)

    return pl.pallas_call(
        body, name=name, grid=(p // tp, q // tq),
        in_specs=[pl.BlockSpec((t, tp), lambda i, j: (0, i)), pl.BlockSpec((t, tq), lambda i, j: (0, j))],
        out_specs=pl.BlockSpec((tp, tq), lambda i, j: (i, j)),
        out_shape=jax.ShapeDtypeStruct((p, q), BF16),
        compiler_params=_mm_params(("parallel", "arbitrary")),
    )(x, y)


N_CHIPS = N_DEV // 2


def _all_gather(arrays, *, name):
    n = len(arrays)
    out_shape = [jax.ShapeDtypeStruct((N_DEV,) + tuple(a.shape), a.dtype) for a in arrays]

    def body(*refs):
        in_refs, out_refs = refs[:n], refs[n:2 * n]
        send_sems, recv_sems, local_sems = refs[2 * n:]
        x, y, c = lax.axis_index("x"), lax.axis_index("y"), lax.axis_index("c")
        me, sibling = (x, y, c), (x, y, 1 - c)
        chips = [(1 - x, y), (x, 1 - y), (1 - x, 1 - y)]

        def copy(a, k, block, to, src=None):
            slot = out_refs[a].at[4 * block[0] + 2 * block[1] + block[2]]
            return pltpu.make_async_remote_copy(
                src_ref=slot if src is None else src, dst_ref=slot, send_sem=send_sems.at[a, k],
                recv_sem=recv_sems.at[a, k], device_id=to, device_id_type=pl.DeviceIdType.MESH)

        mine = [pltpu.make_async_copy(in_refs[a], out_refs[a].at[4 * x + 2 * y + c], local_sems.at[a]) for a in range(n)]
        first = []
        for a in range(n):
            first.append(copy(a, 0, me, sibling, src=in_refs[a]))
            first += [copy(a, 1 + j, me, (*chip, c), src=in_refs[a]) for j, chip in enumerate(chips)]
        for cp in mine + first:
            cp.start()
        passed = []
        for j, chip in enumerate(chips):
            for a in range(n):
                copy(a, 1 + j, (*chip, c), me).wait_recv()
                passed.append(copy(a, 4 + j, (*chip, c), sibling))
                passed[-1].start()
        for a in range(n):
            copy(a, 0, sibling, me).wait_recv()
            for j, chip in enumerate(chips):
                copy(a, 4 + j, (*chip, 1 - c), me).wait_recv()
        for cp in first + passed:
            cp.wait_send()
        for cp in mine:
            cp.wait()

    any_spec = pl.BlockSpec(memory_space=pl.ANY)
    return pl.pallas_call(
        body, name=name, in_specs=[any_spec] * n, out_specs=[any_spec] * n, out_shape=out_shape,
        scratch_shapes=[pltpu.SemaphoreType.DMA((n, N_DEV - 1)), pltpu.SemaphoreType.DMA((n, N_DEV - 1)),
                        pltpu.SemaphoreType.DMA((n,))],
        compiler_params=pltpu.CompilerParams(has_side_effects=True),
    )(*arrays)


def _push(arrays, *, name, out_slots, plan):
    n = len(arrays)
    out_shape = [jax.ShapeDtypeStruct((out_slots,) + tuple(a.shape[1:]), a.dtype) for a in arrays]
    n_tr = len(plan(0, 0, 0)[0])

    def body(*refs):
        in_refs, out_refs = refs[:n], refs[n:2 * n]
        send_sems, recv_sems, local_sems = refs[2 * n:]
        transfers, local = plan(lax.axis_index("x"), lax.axis_index("y"), lax.axis_index("c"))
        copies = []
        for a in range(n):
            if local is not None:
                copies.append(pltpu.make_async_copy(in_refs[a].at[local[0]], out_refs[a].at[local[1]], local_sems.at[a]))
            for k, (peer, src, dst) in enumerate(transfers):
                copies.append(pltpu.make_async_remote_copy(
                    src_ref=in_refs[a].at[src], dst_ref=out_refs[a].at[dst], send_sem=send_sems.at[a, k],
                    recv_sem=recv_sems.at[a, k], device_id=peer, device_id_type=pl.DeviceIdType.MESH))
        for cp in copies:
            cp.start()
        for cp in copies:
            cp.wait()

    any_spec = pl.BlockSpec(memory_space=pl.ANY)
    return pl.pallas_call(
        body, name=name, in_specs=[any_spec] * n, out_specs=[any_spec] * n, out_shape=out_shape,
        scratch_shapes=[pltpu.SemaphoreType.DMA((n, n_tr)), pltpu.SemaphoreType.DMA((n, n_tr)),
                        pltpu.SemaphoreType.DMA((n,))],
        compiler_params=pltpu.CompilerParams(has_side_effects=True),
    )(*arrays)


def _plan_everyone(x, y, c):
    me = 4 * x + 2 * y + c
    peers = [(1 - x if k & 4 else x, 1 - y if k & 2 else y, 1 - c if k & 1 else c) for k in range(1, N_DEV)]
    return [(p, 0, me) for p in peers], (0, me)


def _plan_sibling(x, y, c):
    return [((x, y, 1 - c), 2 * chip + (1 - c), chip) for chip in range(N_CHIPS)], None


def _plan_chips(x, y, c):
    mine = 2 * x + y
    peers = [(1 - x, y), (x, 1 - y), (1 - x, 1 - y)]
    return [((px, py, c), 2 * px + py, mine) for px, py in peers], (mine, mine)


def _pair_sum(parts, received, *, name, tc):
    _, r, c = parts.shape
    core = lax.axis_index("c").astype(jnp.int32).reshape(1)

    def body(core_ref, p_ref, r_ref, o_ref):
        o_ref[...] = (p_ref[...].astype(F32) + r_ref[...].astype(F32)).astype(o_ref.dtype)

    return pl.pallas_call(
        body, name=name,
        grid_spec=pltpu.PrefetchScalarGridSpec(
            num_scalar_prefetch=1, grid=(N_CHIPS, c // tc),
            in_specs=[pl.BlockSpec((None, r, tc), lambda i, j, core_ref: (2 * i + core_ref[0], 0, j)),
                      pl.BlockSpec((None, r, tc), lambda i, j, core_ref: (i, 0, j))],
            out_specs=pl.BlockSpec((None, r, tc), lambda i, j, core_ref: (i, 0, j))),
        out_shape=jax.ShapeDtypeStruct((N_CHIPS, r, c), BF16),
        compiler_params=pltpu.CompilerParams(dimension_semantics=("parallel", "parallel"), vmem_limit_bytes=VMEM_LIMIT),
    )(core, parts, received)


def _sum_blocks(a, *, name, tc):
    nblk, r, c = a.shape

    def body(a_ref, o_ref):
        acc = a_ref[0].astype(F32)
        for i in range(1, nblk):
            acc = acc + a_ref[i].astype(F32)
        o_ref[...] = acc

    return pl.pallas_call(
        body, name=name, grid=(c // tc,),
        in_specs=[pl.BlockSpec((nblk, r, tc), lambda j: (0, 0, j))],
        out_specs=pl.BlockSpec((r, tc), lambda j: (0, j)),
        out_shape=jax.ShapeDtypeStruct((r, c), F32),
        compiler_params=pltpu.CompilerParams(dimension_semantics=("parallel",), vmem_limit_bytes=VMEM_LIMIT),
    )(a)


def _adamw(w, g, m, v, *, name, tr):
    r, c = w.shape

    def body(w_ref, g_ref, m_ref, v_ref, d_ref, mo_ref, vo_ref):
        gv = g_ref[...]
        mn = ADAM_B1 * m_ref[...] + (1.0 - ADAM_B1) * gv
        vn = ADAM_B2 * v_ref[...] + (1.0 - ADAM_B2) * jnp.square(gv)
        m_hat = mn / (1.0 - ADAM_B1 ** ADAM_STEP)
        v_hat = vn / (1.0 - ADAM_B2 ** ADAM_STEP)
        d_ref[...] = -ADAM_LR * (m_hat / (jnp.sqrt(v_hat) + ADAM_EPS) + ADAM_WD * w_ref[...])
        mo_ref[...] = mn
        vo_ref[...] = vn

    spec = pl.BlockSpec((tr, c), lambda i: (i, 0))
    return pl.pallas_call(
        body, name=name, grid=(r // tr,), in_specs=[spec] * 4, out_specs=[spec] * 3,
        out_shape=[jax.ShapeDtypeStruct((r, c), F32)] * 3,
        compiler_params=pltpu.CompilerParams(dimension_semantics=("parallel",), vmem_limit_bytes=VMEM_LIMIT),
    )(w, g, m, v)


def _in_proj_layout():
    z0, xbc0, dt0, gate0 = 8192, 12288, 18432, 18496
    hg = []
    for h in range(HG_HEADS):
        for part in range(4):
            hg.append(part * 2048 + h * HG_DK + np.arange(HG_DK))
    ssm = []
    for g in range(SSM_GROUPS):
        ssm.append(xbc0 + g * SSM_GW + np.arange(SSM_GW))
        ssm.append(xbc0 + SSM_DINNER + g * SSM_DSTATE + np.arange(SSM_DSTATE))
        ssm.append(xbc0 + SSM_DINNER + SSM_GROUPS * SSM_DSTATE + g * SSM_DSTATE + np.arange(SSM_DSTATE))
        ssm.append(np.concatenate([dt0 + g * SSM_HPG + np.arange(SSM_HPG), -np.ones(128 - SSM_HPG, np.int64)]))
        ssm.append(z0 + g * SSM_GW + np.arange(SSM_GW))
    gate = gate0 + np.arange(2 * D_MODEL)
    return np.concatenate(hg), np.concatenate(ssm), gate


def _conv_layout():
    idx = []
    for g in range(SSM_GROUPS):
        idx.append(np.concatenate([g * SSM_GW + np.arange(SSM_GW),
                                   SSM_DINNER + g * SSM_DSTATE + np.arange(SSM_DSTATE),
                                   SSM_DINNER + SSM_GROUPS * SSM_DSTATE + g * SSM_DSTATE + np.arange(SSM_DSTATE)]))
    return np.stack(idx)


def _up_layout():
    idx = []
    for g in range(FFN_G):
        idx.append(g * FFN_GW + np.arange(FFN_GW))
        idx.append(D_FF + g * FFN_GW + np.arange(FFN_GW))
    return np.concatenate(idx)


def _inverse(idx, n):
    inv = np.zeros(n, np.int64)
    pos = np.nonzero(idx >= 0)[0]
    inv[idx[pos]] = pos
    return inv


def _take_rows(a, idx, axis=0):
    idx = np.asarray(idx).reshape(-1)
    pieces, start = [], 0
    for i in range(1, len(idx) + 1):
        same_run = i < len(idx) and ((idx[i] == idx[i - 1] + 1 and idx[i - 1] >= 0) or (idx[i] < 0 and idx[i - 1] < 0))
        if same_run:
            continue
        n = i - start
        if idx[start] < 0:
            shape = list(a.shape)
            shape[axis] = n
            pieces.append(jnp.zeros(shape, a.dtype))
        else:
            pieces.append(lax.slice_in_dim(a, int(idx[start]), int(idx[start]) + n, axis=axis))
        start = i
    return pieces[0] if len(pieces) == 1 else jnp.concatenate(pieces, axis=axis)


_SMALL = (("mix_pre_norm", (1, 2048)), ("mix_post_norm", (1, 2048)), ("hg_lb_table", (2, 2048)), ("hg_out_norm", (1, 128)),
          ("ssm_conv_w", (4, 6144)), ("ssm_conv_b", (1, 6144)), ("ssm_dt_bias", (1, 64)), ("ssm_A_log", (1, 64)),
          ("ssm_D", (1, 64)), ("ssm_out_norm", (1, 4096)), ("ffn_pre_norm", (1, 2048)), ("ffn_post_norm", (1, 2048)),
          ("ffn_conv_w", (3, 5632)), ("ffn_conv_b", (1, 5632)), ("loss", (1, 1)))
_PACK_ROWS = 8 * (-(-sum(int(np.prod(s)) for _, s in _SMALL) // 1024))


def _pack(vals):
    flat = jnp.concatenate([vals[k].astype(F32).reshape(-1) for k, _ in _SMALL])
    return jnp.pad(flat, (0, _PACK_ROWS * 128 - flat.shape[0])).reshape(_PACK_ROWS, 128)


def _unpack(packed):
    flat, out, o = packed.reshape(-1), {}, 0
    for k, s in _SMALL:
        n = int(np.prod(s))
        out[k] = flat[o:o + n].reshape(s)
        o += n
    return out


def _local_step(x, target, w, p):
    t = x.shape[0]
    one = lambda a: a.reshape((1,) + a.shape)
    row = dict(rows=t, groups=1, consts=[], carries=[])

    (h1,), _ = _stage_fwd(_pre_step, name="pre_fwd", chunk=512, nc=1, xs=[(x, D_MODEL)], params=[one(p["mix_pre_norm"])],
                          ys=[(D_MODEL, BF16)], **row)
    proj_hg = _matmul_nt(h1, w["in_hg"], name="proj_hg")
    proj_ssm = _matmul_nt(h1, w["in_ssm"], name="proj_ssm")
    proj_gate = _matmul_nt(h1, w["in_gate"], name="proj_gate")

    hg = dict(rows=t, chunk=HG_CHUNK, nc=8, groups=HG_HEADS, xs=[(proj_hg, HG_BLK)], params=[p["hg_tab"], p["hg_nw"]],
              consts=_hg_consts(), carries=[(HG_DK, HG_DK)], gpb=4)
    (y_hg,), hg_saved = _stage_fwd(_hg_step, name="hg_fwd", ys=[(HG_DK, BF16)], **hg)

    ssd = dict(rows=t, chunk=SSM_CHUNK, nc=2, groups=SSM_GROUPS, xs=[(proj_ssm, SSM_BLK)],
               params=[p["conv_w"], p["conv_b"], p["dt_bias"], p["a_log"], p["d_skip"], p["ssm_nw"]],
               consts=_ssd_consts(), carries=[(4 * 128, SSM_DSTATE), (HALO, SSM_XBC)])
    (y_ssm,), ssd_saved = _stage_fwd(_ssd_step, name="ssd_fwd", ys=[(SSM_GW, BF16)], **ssd)

    u_hg = _matmul_nn(y_hg, w["branch_hg"], name="branch_hg")
    u_ssm = _matmul_nn(y_ssm, w["branch_ssm"], name="branch_ssm")
    mix = dict(chunk=256, nc=1, xs=[(proj_gate, 2 * D_MODEL), (u_hg, D_MODEL), (u_ssm, D_MODEL)], params=[], **row)
    (mixed,), _ = _stage_fwd(_mix_step, name="mix_fwd", ys=[(D_MODEL, BF16)], **mix)
    v = _matmul_nn(mixed, w["out"], name="out_proj")
    post = dict(chunk=256, nc=1, xs=[(x, D_MODEL), (v, D_MODEL)],
                params=[one(p["mix_post_norm"]), one(p["ffn_pre_norm"])], **row)
    (x1, h2), _ = _stage_fwd(_post_step, name="post_fwd", ys=[(D_MODEL, F32), (D_MODEL, BF16)], **post)
    gu = _matmul_nt(h2, w["up"], name="ffn_up")
    ffn = dict(rows=t, chunk=256, nc=2, groups=FFN_G, xs=[(gu, 2 * FFN_GW)], params=[p["ffn_conv_w"], p["ffn_conv_b"]],
               consts=[], carries=[(HALO, FFN_GW)])
    (act,), ffn_saved = _stage_fwd(_ffn_step, name="ffn_fwd", ys=[(FFN_GW, BF16)], **ffn)
    d = _matmul_nn(act, w["down"], name="ffn_down")

    def head_step(carry, xv, civ, pv, cv):
        x1_, d_, tgt = xv

        def per_row_loss(a, b, nw):
            e = a + _rms(b, nw) - tgt
            return 0.5 * jnp.mean(e * e, axis=1, keepdims=True)

        lrow, vjp = jax.vjp(per_row_loss, x1_, d_, pv[0])
        dx1_, dd_, dnw = vjp(jnp.ones_like(lrow))
        loss = jnp.broadcast_to(jnp.sum(lrow, axis=0, keepdims=True), (1, 128))
        return [], [dx1_, dd_], [], [dnw, loss]

    (dy, dd), _, (g_ffn_post, loss) = _scan_call(
        head_step, name="loss_head", chunk=256, nc=1, xs=[(x1, D_MODEL), (d, D_MODEL), (target, D_MODEL)],
        params=[one(p["ffn_post_norm"])], ys=[(D_MODEL, F32), (D_MODEL, BF16)], accs=[(1, D_MODEL), (1, 128)], **row)

    gw = {}
    gw["down"] = _matmul_tn(act, dd, name="g_down")
    dact = _matmul_nt(dd, w["down"], name="d_act")
    (dgu,), (g_fcw, g_fcb) = _stage_bwd(_ffn_step, name="ffn_bwd", saved=ffn_saved, dys=[(dact, FFN_GW)], dxs=[BF16], **ffn)
    gw["up"] = _matmul_tn(dgu, h2, name="g_up")
    dh2 = _matmul_nn(dgu, w["up"], name="d_h2")
    (dx1, dv), (g_mix_post, g_ffn_pre) = _stage_bwd(_post_step, name="post_bwd", saved=[], dys=[(dy, D_MODEL), (dh2, D_MODEL)],
                                                    dxs=[F32, BF16], **post)
    gw["out"] = _matmul_tn(mixed, dv, name="g_out")
    dmixed = _matmul_nt(dv, w["out"], name="d_mixed")
    (dgate, du_hg, du_ssm), _ = _stage_bwd(_mix_step, name="mix_bwd", saved=[], dys=[(dmixed, D_MODEL)],
                                           dxs=[BF16, BF16, BF16], **mix)
    gw["branch_hg"] = _matmul_tn(y_hg, du_hg, name="g_branch_hg")
    gw["branch_ssm"] = _matmul_tn(y_ssm, du_ssm, name="g_branch_ssm")
    dy_hg = _matmul_nt(du_hg, w["branch_hg"], name="d_y_hg")
    dy_ssm = _matmul_nt(du_ssm, w["branch_ssm"], name="d_y_ssm")
    (dproj_ssm,), g_ssd = _stage_bwd(_ssd_step, name="ssd_bwd", saved=ssd_saved, dys=[(dy_ssm, SSM_GW)], dxs=[BF16], **ssd)
    (dproj_hg,), (g_tab, g_hg_nw) = _stage_bwd(_hg_step, name="hg_bwd", saved=hg_saved, dys=[(dy_hg, HG_DK)], dxs=[BF16], **hg)
    gw["in_hg"] = _matmul_tn(dproj_hg, h1, name="g_in_hg")
    gw["in_ssm"] = _matmul_tn(dproj_ssm, h1, name="g_in_ssm")
    gw["in_gate"] = _matmul_tn(dgate, h1, name="g_in_gate")
    dh_a = _matmul_nn(dproj_hg, w["in_hg"], name="d_h1_hg")
    dh_b = _matmul_nn(dproj_ssm, w["in_ssm"], name="d_h1_ssm")
    dh_c = _matmul_nn(dgate, w["in_gate"], name="d_h1_gate")

    def pre_bwd_step(carry, xv, civ, pv, cv):
        x_, da, db, dc, dres = xv
        _, vjp = jax.vjp(_rms, x_, pv[0])
        dx_, dnw = vjp(da + db + dc)
        return [], [dx_ + dres], [], [dnw]

    (grad_x,), _, (g_mix_pre,) = _scan_call(
        pre_bwd_step, name="pre_bwd", chunk=256, nc=1,
        xs=[(x, D_MODEL), (dh_a, D_MODEL), (dh_b, D_MODEL), (dh_c, D_MODEL), (dx1, D_MODEL)],
        params=[one(p["mix_pre_norm"])], ys=[(D_MODEL, F32)], accs=[(1, D_MODEL)], **row)

    gp = dict(mix_pre_norm=g_mix_pre[0], mix_post_norm=g_mix_post[0], ffn_pre_norm=g_ffn_pre[0], ffn_post_norm=g_ffn_post[0],
              hg_tab=g_tab, hg_nw=g_hg_nw, conv_w=g_ssd[0], conv_b=g_ssd[1], dt_bias=g_ssd[2], a_log=g_ssd[3],
              d_skip=g_ssd[4], ssm_nw=g_ssd[5], ffn_conv_w=g_fcw, ffn_conv_b=g_fcb, loss=loss[0, :, :1])
    return grad_x, gw, gp


def _small_to_kernel_layout(s):
    conv_idx = _conv_layout()
    pad_heads = lambda a: jnp.pad(a.reshape(SSM_GROUPS, 1, SSM_HPG), ((0, 0), (0, 0), (0, 128 - SSM_HPG)))
    return dict(
        mix_pre_norm=s["mix_pre_norm"], mix_post_norm=s["mix_post_norm"], ffn_pre_norm=s["ffn_pre_norm"],
        ffn_post_norm=s["ffn_post_norm"],
        hg_tab=s["hg_lb_table"].reshape(2, HG_HEADS, HG_DK).transpose(1, 0, 2),
        hg_nw=jnp.broadcast_to(s["hg_out_norm"].reshape(1, 1, HG_DK), (HG_HEADS, 1, HG_DK)),
        conv_w=_take_rows(s["ssm_conv_w"], conv_idx, axis=1).reshape(SSM_CONV, SSM_GROUPS, SSM_XBC).transpose(1, 0, 2),
        conv_b=_take_rows(s["ssm_conv_b"], conv_idx, axis=1).reshape(SSM_GROUPS, 1, SSM_XBC),
        dt_bias=pad_heads(s["ssm_dt_bias"]), a_log=pad_heads(s["ssm_A_log"]),
        d_skip=jnp.repeat(s["ssm_D"].reshape(SSM_HEADS), SSM_HEADDIM).reshape(SSM_GROUPS, 1, SSM_GW),
        ssm_nw=s["ssm_out_norm"].reshape(SSM_GROUPS, 1, SSM_GW),
        ffn_conv_w=s["ffn_conv_w"].reshape(FFN_CONV, FFN_G, FFN_GW).transpose(1, 0, 2),
        ffn_conv_b=s["ffn_conv_b"].reshape(FFN_G, 1, FFN_GW),
    )


def _small_from_kernel_layout(g):
    conv_inv = _inverse(_conv_layout().reshape(-1), SSM_CONV_DIM)
    heads = lambda a: a[:, 0, :SSM_HPG].reshape(1, SSM_HEADS)
    return dict(
        mix_pre_norm=g["mix_pre_norm"], mix_post_norm=g["mix_post_norm"], ffn_pre_norm=g["ffn_pre_norm"],
        ffn_post_norm=g["ffn_post_norm"],
        hg_lb_table=g["hg_tab"].transpose(1, 0, 2).reshape(2, HG_HEADS * HG_DK),
        hg_out_norm=jnp.sum(g["hg_nw"], axis=0),
        ssm_conv_w=_take_rows(g["conv_w"].transpose(1, 0, 2).reshape(SSM_CONV, -1), conv_inv, axis=1),
        ssm_conv_b=_take_rows(g["conv_b"].reshape(1, -1), conv_inv, axis=1),
        ssm_dt_bias=heads(g["dt_bias"]), ssm_A_log=heads(g["a_log"]),
        ssm_D=jnp.sum(g["d_skip"].reshape(SSM_HEADS, SSM_HEADDIM), axis=1).reshape(1, SSM_HEADS),
        ssm_out_norm=g["ssm_nw"].reshape(1, SSM_DINNER),
        ffn_conv_w=g["ffn_conv_w"].transpose(1, 0, 2).reshape(FFN_CONV, D_FF),
        ffn_conv_b=g["ffn_conv_b"].reshape(1, D_FF),
        loss=g["loss"],
    )


def kernel(x, w_in, mix_pre_norm, mix_post_norm, hg_lb_table, hg_out_norm, ssm_conv_w, ssm_conv_b, ssm_dt_bias, ssm_A_log, ssm_D, ssm_out_norm, w_branch_hg, w_branch_ssm, w_out, ffn_pre_norm, ffn_post_norm, ffn_w_up, ffn_conv_w, ffn_conv_b, ffn_w_down, loss_target, m_w_in, m_mix_pre_norm, m_mix_post_norm, m_hg_lb_table, m_hg_out_norm, m_ssm_conv_w, m_ssm_conv_b, m_ssm_dt_bias, m_ssm_A_log, m_ssm_D, m_ssm_out_norm, m_w_branch_hg, m_w_branch_ssm, m_w_out, m_ffn_pre_norm, m_ffn_post_norm, m_ffn_w_up, m_ffn_conv_w, m_ffn_conv_b, m_ffn_w_down, v_w_in, v_mix_pre_norm, v_mix_post_norm, v_hg_lb_table, v_hg_out_norm, v_ssm_conv_w, v_ssm_conv_b, v_ssm_dt_bias, v_ssm_A_log, v_ssm_D, v_ssm_out_norm, v_w_branch_hg, v_w_branch_ssm, v_w_out, v_ffn_pre_norm, v_ffn_post_norm, v_ffn_w_up, v_ffn_conv_w, v_ffn_conv_b, v_ffn_w_down):
    names = ["w_in", "mix_pre_norm", "mix_post_norm", "hg_lb_table", "hg_out_norm", "ssm_conv_w", "ssm_conv_b", "ssm_dt_bias",
             "ssm_A_log", "ssm_D", "ssm_out_norm", "w_branch_hg", "w_branch_ssm", "w_out", "ffn_pre_norm", "ffn_post_norm",
             "ffn_w_up", "ffn_conv_w", "ffn_conv_b", "ffn_w_down"]
    ws = dict(zip(names, (w_in, mix_pre_norm, mix_post_norm, hg_lb_table, hg_out_norm, ssm_conv_w, ssm_conv_b, ssm_dt_bias,
                          ssm_A_log, ssm_D, ssm_out_norm, w_branch_hg, w_branch_ssm, w_out, ffn_pre_norm, ffn_post_norm,
                          ffn_w_up, ffn_conv_w, ffn_conv_b, ffn_w_down)))
    ms = dict(zip(names, (m_w_in, m_mix_pre_norm, m_mix_post_norm, m_hg_lb_table, m_hg_out_norm, m_ssm_conv_w, m_ssm_conv_b,
                          m_ssm_dt_bias, m_ssm_A_log, m_ssm_D, m_ssm_out_norm, m_w_branch_hg, m_w_branch_ssm, m_w_out,
                          m_ffn_pre_norm, m_ffn_post_norm, m_ffn_w_up, m_ffn_conv_w, m_ffn_conv_b, m_ffn_w_down)))
    vs = dict(zip(names, (v_w_in, v_mix_pre_norm, v_mix_post_norm, v_hg_lb_table, v_hg_out_norm, v_ssm_conv_w, v_ssm_conv_b,
                          v_ssm_dt_bias, v_ssm_A_log, v_ssm_D, v_ssm_out_norm, v_w_branch_hg, v_w_branch_ssm, v_w_out,
                          v_ffn_pre_norm, v_ffn_post_norm, v_ffn_w_up, v_ffn_conv_w, v_ffn_conv_b, v_ffn_w_down)))
    me = 4 * lax.axis_index("x") + 2 * lax.axis_index("y") + lax.axis_index("c")

    gathered = _all_gather(
        [w_in[0].T.astype(BF16), ffn_w_up[0].T.astype(BF16), w_branch_hg[0].astype(BF16), w_branch_ssm[0].astype(BF16),
         w_out[0].astype(BF16), ffn_w_down[0].astype(BF16), ssm_conv_w[0], ffn_conv_w[0]], name="gather_weights")
    in_t = gathered[0].reshape(IN_TOTAL, D_MODEL)
    up_t = gathered[1].reshape(2 * D_FF, D_MODEL)
    idx_hg, idx_ssm, idx_gate = _in_proj_layout()
    up_idx = _up_layout()
    w = dict(in_hg=_take_rows(in_t, idx_hg), in_ssm=_take_rows(in_t, idx_ssm), in_gate=_take_rows(in_t, idx_gate),
             up=_take_rows(up_t, up_idx), branch_hg=gathered[2].reshape(D_MODEL, D_MODEL),
             branch_ssm=gathered[3].reshape(SSM_DINNER, D_MODEL), out=gathered[4].reshape(D_MODEL, D_MODEL),
             down=gathered[5].reshape(D_FF, D_MODEL))
    small = {k: ws[k] for k, _ in _SMALL[:-1]}
    small["ssm_conv_w"] = gathered[6].transpose(1, 0, 2).reshape(SSM_CONV, SSM_CONV_DIM)
    small["ffn_conv_w"] = gathered[7].transpose(1, 0, 2).reshape(FFN_CONV, D_FF)
    small = {k: small[k].reshape(s) for k, s in _SMALL[:-1]}

    grad_x, gw, gp = _local_step(x[0], loss_target[0], w, _small_to_kernel_layout(small))

    in_all = jnp.concatenate([gw["in_hg"], gw["in_ssm"], gw["in_gate"]], axis=0)
    in_inv = _inverse(np.concatenate([idx_hg, idx_ssm, idx_gate]), IN_TOTAL)
    parts = [_take_rows(in_all, in_inv).reshape(N_DEV, IN_TOTAL // N_DEV, D_MODEL),
             _take_rows(gw["up"], _inverse(up_idx, 2 * D_FF)).reshape(N_DEV, -1, D_MODEL),
             gw["branch_hg"].reshape(N_DEV, -1, D_MODEL), gw["branch_ssm"].reshape(N_DEV, -1, D_MODEL),
             gw["out"].reshape(N_DEV, -1, D_MODEL), gw["down"].reshape(N_DEV, -1, D_MODEL)]
    big_names = ["w_in", "ffn_w_up", "w_branch_hg", "w_branch_ssm", "w_out", "ffn_w_down"]
    from_sibling = _push(parts, name="grads_to_sibling", out_slots=N_CHIPS, plan=_plan_sibling)
    chip_sums = [_pair_sum(p, r, name="pair_sum_" + k, tc=256) for k, p, r in zip(big_names, parts, from_sibling)]
    from_chips = _push(chip_sums, name="grads_to_chips", out_slots=N_CHIPS, plan=_plan_chips)
    grads = {}
    for k, r in zip(big_names, from_chips):
        g = _sum_blocks(r, name="sum_" + k, tc=256)
        grads[k] = g.T if k in ("w_in", "ffn_w_up") else g
    small_all = _push([_pack(_small_from_kernel_layout(gp))[None]], name="small_to_everyone", out_slots=N_DEV,
                      plan=_plan_everyone)
    small_g = _unpack(_sum_blocks(small_all[0], name="sum_small", tc=128))
    loss = small_g.pop("loss").reshape(())
    for k, g in small_g.items():
        if k in ("ssm_conv_w", "ffn_conv_w"):
            n = g.shape[1] // N_DEV
            g = lax.dynamic_slice_in_dim(g, me * n, n, axis=1)
        grads[k] = g

    delta, new_m, new_v = {}, {}, {}
    for k in big_names:
        delta[k], new_m[k], new_v[k] = _adamw(ws[k][0], grads[k], ms[k][0], vs[k][0], name="adamw_" + k, tr=64)
    small_names = [k for k in names if k not in big_names]
    flat = lambda d: jnp.concatenate([d[k].astype(F32).reshape(-1) for k in small_names])
    n_small = sum(int(np.prod(ws[k].shape)) for k in small_names)
    rows = 8 * (-(-n_small // 1024))
    pack2 = lambda d: jnp.pad(flat(d), (0, rows * 128 - n_small)).reshape(rows, 128)
    v_packed = jnp.pad(flat(vs), (0, rows * 128 - n_small), constant_values=1.0).reshape(rows, 128)
    packed = _adamw(pack2(ws), pack2(grads), pack2(ms), v_packed, name="adamw_small", tr=rows)
    o = 0
    for k in small_names:
        n = int(np.prod(ws[k].shape))
        delta[k], new_m[k], new_v[k] = (a.reshape(-1)[o:o + n].reshape(ws[k].shape) for a in packed)
        o += n

    full = lambda d: [d[k].reshape(ws[k].shape) for k in names]
    return (loss, grad_x[None], *full(grads), *full(delta), *full(new_m), *full(new_v))
```

```python
import functools

import numpy as np
import jax
import jax.numpy as jnp
from jax import lax
from jax.experimental import pallas as pl
from jax.experimental.pallas import tpu as pltpu

F32, BF16 = jnp.float32, jnp.bfloat16

D_MODEL = 2048
EPS = 1e-6
HG_HEADS, HG_DK, HG_CHUNK = 16, 128, 64
HG_BLK = 4 * HG_DK
SSM_DINNER, SSM_HEADDIM, SSM_HEADS, SSM_GROUPS, SSM_DSTATE, SSM_CONV = 4096, 64, 64, 8, 128, 4
SSM_CHUNK = 128
SSM_GW = SSM_DINNER // SSM_GROUPS
SSM_HPG = SSM_HEADS // SSM_GROUPS
SSM_XBC = SSM_GW + 2 * SSM_DSTATE
SSM_BLK = SSM_XBC + 128 + SSM_GW
SSM_CONV_DIM = SSM_DINNER + 2 * SSM_GROUPS * SSM_DSTATE
D_FF, FFN_CONV = 5632, 3
FFN_GW = 512
FFN_G = D_FF // FFN_GW
IN_TOTAL = 22592
N_DEV = 8
HALO = 8
VMEM_LIMIT = 52 * 1024 * 1024
ADAM_LR, ADAM_B1, ADAM_B2, ADAM_EPS, ADAM_WD, ADAM_STEP = 0.001, 0.9, 0.999, 1e-08, 0.01, 10

_DIMS = {"nn": ((1,), (0,)), "nt": ((1,), (1,)), "tn": ((0,), (0,))}


def _mm_raw(a, b, mode):
    return lax.dot_general(a.astype(BF16), b.astype(BF16), (_DIMS[mode], ((), ())), preferred_element_type=F32)


@functools.partial(jax.custom_vjp, nondiff_argnums=(2,))
def _mm(a, b, mode):
    return _mm_raw(a, b, mode)


def _mm_fwd(a, b, mode):
    return _mm_raw(a, b, mode), (a, b)


def _mm_bwd(mode, res, dc):
    a, b = res
    if mode == "nn":
        return _mm_raw(dc, b, "nt"), _mm_raw(a, dc, "tn")
    if mode == "nt":
        return _mm_raw(dc, b, "nn"), _mm_raw(dc, a, "tn")
    return _mm_raw(b, dc, "nt"), _mm_raw(a, dc, "nn")


_mm.defvjp(_mm_fwd, _mm_bwd)


def _cmm_raw(m, x, mode):
    hi = x.astype(BF16)
    r1 = x - hi.astype(F32)
    mid = r1.astype(BF16)
    lo = (r1 - mid.astype(F32)).astype(BF16)
    dn = (_DIMS[mode], ((), ()))
    dot = lambda p: lax.dot_general(m, p, dn, preferred_element_type=F32)
    return dot(hi) + dot(mid) + dot(lo)


@jax.custom_vjp
def _cmm(m, x):
    return _cmm_raw(m, x, "nn")


def _cmm_fwd(m, x):
    return _cmm_raw(m, x, "nn"), m


def _cmm_bwd(m, dy):
    return jnp.zeros_like(m), _cmm_raw(m, dy, "tn")


_cmm.defvjp(_cmm_fwd, _cmm_bwd)


@functools.partial(jax.custom_vjp, nondiff_argnums=(1,))
def _sroll(x, s):
    return pltpu.roll(x, s, 0) if s else x


def _sroll_fwd(x, s):
    return _sroll(x, s), None


def _sroll_bwd(s, _, ct):
    return ((pltpu.roll(ct, ct.shape[0] - s, 0) if s else ct),)


_sroll.defvjp(_sroll_fwd, _sroll_bwd)


def _rms(x, w):
    return x * lax.rsqrt(jnp.mean(x * x, axis=-1, keepdims=True) + EPS) * w


def _softplus(x):
    return jnp.maximum(x, 0.0) + jnp.log(1.0 + jnp.exp(-jnp.abs(x)))


def _causal_conv(halo, x, w, b):
    k_taps = w.shape[0]
    xe = jnp.concatenate([halo, x], axis=0)
    out = b
    for k in range(k_taps):
        out = out + w[k:k + 1, :] * _sroll(xe, k_taps - 1 - k)[HALO:, :]
    return out


def _hg_consts():
    c = HG_CHUNK
    t = np.arange(c)
    blocks, rowmask, pair = [], [], []
    lates, earlies = [], []
    for m in (32, 16, 8, 4, 2, 1):
        pos = t % (2 * m)
        late = pos >= m
        mid = t - pos + m
        j = t[None, :]
        mq = late[:, None] & (j >= mid[:, None]) & (j <= t[:, None])
        mk = (~late)[:, None] & (j > t[:, None]) & (j <= mid[:, None] - 1)
        lates.append((mq, late))
        earlies.append((mk, ~late))
        parent = t // (2 * m)
        pair.append((parent[:, None] == parent[None, :]) & late[:, None] & (~late)[None, :])
    for mat, msk in lates + earlies:
        blocks.append(mat)
        rowmask.append(np.repeat(msk[:, None], HG_DK, axis=1))
    blocks.append(t[None, :] <= t[:, None])
    mall = jnp.asarray(np.concatenate(blocks, 0).astype(np.float32), BF16)
    rowmask = jnp.asarray(np.concatenate(rowmask, 0).astype(np.float32))
    pair = jnp.asarray(np.stack(pair, 0).astype(np.float32))
    eye = jnp.asarray(np.eye(c, dtype=np.float32))
    return [mall, rowmask, pair, eye]


def _hg_step(carry, xs, params, consts):
    (st,) = carry
    blk = xs[0].astype(F32)
    tab, nw = params
    mall, rowmask, pair, eye = consts
    c, dk = HG_CHUNK, HG_DK
    q_raw, f_raw, v, og = blk[:, :dk], blk[:, dk:2 * dk], blk[:, 2 * dk:3 * dk], blk[:, 3 * dk:]
    lb = jax.nn.sigmoid(tab[0:1, :] - tab[1:2, :])
    f = lb + (1.0 - lb) * jax.nn.sigmoid(f_raw)
    g = jnp.log(f)
    kk = 1.0 - f
    qh = jax.nn.silu(q_raw) * (HG_DK ** -0.5)
    yield
    sums = _cmm(mall, g)
    yield
    b = sums[12 * c:, :]
    fac = jnp.exp(sums[:12 * c, :]) * rowmask
    scores = eye * jnp.sum(qh * kk, axis=1, keepdims=True)
    b_last = jnp.sum(g, axis=0, keepdims=True)
    yield
    inter = _mm(qh * jnp.exp(b), st, "nt")
    st_new = st * jnp.exp(b_last) + _mm(v, kk * jnp.exp(b_last - b), "tn")
    yield
    for l in range(6):
        qs = qh * fac[l * c:(l + 1) * c, :]
        ks = kk * fac[(6 + l) * c:(7 + l) * c, :]
        scores = scores + pair[l] * _mm(qs, ks, "nt")
        if l % 2:
            yield
    o = _mm(scores, v, "nn") + inter
    yield
    y = _rms(o, nw) * jax.nn.silu(og)
    return [st_new], [y]


def _ssd_consts():
    t = np.arange(SSM_CHUNK)
    tril = (t[None, :] <= t[:, None]).astype(np.float32)
    return [jnp.asarray(tril, BF16), jnp.asarray(tril)]


def _ssd_step(carry, xs, params, consts):
    st, halo = carry
    blk = xs[0].astype(F32)
    conv_w, conv_b, dtb, alog, dskip, nw = params
    tril_b, tril = consts
    c = SSM_CHUNK
    raw, dtr, z = blk[:, :SSM_XBC], blk[:, SSM_XBC:SSM_XBC + 128], blk[:, SSM_XBC + 128:]
    act = jax.nn.silu(_causal_conv(halo, raw, conv_w, conv_b))
    xh, bm, cm = act[:, :SSM_GW], act[:, SSM_GW:SSM_GW + SSM_DSTATE], act[:, SSM_GW + SSM_DSTATE:]
    dt = _softplus(dtr + dtb)
    da = dt * (-jnp.exp(alog))
    acum = _cmm(tril_b, da)
    acum_t = acum.T
    a_last = jnp.sum(da, axis=0, keepdims=True)
    cb_causal = _mm(cm, bm, "nt") * tril
    lane = lax.broadcasted_iota(jnp.int32, (c, 128), 1)
    row = lax.broadcasted_iota(jnp.int32, (128, 128), 0)
    first = lane < SSM_HEADDIM
    ys, st_new = [], []
    for j in range(SSM_HPG // 2):
        xp = xh[:, 128 * j:128 * (j + 1)]
        sp = st[128 * j:128 * (j + 1), :]
        r0, r1 = 2 * j, 2 * j + 1
        col = lambda a, r: jnp.broadcast_to(a[:, r:r + 1], (c, 128))
        xdt = xp * jnp.where(first, col(dt, r0), col(dt, r1))
        yj = _mm(cm, sp, "nt") * jnp.exp(jnp.where(first, col(acum, r0), col(acum, r1)))
        for r, keep in ((r0, first), (r1, ~first)):
            dec = jnp.broadcast_to(acum[:, r:r + 1], (c, c)) - jnp.broadcast_to(acum_t[r:r + 1, :], (c, c))
            m = cb_causal * jnp.exp(jnp.minimum(dec, 0.0))
            yj = yj + _mm(m, jnp.where(keep, xdt, 0.0), "nn")
        al0, al1 = a_last[:, r0:r0 + 1], a_last[:, r1:r1 + 1]
        wts = jnp.exp(jnp.where(first, al0 - col(acum, r0), al1 - col(acum, r1)))
        st_new.append(jnp.where(row < SSM_HEADDIM, jnp.exp(al0), jnp.exp(al1)) * sp + _mm(xdt * wts, bm, "tn"))
        ys.append(yj)
    y = jnp.concatenate(ys, axis=1) + dskip * xh
    y = _rms(y * jax.nn.silu(z), nw)
    return [jnp.concatenate(st_new, axis=0), raw[c - HALO:, :]], [y]


def _ffn_step(carry, xs, params, consts):
    (halo,) = carry
    blk = xs[0].astype(F32)
    conv_w, conv_b = params
    gate, up = blk[:, :FFN_GW], blk[:, FFN_GW:]
    a = jax.nn.gelu(_causal_conv(halo, gate, conv_w, conv_b), approximate=True) * up
    return [gate[gate.shape[0] - HALO:, :]], [a]


def _pre_step(carry, xs, params, consts):
    return [], [_rms(xs[0], params[0])]


def _mix_step(carry, xs, params, consts):
    gates, uh, us = (a.astype(F32) for a in xs)
    return [], [jax.nn.sigmoid(gates[:, :D_MODEL]) * uh + jax.nn.sigmoid(gates[:, D_MODEL:]) * us]


def _post_step(carry, xs, params, consts):
    x, v = xs
    x1 = x + _rms(v, params[0])
    return [], [x1, _rms(x1, params[1])]


def _scan_call(step, *, name, rows, chunk, nc, groups, xs, cins=(), params=(), consts=(), carries=(), ys=(), couts=(),
               accs=(), reverse=False, gpb=1, multi=False):
    blk_rows = chunk * nc
    nb = rows // blk_rows
    n_chunks = rows // chunk
    assert nb * blk_rows == rows and groups % gpb == 0
    rb = (lambda i: nb - 1 - i) if reverse else (lambda i: i)
    n_x, n_ci, n_p, n_c = len(xs), len(cins), len(params), len(consts)
    n_y, n_co, n_a = len(ys), len(couts), len(accs)

    def chunk_spec(shape):
        zeros = (0,) * len(shape)
        return pl.BlockSpec((gpb, nc) + tuple(shape), lambda g, i: (g, rb(i)) + zeros)

    in_specs = [pl.BlockSpec((blk_rows, gpb * w), lambda g, i: (rb(i), g)) for _, w in xs]
    in_specs += [chunk_spec(a.shape[2:]) for a in cins]
    in_specs += [pl.BlockSpec((gpb,) + tuple(a.shape[1:]), lambda g, i: (g, 0, 0)) for a in params]
    in_specs += [pl.BlockSpec(a.shape, (lambda nd: lambda g, i: (0,) * nd)(a.ndim)) for a in consts]
    out_specs = [pl.BlockSpec((blk_rows, gpb * w), lambda g, i: (rb(i), g)) for w, _ in ys]
    out_specs += [chunk_spec(s) for s in couts]
    out_specs += [pl.BlockSpec((gpb, r, c), lambda g, i: (g, 0, 0)) for r, c in accs]
    out_shape = [jax.ShapeDtypeStruct((rows, groups * w), dt) for w, dt in ys]
    out_shape += [jax.ShapeDtypeStruct((groups, n_chunks) + tuple(s), F32) for s in couts]
    out_shape += [jax.ShapeDtypeStruct((groups, r, c), F32) for r, c in accs]
    x_widths = [w for _, w in xs]
    y_widths = [w for w, _ in ys]

    def body(*refs):
        x_refs = refs[:n_x]
        ci_refs = refs[n_x:n_x + n_ci]
        p_refs = refs[n_x + n_ci:n_x + n_ci + n_p]
        c_refs = refs[n_x + n_ci + n_p:n_x + n_ci + n_p + n_c]
        o = n_x + n_ci + n_p + n_c
        y_refs = refs[o:o + n_y]
        co_refs = refs[o + n_y:o + n_y + n_co]
        a_refs = refs[o + n_y + n_co:o + n_y + n_co + n_a]
        carry_refs = refs[o + n_y + n_co + n_a:]

        @pl.when(pl.program_id(1) == 0)
        def _():
            for s in carry_refs:
                s[...] = jnp.zeros(s.shape, F32)
            for a in a_refs:
                a[...] = jnp.zeros(a.shape, F32)

        cvals = [c[...] for c in c_refs]

        def one_chunk(i, _):
            c = (nc - 1 - i) if reverse else i
            r0 = c * chunk if isinstance(c, int) else pl.multiple_of(c * chunk, chunk)
            loaded = []
            for u in range(gpb):
                carry = [s[u] for s in carry_refs]
                xv = [x[pl.ds(r0, chunk), u * w:(u + 1) * w] for x, w in zip(x_refs, x_widths)]
                civ = [ci[u, c] for ci in ci_refs]
                loaded.append((carry, xv, civ, [p[u] for p in p_refs]))
            results = step(loaded, cvals) if multi else [step(*args, cvals) for args in loaded]
            for u, (new_carry, yv, cov, av) in enumerate(results):
                for s, val in zip(carry_refs, new_carry):
                    s[u] = val
                for y, w, val in zip(y_refs, y_widths, yv):
                    y[pl.ds(r0, chunk), u * w:(u + 1) * w] = val.astype(y.dtype)
                for co, val in zip(co_refs, cov):
                    co[u, c] = val
                for a, val in zip(a_refs, av):
                    a[u] += val
            return 0

        if nc == 1:
            one_chunk(0, 0)
        else:
            lax.fori_loop(0, nc, one_chunk, 0)

    outs = pl.pallas_call(
        body, name=name, grid=(groups // gpb, nb), in_specs=in_specs, out_specs=out_specs, out_shape=out_shape,
        scratch_shapes=[pltpu.VMEM((gpb,) + tuple(s), F32) for s in carries],
        compiler_params=pltpu.CompilerParams(dimension_semantics=("arbitrary", "arbitrary"),
                                             vmem_limit_bytes=VMEM_LIMIT),
    )(*[a for a, _ in xs], *cins, *params, *consts)
    return outs[:n_y], outs[n_y:n_y + n_co], outs[n_y + n_co:]


def _run_interleaved(step, arg_tuples):
    runs = [step(*args) for args in arg_tuples]
    if not hasattr(runs[0], "send"):
        return runs
    results, live = [None] * len(runs), list(range(len(runs)))
    while live:
        for u in list(live):
            try:
                next(runs[u])
            except StopIteration as done:
                results[u] = done.value
                live.remove(u)
    return results


def _stage_fwd(step, *, name, rows, chunk, nc, groups, xs, params, consts, carries, ys, gpb=1):
    def fstep(loaded, cv):
        outs = _run_interleaved(step, [(carry, xv, pv, cv) for carry, xv, _, pv in loaded])
        return [(new_carry, yv, carry, []) for (new_carry, yv), (carry, _, _, _) in zip(outs, loaded)]

    yv, saved, _ = _scan_call(fstep, name=name, rows=rows, chunk=chunk, nc=nc, groups=groups, xs=xs, params=params,
                              consts=consts, carries=carries, ys=ys, couts=carries, gpb=gpb, multi=True)
    return yv, saved


def _stage_bwd(step, *, name, rows, chunk, nc, groups, xs, saved, params, consts, carries, dys, dxs, gpb=1):
    n_x = len(xs)

    def bstep(loaded, cv):
        civs = [list(civ) for _, _, civ, _ in loaded]
        xvs = [list(xv_all[:n_x]) for _, xv_all, _, _ in loaded]
        pvs = [list(pv) for _, _, _, pv in loaded]
        cts = [(list(dcarry), [d.astype(F32) for d in xv_all[n_x:]]) for dcarry, xv_all, _, _ in loaded]

        def fwd(civs_, xvs_, pvs_):
            outs = _run_interleaved(step, [(c_, x_, p_, cv) for c_, x_, p_ in zip(civs_, xvs_, pvs_)])
            return [(list(new_carry), list(yv)) for new_carry, yv in outs]

        _, vjp = jax.vjp(fwd, civs, xvs, pvs)
        dcivs, dxvs, dpvs = vjp(cts)
        return [(dc, dx, [], dp) for dc, dx, dp in zip(dcivs, dxvs, dpvs)]

    dxv, _, dpv = _scan_call(bstep, name=name, rows=rows, chunk=chunk, nc=nc, groups=groups, xs=list(xs) + list(dys),
                             cins=saved, params=params, consts=consts, carries=carries,
                             ys=[(w, dt) for (_, w), dt in zip(xs, dxs)], accs=[a.shape[1:] for a in params],
                             reverse=True, gpb=gpb, multi=True)
    return dxv, dpv


def _mm_params(sem):
    return pltpu.CompilerParams(dimension_semantics=sem, vmem_limit_bytes=VMEM_LIMIT)


def _after(dep):
    return ([], []) if dep is None else ([dep], [pl.BlockSpec(memory_space=pl.ANY)])


def _matmul_nt(a, b, *, name, tm=1024, tn=512, dep=None):
    m, k = a.shape
    n = b.shape[0]
    tm = min(tm, m)
    deps, dep_specs = _after(dep)

    def body(a_ref, b_ref, *rest):
        rest[-1][...] = lax.dot_general(a_ref[...], b_ref[...], (_DIMS["nt"], ((), ())), preferred_element_type=F32)

    return pl.pallas_call(
        body, name=name, grid=(m // tm, n // tn),
        in_specs=[pl.BlockSpec((tm, k), lambda i, j: (i, 0)), pl.BlockSpec((tn, k), lambda i, j: (j, 0))] + dep_specs,
        out_specs=pl.BlockSpec((tm, tn), lambda i, j: (i, j)),
        out_shape=jax.ShapeDtypeStruct((m, n), F32),
        compiler_params=_mm_params(("parallel", "arbitrary")),
    )(a, b, *deps)


def _matmul_nn(a, b, *, name, dep=None):
    m, k = a.shape
    n = b.shape[1]
    tm, tn = (1024, 512) if k <= 4096 else (1024, 256) if k <= 6144 else (512, 512) if k <= 8192 else (512, 256)
    tm = min(tm, m)
    deps, dep_specs = _after(dep)

    def body(a_ref, b_ref, *rest):
        rest[-1][...] = jnp.dot(a_ref[...], b_ref[...], preferred_element_type=F32)

    return pl.pallas_call(
        body, name=name, grid=(m // tm, n // tn),
        in_specs=[pl.BlockSpec((tm, k), lambda i, j: (i, 0)), pl.BlockSpec((k, tn), lambda i, j: (0, j))] + dep_specs,
        out_specs=pl.BlockSpec((tm, tn), lambda i, j: (i, j)),
        out_shape=jax.ShapeDtypeStruct((m, n), F32),
        compiler_params=_mm_params(("parallel", "arbitrary")),
    )(a, b, *deps)


def _matmul_tn(x, y, *, name, tp=512, tq=512):
    t, p = x.shape
    q = y.shape[1]

    def body(x_ref, y_ref, o_ref):
        o_ref[...] = lax.dot_general(x_ref[...], y_ref[...], (_DIMS["tn"], ((), ())),
                                     preferred_element_type=F32).astype(o_ref.dtype)

    return pl.pallas_call(
        body, name=name, grid=(p // tp, q // tq),
        in_specs=[pl.BlockSpec((t, tp), lambda i, j: (0, i)), pl.BlockSpec((t, tq), lambda i, j: (0, j))],
        out_specs=pl.BlockSpec((tp, tq), lambda i, j: (i, j)),
        out_shape=jax.ShapeDtypeStruct((p, q), BF16),
        compiler_params=_mm_params(("parallel", "arbitrary")),
    )(x, y)


N_CHIPS = N_DEV // 2


def _all_gather(arrays, *, name):
    n = len(arrays)
    out_shape = [jax.ShapeDtypeStruct((N_DEV,) + tuple(a.shape), a.dtype) for a in arrays]

    def body(*refs):
        in_refs, out_refs = refs[:n], refs[n:2 * n]
        send_sems, recv_sems, local_sems = refs[2 * n:]
        x, y, c = lax.axis_index("x"), lax.axis_index("y"), lax.axis_index("c")
        me, sibling = (x, y, c), (x, y, 1 - c)
        chips = [(1 - x, y), (x, 1 - y), (1 - x, 1 - y)]

        def copy(a, k, block, to, src=None):
            slot = out_refs[a].at[4 * block[0] + 2 * block[1] + block[2]]
            return pltpu.make_async_remote_copy(
                src_ref=slot if src is None else src, dst_ref=slot, send_sem=send_sems.at[a, k],
                recv_sem=recv_sems.at[a, k], device_id=to, device_id_type=pl.DeviceIdType.MESH)

        mine = [pltpu.make_async_copy(in_refs[a], out_refs[a].at[4 * x + 2 * y + c], local_sems.at[a]) for a in range(n)]
        first = []
        for a in range(n):
            first.append(copy(a, 0, me, sibling, src=in_refs[a]))
            first += [copy(a, 1 + j, me, (*chip, c), src=in_refs[a]) for j, chip in enumerate(chips)]
        for cp in mine + first:
            cp.start()
        passed = []
        for j, chip in enumerate(chips):
            for a in range(n):
                copy(a, 1 + j, (*chip, c), me).wait_recv()
                passed.append(copy(a, 4 + j, (*chip, c), sibling))
                passed[-1].start()
        for a in range(n):
            copy(a, 0, sibling, me).wait_recv()
            for j, chip in enumerate(chips):
                copy(a, 4 + j, (*chip, 1 - c), me).wait_recv()
        for cp in first + passed:
            cp.wait_send()
        for cp in mine:
            cp.wait()

    any_spec = pl.BlockSpec(memory_space=pl.ANY)
    return pl.pallas_call(
        body, name=name, in_specs=[any_spec] * n, out_specs=[any_spec] * n, out_shape=out_shape,
        scratch_shapes=[pltpu.SemaphoreType.DMA((n, N_DEV - 1)), pltpu.SemaphoreType.DMA((n, N_DEV - 1)),
                        pltpu.SemaphoreType.DMA((n,))],
        compiler_params=pltpu.CompilerParams(has_side_effects=True),
    )(*arrays)


def _push(arrays, *, name, plan, out_slots=None):
    n = len(arrays)
    in_place = out_slots is None
    out_shape = [jax.ShapeDtypeStruct(((a.shape[0] if in_place else out_slots),) + tuple(a.shape[1:]), a.dtype) for a in arrays]
    n_tr = len(plan(0, 0, 0)[0])

    def body(*refs):
        in_refs, out_refs = refs[:n], refs[n:2 * n]
        send_sems, recv_sems, local_sems = refs[2 * n:]
        src_refs = out_refs if in_place else in_refs
        transfers, local = plan(lax.axis_index("x"), lax.axis_index("y"), lax.axis_index("c"))
        copies = []
        for a in range(n):
            if local is not None:
                copies.append(pltpu.make_async_copy(src_refs[a].at[local[0]], out_refs[a].at[local[1]], local_sems.at[a]))
            for k, (peer, src, dst) in enumerate(transfers):
                copies.append(pltpu.make_async_remote_copy(
                    src_ref=src_refs[a].at[src], dst_ref=out_refs[a].at[dst], send_sem=send_sems.at[a, k],
                    recv_sem=recv_sems.at[a, k], device_id=peer, device_id_type=pl.DeviceIdType.MESH))
        for cp in copies:
            cp.start()
        for cp in copies:
            cp.wait()

    any_spec = pl.BlockSpec(memory_space=pl.ANY)
    return pl.pallas_call(
        body, name=name, in_specs=[any_spec] * n, out_specs=[any_spec] * n, out_shape=out_shape,
        input_output_aliases={a: a for a in range(n)} if in_place else {},
        scratch_shapes=[pltpu.SemaphoreType.DMA((n, n_tr)), pltpu.SemaphoreType.DMA((n, n_tr)),
                        pltpu.SemaphoreType.DMA((n,))],
        compiler_params=pltpu.CompilerParams(has_side_effects=True),
    )(*arrays)


_HBM_SPEC = pl.BlockSpec(memory_space=pltpu.HBM)
_SEM_SPEC = pl.BlockSpec(memory_space=pltpu.SEMAPHORE)
_DATAFLOW = pltpu.SideEffectType.DATAFLOW_SIDE_EFFECTING


def _push_start(sources, landing, *, name, plan):
    n = len(sources)
    n_tr = len(plan(0, 0, 0)[0])

    def body(*refs):
        src_refs, land_refs = refs[:n], refs[n:2 * n]
        send_sems, recv_sems, token = refs[2 * n], refs[2 * n + 1], refs[-1]
        transfers, _ = plan(lax.axis_index("x"), lax.axis_index("y"), lax.axis_index("c"))
        for a in range(n):
            for k, (peer, src, dst) in enumerate(transfers):
                pltpu.make_async_remote_copy(
                    src_ref=src_refs[a].at[src], dst_ref=land_refs[a].at[dst], send_sem=send_sems.at[a * n_tr + k],
                    recv_sem=recv_sems.at[a * n_tr + k], device_id=peer, device_id_type=pl.DeviceIdType.MESH).start()
        token[...] = jnp.zeros(token.shape, token.dtype)

    hbm = lambda a: pltpu.HBM(a.shape, a.dtype)
    outs = pl.pallas_call(
        body, name=name,
        out_shape=(pltpu.SemaphoreType.DMA((n * n_tr,)), pltpu.SemaphoreType.DMA((n * n_tr,)), *[hbm(a) for a in sources],
                   *[hbm(a) for a in landing], jax.ShapeDtypeStruct((8, 128), F32)),
        in_specs=[_HBM_SPEC] * (2 * n),
        out_specs=(_SEM_SPEC, _SEM_SPEC, *[_HBM_SPEC] * (2 * n), pl.BlockSpec(memory_space=pltpu.VMEM)),
        input_output_aliases={i: 2 + i for i in range(2 * n)},
        compiler_params=pltpu.CompilerParams(has_side_effects=_DATAFLOW),
    )(*[pltpu.with_memory_space_constraint(a, pltpu.HBM) for a in list(sources) + list(landing)])
    return outs[0], outs[1], list(outs[2:2 + n]), list(outs[2 + n:2 + 2 * n]), outs[-1]


def _push_wait(handles, after, *, name, plan):
    send_sems, recv_sems, sources, landing, _ = handles
    n = len(sources)

    def body(*refs):
        src_refs, land_refs = refs[:n], refs[n:2 * n]
        send_sems_, recv_sems_ = refs[2 * n], refs[2 * n + 1]
        transfers, _ = plan(lax.axis_index("x"), lax.axis_index("y"), lax.axis_index("c"))
        n_tr = len(transfers)
        for a in range(n):
            for k, (peer, src, dst) in enumerate(transfers):
                cp = pltpu.make_async_remote_copy(
                    src_ref=src_refs[a].at[src], dst_ref=land_refs[a].at[dst], send_sem=send_sems_.at[a * n_tr + k],
                    recv_sem=recv_sems_.at[a * n_tr + k], device_id=peer, device_id_type=pl.DeviceIdType.MESH)
                cp.wait_send()
                cp.wait_recv()

    hbm = lambda a: pltpu.HBM(a.shape, a.dtype)
    outs = pl.pallas_call(
        body, name=name, out_shape=tuple(hbm(a) for a in list(sources) + list(landing)),
        in_specs=[_HBM_SPEC] * (2 * n) + [_SEM_SPEC, _SEM_SPEC, pl.BlockSpec(memory_space=pl.ANY)],
        out_specs=[_HBM_SPEC] * (2 * n), input_output_aliases={i: i for i in range(2 * n)},
        compiler_params=pltpu.CompilerParams(has_side_effects=_DATAFLOW),
    )(*sources, *landing, send_sems, recv_sems, after)
    return list(outs[n:])


def _plan_everyone(x, y, c):
    me = 4 * x + 2 * y + c
    peers = [(1 - x if k & 4 else x, 1 - y if k & 2 else y, 1 - c if k & 1 else c) for k in range(1, N_DEV)]
    return [(p, 0, me) for p in peers], (0, me)


def _plan_sibling(x, y, c):
    return [((x, y, 1 - c), 2 * chip + (1 - c), chip) for chip in range(N_CHIPS)], None


def _plan_chips(x, y, c):
    mine = 2 * x + y
    peers = [(1 - x, y), (x, 1 - y), (1 - x, 1 - y)]
    return [((px, py, c), 2 * px + py, mine) for px, py in peers], (mine, mine)


def _plan_own_block(x, y, c):
    me = 4 * x + 2 * y + c
    peers = [(x, y, 1 - c), (1 - x, y, c), (x, 1 - y, c), (1 - x, 1 - y, c)]
    return [(p, 0, me) for p in peers], None


def _plan_pass_on(x, y, c):
    slots = [4 * px + 2 * py + c for px, py in ((1 - x, y), (x, 1 - y), (1 - x, 1 - y))]
    return [((x, y, 1 - c), s, s) for s in slots], None


def _pair_sum(parts, received, *, name, tc):
    _, r, c = parts.shape
    core = lax.axis_index("c").astype(jnp.int32).reshape(1)

    def body(core_ref, p_ref, r_ref, o_ref, o2_ref):
        s = (p_ref[...].astype(F32) + r_ref[...].astype(F32)).astype(o_ref.dtype)
        o_ref[...] = s
        o2_ref[...] = s

    out = pl.BlockSpec((None, r, tc), lambda i, j, core_ref: (i, 0, j))
    return pl.pallas_call(
        body, name=name,
        grid_spec=pltpu.PrefetchScalarGridSpec(
            num_scalar_prefetch=1, grid=(N_CHIPS, c // tc),
            in_specs=[pl.BlockSpec((None, r, tc), lambda i, j, core_ref: (2 * i + core_ref[0], 0, j)),
                      pl.BlockSpec((None, r, tc), lambda i, j, core_ref: (i, 0, j))],
            out_specs=[out, out]),
        out_shape=[jax.ShapeDtypeStruct((N_CHIPS, r, c), BF16)] * 2,
        compiler_params=pltpu.CompilerParams(dimension_semantics=("parallel", "parallel"), vmem_limit_bytes=VMEM_LIMIT),
    )(core, parts, received)


def _sum_blocks(a, *, name, tc):
    nblk, r, c = a.shape

    def body(a_ref, o_ref):
        acc = a_ref[0].astype(F32)
        for i in range(1, nblk):
            acc = acc + a_ref[i].astype(F32)
        o_ref[...] = acc

    return pl.pallas_call(
        body, name=name, grid=(c // tc,),
        in_specs=[pl.BlockSpec((nblk, r, tc), lambda j: (0, 0, j))],
        out_specs=pl.BlockSpec((r, tc), lambda j: (0, j)),
        out_shape=jax.ShapeDtypeStruct((r, c), F32),
        compiler_params=pltpu.CompilerParams(dimension_semantics=("parallel",), vmem_limit_bytes=VMEM_LIMIT),
    )(a)


def _adamw(w, g, m, v, *, name, tr):
    r, c = w.shape

    def body(w_ref, g_ref, m_ref, v_ref, d_ref, mo_ref, vo_ref):
        gv = g_ref[...]
        mn = ADAM_B1 * m_ref[...] + (1.0 - ADAM_B1) * gv
        vn = ADAM_B2 * v_ref[...] + (1.0 - ADAM_B2) * jnp.square(gv)
        m_hat = mn / (1.0 - ADAM_B1 ** ADAM_STEP)
        v_hat = vn / (1.0 - ADAM_B2 ** ADAM_STEP)
        d_ref[...] = -ADAM_LR * (m_hat / (jnp.sqrt(v_hat) + ADAM_EPS) + ADAM_WD * w_ref[...])
        mo_ref[...] = mn
        vo_ref[...] = vn

    spec = pl.BlockSpec((tr, c), lambda i: (i, 0))
    return pl.pallas_call(
        body, name=name, grid=(r // tr,), in_specs=[spec] * 4, out_specs=[spec] * 3,
        out_shape=[jax.ShapeDtypeStruct((r, c), F32)] * 3,
        compiler_params=pltpu.CompilerParams(dimension_semantics=("parallel",), vmem_limit_bytes=VMEM_LIMIT),
    )(w, g, m, v)


def _in_proj_layout():
    z0, xbc0, dt0, gate0 = 8192, 12288, 18432, 18496
    hg = []
    for h in range(HG_HEADS):
        for part in range(4):
            hg.append(part * 2048 + h * HG_DK + np.arange(HG_DK))
    ssm = []
    for g in range(SSM_GROUPS):
        ssm.append(xbc0 + g * SSM_GW + np.arange(SSM_GW))
        ssm.append(xbc0 + SSM_DINNER + g * SSM_DSTATE + np.arange(SSM_DSTATE))
        ssm.append(xbc0 + SSM_DINNER + SSM_GROUPS * SSM_DSTATE + g * SSM_DSTATE + np.arange(SSM_DSTATE))
        ssm.append(np.concatenate([dt0 + g * SSM_HPG + np.arange(SSM_HPG), -np.ones(128 - SSM_HPG, np.int64)]))
        ssm.append(z0 + g * SSM_GW + np.arange(SSM_GW))
    gate = gate0 + np.arange(2 * D_MODEL)
    return np.concatenate(hg), np.concatenate(ssm), gate


def _conv_layout():
    idx = []
    for g in range(SSM_GROUPS):
        idx.append(np.concatenate([g * SSM_GW + np.arange(SSM_GW),
                                   SSM_DINNER + g * SSM_DSTATE + np.arange(SSM_DSTATE),
                                   SSM_DINNER + SSM_GROUPS * SSM_DSTATE + g * SSM_DSTATE + np.arange(SSM_DSTATE)]))
    return np.stack(idx)


def _up_layout():
    idx = []
    for g in range(FFN_G):
        idx.append(g * FFN_GW + np.arange(FFN_GW))
        idx.append(D_FF + g * FFN_GW + np.arange(FFN_GW))
    return np.concatenate(idx)


def _inverse(idx, n):
    inv = np.zeros(n, np.int64)
    pos = np.nonzero(idx >= 0)[0]
    inv[idx[pos]] = pos
    return inv


def _take_rows(a, idx, axis=0):
    idx = np.asarray(idx).reshape(-1)
    pieces, start = [], 0
    for i in range(1, len(idx) + 1):
        same_run = i < len(idx) and ((idx[i] == idx[i - 1] + 1 and idx[i - 1] >= 0) or (idx[i] < 0 and idx[i - 1] < 0))
        if same_run:
            continue
        n = i - start
        if idx[start] < 0:
            shape = list(a.shape)
            shape[axis] = n
            pieces.append(jnp.zeros(shape, a.dtype))
        else:
            pieces.append(lax.slice_in_dim(a, int(idx[start]), int(idx[start]) + n, axis=axis))
        start = i
    return pieces[0] if len(pieces) == 1 else jnp.concatenate(pieces, axis=axis)


_SMALL = (("mix_pre_norm", (1, 2048)), ("mix_post_norm", (1, 2048)), ("hg_lb_table", (2, 2048)), ("hg_out_norm", (1, 128)),
          ("ssm_conv_w", (4, 6144)), ("ssm_conv_b", (1, 6144)), ("ssm_dt_bias", (1, 64)), ("ssm_A_log", (1, 64)),
          ("ssm_D", (1, 64)), ("ssm_out_norm", (1, 4096)), ("ffn_pre_norm", (1, 2048)), ("ffn_post_norm", (1, 2048)),
          ("ffn_conv_w", (3, 5632)), ("ffn_conv_b", (1, 5632)), ("loss", (1, 1)))
_PACK_ROWS = 8 * (-(-sum(int(np.prod(s)) for _, s in _SMALL) // 1024))


def _pack(vals):
    flat = jnp.concatenate([vals[k].astype(F32).reshape(-1) for k, _ in _SMALL])
    return jnp.pad(flat, (0, _PACK_ROWS * 128 - flat.shape[0])).reshape(_PACK_ROWS, 128)


def _unpack(packed):
    flat, out, o = packed.reshape(-1), {}, 0
    for k, s in _SMALL:
        n = int(np.prod(s))
        out[k] = flat[o:o + n].reshape(s)
        o += n
    return out


def _local_step(x, target, w, p, late_weights=None, emit=lambda key, gw: None):
    t = x.shape[0]
    one = lambda a: a.reshape((1,) + a.shape)
    row = dict(rows=t, groups=1, consts=[], carries=[])

    (h1,), _ = _stage_fwd(_pre_step, name="pre_fwd", chunk=512, nc=1, xs=[(x, D_MODEL)], params=[one(p["mix_pre_norm"])],
                          ys=[(D_MODEL, BF16)], **row)
    proj_hg = _matmul_nt(h1, w["in_hg"], name="proj_hg")
    proj_ssm = _matmul_nt(h1, w["in_ssm"], name="proj_ssm")
    proj_gate = _matmul_nt(h1, w["in_gate"], name="proj_gate")

    hg = dict(rows=t, chunk=HG_CHUNK, nc=8, groups=HG_HEADS, xs=[(proj_hg, HG_BLK)], params=[p["hg_tab"], p["hg_nw"]],
              consts=_hg_consts(), carries=[(HG_DK, HG_DK)], gpb=4)
    (y_hg,), hg_saved = _stage_fwd(_hg_step, name="hg_fwd", ys=[(HG_DK, BF16)], **hg)

    ssd = dict(rows=t, chunk=SSM_CHUNK, nc=4, groups=SSM_GROUPS, xs=[(proj_ssm, SSM_BLK)],
               params=[p["conv_w"], p["conv_b"], p["dt_bias"], p["a_log"], p["d_skip"], p["ssm_nw"]],
               consts=_ssd_consts(), carries=[(4 * 128, SSM_DSTATE), (HALO, SSM_XBC)])
    (y_ssm,), ssd_saved = _stage_fwd(_ssd_step, name="ssd_fwd", ys=[(SSM_GW, BF16)], **ssd)

    if late_weights is not None:
        w = {**w, **late_weights(y_ssm)}
    u_hg = _matmul_nn(y_hg, w["branch_hg"], name="branch_hg")
    u_ssm = _matmul_nn(y_ssm, w["branch_ssm"], name="branch_ssm")
    mix = dict(chunk=256, nc=1, xs=[(proj_gate, 2 * D_MODEL), (u_hg, D_MODEL), (u_ssm, D_MODEL)], params=[], **row)
    (mixed,), _ = _stage_fwd(_mix_step, name="mix_fwd", ys=[(D_MODEL, BF16)], **mix)
    v = _matmul_nn(mixed, w["out"], name="out_proj")
    post = dict(chunk=256, nc=1, xs=[(x, D_MODEL), (v, D_MODEL)],
                params=[one(p["mix_post_norm"]), one(p["ffn_pre_norm"])], **row)
    (x1, h2), _ = _stage_fwd(_post_step, name="post_fwd", ys=[(D_MODEL, F32), (D_MODEL, BF16)], **post)
    gu = _matmul_nt(h2, w["up"], name="ffn_up")
    ffn = dict(rows=t, chunk=256, nc=2, groups=FFN_G, xs=[(gu, 2 * FFN_GW)], params=[p["ffn_conv_w"], p["ffn_conv_b"]],
               consts=[], carries=[(HALO, FFN_GW)])
    (act,), ffn_saved = _stage_fwd(_ffn_step, name="ffn_fwd", ys=[(FFN_GW, BF16)], **ffn)
    d = _matmul_nn(act, w["down"], name="ffn_down")

    def head_step(carry, xv, civ, pv, cv):
        x1_, d_, tgt = xv

        def per_row_loss(a, b, nw):
            e = a + _rms(b, nw) - tgt
            return 0.5 * jnp.mean(e * e, axis=1, keepdims=True)

        lrow, vjp = jax.vjp(per_row_loss, x1_, d_, pv[0])
        dx1_, dd_, dnw = vjp(jnp.ones_like(lrow))
        loss = jnp.broadcast_to(jnp.sum(lrow, axis=0, keepdims=True), (1, 128))
        return [], [dx1_, dd_], [], [dnw, loss]

    (dy, dd), _, (g_ffn_post, loss) = _scan_call(
        head_step, name="loss_head", chunk=256, nc=1, xs=[(x1, D_MODEL), (d, D_MODEL), (target, D_MODEL)],
        params=[one(p["ffn_post_norm"])], ys=[(D_MODEL, F32), (D_MODEL, BF16)], accs=[(1, D_MODEL), (1, 128)], **row)

    gw = {}
    gw["down"] = _matmul_tn(act, dd, name="g_down")
    dact = _matmul_nt(dd, w["down"], name="d_act", dep=emit("down", gw))
    (dgu,), (g_fcw, g_fcb) = _stage_bwd(_ffn_step, name="ffn_bwd", saved=ffn_saved, dys=[(dact, FFN_GW)], dxs=[BF16], **ffn)
    gw["up"] = _matmul_tn(dgu, h2, name="g_up")
    dh2 = _matmul_nn(dgu, w["up"], name="d_h2", dep=emit("up", gw))
    (dx1, dv), (g_mix_post, g_ffn_pre) = _stage_bwd(_post_step, name="post_bwd", saved=[], dys=[(dy, D_MODEL), (dh2, D_MODEL)],
                                                    dxs=[F32, BF16], **post)
    gw["out"] = _matmul_tn(mixed, dv, name="g_out")
    dmixed = _matmul_nt(dv, w["out"], name="d_mixed")
    (dgate, du_hg, du_ssm), _ = _stage_bwd(_mix_step, name="mix_bwd", saved=[], dys=[(dmixed, D_MODEL)],
                                           dxs=[BF16, BF16, BF16], **mix)
    gw["in_gate"] = _matmul_tn(dgate, h1, name="g_in_gate")
    gw["branch_hg"] = _matmul_tn(y_hg, du_hg, name="g_branch_hg")
    gw["branch_ssm"] = _matmul_tn(y_ssm, du_ssm, name="g_branch_ssm")
    dy_hg = _matmul_nt(du_hg, w["branch_hg"], name="d_y_hg", dep=emit("branches", gw))
    dy_ssm = _matmul_nt(du_ssm, w["branch_ssm"], name="d_y_ssm")
    (dproj_ssm,), g_ssd = _stage_bwd(_ssd_step, name="ssd_bwd", saved=ssd_saved, dys=[(dy_ssm, SSM_GW)], dxs=[BF16], **ssd)
    gw["in_ssm"] = _matmul_tn(dproj_ssm, h1, name="g_in_ssm")
    (dproj_hg,), (g_tab, g_hg_nw) = _stage_bwd(_hg_step, name="hg_bwd", saved=hg_saved, dys=[(dy_hg, HG_DK)], dxs=[BF16], **hg)
    gw["in_hg"] = _matmul_tn(dproj_hg, h1, name="g_in_hg")
    dh_a = _matmul_nn(dproj_hg, w["in_hg"], name="d_h1_hg", dep=emit("in", gw))
    dh_b = _matmul_nn(dproj_ssm, w["in_ssm"], name="d_h1_ssm")
    dh_c = _matmul_nn(dgate, w["in_gate"], name="d_h1_gate")

    def pre_bwd_step(carry, xv, civ, pv, cv):
        x_, da, db, dc, dres = xv
        _, vjp = jax.vjp(_rms, x_, pv[0])
        dx_, dnw = vjp(da + db + dc)
        return [], [dx_ + dres], [], [dnw]

    (grad_x,), _, (g_mix_pre,) = _scan_call(
        pre_bwd_step, name="pre_bwd", chunk=256, nc=1,
        xs=[(x, D_MODEL), (dh_a, D_MODEL), (dh_b, D_MODEL), (dh_c, D_MODEL), (dx1, D_MODEL)],
        params=[one(p["mix_pre_norm"])], ys=[(D_MODEL, F32)], accs=[(1, D_MODEL)], **row)

    gp = dict(mix_pre_norm=g_mix_pre[0], mix_post_norm=g_mix_post[0], ffn_pre_norm=g_ffn_pre[0], ffn_post_norm=g_ffn_post[0],
              hg_tab=g_tab, hg_nw=g_hg_nw, conv_w=g_ssd[0], conv_b=g_ssd[1], dt_bias=g_ssd[2], a_log=g_ssd[3],
              d_skip=g_ssd[4], ssm_nw=g_ssd[5], ffn_conv_w=g_fcw, ffn_conv_b=g_fcb, loss=loss[0, :, :1])
    return grad_x, gw, gp


def _small_to_kernel_layout(s):
    conv_idx = _conv_layout()
    pad_heads = lambda a: jnp.pad(a.reshape(SSM_GROUPS, 1, SSM_HPG), ((0, 0), (0, 0), (0, 128 - SSM_HPG)))
    return dict(
        mix_pre_norm=s["mix_pre_norm"], mix_post_norm=s["mix_post_norm"], ffn_pre_norm=s["ffn_pre_norm"],
        ffn_post_norm=s["ffn_post_norm"],
        hg_tab=s["hg_lb_table"].reshape(2, HG_HEADS, HG_DK).transpose(1, 0, 2),
        hg_nw=jnp.broadcast_to(s["hg_out_norm"].reshape(1, 1, HG_DK), (HG_HEADS, 1, HG_DK)),
        conv_w=_take_rows(s["ssm_conv_w"], conv_idx, axis=1).reshape(SSM_CONV, SSM_GROUPS, SSM_XBC).transpose(1, 0, 2),
        conv_b=_take_rows(s["ssm_conv_b"], conv_idx, axis=1).reshape(SSM_GROUPS, 1, SSM_XBC),
        dt_bias=pad_heads(s["ssm_dt_bias"]), a_log=pad_heads(s["ssm_A_log"]),
        d_skip=jnp.repeat(s["ssm_D"].reshape(SSM_HEADS), SSM_HEADDIM).reshape(SSM_GROUPS, 1, SSM_GW),
        ssm_nw=s["ssm_out_norm"].reshape(SSM_GROUPS, 1, SSM_GW),
        ffn_conv_w=s["ffn_conv_w"].reshape(FFN_CONV, FFN_G, FFN_GW).transpose(1, 0, 2),
        ffn_conv_b=s["ffn_conv_b"].reshape(FFN_G, 1, FFN_GW),
    )


def _small_from_kernel_layout(g):
    conv_inv = _inverse(_conv_layout().reshape(-1), SSM_CONV_DIM)
    heads = lambda a: a[:, 0, :SSM_HPG].reshape(1, SSM_HEADS)
    return dict(
        mix_pre_norm=g["mix_pre_norm"], mix_post_norm=g["mix_post_norm"], ffn_pre_norm=g["ffn_pre_norm"],
        ffn_post_norm=g["ffn_post_norm"],
        hg_lb_table=g["hg_tab"].transpose(1, 0, 2).reshape(2, HG_HEADS * HG_DK),
        hg_out_norm=jnp.sum(g["hg_nw"], axis=0),
        ssm_conv_w=_take_rows(g["conv_w"].transpose(1, 0, 2).reshape(SSM_CONV, -1), conv_inv, axis=1),
        ssm_conv_b=_take_rows(g["conv_b"].reshape(1, -1), conv_inv, axis=1),
        ssm_dt_bias=heads(g["dt_bias"]), ssm_A_log=heads(g["a_log"]),
        ssm_D=jnp.sum(g["d_skip"].reshape(SSM_HEADS, SSM_HEADDIM), axis=1).reshape(1, SSM_HEADS),
        ssm_out_norm=g["ssm_nw"].reshape(1, SSM_DINNER),
        ffn_conv_w=g["ffn_conv_w"].transpose(1, 0, 2).reshape(FFN_CONV, D_FF),
        ffn_conv_b=g["ffn_conv_b"].reshape(1, D_FF),
        loss=g["loss"],
    )


def kernel(x, w_in, mix_pre_norm, mix_post_norm, hg_lb_table, hg_out_norm, ssm_conv_w, ssm_conv_b, ssm_dt_bias, ssm_A_log, ssm_D, ssm_out_norm, w_branch_hg, w_branch_ssm, w_out, ffn_pre_norm, ffn_post_norm, ffn_w_up, ffn_conv_w, ffn_conv_b, ffn_w_down, loss_target, m_w_in, m_mix_pre_norm, m_mix_post_norm, m_hg_lb_table, m_hg_out_norm, m_ssm_conv_w, m_ssm_conv_b, m_ssm_dt_bias, m_ssm_A_log, m_ssm_D, m_ssm_out_norm, m_w_branch_hg, m_w_branch_ssm, m_w_out, m_ffn_pre_norm, m_ffn_post_norm, m_ffn_w_up, m_ffn_conv_w, m_ffn_conv_b, m_ffn_w_down, v_w_in, v_mix_pre_norm, v_mix_post_norm, v_hg_lb_table, v_hg_out_norm, v_ssm_conv_w, v_ssm_conv_b, v_ssm_dt_bias, v_ssm_A_log, v_ssm_D, v_ssm_out_norm, v_w_branch_hg, v_w_branch_ssm, v_w_out, v_ffn_pre_norm, v_ffn_post_norm, v_ffn_w_up, v_ffn_conv_w, v_ffn_conv_b, v_ffn_w_down):
    names = ["w_in", "mix_pre_norm", "mix_post_norm", "hg_lb_table", "hg_out_norm", "ssm_conv_w", "ssm_conv_b", "ssm_dt_bias",
             "ssm_A_log", "ssm_D", "ssm_out_norm", "w_branch_hg", "w_branch_ssm", "w_out", "ffn_pre_norm", "ffn_post_norm",
             "ffn_w_up", "ffn_conv_w", "ffn_conv_b", "ffn_w_down"]
    ws = dict(zip(names, (w_in, mix_pre_norm, mix_post_norm, hg_lb_table, hg_out_norm, ssm_conv_w, ssm_conv_b, ssm_dt_bias,
                          ssm_A_log, ssm_D, ssm_out_norm, w_branch_hg, w_branch_ssm, w_out, ffn_pre_norm, ffn_post_norm,
                          ffn_w_up, ffn_conv_w, ffn_conv_b, ffn_w_down)))
    ms = dict(zip(names, (m_w_in, m_mix_pre_norm, m_mix_post_norm, m_hg_lb_table, m_hg_out_norm, m_ssm_conv_w, m_ssm_conv_b,
                          m_ssm_dt_bias, m_ssm_A_log, m_ssm_D, m_ssm_out_norm, m_w_branch_hg, m_w_branch_ssm, m_w_out,
                          m_ffn_pre_norm, m_ffn_post_norm, m_ffn_w_up, m_ffn_conv_w, m_ffn_conv_b, m_ffn_w_down)))
    vs = dict(zip(names, (v_w_in, v_mix_pre_norm, v_mix_post_norm, v_hg_lb_table, v_hg_out_norm, v_ssm_conv_w, v_ssm_conv_b,
                          v_ssm_dt_bias, v_ssm_A_log, v_ssm_D, v_ssm_out_norm, v_w_branch_hg, v_w_branch_ssm, v_w_out,
                          v_ffn_pre_norm, v_ffn_post_norm, v_ffn_w_up, v_ffn_conv_w, v_ffn_conv_b, v_ffn_w_down)))
    me = 4 * lax.axis_index("x") + 2 * lax.axis_index("y") + lax.axis_index("c")

    idx_hg, idx_ssm, idx_gate = _in_proj_layout()
    up_idx = _up_layout()
    late_shards = [ffn_w_up[0].T.astype(BF16), w_branch_hg[0].astype(BF16), w_branch_ssm[0].astype(BF16),
                   w_out[0].astype(BF16), ffn_w_down[0].astype(BF16)]
    landing = [lax.dynamic_update_slice_in_dim(lax.empty((N_DEV,) + s.shape, s.dtype), s[None], me, axis=0)
               for s in late_shards]
    late = _push_start([s[None] for s in late_shards], landing, name="late_weights_start", plan=_plan_own_block)
    gathered = _all_gather([w_in[0].T.astype(BF16), ssm_conv_w[0] + late[4][0, 0], ffn_conv_w[0]], name="gather_in_proj")
    in_t = gathered[0].reshape(IN_TOTAL, D_MODEL)
    w = dict(in_hg=_take_rows(in_t, idx_hg), in_ssm=_take_rows(in_t, idx_ssm), in_gate=_take_rows(in_t, idx_gate))
    small = {k: ws[k] for k, _ in _SMALL[:-1]}
    small["ssm_conv_w"] = gathered[1].transpose(1, 0, 2).reshape(SSM_CONV, SSM_CONV_DIM)
    small["ffn_conv_w"] = gathered[2].transpose(1, 0, 2).reshape(FFN_CONV, D_FF)
    small = {k: small[k].reshape(s) for k, s in _SMALL[:-1]}

    def late_weights(after):
        landed = _push_wait(late, after, name="late_weights_wait", plan=_plan_own_block)
        up_all, bhg, bssm, out, down = _push(landed, name="late_weights_pass_on", plan=_plan_pass_on)
        return dict(up=_take_rows(up_all.reshape(2 * D_FF, D_MODEL), up_idx), branch_hg=bhg.reshape(D_MODEL, D_MODEL),
                    branch_ssm=bssm.reshape(SSM_DINNER, D_MODEL), out=out.reshape(D_MODEL, D_MODEL),
                    down=down.reshape(D_FF, D_MODEL))

    in_flight = []

    def launch(key, named_parts):
        ks, parts = zip(*named_parts)
        from_sibling = _push(list(parts), name="grads_to_sibling_" + key, out_slots=N_CHIPS, plan=_plan_sibling)
        sums = [_pair_sum(p, r, name="pair_sum_" + k, tc=256) for k, p, r in zip(ks, parts, from_sibling)]
        handles = _push_start([q for q, _ in sums], [z for _, z in sums], name="grads_to_chips_start_" + key, plan=_plan_chips)
        in_flight.append((key, ks, handles))
        return handles[4]

    def emit(key, gw):
        blocks = lambda a: a.reshape(N_DEV, -1, D_MODEL)
        if key == "down":
            return launch(key, [("ffn_w_down", blocks(gw["down"]))])
        if key == "up":
            return launch(key, [("ffn_w_up", blocks(_take_rows(gw["up"], _inverse(up_idx, 2 * D_FF))))])
        if key == "branches":
            return launch(key, [("w_branch_hg", blocks(gw["branch_hg"])), ("w_branch_ssm", blocks(gw["branch_ssm"])),
                                ("w_out", blocks(gw["out"]))])
        in_all = jnp.concatenate([gw["in_hg"], gw["in_ssm"], gw["in_gate"]], axis=0)
        in_inv = _inverse(np.concatenate([idx_hg, idx_ssm, idx_gate]), IN_TOTAL)
        return launch(key, [("w_in", blocks(_take_rows(in_all, in_inv)))])

    grad_x, gw, gp = _local_step(x[0], loss_target[0], w, _small_to_kernel_layout(small), late_weights, emit)

    big_names = ["w_in", "ffn_w_up", "w_branch_hg", "w_branch_ssm", "w_out", "ffn_w_down"]
    grads = {}
    for key, ks, handles in in_flight:
        landed = _push_wait(handles, grad_x, name="grads_to_chips_wait_" + key, plan=_plan_chips)
        for k, r in zip(ks, landed):
            g = _sum_blocks(r, name="sum_" + k, tc=256)
            grads[k] = g.T if k in ("w_in", "ffn_w_up") else g
    small_all = _push([_pack(_small_from_kernel_layout(gp))[None]], name="small_to_everyone", out_slots=N_DEV,
                      plan=_plan_everyone)
    small_g = _unpack(_sum_blocks(small_all[0], name="sum_small", tc=128))
    loss = small_g.pop("loss").reshape(())
    for k, g in small_g.items():
        if k in ("ssm_conv_w", "ffn_conv_w"):
            n = g.shape[1] // N_DEV
            g = lax.dynamic_slice_in_dim(g, me * n, n, axis=1)
        grads[k] = g

    delta, new_m, new_v = {}, {}, {}
    for k in big_names:
        delta[k], new_m[k], new_v[k] = _adamw(ws[k][0], grads[k], ms[k][0], vs[k][0], name="adamw_" + k, tr=64)
    small_names = [k for k in names if k not in big_names]
    flat = lambda d: jnp.concatenate([d[k].astype(F32).reshape(-1) for k in small_names])
    n_small = sum(int(np.prod(ws[k].shape)) for k in small_names)
    rows = 8 * (-(-n_small // 1024))
    pack2 = lambda d: jnp.pad(flat(d), (0, rows * 128 - n_small)).reshape(rows, 128)
    v_packed = jnp.pad(flat(vs), (0, rows * 128 - n_small), constant_values=1.0).reshape(rows, 128)
    packed = _adamw(pack2(ws), pack2(grads), pack2(ms), v_packed, name="adamw_small", tr=rows)
    o = 0
    for k in small_names:
        n = int(np.prod(ws[k].shape))
        delta[k], new_m[k], new_v[k] = (a.reshape(-1)[o:o + n].reshape(ws[k].shape) for a in packed)
        o += n

    full = lambda d: [d[k].reshape(ws[k].shape) for k in names]
    return (loss, grad_x[None], *full(grads), *full(delta), *full(new_m), *full(new_v))
```

```python
import functools

import numpy as np
import jax
import jax.numpy as jnp
from jax import lax
from jax.experimental import pallas as pl
from jax.experimental.pallas import tpu as pltpu

F32, BF16 = jnp.float32, jnp.bfloat16

D_MODEL = 2048
EPS = 1e-6
HG_HEADS, HG_DK, HG_CHUNK = 16, 128, 64
HG_BLK = 4 * HG_DK
SSM_DINNER, SSM_HEADDIM, SSM_HEADS, SSM_GROUPS, SSM_DSTATE, SSM_CONV = 4096, 64, 64, 8, 128, 4
SSM_CHUNK = 128
SSM_GW = SSM_DINNER // SSM_GROUPS
SSM_HPG = SSM_HEADS // SSM_GROUPS
SSM_XBC = SSM_GW + 2 * SSM_DSTATE
SSM_BLK = SSM_XBC + 128 + SSM_GW
SSM_CONV_DIM = SSM_DINNER + 2 * SSM_GROUPS * SSM_DSTATE
D_FF, FFN_CONV = 5632, 3
FFN_GW = 512
FFN_G = D_FF // FFN_GW
IN_TOTAL = 22592
N_DEV = 8
HALO = 8
VMEM_LIMIT = 52 * 1024 * 1024
ADAM_LR, ADAM_B1, ADAM_B2, ADAM_EPS, ADAM_WD, ADAM_STEP = 0.001, 0.9, 0.999, 1e-08, 0.01, 10

_DIMS = {"nn": ((1,), (0,)), "nt": ((1,), (1,)), "tn": ((0,), (0,))}


def _mm_raw(a, b, mode):
    return lax.dot_general(a.astype(BF16), b.astype(BF16), (_DIMS[mode], ((), ())), preferred_element_type=F32)


@functools.partial(jax.custom_vjp, nondiff_argnums=(2,))
def _mm(a, b, mode):
    return _mm_raw(a, b, mode)


def _mm_fwd(a, b, mode):
    return _mm_raw(a, b, mode), (a, b)


def _mm_bwd(mode, res, dc):
    a, b = res
    if mode == "nn":
        return _mm_raw(dc, b, "nt"), _mm_raw(a, dc, "tn")
    if mode == "nt":
        return _mm_raw(dc, b, "nn"), _mm_raw(dc, a, "tn")
    return _mm_raw(b, dc, "nt"), _mm_raw(a, dc, "nn")


_mm.defvjp(_mm_fwd, _mm_bwd)


def _cmm_raw(m, x, mode):
    hi = x.astype(BF16)
    r1 = x - hi.astype(F32)
    mid = r1.astype(BF16)
    lo = (r1 - mid.astype(F32)).astype(BF16)
    dn = (_DIMS[mode], ((), ()))
    dot = lambda p: lax.dot_general(m, p, dn, preferred_element_type=F32)
    return dot(hi) + dot(mid) + dot(lo)


@jax.custom_vjp
def _cmm(m, x):
    return _cmm_raw(m, x, "nn")


def _cmm_fwd(m, x):
    return _cmm_raw(m, x, "nn"), m


def _cmm_bwd(m, dy):
    return jnp.zeros_like(m), _cmm_raw(m, dy, "tn")


_cmm.defvjp(_cmm_fwd, _cmm_bwd)


@functools.partial(jax.custom_vjp, nondiff_argnums=(1,))
def _sroll(x, s):
    return pltpu.roll(x, s, 0) if s else x


def _sroll_fwd(x, s):
    return _sroll(x, s), None


def _sroll_bwd(s, _, ct):
    return ((pltpu.roll(ct, ct.shape[0] - s, 0) if s else ct),)


_sroll.defvjp(_sroll_fwd, _sroll_bwd)


def _rms(x, w):
    return x * lax.rsqrt(jnp.mean(x * x, axis=-1, keepdims=True) + EPS) * w


def _softplus(x):
    return jnp.maximum(x, 0.0) + jnp.log(1.0 + jnp.exp(-jnp.abs(x)))


def _causal_conv(halo, x, w, b):
    k_taps = w.shape[0]
    xe = jnp.concatenate([halo, x], axis=0)
    out = b
    for k in range(k_taps):
        out = out + w[k:k + 1, :] * _sroll(xe, k_taps - 1 - k)[HALO:, :]
    return out


def _hg_consts():
    c = HG_CHUNK
    t = np.arange(c)
    blocks, pair = [], []
    for m in (32, 16, 8, 4, 2, 1):
        pos = t % (2 * m)
        late = pos >= m
        mid = t - pos + m
        j = t[None, :]
        mq = late[:, None] & (j >= mid[:, None]) & (j <= t[:, None])
        mk = (~late)[:, None] & (j > t[:, None]) & (j <= mid[:, None] - 1)
        blocks.append(mq | mk)
        parent = t // (2 * m)
        pair.append((parent[:, None] == parent[None, :]) & late[:, None] & (~late)[None, :])
    blocks.append(t[None, :] <= t[:, None])
    mall = jnp.asarray(np.concatenate(blocks, 0).astype(np.float32), BF16)
    pair = jnp.asarray(np.stack(pair, 0).astype(np.float32))
    eye = jnp.asarray(np.eye(c, dtype=np.float32))
    return [mall, pair, eye]


def _hg_step(carry, xs, params, consts):
    (st,) = carry
    blk = xs[0].astype(F32)
    tab, nw = params
    mall, pair, eye = consts
    c, dk = HG_CHUNK, HG_DK
    q_raw, f_raw, v, og = blk[:, :dk], blk[:, dk:2 * dk], blk[:, 2 * dk:3 * dk], blk[:, 3 * dk:]
    lb = jax.nn.sigmoid(tab[0:1, :] - tab[1:2, :])
    f = lb + (1.0 - lb) * jax.nn.sigmoid(f_raw)
    g = jnp.log(f)
    kk = 1.0 - f
    qh = jax.nn.silu(q_raw) * (HG_DK ** -0.5)
    yield
    sums = _cmm(mall, g)
    yield
    b = sums[6 * c:, :]
    fac = jnp.exp(sums[:6 * c, :])
    scores = eye * jnp.sum(qh * kk, axis=1, keepdims=True)
    b_last = jnp.sum(g, axis=0, keepdims=True)
    yield
    inter = _mm(qh * jnp.exp(b), st, "nt")
    st_new = st * jnp.exp(b_last) + _mm(v, kk * jnp.exp(b_last - b), "tn")
    yield
    for l in range(6):
        fl = fac[l * c:(l + 1) * c, :]
        scores = scores + pair[l] * _mm(qh * fl, kk * fl, "nt")
        if l % 2:
            yield
    o = _mm(scores, v, "nn") + inter
    yield
    y = _rms(o, nw) * jax.nn.silu(og)
    return [st_new], [y]


def _ssd_consts():
    t = np.arange(SSM_CHUNK)
    tril = (t[None, :] <= t[:, None]).astype(np.float32)
    return [jnp.asarray(tril, BF16), jnp.asarray(tril)]


def _ssd_step(carry, xs, params, consts):
    st, halo = carry
    blk = xs[0].astype(F32)
    conv_w, conv_b, dtb, alog, dskip, nw = params
    tril_b, tril = consts
    c = SSM_CHUNK
    raw, dtr, z = blk[:, :SSM_XBC], blk[:, SSM_XBC:SSM_XBC + 128], blk[:, SSM_XBC + 128:]
    act = jax.nn.silu(_causal_conv(halo, raw, conv_w, conv_b))
    xh, bm, cm = act[:, :SSM_GW], act[:, SSM_GW:SSM_GW + SSM_DSTATE], act[:, SSM_GW + SSM_DSTATE:]
    dt = _softplus(dtr + dtb)
    da = dt * (-jnp.exp(alog))
    acum = _cmm(tril_b, da)
    acum_t = acum.T
    a_last = jnp.sum(da, axis=0, keepdims=True)
    cb_causal = _mm(cm, bm, "nt") * tril
    lane = lax.broadcasted_iota(jnp.int32, (c, 128), 1)
    row = lax.broadcasted_iota(jnp.int32, (128, 128), 0)
    first = lane < SSM_HEADDIM
    ys, st_new = [], []
    for j in range(SSM_HPG // 2):
        xp = xh[:, 128 * j:128 * (j + 1)]
        sp = st[128 * j:128 * (j + 1), :]
        r0, r1 = 2 * j, 2 * j + 1
        col = lambda a, r: jnp.broadcast_to(a[:, r:r + 1], (c, 128))
        xdt = xp * jnp.where(first, col(dt, r0), col(dt, r1))
        yj = _mm(cm, sp, "nt") * jnp.exp(jnp.where(first, col(acum, r0), col(acum, r1)))
        for r, keep in ((r0, first), (r1, ~first)):
            dec = jnp.broadcast_to(acum[:, r:r + 1], (c, c)) - jnp.broadcast_to(acum_t[r:r + 1, :], (c, c))
            m = cb_causal * jnp.exp(jnp.minimum(dec, 0.0))
            yj = yj + _mm(m, jnp.where(keep, xdt, 0.0), "nn")
        al0, al1 = a_last[:, r0:r0 + 1], a_last[:, r1:r1 + 1]
        wts = jnp.exp(jnp.where(first, al0 - col(acum, r0), al1 - col(acum, r1)))
        st_new.append(jnp.where(row < SSM_HEADDIM, jnp.exp(al0), jnp.exp(al1)) * sp + _mm(xdt * wts, bm, "tn"))
        ys.append(yj)
    y = jnp.concatenate(ys, axis=1) + dskip * xh
    y = _rms(y * jax.nn.silu(z), nw)
    return [jnp.concatenate(st_new, axis=0), raw[c - HALO:, :]], [y]


def _ffn_step(carry, xs, params, consts):
    (halo,) = carry
    blk = xs[0].astype(F32)
    conv_w, conv_b = params
    gate, up = blk[:, :FFN_GW], blk[:, FFN_GW:]
    a = jax.nn.gelu(_causal_conv(halo, gate, conv_w, conv_b), approximate=True) * up
    return [gate[gate.shape[0] - HALO:, :]], [a]


def _pre_step(carry, xs, params, consts):
    return [], [_rms(xs[0], params[0])]


def _mix_step(carry, xs, params, consts):
    gates, uh, us = (a.astype(F32) for a in xs)
    return [], [jax.nn.sigmoid(gates[:, :D_MODEL]) * uh + jax.nn.sigmoid(gates[:, D_MODEL:]) * us]


def _post_step(carry, xs, params, consts):
    x, v = xs
    x1 = x + _rms(v, params[0])
    return [], [x1, _rms(x1, params[1])]


def _scan_call(step, *, name, rows, chunk, nc, groups, xs, cins=(), params=(), consts=(), carries=(), ys=(), couts=(),
               accs=(), reverse=False, gpb=1, multi=False):
    blk_rows = chunk * nc
    nb = rows // blk_rows
    n_chunks = rows // chunk
    assert nb * blk_rows == rows and groups % gpb == 0
    rb = (lambda i: nb - 1 - i) if reverse else (lambda i: i)
    n_x, n_ci, n_p, n_c = len(xs), len(cins), len(params), len(consts)
    n_y, n_co, n_a = len(ys), len(couts), len(accs)

    def chunk_spec(shape):
        zeros = (0,) * len(shape)
        return pl.BlockSpec((gpb, nc) + tuple(shape), lambda g, i: (g, rb(i)) + zeros)

    in_specs = [pl.BlockSpec((blk_rows, gpb * w), lambda g, i: (rb(i), g)) for _, w in xs]
    in_specs += [chunk_spec(a.shape[2:]) for a in cins]
    in_specs += [pl.BlockSpec((gpb,) + tuple(a.shape[1:]), lambda g, i: (g, 0, 0)) for a in params]
    in_specs += [pl.BlockSpec(a.shape, (lambda nd: lambda g, i: (0,) * nd)(a.ndim)) for a in consts]
    out_specs = [pl.BlockSpec((blk_rows, gpb * w), lambda g, i: (rb(i), g)) for w, _ in ys]
    out_specs += [chunk_spec(s) for s in couts]
    out_specs += [pl.BlockSpec((gpb, r, c), lambda g, i: (g, 0, 0)) for r, c in accs]
    out_shape = [jax.ShapeDtypeStruct((rows, groups * w), dt) for w, dt in ys]
    out_shape += [jax.ShapeDtypeStruct((groups, n_chunks) + tuple(s), F32) for s in couts]
    out_shape += [jax.ShapeDtypeStruct((groups, r, c), F32) for r, c in accs]
    x_widths = [w for _, w in xs]
    y_widths = [w for w, _ in ys]

    def body(*refs):
        x_refs = refs[:n_x]
        ci_refs = refs[n_x:n_x + n_ci]
        p_refs = refs[n_x + n_ci:n_x + n_ci + n_p]
        c_refs = refs[n_x + n_ci + n_p:n_x + n_ci + n_p + n_c]
        o = n_x + n_ci + n_p + n_c
        y_refs = refs[o:o + n_y]
        co_refs = refs[o + n_y:o + n_y + n_co]
        a_refs = refs[o + n_y + n_co:o + n_y + n_co + n_a]
        carry_refs = refs[o + n_y + n_co + n_a:]

        @pl.when(pl.program_id(1) == 0)
        def _():
            for s in carry_refs:
                s[...] = jnp.zeros(s.shape, F32)
            for a in a_refs:
                a[...] = jnp.zeros(a.shape, F32)

        cvals = [c[...] for c in c_refs]

        def one_chunk(i, _):
            c = (nc - 1 - i) if reverse else i
            r0 = c * chunk if isinstance(c, int) else pl.multiple_of(c * chunk, chunk)
            loaded = []
            for u in range(gpb):
                carry = [s[u] for s in carry_refs]
                xv = [x[pl.ds(r0, chunk), u * w:(u + 1) * w] for x, w in zip(x_refs, x_widths)]
                civ = [ci[u, c] for ci in ci_refs]
                loaded.append((carry, xv, civ, [p[u] for p in p_refs]))
            results = step(loaded, cvals) if multi else [step(*args, cvals) for args in loaded]
            for u, (new_carry, yv, cov, av) in enumerate(results):
                for s, val in zip(carry_refs, new_carry):
                    s[u] = val
                for y, w, val in zip(y_refs, y_widths, yv):
                    y[pl.ds(r0, chunk), u * w:(u + 1) * w] = val.astype(y.dtype)
                for co, val in zip(co_refs, cov):
                    co[u, c] = val
                for a, val in zip(a_refs, av):
                    a[u] += val
            return 0

        if nc == 1:
            one_chunk(0, 0)
        else:
            lax.fori_loop(0, nc, one_chunk, 0)

    outs = pl.pallas_call(
        body, name=name, grid=(groups // gpb, nb), in_specs=in_specs, out_specs=out_specs, out_shape=out_shape,
        scratch_shapes=[pltpu.VMEM((gpb,) + tuple(s), F32) for s in carries],
        compiler_params=pltpu.CompilerParams(dimension_semantics=("arbitrary", "arbitrary"),
                                             vmem_limit_bytes=VMEM_LIMIT),
    )(*[a for a, _ in xs], *cins, *params, *consts)
    return outs[:n_y], outs[n_y:n_y + n_co], outs[n_y + n_co:]


def _run_interleaved(step, arg_tuples):
    runs = [step(*args) for args in arg_tuples]
    if not hasattr(runs[0], "send"):
        return runs
    results, live = [None] * len(runs), list(range(len(runs)))
    while live:
        for u in list(live):
            try:
                next(runs[u])
            except StopIteration as done:
                results[u] = done.value
                live.remove(u)
    return results


def _stage_fwd(step, *, name, rows, chunk, nc, groups, xs, params, consts, carries, ys, gpb=1):
    def fstep(loaded, cv):
        outs = _run_interleaved(step, [(carry, xv, pv, cv) for carry, xv, _, pv in loaded])
        return [(new_carry, yv, carry, []) for (new_carry, yv), (carry, _, _, _) in zip(outs, loaded)]

    yv, saved, _ = _scan_call(fstep, name=name, rows=rows, chunk=chunk, nc=nc, groups=groups, xs=xs, params=params,
                              consts=consts, carries=carries, ys=ys, couts=carries, gpb=gpb, multi=True)
    return yv, saved


def _stage_bwd(step, *, name, rows, chunk, nc, groups, xs, saved, params, consts, carries, dys, dxs, gpb=1):
    n_x = len(xs)

    def bstep(loaded, cv):
        civs = [list(civ) for _, _, civ, _ in loaded]
        xvs = [list(xv_all[:n_x]) for _, xv_all, _, _ in loaded]
        pvs = [list(pv) for _, _, _, pv in loaded]
        cts = [(list(dcarry), [d.astype(F32) for d in xv_all[n_x:]]) for dcarry, xv_all, _, _ in loaded]

        def fwd(civs_, xvs_, pvs_):
            outs = _run_interleaved(step, [(c_, x_, p_, cv) for c_, x_, p_ in zip(civs_, xvs_, pvs_)])
            return [(list(new_carry), list(yv)) for new_carry, yv in outs]

        _, vjp = jax.vjp(fwd, civs, xvs, pvs)
        dcivs, dxvs, dpvs = vjp(cts)
        return [(dc, dx, [], dp) for dc, dx, dp in zip(dcivs, dxvs, dpvs)]

    dxv, _, dpv = _scan_call(bstep, name=name, rows=rows, chunk=chunk, nc=nc, groups=groups, xs=list(xs) + list(dys),
                             cins=saved, params=params, consts=consts, carries=carries,
                             ys=[(w, dt) for (_, w), dt in zip(xs, dxs)], accs=[a.shape[1:] for a in params],
                             reverse=True, gpb=gpb, multi=True)
    return dxv, dpv


def _mm_params(sem):
    return pltpu.CompilerParams(dimension_semantics=sem, vmem_limit_bytes=VMEM_LIMIT)


def _after(dep):
    return ([], []) if dep is None else ([dep], [pl.BlockSpec(memory_space=pl.ANY)])


def _matmul_nt(a, b, *, name, tm=1024, tn=512, dep=None):
    m, k = a.shape
    n = b.shape[0]
    tm = min(tm, m)
    deps, dep_specs = _after(dep)

    def body(a_ref, b_ref, *rest):
        rest[-1][...] = lax.dot_general(a_ref[...], b_ref[...], (_DIMS["nt"], ((), ())), preferred_element_type=F32)

    return pl.pallas_call(
        body, name=name, grid=(m // tm, n // tn),
        in_specs=[pl.BlockSpec((tm, k), lambda i, j: (i, 0)), pl.BlockSpec((tn, k), lambda i, j: (j, 0))] + dep_specs,
        out_specs=pl.BlockSpec((tm, tn), lambda i, j: (i, j)),
        out_shape=jax.ShapeDtypeStruct((m, n), F32),
        compiler_params=_mm_params(("parallel", "arbitrary")),
    )(a, b, *deps)


def _matmul_nn(a, b, *, name, dep=None):
    m, k = a.shape
    n = b.shape[1]
    tm, tn = (1024, 512) if k <= 4096 else (1024, 256) if k <= 6144 else (512, 512) if k <= 8192 else (512, 256)
    tm = min(tm, m)
    deps, dep_specs = _after(dep)

    def body(a_ref, b_ref, *rest):
        rest[-1][...] = jnp.dot(a_ref[...], b_ref[...], preferred_element_type=F32)

    return pl.pallas_call(
        body, name=name, grid=(m // tm, n // tn),
        in_specs=[pl.BlockSpec((tm, k), lambda i, j: (i, 0)), pl.BlockSpec((k, tn), lambda i, j: (0, j))] + dep_specs,
        out_specs=pl.BlockSpec((tm, tn), lambda i, j: (i, j)),
        out_shape=jax.ShapeDtypeStruct((m, n), F32),
        compiler_params=_mm_params(("parallel", "arbitrary")),
    )(a, b, *deps)


def _matmul_tn(x, y, *, name, tp=512, tq=512):
    t, p = x.shape
    q = y.shape[1]

    def body(x_ref, y_ref, o_ref):
        o_ref[...] = lax.dot_general(x_ref[...], y_ref[...], (_DIMS["tn"], ((), ())),
                                     preferred_element_type=F32).astype(o_ref.dtype)

    return pl.pallas_call(
        body, name=name, grid=(p // tp, q // tq),
        in_specs=[pl.BlockSpec((t, tp), lambda i, j: (0, i)), pl.BlockSpec((t, tq), lambda i, j: (0, j))],
        out_specs=pl.BlockSpec((tp, tq), lambda i, j: (i, j)),
        out_shape=jax.ShapeDtypeStruct((p, q), BF16),
        compiler_params=_mm_params(("parallel", "arbitrary")),
    )(x, y)


N_CHIPS = N_DEV // 2


def _all_gather(arrays, *, name):
    n = len(arrays)
    out_shape = [jax.ShapeDtypeStruct((N_DEV,) + tuple(a.shape), a.dtype) for a in arrays]

    def body(*refs):
        in_refs, out_refs = refs[:n], refs[n:2 * n]
        send_sems, recv_sems, local_sems = refs[2 * n:]
        x, y, c = lax.axis_index("x"), lax.axis_index("y"), lax.axis_index("c")
        me, sibling = (x, y, c), (x, y, 1 - c)
        chips = [(1 - x, y), (x, 1 - y), (1 - x, 1 - y)]

        def copy(a, k, block, to, src=None):
            slot = out_refs[a].at[4 * block[0] + 2 * block[1] + block[2]]
            return pltpu.make_async_remote_copy(
                src_ref=slot if src is None else src, dst_ref=slot, send_sem=send_sems.at[a, k],
                recv_sem=recv_sems.at[a, k], device_id=to, device_id_type=pl.DeviceIdType.MESH)

        mine = [pltpu.make_async_copy(in_refs[a], out_refs[a].at[4 * x + 2 * y + c], local_sems.at[a]) for a in range(n)]
        first = []
        for a in range(n):
            first.append(copy(a, 0, me, sibling, src=in_refs[a]))
            first += [copy(a, 1 + j, me, (*chip, c), src=in_refs[a]) for j, chip in enumerate(chips)]
        for cp in mine + first:
            cp.start()
        passed = []
        for j, chip in enumerate(chips):
            for a in range(n):
                copy(a, 1 + j, (*chip, c), me).wait_recv()
                passed.append(copy(a, 4 + j, (*chip, c), sibling))
                passed[-1].start()
        for a in range(n):
            copy(a, 0, sibling, me).wait_recv()
            for j, chip in enumerate(chips):
                copy(a, 4 + j, (*chip, 1 - c), me).wait_recv()
        for cp in first + passed:
            cp.wait_send()
        for cp in mine:
            cp.wait()

    any_spec = pl.BlockSpec(memory_space=pl.ANY)
    return pl.pallas_call(
        body, name=name, in_specs=[any_spec] * n, out_specs=[any_spec] * n, out_shape=out_shape,
        scratch_shapes=[pltpu.SemaphoreType.DMA((n, N_DEV - 1)), pltpu.SemaphoreType.DMA((n, N_DEV - 1)),
                        pltpu.SemaphoreType.DMA((n,))],
        compiler_params=pltpu.CompilerParams(has_side_effects=True),
    )(*arrays)


def _push(arrays, *, name, plan, out_slots=None):
    n = len(arrays)
    in_place = out_slots is None
    out_shape = [jax.ShapeDtypeStruct(((a.shape[0] if in_place else out_slots),) + tuple(a.shape[1:]), a.dtype) for a in arrays]
    n_tr = len(plan(0, 0, 0)[0])

    def body(*refs):
        in_refs, out_refs = refs[:n], refs[n:2 * n]
        send_sems, recv_sems, local_sems = refs[2 * n:]
        src_refs = out_refs if in_place else in_refs
        transfers, local = plan(lax.axis_index("x"), lax.axis_index("y"), lax.axis_index("c"))
        copies = []
        for a in range(n):
            if local is not None:
                copies.append(pltpu.make_async_copy(src_refs[a].at[local[0]], out_refs[a].at[local[1]], local_sems.at[a]))
            for k, (peer, src, dst) in enumerate(transfers):
                copies.append(pltpu.make_async_remote_copy(
                    src_ref=src_refs[a].at[src], dst_ref=out_refs[a].at[dst], send_sem=send_sems.at[a, k],
                    recv_sem=recv_sems.at[a, k], device_id=peer, device_id_type=pl.DeviceIdType.MESH))
        for cp in copies:
            cp.start()
        for cp in copies:
            cp.wait()

    any_spec = pl.BlockSpec(memory_space=pl.ANY)
    return pl.pallas_call(
        body, name=name, in_specs=[any_spec] * n, out_specs=[any_spec] * n, out_shape=out_shape,
        input_output_aliases={a: a for a in range(n)} if in_place else {},
        scratch_shapes=[pltpu.SemaphoreType.DMA((n, n_tr)), pltpu.SemaphoreType.DMA((n, n_tr)),
                        pltpu.SemaphoreType.DMA((n,))],
        compiler_params=pltpu.CompilerParams(has_side_effects=True),
    )(*arrays)


_HBM_SPEC = pl.BlockSpec(memory_space=pltpu.HBM)
_SEM_SPEC = pl.BlockSpec(memory_space=pltpu.SEMAPHORE)
_DATAFLOW = pltpu.SideEffectType.DATAFLOW_SIDE_EFFECTING


def _push_start(sources, landing, *, name, plan, after=None):
    n = len(sources)
    n_tr = len(plan(0, 0, 0)[0])
    deps, dep_specs = _after(after)

    def body(*refs):
        src_refs, land_refs = refs[:n], refs[n:2 * n]
        o = 2 * n + len(deps)
        send_sems, recv_sems, token = refs[o], refs[o + 1], refs[-1]
        transfers, _ = plan(lax.axis_index("x"), lax.axis_index("y"), lax.axis_index("c"))
        for a in range(n):
            for k, (peer, src, dst) in enumerate(transfers):
                pltpu.make_async_remote_copy(
                    src_ref=src_refs[a].at[src], dst_ref=land_refs[a].at[dst], send_sem=send_sems.at[a * n_tr + k],
                    recv_sem=recv_sems.at[a * n_tr + k], device_id=peer, device_id_type=pl.DeviceIdType.MESH).start()
        token[...] = jnp.zeros(token.shape, token.dtype)

    hbm = lambda a: pltpu.HBM(a.shape, a.dtype)
    outs = pl.pallas_call(
        body, name=name,
        out_shape=(pltpu.SemaphoreType.DMA((n * n_tr,)), pltpu.SemaphoreType.DMA((n * n_tr,)), *[hbm(a) for a in sources],
                   *[hbm(a) for a in landing], jax.ShapeDtypeStruct((8, 128), F32)),
        in_specs=[_HBM_SPEC] * (2 * n) + dep_specs,
        out_specs=(_SEM_SPEC, _SEM_SPEC, *[_HBM_SPEC] * (2 * n), pl.BlockSpec(memory_space=pltpu.VMEM)),
        input_output_aliases={i: 2 + i for i in range(2 * n)},
        compiler_params=pltpu.CompilerParams(has_side_effects=_DATAFLOW),
    )(*[pltpu.with_memory_space_constraint(a, pltpu.HBM) for a in list(sources) + list(landing)], *deps)
    return outs[0], outs[1], list(outs[2:2 + n]), list(outs[2 + n:2 + 2 * n]), outs[-1]


def _push_wait(handles, after, *, name, plan):
    send_sems, recv_sems, sources, landing, _ = handles
    n = len(sources)

    def body(*refs):
        src_refs, land_refs = refs[:n], refs[n:2 * n]
        send_sems_, recv_sems_ = refs[2 * n], refs[2 * n + 1]
        transfers, _ = plan(lax.axis_index("x"), lax.axis_index("y"), lax.axis_index("c"))
        n_tr = len(transfers)
        for a in range(n):
            for k, (peer, src, dst) in enumerate(transfers):
                cp = pltpu.make_async_remote_copy(
                    src_ref=src_refs[a].at[src], dst_ref=land_refs[a].at[dst], send_sem=send_sems_.at[a * n_tr + k],
                    recv_sem=recv_sems_.at[a * n_tr + k], device_id=peer, device_id_type=pl.DeviceIdType.MESH)
                cp.wait_send()
                cp.wait_recv()

    hbm = lambda a: pltpu.HBM(a.shape, a.dtype)
    outs = pl.pallas_call(
        body, name=name, out_shape=tuple(hbm(a) for a in list(sources) + list(landing)),
        in_specs=[_HBM_SPEC] * (2 * n) + [_SEM_SPEC, _SEM_SPEC, pl.BlockSpec(memory_space=pl.ANY)],
        out_specs=[_HBM_SPEC] * (2 * n), input_output_aliases={i: i for i in range(2 * n)},
        compiler_params=pltpu.CompilerParams(has_side_effects=_DATAFLOW),
    )(*sources, *landing, send_sems, recv_sems, after)
    return list(outs[n:])


def _plan_everyone(x, y, c):
    me = 4 * x + 2 * y + c
    peers = [(1 - x if k & 4 else x, 1 - y if k & 2 else y, 1 - c if k & 1 else c) for k in range(1, N_DEV)]
    return [(p, 0, me) for p in peers], (0, me)


def _plan_sibling(x, y, c):
    return [((x, y, 1 - c), 2 * chip + (1 - c), chip) for chip in range(N_CHIPS)], None


def _plan_chips(x, y, c):
    mine = 2 * x + y
    peers = [(1 - x, y), (x, 1 - y), (1 - x, 1 - y)]
    return [((px, py, c), 2 * px + py, mine) for px, py in peers], (mine, mine)


def _plan_own_block(x, y, c):
    me = 4 * x + 2 * y + c
    peers = [(x, y, 1 - c), (1 - x, y, c), (x, 1 - y, c), (1 - x, 1 - y, c)]
    return [(p, 0, me) for p in peers], None


def _plan_pass_on(x, y, c):
    slots = [4 * px + 2 * py + c for px, py in ((1 - x, y), (x, 1 - y), (1 - x, 1 - y))]
    return [((x, y, 1 - c), s, s) for s in slots], None


def _pair_sum(parts, received, *, name, tc):
    _, r, c = parts.shape
    core = lax.axis_index("c").astype(jnp.int32).reshape(1)

    def body(core_ref, p_ref, r_ref, o_ref, o2_ref):
        s = (p_ref[...].astype(F32) + r_ref[...].astype(F32)).astype(o_ref.dtype)
        o_ref[...] = s
        o2_ref[...] = s

    out = pl.BlockSpec((None, r, tc), lambda i, j, core_ref: (i, 0, j))
    return pl.pallas_call(
        body, name=name,
        grid_spec=pltpu.PrefetchScalarGridSpec(
            num_scalar_prefetch=1, grid=(N_CHIPS, c // tc),
            in_specs=[pl.BlockSpec((None, r, tc), lambda i, j, core_ref: (2 * i + core_ref[0], 0, j)),
                      pl.BlockSpec((None, r, tc), lambda i, j, core_ref: (i, 0, j))],
            out_specs=[out, out]),
        out_shape=[jax.ShapeDtypeStruct((N_CHIPS, r, c), BF16)] * 2,
        compiler_params=pltpu.CompilerParams(dimension_semantics=("parallel", "parallel"), vmem_limit_bytes=VMEM_LIMIT),
    )(core, parts, received)


def _sum_blocks(a, *, name, tc):
    nblk, r, c = a.shape

    def body(a_ref, o_ref):
        acc = a_ref[0].astype(F32)
        for i in range(1, nblk):
            acc = acc + a_ref[i].astype(F32)
        o_ref[...] = acc

    return pl.pallas_call(
        body, name=name, grid=(c // tc,),
        in_specs=[pl.BlockSpec((nblk, r, tc), lambda j: (0, 0, j))],
        out_specs=pl.BlockSpec((r, tc), lambda j: (0, j)),
        out_shape=jax.ShapeDtypeStruct((r, c), F32),
        compiler_params=pltpu.CompilerParams(dimension_semantics=("parallel",), vmem_limit_bytes=VMEM_LIMIT),
    )(a)


def _adamw(w, g, m, v, *, name, tr):
    r, c = w.shape

    def body(w_ref, g_ref, m_ref, v_ref, d_ref, mo_ref, vo_ref):
        gv = g_ref[...]
        mn = ADAM_B1 * m_ref[...] + (1.0 - ADAM_B1) * gv
        vn = ADAM_B2 * v_ref[...] + (1.0 - ADAM_B2) * jnp.square(gv)
        m_hat = mn / (1.0 - ADAM_B1 ** ADAM_STEP)
        v_hat = vn / (1.0 - ADAM_B2 ** ADAM_STEP)
        d_ref[...] = -ADAM_LR * (m_hat / (jnp.sqrt(v_hat) + ADAM_EPS) + ADAM_WD * w_ref[...])
        mo_ref[...] = mn
        vo_ref[...] = vn

    spec = pl.BlockSpec((tr, c), lambda i: (i, 0))
    return pl.pallas_call(
        body, name=name, grid=(r // tr,), in_specs=[spec] * 4, out_specs=[spec] * 3,
        out_shape=[jax.ShapeDtypeStruct((r, c), F32)] * 3,
        compiler_params=pltpu.CompilerParams(dimension_semantics=("parallel",), vmem_limit_bytes=VMEM_LIMIT),
    )(w, g, m, v)


def _in_proj_layout():
    z0, xbc0, dt0, gate0 = 8192, 12288, 18432, 18496
    hg = []
    for h in range(HG_HEADS):
        for part in range(4):
            hg.append(part * 2048 + h * HG_DK + np.arange(HG_DK))
    ssm = []
    for g in range(SSM_GROUPS):
        ssm.append(xbc0 + g * SSM_GW + np.arange(SSM_GW))
        ssm.append(xbc0 + SSM_DINNER + g * SSM_DSTATE + np.arange(SSM_DSTATE))
        ssm.append(xbc0 + SSM_DINNER + SSM_GROUPS * SSM_DSTATE + g * SSM_DSTATE + np.arange(SSM_DSTATE))
        ssm.append(np.concatenate([dt0 + g * SSM_HPG + np.arange(SSM_HPG), -np.ones(128 - SSM_HPG, np.int64)]))
        ssm.append(z0 + g * SSM_GW + np.arange(SSM_GW))
    gate = gate0 + np.arange(2 * D_MODEL)
    return np.concatenate(hg), np.concatenate(ssm), gate


def _conv_layout():
    idx = []
    for g in range(SSM_GROUPS):
        idx.append(np.concatenate([g * SSM_GW + np.arange(SSM_GW),
                                   SSM_DINNER + g * SSM_DSTATE + np.arange(SSM_DSTATE),
                                   SSM_DINNER + SSM_GROUPS * SSM_DSTATE + g * SSM_DSTATE + np.arange(SSM_DSTATE)]))
    return np.stack(idx)


def _up_layout():
    idx = []
    for g in range(FFN_G):
        idx.append(g * FFN_GW + np.arange(FFN_GW))
        idx.append(D_FF + g * FFN_GW + np.arange(FFN_GW))
    return np.concatenate(idx)


def _inverse(idx, n):
    inv = np.zeros(n, np.int64)
    pos = np.nonzero(idx >= 0)[0]
    inv[idx[pos]] = pos
    return inv


def _take_rows(a, idx, axis=0):
    idx = np.asarray(idx).reshape(-1)
    pieces, start = [], 0
    for i in range(1, len(idx) + 1):
        same_run = i < len(idx) and ((idx[i] == idx[i - 1] + 1 and idx[i - 1] >= 0) or (idx[i] < 0 and idx[i - 1] < 0))
        if same_run:
            continue
        n = i - start
        if idx[start] < 0:
            shape = list(a.shape)
            shape[axis] = n
            pieces.append(jnp.zeros(shape, a.dtype))
        else:
            pieces.append(lax.slice_in_dim(a, int(idx[start]), int(idx[start]) + n, axis=axis))
        start = i
    return pieces[0] if len(pieces) == 1 else jnp.concatenate(pieces, axis=axis)


_SMALL = (("mix_pre_norm", (1, 2048)), ("mix_post_norm", (1, 2048)), ("hg_lb_table", (2, 2048)), ("hg_out_norm", (1, 128)),
          ("ssm_conv_w", (4, 6144)), ("ssm_conv_b", (1, 6144)), ("ssm_dt_bias", (1, 64)), ("ssm_A_log", (1, 64)),
          ("ssm_D", (1, 64)), ("ssm_out_norm", (1, 4096)), ("ffn_pre_norm", (1, 2048)), ("ffn_post_norm", (1, 2048)),
          ("ffn_conv_w", (3, 5632)), ("ffn_conv_b", (1, 5632)), ("loss", (1, 1)))
_PACK_ROWS = 8 * (-(-sum(int(np.prod(s)) for _, s in _SMALL) // 1024))


def _pack(vals):
    flat = jnp.concatenate([vals[k].astype(F32).reshape(-1) for k, _ in _SMALL])
    return jnp.pad(flat, (0, _PACK_ROWS * 128 - flat.shape[0])).reshape(_PACK_ROWS, 128)


def _unpack(packed):
    flat, out, o = packed.reshape(-1), {}, 0
    for k, s in _SMALL:
        n = int(np.prod(s))
        out[k] = flat[o:o + n].reshape(s)
        o += n
    return out


def _local_step(x, target, w, p, late_weights=None, emit=lambda key, gw: None):
    t = x.shape[0]
    one = lambda a: a.reshape((1,) + a.shape)
    row = dict(rows=t, groups=1, consts=[], carries=[])

    (h1,), _ = _stage_fwd(_pre_step, name="pre_fwd", chunk=512, nc=1, xs=[(x, D_MODEL)], params=[one(p["mix_pre_norm"])],
                          ys=[(D_MODEL, BF16)], **row)
    proj_hg = _matmul_nt(h1, w["in_hg"], name="proj_hg", dep=w.get("issued_before"))
    proj_ssm = _matmul_nt(h1, w["in_ssm"], name="proj_ssm")
    proj_gate = _matmul_nt(h1, w["in_gate"], name="proj_gate")

    hg = dict(rows=t, chunk=HG_CHUNK, nc=8, groups=HG_HEADS, xs=[(proj_hg, HG_BLK)], params=[p["hg_tab"], p["hg_nw"]],
              consts=_hg_consts(), carries=[(HG_DK, HG_DK)], gpb=4)
    (y_hg,), hg_saved = _stage_fwd(_hg_step, name="hg_fwd", ys=[(HG_DK, BF16)], **hg)

    ssd = dict(rows=t, chunk=SSM_CHUNK, nc=4, groups=SSM_GROUPS, xs=[(proj_ssm, SSM_BLK)],
               params=[p["conv_w"], p["conv_b"], p["dt_bias"], p["a_log"], p["d_skip"], p["ssm_nw"]],
               consts=_ssd_consts(), carries=[(4 * 128, SSM_DSTATE), (HALO, SSM_XBC)])
    (y_ssm,), ssd_saved = _stage_fwd(_ssd_step, name="ssd_fwd", ys=[(SSM_GW, BF16)], **ssd)

    if late_weights is not None:
        w = {**w, **late_weights(y_ssm)}
    u_hg = _matmul_nn(y_hg, w["branch_hg"], name="branch_hg")
    u_ssm = _matmul_nn(y_ssm, w["branch_ssm"], name="branch_ssm")
    mix = dict(chunk=256, nc=1, xs=[(proj_gate, 2 * D_MODEL), (u_hg, D_MODEL), (u_ssm, D_MODEL)], params=[], **row)
    (mixed,), _ = _stage_fwd(_mix_step, name="mix_fwd", ys=[(D_MODEL, BF16)], **mix)
    v = _matmul_nn(mixed, w["out"], name="out_proj")
    post = dict(chunk=256, nc=1, xs=[(x, D_MODEL), (v, D_MODEL)],
                params=[one(p["mix_post_norm"]), one(p["ffn_pre_norm"])], **row)
    (x1, h2), _ = _stage_fwd(_post_step, name="post_fwd", ys=[(D_MODEL, F32), (D_MODEL, BF16)], **post)
    gu = _matmul_nt(h2, w["up"], name="ffn_up")
    ffn = dict(rows=t, chunk=256, nc=2, groups=FFN_G, xs=[(gu, 2 * FFN_GW)], params=[p["ffn_conv_w"], p["ffn_conv_b"]],
               consts=[], carries=[(HALO, FFN_GW)])
    (act,), ffn_saved = _stage_fwd(_ffn_step, name="ffn_fwd", ys=[(FFN_GW, BF16)], **ffn)
    d = _matmul_nn(act, w["down"], name="ffn_down")

    def head_step(carry, xv, civ, pv, cv):
        x1_, d_, tgt = xv

        def per_row_loss(a, b, nw):
            e = a + _rms(b, nw) - tgt
            return 0.5 * jnp.mean(e * e, axis=1, keepdims=True)

        lrow, vjp = jax.vjp(per_row_loss, x1_, d_, pv[0])
        dx1_, dd_, dnw = vjp(jnp.ones_like(lrow))
        loss = jnp.broadcast_to(jnp.sum(lrow, axis=0, keepdims=True), (1, 128))
        return [], [dx1_, dd_], [], [dnw, loss]

    (dy, dd), _, (g_ffn_post, loss) = _scan_call(
        head_step, name="loss_head", chunk=256, nc=1, xs=[(x1, D_MODEL), (d, D_MODEL), (target, D_MODEL)],
        params=[one(p["ffn_post_norm"])], ys=[(D_MODEL, F32), (D_MODEL, BF16)], accs=[(1, D_MODEL), (1, 128)], **row)

    gw = {}
    gw["down"] = _matmul_tn(act, dd, name="g_down")
    dact = _matmul_nt(dd, w["down"], name="d_act", dep=emit("down", gw))
    (dgu,), (g_fcw, g_fcb) = _stage_bwd(_ffn_step, name="ffn_bwd", saved=ffn_saved, dys=[(dact, FFN_GW)], dxs=[BF16], **ffn)
    gw["up"] = _matmul_tn(dgu, h2, name="g_up")
    dh2 = _matmul_nn(dgu, w["up"], name="d_h2", dep=emit("up", gw))
    (dx1, dv), (g_mix_post, g_ffn_pre) = _stage_bwd(_post_step, name="post_bwd", saved=[], dys=[(dy, D_MODEL), (dh2, D_MODEL)],
                                                    dxs=[F32, BF16], **post)
    gw["out"] = _matmul_tn(mixed, dv, name="g_out")
    dmixed = _matmul_nt(dv, w["out"], name="d_mixed")
    (dgate, du_hg, du_ssm), _ = _stage_bwd(_mix_step, name="mix_bwd", saved=[], dys=[(dmixed, D_MODEL)],
                                           dxs=[BF16, BF16, BF16], **mix)
    gw["in_gate"] = _matmul_tn(dgate, h1, name="g_in_gate")
    gw["branch_hg"] = _matmul_tn(y_hg, du_hg, name="g_branch_hg")
    gw["branch_ssm"] = _matmul_tn(y_ssm, du_ssm, name="g_branch_ssm")
    dy_hg = _matmul_nt(du_hg, w["branch_hg"], name="d_y_hg", dep=emit("branches", gw))
    dy_ssm = _matmul_nt(du_ssm, w["branch_ssm"], name="d_y_ssm")
    (dproj_ssm,), g_ssd = _stage_bwd(_ssd_step, name="ssd_bwd", saved=ssd_saved, dys=[(dy_ssm, SSM_GW)], dxs=[BF16], **ssd)
    gw["in_ssm"] = _matmul_tn(dproj_ssm, h1, name="g_in_ssm")
    (dproj_hg,), (g_tab, g_hg_nw) = _stage_bwd(_hg_step, name="hg_bwd", saved=hg_saved, dys=[(dy_hg, HG_DK)], dxs=[BF16], **hg)
    gw["in_hg"] = _matmul_tn(dproj_hg, h1, name="g_in_hg")
    dh_a = _matmul_nn(dproj_hg, w["in_hg"], name="d_h1_hg", dep=emit("in", gw))
    dh_b = _matmul_nn(dproj_ssm, w["in_ssm"], name="d_h1_ssm")
    dh_c = _matmul_nn(dgate, w["in_gate"], name="d_h1_gate")

    def pre_bwd_step(carry, xv, civ, pv, cv):
        x_, da, db, dc, dres = xv
        _, vjp = jax.vjp(_rms, x_, pv[0])
        dx_, dnw = vjp(da + db + dc)
        return [], [dx_ + dres], [], [dnw]

    (grad_x,), _, (g_mix_pre,) = _scan_call(
        pre_bwd_step, name="pre_bwd", chunk=256, nc=1,
        xs=[(x, D_MODEL), (dh_a, D_MODEL), (dh_b, D_MODEL), (dh_c, D_MODEL), (dx1, D_MODEL)],
        params=[one(p["mix_pre_norm"])], ys=[(D_MODEL, F32)], accs=[(1, D_MODEL)], **row)

    gp = dict(mix_pre_norm=g_mix_pre[0], mix_post_norm=g_mix_post[0], ffn_pre_norm=g_ffn_pre[0], ffn_post_norm=g_ffn_post[0],
              hg_tab=g_tab, hg_nw=g_hg_nw, conv_w=g_ssd[0], conv_b=g_ssd[1], dt_bias=g_ssd[2], a_log=g_ssd[3],
              d_skip=g_ssd[4], ssm_nw=g_ssd[5], ffn_conv_w=g_fcw, ffn_conv_b=g_fcb, loss=loss[0, :, :1])
    return grad_x, gw, gp


def _small_to_kernel_layout(s):
    conv_idx = _conv_layout()
    pad_heads = lambda a: jnp.pad(a.reshape(SSM_GROUPS, 1, SSM_HPG), ((0, 0), (0, 0), (0, 128 - SSM_HPG)))
    return dict(
        mix_pre_norm=s["mix_pre_norm"], mix_post_norm=s["mix_post_norm"], ffn_pre_norm=s["ffn_pre_norm"],
        ffn_post_norm=s["ffn_post_norm"],
        hg_tab=s["hg_lb_table"].reshape(2, HG_HEADS, HG_DK).transpose(1, 0, 2),
        hg_nw=jnp.broadcast_to(s["hg_out_norm"].reshape(1, 1, HG_DK), (HG_HEADS, 1, HG_DK)),
        conv_w=_take_rows(s["ssm_conv_w"], conv_idx, axis=1).reshape(SSM_CONV, SSM_GROUPS, SSM_XBC).transpose(1, 0, 2),
        conv_b=_take_rows(s["ssm_conv_b"], conv_idx, axis=1).reshape(SSM_GROUPS, 1, SSM_XBC),
        dt_bias=pad_heads(s["ssm_dt_bias"]), a_log=pad_heads(s["ssm_A_log"]),
        d_skip=jnp.repeat(s["ssm_D"].reshape(SSM_HEADS), SSM_HEADDIM).reshape(SSM_GROUPS, 1, SSM_GW),
        ssm_nw=s["ssm_out_norm"].reshape(SSM_GROUPS, 1, SSM_GW),
        ffn_conv_w=s["ffn_conv_w"].reshape(FFN_CONV, FFN_G, FFN_GW).transpose(1, 0, 2),
        ffn_conv_b=s["ffn_conv_b"].reshape(FFN_G, 1, FFN_GW),
    )


def _small_from_kernel_layout(g):
    conv_inv = _inverse(_conv_layout().reshape(-1), SSM_CONV_DIM)
    heads = lambda a: a[:, 0, :SSM_HPG].reshape(1, SSM_HEADS)
    return dict(
        mix_pre_norm=g["mix_pre_norm"], mix_post_norm=g["mix_post_norm"], ffn_pre_norm=g["ffn_pre_norm"],
        ffn_post_norm=g["ffn_post_norm"],
        hg_lb_table=g["hg_tab"].transpose(1, 0, 2).reshape(2, HG_HEADS * HG_DK),
        hg_out_norm=jnp.sum(g["hg_nw"], axis=0),
        ssm_conv_w=_take_rows(g["conv_w"].transpose(1, 0, 2).reshape(SSM_CONV, -1), conv_inv, axis=1),
        ssm_conv_b=_take_rows(g["conv_b"].reshape(1, -1), conv_inv, axis=1),
        ssm_dt_bias=heads(g["dt_bias"]), ssm_A_log=heads(g["a_log"]),
        ssm_D=jnp.sum(g["d_skip"].reshape(SSM_HEADS, SSM_HEADDIM), axis=1).reshape(1, SSM_HEADS),
        ssm_out_norm=g["ssm_nw"].reshape(1, SSM_DINNER),
        ffn_conv_w=g["ffn_conv_w"].transpose(1, 0, 2).reshape(FFN_CONV, D_FF),
        ffn_conv_b=g["ffn_conv_b"].reshape(1, D_FF),
        loss=g["loss"],
    )


def kernel(x, w_in, mix_pre_norm, mix_post_norm, hg_lb_table, hg_out_norm, ssm_conv_w, ssm_conv_b, ssm_dt_bias, ssm_A_log, ssm_D, ssm_out_norm, w_branch_hg, w_branch_ssm, w_out, ffn_pre_norm, ffn_post_norm, ffn_w_up, ffn_conv_w, ffn_conv_b, ffn_w_down, loss_target, m_w_in, m_mix_pre_norm, m_mix_post_norm, m_hg_lb_table, m_hg_out_norm, m_ssm_conv_w, m_ssm_conv_b, m_ssm_dt_bias, m_ssm_A_log, m_ssm_D, m_ssm_out_norm, m_w_branch_hg, m_w_branch_ssm, m_w_out, m_ffn_pre_norm, m_ffn_post_norm, m_ffn_w_up, m_ffn_conv_w, m_ffn_conv_b, m_ffn_w_down, v_w_in, v_mix_pre_norm, v_mix_post_norm, v_hg_lb_table, v_hg_out_norm, v_ssm_conv_w, v_ssm_conv_b, v_ssm_dt_bias, v_ssm_A_log, v_ssm_D, v_ssm_out_norm, v_w_branch_hg, v_w_branch_ssm, v_w_out, v_ffn_pre_norm, v_ffn_post_norm, v_ffn_w_up, v_ffn_conv_w, v_ffn_conv_b, v_ffn_w_down):
    names = ["w_in", "mix_pre_norm", "mix_post_norm", "hg_lb_table", "hg_out_norm", "ssm_conv_w", "ssm_conv_b", "ssm_dt_bias",
             "ssm_A_log", "ssm_D", "ssm_out_norm", "w_branch_hg", "w_branch_ssm", "w_out", "ffn_pre_norm", "ffn_post_norm",
             "ffn_w_up", "ffn_conv_w", "ffn_conv_b", "ffn_w_down"]
    ws = dict(zip(names, (w_in, mix_pre_norm, mix_post_norm, hg_lb_table, hg_out_norm, ssm_conv_w, ssm_conv_b, ssm_dt_bias,
                          ssm_A_log, ssm_D, ssm_out_norm, w_branch_hg, w_branch_ssm, w_out, ffn_pre_norm, ffn_post_norm,
                          ffn_w_up, ffn_conv_w, ffn_conv_b, ffn_w_down)))
    ms = dict(zip(names, (m_w_in, m_mix_pre_norm, m_mix_post_norm, m_hg_lb_table, m_hg_out_norm, m_ssm_conv_w, m_ssm_conv_b,
                          m_ssm_dt_bias, m_ssm_A_log, m_ssm_D, m_ssm_out_norm, m_w_branch_hg, m_w_branch_ssm, m_w_out,
                          m_ffn_pre_norm, m_ffn_post_norm, m_ffn_w_up, m_ffn_conv_w, m_ffn_conv_b, m_ffn_w_down)))
    vs = dict(zip(names, (v_w_in, v_mix_pre_norm, v_mix_post_norm, v_hg_lb_table, v_hg_out_norm, v_ssm_conv_w, v_ssm_conv_b,
                          v_ssm_dt_bias, v_ssm_A_log, v_ssm_D, v_ssm_out_norm, v_w_branch_hg, v_w_branch_ssm, v_w_out,
                          v_ffn_pre_norm, v_ffn_post_norm, v_ffn_w_up, v_ffn_conv_w, v_ffn_conv_b, v_ffn_w_down)))
    me = 4 * lax.axis_index("x") + 2 * lax.axis_index("y") + lax.axis_index("c")

    idx_hg, idx_ssm, idx_gate = _in_proj_layout()
    up_idx = _up_layout()
    late_shards = [ffn_w_up[0].T.astype(BF16), w_branch_hg[0].astype(BF16), w_branch_ssm[0].astype(BF16),
                   w_out[0].astype(BF16), ffn_w_down[0].astype(BF16)]
    landing = [lax.dynamic_update_slice_in_dim(lax.empty((N_DEV,) + s.shape, s.dtype), s[None], me, axis=0)
               for s in late_shards]
    gathered = _all_gather([w_in[0].T.astype(BF16), ssm_conv_w[0], ffn_conv_w[0]], name="gather_in_proj")
    late = _push_start([s[None] for s in late_shards], landing, name="late_weights_start", plan=_plan_own_block,
                       after=gathered[1])
    in_t = gathered[0].reshape(IN_TOTAL, D_MODEL)
    w = dict(in_hg=_take_rows(in_t, idx_hg), in_ssm=_take_rows(in_t, idx_ssm), in_gate=_take_rows(in_t, idx_gate),
             issued_before=late[4])
    small = {k: ws[k] for k, _ in _SMALL[:-1]}
    small["ssm_conv_w"] = gathered[1].transpose(1, 0, 2).reshape(SSM_CONV, SSM_CONV_DIM)
    small["ffn_conv_w"] = gathered[2].transpose(1, 0, 2).reshape(FFN_CONV, D_FF)
    small = {k: small[k].reshape(s) for k, s in _SMALL[:-1]}

    def late_weights(after):
        landed = _push_wait(late, after, name="late_weights_wait", plan=_plan_own_block)
        up_all, bhg, bssm, out, down = _push(landed, name="late_weights_pass_on", plan=_plan_pass_on)
        return dict(up=_take_rows(up_all.reshape(2 * D_FF, D_MODEL), up_idx), branch_hg=bhg.reshape(D_MODEL, D_MODEL),
                    branch_ssm=bssm.reshape(SSM_DINNER, D_MODEL), out=out.reshape(D_MODEL, D_MODEL),
                    down=down.reshape(D_FF, D_MODEL))

    in_flight = []

    def launch(key, named_parts):
        ks, parts = zip(*named_parts)
        from_sibling = _push(list(parts), name="grads_to_sibling_" + key, out_slots=N_CHIPS, plan=_plan_sibling)
        sums = [_pair_sum(p, r, name="pair_sum_" + k, tc=256) for k, p, r in zip(ks, parts, from_sibling)]
        handles = _push_start([q for q, _ in sums], [z for _, z in sums], name="grads_to_chips_start_" + key, plan=_plan_chips)
        in_flight.append((key, ks, handles))
        return handles[4]

    def emit(key, gw):
        blocks = lambda a: a.reshape(N_DEV, -1, D_MODEL)
        if key == "down":
            return launch(key, [("ffn_w_down", blocks(gw["down"]))])
        if key == "up":
            return launch(key, [("ffn_w_up", blocks(_take_rows(gw["up"], _inverse(up_idx, 2 * D_FF))))])
        if key == "branches":
            return launch(key, [("w_branch_hg", blocks(gw["branch_hg"])), ("w_branch_ssm", blocks(gw["branch_ssm"])),
                                ("w_out", blocks(gw["out"]))])
        in_all = jnp.concatenate([gw["in_hg"], gw["in_ssm"], gw["in_gate"]], axis=0)
        in_inv = _inverse(np.concatenate([idx_hg, idx_ssm, idx_gate]), IN_TOTAL)
        return launch(key, [("w_in", blocks(_take_rows(in_all, in_inv)))])

    grad_x, gw, gp = _local_step(x[0], loss_target[0], w, _small_to_kernel_layout(small), late_weights, emit)

    big_names = ["w_in", "ffn_w_up", "w_branch_hg", "w_branch_ssm", "w_out", "ffn_w_down"]
    grads = {}
    for key, ks, handles in in_flight:
        landed = _push_wait(handles, grad_x, name="grads_to_chips_wait_" + key, plan=_plan_chips)
        for k, r in zip(ks, landed):
            g = _sum_blocks(r, name="sum_" + k, tc=256)
            grads[k] = g.T if k in ("w_in", "ffn_w_up") else g
    small_all = _push([_pack(_small_from_kernel_layout(gp))[None]], name="small_to_everyone", out_slots=N_DEV,
                      plan=_plan_everyone)
    small_g = _unpack(_sum_blocks(small_all[0], name="sum_small", tc=128))
    loss = small_g.pop("loss").reshape(())
    for k, g in small_g.items():
        if k in ("ssm_conv_w", "ffn_conv_w"):
            n = g.shape[1] // N_DEV
            g = lax.dynamic_slice_in_dim(g, me * n, n, axis=1)
        grads[k] = g

    delta, new_m, new_v = {}, {}, {}
    for k in big_names:
        delta[k], new_m[k], new_v[k] = _adamw(ws[k][0], grads[k], ms[k][0], vs[k][0], name="adamw_" + k, tr=64)
    small_names = [k for k in names if k not in big_names]
    flat = lambda d: jnp.concatenate([d[k].astype(F32).reshape(-1) for k in small_names])
    n_small = sum(int(np.prod(ws[k].shape)) for k in small_names)
    rows = 8 * (-(-n_small // 1024))
    pack2 = lambda d: jnp.pad(flat(d), (0, rows * 128 - n_small)).reshape(rows, 128)
    v_packed = jnp.pad(flat(vs), (0, rows * 128 - n_small), constant_values=1.0).reshape(rows, 128)
    packed = _adamw(pack2(ws), pack2(grads), pack2(ms), v_packed, name="adamw_small", tr=rows)
    o = 0
    for k in small_names:
        n = int(np.prod(ws[k].shape))
        delta[k], new_m[k], new_v[k] = (a.reshape(-1)[o:o + n].reshape(ws[k].shape) for a in packed)
        o += n

    full = lambda d: [d[k].reshape(ws[k].shape) for k in names]
    return (loss, grad_x[None], *full(grads), *full(delta), *full(new_m), *full(new_v))
```

```python
import functools

import numpy as np
import jax
import jax.numpy as jnp
from jax import lax
from jax.experimental import pallas as pl
from jax.experimental.pallas import tpu as pltpu

F32, BF16 = jnp.float32, jnp.bfloat16

D_MODEL = 2048
EPS = 1e-6
HG_HEADS, HG_DK, HG_CHUNK = 16, 128, 64
HG_BLK = 4 * HG_DK
SSM_DINNER, SSM_HEADDIM, SSM_HEADS, SSM_GROUPS, SSM_DSTATE, SSM_CONV = 4096, 64, 64, 8, 128, 4
SSM_CHUNK = 128
SSM_GW = SSM_DINNER // SSM_GROUPS
SSM_HPG = SSM_HEADS // SSM_GROUPS
SSM_XBC = SSM_GW + 2 * SSM_DSTATE
SSM_BLK = SSM_XBC + 128 + SSM_GW
SSM_CONV_DIM = SSM_DINNER + 2 * SSM_GROUPS * SSM_DSTATE
D_FF, FFN_CONV = 5632, 3
FFN_GW = 512
FFN_G = D_FF // FFN_GW
IN_TOTAL = 22592
N_DEV = 8
HALO = 8
VMEM_LIMIT = 52 * 1024 * 1024
ADAM_LR, ADAM_B1, ADAM_B2, ADAM_EPS, ADAM_WD, ADAM_STEP = 0.001, 0.9, 0.999, 1e-08, 0.01, 10

_DIMS = {"nn": ((1,), (0,)), "nt": ((1,), (1,)), "tn": ((0,), (0,))}


def _mm_raw(a, b, mode):
    return lax.dot_general(a.astype(BF16), b.astype(BF16), (_DIMS[mode], ((), ())), preferred_element_type=F32)


@functools.partial(jax.custom_vjp, nondiff_argnums=(2,))
def _mm(a, b, mode):
    return _mm_raw(a, b, mode)


def _mm_fwd(a, b, mode):
    return _mm_raw(a, b, mode), (a, b)


def _mm_bwd(mode, res, dc):
    a, b = res
    if mode == "nn":
        return _mm_raw(dc, b, "nt"), _mm_raw(a, dc, "tn")
    if mode == "nt":
        return _mm_raw(dc, b, "nn"), _mm_raw(dc, a, "tn")
    return _mm_raw(b, dc, "nt"), _mm_raw(a, dc, "nn")


_mm.defvjp(_mm_fwd, _mm_bwd)


def _cmm_raw(m, x, mode):
    hi = x.astype(BF16)
    r1 = x - hi.astype(F32)
    mid = r1.astype(BF16)
    lo = (r1 - mid.astype(F32)).astype(BF16)
    dn = (_DIMS[mode], ((), ()))
    dot = lambda p: lax.dot_general(m, p, dn, preferred_element_type=F32)
    return dot(hi) + dot(mid) + dot(lo)


@jax.custom_vjp
def _cmm(m, x):
    return _cmm_raw(m, x, "nn")


def _cmm_fwd(m, x):
    return _cmm_raw(m, x, "nn"), m


def _cmm_bwd(m, dy):
    return jnp.zeros_like(m), _cmm_raw(m, dy, "tn")


_cmm.defvjp(_cmm_fwd, _cmm_bwd)


@functools.partial(jax.custom_vjp, nondiff_argnums=(1,))
def _sroll(x, s):
    return pltpu.roll(x, s, 0) if s else x


def _sroll_fwd(x, s):
    return _sroll(x, s), None


def _sroll_bwd(s, _, ct):
    return ((pltpu.roll(ct, ct.shape[0] - s, 0) if s else ct),)


_sroll.defvjp(_sroll_fwd, _sroll_bwd)


def _rms(x, w):
    return x * lax.rsqrt(jnp.mean(x * x, axis=-1, keepdims=True) + EPS) * w


def _softplus(x):
    return jnp.maximum(x, 0.0) + jnp.log(1.0 + jnp.exp(-jnp.abs(x)))


def _causal_conv(halo, x, w, b):
    k_taps = w.shape[0]
    xe = jnp.concatenate([halo, x], axis=0)
    out = b
    for k in range(k_taps):
        out = out + w[k:k + 1, :] * _sroll(xe, k_taps - 1 - k)[HALO:, :]
    return out


def _hg_consts():
    c = HG_CHUNK
    t = np.arange(c)
    blocks, pair = [], []
    for m in (32, 16, 8, 4, 2, 1):
        pos = t % (2 * m)
        late = pos >= m
        mid = t - pos + m
        j = t[None, :]
        mq = late[:, None] & (j >= mid[:, None]) & (j <= t[:, None])
        mk = (~late)[:, None] & (j > t[:, None]) & (j <= mid[:, None] - 1)
        blocks.append(mq | mk)
        parent = t // (2 * m)
        pair.append((parent[:, None] == parent[None, :]) & late[:, None] & (~late)[None, :])
    blocks.append(t[None, :] <= t[:, None])
    mall = jnp.asarray(np.concatenate(blocks, 0).astype(np.float32), BF16)
    pair = jnp.asarray(np.stack(pair, 0).astype(np.float32))
    eye = jnp.asarray(np.eye(c, dtype=np.float32))
    return [mall, pair, eye]


def _hg_step(carry, xs, params, consts):
    (st,) = carry
    blk = xs[0].astype(F32)
    tab, nw = params
    mall, pair, eye = consts
    c, dk = HG_CHUNK, HG_DK
    q_raw, f_raw, v, og = blk[:, :dk], blk[:, dk:2 * dk], blk[:, 2 * dk:3 * dk], blk[:, 3 * dk:]
    lb = jax.nn.sigmoid(tab[0:1, :] - tab[1:2, :])
    f = lb + (1.0 - lb) * jax.nn.sigmoid(f_raw)
    g = jnp.log(f)
    kk = 1.0 - f
    qh = jax.nn.silu(q_raw) * (HG_DK ** -0.5)
    yield
    sums = _cmm(mall, g)
    yield
    b = sums[6 * c:, :]
    fac = jnp.exp(sums[:6 * c, :])
    scores = eye * jnp.sum(qh * kk, axis=1, keepdims=True)
    b_last = jnp.sum(g, axis=0, keepdims=True)
    yield
    inter = _mm(qh * jnp.exp(b), st, "nt")
    st_new = st * jnp.exp(b_last) + _mm(v, kk * jnp.exp(b_last - b), "tn")
    yield
    for l in range(6):
        fl = fac[l * c:(l + 1) * c, :]
        scores = scores + pair[l] * _mm(qh * fl, kk * fl, "nt")
        if l % 2:
            yield
    o = _mm(scores, v, "nn") + inter
    yield
    y = _rms(o, nw) * jax.nn.silu(og)
    return [st_new], [y]


def _ssd_consts():
    t = np.arange(SSM_CHUNK)
    tril = (t[None, :] <= t[:, None]).astype(np.float32)
    return [jnp.asarray(tril, BF16), jnp.asarray(tril)]


def _ssd_step(carry, xs, params, consts):
    st, halo = carry
    blk = xs[0].astype(F32)
    conv_w, conv_b, dtb, alog, dskip, nw = params
    tril_b, tril = consts
    c = SSM_CHUNK
    raw, dtr, z = blk[:, :SSM_XBC], blk[:, SSM_XBC:SSM_XBC + 128], blk[:, SSM_XBC + 128:]
    act = jax.nn.silu(_causal_conv(halo, raw, conv_w, conv_b))
    xh, bm, cm = act[:, :SSM_GW], act[:, SSM_GW:SSM_GW + SSM_DSTATE], act[:, SSM_GW + SSM_DSTATE:]
    dt = _softplus(dtr + dtb)
    da = dt * (-jnp.exp(alog))
    acum = _cmm(tril_b, da)
    acum_t = acum.T
    a_last = jnp.sum(da, axis=0, keepdims=True)
    cb_causal = _mm(cm, bm, "nt") * tril
    lane = lax.broadcasted_iota(jnp.int32, (c, 128), 1)
    row = lax.broadcasted_iota(jnp.int32, (128, 128), 0)
    first = lane < SSM_HEADDIM
    ys, st_new = [], []
    for j in range(SSM_HPG // 2):
        xp = xh[:, 128 * j:128 * (j + 1)]
        sp = st[128 * j:128 * (j + 1), :]
        r0, r1 = 2 * j, 2 * j + 1
        col = lambda a, r: jnp.broadcast_to(a[:, r:r + 1], (c, 128))
        xdt = xp * jnp.where(first, col(dt, r0), col(dt, r1))
        yj = _mm(cm, sp, "nt") * jnp.exp(jnp.where(first, col(acum, r0), col(acum, r1)))
        for r, keep in ((r0, first), (r1, ~first)):
            dec = jnp.broadcast_to(acum[:, r:r + 1], (c, c)) - jnp.broadcast_to(acum_t[r:r + 1, :], (c, c))
            m = cb_causal * jnp.exp(jnp.minimum(dec, 0.0))
            yj = yj + _mm(m, jnp.where(keep, xdt, 0.0), "nn")
        al0, al1 = a_last[:, r0:r0 + 1], a_last[:, r1:r1 + 1]
        wts = jnp.exp(jnp.where(first, al0 - col(acum, r0), al1 - col(acum, r1)))
        st_new.append(jnp.where(row < SSM_HEADDIM, jnp.exp(al0), jnp.exp(al1)) * sp + _mm(xdt * wts, bm, "tn"))
        ys.append(yj)
    y = jnp.concatenate(ys, axis=1) + dskip * xh
    y = _rms(y * jax.nn.silu(z), nw)
    return [jnp.concatenate(st_new, axis=0), raw[c - HALO:, :]], [y]


def _ffn_step(carry, xs, params, consts):
    (halo,) = carry
    blk = xs[0].astype(F32)
    conv_w, conv_b = params
    gate, up = blk[:, :FFN_GW], blk[:, FFN_GW:]
    a = jax.nn.gelu(_causal_conv(halo, gate, conv_w, conv_b), approximate=True) * up
    return [gate[gate.shape[0] - HALO:, :]], [a]


def _pre_step(carry, xs, params, consts):
    return [], [_rms(xs[0], params[0])]


def _mix_step(carry, xs, params, consts):
    gates, uh, us = (a.astype(F32) for a in xs)
    return [], [jax.nn.sigmoid(gates[:, :D_MODEL]) * uh + jax.nn.sigmoid(gates[:, D_MODEL:]) * us]


def _post_step(carry, xs, params, consts):
    x, v = xs
    x1 = x + _rms(v, params[0])
    return [], [x1, _rms(x1, params[1])]


def _scan_call(step, *, name, rows, chunk, nc, groups, xs, cins=(), params=(), consts=(), carries=(), ys=(), couts=(),
               accs=(), reverse=False, gpb=1, multi=False):
    blk_rows = chunk * nc
    nb = rows // blk_rows
    n_chunks = rows // chunk
    assert nb * blk_rows == rows and groups % gpb == 0
    rb = (lambda i: nb - 1 - i) if reverse else (lambda i: i)
    n_x, n_ci, n_p, n_c = len(xs), len(cins), len(params), len(consts)
    n_y, n_co, n_a = len(ys), len(couts), len(accs)

    def chunk_spec(shape):
        zeros = (0,) * len(shape)
        return pl.BlockSpec((gpb, nc) + tuple(shape), lambda g, i: (g, rb(i)) + zeros)

    in_specs = [pl.BlockSpec((blk_rows, gpb * w), lambda g, i: (rb(i), g)) for _, w in xs]
    in_specs += [chunk_spec(a.shape[2:]) for a in cins]
    in_specs += [pl.BlockSpec((gpb,) + tuple(a.shape[1:]), lambda g, i: (g, 0, 0)) for a in params]
    in_specs += [pl.BlockSpec(a.shape, (lambda nd: lambda g, i: (0,) * nd)(a.ndim)) for a in consts]
    out_specs = [pl.BlockSpec((blk_rows, gpb * w), lambda g, i: (rb(i), g)) for w, _ in ys]
    out_specs += [chunk_spec(s) for s in couts]
    out_specs += [pl.BlockSpec((gpb, r, c), lambda g, i: (g, 0, 0)) for r, c in accs]
    out_shape = [jax.ShapeDtypeStruct((rows, groups * w), dt) for w, dt in ys]
    out_shape += [jax.ShapeDtypeStruct((groups, n_chunks) + tuple(s), F32) for s in couts]
    out_shape += [jax.ShapeDtypeStruct((groups, r, c), F32) for r, c in accs]
    x_widths = [w for _, w in xs]
    y_widths = [w for w, _ in ys]

    def body(*refs):
        x_refs = refs[:n_x]
        ci_refs = refs[n_x:n_x + n_ci]
        p_refs = refs[n_x + n_ci:n_x + n_ci + n_p]
        c_refs = refs[n_x + n_ci + n_p:n_x + n_ci + n_p + n_c]
        o = n_x + n_ci + n_p + n_c
        y_refs = refs[o:o + n_y]
        co_refs = refs[o + n_y:o + n_y + n_co]
        a_refs = refs[o + n_y + n_co:o + n_y + n_co + n_a]
        carry_refs = refs[o + n_y + n_co + n_a:]

        @pl.when(pl.program_id(1) == 0)
        def _():
            for s in carry_refs:
                s[...] = jnp.zeros(s.shape, F32)
            for a in a_refs:
                a[...] = jnp.zeros(a.shape, F32)

        cvals = [c[...] for c in c_refs]

        def one_chunk(i, _):
            c = (nc - 1 - i) if reverse else i
            r0 = c * chunk if isinstance(c, int) else pl.multiple_of(c * chunk, chunk)
            loaded = []
            for u in range(gpb):
                carry = [s[u] for s in carry_refs]
                xv = [x[pl.ds(r0, chunk), u * w:(u + 1) * w] for x, w in zip(x_refs, x_widths)]
                civ = [ci[u, c] for ci in ci_refs]
                loaded.append((carry, xv, civ, [p[u] for p in p_refs]))
            results = step(loaded, cvals) if multi else [step(*args, cvals) for args in loaded]
            for u, (new_carry, yv, cov, av) in enumerate(results):
                for s, val in zip(carry_refs, new_carry):
                    s[u] = val
                for y, w, val in zip(y_refs, y_widths, yv):
                    y[pl.ds(r0, chunk), u * w:(u + 1) * w] = val.astype(y.dtype)
                for co, val in zip(co_refs, cov):
                    co[u, c] = val
                for a, val in zip(a_refs, av):
                    a[u] += val
            return 0

        if nc == 1:
            one_chunk(0, 0)
        else:
            lax.fori_loop(0, nc, one_chunk, 0)

    outs = pl.pallas_call(
        body, name=name, grid=(groups // gpb, nb), in_specs=in_specs, out_specs=out_specs, out_shape=out_shape,
        scratch_shapes=[pltpu.VMEM((gpb,) + tuple(s), F32) for s in carries],
        compiler_params=pltpu.CompilerParams(dimension_semantics=("arbitrary", "arbitrary"),
                                             vmem_limit_bytes=VMEM_LIMIT),
    )(*[a for a, _ in xs], *cins, *params, *consts)
    return outs[:n_y], outs[n_y:n_y + n_co], outs[n_y + n_co:]


def _run_interleaved(step, arg_tuples):
    runs = [step(*args) for args in arg_tuples]
    if not hasattr(runs[0], "send"):
        return runs
    results, live = [None] * len(runs), list(range(len(runs)))
    while live:
        for u in list(live):
            try:
                next(runs[u])
            except StopIteration as done:
                results[u] = done.value
                live.remove(u)
    return results


def _stage_fwd(step, *, name, rows, chunk, nc, groups, xs, params, consts, carries, ys, gpb=1):
    def fstep(loaded, cv):
        outs = _run_interleaved(step, [(carry, xv, pv, cv) for carry, xv, _, pv in loaded])
        return [(new_carry, yv, carry, []) for (new_carry, yv), (carry, _, _, _) in zip(outs, loaded)]

    yv, saved, _ = _scan_call(fstep, name=name, rows=rows, chunk=chunk, nc=nc, groups=groups, xs=xs, params=params,
                              consts=consts, carries=carries, ys=ys, couts=carries, gpb=gpb, multi=True)
    return yv, saved


def _stage_bwd(step, *, name, rows, chunk, nc, groups, xs, saved, params, consts, carries, dys, dxs, gpb=1):
    n_x = len(xs)

    def bstep(loaded, cv):
        civs = [list(civ) for _, _, civ, _ in loaded]
        xvs = [list(xv_all[:n_x]) for _, xv_all, _, _ in loaded]
        pvs = [list(pv) for _, _, _, pv in loaded]
        cts = [(list(dcarry), [d.astype(F32) for d in xv_all[n_x:]]) for dcarry, xv_all, _, _ in loaded]

        def fwd(civs_, xvs_, pvs_):
            outs = _run_interleaved(step, [(c_, x_, p_, cv) for c_, x_, p_ in zip(civs_, xvs_, pvs_)])
            return [(list(new_carry), list(yv)) for new_carry, yv in outs]

        _, vjp = jax.vjp(fwd, civs, xvs, pvs)
        dcivs, dxvs, dpvs = vjp(cts)
        return [(dc, dx, [], dp) for dc, dx, dp in zip(dcivs, dxvs, dpvs)]

    dxv, _, dpv = _scan_call(bstep, name=name, rows=rows, chunk=chunk, nc=nc, groups=groups, xs=list(xs) + list(dys),
                             cins=saved, params=params, consts=consts, carries=carries,
                             ys=[(w, dt) for (_, w), dt in zip(xs, dxs)], accs=[a.shape[1:] for a in params],
                             reverse=True, gpb=gpb, multi=True)
    return dxv, dpv


def _mm_params(sem):
    return pltpu.CompilerParams(dimension_semantics=sem, vmem_limit_bytes=VMEM_LIMIT)


def _after(dep):
    return ([], []) if dep is None else ([dep], [pl.BlockSpec(memory_space=pl.ANY)])


def _matmul_nt(a, b, *, name, dep=None):
    m, k = a.shape
    n = b.shape[0]
    tm = min(1024, m)
    tn = 1024 if n % 1024 == 0 else 512
    deps, dep_specs = _after(dep)

    def body(a_ref, b_ref, *rest):
        rest[-1][...] = lax.dot_general(a_ref[...], b_ref[...], (_DIMS["nt"], ((), ())), preferred_element_type=F32)

    return pl.pallas_call(
        body, name=name, grid=(m // tm, n // tn),
        in_specs=[pl.BlockSpec((tm, k), lambda i, j: (i, 0)), pl.BlockSpec((tn, k), lambda i, j: (j, 0))] + dep_specs,
        out_specs=pl.BlockSpec((tm, tn), lambda i, j: (i, j)),
        out_shape=jax.ShapeDtypeStruct((m, n), F32),
        compiler_params=_mm_params(("parallel", "arbitrary")),
    )(a, b, *deps)


def _matmul_nn(a, b, *, name, dep=None):
    m, k = a.shape
    n = b.shape[1]
    tm, tn = (1024, 1024) if k <= 4096 else (1024, 512) if k <= 6144 else (512, 512) if k <= 8192 else (512, 256)
    tm = min(tm, m)
    deps, dep_specs = _after(dep)

    def body(a_ref, b_ref, *rest):
        rest[-1][...] = jnp.dot(a_ref[...], b_ref[...], preferred_element_type=F32)

    return pl.pallas_call(
        body, name=name, grid=(m // tm, n // tn),
        in_specs=[pl.BlockSpec((tm, k), lambda i, j: (i, 0)), pl.BlockSpec((k, tn), lambda i, j: (0, j))] + dep_specs,
        out_specs=pl.BlockSpec((tm, tn), lambda i, j: (i, j)),
        out_shape=jax.ShapeDtypeStruct((m, n), F32),
        compiler_params=_mm_params(("parallel", "arbitrary")),
    )(a, b, *deps)


def _matmul_tn(x, y, *, name, tp=512, tq=512):
    t, p = x.shape
    q = y.shape[1]

    def body(x_ref, y_ref, o_ref):
        o_ref[...] = lax.dot_general(x_ref[...], y_ref[...], (_DIMS["tn"], ((), ())),
                                     preferred_element_type=F32).astype(o_ref.dtype)

    return pl.pallas_call(
        body, name=name, grid=(p // tp, q // tq),
        in_specs=[pl.BlockSpec((t, tp), lambda i, j: (0, i)), pl.BlockSpec((t, tq), lambda i, j: (0, j))],
        out_specs=pl.BlockSpec((tp, tq), lambda i, j: (i, j)),
        out_shape=jax.ShapeDtypeStruct((p, q), BF16),
        compiler_params=_mm_params(("parallel", "arbitrary")),
    )(x, y)


N_CHIPS = N_DEV // 2


def _all_gather(arrays, *, name):
    n = len(arrays)
    out_shape = [jax.ShapeDtypeStruct((N_DEV,) + tuple(a.shape), a.dtype) for a in arrays]

    def body(*refs):
        in_refs, out_refs = refs[:n], refs[n:2 * n]
        send_sems, recv_sems, local_sems = refs[2 * n:]
        x, y, c = lax.axis_index("x"), lax.axis_index("y"), lax.axis_index("c")
        me, sibling = (x, y, c), (x, y, 1 - c)
        chips = [(1 - x, y), (x, 1 - y), (1 - x, 1 - y)]

        def copy(a, k, block, to, src=None):
            slot = out_refs[a].at[4 * block[0] + 2 * block[1] + block[2]]
            return pltpu.make_async_remote_copy(
                src_ref=slot if src is None else src, dst_ref=slot, send_sem=send_sems.at[a, k],
                recv_sem=recv_sems.at[a, k], device_id=to, device_id_type=pl.DeviceIdType.MESH)

        mine = [pltpu.make_async_copy(in_refs[a], out_refs[a].at[4 * x + 2 * y + c], local_sems.at[a]) for a in range(n)]
        first = []
        for a in range(n):
            first.append(copy(a, 0, me, sibling, src=in_refs[a]))
            first += [copy(a, 1 + j, me, (*chip, c), src=in_refs[a]) for j, chip in enumerate(chips)]
        for cp in mine + first:
            cp.start()
        passed = []
        for j, chip in enumerate(chips):
            for a in range(n):
                copy(a, 1 + j, (*chip, c), me).wait_recv()
                passed.append(copy(a, 4 + j, (*chip, c), sibling))
                passed[-1].start()
        for a in range(n):
            copy(a, 0, sibling, me).wait_recv()
            for j, chip in enumerate(chips):
                copy(a, 4 + j, (*chip, 1 - c), me).wait_recv()
        for cp in first + passed:
            cp.wait_send()
        for cp in mine:
            cp.wait()

    any_spec = pl.BlockSpec(memory_space=pl.ANY)
    return pl.pallas_call(
        body, name=name, in_specs=[any_spec] * n, out_specs=[any_spec] * n, out_shape=out_shape,
        scratch_shapes=[pltpu.SemaphoreType.DMA((n, N_DEV - 1)), pltpu.SemaphoreType.DMA((n, N_DEV - 1)),
                        pltpu.SemaphoreType.DMA((n,))],
        compiler_params=pltpu.CompilerParams(has_side_effects=True),
    )(*arrays)


def _push(arrays, *, name, plan, out_slots=None):
    n = len(arrays)
    in_place = out_slots is None
    out_shape = [jax.ShapeDtypeStruct(((a.shape[0] if in_place else out_slots),) + tuple(a.shape[1:]), a.dtype) for a in arrays]
    n_tr = len(plan(0, 0, 0)[0])

    def body(*refs):
        in_refs, out_refs = refs[:n], refs[n:2 * n]
        send_sems, recv_sems, local_sems = refs[2 * n:]
        src_refs = out_refs if in_place else in_refs
        transfers, local = plan(lax.axis_index("x"), lax.axis_index("y"), lax.axis_index("c"))
        copies = []
        for a in range(n):
            if local is not None:
                copies.append(pltpu.make_async_copy(src_refs[a].at[local[0]], out_refs[a].at[local[1]], local_sems.at[a]))
            for k, (peer, src, dst) in enumerate(transfers):
                copies.append(pltpu.make_async_remote_copy(
                    src_ref=src_refs[a].at[src], dst_ref=out_refs[a].at[dst], send_sem=send_sems.at[a, k],
                    recv_sem=recv_sems.at[a, k], device_id=peer, device_id_type=pl.DeviceIdType.MESH))
        for cp in copies:
            cp.start()
        for cp in copies:
            cp.wait()

    any_spec = pl.BlockSpec(memory_space=pl.ANY)
    return pl.pallas_call(
        body, name=name, in_specs=[any_spec] * n, out_specs=[any_spec] * n, out_shape=out_shape,
        input_output_aliases={a: a for a in range(n)} if in_place else {},
        scratch_shapes=[pltpu.SemaphoreType.DMA((n, n_tr)), pltpu.SemaphoreType.DMA((n, n_tr)),
                        pltpu.SemaphoreType.DMA((n,))],
        compiler_params=pltpu.CompilerParams(has_side_effects=True),
    )(*arrays)


_HBM_SPEC = pl.BlockSpec(memory_space=pltpu.HBM)
_SEM_SPEC = pl.BlockSpec(memory_space=pltpu.SEMAPHORE)
_DATAFLOW = pltpu.SideEffectType.DATAFLOW_SIDE_EFFECTING


def _push_start(sources, landing, *, name, plan, after=None):
    n = len(sources)
    n_tr = len(plan(0, 0, 0)[0])
    deps, dep_specs = _after(after)

    def body(*refs):
        src_refs, land_refs = refs[:n], refs[n:2 * n]
        o = 2 * n + len(deps)
        send_sems, recv_sems, token = refs[o], refs[o + 1], refs[-1]
        transfers, _ = plan(lax.axis_index("x"), lax.axis_index("y"), lax.axis_index("c"))
        for a in range(n):
            for k, (peer, src, dst) in enumerate(transfers):
                pltpu.make_async_remote_copy(
                    src_ref=src_refs[a].at[src], dst_ref=land_refs[a].at[dst], send_sem=send_sems.at[a * n_tr + k],
                    recv_sem=recv_sems.at[a * n_tr + k], device_id=peer, device_id_type=pl.DeviceIdType.MESH).start()
        token[...] = jnp.zeros(token.shape, token.dtype)

    hbm = lambda a: pltpu.HBM(a.shape, a.dtype)
    outs = pl.pallas_call(
        body, name=name,
        out_shape=(pltpu.SemaphoreType.DMA((n * n_tr,)), pltpu.SemaphoreType.DMA((n * n_tr,)), *[hbm(a) for a in sources],
                   *[hbm(a) for a in landing], jax.ShapeDtypeStruct((8, 128), F32)),
        in_specs=[_HBM_SPEC] * (2 * n) + dep_specs,
        out_specs=(_SEM_SPEC, _SEM_SPEC, *[_HBM_SPEC] * (2 * n), pl.BlockSpec(memory_space=pltpu.VMEM)),
        input_output_aliases={i: 2 + i for i in range(2 * n)},
        compiler_params=pltpu.CompilerParams(has_side_effects=_DATAFLOW),
    )(*[pltpu.with_memory_space_constraint(a, pltpu.HBM) for a in list(sources) + list(landing)], *deps)
    return outs[0], outs[1], list(outs[2:2 + n]), list(outs[2 + n:2 + 2 * n]), outs[-1]


def _push_wait(handles, after, *, name, plan):
    send_sems, recv_sems, sources, landing, _ = handles
    n = len(sources)
    after = list(after) if isinstance(after, (list, tuple)) else [after]

    def body(*refs):
        src_refs, land_refs = refs[:n], refs[n:2 * n]
        send_sems_, recv_sems_ = refs[2 * n], refs[2 * n + 1]
        transfers, _ = plan(lax.axis_index("x"), lax.axis_index("y"), lax.axis_index("c"))
        n_tr = len(transfers)
        for a in range(n):
            for k, (peer, src, dst) in enumerate(transfers):
                cp = pltpu.make_async_remote_copy(
                    src_ref=src_refs[a].at[src], dst_ref=land_refs[a].at[dst], send_sem=send_sems_.at[a * n_tr + k],
                    recv_sem=recv_sems_.at[a * n_tr + k], device_id=peer, device_id_type=pl.DeviceIdType.MESH)
                cp.wait_send()
                cp.wait_recv()

    hbm = lambda a: pltpu.HBM(a.shape, a.dtype)
    outs = pl.pallas_call(
        body, name=name, out_shape=tuple(hbm(a) for a in list(sources) + list(landing)),
        in_specs=[_HBM_SPEC] * (2 * n) + [_SEM_SPEC, _SEM_SPEC] + [pl.BlockSpec(memory_space=pl.ANY)] * len(after),
        out_specs=[_HBM_SPEC] * (2 * n), input_output_aliases={i: i for i in range(2 * n)},
        compiler_params=pltpu.CompilerParams(has_side_effects=_DATAFLOW),
    )(*sources, *landing, send_sems, recv_sems, *after)
    return list(outs[n:])


def _plan_everyone(x, y, c):
    me = 4 * x + 2 * y + c
    peers = [(1 - x if k & 4 else x, 1 - y if k & 2 else y, 1 - c if k & 1 else c) for k in range(1, N_DEV)]
    return [(p, 0, me) for p in peers], (0, me)


def _plan_sibling(x, y, c):
    return [((x, y, 1 - c), 2 * chip + (1 - c), chip) for chip in range(N_CHIPS)], None


def _plan_chips(x, y, c):
    mine = 2 * x + y
    peers = [(1 - x, y), (x, 1 - y), (1 - x, 1 - y)]
    return [((px, py, c), 2 * px + py, mine) for px, py in peers], (mine, mine)


def _plan_own_block(x, y, c):
    me = 4 * x + 2 * y + c
    peers = [(x, y, 1 - c), (1 - x, y, c), (x, 1 - y, c), (1 - x, 1 - y, c)]
    return [(p, 0, me) for p in peers], None


def _plan_pass_on(x, y, c):
    slots = [4 * px + 2 * py + c for px, py in ((1 - x, y), (x, 1 - y), (1 - x, 1 - y))]
    return [((x, y, 1 - c), s, s) for s in slots], None


def _pair_sum(parts, received, *, name, tc):
    _, r, c = parts.shape
    core = lax.axis_index("c").astype(jnp.int32).reshape(1)

    def body(core_ref, p_ref, r_ref, o_ref, o2_ref):
        s = (p_ref[...].astype(F32) + r_ref[...].astype(F32)).astype(o_ref.dtype)
        o_ref[...] = s
        o2_ref[...] = s

    out = pl.BlockSpec((None, r, tc), lambda i, j, core_ref: (i, 0, j))
    return pl.pallas_call(
        body, name=name,
        grid_spec=pltpu.PrefetchScalarGridSpec(
            num_scalar_prefetch=1, grid=(N_CHIPS, c // tc),
            in_specs=[pl.BlockSpec((None, r, tc), lambda i, j, core_ref: (2 * i + core_ref[0], 0, j)),
                      pl.BlockSpec((None, r, tc), lambda i, j, core_ref: (i, 0, j))],
            out_specs=[out, out]),
        out_shape=[jax.ShapeDtypeStruct((N_CHIPS, r, c), BF16)] * 2,
        compiler_params=pltpu.CompilerParams(dimension_semantics=("parallel", "parallel"), vmem_limit_bytes=VMEM_LIMIT),
    )(core, parts, received)


def _sum_blocks(a, *, name, tc):
    nblk, r, c = a.shape

    def body(a_ref, o_ref):
        acc = a_ref[0].astype(F32)
        for i in range(1, nblk):
            acc = acc + a_ref[i].astype(F32)
        o_ref[...] = acc

    return pl.pallas_call(
        body, name=name, grid=(c // tc,),
        in_specs=[pl.BlockSpec((nblk, r, tc), lambda j: (0, 0, j))],
        out_specs=pl.BlockSpec((r, tc), lambda j: (0, j)),
        out_shape=jax.ShapeDtypeStruct((r, c), F32),
        compiler_params=pltpu.CompilerParams(dimension_semantics=("parallel",), vmem_limit_bytes=VMEM_LIMIT),
    )(a)


def _adamw(w, g, m, v, *, name, tr):
    r, c = w.shape

    def body(w_ref, g_ref, m_ref, v_ref, d_ref, mo_ref, vo_ref):
        gv = g_ref[...]
        mn = ADAM_B1 * m_ref[...] + (1.0 - ADAM_B1) * gv
        vn = ADAM_B2 * v_ref[...] + (1.0 - ADAM_B2) * jnp.square(gv)
        m_hat = mn / (1.0 - ADAM_B1 ** ADAM_STEP)
        v_hat = vn / (1.0 - ADAM_B2 ** ADAM_STEP)
        d_ref[...] = -ADAM_LR * (m_hat / (jnp.sqrt(v_hat) + ADAM_EPS) + ADAM_WD * w_ref[...])
        mo_ref[...] = mn
        vo_ref[...] = vn

    spec = pl.BlockSpec((tr, c), lambda i: (i, 0))
    return pl.pallas_call(
        body, name=name, grid=(r // tr,), in_specs=[spec] * 4, out_specs=[spec] * 3,
        out_shape=[jax.ShapeDtypeStruct((r, c), F32)] * 3,
        compiler_params=pltpu.CompilerParams(dimension_semantics=("parallel",), vmem_limit_bytes=VMEM_LIMIT),
    )(w, g, m, v)


def _in_proj_layout():
    z0, xbc0, dt0, gate0 = 8192, 12288, 18432, 18496
    hg = []
    for h in range(HG_HEADS):
        for part in range(4):
            hg.append(part * 2048 + h * HG_DK + np.arange(HG_DK))
    ssm = []
    for g in range(SSM_GROUPS):
        ssm.append(xbc0 + g * SSM_GW + np.arange(SSM_GW))
        ssm.append(xbc0 + SSM_DINNER + g * SSM_DSTATE + np.arange(SSM_DSTATE))
        ssm.append(xbc0 + SSM_DINNER + SSM_GROUPS * SSM_DSTATE + g * SSM_DSTATE + np.arange(SSM_DSTATE))
        ssm.append(np.concatenate([dt0 + g * SSM_HPG + np.arange(SSM_HPG), -np.ones(128 - SSM_HPG, np.int64)]))
        ssm.append(z0 + g * SSM_GW + np.arange(SSM_GW))
    gate = gate0 + np.arange(2 * D_MODEL)
    return np.concatenate(hg), np.concatenate(ssm), gate


def _conv_layout():
    idx = []
    for g in range(SSM_GROUPS):
        idx.append(np.concatenate([g * SSM_GW + np.arange(SSM_GW),
                                   SSM_DINNER + g * SSM_DSTATE + np.arange(SSM_DSTATE),
                                   SSM_DINNER + SSM_GROUPS * SSM_DSTATE + g * SSM_DSTATE + np.arange(SSM_DSTATE)]))
    return np.stack(idx)


def _up_layout():
    idx = []
    for g in range(FFN_G):
        idx.append(g * FFN_GW + np.arange(FFN_GW))
        idx.append(D_FF + g * FFN_GW + np.arange(FFN_GW))
    return np.concatenate(idx)


def _inverse(idx, n):
    inv = np.zeros(n, np.int64)
    pos = np.nonzero(idx >= 0)[0]
    inv[idx[pos]] = pos
    return inv


def _take_rows(a, idx, axis=0):
    idx = np.asarray(idx).reshape(-1)
    pieces, start = [], 0
    for i in range(1, len(idx) + 1):
        same_run = i < len(idx) and ((idx[i] == idx[i - 1] + 1 and idx[i - 1] >= 0) or (idx[i] < 0 and idx[i - 1] < 0))
        if same_run:
            continue
        n = i - start
        if idx[start] < 0:
            shape = list(a.shape)
            shape[axis] = n
            pieces.append(jnp.zeros(shape, a.dtype))
        else:
            pieces.append(lax.slice_in_dim(a, int(idx[start]), int(idx[start]) + n, axis=axis))
        start = i
    return pieces[0] if len(pieces) == 1 else jnp.concatenate(pieces, axis=axis)


def _copy_runs(sources, out_rows, runs, *, name):
    ns, d, dtype = len(sources), sources[0].shape[1], sources[0].dtype

    def body(*refs):
        srcs, outs, sems = refs[:ns], refs[ns:-1], refs[-1]
        copies = [pltpu.make_async_copy(srcs[i].at[pl.ds(s, n)], outs[o].at[pl.ds(t, n)], sems.at[k])
                  for k, (i, s, o, t, n) in enumerate(runs)]
        for cp in copies:
            cp.start()
        for cp in copies:
            cp.wait()

    any_spec = pl.BlockSpec(memory_space=pl.ANY)
    return pl.pallas_call(
        body, name=name, in_specs=[any_spec] * ns, out_specs=[any_spec] * len(out_rows),
        out_shape=[jax.ShapeDtypeStruct((r, d), dtype) for r in out_rows],
        scratch_shapes=[pltpu.SemaphoreType.DMA((len(runs),))],
    )(*sources)


_Z0, _XBC0, _DT0, _GATE0 = 8192, 12288, 18432, 18496
_B0, _C0 = _XBC0 + SSM_DINNER, _XBC0 + SSM_DINNER + SSM_GROUPS * SSM_DSTATE


def _in_proj_runs():
    runs = [(part * 2048 + h * HG_DK, 0, h * HG_BLK + part * HG_DK, HG_DK) for h in range(HG_HEADS) for part in range(4)]
    for g in range(SSM_GROUPS):
        base = g * SSM_BLK
        runs += [(_XBC0 + g * SSM_GW, 1, base, SSM_GW), (_B0 + g * SSM_DSTATE, 1, base + SSM_GW, SSM_DSTATE),
                 (_C0 + g * SSM_DSTATE, 1, base + SSM_GW + SSM_DSTATE, SSM_DSTATE),
                 (_Z0 + g * SSM_GW, 1, base + SSM_XBC + 128, SSM_GW)]
    return runs + [(_GATE0, 2, 0, 2 * D_MODEL)]


def _in_proj_to_kernel(in_t):
    d = in_t.shape[1]
    dt = jnp.pad(in_t[_DT0:_GATE0].reshape(SSM_GROUPS, SSM_HPG, d), ((0, 0), (0, 128 - SSM_HPG), (0, 0)))
    runs = [(0, src, sec, dst, n) for src, sec, dst, n in _in_proj_runs()]
    runs += [(1, g * 128, 1, g * SSM_BLK + SSM_XBC, 128) for g in range(SSM_GROUPS)]
    return _copy_runs([in_t, dt.reshape(SSM_GROUPS * 128, d)], [_Z0, SSM_GROUPS * SSM_BLK, 2 * D_MODEL], runs,
                      name="in_proj_to_kernel_layout")


def _in_proj_from_kernel(hg, ssm, gate):
    d = hg.shape[1]
    dt = ssm.reshape(SSM_GROUPS, SSM_BLK, d)[:, SSM_XBC:SSM_XBC + SSM_HPG].reshape(SSM_HEADS, d)
    runs = [(sec, dst, 0, src, n) for src, sec, dst, n in _in_proj_runs()] + [(3, 0, 0, _DT0, SSM_HEADS)]
    return _copy_runs([hg, ssm, gate, dt], [IN_TOTAL], runs, name="in_proj_to_global_layout")[0]


def _up_to_kernel(up_t):
    runs = [(0, part * D_FF + g * FFN_GW, 0, (2 * g + part) * FFN_GW, FFN_GW) for g in range(FFN_G) for part in range(2)]
    return _copy_runs([up_t], [2 * D_FF], runs, name="up_to_kernel_layout")[0]


def _up_from_kernel(up):
    runs = [(0, (2 * g + part) * FFN_GW, 0, part * D_FF + g * FFN_GW, FFN_GW) for g in range(FFN_G) for part in range(2)]
    return _copy_runs([up], [2 * D_FF], runs, name="up_to_global_layout")[0]


_SMALL = (("mix_pre_norm", (1, 2048)), ("mix_post_norm", (1, 2048)), ("hg_lb_table", (2, 2048)), ("hg_out_norm", (1, 128)),
          ("ssm_conv_w", (4, 6144)), ("ssm_conv_b", (1, 6144)), ("ssm_dt_bias", (1, 64)), ("ssm_A_log", (1, 64)),
          ("ssm_D", (1, 64)), ("ssm_out_norm", (1, 4096)), ("ffn_pre_norm", (1, 2048)), ("ffn_post_norm", (1, 2048)),
          ("ffn_conv_w", (3, 5632)), ("ffn_conv_b", (1, 5632)), ("loss", (1, 1)))
_PACK_ROWS = 8 * (-(-sum(int(np.prod(s)) for _, s in _SMALL) // 1024))


def _pack(vals):
    flat = jnp.concatenate([vals[k].astype(F32).reshape(-1) for k, _ in _SMALL])
    return jnp.pad(flat, (0, _PACK_ROWS * 128 - flat.shape[0])).reshape(_PACK_ROWS, 128)


def _unpack(packed):
    flat, out, o = packed.reshape(-1), {}, 0
    for k, s in _SMALL:
        n = int(np.prod(s))
        out[k] = flat[o:o + n].reshape(s)
        o += n
    return out


def _local_step(x, target, w, p, late_weights=None, emit=lambda key, gw: None):
    t = x.shape[0]
    one = lambda a: a.reshape((1,) + a.shape)
    row = dict(rows=t, groups=1, consts=[], carries=[])

    (h1,), _ = _stage_fwd(_pre_step, name="pre_fwd", chunk=512, nc=1, xs=[(x, D_MODEL)], params=[one(p["mix_pre_norm"])],
                          ys=[(D_MODEL, BF16)], **row)
    proj_hg = _matmul_nt(h1, w["in_hg"], name="proj_hg")
    proj_ssm = _matmul_nt(h1, w["in_ssm"], name="proj_ssm")
    proj_gate = _matmul_nt(h1, w["in_gate"], name="proj_gate")

    hg = dict(rows=t, chunk=HG_CHUNK, nc=8, groups=HG_HEADS, xs=[(proj_hg, HG_BLK)], params=[p["hg_tab"], p["hg_nw"]],
              consts=_hg_consts(), carries=[(HG_DK, HG_DK)], gpb=4)
    (y_hg,), hg_saved = _stage_fwd(_hg_step, name="hg_fwd", ys=[(HG_DK, BF16)], **hg)

    ssd = dict(rows=t, chunk=SSM_CHUNK, nc=4, groups=SSM_GROUPS, xs=[(proj_ssm, SSM_BLK)],
               params=[p["conv_w"], p["conv_b"], p["dt_bias"], p["a_log"], p["d_skip"], p["ssm_nw"]],
               consts=_ssd_consts(), carries=[(4 * 128, SSM_DSTATE), (HALO, SSM_XBC)])
    (y_ssm,), ssd_saved = _stage_fwd(_ssd_step, name="ssd_fwd", ys=[(SSM_GW, BF16)], **ssd)

    if late_weights is not None:
        w = {**w, **late_weights([y_hg, y_ssm])}
    u_hg = _matmul_nn(y_hg, w["branch_hg"], name="branch_hg")
    u_ssm = _matmul_nn(y_ssm, w["branch_ssm"], name="branch_ssm")
    mix = dict(chunk=256, nc=1, xs=[(proj_gate, 2 * D_MODEL), (u_hg, D_MODEL), (u_ssm, D_MODEL)], params=[], **row)
    (mixed,), _ = _stage_fwd(_mix_step, name="mix_fwd", ys=[(D_MODEL, BF16)], **mix)
    v = _matmul_nn(mixed, w["out"], name="out_proj")
    post = dict(chunk=256, nc=1, xs=[(x, D_MODEL), (v, D_MODEL)],
                params=[one(p["mix_post_norm"]), one(p["ffn_pre_norm"])], **row)
    (x1, h2), _ = _stage_fwd(_post_step, name="post_fwd", ys=[(D_MODEL, F32), (D_MODEL, BF16)], **post)
    gu = _matmul_nt(h2, w["up"], name="ffn_up")
    ffn = dict(rows=t, chunk=256, nc=2, groups=FFN_G, xs=[(gu, 2 * FFN_GW)], params=[p["ffn_conv_w"], p["ffn_conv_b"]],
               consts=[], carries=[(HALO, FFN_GW)])
    (act,), ffn_saved = _stage_fwd(_ffn_step, name="ffn_fwd", ys=[(FFN_GW, BF16)], **ffn)
    d = _matmul_nn(act, w["down"], name="ffn_down")

    def head_step(carry, xv, civ, pv, cv):
        x1_, d_, tgt = xv

        def per_row_loss(a, b, nw):
            e = a + _rms(b, nw) - tgt
            return 0.5 * jnp.mean(e * e, axis=1, keepdims=True)

        lrow, vjp = jax.vjp(per_row_loss, x1_, d_, pv[0])
        dx1_, dd_, dnw = vjp(jnp.ones_like(lrow))
        loss = jnp.broadcast_to(jnp.sum(lrow, axis=0, keepdims=True), (1, 128))
        return [], [dx1_, dd_], [], [dnw, loss]

    (dy, dd), _, (g_ffn_post, loss) = _scan_call(
        head_step, name="loss_head", chunk=256, nc=1, xs=[(x1, D_MODEL), (d, D_MODEL), (target, D_MODEL)],
        params=[one(p["ffn_post_norm"])], ys=[(D_MODEL, F32), (D_MODEL, BF16)], accs=[(1, D_MODEL), (1, 128)], **row)

    gw = {}
    gw["down"] = _matmul_tn(act, dd, name="g_down")
    dact = _matmul_nt(dd, w["down"], name="d_act", dep=emit("down", gw))
    (dgu,), (g_fcw, g_fcb) = _stage_bwd(_ffn_step, name="ffn_bwd", saved=ffn_saved, dys=[(dact, FFN_GW)], dxs=[BF16], **ffn)
    gw["up"] = _matmul_tn(dgu, h2, name="g_up")
    dh2 = _matmul_nn(dgu, w["up"], name="d_h2", dep=emit("up", gw))
    (dx1, dv), (g_mix_post, g_ffn_pre) = _stage_bwd(_post_step, name="post_bwd", saved=[], dys=[(dy, D_MODEL), (dh2, D_MODEL)],
                                                    dxs=[F32, BF16], **post)
    gw["out"] = _matmul_tn(mixed, dv, name="g_out")
    dmixed = _matmul_nt(dv, w["out"], name="d_mixed")
    (dgate, du_hg, du_ssm), _ = _stage_bwd(_mix_step, name="mix_bwd", saved=[], dys=[(dmixed, D_MODEL)],
                                           dxs=[BF16, BF16, BF16], **mix)
    gw["in_gate"] = _matmul_tn(dgate, h1, name="g_in_gate")
    gw["branch_hg"] = _matmul_tn(y_hg, du_hg, name="g_branch_hg")
    gw["branch_ssm"] = _matmul_tn(y_ssm, du_ssm, name="g_branch_ssm")
    dy_hg = _matmul_nt(du_hg, w["branch_hg"], name="d_y_hg", dep=emit("branches", gw))
    dy_ssm = _matmul_nt(du_ssm, w["branch_ssm"], name="d_y_ssm")
    (dproj_ssm,), g_ssd = _stage_bwd(_ssd_step, name="ssd_bwd", saved=ssd_saved, dys=[(dy_ssm, SSM_GW)], dxs=[BF16], **ssd)
    gw["in_ssm"] = _matmul_tn(dproj_ssm, h1, name="g_in_ssm")
    (dproj_hg,), (g_tab, g_hg_nw) = _stage_bwd(_hg_step, name="hg_bwd", saved=hg_saved, dys=[(dy_hg, HG_DK)], dxs=[BF16], **hg)
    gw["in_hg"] = _matmul_tn(dproj_hg, h1, name="g_in_hg")
    dh_a = _matmul_nn(dproj_hg, w["in_hg"], name="d_h1_hg", dep=emit("in", gw))
    dh_b = _matmul_nn(dproj_ssm, w["in_ssm"], name="d_h1_ssm")
    dh_c = _matmul_nn(dgate, w["in_gate"], name="d_h1_gate")

    def pre_bwd_step(carry, xv, civ, pv, cv):
        x_, da, db, dc, dres = xv
        _, vjp = jax.vjp(_rms, x_, pv[0])
        dx_, dnw = vjp(da + db + dc)
        return [], [dx_ + dres], [], [dnw]

    (grad_x,), _, (g_mix_pre,) = _scan_call(
        pre_bwd_step, name="pre_bwd", chunk=256, nc=1,
        xs=[(x, D_MODEL), (dh_a, D_MODEL), (dh_b, D_MODEL), (dh_c, D_MODEL), (dx1, D_MODEL)],
        params=[one(p["mix_pre_norm"])], ys=[(D_MODEL, F32)], accs=[(1, D_MODEL)], **row)

    gp = dict(mix_pre_norm=g_mix_pre[0], mix_post_norm=g_mix_post[0], ffn_pre_norm=g_ffn_pre[0], ffn_post_norm=g_ffn_post[0],
              hg_tab=g_tab, hg_nw=g_hg_nw, conv_w=g_ssd[0], conv_b=g_ssd[1], dt_bias=g_ssd[2], a_log=g_ssd[3],
              d_skip=g_ssd[4], ssm_nw=g_ssd[5], ffn_conv_w=g_fcw, ffn_conv_b=g_fcb, loss=loss[0, :, :1])
    return grad_x, gw, gp


def _small_to_kernel_layout(s):
    conv_idx = _conv_layout()
    pad_heads = lambda a: jnp.pad(a.reshape(SSM_GROUPS, 1, SSM_HPG), ((0, 0), (0, 0), (0, 128 - SSM_HPG)))
    return dict(
        mix_pre_norm=s["mix_pre_norm"], mix_post_norm=s["mix_post_norm"], ffn_pre_norm=s["ffn_pre_norm"],
        ffn_post_norm=s["ffn_post_norm"],
        hg_tab=s["hg_lb_table"].reshape(2, HG_HEADS, HG_DK).transpose(1, 0, 2),
        hg_nw=jnp.broadcast_to(s["hg_out_norm"].reshape(1, 1, HG_DK), (HG_HEADS, 1, HG_DK)),
        conv_w=_take_rows(s["ssm_conv_w"], conv_idx, axis=1).reshape(SSM_CONV, SSM_GROUPS, SSM_XBC).transpose(1, 0, 2),
        conv_b=_take_rows(s["ssm_conv_b"], conv_idx, axis=1).reshape(SSM_GROUPS, 1, SSM_XBC),
        dt_bias=pad_heads(s["ssm_dt_bias"]), a_log=pad_heads(s["ssm_A_log"]),
        d_skip=jnp.repeat(s["ssm_D"].reshape(SSM_HEADS), SSM_HEADDIM).reshape(SSM_GROUPS, 1, SSM_GW),
        ssm_nw=s["ssm_out_norm"].reshape(SSM_GROUPS, 1, SSM_GW),
        ffn_conv_w=s["ffn_conv_w"].reshape(FFN_CONV, FFN_G, FFN_GW).transpose(1, 0, 2),
        ffn_conv_b=s["ffn_conv_b"].reshape(FFN_G, 1, FFN_GW),
    )


def _small_from_kernel_layout(g):
    conv_inv = _inverse(_conv_layout().reshape(-1), SSM_CONV_DIM)
    heads = lambda a: a[:, 0, :SSM_HPG].reshape(1, SSM_HEADS)
    return dict(
        mix_pre_norm=g["mix_pre_norm"], mix_post_norm=g["mix_post_norm"], ffn_pre_norm=g["ffn_pre_norm"],
        ffn_post_norm=g["ffn_post_norm"],
        hg_lb_table=g["hg_tab"].transpose(1, 0, 2).reshape(2, HG_HEADS * HG_DK),
        hg_out_norm=jnp.sum(g["hg_nw"], axis=0),
        ssm_conv_w=_take_rows(g["conv_w"].transpose(1, 0, 2).reshape(SSM_CONV, -1), conv_inv, axis=1),
        ssm_conv_b=_take_rows(g["conv_b"].reshape(1, -1), conv_inv, axis=1),
        ssm_dt_bias=heads(g["dt_bias"]), ssm_A_log=heads(g["a_log"]),
        ssm_D=jnp.sum(g["d_skip"].reshape(SSM_HEADS, SSM_HEADDIM), axis=1).reshape(1, SSM_HEADS),
        ssm_out_norm=g["ssm_nw"].reshape(1, SSM_DINNER),
        ffn_conv_w=g["ffn_conv_w"].transpose(1, 0, 2).reshape(FFN_CONV, D_FF),
        ffn_conv_b=g["ffn_conv_b"].reshape(1, D_FF),
        loss=g["loss"],
    )


def kernel(x, w_in, mix_pre_norm, mix_post_norm, hg_lb_table, hg_out_norm, ssm_conv_w, ssm_conv_b, ssm_dt_bias, ssm_A_log, ssm_D, ssm_out_norm, w_branch_hg, w_branch_ssm, w_out, ffn_pre_norm, ffn_post_norm, ffn_w_up, ffn_conv_w, ffn_conv_b, ffn_w_down, loss_target, m_w_in, m_mix_pre_norm, m_mix_post_norm, m_hg_lb_table, m_hg_out_norm, m_ssm_conv_w, m_ssm_conv_b, m_ssm_dt_bias, m_ssm_A_log, m_ssm_D, m_ssm_out_norm, m_w_branch_hg, m_w_branch_ssm, m_w_out, m_ffn_pre_norm, m_ffn_post_norm, m_ffn_w_up, m_ffn_conv_w, m_ffn_conv_b, m_ffn_w_down, v_w_in, v_mix_pre_norm, v_mix_post_norm, v_hg_lb_table, v_hg_out_norm, v_ssm_conv_w, v_ssm_conv_b, v_ssm_dt_bias, v_ssm_A_log, v_ssm_D, v_ssm_out_norm, v_w_branch_hg, v_w_branch_ssm, v_w_out, v_ffn_pre_norm, v_ffn_post_norm, v_ffn_w_up, v_ffn_conv_w, v_ffn_conv_b, v_ffn_w_down):
    names = ["w_in", "mix_pre_norm", "mix_post_norm", "hg_lb_table", "hg_out_norm", "ssm_conv_w", "ssm_conv_b", "ssm_dt_bias",
             "ssm_A_log", "ssm_D", "ssm_out_norm", "w_branch_hg", "w_branch_ssm", "w_out", "ffn_pre_norm", "ffn_post_norm",
             "ffn_w_up", "ffn_conv_w", "ffn_conv_b", "ffn_w_down"]
    ws = dict(zip(names, (w_in, mix_pre_norm, mix_post_norm, hg_lb_table, hg_out_norm, ssm_conv_w, ssm_conv_b, ssm_dt_bias,
                          ssm_A_log, ssm_D, ssm_out_norm, w_branch_hg, w_branch_ssm, w_out, ffn_pre_norm, ffn_post_norm,
                          ffn_w_up, ffn_conv_w, ffn_conv_b, ffn_w_down)))
    ms = dict(zip(names, (m_w_in, m_mix_pre_norm, m_mix_post_norm, m_hg_lb_table, m_hg_out_norm, m_ssm_conv_w, m_ssm_conv_b,
                          m_ssm_dt_bias, m_ssm_A_log, m_ssm_D, m_ssm_out_norm, m_w_branch_hg, m_w_branch_ssm, m_w_out,
                          m_ffn_pre_norm, m_ffn_post_norm, m_ffn_w_up, m_ffn_conv_w, m_ffn_conv_b, m_ffn_w_down)))
    vs = dict(zip(names, (v_w_in, v_mix_pre_norm, v_mix_post_norm, v_hg_lb_table, v_hg_out_norm, v_ssm_conv_w, v_ssm_conv_b,
                          v_ssm_dt_bias, v_ssm_A_log, v_ssm_D, v_ssm_out_norm, v_w_branch_hg, v_w_branch_ssm, v_w_out,
                          v_ffn_pre_norm, v_ffn_post_norm, v_ffn_w_up, v_ffn_conv_w, v_ffn_conv_b, v_ffn_w_down)))
    me = 4 * lax.axis_index("x") + 2 * lax.axis_index("y") + lax.axis_index("c")

    late_shards = [ffn_w_up[0].T.astype(BF16), w_branch_hg[0].astype(BF16), w_branch_ssm[0].astype(BF16),
                   w_out[0].astype(BF16), ffn_w_down[0].astype(BF16)]
    landing = [lax.dynamic_update_slice_in_dim(lax.empty((N_DEV,) + s.shape, s.dtype), s[None], me, axis=0)
               for s in late_shards]
    gathered = _all_gather([w_in[0].T.astype(BF16), ssm_conv_w[0], ffn_conv_w[0]], name="gather_in_proj")
    late = _push_start([s[None] for s in late_shards], landing, name="late_weights_start", plan=_plan_own_block,
                       after=gathered[1])
    in_hg, in_ssm, in_gate = _in_proj_to_kernel(gathered[0].reshape(IN_TOTAL, D_MODEL))
    w = dict(in_hg=in_hg, in_ssm=in_ssm, in_gate=in_gate)
    small = {k: ws[k] for k, _ in _SMALL[:-1]}
    small["mix_pre_norm"] = mix_pre_norm + late[4][0, 0]
    small["ssm_conv_w"] = gathered[1].transpose(1, 0, 2).reshape(SSM_CONV, SSM_CONV_DIM)
    small["ffn_conv_w"] = gathered[2].transpose(1, 0, 2).reshape(FFN_CONV, D_FF)
    small = {k: small[k].reshape(s) for k, s in _SMALL[:-1]}

    def late_weights(after):
        landed = _push_wait(late, after, name="late_weights_wait", plan=_plan_own_block)
        up_all, bhg, bssm, out, down = _push(landed, name="late_weights_pass_on", plan=_plan_pass_on)
        return dict(up=_up_to_kernel(up_all.reshape(2 * D_FF, D_MODEL)), branch_hg=bhg.reshape(D_MODEL, D_MODEL),
                    branch_ssm=bssm.reshape(SSM_DINNER, D_MODEL), out=out.reshape(D_MODEL, D_MODEL),
                    down=down.reshape(D_FF, D_MODEL))

    in_flight = []

    def launch(key, named_parts):
        ks, parts = zip(*named_parts)
        from_sibling = _push(list(parts), name="grads_to_sibling_" + key, out_slots=N_CHIPS, plan=_plan_sibling)
        sums = [_pair_sum(p, r, name="pair_sum_" + k, tc=256) for k, p, r in zip(ks, parts, from_sibling)]
        handles = _push_start([q for q, _ in sums], [z for _, z in sums], name="grads_to_chips_start_" + key, plan=_plan_chips)
        in_flight.append((key, ks, handles))
        return handles[4]

    def emit(key, gw):
        blocks = lambda a: a.reshape(N_DEV, -1, D_MODEL)
        if key == "down":
            return launch(key, [("ffn_w_down", blocks(gw["down"]))])
        if key == "up":
            return launch(key, [("ffn_w_up", blocks(_up_from_kernel(gw["up"])))])
        if key == "branches":
            return launch(key, [("w_branch_hg", blocks(gw["branch_hg"])), ("w_branch_ssm", blocks(gw["branch_ssm"])),
                                ("w_out", blocks(gw["out"]))])
        return launch(key, [("w_in", blocks(_in_proj_from_kernel(gw["in_hg"], gw["in_ssm"], gw["in_gate"])))])

    grad_x, gw, gp = _local_step(x[0], loss_target[0], w, _small_to_kernel_layout(small), late_weights, emit)

    big_names = ["w_in", "ffn_w_up", "w_branch_hg", "w_branch_ssm", "w_out", "ffn_w_down"]
    grads = {}
    for key, ks, handles in in_flight:
        landed = _push_wait(handles, grad_x, name="grads_to_chips_wait_" + key, plan=_plan_chips)
        for k, r in zip(ks, landed):
            g = _sum_blocks(r, name="sum_" + k, tc=256)
            grads[k] = g.T if k in ("w_in", "ffn_w_up") else g
    small_all = _push([_pack(_small_from_kernel_layout(gp))[None]], name="small_to_everyone", out_slots=N_DEV,
                      plan=_plan_everyone)
    small_g = _unpack(_sum_blocks(small_all[0], name="sum_small", tc=128))
    loss = small_g.pop("loss").reshape(())
    for k, g in small_g.items():
        if k in ("ssm_conv_w", "ffn_conv_w"):
            n = g.shape[1] // N_DEV
            g = lax.dynamic_slice_in_dim(g, me * n, n, axis=1)
        grads[k] = g

    delta, new_m, new_v = {}, {}, {}
    for k in big_names:
        delta[k], new_m[k], new_v[k] = _adamw(ws[k][0], grads[k], ms[k][0], vs[k][0], name="adamw_" + k, tr=64)
    small_names = [k for k in names if k not in big_names]
    flat = lambda d: jnp.concatenate([d[k].astype(F32).reshape(-1) for k in small_names])
    n_small = sum(int(np.prod(ws[k].shape)) for k in small_names)
    rows = 8 * (-(-n_small // 1024))
    pack2 = lambda d: jnp.pad(flat(d), (0, rows * 128 - n_small)).reshape(rows, 128)
    v_packed = jnp.pad(flat(vs), (0, rows * 128 - n_small), constant_values=1.0).reshape(rows, 128)
    packed = _adamw(pack2(ws), pack2(grads), pack2(ms), v_packed, name="adamw_small", tr=rows)
    o = 0
    for k in small_names:
        n = int(np.prod(ws[k].shape))
        delta[k], new_m[k], new_v[k] = (a.reshape(-1)[o:o + n].reshape(ws[k].shape) for a in packed)
        o += n

    full = lambda d: [d[k].reshape(ws[k].shape) for k in names]
    return (loss, grad_x[None], *full(grads), *full(delta), *full(new_m), *full(new_v))
```

```python
import functools

import numpy as np
import jax
import jax.numpy as jnp
from jax import lax
from jax.experimental import pallas as pl
from jax.experimental.pallas import tpu as pltpu

F32, BF16 = jnp.float32, jnp.bfloat16

D_MODEL = 2048
EPS = 1e-6
HG_HEADS, HG_DK, HG_CHUNK = 16, 128, 64
HG_BLK = 4 * HG_DK
SSM_DINNER, SSM_HEADDIM, SSM_HEADS, SSM_GROUPS, SSM_DSTATE, SSM_CONV = 4096, 64, 64, 8, 128, 4
SSM_CHUNK = 128
SSM_GW = SSM_DINNER // SSM_GROUPS
SSM_HPG = SSM_HEADS // SSM_GROUPS
SSM_XBC = SSM_GW + 2 * SSM_DSTATE
SSM_BLK = SSM_XBC + 128 + SSM_GW
SSM_CONV_DIM = SSM_DINNER + 2 * SSM_GROUPS * SSM_DSTATE
D_FF, FFN_CONV = 5632, 3
FFN_GW = 512
FFN_G = D_FF // FFN_GW
IN_TOTAL = 22592
N_DEV = 8
HALO = 8
VMEM_LIMIT = 52 * 1024 * 1024
ADAM_LR, ADAM_B1, ADAM_B2, ADAM_EPS, ADAM_WD, ADAM_STEP = 0.001, 0.9, 0.999, 1e-08, 0.01, 10

_DIMS = {"nn": ((1,), (0,)), "nt": ((1,), (1,)), "tn": ((0,), (0,))}


def _mm_raw(a, b, mode):
    return lax.dot_general(a.astype(BF16), b.astype(BF16), (_DIMS[mode], ((), ())), preferred_element_type=F32)


@functools.partial(jax.custom_vjp, nondiff_argnums=(2,))
def _mm(a, b, mode):
    return _mm_raw(a, b, mode)


def _mm_fwd(a, b, mode):
    return _mm_raw(a, b, mode), (a, b)


def _mm_bwd(mode, res, dc):
    a, b = res
    if mode == "nn":
        return _mm_raw(dc, b, "nt"), _mm_raw(a, dc, "tn")
    if mode == "nt":
        return _mm_raw(dc, b, "nn"), _mm_raw(dc, a, "tn")
    return _mm_raw(b, dc, "nt"), _mm_raw(a, dc, "nn")


_mm.defvjp(_mm_fwd, _mm_bwd)


def _cmm_raw(m, x, mode):
    hi = x.astype(BF16)
    r1 = x - hi.astype(F32)
    mid = r1.astype(BF16)
    lo = (r1 - mid.astype(F32)).astype(BF16)
    dn = (_DIMS[mode], ((), ()))
    dot = lambda p: lax.dot_general(m, p, dn, preferred_element_type=F32)
    return dot(hi) + dot(mid) + dot(lo)


@jax.custom_vjp
def _cmm(m, x):
    return _cmm_raw(m, x, "nn")


def _cmm_fwd(m, x):
    return _cmm_raw(m, x, "nn"), m


def _cmm_bwd(m, dy):
    return jnp.zeros_like(m), _cmm_raw(m, dy, "tn")


_cmm.defvjp(_cmm_fwd, _cmm_bwd)


@functools.partial(jax.custom_vjp, nondiff_argnums=(1,))
def _sroll(x, s):
    return pltpu.roll(x, s, 0) if s else x


def _sroll_fwd(x, s):
    return _sroll(x, s), None


def _sroll_bwd(s, _, ct):
    return ((pltpu.roll(ct, ct.shape[0] - s, 0) if s else ct),)


_sroll.defvjp(_sroll_fwd, _sroll_bwd)


def _rms(x, w):
    return x * lax.rsqrt(jnp.mean(x * x, axis=-1, keepdims=True) + EPS) * w


def _softplus(x):
    return jnp.maximum(x, 0.0) + jnp.log(1.0 + jnp.exp(-jnp.abs(x)))


def _causal_conv(halo, x, w, b):
    k_taps = w.shape[0]
    xe = jnp.concatenate([halo, x], axis=0)
    out = b
    for k in range(k_taps):
        out = out + w[k:k + 1, :] * _sroll(xe, k_taps - 1 - k)[HALO:, :]
    return out


def _hg_consts():
    c = HG_CHUNK
    t = np.arange(c)
    blocks, pair = [], []
    for m in (32, 16, 8, 4, 2, 1):
        pos = t % (2 * m)
        late = pos >= m
        mid = t - pos + m
        j = t[None, :]
        mq = late[:, None] & (j >= mid[:, None]) & (j <= t[:, None])
        mk = (~late)[:, None] & (j > t[:, None]) & (j <= mid[:, None] - 1)
        blocks.append(mq | mk)
        parent = t // (2 * m)
        pair.append((parent[:, None] == parent[None, :]) & late[:, None] & (~late)[None, :])
    blocks.append(t[None, :] <= t[:, None])
    mall = jnp.asarray(np.concatenate(blocks, 0).astype(np.float32), BF16)
    pair = jnp.asarray(np.stack(pair, 0).astype(np.float32))
    eye = jnp.asarray(np.eye(c, dtype=np.float32))
    return [mall, pair, eye]


def _hg_step(carry, xs, params, consts):
    (st,) = carry
    blk = xs[0].astype(F32)
    tab, nw = params
    mall, pair, eye = consts
    c, dk = HG_CHUNK, HG_DK
    q_raw, f_raw, v, og = blk[:, :dk], blk[:, dk:2 * dk], blk[:, 2 * dk:3 * dk], blk[:, 3 * dk:]
    lb = jax.nn.sigmoid(tab[0:1, :] - tab[1:2, :])
    f = lb + (1.0 - lb) * jax.nn.sigmoid(f_raw)
    g = jnp.log(f)
    kk = 1.0 - f
    qh = jax.nn.silu(q_raw) * (HG_DK ** -0.5)
    yield
    sums = _cmm(mall, g)
    yield
    b = sums[6 * c:, :]
    fac = jnp.exp(sums[:6 * c, :])
    scores = eye * jnp.sum(qh * kk, axis=1, keepdims=True)
    b_last = jnp.sum(g, axis=0, keepdims=True)
    yield
    inter = _mm(qh * jnp.exp(b), st, "nt")
    st_new = st * jnp.exp(b_last) + _mm(v, kk * jnp.exp(b_last - b), "tn")
    yield
    for l in range(6):
        fl = fac[l * c:(l + 1) * c, :]
        scores = scores + pair[l] * _mm(qh * fl, kk * fl, "nt")
        if l % 2:
            yield
    o = _mm(scores, v, "nn") + inter
    yield
    y = _rms(o, nw) * jax.nn.silu(og)
    return [st_new], [y]


def _ssd_consts():
    t = np.arange(SSM_CHUNK)
    tril = (t[None, :] <= t[:, None]).astype(np.float32)
    return [jnp.asarray(tril, BF16), jnp.asarray(tril)]


def _ssd_step(carry, xs, params, consts):
    st, halo = carry
    blk = xs[0].astype(F32)
    conv_w, conv_b, dtb, alog, dskip, nw = params
    tril_b, tril = consts
    c = SSM_CHUNK
    raw, dtr, z = blk[:, :SSM_XBC], blk[:, SSM_XBC:SSM_XBC + 128], blk[:, SSM_XBC + 128:]
    act = jax.nn.silu(_causal_conv(halo, raw, conv_w, conv_b))
    xh, bm, cm = act[:, :SSM_GW], act[:, SSM_GW:SSM_GW + SSM_DSTATE], act[:, SSM_GW + SSM_DSTATE:]
    dt = _softplus(dtr + dtb)
    da = dt * (-jnp.exp(alog))
    acum = _cmm(tril_b, da)
    acum_t = acum.T
    a_last = jnp.sum(da, axis=0, keepdims=True)
    cb_causal = _mm(cm, bm, "nt") * tril
    lane = lax.broadcasted_iota(jnp.int32, (c, 128), 1)
    row = lax.broadcasted_iota(jnp.int32, (128, 128), 0)
    first = lane < SSM_HEADDIM
    ys, st_new = [], []
    for j in range(SSM_HPG // 2):
        xp = xh[:, 128 * j:128 * (j + 1)]
        sp = st[128 * j:128 * (j + 1), :]
        r0, r1 = 2 * j, 2 * j + 1
        col = lambda a, r: jnp.broadcast_to(a[:, r:r + 1], (c, 128))
        xdt = xp * jnp.where(first, col(dt, r0), col(dt, r1))
        yj = _mm(cm, sp, "nt") * jnp.exp(jnp.where(first, col(acum, r0), col(acum, r1)))
        for r, keep in ((r0, first), (r1, ~first)):
            dec = jnp.broadcast_to(acum[:, r:r + 1], (c, c)) - jnp.broadcast_to(acum_t[r:r + 1, :], (c, c))
            m = cb_causal * jnp.exp(jnp.minimum(dec, 0.0))
            yj = yj + _mm(m, jnp.where(keep, xdt, 0.0), "nn")
        al0, al1 = a_last[:, r0:r0 + 1], a_last[:, r1:r1 + 1]
        wts = jnp.exp(jnp.where(first, al0 - col(acum, r0), al1 - col(acum, r1)))
        st_new.append(jnp.where(row < SSM_HEADDIM, jnp.exp(al0), jnp.exp(al1)) * sp + _mm(xdt * wts, bm, "tn"))
        ys.append(yj)
    y = jnp.concatenate(ys, axis=1) + dskip * xh
    y = _rms(y * jax.nn.silu(z), nw)
    return [jnp.concatenate(st_new, axis=0), raw[c - HALO:, :]], [y]


def _ffn_step(carry, xs, params, consts):
    (halo,) = carry
    blk = xs[0].astype(F32)
    conv_w, conv_b = params
    gate, up = blk[:, :FFN_GW], blk[:, FFN_GW:]
    a = jax.nn.gelu(_causal_conv(halo, gate, conv_w, conv_b), approximate=True) * up
    return [gate[gate.shape[0] - HALO:, :]], [a]


def _pre_step(carry, xs, params, consts):
    return [], [_rms(xs[0], params[0])]


def _mix_step(carry, xs, params, consts):
    gates, uh, us = (a.astype(F32) for a in xs)
    return [], [jax.nn.sigmoid(gates[:, :D_MODEL]) * uh + jax.nn.sigmoid(gates[:, D_MODEL:]) * us]


def _post_step(carry, xs, params, consts):
    x, v = xs
    x1 = x + _rms(v, params[0])
    return [], [x1, _rms(x1, params[1])]


def _scan_call(step, *, name, rows, chunk, nc, groups, xs, cins=(), params=(), consts=(), carries=(), ys=(), couts=(),
               accs=(), reverse=False, gpb=1, multi=False):
    blk_rows = chunk * nc
    nb = rows // blk_rows
    n_chunks = rows // chunk
    assert nb * blk_rows == rows and groups % gpb == 0
    rb = (lambda i: nb - 1 - i) if reverse else (lambda i: i)
    n_x, n_ci, n_p, n_c = len(xs), len(cins), len(params), len(consts)
    n_y, n_co, n_a = len(ys), len(couts), len(accs)

    def chunk_spec(shape):
        zeros = (0,) * len(shape)
        return pl.BlockSpec((gpb, nc) + tuple(shape), lambda g, i: (g, rb(i)) + zeros)

    in_specs = [pl.BlockSpec((blk_rows, gpb * w), lambda g, i: (rb(i), g)) for _, w in xs]
    in_specs += [chunk_spec(a.shape[2:]) for a in cins]
    in_specs += [pl.BlockSpec((gpb,) + tuple(a.shape[1:]), lambda g, i: (g, 0, 0)) for a in params]
    in_specs += [pl.BlockSpec(a.shape, (lambda nd: lambda g, i: (0,) * nd)(a.ndim)) for a in consts]
    out_specs = [pl.BlockSpec((blk_rows, gpb * w), lambda g, i: (rb(i), g)) for w, _ in ys]
    out_specs += [chunk_spec(s) for s in couts]
    out_specs += [pl.BlockSpec((gpb, r, c), lambda g, i: (g, 0, 0)) for r, c in accs]
    out_shape = [jax.ShapeDtypeStruct((rows, groups * w), dt) for w, dt in ys]
    out_shape += [jax.ShapeDtypeStruct((groups, n_chunks) + tuple(s), F32) for s in couts]
    out_shape += [jax.ShapeDtypeStruct((groups, r, c), F32) for r, c in accs]
    x_widths = [w for _, w in xs]
    y_widths = [w for w, _ in ys]

    def body(*refs):
        x_refs = refs[:n_x]
        ci_refs = refs[n_x:n_x + n_ci]
        p_refs = refs[n_x + n_ci:n_x + n_ci + n_p]
        c_refs = refs[n_x + n_ci + n_p:n_x + n_ci + n_p + n_c]
        o = n_x + n_ci + n_p + n_c
        y_refs = refs[o:o + n_y]
        co_refs = refs[o + n_y:o + n_y + n_co]
        a_refs = refs[o + n_y + n_co:o + n_y + n_co + n_a]
        carry_refs = refs[o + n_y + n_co + n_a:]

        @pl.when(pl.program_id(1) == 0)
        def _():
            for s in carry_refs:
                s[...] = jnp.zeros(s.shape, F32)
            for a in a_refs:
                a[...] = jnp.zeros(a.shape, F32)

        cvals = [c[...] for c in c_refs]

        def one_chunk(i, _):
            c = (nc - 1 - i) if reverse else i
            r0 = c * chunk if isinstance(c, int) else pl.multiple_of(c * chunk, chunk)
            loaded = []
            for u in range(gpb):
                carry = [s[u] for s in carry_refs]
                xv = [x[pl.ds(r0, chunk), u * w:(u + 1) * w] for x, w in zip(x_refs, x_widths)]
                civ = [ci[u, c] for ci in ci_refs]
                loaded.append((carry, xv, civ, [p[u] for p in p_refs]))
            results = step(loaded, cvals) if multi else [step(*args, cvals) for args in loaded]
            for u, (new_carry, yv, cov, av) in enumerate(results):
                for s, val in zip(carry_refs, new_carry):
                    s[u] = val
                for y, w, val in zip(y_refs, y_widths, yv):
                    y[pl.ds(r0, chunk), u * w:(u + 1) * w] = val.astype(y.dtype)
                for co, val in zip(co_refs, cov):
                    co[u, c] = val
                for a, val in zip(a_refs, av):
                    a[u] += val
            return 0

        if nc == 1:
            one_chunk(0, 0)
        else:
            lax.fori_loop(0, nc, one_chunk, 0)

    outs = pl.pallas_call(
        body, name=name, grid=(groups // gpb, nb), in_specs=in_specs, out_specs=out_specs, out_shape=out_shape,
        scratch_shapes=[pltpu.VMEM((gpb,) + tuple(s), F32) for s in carries],
        compiler_params=pltpu.CompilerParams(dimension_semantics=("arbitrary", "arbitrary"),
                                             vmem_limit_bytes=VMEM_LIMIT),
    )(*[a for a, _ in xs], *cins, *params, *consts)
    return outs[:n_y], outs[n_y:n_y + n_co], outs[n_y + n_co:]


def _run_interleaved(step, arg_tuples):
    runs = [step(*args) for args in arg_tuples]
    if not hasattr(runs[0], "send"):
        return runs
    results, live = [None] * len(runs), list(range(len(runs)))
    while live:
        for u in list(live):
            try:
                next(runs[u])
            except StopIteration as done:
                results[u] = done.value
                live.remove(u)
    return results


def _stage_fwd(step, *, name, rows, chunk, nc, groups, xs, params, consts, carries, ys, gpb=1):
    def fstep(loaded, cv):
        outs = _run_interleaved(step, [(carry, xv, pv, cv) for carry, xv, _, pv in loaded])
        return [(new_carry, yv, carry, []) for (new_carry, yv), (carry, _, _, _) in zip(outs, loaded)]

    yv, saved, _ = _scan_call(fstep, name=name, rows=rows, chunk=chunk, nc=nc, groups=groups, xs=xs, params=params,
                              consts=consts, carries=carries, ys=ys, couts=carries, gpb=gpb, multi=True)
    return yv, saved


def _stage_bwd(step, *, name, rows, chunk, nc, groups, xs, saved, params, consts, carries, dys, dxs, gpb=1):
    n_x = len(xs)

    def bstep(loaded, cv):
        civs = [list(civ) for _, _, civ, _ in loaded]
        xvs = [list(xv_all[:n_x]) for _, xv_all, _, _ in loaded]
        pvs = [list(pv) for _, _, _, pv in loaded]
        cts = [(list(dcarry), [d.astype(F32) for d in xv_all[n_x:]]) for dcarry, xv_all, _, _ in loaded]

        def fwd(civs_, xvs_, pvs_):
            outs = _run_interleaved(step, [(c_, x_, p_, cv) for c_, x_, p_ in zip(civs_, xvs_, pvs_)])
            return [(list(new_carry), list(yv)) for new_carry, yv in outs]

        _, vjp = jax.vjp(fwd, civs, xvs, pvs)
        dcivs, dxvs, dpvs = vjp(cts)
        return [(dc, dx, [], dp) for dc, dx, dp in zip(dcivs, dxvs, dpvs)]

    dxv, _, dpv = _scan_call(bstep, name=name, rows=rows, chunk=chunk, nc=nc, groups=groups, xs=list(xs) + list(dys),
                             cins=saved, params=params, consts=consts, carries=carries,
                             ys=[(w, dt) for (_, w), dt in zip(xs, dxs)], accs=[a.shape[1:] for a in params],
                             reverse=True, gpb=gpb, multi=True)
    return dxv, dpv


def _mm_params(sem):
    return pltpu.CompilerParams(dimension_semantics=sem, vmem_limit_bytes=VMEM_LIMIT)


def _after(dep):
    return ([], []) if dep is None else ([dep], [pl.BlockSpec(memory_space=pl.ANY)])


def _matmul_nt(a, b, *, name, dep=None):
    m, k = a.shape
    n = b.shape[0]
    tm = min(1024, m)
    tn = 1024 if n % 1024 == 0 else 512
    deps, dep_specs = _after(dep)

    def body(a_ref, b_ref, *rest):
        rest[-1][...] = lax.dot_general(a_ref[...], b_ref[...], (_DIMS["nt"], ((), ())), preferred_element_type=F32)

    return pl.pallas_call(
        body, name=name, grid=(m // tm, n // tn),
        in_specs=[pl.BlockSpec((tm, k), lambda i, j: (i, 0)), pl.BlockSpec((tn, k), lambda i, j: (j, 0))] + dep_specs,
        out_specs=pl.BlockSpec((tm, tn), lambda i, j: (i, j)),
        out_shape=jax.ShapeDtypeStruct((m, n), F32),
        compiler_params=_mm_params(("parallel", "arbitrary")),
    )(a, b, *deps)


def _matmul_nn(a, b, *, name, dep=None):
    m, k = a.shape
    n = b.shape[1]
    tm, tn = (1024, 1024) if k <= 4096 else (1024, 512) if k <= 6144 else (512, 512) if k <= 8192 else (512, 256)
    tm = min(tm, m)
    deps, dep_specs = _after(dep)

    def body(a_ref, b_ref, *rest):
        rest[-1][...] = jnp.dot(a_ref[...], b_ref[...], preferred_element_type=F32)

    return pl.pallas_call(
        body, name=name, grid=(m // tm, n // tn),
        in_specs=[pl.BlockSpec((tm, k), lambda i, j: (i, 0)), pl.BlockSpec((k, tn), lambda i, j: (0, j))] + dep_specs,
        out_specs=pl.BlockSpec((tm, tn), lambda i, j: (i, j)),
        out_shape=jax.ShapeDtypeStruct((m, n), F32),
        compiler_params=_mm_params(("parallel", "arbitrary")),
    )(a, b, *deps)


def _matmul_tn(x, y, *, name, tp=512, tq=512):
    t, p = x.shape
    q = y.shape[1]

    def body(x_ref, y_ref, o_ref):
        o_ref[...] = lax.dot_general(x_ref[...], y_ref[...], (_DIMS["tn"], ((), ())),
                                     preferred_element_type=F32).astype(o_ref.dtype)

    return pl.pallas_call(
        body, name=name, grid=(p // tp, q // tq),
        in_specs=[pl.BlockSpec((t, tp), lambda i, j: (0, i)), pl.BlockSpec((t, tq), lambda i, j: (0, j))],
        out_specs=pl.BlockSpec((tp, tq), lambda i, j: (i, j)),
        out_shape=jax.ShapeDtypeStruct((p, q), BF16),
        compiler_params=_mm_params(("parallel", "arbitrary")),
    )(x, y)


N_CHIPS = N_DEV // 2


def _all_gather(arrays, *, name):
    n = len(arrays)
    out_shape = [jax.ShapeDtypeStruct((N_DEV,) + tuple(a.shape), a.dtype) for a in arrays]

    def body(*refs):
        in_refs, out_refs = refs[:n], refs[n:2 * n]
        send_sems, recv_sems, local_sems = refs[2 * n:]
        x, y, c = lax.axis_index("x"), lax.axis_index("y"), lax.axis_index("c")
        me, sibling = (x, y, c), (x, y, 1 - c)
        chips = [(1 - x, y), (x, 1 - y), (1 - x, 1 - y)]

        def copy(a, k, block, to, src=None):
            slot = out_refs[a].at[4 * block[0] + 2 * block[1] + block[2]]
            return pltpu.make_async_remote_copy(
                src_ref=slot if src is None else src, dst_ref=slot, send_sem=send_sems.at[a, k],
                recv_sem=recv_sems.at[a, k], device_id=to, device_id_type=pl.DeviceIdType.MESH)

        mine = [pltpu.make_async_copy(in_refs[a], out_refs[a].at[4 * x + 2 * y + c], local_sems.at[a]) for a in range(n)]
        first = []
        for a in range(n):
            first.append(copy(a, 0, me, sibling, src=in_refs[a]))
            first += [copy(a, 1 + j, me, (*chip, c), src=in_refs[a]) for j, chip in enumerate(chips)]
        for cp in mine + first:
            cp.start()
        passed = []
        for j, chip in enumerate(chips):
            for a in range(n):
                copy(a, 1 + j, (*chip, c), me).wait_recv()
                passed.append(copy(a, 4 + j, (*chip, c), sibling))
                passed[-1].start()
        for a in range(n):
            copy(a, 0, sibling, me).wait_recv()
            for j, chip in enumerate(chips):
                copy(a, 4 + j, (*chip, 1 - c), me).wait_recv()
        for cp in first + passed:
            cp.wait_send()
        for cp in mine:
            cp.wait()

    any_spec = pl.BlockSpec(memory_space=pl.ANY)
    return pl.pallas_call(
        body, name=name, in_specs=[any_spec] * n, out_specs=[any_spec] * n, out_shape=out_shape,
        scratch_shapes=[pltpu.SemaphoreType.DMA((n, N_DEV - 1)), pltpu.SemaphoreType.DMA((n, N_DEV - 1)),
                        pltpu.SemaphoreType.DMA((n,))],
        compiler_params=pltpu.CompilerParams(has_side_effects=True),
    )(*arrays)


def _push(arrays, *, name, plan, out_slots=None):
    n = len(arrays)
    in_place = out_slots is None
    out_shape = [jax.ShapeDtypeStruct(((a.shape[0] if in_place else out_slots),) + tuple(a.shape[1:]), a.dtype) for a in arrays]
    n_tr = len(plan(0, 0, 0)[0])

    def body(*refs):
        in_refs, out_refs = refs[:n], refs[n:2 * n]
        send_sems, recv_sems, local_sems = refs[2 * n:]
        src_refs = out_refs if in_place else in_refs
        transfers, local = plan(lax.axis_index("x"), lax.axis_index("y"), lax.axis_index("c"))
        copies = []
        for a in range(n):
            if local is not None:
                copies.append(pltpu.make_async_copy(src_refs[a].at[local[0]], out_refs[a].at[local[1]], local_sems.at[a]))
            for k, (peer, src, dst) in enumerate(transfers):
                copies.append(pltpu.make_async_remote_copy(
                    src_ref=src_refs[a].at[src], dst_ref=out_refs[a].at[dst], send_sem=send_sems.at[a, k],
                    recv_sem=recv_sems.at[a, k], device_id=peer, device_id_type=pl.DeviceIdType.MESH))
        for cp in copies:
            cp.start()
        for cp in copies:
            cp.wait()

    any_spec = pl.BlockSpec(memory_space=pl.ANY)
    return pl.pallas_call(
        body, name=name, in_specs=[any_spec] * n, out_specs=[any_spec] * n, out_shape=out_shape,
        input_output_aliases={a: a for a in range(n)} if in_place else {},
        scratch_shapes=[pltpu.SemaphoreType.DMA((n, n_tr)), pltpu.SemaphoreType.DMA((n, n_tr)),
                        pltpu.SemaphoreType.DMA((n,))],
        compiler_params=pltpu.CompilerParams(has_side_effects=True),
    )(*arrays)


_HBM_SPEC = pl.BlockSpec(memory_space=pltpu.HBM)
_SEM_SPEC = pl.BlockSpec(memory_space=pltpu.SEMAPHORE)
_DATAFLOW = pltpu.SideEffectType.DATAFLOW_SIDE_EFFECTING


def _push_start(sources, landing, *, name, plan, after=None):
    n = len(sources)
    n_tr = len(plan(0, 0, 0)[0])
    deps, dep_specs = _after(after)

    def body(*refs):
        src_refs, land_refs = refs[:n], refs[n:2 * n]
        o = 2 * n + len(deps)
        send_sems, recv_sems, token = refs[o], refs[o + 1], refs[-1]
        transfers, _ = plan(lax.axis_index("x"), lax.axis_index("y"), lax.axis_index("c"))
        for a in range(n):
            for k, (peer, src, dst) in enumerate(transfers):
                pltpu.make_async_remote_copy(
                    src_ref=src_refs[a].at[src], dst_ref=land_refs[a].at[dst], send_sem=send_sems.at[a * n_tr + k],
                    recv_sem=recv_sems.at[a * n_tr + k], device_id=peer, device_id_type=pl.DeviceIdType.MESH).start()
        token[...] = jnp.zeros(token.shape, token.dtype)

    hbm = lambda a: pltpu.HBM(a.shape, a.dtype)
    outs = pl.pallas_call(
        body, name=name,
        out_shape=(pltpu.SemaphoreType.DMA((n * n_tr,)), pltpu.SemaphoreType.DMA((n * n_tr,)), *[hbm(a) for a in sources],
                   *[hbm(a) for a in landing], jax.ShapeDtypeStruct((8, 128), F32)),
        in_specs=[_HBM_SPEC] * (2 * n) + dep_specs,
        out_specs=(_SEM_SPEC, _SEM_SPEC, *[_HBM_SPEC] * (2 * n), pl.BlockSpec(memory_space=pltpu.VMEM)),
        input_output_aliases={i: 2 + i for i in range(2 * n)},
        compiler_params=pltpu.CompilerParams(has_side_effects=_DATAFLOW),
    )(*[pltpu.with_memory_space_constraint(a, pltpu.HBM) for a in list(sources) + list(landing)], *deps)
    return outs[0], outs[1], list(outs[2:2 + n]), list(outs[2 + n:2 + 2 * n]), outs[-1]


def _push_wait(handles, after, *, name, plan):
    send_sems, recv_sems, sources, landing, _ = handles
    n = len(sources)
    after = list(after) if isinstance(after, (list, tuple)) else [after]

    def body(*refs):
        src_refs, land_refs = refs[:n], refs[n:2 * n]
        send_sems_, recv_sems_ = refs[2 * n], refs[2 * n + 1]
        transfers, _ = plan(lax.axis_index("x"), lax.axis_index("y"), lax.axis_index("c"))
        n_tr = len(transfers)
        for a in range(n):
            for k, (peer, src, dst) in enumerate(transfers):
                cp = pltpu.make_async_remote_copy(
                    src_ref=src_refs[a].at[src], dst_ref=land_refs[a].at[dst], send_sem=send_sems_.at[a * n_tr + k],
                    recv_sem=recv_sems_.at[a * n_tr + k], device_id=peer, device_id_type=pl.DeviceIdType.MESH)
                cp.wait_send()
                cp.wait_recv()

    hbm = lambda a: pltpu.HBM(a.shape, a.dtype)
    outs = pl.pallas_call(
        body, name=name, out_shape=tuple(hbm(a) for a in list(sources) + list(landing)),
        in_specs=[_HBM_SPEC] * (2 * n) + [_SEM_SPEC, _SEM_SPEC] + [pl.BlockSpec(memory_space=pl.ANY)] * len(after),
        out_specs=[_HBM_SPEC] * (2 * n), input_output_aliases={i: i for i in range(2 * n)},
        compiler_params=pltpu.CompilerParams(has_side_effects=_DATAFLOW),
    )(*sources, *landing, send_sems, recv_sems, *after)
    return list(outs[n:])


def _plan_everyone(x, y, c):
    me = 4 * x + 2 * y + c
    peers = [(1 - x if k & 4 else x, 1 - y if k & 2 else y, 1 - c if k & 1 else c) for k in range(1, N_DEV)]
    return [(p, 0, me) for p in peers], (0, me)


def _plan_sibling(x, y, c):
    return [((x, y, 1 - c), 2 * chip + (1 - c), chip) for chip in range(N_CHIPS)], None


def _plan_chips(x, y, c):
    mine = 2 * x + y
    peers = [(1 - x, y), (x, 1 - y), (1 - x, 1 - y)]
    return [((px, py, c), 2 * px + py, mine) for px, py in peers], (mine, mine)


def _plan_own_block(x, y, c):
    me = 4 * x + 2 * y + c
    peers = [(x, y, 1 - c), (1 - x, y, c), (x, 1 - y, c), (1 - x, 1 - y, c)]
    return [(p, 0, me) for p in peers], None


def _plan_pass_on(x, y, c):
    slots = [4 * px + 2 * py + c for px, py in ((1 - x, y), (x, 1 - y), (1 - x, 1 - y))]
    return [((x, y, 1 - c), s, s) for s in slots], None


def _pair_sum(parts, received, *, name, tc):
    _, r, c = parts.shape
    core = lax.axis_index("c").astype(jnp.int32).reshape(1)

    def body(core_ref, p_ref, r_ref, o_ref, o2_ref):
        s = (p_ref[...].astype(F32) + r_ref[...].astype(F32)).astype(o_ref.dtype)
        o_ref[...] = s
        o2_ref[...] = s

    out = pl.BlockSpec((None, r, tc), lambda i, j, core_ref: (i, 0, j))
    return pl.pallas_call(
        body, name=name,
        grid_spec=pltpu.PrefetchScalarGridSpec(
            num_scalar_prefetch=1, grid=(N_CHIPS, c // tc),
            in_specs=[pl.BlockSpec((None, r, tc), lambda i, j, core_ref: (2 * i + core_ref[0], 0, j)),
                      pl.BlockSpec((None, r, tc), lambda i, j, core_ref: (i, 0, j))],
            out_specs=[out, out]),
        out_shape=[jax.ShapeDtypeStruct((N_CHIPS, r, c), BF16)] * 2,
        compiler_params=pltpu.CompilerParams(dimension_semantics=("parallel", "parallel"), vmem_limit_bytes=VMEM_LIMIT),
    )(core, parts, received)


def _sum_blocks(a, *, name, tc):
    nblk, r, c = a.shape

    def body(a_ref, o_ref):
        acc = a_ref[0].astype(F32)
        for i in range(1, nblk):
            acc = acc + a_ref[i].astype(F32)
        o_ref[...] = acc

    return pl.pallas_call(
        body, name=name, grid=(c // tc,),
        in_specs=[pl.BlockSpec((nblk, r, tc), lambda j: (0, 0, j))],
        out_specs=pl.BlockSpec((r, tc), lambda j: (0, j)),
        out_shape=jax.ShapeDtypeStruct((r, c), F32),
        compiler_params=pltpu.CompilerParams(dimension_semantics=("parallel",), vmem_limit_bytes=VMEM_LIMIT),
    )(a)


def _adamw(w, g, m, v, *, name, tr):
    r, c = w.shape

    def body(w_ref, g_ref, m_ref, v_ref, d_ref, mo_ref, vo_ref):
        gv = g_ref[...]
        mn = ADAM_B1 * m_ref[...] + (1.0 - ADAM_B1) * gv
        vn = ADAM_B2 * v_ref[...] + (1.0 - ADAM_B2) * jnp.square(gv)
        m_hat = mn / (1.0 - ADAM_B1 ** ADAM_STEP)
        v_hat = vn / (1.0 - ADAM_B2 ** ADAM_STEP)
        d_ref[...] = -ADAM_LR * (m_hat / (jnp.sqrt(v_hat) + ADAM_EPS) + ADAM_WD * w_ref[...])
        mo_ref[...] = mn
        vo_ref[...] = vn

    spec = pl.BlockSpec((tr, c), lambda i: (i, 0))
    return pl.pallas_call(
        body, name=name, grid=(r // tr,), in_specs=[spec] * 4, out_specs=[spec] * 3,
        out_shape=[jax.ShapeDtypeStruct((r, c), F32)] * 3,
        compiler_params=pltpu.CompilerParams(dimension_semantics=("parallel",), vmem_limit_bytes=VMEM_LIMIT),
    )(w, g, m, v)


def _in_proj_layout():
    z0, xbc0, dt0, gate0 = 8192, 12288, 18432, 18496
    hg = []
    for h in range(HG_HEADS):
        for part in range(4):
            hg.append(part * 2048 + h * HG_DK + np.arange(HG_DK))
    ssm = []
    for g in range(SSM_GROUPS):
        ssm.append(xbc0 + g * SSM_GW + np.arange(SSM_GW))
        ssm.append(xbc0 + SSM_DINNER + g * SSM_DSTATE + np.arange(SSM_DSTATE))
        ssm.append(xbc0 + SSM_DINNER + SSM_GROUPS * SSM_DSTATE + g * SSM_DSTATE + np.arange(SSM_DSTATE))
        ssm.append(np.concatenate([dt0 + g * SSM_HPG + np.arange(SSM_HPG), -np.ones(128 - SSM_HPG, np.int64)]))
        ssm.append(z0 + g * SSM_GW + np.arange(SSM_GW))
    gate = gate0 + np.arange(2 * D_MODEL)
    return np.concatenate(hg), np.concatenate(ssm), gate


def _conv_layout():
    idx = []
    for g in range(SSM_GROUPS):
        idx.append(np.concatenate([g * SSM_GW + np.arange(SSM_GW),
                                   SSM_DINNER + g * SSM_DSTATE + np.arange(SSM_DSTATE),
                                   SSM_DINNER + SSM_GROUPS * SSM_DSTATE + g * SSM_DSTATE + np.arange(SSM_DSTATE)]))
    return np.stack(idx)


def _up_layout():
    idx = []
    for g in range(FFN_G):
        idx.append(g * FFN_GW + np.arange(FFN_GW))
        idx.append(D_FF + g * FFN_GW + np.arange(FFN_GW))
    return np.concatenate(idx)


def _inverse(idx, n):
    inv = np.zeros(n, np.int64)
    pos = np.nonzero(idx >= 0)[0]
    inv[idx[pos]] = pos
    return inv


def _take_rows(a, idx, axis=0):
    idx = np.asarray(idx).reshape(-1)
    pieces, start = [], 0
    for i in range(1, len(idx) + 1):
        same_run = i < len(idx) and ((idx[i] == idx[i - 1] + 1 and idx[i - 1] >= 0) or (idx[i] < 0 and idx[i - 1] < 0))
        if same_run:
            continue
        n = i - start
        if idx[start] < 0:
            shape = list(a.shape)
            shape[axis] = n
            pieces.append(jnp.zeros(shape, a.dtype))
        else:
            pieces.append(lax.slice_in_dim(a, int(idx[start]), int(idx[start]) + n, axis=axis))
        start = i
    return pieces[0] if len(pieces) == 1 else jnp.concatenate(pieces, axis=axis)


def _copy_runs(sources, out_rows, runs, *, name, block):
    d, dtype = sources[0].shape[1], sources[0].dtype
    outs = []
    for o, rows in enumerate(out_rows):
        mine = sorted({i for i, _, oo, _, _ in runs if oo == o})
        ns, nblk = len(mine), rows // block
        sel = np.zeros(nblk, np.int32)
        idx = np.full((ns, nblk), -1, np.int64)
        for i, s, oo, t, n in runs:
            if oo == o:
                assert s % block == 0 and t % block == 0 and n % block == 0
                for b in range(n // block):
                    sel[t // block + b] = mine.index(i)
                    idx[mine.index(i), t // block + b] = s // block + b
        assert (idx.max(axis=0) >= 0).all()
        for i in range(ns):
            first = idx[i, np.nonzero(idx[i] >= 0)[0][0]]
            for b in range(nblk):
                if idx[i, b] < 0:
                    idx[i, b] = idx[i, b - 1] if b > 0 else first

        def body(sel_ref, idx_ref, *refs, ns=ns):
            srcs, out = refs[:ns], refs[ns]
            which = sel_ref[pl.program_id(0)]
            val = srcs[ns - 1][...]
            for i in range(ns - 2, -1, -1):
                val = jnp.where(which == i, srcs[i][...], val)
            out[...] = val

        in_specs = [pl.BlockSpec((block, d), (lambda i_, n_: lambda b, sel_ref, idx_ref: (idx_ref[i_ * n_ + b], 0))(i, nblk))
                    for i in range(ns)]
        outs.append(pl.pallas_call(
            body, name=f"{name}_{o}" if len(out_rows) > 1 else name,
            grid_spec=pltpu.PrefetchScalarGridSpec(
                num_scalar_prefetch=2, grid=(nblk,), in_specs=in_specs,
                out_specs=pl.BlockSpec((block, d), lambda b, sel_ref, idx_ref: (b, 0))),
            out_shape=jax.ShapeDtypeStruct((rows, d), dtype),
            compiler_params=pltpu.CompilerParams(dimension_semantics=("arbitrary",), vmem_limit_bytes=VMEM_LIMIT),
        )(jnp.asarray(sel), jnp.asarray(idx.reshape(-1), jnp.int32), *[sources[i] for i in mine]))
    return outs


_Z0, _XBC0, _DT0, _GATE0 = 8192, 12288, 18432, 18496
_B0, _C0 = _XBC0 + SSM_DINNER, _XBC0 + SSM_DINNER + SSM_GROUPS * SSM_DSTATE


def _in_proj_runs():
    runs = [(part * 2048 + h * HG_DK, 0, h * HG_BLK + part * HG_DK, HG_DK) for h in range(HG_HEADS) for part in range(4)]
    for g in range(SSM_GROUPS):
        base = g * SSM_BLK
        runs += [(_XBC0 + g * SSM_GW, 1, base, SSM_GW), (_B0 + g * SSM_DSTATE, 1, base + SSM_GW, SSM_DSTATE),
                 (_C0 + g * SSM_DSTATE, 1, base + SSM_GW + SSM_DSTATE, SSM_DSTATE),
                 (_Z0 + g * SSM_GW, 1, base + SSM_XBC + 128, SSM_GW)]
    return runs + [(_GATE0, 2, 0, 2 * D_MODEL)]


def _in_proj_to_kernel(in_t):
    d = in_t.shape[1]
    dt = jnp.pad(in_t[_DT0:_GATE0].reshape(SSM_GROUPS, SSM_HPG, d), ((0, 0), (0, 128 - SSM_HPG), (0, 0)))
    runs = [(0, src, sec, dst, n) for src, sec, dst, n in _in_proj_runs() if sec < 2]
    runs += [(1, g * 128, 1, g * SSM_BLK + SSM_XBC, 128) for g in range(SSM_GROUPS)]
    hg, ssm = _copy_runs([in_t, dt.reshape(SSM_GROUPS * 128, d)], [_Z0, SSM_GROUPS * SSM_BLK], runs,
                         name="in_proj_to_kernel_layout", block=128)
    return hg, ssm, in_t[_GATE0:]


def _in_proj_from_kernel(hg, ssm, gate):
    d = hg.shape[1]
    dt = ssm.reshape(SSM_GROUPS, SSM_BLK, d)[:, SSM_XBC:SSM_XBC + SSM_HPG].reshape(SSM_HEADS, d)
    runs = [(sec, dst, 0, src, n) for src, sec, dst, n in _in_proj_runs()] + [(3, 0, 0, _DT0, SSM_HEADS)]
    return _copy_runs([hg, ssm, gate, dt], [IN_TOTAL], runs, name="in_proj_to_global_layout", block=SSM_HEADS)[0]


def _up_to_kernel(up_t):
    runs = [(0, part * D_FF + g * FFN_GW, 0, (2 * g + part) * FFN_GW, FFN_GW) for g in range(FFN_G) for part in range(2)]
    return _copy_runs([up_t], [2 * D_FF], runs, name="up_to_kernel_layout", block=FFN_GW)[0]


def _up_from_kernel(up):
    runs = [(0, (2 * g + part) * FFN_GW, 0, part * D_FF + g * FFN_GW, FFN_GW) for g in range(FFN_G) for part in range(2)]
    return _copy_runs([up], [2 * D_FF], runs, name="up_to_global_layout", block=FFN_GW)[0]


_SMALL = (("mix_pre_norm", (1, 2048)), ("mix_post_norm", (1, 2048)), ("hg_lb_table", (2, 2048)), ("hg_out_norm", (1, 128)),
          ("ssm_conv_w", (4, 6144)), ("ssm_conv_b", (1, 6144)), ("ssm_dt_bias", (1, 64)), ("ssm_A_log", (1, 64)),
          ("ssm_D", (1, 64)), ("ssm_out_norm", (1, 4096)), ("ffn_pre_norm", (1, 2048)), ("ffn_post_norm", (1, 2048)),
          ("ffn_conv_w", (3, 5632)), ("ffn_conv_b", (1, 5632)), ("loss", (1, 1)))
_PACK_ROWS = 8 * (-(-sum(int(np.prod(s)) for _, s in _SMALL) // 1024))


def _pack(vals):
    flat = jnp.concatenate([vals[k].astype(F32).reshape(-1) for k, _ in _SMALL])
    return jnp.pad(flat, (0, _PACK_ROWS * 128 - flat.shape[0])).reshape(_PACK_ROWS, 128)


def _unpack(packed):
    flat, out, o = packed.reshape(-1), {}, 0
    for k, s in _SMALL:
        n = int(np.prod(s))
        out[k] = flat[o:o + n].reshape(s)
        o += n
    return out


def _local_step(x, target, w, p, late_weights=None, emit=lambda key, gw: None):
    t = x.shape[0]
    one = lambda a: a.reshape((1,) + a.shape)
    row = dict(rows=t, groups=1, consts=[], carries=[])

    (h1,), _ = _stage_fwd(_pre_step, name="pre_fwd", chunk=512, nc=1, xs=[(x, D_MODEL)], params=[one(p["mix_pre_norm"])],
                          ys=[(D_MODEL, BF16)], **row)
    proj_hg = _matmul_nt(h1, w["in_hg"], name="proj_hg")
    proj_ssm = _matmul_nt(h1, w["in_ssm"], name="proj_ssm")
    proj_gate = _matmul_nt(h1, w["in_gate"], name="proj_gate")

    hg = dict(rows=t, chunk=HG_CHUNK, nc=8, groups=HG_HEADS, xs=[(proj_hg, HG_BLK)], params=[p["hg_tab"], p["hg_nw"]],
              consts=_hg_consts(), carries=[(HG_DK, HG_DK)], gpb=8)
    (y_hg,), hg_saved = _stage_fwd(_hg_step, name="hg_fwd", ys=[(HG_DK, BF16)], **hg)

    ssd = dict(rows=t, chunk=SSM_CHUNK, nc=4, groups=SSM_GROUPS, xs=[(proj_ssm, SSM_BLK)],
               params=[p["conv_w"], p["conv_b"], p["dt_bias"], p["a_log"], p["d_skip"], p["ssm_nw"]],
               consts=_ssd_consts(), carries=[(4 * 128, SSM_DSTATE), (HALO, SSM_XBC)])
    (y_ssm,), ssd_saved = _stage_fwd(_ssd_step, name="ssd_fwd", ys=[(SSM_GW, BF16)], **ssd)

    if late_weights is not None:
        w = {**w, **late_weights([y_hg, y_ssm])}
    u_hg = _matmul_nn(y_hg, w["branch_hg"], name="branch_hg")
    u_ssm = _matmul_nn(y_ssm, w["branch_ssm"], name="branch_ssm")
    mix = dict(chunk=256, nc=1, xs=[(proj_gate, 2 * D_MODEL), (u_hg, D_MODEL), (u_ssm, D_MODEL)], params=[], **row)
    (mixed,), _ = _stage_fwd(_mix_step, name="mix_fwd", ys=[(D_MODEL, BF16)], **mix)
    v = _matmul_nn(mixed, w["out"], name="out_proj")
    post = dict(chunk=256, nc=1, xs=[(x, D_MODEL), (v, D_MODEL)],
                params=[one(p["mix_post_norm"]), one(p["ffn_pre_norm"])], **row)
    (x1, h2), _ = _stage_fwd(_post_step, name="post_fwd", ys=[(D_MODEL, F32), (D_MODEL, BF16)], **post)
    gu = _matmul_nt(h2, w["up"], name="ffn_up")
    ffn = dict(rows=t, chunk=256, nc=2, groups=FFN_G, xs=[(gu, 2 * FFN_GW)], params=[p["ffn_conv_w"], p["ffn_conv_b"]],
               consts=[], carries=[(HALO, FFN_GW)])
    (act,), ffn_saved = _stage_fwd(_ffn_step, name="ffn_fwd", ys=[(FFN_GW, BF16)], **ffn)
    d = _matmul_nn(act, w["down"], name="ffn_down")

    def head_step(carry, xv, civ, pv, cv):
        x1_, d_, tgt = xv

        def per_row_loss(a, b, nw):
            e = a + _rms(b, nw) - tgt
            return 0.5 * jnp.mean(e * e, axis=1, keepdims=True)

        lrow, vjp = jax.vjp(per_row_loss, x1_, d_, pv[0])
        dx1_, dd_, dnw = vjp(jnp.ones_like(lrow))
        loss = jnp.broadcast_to(jnp.sum(lrow, axis=0, keepdims=True), (1, 128))
        return [], [dx1_, dd_], [], [dnw, loss]

    (dy, dd), _, (g_ffn_post, loss) = _scan_call(
        head_step, name="loss_head", chunk=256, nc=1, xs=[(x1, D_MODEL), (d, D_MODEL), (target, D_MODEL)],
        params=[one(p["ffn_post_norm"])], ys=[(D_MODEL, F32), (D_MODEL, BF16)], accs=[(1, D_MODEL), (1, 128)], **row)

    gw = {}
    gw["down"] = _matmul_tn(act, dd, name="g_down")
    dact = _matmul_nt(dd, w["down"], name="d_act", dep=emit("down", gw))
    (dgu,), (g_fcw, g_fcb) = _stage_bwd(_ffn_step, name="ffn_bwd", saved=ffn_saved, dys=[(dact, FFN_GW)], dxs=[BF16], **ffn)
    gw["up"] = _matmul_tn(dgu, h2, name="g_up")
    dh2 = _matmul_nn(dgu, w["up"], name="d_h2", dep=emit("up", gw))
    (dx1, dv), (g_mix_post, g_ffn_pre) = _stage_bwd(_post_step, name="post_bwd", saved=[], dys=[(dy, D_MODEL), (dh2, D_MODEL)],
                                                    dxs=[F32, BF16], **post)
    gw["out"] = _matmul_tn(mixed, dv, name="g_out")
    dmixed = _matmul_nt(dv, w["out"], name="d_mixed")
    (dgate, du_hg, du_ssm), _ = _stage_bwd(_mix_step, name="mix_bwd", saved=[], dys=[(dmixed, D_MODEL)],
                                           dxs=[BF16, BF16, BF16], **mix)
    gw["in_gate"] = _matmul_tn(dgate, h1, name="g_in_gate")
    gw["branch_hg"] = _matmul_tn(y_hg, du_hg, name="g_branch_hg")
    gw["branch_ssm"] = _matmul_tn(y_ssm, du_ssm, name="g_branch_ssm")
    dy_hg = _matmul_nt(du_hg, w["branch_hg"], name="d_y_hg", dep=emit("branches", gw))
    dy_ssm = _matmul_nt(du_ssm, w["branch_ssm"], name="d_y_ssm")
    (dproj_ssm,), g_ssd = _stage_bwd(_ssd_step, name="ssd_bwd", saved=ssd_saved, dys=[(dy_ssm, SSM_GW)], dxs=[BF16], **ssd)
    gw["in_ssm"] = _matmul_tn(dproj_ssm, h1, name="g_in_ssm")
    (dproj_hg,), (g_tab, g_hg_nw) = _stage_bwd(_hg_step, name="hg_bwd", saved=hg_saved, dys=[(dy_hg, HG_DK)], dxs=[BF16], **hg)
    gw["in_hg"] = _matmul_tn(dproj_hg, h1, name="g_in_hg")
    dh_a = _matmul_nn(dproj_hg, w["in_hg"], name="d_h1_hg", dep=emit("in", gw))
    dh_b = _matmul_nn(dproj_ssm, w["in_ssm"], name="d_h1_ssm")
    dh_c = _matmul_nn(dgate, w["in_gate"], name="d_h1_gate")

    def pre_bwd_step(carry, xv, civ, pv, cv):
        x_, da, db, dc, dres = xv
        _, vjp = jax.vjp(_rms, x_, pv[0])
        dx_, dnw = vjp(da + db + dc)
        return [], [dx_ + dres], [], [dnw]

    (grad_x,), _, (g_mix_pre,) = _scan_call(
        pre_bwd_step, name="pre_bwd", chunk=256, nc=1,
        xs=[(x, D_MODEL), (dh_a, D_MODEL), (dh_b, D_MODEL), (dh_c, D_MODEL), (dx1, D_MODEL)],
        params=[one(p["mix_pre_norm"])], ys=[(D_MODEL, F32)], accs=[(1, D_MODEL)], **row)

    gp = dict(mix_pre_norm=g_mix_pre[0], mix_post_norm=g_mix_post[0], ffn_pre_norm=g_ffn_pre[0], ffn_post_norm=g_ffn_post[0],
              hg_tab=g_tab, hg_nw=g_hg_nw, conv_w=g_ssd[0], conv_b=g_ssd[1], dt_bias=g_ssd[2], a_log=g_ssd[3],
              d_skip=g_ssd[4], ssm_nw=g_ssd[5], ffn_conv_w=g_fcw, ffn_conv_b=g_fcb, loss=loss[0, :, :1])
    return grad_x, gw, gp


def _small_to_kernel_layout(s):
    conv_idx = _conv_layout()
    pad_heads = lambda a: jnp.pad(a.reshape(SSM_GROUPS, 1, SSM_HPG), ((0, 0), (0, 0), (0, 128 - SSM_HPG)))
    return dict(
        mix_pre_norm=s["mix_pre_norm"], mix_post_norm=s["mix_post_norm"], ffn_pre_norm=s["ffn_pre_norm"],
        ffn_post_norm=s["ffn_post_norm"],
        hg_tab=s["hg_lb_table"].reshape(2, HG_HEADS, HG_DK).transpose(1, 0, 2),
        hg_nw=jnp.broadcast_to(s["hg_out_norm"].reshape(1, 1, HG_DK), (HG_HEADS, 1, HG_DK)),
        conv_w=_take_rows(s["ssm_conv_w"], conv_idx, axis=1).reshape(SSM_CONV, SSM_GROUPS, SSM_XBC).transpose(1, 0, 2),
        conv_b=_take_rows(s["ssm_conv_b"], conv_idx, axis=1).reshape(SSM_GROUPS, 1, SSM_XBC),
        dt_bias=pad_heads(s["ssm_dt_bias"]), a_log=pad_heads(s["ssm_A_log"]),
        d_skip=jnp.repeat(s["ssm_D"].reshape(SSM_HEADS), SSM_HEADDIM).reshape(SSM_GROUPS, 1, SSM_GW),
        ssm_nw=s["ssm_out_norm"].reshape(SSM_GROUPS, 1, SSM_GW),
        ffn_conv_w=s["ffn_conv_w"].reshape(FFN_CONV, FFN_G, FFN_GW).transpose(1, 0, 2),
        ffn_conv_b=s["ffn_conv_b"].reshape(FFN_G, 1, FFN_GW),
    )


def _small_from_kernel_layout(g):
    conv_inv = _inverse(_conv_layout().reshape(-1), SSM_CONV_DIM)
    heads = lambda a: a[:, 0, :SSM_HPG].reshape(1, SSM_HEADS)
    return dict(
        mix_pre_norm=g["mix_pre_norm"], mix_post_norm=g["mix_post_norm"], ffn_pre_norm=g["ffn_pre_norm"],
        ffn_post_norm=g["ffn_post_norm"],
        hg_lb_table=g["hg_tab"].transpose(1, 0, 2).reshape(2, HG_HEADS * HG_DK),
        hg_out_norm=jnp.sum(g["hg_nw"], axis=0),
        ssm_conv_w=_take_rows(g["conv_w"].transpose(1, 0, 2).reshape(SSM_CONV, -1), conv_inv, axis=1),
        ssm_conv_b=_take_rows(g["conv_b"].reshape(1, -1), conv_inv, axis=1),
        ssm_dt_bias=heads(g["dt_bias"]), ssm_A_log=heads(g["a_log"]),
        ssm_D=jnp.sum(g["d_skip"].reshape(SSM_HEADS, SSM_HEADDIM), axis=1).reshape(1, SSM_HEADS),
        ssm_out_norm=g["ssm_nw"].reshape(1, SSM_DINNER),
        ffn_conv_w=g["ffn_conv_w"].transpose(1, 0, 2).reshape(FFN_CONV, D_FF),
        ffn_conv_b=g["ffn_conv_b"].reshape(1, D_FF),
        loss=g["loss"],
    )


def kernel(x, w_in, mix_pre_norm, mix_post_norm, hg_lb_table, hg_out_norm, ssm_conv_w, ssm_conv_b, ssm_dt_bias, ssm_A_log, ssm_D, ssm_out_norm, w_branch_hg, w_branch_ssm, w_out, ffn_pre_norm, ffn_post_norm, ffn_w_up, ffn_conv_w, ffn_conv_b, ffn_w_down, loss_target, m_w_in, m_mix_pre_norm, m_mix_post_norm, m_hg_lb_table, m_hg_out_norm, m_ssm_conv_w, m_ssm_conv_b, m_ssm_dt_bias, m_ssm_A_log, m_ssm_D, m_ssm_out_norm, m_w_branch_hg, m_w_branch_ssm, m_w_out, m_ffn_pre_norm, m_ffn_post_norm, m_ffn_w_up, m_ffn_conv_w, m_ffn_conv_b, m_ffn_w_down, v_w_in, v_mix_pre_norm, v_mix_post_norm, v_hg_lb_table, v_hg_out_norm, v_ssm_conv_w, v_ssm_conv_b, v_ssm_dt_bias, v_ssm_A_log, v_ssm_D, v_ssm_out_norm, v_w_branch_hg, v_w_branch_ssm, v_w_out, v_ffn_pre_norm, v_ffn_post_norm, v_ffn_w_up, v_ffn_conv_w, v_ffn_conv_b, v_ffn_w_down):
    names = ["w_in", "mix_pre_norm", "mix_post_norm", "hg_lb_table", "hg_out_norm", "ssm_conv_w", "ssm_conv_b", "ssm_dt_bias",
             "ssm_A_log", "ssm_D", "ssm_out_norm", "w_branch_hg", "w_branch_ssm", "w_out", "ffn_pre_norm", "ffn_post_norm",
             "ffn_w_up", "ffn_conv_w", "ffn_conv_b", "ffn_w_down"]
    ws = dict(zip(names, (w_in, mix_pre_norm, mix_post_norm, hg_lb_table, hg_out_norm, ssm_conv_w, ssm_conv_b, ssm_dt_bias,
                          ssm_A_log, ssm_D, ssm_out_norm, w_branch_hg, w_branch_ssm, w_out, ffn_pre_norm, ffn_post_norm,
                          ffn_w_up, ffn_conv_w, ffn_conv_b, ffn_w_down)))
    ms = dict(zip(names, (m_w_in, m_mix_pre_norm, m_mix_post_norm, m_hg_lb_table, m_hg_out_norm, m_ssm_conv_w, m_ssm_conv_b,
                          m_ssm_dt_bias, m_ssm_A_log, m_ssm_D, m_ssm_out_norm, m_w_branch_hg, m_w_branch_ssm, m_w_out,
                          m_ffn_pre_norm, m_ffn_post_norm, m_ffn_w_up, m_ffn_conv_w, m_ffn_conv_b, m_ffn_w_down)))
    vs = dict(zip(names, (v_w_in, v_mix_pre_norm, v_mix_post_norm, v_hg_lb_table, v_hg_out_norm, v_ssm_conv_w, v_ssm_conv_b,
                          v_ssm_dt_bias, v_ssm_A_log, v_ssm_D, v_ssm_out_norm, v_w_branch_hg, v_w_branch_ssm, v_w_out,
                          v_ffn_pre_norm, v_ffn_post_norm, v_ffn_w_up, v_ffn_conv_w, v_ffn_conv_b, v_ffn_w_down)))
    me = 4 * lax.axis_index("x") + 2 * lax.axis_index("y") + lax.axis_index("c")

    late_shards = [ffn_w_up[0].T.astype(BF16), w_branch_hg[0].astype(BF16), w_branch_ssm[0].astype(BF16),
                   w_out[0].astype(BF16), ffn_w_down[0].astype(BF16)]
    landing = [lax.dynamic_update_slice_in_dim(lax.empty((N_DEV,) + s.shape, s.dtype), s[None], me, axis=0)
               for s in late_shards]
    gathered = _all_gather([w_in[0].T.astype(BF16), ssm_conv_w[0], ffn_conv_w[0]], name="gather_in_proj")
    late = _push_start([s[None] for s in late_shards], landing, name="late_weights_start", plan=_plan_own_block,
                       after=gathered[1])
    in_hg, in_ssm, in_gate = _in_proj_to_kernel(gathered[0].reshape(IN_TOTAL, D_MODEL))
    w = dict(in_hg=in_hg, in_ssm=in_ssm, in_gate=in_gate)
    small = {k: ws[k] for k, _ in _SMALL[:-1]}
    small["mix_pre_norm"] = mix_pre_norm + late[4][0, 0]
    small["ssm_conv_w"] = gathered[1].transpose(1, 0, 2).reshape(SSM_CONV, SSM_CONV_DIM)
    small["ffn_conv_w"] = gathered[2].transpose(1, 0, 2).reshape(FFN_CONV, D_FF)
    small = {k: small[k].reshape(s) for k, s in _SMALL[:-1]}

    def late_weights(after):
        landed = _push_wait(late, after, name="late_weights_wait", plan=_plan_own_block)
        up_all, bhg, bssm, out, down = _push(landed, name="late_weights_pass_on", plan=_plan_pass_on)
        return dict(up=_up_to_kernel(up_all.reshape(2 * D_FF, D_MODEL)), branch_hg=bhg.reshape(D_MODEL, D_MODEL),
                    branch_ssm=bssm.reshape(SSM_DINNER, D_MODEL), out=out.reshape(D_MODEL, D_MODEL),
                    down=down.reshape(D_FF, D_MODEL))

    in_flight = []

    def launch(key, named_parts):
        ks, parts = zip(*named_parts)
        from_sibling = _push(list(parts), name="grads_to_sibling_" + key, out_slots=N_CHIPS, plan=_plan_sibling)
        sums = [_pair_sum(p, r, name="pair_sum_" + k, tc=256) for k, p, r in zip(ks, parts, from_sibling)]
        handles = _push_start([q for q, _ in sums], [z for _, z in sums], name="grads_to_chips_start_" + key, plan=_plan_chips)
        in_flight.append((key, ks, handles))
        return handles[4]

    def emit(key, gw):
        blocks = lambda a: a.reshape(N_DEV, -1, D_MODEL)
        if key == "down":
            return launch(key, [("ffn_w_down", blocks(gw["down"]))])
        if key == "up":
            return launch(key, [("ffn_w_up", blocks(_up_from_kernel(gw["up"])))])
        if key == "branches":
            return launch(key, [("w_branch_hg", blocks(gw["branch_hg"])), ("w_branch_ssm", blocks(gw["branch_ssm"])),
                                ("w_out", blocks(gw["out"]))])
        return launch(key, [("w_in", blocks(_in_proj_from_kernel(gw["in_hg"], gw["in_ssm"], gw["in_gate"])))])

    grad_x, gw, gp = _local_step(x[0], loss_target[0], w, _small_to_kernel_layout(small), late_weights, emit)

    big_names = ["w_in", "ffn_w_up", "w_branch_hg", "w_branch_ssm", "w_out", "ffn_w_down"]
    grads = {}
    for key, ks, handles in in_flight:
        landed = _push_wait(handles, grad_x, name="grads_to_chips_wait_" + key, plan=_plan_chips)
        for k, r in zip(ks, landed):
            g = _sum_blocks(r, name="sum_" + k, tc=256)
            grads[k] = g.T if k in ("w_in", "ffn_w_up") else g
    small_all = _push([_pack(_small_from_kernel_layout(gp))[None]], name="small_to_everyone", out_slots=N_DEV,
                      plan=_plan_everyone)
    small_g = _unpack(_sum_blocks(small_all[0], name="sum_small", tc=128))
    loss = small_g.pop("loss").reshape(())
    for k, g in small_g.items():
        if k in ("ssm_conv_w", "ffn_conv_w"):
            n = g.shape[1] // N_DEV
            g = lax.dynamic_slice_in_dim(g, me * n, n, axis=1)
        grads[k] = g

    delta, new_m, new_v = {}, {}, {}
    for k in big_names:
        delta[k], new_m[k], new_v[k] = _adamw(ws[k][0], grads[k], ms[k][0], vs[k][0], name="adamw_" + k, tr=64)
    small_names = [k for k in names if k not in big_names]
    flat = lambda d: jnp.concatenate([d[k].astype(F32).reshape(-1) for k in small_names])
    n_small = sum(int(np.prod(ws[k].shape)) for k in small_names)
    rows = 8 * (-(-n_small // 1024))
    pack2 = lambda d: jnp.pad(flat(d), (0, rows * 128 - n_small)).reshape(rows, 128)
    v_packed = jnp.pad(flat(vs), (0, rows * 128 - n_small), constant_values=1.0).reshape(rows, 128)
    packed = _adamw(pack2(ws), pack2(grads), pack2(ms), v_packed, name="adamw_small", tr=rows)
    o = 0
    for k in small_names:
        n = int(np.prod(ws[k].shape))
        delta[k], new_m[k], new_v[k] = (a.reshape(-1)[o:o + n].reshape(ws[k].shape) for a in packed)
        o += n

    full = lambda d: [d[k].reshape(ws[k].shape) for k in names]
    return (loss, grad_x[None], *full(grads), *full(delta), *full(new_m), *full(new_v))
```

```python
import functools

import numpy as np
import jax
import jax.numpy as jnp
from jax import lax
from jax.experimental import pallas as pl
from jax.experimental.pallas import tpu as pltpu

F32, BF16 = jnp.float32, jnp.bfloat16

D_MODEL = 2048
EPS = 1e-6
HG_HEADS, HG_DK, HG_CHUNK = 16, 128, 64
HG_BLK = 4 * HG_DK
SSM_DINNER, SSM_HEADDIM, SSM_HEADS, SSM_GROUPS, SSM_DSTATE, SSM_CONV = 4096, 64, 64, 8, 128, 4
SSM_CHUNK = 128
SSM_GW = SSM_DINNER // SSM_GROUPS
SSM_HPG = SSM_HEADS // SSM_GROUPS
SSM_XBC = SSM_GW + 2 * SSM_DSTATE
SSM_BLK = SSM_XBC + 128 + SSM_GW
SSM_CONV_DIM = SSM_DINNER + 2 * SSM_GROUPS * SSM_DSTATE
D_FF, FFN_CONV = 5632, 3
FFN_GW = 512
FFN_G = D_FF // FFN_GW
IN_TOTAL = 22592
N_DEV = 8
HALO = 8
VMEM_LIMIT = 52 * 1024 * 1024
ADAM_LR, ADAM_B1, ADAM_B2, ADAM_EPS, ADAM_WD, ADAM_STEP = 0.001, 0.9, 0.999, 1e-08, 0.01, 10

_DIMS = {"nn": ((1,), (0,)), "nt": ((1,), (1,)), "tn": ((0,), (0,))}


def _mm_raw(a, b, mode):
    return lax.dot_general(a.astype(BF16), b.astype(BF16), (_DIMS[mode], ((), ())), preferred_element_type=F32)


@functools.partial(jax.custom_vjp, nondiff_argnums=(2,))
def _mm(a, b, mode):
    return _mm_raw(a, b, mode)


def _mm_fwd(a, b, mode):
    return _mm_raw(a, b, mode), (a, b)


def _mm_bwd(mode, res, dc):
    a, b = res
    if mode == "nn":
        return _mm_raw(dc, b, "nt"), _mm_raw(a, dc, "tn")
    if mode == "nt":
        return _mm_raw(dc, b, "nn"), _mm_raw(dc, a, "tn")
    return _mm_raw(b, dc, "nt"), _mm_raw(a, dc, "nn")


_mm.defvjp(_mm_fwd, _mm_bwd)


def _cmm_raw(m, x, mode):
    hi = x.astype(BF16)
    r1 = x - hi.astype(F32)
    mid = r1.astype(BF16)
    lo = (r1 - mid.astype(F32)).astype(BF16)
    dn = (_DIMS[mode], ((), ()))
    dot = lambda p: lax.dot_general(m, p, dn, preferred_element_type=F32)
    return dot(hi) + dot(mid) + dot(lo)


@jax.custom_vjp
def _cmm(m, x):
    return _cmm_raw(m, x, "nn")


def _cmm_fwd(m, x):
    return _cmm_raw(m, x, "nn"), m


def _cmm_bwd(m, dy):
    return jnp.zeros_like(m), _cmm_raw(m, dy, "tn")


_cmm.defvjp(_cmm_fwd, _cmm_bwd)


@functools.partial(jax.custom_vjp, nondiff_argnums=(1,))
def _sroll(x, s):
    return pltpu.roll(x, s, 0) if s else x


def _sroll_fwd(x, s):
    return _sroll(x, s), None


def _sroll_bwd(s, _, ct):
    return ((pltpu.roll(ct, ct.shape[0] - s, 0) if s else ct),)


_sroll.defvjp(_sroll_fwd, _sroll_bwd)


def _rms(x, w):
    return x * lax.rsqrt(jnp.mean(x * x, axis=-1, keepdims=True) + EPS) * w


def _softplus(x):
    return jnp.maximum(x, 0.0) + jnp.log(1.0 + jnp.exp(-jnp.abs(x)))


def _causal_conv(halo, x, w, b):
    k_taps = w.shape[0]
    xe = jnp.concatenate([halo, x], axis=0)
    out = b
    for k in range(k_taps):
        out = out + w[k:k + 1, :] * _sroll(xe, k_taps - 1 - k)[HALO:, :]
    return out


def _hg_consts():
    c = HG_CHUNK
    t = np.arange(c)
    blocks, pair = [], []
    for m in (32, 16, 8, 4, 2, 1):
        pos = t % (2 * m)
        late = pos >= m
        mid = t - pos + m
        j = t[None, :]
        mq = late[:, None] & (j >= mid[:, None]) & (j <= t[:, None])
        mk = (~late)[:, None] & (j > t[:, None]) & (j <= mid[:, None] - 1)
        blocks.append(mq | mk)
        parent = t // (2 * m)
        pair.append((parent[:, None] == parent[None, :]) & late[:, None] & (~late)[None, :])
    blocks.append(t[None, :] <= t[:, None])
    mall = jnp.asarray(np.concatenate(blocks, 0).astype(np.float32), BF16)
    pair = jnp.asarray(np.stack(pair, 0).astype(np.float32))
    eye = jnp.asarray(np.eye(c, dtype=np.float32))
    return [mall, pair, eye]


def _hg_step(carry, xs, params, consts):
    (st,) = carry
    blk = xs[0].astype(F32)
    tab, nw = params
    mall, pair, eye = consts
    c, dk = HG_CHUNK, HG_DK
    q_raw, f_raw, v, og = blk[:, :dk], blk[:, dk:2 * dk], blk[:, 2 * dk:3 * dk], blk[:, 3 * dk:]
    lb = jax.nn.sigmoid(tab[0:1, :] - tab[1:2, :])
    f = lb + (1.0 - lb) * jax.nn.sigmoid(f_raw)
    g = jnp.log(f)
    kk = 1.0 - f
    qh = jax.nn.silu(q_raw) * (HG_DK ** -0.5)
    yield
    sums = _cmm(mall, g)
    yield
    b = sums[6 * c:, :]
    fac = jnp.exp(sums[:6 * c, :])
    scores = eye * jnp.sum(qh * kk, axis=1, keepdims=True)
    b_last = jnp.sum(g, axis=0, keepdims=True)
    yield
    inter = _mm(qh * jnp.exp(b), st, "nt")
    st_new = st * jnp.exp(b_last) + _mm(v, kk * jnp.exp(b_last - b), "tn")
    yield
    for l in range(6):
        fl = fac[l * c:(l + 1) * c, :]
        scores = scores + pair[l] * _mm(qh * fl, kk * fl, "nt")
        if l % 2:
            yield
    o = _mm(scores, v, "nn") + inter
    yield
    y = _rms(o, nw) * jax.nn.silu(og)
    return [st_new], [y]


def _ssd_consts():
    t = np.arange(SSM_CHUNK)
    tril = (t[None, :] <= t[:, None]).astype(np.float32)
    return [jnp.asarray(tril, BF16), jnp.asarray(tril)]


def _ssd_step(carry, xs, params, consts):
    st, halo = carry
    blk = xs[0].astype(F32)
    conv_w, conv_b, dtb, alog, dskip, nw = params
    tril_b, tril = consts
    c = SSM_CHUNK
    raw, dtr, z = blk[:, :SSM_XBC], blk[:, SSM_XBC:SSM_XBC + 128], blk[:, SSM_XBC + 128:]
    act = jax.nn.silu(_causal_conv(halo, raw, conv_w, conv_b))
    xh, bm, cm = act[:, :SSM_GW], act[:, SSM_GW:SSM_GW + SSM_DSTATE], act[:, SSM_GW + SSM_DSTATE:]
    dt = _softplus(dtr + dtb)
    da = dt * (-jnp.exp(alog))
    acum = _cmm(tril_b, da)
    acum_t = acum.T
    a_last = jnp.sum(da, axis=0, keepdims=True)
    cb_causal = _mm(cm, bm, "nt") * tril
    lane = lax.broadcasted_iota(jnp.int32, (c, 128), 1)
    row = lax.broadcasted_iota(jnp.int32, (128, 128), 0)
    first = lane < SSM_HEADDIM
    ys, st_new = [], []
    for j in range(SSM_HPG // 2):
        xp = xh[:, 128 * j:128 * (j + 1)]
        sp = st[128 * j:128 * (j + 1), :]
        r0, r1 = 2 * j, 2 * j + 1
        col = lambda a, r: jnp.broadcast_to(a[:, r:r + 1], (c, 128))
        xdt = xp * jnp.where(first, col(dt, r0), col(dt, r1))
        yj = _mm(cm, sp, "nt") * jnp.exp(jnp.where(first, col(acum, r0), col(acum, r1)))
        for r, keep in ((r0, first), (r1, ~first)):
            dec = jnp.broadcast_to(acum[:, r:r + 1], (c, c)) - jnp.broadcast_to(acum_t[r:r + 1, :], (c, c))
            m = cb_causal * jnp.exp(jnp.minimum(dec, 0.0))
            yj = yj + _mm(m, jnp.where(keep, xdt, 0.0), "nn")
        al0, al1 = a_last[:, r0:r0 + 1], a_last[:, r1:r1 + 1]
        wts = jnp.exp(jnp.where(first, al0 - col(acum, r0), al1 - col(acum, r1)))
        st_new.append(jnp.where(row < SSM_HEADDIM, jnp.exp(al0), jnp.exp(al1)) * sp + _mm(xdt * wts, bm, "tn"))
        ys.append(yj)
    y = jnp.concatenate(ys, axis=1) + dskip * xh
    y = _rms(y * jax.nn.silu(z), nw)
    return [jnp.concatenate(st_new, axis=0), raw[c - HALO:, :]], [y]


def _ffn_step(carry, xs, params, consts):
    (halo,) = carry
    blk = xs[0].astype(F32)
    conv_w, conv_b = params
    gate, up = blk[:, :FFN_GW], blk[:, FFN_GW:]
    a = jax.nn.gelu(_causal_conv(halo, gate, conv_w, conv_b), approximate=True) * up
    return [gate[gate.shape[0] - HALO:, :]], [a]


def _pre_step(carry, xs, params, consts):
    return [], [_rms(xs[0], params[0])]


def _mix_step(carry, xs, params, consts):
    gates, uh, us = (a.astype(F32) for a in xs)
    return [], [jax.nn.sigmoid(gates[:, :D_MODEL]) * uh + jax.nn.sigmoid(gates[:, D_MODEL:]) * us]


def _post_step(carry, xs, params, consts):
    x, v = xs
    x1 = x + _rms(v, params[0])
    return [], [x1, _rms(x1, params[1])]


def _scan_call(step, *, name, rows, chunk, nc, groups, xs, cins=(), params=(), consts=(), carries=(), ys=(), couts=(),
               accs=(), reverse=False, gpb=1, multi=False):
    blk_rows = chunk * nc
    nb = rows // blk_rows
    n_chunks = rows // chunk
    assert nb * blk_rows == rows and groups % gpb == 0
    rb = (lambda i: nb - 1 - i) if reverse else (lambda i: i)
    n_x, n_ci, n_p, n_c = len(xs), len(cins), len(params), len(consts)
    n_y, n_co, n_a = len(ys), len(couts), len(accs)

    def chunk_spec(shape):
        zeros = (0,) * len(shape)
        return pl.BlockSpec((gpb, nc) + tuple(shape), lambda g, i: (g, rb(i)) + zeros)

    in_specs = [pl.BlockSpec((blk_rows, gpb * w), lambda g, i: (rb(i), g)) for _, w in xs]
    in_specs += [chunk_spec(a.shape[2:]) for a in cins]
    in_specs += [pl.BlockSpec((gpb,) + tuple(a.shape[1:]), lambda g, i: (g, 0, 0)) for a in params]
    in_specs += [pl.BlockSpec(a.shape, (lambda nd: lambda g, i: (0,) * nd)(a.ndim)) for a in consts]
    out_specs = [pl.BlockSpec((blk_rows, gpb * w), lambda g, i: (rb(i), g)) for w, _ in ys]
    out_specs += [chunk_spec(s) for s in couts]
    out_specs += [pl.BlockSpec((gpb, r, c), lambda g, i: (g, 0, 0)) for r, c in accs]
    out_shape = [jax.ShapeDtypeStruct((rows, groups * w), dt) for w, dt in ys]
    out_shape += [jax.ShapeDtypeStruct((groups, n_chunks) + tuple(s), F32) for s in couts]
    out_shape += [jax.ShapeDtypeStruct((groups, r, c), F32) for r, c in accs]
    x_widths = [w for _, w in xs]
    y_widths = [w for w, _ in ys]

    def body(*refs):
        x_refs = refs[:n_x]
        ci_refs = refs[n_x:n_x + n_ci]
        p_refs = refs[n_x + n_ci:n_x + n_ci + n_p]
        c_refs = refs[n_x + n_ci + n_p:n_x + n_ci + n_p + n_c]
        o = n_x + n_ci + n_p + n_c
        y_refs = refs[o:o + n_y]
        co_refs = refs[o + n_y:o + n_y + n_co]
        a_refs = refs[o + n_y + n_co:o + n_y + n_co + n_a]
        carry_refs = refs[o + n_y + n_co + n_a:]

        @pl.when(pl.program_id(1) == 0)
        def _():
            for s in carry_refs:
                s[...] = jnp.zeros(s.shape, F32)
            for a in a_refs:
                a[...] = jnp.zeros(a.shape, F32)

        cvals = [c[...] for c in c_refs]

        def one_chunk(i, _):
            c = (nc - 1 - i) if reverse else i
            r0 = c * chunk if isinstance(c, int) else pl.multiple_of(c * chunk, chunk)
            loaded = []
            for u in range(gpb):
                carry = [s[u] for s in carry_refs]
                xv = [x[pl.ds(r0, chunk), u * w:(u + 1) * w] for x, w in zip(x_refs, x_widths)]
                civ = [ci[u, c] for ci in ci_refs]
                loaded.append((carry, xv, civ, [p[u] for p in p_refs]))
            results = step(loaded, cvals) if multi else [step(*args, cvals) for args in loaded]
            for u, (new_carry, yv, cov, av) in enumerate(results):
                for s, val in zip(carry_refs, new_carry):
                    s[u] = val
                for y, w, val in zip(y_refs, y_widths, yv):
                    y[pl.ds(r0, chunk), u * w:(u + 1) * w] = val.astype(y.dtype)
                for co, val in zip(co_refs, cov):
                    co[u, c] = val
                for a, val in zip(a_refs, av):
                    a[u] += val
            return 0

        if nc == 1:
            one_chunk(0, 0)
        else:
            lax.fori_loop(0, nc, one_chunk, 0)

    outs = pl.pallas_call(
        body, name=name, grid=(groups // gpb, nb), in_specs=in_specs, out_specs=out_specs, out_shape=out_shape,
        scratch_shapes=[pltpu.VMEM((gpb,) + tuple(s), F32) for s in carries],
        compiler_params=pltpu.CompilerParams(dimension_semantics=("arbitrary", "arbitrary"),
                                             vmem_limit_bytes=VMEM_LIMIT),
    )(*[a for a, _ in xs], *cins, *params, *consts)
    return outs[:n_y], outs[n_y:n_y + n_co], outs[n_y + n_co:]


def _run_interleaved(step, arg_tuples):
    runs = [step(*args) for args in arg_tuples]
    if not hasattr(runs[0], "send"):
        return runs
    results, live = [None] * len(runs), list(range(len(runs)))
    while live:
        for u in list(live):
            try:
                next(runs[u])
            except StopIteration as done:
                results[u] = done.value
                live.remove(u)
    return results


def _stage_fwd(step, *, name, rows, chunk, nc, groups, xs, params, consts, carries, ys, gpb=1):
    def fstep(loaded, cv):
        outs = _run_interleaved(step, [(carry, xv, pv, cv) for carry, xv, _, pv in loaded])
        return [(new_carry, yv, carry, []) for (new_carry, yv), (carry, _, _, _) in zip(outs, loaded)]

    yv, saved, _ = _scan_call(fstep, name=name, rows=rows, chunk=chunk, nc=nc, groups=groups, xs=xs, params=params,
                              consts=consts, carries=carries, ys=ys, couts=carries, gpb=gpb, multi=True)
    return yv, saved


def _stage_bwd(step, *, name, rows, chunk, nc, groups, xs, saved, params, consts, carries, dys, dxs, gpb=1):
    n_x = len(xs)

    def bstep(loaded, cv):
        civs = [list(civ) for _, _, civ, _ in loaded]
        xvs = [list(xv_all[:n_x]) for _, xv_all, _, _ in loaded]
        pvs = [list(pv) for _, _, _, pv in loaded]
        cts = [(list(dcarry), [d.astype(F32) for d in xv_all[n_x:]]) for dcarry, xv_all, _, _ in loaded]

        def fwd(civs_, xvs_, pvs_):
            outs = _run_interleaved(step, [(c_, x_, p_, cv) for c_, x_, p_ in zip(civs_, xvs_, pvs_)])
            return [(list(new_carry), list(yv)) for new_carry, yv in outs]

        _, vjp = jax.vjp(fwd, civs, xvs, pvs)
        dcivs, dxvs, dpvs = vjp(cts)
        return [(dc, dx, [], dp) for dc, dx, dp in zip(dcivs, dxvs, dpvs)]

    dxv, _, dpv = _scan_call(bstep, name=name, rows=rows, chunk=chunk, nc=nc, groups=groups, xs=list(xs) + list(dys),
                             cins=saved, params=params, consts=consts, carries=carries,
                             ys=[(w, dt) for (_, w), dt in zip(xs, dxs)], accs=[a.shape[1:] for a in params],
                             reverse=True, gpb=gpb, multi=True)
    return dxv, dpv


def _mm_params(sem):
    return pltpu.CompilerParams(dimension_semantics=sem, vmem_limit_bytes=VMEM_LIMIT)


def _after(dep):
    return ([], []) if dep is None else ([dep], [pl.BlockSpec(memory_space=pl.ANY)])


def _matmul_nt(a, b, *, name, dep=None):
    m, k = a.shape
    n = b.shape[0]
    tm = min(1024, m)
    tn = 1024 if n % 1024 == 0 else 512
    deps, dep_specs = _after(dep)

    def body(a_ref, b_ref, *rest):
        rest[-1][...] = lax.dot_general(a_ref[...], b_ref[...], (_DIMS["nt"], ((), ())), preferred_element_type=F32)

    return pl.pallas_call(
        body, name=name, grid=(m // tm, n // tn),
        in_specs=[pl.BlockSpec((tm, k), lambda i, j: (i, 0)), pl.BlockSpec((tn, k), lambda i, j: (j, 0))] + dep_specs,
        out_specs=pl.BlockSpec((tm, tn), lambda i, j: (i, j)),
        out_shape=jax.ShapeDtypeStruct((m, n), F32),
        compiler_params=_mm_params(("parallel", "arbitrary")),
    )(a, b, *deps)


def _matmul_nn(a, b, *, name, dep=None):
    m, k = a.shape
    n = b.shape[1]
    deps, dep_specs = _after(dep)
    if k > 6144:
        tm, tk, steps = min(512, m), k // 4, 4

        def body_k(a_ref, b_ref, *rest):
            part = jnp.dot(a_ref[...], b_ref[...], preferred_element_type=F32)

            @pl.when(pl.program_id(1) == 0)
            def _():
                rest[-1][...] = part

            @pl.when(pl.program_id(1) != 0)
            def _():
                rest[-1][...] += part

        return pl.pallas_call(
            body_k, name=name, grid=(m // tm, steps),
            in_specs=[pl.BlockSpec((tm, tk), lambda i, j: (i, j)), pl.BlockSpec((tk, n), lambda i, j: (j, 0))] + dep_specs,
            out_specs=pl.BlockSpec((tm, n), lambda i, j: (i, 0)),
            out_shape=jax.ShapeDtypeStruct((m, n), F32),
            compiler_params=_mm_params(("parallel", "arbitrary")),
        )(a, b, *deps)
    tm, tn = (1024, 1024) if k <= 4096 else (1024, 512)
    tm = min(tm, m)

    def body(a_ref, b_ref, *rest):
        rest[-1][...] = jnp.dot(a_ref[...], b_ref[...], preferred_element_type=F32)

    return pl.pallas_call(
        body, name=name, grid=(m // tm, n // tn),
        in_specs=[pl.BlockSpec((tm, k), lambda i, j: (i, 0)), pl.BlockSpec((k, tn), lambda i, j: (0, j))] + dep_specs,
        out_specs=pl.BlockSpec((tm, tn), lambda i, j: (i, j)),
        out_shape=jax.ShapeDtypeStruct((m, n), F32),
        compiler_params=_mm_params(("parallel", "arbitrary")),
    )(a, b, *deps)


def _matmul_tn(x, y, *, name, tp=512, tq=512):
    t, p = x.shape
    q = y.shape[1]

    def body(x_ref, y_ref, o_ref):
        o_ref[...] = lax.dot_general(x_ref[...], y_ref[...], (_DIMS["tn"], ((), ())),
                                     preferred_element_type=F32).astype(o_ref.dtype)

    return pl.pallas_call(
        body, name=name, grid=(p // tp, q // tq),
        in_specs=[pl.BlockSpec((t, tp), lambda i, j: (0, i)), pl.BlockSpec((t, tq), lambda i, j: (0, j))],
        out_specs=pl.BlockSpec((tp, tq), lambda i, j: (i, j)),
        out_shape=jax.ShapeDtypeStruct((p, q), BF16),
        compiler_params=_mm_params(("parallel", "arbitrary")),
    )(x, y)


N_CHIPS = N_DEV // 2


def _all_gather(arrays, *, name):
    n = len(arrays)
    out_shape = [jax.ShapeDtypeStruct((N_DEV,) + tuple(a.shape), a.dtype) for a in arrays]

    def body(*refs):
        in_refs, out_refs = refs[:n], refs[n:2 * n]
        send_sems, recv_sems, local_sems = refs[2 * n:]
        x, y, c = lax.axis_index("x"), lax.axis_index("y"), lax.axis_index("c")
        me, sibling = (x, y, c), (x, y, 1 - c)
        chips = [(1 - x, y), (x, 1 - y), (1 - x, 1 - y)]

        def copy(a, k, block, to, src=None):
            slot = out_refs[a].at[4 * block[0] + 2 * block[1] + block[2]]
            return pltpu.make_async_remote_copy(
                src_ref=slot if src is None else src, dst_ref=slot, send_sem=send_sems.at[a, k],
                recv_sem=recv_sems.at[a, k], device_id=to, device_id_type=pl.DeviceIdType.MESH)

        mine = [pltpu.make_async_copy(in_refs[a], out_refs[a].at[4 * x + 2 * y + c], local_sems.at[a]) for a in range(n)]
        first = []
        for a in range(n):
            first.append(copy(a, 0, me, sibling, src=in_refs[a]))
            first += [copy(a, 1 + j, me, (*chip, c), src=in_refs[a]) for j, chip in enumerate(chips)]
        for cp in mine + first:
            cp.start()
        passed = []
        for j, chip in enumerate(chips):
            for a in range(n):
                copy(a, 1 + j, (*chip, c), me).wait_recv()
                passed.append(copy(a, 4 + j, (*chip, c), sibling))
                passed[-1].start()
        for a in range(n):
            copy(a, 0, sibling, me).wait_recv()
            for j, chip in enumerate(chips):
                copy(a, 4 + j, (*chip, 1 - c), me).wait_recv()
        for cp in first + passed:
            cp.wait_send()
        for cp in mine:
            cp.wait()

    any_spec = pl.BlockSpec(memory_space=pl.ANY)
    return pl.pallas_call(
        body, name=name, in_specs=[any_spec] * n, out_specs=[any_spec] * n, out_shape=out_shape,
        scratch_shapes=[pltpu.SemaphoreType.DMA((n, N_DEV - 1)), pltpu.SemaphoreType.DMA((n, N_DEV - 1)),
                        pltpu.SemaphoreType.DMA((n,))],
        compiler_params=pltpu.CompilerParams(has_side_effects=True),
    )(*arrays)


def _push(arrays, *, name, plan, out_slots=None):
    n = len(arrays)
    in_place = out_slots is None
    out_shape = [jax.ShapeDtypeStruct(((a.shape[0] if in_place else out_slots),) + tuple(a.shape[1:]), a.dtype) for a in arrays]
    n_tr = len(plan(0, 0, 0)[0])

    def body(*refs):
        in_refs, out_refs = refs[:n], refs[n:2 * n]
        send_sems, recv_sems, local_sems = refs[2 * n:]
        src_refs = out_refs if in_place else in_refs
        transfers, local = plan(lax.axis_index("x"), lax.axis_index("y"), lax.axis_index("c"))
        copies = []
        for a in range(n):
            if local is not None:
                copies.append(pltpu.make_async_copy(src_refs[a].at[local[0]], out_refs[a].at[local[1]], local_sems.at[a]))
            for k, (peer, src, dst) in enumerate(transfers):
                copies.append(pltpu.make_async_remote_copy(
                    src_ref=src_refs[a].at[src], dst_ref=out_refs[a].at[dst], send_sem=send_sems.at[a, k],
                    recv_sem=recv_sems.at[a, k], device_id=peer, device_id_type=pl.DeviceIdType.MESH))
        for cp in copies:
            cp.start()
        for cp in copies:
            cp.wait()

    any_spec = pl.BlockSpec(memory_space=pl.ANY)
    return pl.pallas_call(
        body, name=name, in_specs=[any_spec] * n, out_specs=[any_spec] * n, out_shape=out_shape,
        input_output_aliases={a: a for a in range(n)} if in_place else {},
        scratch_shapes=[pltpu.SemaphoreType.DMA((n, n_tr)), pltpu.SemaphoreType.DMA((n, n_tr)),
                        pltpu.SemaphoreType.DMA((n,))],
        compiler_params=pltpu.CompilerParams(has_side_effects=True),
    )(*arrays)


_HBM_SPEC = pl.BlockSpec(memory_space=pltpu.HBM)
_SEM_SPEC = pl.BlockSpec(memory_space=pltpu.SEMAPHORE)
_DATAFLOW = pltpu.SideEffectType.DATAFLOW_SIDE_EFFECTING


def _push_start(sources, landing, *, name, plan, after=None):
    n = len(sources)
    n_tr = len(plan(0, 0, 0)[0])
    deps, dep_specs = _after(after)

    def body(*refs):
        src_refs, land_refs = refs[:n], refs[n:2 * n]
        o = 2 * n + len(deps)
        send_sems, recv_sems, token = refs[o], refs[o + 1], refs[-1]
        transfers, _ = plan(lax.axis_index("x"), lax.axis_index("y"), lax.axis_index("c"))
        for a in range(n):
            for k, (peer, src, dst) in enumerate(transfers):
                pltpu.make_async_remote_copy(
                    src_ref=src_refs[a].at[src], dst_ref=land_refs[a].at[dst], send_sem=send_sems.at[a * n_tr + k],
                    recv_sem=recv_sems.at[a * n_tr + k], device_id=peer, device_id_type=pl.DeviceIdType.MESH).start()
        token[...] = jnp.zeros(token.shape, token.dtype)

    hbm = lambda a: pltpu.HBM(a.shape, a.dtype)
    outs = pl.pallas_call(
        body, name=name,
        out_shape=(pltpu.SemaphoreType.DMA((n * n_tr,)), pltpu.SemaphoreType.DMA((n * n_tr,)), *[hbm(a) for a in sources],
                   *[hbm(a) for a in landing], jax.ShapeDtypeStruct((8, 128), F32)),
        in_specs=[_HBM_SPEC] * (2 * n) + dep_specs,
        out_specs=(_SEM_SPEC, _SEM_SPEC, *[_HBM_SPEC] * (2 * n), pl.BlockSpec(memory_space=pltpu.VMEM)),
        input_output_aliases={i: 2 + i for i in range(2 * n)},
        compiler_params=pltpu.CompilerParams(has_side_effects=_DATAFLOW),
    )(*[pltpu.with_memory_space_constraint(a, pltpu.HBM) for a in list(sources) + list(landing)], *deps)
    return outs[0], outs[1], list(outs[2:2 + n]), list(outs[2 + n:2 + 2 * n]), outs[-1]


def _push_wait(handles, after, *, name, plan):
    send_sems, recv_sems, sources, landing, _ = handles
    n = len(sources)
    after = list(after) if isinstance(after, (list, tuple)) else [after]

    def body(*refs):
        src_refs, land_refs = refs[:n], refs[n:2 * n]
        send_sems_, recv_sems_ = refs[2 * n], refs[2 * n + 1]
        transfers, _ = plan(lax.axis_index("x"), lax.axis_index("y"), lax.axis_index("c"))
        n_tr = len(transfers)
        for a in range(n):
            for k, (peer, src, dst) in enumerate(transfers):
                cp = pltpu.make_async_remote_copy(
                    src_ref=src_refs[a].at[src], dst_ref=land_refs[a].at[dst], send_sem=send_sems_.at[a * n_tr + k],
                    recv_sem=recv_sems_.at[a * n_tr + k], device_id=peer, device_id_type=pl.DeviceIdType.MESH)
                cp.wait_send()
                cp.wait_recv()

    hbm = lambda a: pltpu.HBM(a.shape, a.dtype)
    outs = pl.pallas_call(
        body, name=name, out_shape=tuple(hbm(a) for a in list(sources) + list(landing)),
        in_specs=[_HBM_SPEC] * (2 * n) + [_SEM_SPEC, _SEM_SPEC] + [pl.BlockSpec(memory_space=pl.ANY)] * len(after),
        out_specs=[_HBM_SPEC] * (2 * n), input_output_aliases={i: i for i in range(2 * n)},
        compiler_params=pltpu.CompilerParams(has_side_effects=_DATAFLOW),
    )(*sources, *landing, send_sems, recv_sems, *after)
    return list(outs[n:])


def _plan_everyone(x, y, c):
    me = 4 * x + 2 * y + c
    peers = [(1 - x if k & 4 else x, 1 - y if k & 2 else y, 1 - c if k & 1 else c) for k in range(1, N_DEV)]
    return [(p, 0, me) for p in peers], (0, me)


def _plan_sibling(x, y, c):
    return [((x, y, 1 - c), 2 * chip + (1 - c), chip) for chip in range(N_CHIPS)], None


def _plan_chips(x, y, c):
    mine = 2 * x + y
    peers = [(1 - x, y), (x, 1 - y), (1 - x, 1 - y)]
    return [((px, py, c), 2 * px + py, mine) for px, py in peers], (mine, mine)


def _plan_own_block(x, y, c):
    me = 4 * x + 2 * y + c
    peers = [(x, y, 1 - c), (1 - x, y, c), (x, 1 - y, c), (1 - x, 1 - y, c)]
    return [(p, 0, me) for p in peers], None


def _plan_pass_on(x, y, c):
    slots = [4 * px + 2 * py + c for px, py in ((1 - x, y), (x, 1 - y), (1 - x, 1 - y))]
    return [((x, y, 1 - c), s, s) for s in slots], None


def _pair_sum(parts, received, *, name, tc):
    _, r, c = parts.shape
    core = lax.axis_index("c").astype(jnp.int32).reshape(1)

    def body(core_ref, p_ref, r_ref, o_ref, o2_ref):
        s = (p_ref[...].astype(F32) + r_ref[...].astype(F32)).astype(o_ref.dtype)
        o_ref[...] = s
        o2_ref[...] = s

    out = pl.BlockSpec((None, r, tc), lambda i, j, core_ref: (i, 0, j))
    return pl.pallas_call(
        body, name=name,
        grid_spec=pltpu.PrefetchScalarGridSpec(
            num_scalar_prefetch=1, grid=(N_CHIPS, c // tc),
            in_specs=[pl.BlockSpec((None, r, tc), lambda i, j, core_ref: (2 * i + core_ref[0], 0, j)),
                      pl.BlockSpec((None, r, tc), lambda i, j, core_ref: (i, 0, j))],
            out_specs=[out, out]),
        out_shape=[jax.ShapeDtypeStruct((N_CHIPS, r, c), BF16)] * 2,
        compiler_params=pltpu.CompilerParams(dimension_semantics=("parallel", "parallel"), vmem_limit_bytes=VMEM_LIMIT),
    )(core, parts, received)


def _sum_blocks(a, *, name, tc):
    nblk, r, c = a.shape

    def body(a_ref, o_ref):
        acc = a_ref[0].astype(F32)
        for i in range(1, nblk):
            acc = acc + a_ref[i].astype(F32)
        o_ref[...] = acc

    return pl.pallas_call(
        body, name=name, grid=(c // tc,),
        in_specs=[pl.BlockSpec((nblk, r, tc), lambda j: (0, 0, j))],
        out_specs=pl.BlockSpec((r, tc), lambda j: (0, j)),
        out_shape=jax.ShapeDtypeStruct((r, c), F32),
        compiler_params=pltpu.CompilerParams(dimension_semantics=("parallel",), vmem_limit_bytes=VMEM_LIMIT),
    )(a)


def _adamw(w, g, m, v, *, name, tr):
    r, c = w.shape

    def body(w_ref, g_ref, m_ref, v_ref, d_ref, mo_ref, vo_ref):
        gv = g_ref[...]
        mn = ADAM_B1 * m_ref[...] + (1.0 - ADAM_B1) * gv
        vn = ADAM_B2 * v_ref[...] + (1.0 - ADAM_B2) * jnp.square(gv)
        m_hat = mn / (1.0 - ADAM_B1 ** ADAM_STEP)
        v_hat = vn / (1.0 - ADAM_B2 ** ADAM_STEP)
        d_ref[...] = -ADAM_LR * (m_hat / (jnp.sqrt(v_hat) + ADAM_EPS) + ADAM_WD * w_ref[...])
        mo_ref[...] = mn
        vo_ref[...] = vn

    spec = pl.BlockSpec((tr, c), lambda i: (i, 0))
    return pl.pallas_call(
        body, name=name, grid=(r // tr,), in_specs=[spec] * 4, out_specs=[spec] * 3,
        out_shape=[jax.ShapeDtypeStruct((r, c), F32)] * 3,
        compiler_params=pltpu.CompilerParams(dimension_semantics=("parallel",), vmem_limit_bytes=VMEM_LIMIT),
    )(w, g, m, v)


def _in_proj_layout():
    z0, xbc0, dt0, gate0 = 8192, 12288, 18432, 18496
    hg = []
    for h in range(HG_HEADS):
        for part in range(4):
            hg.append(part * 2048 + h * HG_DK + np.arange(HG_DK))
    ssm = []
    for g in range(SSM_GROUPS):
        ssm.append(xbc0 + g * SSM_GW + np.arange(SSM_GW))
        ssm.append(xbc0 + SSM_DINNER + g * SSM_DSTATE + np.arange(SSM_DSTATE))
        ssm.append(xbc0 + SSM_DINNER + SSM_GROUPS * SSM_DSTATE + g * SSM_DSTATE + np.arange(SSM_DSTATE))
        ssm.append(np.concatenate([dt0 + g * SSM_HPG + np.arange(SSM_HPG), -np.ones(128 - SSM_HPG, np.int64)]))
        ssm.append(z0 + g * SSM_GW + np.arange(SSM_GW))
    gate = gate0 + np.arange(2 * D_MODEL)
    return np.concatenate(hg), np.concatenate(ssm), gate


def _conv_layout():
    idx = []
    for g in range(SSM_GROUPS):
        idx.append(np.concatenate([g * SSM_GW + np.arange(SSM_GW),
                                   SSM_DINNER + g * SSM_DSTATE + np.arange(SSM_DSTATE),
                                   SSM_DINNER + SSM_GROUPS * SSM_DSTATE + g * SSM_DSTATE + np.arange(SSM_DSTATE)]))
    return np.stack(idx)


def _up_layout():
    idx = []
    for g in range(FFN_G):
        idx.append(g * FFN_GW + np.arange(FFN_GW))
        idx.append(D_FF + g * FFN_GW + np.arange(FFN_GW))
    return np.concatenate(idx)


def _inverse(idx, n):
    inv = np.zeros(n, np.int64)
    pos = np.nonzero(idx >= 0)[0]
    inv[idx[pos]] = pos
    return inv


def _take_rows(a, idx, axis=0):
    idx = np.asarray(idx).reshape(-1)
    pieces, start = [], 0
    for i in range(1, len(idx) + 1):
        same_run = i < len(idx) and ((idx[i] == idx[i - 1] + 1 and idx[i - 1] >= 0) or (idx[i] < 0 and idx[i - 1] < 0))
        if same_run:
            continue
        n = i - start
        if idx[start] < 0:
            shape = list(a.shape)
            shape[axis] = n
            pieces.append(jnp.zeros(shape, a.dtype))
        else:
            pieces.append(lax.slice_in_dim(a, int(idx[start]), int(idx[start]) + n, axis=axis))
        start = i
    return pieces[0] if len(pieces) == 1 else jnp.concatenate(pieces, axis=axis)


def _copy_runs(sources, out_rows, runs, *, name, block):
    d, dtype = sources[0].shape[1], sources[0].dtype
    outs = []
    for o, rows in enumerate(out_rows):
        mine = sorted({i for i, _, oo, _, _ in runs if oo == o})
        ns, nblk = len(mine), rows // block
        sel = np.zeros(nblk, np.int32)
        idx = np.full((ns, nblk), -1, np.int64)
        for i, s, oo, t, n in runs:
            if oo == o:
                assert s % block == 0 and t % block == 0 and n % block == 0
                for b in range(n // block):
                    sel[t // block + b] = mine.index(i)
                    idx[mine.index(i), t // block + b] = s // block + b
        assert (idx.max(axis=0) >= 0).all()
        for i in range(ns):
            first = idx[i, np.nonzero(idx[i] >= 0)[0][0]]
            for b in range(nblk):
                if idx[i, b] < 0:
                    idx[i, b] = idx[i, b - 1] if b > 0 else first

        def body(sel_ref, idx_ref, *refs, ns=ns):
            srcs, out = refs[:ns], refs[ns]
            which = sel_ref[pl.program_id(0)]
            val = srcs[ns - 1][...]
            for i in range(ns - 2, -1, -1):
                val = jnp.where(which == i, srcs[i][...], val)
            out[...] = val

        in_specs = [pl.BlockSpec((block, d), (lambda i_, n_: lambda b, sel_ref, idx_ref: (idx_ref[i_ * n_ + b], 0))(i, nblk))
                    for i in range(ns)]
        outs.append(pl.pallas_call(
            body, name=f"{name}_{o}" if len(out_rows) > 1 else name,
            grid_spec=pltpu.PrefetchScalarGridSpec(
                num_scalar_prefetch=2, grid=(nblk,), in_specs=in_specs,
                out_specs=pl.BlockSpec((block, d), lambda b, sel_ref, idx_ref: (b, 0))),
            out_shape=jax.ShapeDtypeStruct((rows, d), dtype),
            compiler_params=pltpu.CompilerParams(dimension_semantics=("arbitrary",), vmem_limit_bytes=VMEM_LIMIT),
        )(jnp.asarray(sel), jnp.asarray(idx.reshape(-1), jnp.int32), *[sources[i] for i in mine]))
    return outs


_Z0, _XBC0, _DT0, _GATE0 = 8192, 12288, 18432, 18496
_B0, _C0 = _XBC0 + SSM_DINNER, _XBC0 + SSM_DINNER + SSM_GROUPS * SSM_DSTATE


def _in_proj_runs():
    runs = [(part * 2048 + h * HG_DK, 0, h * HG_BLK + part * HG_DK, HG_DK) for h in range(HG_HEADS) for part in range(4)]
    for g in range(SSM_GROUPS):
        base = g * SSM_BLK
        runs += [(_XBC0 + g * SSM_GW, 1, base, SSM_GW), (_B0 + g * SSM_DSTATE, 1, base + SSM_GW, SSM_DSTATE),
                 (_C0 + g * SSM_DSTATE, 1, base + SSM_GW + SSM_DSTATE, SSM_DSTATE),
                 (_Z0 + g * SSM_GW, 1, base + SSM_XBC + 128, SSM_GW)]
    return runs + [(_GATE0, 2, 0, 2 * D_MODEL)]


def _in_proj_to_kernel(in_t):
    d = in_t.shape[1]
    dt = jnp.pad(in_t[_DT0:_GATE0].reshape(SSM_GROUPS, SSM_HPG, d), ((0, 0), (0, 128 - SSM_HPG), (0, 0)))
    runs = [(0, src, sec, dst, n) for src, sec, dst, n in _in_proj_runs() if sec < 2]
    runs += [(1, g * 128, 1, g * SSM_BLK + SSM_XBC, 128) for g in range(SSM_GROUPS)]
    hg, ssm = _copy_runs([in_t, dt.reshape(SSM_GROUPS * 128, d)], [_Z0, SSM_GROUPS * SSM_BLK], runs,
                         name="in_proj_to_kernel_layout", block=128)
    return hg, ssm, in_t[_GATE0:]


def _in_proj_from_kernel(hg, ssm, gate):
    d = hg.shape[1]
    dt = ssm.reshape(SSM_GROUPS, SSM_BLK, d)[:, SSM_XBC:SSM_XBC + SSM_HPG].reshape(SSM_HEADS, d)
    runs = [(sec, dst, 0, src, n) for src, sec, dst, n in _in_proj_runs()] + [(3, 0, 0, _DT0, SSM_HEADS)]
    return _copy_runs([hg, ssm, gate, dt], [IN_TOTAL], runs, name="in_proj_to_global_layout", block=SSM_HEADS)[0]


def _up_to_kernel(up_t):
    runs = [(0, part * D_FF + g * FFN_GW, 0, (2 * g + part) * FFN_GW, FFN_GW) for g in range(FFN_G) for part in range(2)]
    return _copy_runs([up_t], [2 * D_FF], runs, name="up_to_kernel_layout", block=FFN_GW)[0]


def _up_from_kernel(up):
    runs = [(0, (2 * g + part) * FFN_GW, 0, part * D_FF + g * FFN_GW, FFN_GW) for g in range(FFN_G) for part in range(2)]
    return _copy_runs([up], [2 * D_FF], runs, name="up_to_global_layout", block=FFN_GW)[0]


_SMALL = (("mix_pre_norm", (1, 2048)), ("mix_post_norm", (1, 2048)), ("hg_lb_table", (2, 2048)), ("hg_out_norm", (1, 128)),
          ("ssm_conv_w", (4, 6144)), ("ssm_conv_b", (1, 6144)), ("ssm_dt_bias", (1, 64)), ("ssm_A_log", (1, 64)),
          ("ssm_D", (1, 64)), ("ssm_out_norm", (1, 4096)), ("ffn_pre_norm", (1, 2048)), ("ffn_post_norm", (1, 2048)),
          ("ffn_conv_w", (3, 5632)), ("ffn_conv_b", (1, 5632)), ("loss", (1, 1)))
_PACK_ROWS = 8 * (-(-sum(int(np.prod(s)) for _, s in _SMALL) // 1024))


def _pack(vals):
    flat = jnp.concatenate([vals[k].astype(F32).reshape(-1) for k, _ in _SMALL])
    return jnp.pad(flat, (0, _PACK_ROWS * 128 - flat.shape[0])).reshape(_PACK_ROWS, 128)


def _unpack(packed):
    flat, out, o = packed.reshape(-1), {}, 0
    for k, s in _SMALL:
        n = int(np.prod(s))
        out[k] = flat[o:o + n].reshape(s)
        o += n
    return out


def _local_step(x, target, w, p, late_weights=None, emit=lambda key, gw: None):
    t = x.shape[0]
    one = lambda a: a.reshape((1,) + a.shape)
    row = dict(rows=t, groups=1, consts=[], carries=[])

    (h1,), _ = _stage_fwd(_pre_step, name="pre_fwd", chunk=512, nc=1, xs=[(x, D_MODEL)], params=[one(p["mix_pre_norm"])],
                          ys=[(D_MODEL, BF16)], **row)
    proj_hg = _matmul_nt(h1, w["in_hg"], name="proj_hg")
    proj_ssm = _matmul_nt(h1, w["in_ssm"], name="proj_ssm")
    proj_gate = _matmul_nt(h1, w["in_gate"], name="proj_gate")

    hg = dict(rows=t, chunk=HG_CHUNK, nc=8, groups=HG_HEADS, xs=[(proj_hg, HG_BLK)], params=[p["hg_tab"], p["hg_nw"]],
              consts=_hg_consts(), carries=[(HG_DK, HG_DK)], gpb=8)
    (y_hg,), hg_saved = _stage_fwd(_hg_step, name="hg_fwd", ys=[(HG_DK, BF16)], **hg)

    ssd = dict(rows=t, chunk=SSM_CHUNK, nc=4, groups=SSM_GROUPS, xs=[(proj_ssm, SSM_BLK)],
               params=[p["conv_w"], p["conv_b"], p["dt_bias"], p["a_log"], p["d_skip"], p["ssm_nw"]],
               consts=_ssd_consts(), carries=[(4 * 128, SSM_DSTATE), (HALO, SSM_XBC)])
    (y_ssm,), ssd_saved = _stage_fwd(_ssd_step, name="ssd_fwd", ys=[(SSM_GW, BF16)], **ssd)

    if late_weights is not None:
        w = {**w, **late_weights([y_hg, y_ssm])}
    u_hg = _matmul_nn(y_hg, w["branch_hg"], name="branch_hg")
    u_ssm = _matmul_nn(y_ssm, w["branch_ssm"], name="branch_ssm")
    mix = dict(chunk=256, nc=1, xs=[(proj_gate, 2 * D_MODEL), (u_hg, D_MODEL), (u_ssm, D_MODEL)], params=[], **row)
    (mixed,), _ = _stage_fwd(_mix_step, name="mix_fwd", ys=[(D_MODEL, BF16)], **mix)
    v = _matmul_nn(mixed, w["out"], name="out_proj")
    post = dict(chunk=256, nc=1, xs=[(x, D_MODEL), (v, D_MODEL)],
                params=[one(p["mix_post_norm"]), one(p["ffn_pre_norm"])], **row)
    (x1, h2), _ = _stage_fwd(_post_step, name="post_fwd", ys=[(D_MODEL, F32), (D_MODEL, BF16)], **post)
    gu = _matmul_nt(h2, w["up"], name="ffn_up")
    ffn = dict(rows=t, chunk=256, nc=2, groups=FFN_G, xs=[(gu, 2 * FFN_GW)], params=[p["ffn_conv_w"], p["ffn_conv_b"]],
               consts=[], carries=[(HALO, FFN_GW)])
    (act,), ffn_saved = _stage_fwd(_ffn_step, name="ffn_fwd", ys=[(FFN_GW, BF16)], **ffn)
    d = _matmul_nn(act, w["down"], name="ffn_down")

    def head_step(carry, xv, civ, pv, cv):
        x1_, d_, tgt = xv

        def per_row_loss(a, b, nw):
            e = a + _rms(b, nw) - tgt
            return 0.5 * jnp.mean(e * e, axis=1, keepdims=True)

        lrow, vjp = jax.vjp(per_row_loss, x1_, d_, pv[0])
        dx1_, dd_, dnw = vjp(jnp.ones_like(lrow))
        loss = jnp.broadcast_to(jnp.sum(lrow, axis=0, keepdims=True), (1, 128))
        return [], [dx1_, dd_], [], [dnw, loss]

    (dy, dd), _, (g_ffn_post, loss) = _scan_call(
        head_step, name="loss_head", chunk=256, nc=1, xs=[(x1, D_MODEL), (d, D_MODEL), (target, D_MODEL)],
        params=[one(p["ffn_post_norm"])], ys=[(D_MODEL, F32), (D_MODEL, BF16)], accs=[(1, D_MODEL), (1, 128)], **row)

    gw = {}
    gw["down"] = _matmul_tn(act, dd, name="g_down")
    dact = _matmul_nt(dd, w["down"], name="d_act", dep=emit("down", gw))
    (dgu,), (g_fcw, g_fcb) = _stage_bwd(_ffn_step, name="ffn_bwd", saved=ffn_saved, dys=[(dact, FFN_GW)], dxs=[BF16], **ffn)
    gw["up"] = _matmul_tn(dgu, h2, name="g_up")
    dh2 = _matmul_nn(dgu, w["up"], name="d_h2", dep=emit("up", gw))
    (dx1, dv), (g_mix_post, g_ffn_pre) = _stage_bwd(_post_step, name="post_bwd", saved=[], dys=[(dy, D_MODEL), (dh2, D_MODEL)],
                                                    dxs=[F32, BF16], **post)
    gw["out"] = _matmul_tn(mixed, dv, name="g_out")
    dmixed = _matmul_nt(dv, w["out"], name="d_mixed")
    (dgate, du_hg, du_ssm), _ = _stage_bwd(_mix_step, name="mix_bwd", saved=[], dys=[(dmixed, D_MODEL)],
                                           dxs=[BF16, BF16, BF16], **mix)
    gw["in_gate"] = _matmul_tn(dgate, h1, name="g_in_gate")
    gw["branch_hg"] = _matmul_tn(y_hg, du_hg, name="g_branch_hg")
    gw["branch_ssm"] = _matmul_tn(y_ssm, du_ssm, name="g_branch_ssm")
    dy_hg = _matmul_nt(du_hg, w["branch_hg"], name="d_y_hg", dep=emit("branches", gw))
    dy_ssm = _matmul_nt(du_ssm, w["branch_ssm"], name="d_y_ssm")
    (dproj_ssm,), g_ssd = _stage_bwd(_ssd_step, name="ssd_bwd", saved=ssd_saved, dys=[(dy_ssm, SSM_GW)], dxs=[BF16], **ssd)
    gw["in_ssm"] = _matmul_tn(dproj_ssm, h1, name="g_in_ssm")
    (dproj_hg,), (g_tab, g_hg_nw) = _stage_bwd(_hg_step, name="hg_bwd", saved=hg_saved, dys=[(dy_hg, HG_DK)], dxs=[BF16], **hg)
    gw["in_hg"] = _matmul_tn(dproj_hg, h1, name="g_in_hg")
    dh_a = _matmul_nn(dproj_hg, w["in_hg"], name="d_h1_hg", dep=emit("in", gw))
    dh_b = _matmul_nn(dproj_ssm, w["in_ssm"], name="d_h1_ssm")
    dh_c = _matmul_nn(dgate, w["in_gate"], name="d_h1_gate")

    def pre_bwd_step(carry, xv, civ, pv, cv):
        x_, da, db, dc, dres = xv
        _, vjp = jax.vjp(_rms, x_, pv[0])
        dx_, dnw = vjp(da + db + dc)
        return [], [dx_ + dres], [], [dnw]

    (grad_x,), _, (g_mix_pre,) = _scan_call(
        pre_bwd_step, name="pre_bwd", chunk=256, nc=1,
        xs=[(x, D_MODEL), (dh_a, D_MODEL), (dh_b, D_MODEL), (dh_c, D_MODEL), (dx1, D_MODEL)],
        params=[one(p["mix_pre_norm"])], ys=[(D_MODEL, F32)], accs=[(1, D_MODEL)], **row)

    gp = dict(mix_pre_norm=g_mix_pre[0], mix_post_norm=g_mix_post[0], ffn_pre_norm=g_ffn_pre[0], ffn_post_norm=g_ffn_post[0],
              hg_tab=g_tab, hg_nw=g_hg_nw, conv_w=g_ssd[0], conv_b=g_ssd[1], dt_bias=g_ssd[2], a_log=g_ssd[3],
              d_skip=g_ssd[4], ssm_nw=g_ssd[5], ffn_conv_w=g_fcw, ffn_conv_b=g_fcb, loss=loss[0, :, :1])
    return grad_x, gw, gp


def _small_to_kernel_layout(s):
    conv_idx = _conv_layout()
    pad_heads = lambda a: jnp.pad(a.reshape(SSM_GROUPS, 1, SSM_HPG), ((0, 0), (0, 0), (0, 128 - SSM_HPG)))
    return dict(
        mix_pre_norm=s["mix_pre_norm"], mix_post_norm=s["mix_post_norm"], ffn_pre_norm=s["ffn_pre_norm"],
        ffn_post_norm=s["ffn_post_norm"],
        hg_tab=s["hg_lb_table"].reshape(2, HG_HEADS, HG_DK).transpose(1, 0, 2),
        hg_nw=jnp.broadcast_to(s["hg_out_norm"].reshape(1, 1, HG_DK), (HG_HEADS, 1, HG_DK)),
        conv_w=_take_rows(s["ssm_conv_w"], conv_idx, axis=1).reshape(SSM_CONV, SSM_GROUPS, SSM_XBC).transpose(1, 0, 2),
        conv_b=_take_rows(s["ssm_conv_b"], conv_idx, axis=1).reshape(SSM_GROUPS, 1, SSM_XBC),
        dt_bias=pad_heads(s["ssm_dt_bias"]), a_log=pad_heads(s["ssm_A_log"]),
        d_skip=jnp.repeat(s["ssm_D"].reshape(SSM_HEADS), SSM_HEADDIM).reshape(SSM_GROUPS, 1, SSM_GW),
        ssm_nw=s["ssm_out_norm"].reshape(SSM_GROUPS, 1, SSM_GW),
        ffn_conv_w=s["ffn_conv_w"].reshape(FFN_CONV, FFN_G, FFN_GW).transpose(1, 0, 2),
        ffn_conv_b=s["ffn_conv_b"].reshape(FFN_G, 1, FFN_GW),
    )


def _small_from_kernel_layout(g):
    conv_inv = _inverse(_conv_layout().reshape(-1), SSM_CONV_DIM)
    heads = lambda a: a[:, 0, :SSM_HPG].reshape(1, SSM_HEADS)
    return dict(
        mix_pre_norm=g["mix_pre_norm"], mix_post_norm=g["mix_post_norm"], ffn_pre_norm=g["ffn_pre_norm"],
        ffn_post_norm=g["ffn_post_norm"],
        hg_lb_table=g["hg_tab"].transpose(1, 0, 2).reshape(2, HG_HEADS * HG_DK),
        hg_out_norm=jnp.sum(g["hg_nw"], axis=0),
        ssm_conv_w=_take_rows(g["conv_w"].transpose(1, 0, 2).reshape(SSM_CONV, -1), conv_inv, axis=1),
        ssm_conv_b=_take_rows(g["conv_b"].reshape(1, -1), conv_inv, axis=1),
        ssm_dt_bias=heads(g["dt_bias"]), ssm_A_log=heads(g["a_log"]),
        ssm_D=jnp.sum(g["d_skip"].reshape(SSM_HEADS, SSM_HEADDIM), axis=1).reshape(1, SSM_HEADS),
        ssm_out_norm=g["ssm_nw"].reshape(1, SSM_DINNER),
        ffn_conv_w=g["ffn_conv_w"].transpose(1, 0, 2).reshape(FFN_CONV, D_FF),
        ffn_conv_b=g["ffn_conv_b"].reshape(1, D_FF),
        loss=g["loss"],
    )


def kernel(x, w_in, mix_pre_norm, mix_post_norm, hg_lb_table, hg_out_norm, ssm_conv_w, ssm_conv_b, ssm_dt_bias, ssm_A_log, ssm_D, ssm_out_norm, w_branch_hg, w_branch_ssm, w_out, ffn_pre_norm, ffn_post_norm, ffn_w_up, ffn_conv_w, ffn_conv_b, ffn_w_down, loss_target, m_w_in, m_mix_pre_norm, m_mix_post_norm, m_hg_lb_table, m_hg_out_norm, m_ssm_conv_w, m_ssm_conv_b, m_ssm_dt_bias, m_ssm_A_log, m_ssm_D, m_ssm_out_norm, m_w_branch_hg, m_w_branch_ssm, m_w_out, m_ffn_pre_norm, m_ffn_post_norm, m_ffn_w_up, m_ffn_conv_w, m_ffn_conv_b, m_ffn_w_down, v_w_in, v_mix_pre_norm, v_mix_post_norm, v_hg_lb_table, v_hg_out_norm, v_ssm_conv_w, v_ssm_conv_b, v_ssm_dt_bias, v_ssm_A_log, v_ssm_D, v_ssm_out_norm, v_w_branch_hg, v_w_branch_ssm, v_w_out, v_ffn_pre_norm, v_ffn_post_norm, v_ffn_w_up, v_ffn_conv_w, v_ffn_conv_b, v_ffn_w_down):
    names = ["w_in", "mix_pre_norm", "mix_post_norm", "hg_lb_table", "hg_out_norm", "ssm_conv_w", "ssm_conv_b", "ssm_dt_bias",
             "ssm_A_log", "ssm_D", "ssm_out_norm", "w_branch_hg", "w_branch_ssm", "w_out", "ffn_pre_norm", "ffn_post_norm",
             "ffn_w_up", "ffn_conv_w", "ffn_conv_b", "ffn_w_down"]
    ws = dict(zip(names, (w_in, mix_pre_norm, mix_post_norm, hg_lb_table, hg_out_norm, ssm_conv_w, ssm_conv_b, ssm_dt_bias,
                          ssm_A_log, ssm_D, ssm_out_norm, w_branch_hg, w_branch_ssm, w_out, ffn_pre_norm, ffn_post_norm,
                          ffn_w_up, ffn_conv_w, ffn_conv_b, ffn_w_down)))
    ms = dict(zip(names, (m_w_in, m_mix_pre_norm, m_mix_post_norm, m_hg_lb_table, m_hg_out_norm, m_ssm_conv_w, m_ssm_conv_b,
                          m_ssm_dt_bias, m_ssm_A_log, m_ssm_D, m_ssm_out_norm, m_w_branch_hg, m_w_branch_ssm, m_w_out,
                          m_ffn_pre_norm, m_ffn_post_norm, m_ffn_w_up, m_ffn_conv_w, m_ffn_conv_b, m_ffn_w_down)))
    vs = dict(zip(names, (v_w_in, v_mix_pre_norm, v_mix_post_norm, v_hg_lb_table, v_hg_out_norm, v_ssm_conv_w, v_ssm_conv_b,
                          v_ssm_dt_bias, v_ssm_A_log, v_ssm_D, v_ssm_out_norm, v_w_branch_hg, v_w_branch_ssm, v_w_out,
                          v_ffn_pre_norm, v_ffn_post_norm, v_ffn_w_up, v_ffn_conv_w, v_ffn_conv_b, v_ffn_w_down)))
    me = 4 * lax.axis_index("x") + 2 * lax.axis_index("y") + lax.axis_index("c")

    late_shards = [ffn_w_up[0].T.astype(BF16), w_branch_hg[0].astype(BF16), w_branch_ssm[0].astype(BF16),
                   w_out[0].astype(BF16), ffn_w_down[0].astype(BF16)]
    landing = [lax.dynamic_update_slice_in_dim(lax.empty((N_DEV,) + s.shape, s.dtype), s[None], me, axis=0)
               for s in late_shards]
    gathered = _all_gather([w_in[0].T.astype(BF16), ssm_conv_w[0], ffn_conv_w[0]], name="gather_in_proj")
    late = _push_start([s[None] for s in late_shards], landing, name="late_weights_start", plan=_plan_own_block,
                       after=gathered[1])
    in_hg, in_ssm, in_gate = _in_proj_to_kernel(gathered[0].reshape(IN_TOTAL, D_MODEL))
    w = dict(in_hg=in_hg, in_ssm=in_ssm, in_gate=in_gate)
    small = {k: ws[k] for k, _ in _SMALL[:-1]}
    small["mix_pre_norm"] = mix_pre_norm + late[4][0, 0]
    small["ssm_conv_w"] = gathered[1].transpose(1, 0, 2).reshape(SSM_CONV, SSM_CONV_DIM)
    small["ffn_conv_w"] = gathered[2].transpose(1, 0, 2).reshape(FFN_CONV, D_FF)
    small = {k: small[k].reshape(s) for k, s in _SMALL[:-1]}

    def late_weights(after):
        landed = _push_wait(late, after, name="late_weights_wait", plan=_plan_own_block)
        up_all, bhg, bssm, out, down = _push(landed, name="late_weights_pass_on", plan=_plan_pass_on)
        return dict(up=_up_to_kernel(up_all.reshape(2 * D_FF, D_MODEL)), branch_hg=bhg.reshape(D_MODEL, D_MODEL),
                    branch_ssm=bssm.reshape(SSM_DINNER, D_MODEL), out=out.reshape(D_MODEL, D_MODEL),
                    down=down.reshape(D_FF, D_MODEL))

    in_flight = []

    def launch(key, named_parts):
        ks, parts = zip(*named_parts)
        from_sibling = _push(list(parts), name="grads_to_sibling_" + key, out_slots=N_CHIPS, plan=_plan_sibling)
        sums = [_pair_sum(p, r, name="pair_sum_" + k, tc=256) for k, p, r in zip(ks, parts, from_sibling)]
        handles = _push_start([q for q, _ in sums], [z for _, z in sums], name="grads_to_chips_start_" + key, plan=_plan_chips)
        in_flight.append((key, ks, handles))
        return handles[4]

    def emit(key, gw):
        blocks = lambda a: a.reshape(N_DEV, -1, D_MODEL)
        if key == "down":
            return launch(key, [("ffn_w_down", blocks(gw["down"]))])
        if key == "up":
            return launch(key, [("ffn_w_up", blocks(_up_from_kernel(gw["up"])))])
        if key == "branches":
            return launch(key, [("w_branch_hg", blocks(gw["branch_hg"])), ("w_branch_ssm", blocks(gw["branch_ssm"])),
                                ("w_out", blocks(gw["out"]))])
        return launch(key, [("w_in", blocks(_in_proj_from_kernel(gw["in_hg"], gw["in_ssm"], gw["in_gate"])))])

    grad_x, gw, gp = _local_step(x[0], loss_target[0], w, _small_to_kernel_layout(small), late_weights, emit)

    big_names = ["w_in", "ffn_w_up", "w_branch_hg", "w_branch_ssm", "w_out", "ffn_w_down"]
    grads = {}
    for key, ks, handles in in_flight:
        landed = _push_wait(handles, grad_x, name="grads_to_chips_wait_" + key, plan=_plan_chips)
        for k, r in zip(ks, landed):
            g = _sum_blocks(r, name="sum_" + k, tc=256)
            grads[k] = g.T if k in ("w_in", "ffn_w_up") else g
    small_all = _push([_pack(_small_from_kernel_layout(gp))[None]], name="small_to_everyone", out_slots=N_DEV,
                      plan=_plan_everyone)
    small_g = _unpack(_sum_blocks(small_all[0], name="sum_small", tc=128))
    loss = small_g.pop("loss").reshape(())
    for k, g in small_g.items():
        if k in ("ssm_conv_w", "ffn_conv_w"):
            n = g.shape[1] // N_DEV
            g = lax.dynamic_slice_in_dim(g, me * n, n, axis=1)
        grads[k] = g

    delta, new_m, new_v = {}, {}, {}
    for k in big_names:
        delta[k], new_m[k], new_v[k] = _adamw(ws[k][0], grads[k], ms[k][0], vs[k][0], name="adamw_" + k, tr=64)
    small_names = [k for k in names if k not in big_names]
    flat = lambda d: jnp.concatenate([d[k].astype(F32).reshape(-1) for k in small_names])
    n_small = sum(int(np.prod(ws[k].shape)) for k in small_names)
    rows = 8 * (-(-n_small // 1024))
    pack2 = lambda d: jnp.pad(flat(d), (0, rows * 128 - n_small)).reshape(rows, 128)
    v_packed = jnp.pad(flat(vs), (0, rows * 128 - n_small), constant_values=1.0).reshape(rows, 128)
    packed = _adamw(pack2(ws), pack2(grads), pack2(ms), v_packed, name="adamw_small", tr=rows)
    o = 0
    for k in small_names:
        n = int(np.prod(ws[k].shape))
        delta[k], new_m[k], new_v[k] = (a.reshape(-1)[o:o + n].reshape(ws[k].shape) for a in packed)
        o += n

    full = lambda d: [d[k].reshape(ws[k].shape) for k in names]
    return (loss, grad_x[None], *full(grads), *full(delta), *full(new_m), *full(new_v))
```

```python
import functools

import numpy as np
import jax
import jax.numpy as jnp
from jax import lax
from jax.experimental import pallas as pl
from jax.experimental.pallas import tpu as pltpu

F32, BF16 = jnp.float32, jnp.bfloat16

D_MODEL = 2048
EPS = 1e-6
HG_HEADS, HG_DK, HG_CHUNK = 16, 128, 64
HG_BLK = 4 * HG_DK
SSM_DINNER, SSM_HEADDIM, SSM_HEADS, SSM_GROUPS, SSM_DSTATE, SSM_CONV = 4096, 64, 64, 8, 128, 4
SSM_CHUNK = 128
SSM_GW = SSM_DINNER // SSM_GROUPS
SSM_HPG = SSM_HEADS // SSM_GROUPS
SSM_XBC = SSM_GW + 2 * SSM_DSTATE
SSM_BLK = SSM_XBC + 128 + SSM_GW
SSM_CONV_DIM = SSM_DINNER + 2 * SSM_GROUPS * SSM_DSTATE
D_FF, FFN_CONV = 5632, 3
FFN_GW = 512
FFN_G = D_FF // FFN_GW
IN_TOTAL = 22592
N_DEV = 8
HALO = 8
VMEM_LIMIT = 52 * 1024 * 1024
ADAM_LR, ADAM_B1, ADAM_B2, ADAM_EPS, ADAM_WD, ADAM_STEP = 0.001, 0.9, 0.999, 1e-08, 0.01, 10

_DIMS = {"nn": ((1,), (0,)), "nt": ((1,), (1,)), "tn": ((0,), (0,))}


def _mm_raw(a, b, mode):
    return lax.dot_general(a.astype(BF16), b.astype(BF16), (_DIMS[mode], ((), ())), preferred_element_type=F32)


@functools.partial(jax.custom_vjp, nondiff_argnums=(2,))
def _mm(a, b, mode):
    return _mm_raw(a, b, mode)


def _mm_fwd(a, b, mode):
    return _mm_raw(a, b, mode), (a, b)


def _mm_bwd(mode, res, dc):
    a, b = res
    if mode == "nn":
        return _mm_raw(dc, b, "nt"), _mm_raw(a, dc, "tn")
    if mode == "nt":
        return _mm_raw(dc, b, "nn"), _mm_raw(dc, a, "tn")
    return _mm_raw(b, dc, "nt"), _mm_raw(a, dc, "nn")


_mm.defvjp(_mm_fwd, _mm_bwd)


def _cmm_raw(m, x, mode):
    hi = x.astype(BF16)
    r1 = x - hi.astype(F32)
    mid = r1.astype(BF16)
    lo = (r1 - mid.astype(F32)).astype(BF16)
    dn = (_DIMS[mode], ((), ()))
    dot = lambda p: lax.dot_general(m, p, dn, preferred_element_type=F32)
    return dot(hi) + dot(mid) + dot(lo)


@jax.custom_vjp
def _cmm(m, x):
    return _cmm_raw(m, x, "nn")


def _cmm_fwd(m, x):
    return _cmm_raw(m, x, "nn"), m


def _cmm_bwd(m, dy):
    return jnp.zeros_like(m), _cmm_raw(m, dy, "tn")


_cmm.defvjp(_cmm_fwd, _cmm_bwd)


@functools.partial(jax.custom_vjp, nondiff_argnums=(1,))
def _sroll(x, s):
    return pltpu.roll(x, s, 0) if s else x


def _sroll_fwd(x, s):
    return _sroll(x, s), None


def _sroll_bwd(s, _, ct):
    return ((pltpu.roll(ct, ct.shape[0] - s, 0) if s else ct),)


_sroll.defvjp(_sroll_fwd, _sroll_bwd)


def _rms(x, w):
    return x * lax.rsqrt(jnp.mean(x * x, axis=-1, keepdims=True) + EPS) * w


def _softplus(x):
    return jnp.maximum(x, 0.0) + jnp.log(1.0 + jnp.exp(-jnp.abs(x)))


def _causal_conv(halo, x, w, b):
    k_taps = w.shape[0]
    xe = jnp.concatenate([halo, x], axis=0)
    out = b
    for k in range(k_taps):
        out = out + w[k:k + 1, :] * _sroll(xe, k_taps - 1 - k)[HALO:, :]
    return out


def _hg_consts():
    c = HG_CHUNK
    t = np.arange(c)
    blocks, pair = [], []
    for m in (32, 16, 8, 4, 2, 1):
        pos = t % (2 * m)
        late = pos >= m
        mid = t - pos + m
        j = t[None, :]
        mq = late[:, None] & (j >= mid[:, None]) & (j <= t[:, None])
        mk = (~late)[:, None] & (j > t[:, None]) & (j <= mid[:, None] - 1)
        blocks.append(mq | mk)
        parent = t // (2 * m)
        pair.append((parent[:, None] == parent[None, :]) & late[:, None] & (~late)[None, :])
    blocks.append(t[None, :] <= t[:, None])
    mall = jnp.asarray(np.concatenate(blocks, 0).astype(np.float32), BF16)
    pair = jnp.asarray(np.stack(pair, 0).astype(np.float32))
    eye = jnp.asarray(np.eye(c, dtype=np.float32))
    return [mall, pair, eye]


def _hg_step(carry, xs, params, consts):
    (st,) = carry
    blk = xs[0].astype(F32)
    tab, nw = params
    mall, pair, eye = consts
    c, dk = HG_CHUNK, HG_DK
    q_raw, f_raw, v, og = blk[:, :dk], blk[:, dk:2 * dk], blk[:, 2 * dk:3 * dk], blk[:, 3 * dk:]
    lb = jax.nn.sigmoid(tab[0:1, :] - tab[1:2, :])
    f = lb + (1.0 - lb) * jax.nn.sigmoid(f_raw)
    g = jnp.log(f)
    kk = 1.0 - f
    qh = jax.nn.silu(q_raw) * (HG_DK ** -0.5)
    yield
    sums = _cmm(mall, g)
    yield
    b = sums[6 * c:, :]
    fac = jnp.exp(sums[:6 * c, :])
    scores = eye * jnp.sum(qh * kk, axis=1, keepdims=True)
    b_last = jnp.sum(g, axis=0, keepdims=True)
    yield
    inter = _mm(qh * jnp.exp(b), st, "nt")
    st_new = st * jnp.exp(b_last) + _mm(v, kk * jnp.exp(b_last - b), "tn")
    yield
    for l in range(6):
        fl = fac[l * c:(l + 1) * c, :]
        scores = scores + pair[l] * _mm(qh * fl, kk * fl, "nt")
        if l % 2:
            yield
    o = _mm(scores, v, "nn") + inter
    yield
    y = _rms(o, nw) * jax.nn.silu(og)
    return [st_new], [y]


def _ssd_consts():
    t = np.arange(SSM_CHUNK)
    tril = (t[None, :] <= t[:, None]).astype(np.float32)
    return [jnp.asarray(tril, BF16), jnp.asarray(tril)]


def _ssd_step(carry, xs, params, consts):
    st, halo = carry
    blk = xs[0].astype(F32)
    conv_w, conv_b, dtb, alog, dskip, nw = params
    tril_b, tril = consts
    c = SSM_CHUNK
    raw, dtr, z = blk[:, :SSM_XBC], blk[:, SSM_XBC:SSM_XBC + 128], blk[:, SSM_XBC + 128:]
    act = jax.nn.silu(_causal_conv(halo, raw, conv_w, conv_b))
    xh, bm, cm = act[:, :SSM_GW], act[:, SSM_GW:SSM_GW + SSM_DSTATE], act[:, SSM_GW + SSM_DSTATE:]
    dt = _softplus(dtr + dtb)
    da = dt * (-jnp.exp(alog))
    acum = _cmm(tril_b, da)
    acum_t = acum.T
    a_last = jnp.sum(da, axis=0, keepdims=True)
    cb_causal = _mm(cm, bm, "nt") * tril
    lane = lax.broadcasted_iota(jnp.int32, (c, 128), 1)
    row = lax.broadcasted_iota(jnp.int32, (128, 128), 0)
    first = lane < SSM_HEADDIM
    ys, st_new = [], []
    for j in range(SSM_HPG // 2):
        xp = xh[:, 128 * j:128 * (j + 1)]
        sp = st[128 * j:128 * (j + 1), :]
        r0, r1 = 2 * j, 2 * j + 1
        col = lambda a, r: jnp.broadcast_to(a[:, r:r + 1], (c, 128))
        xdt = xp * jnp.where(first, col(dt, r0), col(dt, r1))
        yj = _mm(cm, sp, "nt") * jnp.exp(jnp.where(first, col(acum, r0), col(acum, r1)))
        for r, keep in ((r0, first), (r1, ~first)):
            dec = jnp.broadcast_to(acum[:, r:r + 1], (c, c)) - jnp.broadcast_to(acum_t[r:r + 1, :], (c, c))
            m = cb_causal * jnp.exp(jnp.minimum(dec, 0.0))
            yj = yj + _mm(m, jnp.where(keep, xdt, 0.0), "nn")
        al0, al1 = a_last[:, r0:r0 + 1], a_last[:, r1:r1 + 1]
        wts = jnp.exp(jnp.where(first, al0 - col(acum, r0), al1 - col(acum, r1)))
        st_new.append(jnp.where(row < SSM_HEADDIM, jnp.exp(al0), jnp.exp(al1)) * sp + _mm(xdt * wts, bm, "tn"))
        ys.append(yj)
    y = jnp.concatenate(ys, axis=1) + dskip * xh
    y = _rms(y * jax.nn.silu(z), nw)
    return [jnp.concatenate(st_new, axis=0), raw[c - HALO:, :]], [y]


def _ffn_step(carry, xs, params, consts):
    (halo,) = carry
    blk = xs[0].astype(F32)
    conv_w, conv_b = params
    gate, up = blk[:, :FFN_GW], blk[:, FFN_GW:]
    a = jax.nn.gelu(_causal_conv(halo, gate, conv_w, conv_b), approximate=True) * up
    return [gate[gate.shape[0] - HALO:, :]], [a]


def _pre_step(carry, xs, params, consts):
    return [], [_rms(xs[0], params[0])]


def _mix_step(carry, xs, params, consts):
    gates, uh, us = (a.astype(F32) for a in xs)
    return [], [jax.nn.sigmoid(gates[:, :D_MODEL]) * uh + jax.nn.sigmoid(gates[:, D_MODEL:]) * us]


def _post_step(carry, xs, params, consts):
    x, v = xs
    x1 = x + _rms(v, params[0])
    return [], [x1, _rms(x1, params[1])]


def _scan_call(step, *, name, rows, chunk, nc, groups, xs, cins=(), params=(), consts=(), carries=(), ys=(), couts=(),
               accs=(), reverse=False, gpb=1, multi=False):
    blk_rows = chunk * nc
    nb = rows // blk_rows
    n_chunks = rows // chunk
    assert nb * blk_rows == rows and groups % gpb == 0
    rb = (lambda i: nb - 1 - i) if reverse else (lambda i: i)
    n_x, n_ci, n_p, n_c = len(xs), len(cins), len(params), len(consts)
    n_y, n_co, n_a = len(ys), len(couts), len(accs)

    def chunk_spec(shape):
        zeros = (0,) * len(shape)
        return pl.BlockSpec((gpb, nc) + tuple(shape), lambda g, i: (g, rb(i)) + zeros)

    in_specs = [pl.BlockSpec((blk_rows, gpb * w), lambda g, i: (rb(i), g)) for _, w in xs]
    in_specs += [chunk_spec(a.shape[2:]) for a in cins]
    in_specs += [pl.BlockSpec((gpb,) + tuple(a.shape[1:]), lambda g, i: (g, 0, 0)) for a in params]
    in_specs += [pl.BlockSpec(a.shape, (lambda nd: lambda g, i: (0,) * nd)(a.ndim)) for a in consts]
    out_specs = [pl.BlockSpec((blk_rows, gpb * w), lambda g, i: (rb(i), g)) for w, _ in ys]
    out_specs += [chunk_spec(s) for s in couts]
    out_specs += [pl.BlockSpec((gpb, r, c), lambda g, i: (g, 0, 0)) for r, c in accs]
    out_shape = [jax.ShapeDtypeStruct((rows, groups * w), dt) for w, dt in ys]
    out_shape += [jax.ShapeDtypeStruct((groups, n_chunks) + tuple(s), F32) for s in couts]
    out_shape += [jax.ShapeDtypeStruct((groups, r, c), F32) for r, c in accs]
    x_widths = [w for _, w in xs]
    y_widths = [w for w, _ in ys]

    def body(*refs):
        x_refs = refs[:n_x]
        ci_refs = refs[n_x:n_x + n_ci]
        p_refs = refs[n_x + n_ci:n_x + n_ci + n_p]
        c_refs = refs[n_x + n_ci + n_p:n_x + n_ci + n_p + n_c]
        o = n_x + n_ci + n_p + n_c
        y_refs = refs[o:o + n_y]
        co_refs = refs[o + n_y:o + n_y + n_co]
        a_refs = refs[o + n_y + n_co:o + n_y + n_co + n_a]
        carry_refs = refs[o + n_y + n_co + n_a:]

        @pl.when(pl.program_id(1) == 0)
        def _():
            for s in carry_refs:
                s[...] = jnp.zeros(s.shape, F32)
            for a in a_refs:
                a[...] = jnp.zeros(a.shape, F32)

        cvals = [c[...] for c in c_refs]

        def one_chunk(i, _):
            c = (nc - 1 - i) if reverse else i
            r0 = c * chunk if isinstance(c, int) else pl.multiple_of(c * chunk, chunk)
            loaded = []
            for u in range(gpb):
                carry = [s[u] for s in carry_refs]
                xv = [x[pl.ds(r0, chunk), u * w:(u + 1) * w] for x, w in zip(x_refs, x_widths)]
                civ = [ci[u, c] for ci in ci_refs]
                loaded.append((carry, xv, civ, [p[u] for p in p_refs]))
            results = step(loaded, cvals) if multi else [step(*args, cvals) for args in loaded]
            for u, (new_carry, yv, cov, av) in enumerate(results):
                for s, val in zip(carry_refs, new_carry):
                    s[u] = val
                for y, w, val in zip(y_refs, y_widths, yv):
                    y[pl.ds(r0, chunk), u * w:(u + 1) * w] = val.astype(y.dtype)
                for co, val in zip(co_refs, cov):
                    co[u, c] = val
                for a, val in zip(a_refs, av):
                    a[u] += val
            return 0

        if nc == 1:
            one_chunk(0, 0)
        else:
            lax.fori_loop(0, nc, one_chunk, 0)

    outs = pl.pallas_call(
        body, name=name, grid=(groups // gpb, nb), in_specs=in_specs, out_specs=out_specs, out_shape=out_shape,
        scratch_shapes=[pltpu.VMEM((gpb,) + tuple(s), F32) for s in carries],
        compiler_params=pltpu.CompilerParams(dimension_semantics=("arbitrary", "arbitrary"),
                                             vmem_limit_bytes=VMEM_LIMIT),
    )(*[a for a, _ in xs], *cins, *params, *consts)
    return outs[:n_y], outs[n_y:n_y + n_co], outs[n_y + n_co:]


def _run_interleaved(step, arg_tuples):
    runs = [step(*args) for args in arg_tuples]
    if not hasattr(runs[0], "send"):
        return runs
    results, live = [None] * len(runs), list(range(len(runs)))
    while live:
        for u in list(live):
            try:
                next(runs[u])
            except StopIteration as done:
                results[u] = done.value
                live.remove(u)
    return results


def _stage_fwd(step, *, name, rows, chunk, nc, groups, xs, params, consts, carries, ys, gpb=1):
    def fstep(loaded, cv):
        outs = _run_interleaved(step, [(carry, xv, pv, cv) for carry, xv, _, pv in loaded])
        return [(new_carry, yv, carry, []) for (new_carry, yv), (carry, _, _, _) in zip(outs, loaded)]

    yv, saved, _ = _scan_call(fstep, name=name, rows=rows, chunk=chunk, nc=nc, groups=groups, xs=xs, params=params,
                              consts=consts, carries=carries, ys=ys, couts=carries, gpb=gpb, multi=True)
    return yv, saved


def _stage_bwd(step, *, name, rows, chunk, nc, groups, xs, saved, params, consts, carries, dys, dxs, gpb=1):
    n_x = len(xs)

    def bstep(loaded, cv):
        civs = [list(civ) for _, _, civ, _ in loaded]
        xvs = [list(xv_all[:n_x]) for _, xv_all, _, _ in loaded]
        pvs = [list(pv) for _, _, _, pv in loaded]
        cts = [(list(dcarry), [d.astype(F32) for d in xv_all[n_x:]]) for dcarry, xv_all, _, _ in loaded]

        def fwd(civs_, xvs_, pvs_):
            outs = _run_interleaved(step, [(c_, x_, p_, cv) for c_, x_, p_ in zip(civs_, xvs_, pvs_)])
            return [(list(new_carry), list(yv)) for new_carry, yv in outs]

        _, vjp = jax.vjp(fwd, civs, xvs, pvs)
        dcivs, dxvs, dpvs = vjp(cts)
        return [(dc, dx, [], dp) for dc, dx, dp in zip(dcivs, dxvs, dpvs)]

    dxv, _, dpv = _scan_call(bstep, name=name, rows=rows, chunk=chunk, nc=nc, groups=groups, xs=list(xs) + list(dys),
                             cins=saved, params=params, consts=consts, carries=carries,
                             ys=[(w, dt) for (_, w), dt in zip(xs, dxs)], accs=[a.shape[1:] for a in params],
                             reverse=True, gpb=gpb, multi=True)
    return dxv, dpv


def _mm_params(sem):
    return pltpu.CompilerParams(dimension_semantics=sem, vmem_limit_bytes=VMEM_LIMIT)


def _after(dep):
    return ([], []) if dep is None else ([dep], [pl.BlockSpec(memory_space=pl.ANY)])


def _matmul_nt(a, b, *, name, dep=None):
    m, k = a.shape
    n = b.shape[0]
    tm = min(1024, m)
    tn = 1024 if n % 1024 == 0 else 512
    deps, dep_specs = _after(dep)

    def body(a_ref, b_ref, *rest):
        rest[-1][...] = lax.dot_general(a_ref[...], b_ref[...], (_DIMS["nt"], ((), ())), preferred_element_type=F32)

    return pl.pallas_call(
        body, name=name, grid=(m // tm, n // tn),
        in_specs=[pl.BlockSpec((tm, k), lambda i, j: (i, 0)), pl.BlockSpec((tn, k), lambda i, j: (j, 0))] + dep_specs,
        out_specs=pl.BlockSpec((tm, tn), lambda i, j: (i, j)),
        out_shape=jax.ShapeDtypeStruct((m, n), F32),
        compiler_params=_mm_params(("parallel", "arbitrary")),
    )(a, b, *deps)


def _matmul_nn(a, b, *, name, dep=None):
    m, k = a.shape
    n = b.shape[1]
    deps, dep_specs = _after(dep)
    if k > 6144:
        tm, tk, steps = min(512, m), k // 4, 4

        def body_k(a_ref, b_ref, *rest):
            part = jnp.dot(a_ref[...], b_ref[...], preferred_element_type=F32)

            @pl.when(pl.program_id(1) == 0)
            def _():
                rest[-1][...] = part

            @pl.when(pl.program_id(1) != 0)
            def _():
                rest[-1][...] += part

        return pl.pallas_call(
            body_k, name=name, grid=(m // tm, steps),
            in_specs=[pl.BlockSpec((tm, tk), lambda i, j: (i, j)), pl.BlockSpec((tk, n), lambda i, j: (j, 0))] + dep_specs,
            out_specs=pl.BlockSpec((tm, n), lambda i, j: (i, 0)),
            out_shape=jax.ShapeDtypeStruct((m, n), F32),
            compiler_params=_mm_params(("parallel", "arbitrary")),
        )(a, b, *deps)
    tm, tn = (1024, 1024) if k <= 4096 else (1024, 512)
    tm = min(tm, m)

    def body(a_ref, b_ref, *rest):
        rest[-1][...] = jnp.dot(a_ref[...], b_ref[...], preferred_element_type=F32)

    return pl.pallas_call(
        body, name=name, grid=(m // tm, n // tn),
        in_specs=[pl.BlockSpec((tm, k), lambda i, j: (i, 0)), pl.BlockSpec((k, tn), lambda i, j: (0, j))] + dep_specs,
        out_specs=pl.BlockSpec((tm, tn), lambda i, j: (i, j)),
        out_shape=jax.ShapeDtypeStruct((m, n), F32),
        compiler_params=_mm_params(("parallel", "arbitrary")),
    )(a, b, *deps)


def _matmul_tn(x, y, *, name, tp=512, tq=512):
    t, p = x.shape
    q = y.shape[1]

    def body(x_ref, y_ref, o_ref):
        o_ref[...] = lax.dot_general(x_ref[...], y_ref[...], (_DIMS["tn"], ((), ())),
                                     preferred_element_type=F32).astype(o_ref.dtype)

    return pl.pallas_call(
        body, name=name, grid=(p // tp, q // tq),
        in_specs=[pl.BlockSpec((t, tp), lambda i, j: (0, i)), pl.BlockSpec((t, tq), lambda i, j: (0, j))],
        out_specs=pl.BlockSpec((tp, tq), lambda i, j: (i, j)),
        out_shape=jax.ShapeDtypeStruct((p, q), BF16),
        compiler_params=_mm_params(("parallel", "arbitrary")),
    )(x, y)


N_CHIPS = N_DEV // 2


def _all_gather(arrays, *, name):
    n = len(arrays)
    out_shape = [jax.ShapeDtypeStruct((N_DEV,) + tuple(a.shape), a.dtype) for a in arrays]

    def body(*refs):
        in_refs, out_refs = refs[:n], refs[n:2 * n]
        send_sems, recv_sems, local_sems = refs[2 * n:]
        x, y, c = lax.axis_index("x"), lax.axis_index("y"), lax.axis_index("c")
        me, sibling = (x, y, c), (x, y, 1 - c)
        chips = [(1 - x, y), (x, 1 - y), (1 - x, 1 - y)]

        def copy(a, k, block, to, src=None):
            slot = out_refs[a].at[4 * block[0] + 2 * block[1] + block[2]]
            return pltpu.make_async_remote_copy(
                src_ref=slot if src is None else src, dst_ref=slot, send_sem=send_sems.at[a, k],
                recv_sem=recv_sems.at[a, k], device_id=to, device_id_type=pl.DeviceIdType.MESH)

        mine = [pltpu.make_async_copy(in_refs[a], out_refs[a].at[4 * x + 2 * y + c], local_sems.at[a]) for a in range(n)]
        first = []
        for a in range(n):
            first.append(copy(a, 0, me, sibling, src=in_refs[a]))
            first += [copy(a, 1 + j, me, (*chip, c), src=in_refs[a]) for j, chip in enumerate(chips)]
        for cp in mine + first:
            cp.start()
        passed = []
        for j, chip in enumerate(chips):
            for a in range(n):
                copy(a, 1 + j, (*chip, c), me).wait_recv()
                passed.append(copy(a, 4 + j, (*chip, c), sibling))
                passed[-1].start()
        for a in range(n):
            copy(a, 0, sibling, me).wait_recv()
            for j, chip in enumerate(chips):
                copy(a, 4 + j, (*chip, 1 - c), me).wait_recv()
        for cp in first + passed:
            cp.wait_send()
        for cp in mine:
            cp.wait()

    any_spec = pl.BlockSpec(memory_space=pl.ANY)
    return pl.pallas_call(
        body, name=name, in_specs=[any_spec] * n, out_specs=[any_spec] * n, out_shape=out_shape,
        scratch_shapes=[pltpu.SemaphoreType.DMA((n, N_DEV - 1)), pltpu.SemaphoreType.DMA((n, N_DEV - 1)),
                        pltpu.SemaphoreType.DMA((n,))],
        compiler_params=pltpu.CompilerParams(has_side_effects=True),
    )(*arrays)


def _push(arrays, *, name, plan, out_slots=None):
    n = len(arrays)
    in_place = out_slots is None
    out_shape = [jax.ShapeDtypeStruct(((a.shape[0] if in_place else out_slots),) + tuple(a.shape[1:]), a.dtype) for a in arrays]
    n_tr = len(plan(0, 0, 0)[0])

    def body(*refs):
        in_refs, out_refs = refs[:n], refs[n:2 * n]
        send_sems, recv_sems, local_sems = refs[2 * n:]
        src_refs = out_refs if in_place else in_refs
        transfers, local = plan(lax.axis_index("x"), lax.axis_index("y"), lax.axis_index("c"))
        copies = []
        for a in range(n):
            if local is not None:
                copies.append(pltpu.make_async_copy(src_refs[a].at[local[0]], out_refs[a].at[local[1]], local_sems.at[a]))
            for k, (peer, src, dst) in enumerate(transfers):
                copies.append(pltpu.make_async_remote_copy(
                    src_ref=src_refs[a].at[src], dst_ref=out_refs[a].at[dst], send_sem=send_sems.at[a, k],
                    recv_sem=recv_sems.at[a, k], device_id=peer, device_id_type=pl.DeviceIdType.MESH))
        for cp in copies:
            cp.start()
        for cp in copies:
            cp.wait()

    any_spec = pl.BlockSpec(memory_space=pl.ANY)
    return pl.pallas_call(
        body, name=name, in_specs=[any_spec] * n, out_specs=[any_spec] * n, out_shape=out_shape,
        input_output_aliases={a: a for a in range(n)} if in_place else {},
        scratch_shapes=[pltpu.SemaphoreType.DMA((n, n_tr)), pltpu.SemaphoreType.DMA((n, n_tr)),
                        pltpu.SemaphoreType.DMA((n,))],
        compiler_params=pltpu.CompilerParams(has_side_effects=True),
    )(*arrays)


_HBM_SPEC = pl.BlockSpec(memory_space=pltpu.HBM)
_SEM_SPEC = pl.BlockSpec(memory_space=pltpu.SEMAPHORE)
_DATAFLOW = pltpu.SideEffectType.DATAFLOW_SIDE_EFFECTING


def _push_start(sources, landing, *, name, plan, after=None):
    n = len(sources)
    n_tr = len(plan(0, 0, 0)[0])
    deps, dep_specs = _after(after)

    def body(*refs):
        src_refs, land_refs = refs[:n], refs[n:2 * n]
        o = 2 * n + len(deps)
        send_sems, recv_sems, token = refs[o], refs[o + 1], refs[-1]
        transfers, _ = plan(lax.axis_index("x"), lax.axis_index("y"), lax.axis_index("c"))
        for a in range(n):
            for k, (peer, src, dst) in enumerate(transfers):
                pltpu.make_async_remote_copy(
                    src_ref=src_refs[a].at[src], dst_ref=land_refs[a].at[dst], send_sem=send_sems.at[a * n_tr + k],
                    recv_sem=recv_sems.at[a * n_tr + k], device_id=peer, device_id_type=pl.DeviceIdType.MESH).start()
        token[...] = jnp.zeros(token.shape, token.dtype)

    hbm = lambda a: pltpu.HBM(a.shape, a.dtype)
    outs = pl.pallas_call(
        body, name=name,
        out_shape=(pltpu.SemaphoreType.DMA((n * n_tr,)), pltpu.SemaphoreType.DMA((n * n_tr,)), *[hbm(a) for a in sources],
                   *[hbm(a) for a in landing], jax.ShapeDtypeStruct((8, 128), F32)),
        in_specs=[_HBM_SPEC] * (2 * n) + dep_specs,
        out_specs=(_SEM_SPEC, _SEM_SPEC, *[_HBM_SPEC] * (2 * n), pl.BlockSpec(memory_space=pltpu.VMEM)),
        input_output_aliases={i: 2 + i for i in range(2 * n)},
        compiler_params=pltpu.CompilerParams(has_side_effects=_DATAFLOW),
    )(*[pltpu.with_memory_space_constraint(a, pltpu.HBM) for a in list(sources) + list(landing)], *deps)
    return outs[0], outs[1], list(outs[2:2 + n]), list(outs[2 + n:2 + 2 * n]), outs[-1]


def _push_wait(handles, after, *, name, plan):
    send_sems, recv_sems, sources, landing, _ = handles
    n = len(sources)
    after = list(after) if isinstance(after, (list, tuple)) else [after]

    def body(*refs):
        src_refs, land_refs = refs[:n], refs[n:2 * n]
        send_sems_, recv_sems_ = refs[2 * n], refs[2 * n + 1]
        transfers, _ = plan(lax.axis_index("x"), lax.axis_index("y"), lax.axis_index("c"))
        n_tr = len(transfers)
        for a in range(n):
            for k, (peer, src, dst) in enumerate(transfers):
                cp = pltpu.make_async_remote_copy(
                    src_ref=src_refs[a].at[src], dst_ref=land_refs[a].at[dst], send_sem=send_sems_.at[a * n_tr + k],
                    recv_sem=recv_sems_.at[a * n_tr + k], device_id=peer, device_id_type=pl.DeviceIdType.MESH)
                cp.wait_send()
                cp.wait_recv()

    hbm = lambda a: pltpu.HBM(a.shape, a.dtype)
    outs = pl.pallas_call(
        body, name=name, out_shape=tuple(hbm(a) for a in list(sources) + list(landing)),
        in_specs=[_HBM_SPEC] * (2 * n) + [_SEM_SPEC, _SEM_SPEC] + [pl.BlockSpec(memory_space=pl.ANY)] * len(after),
        out_specs=[_HBM_SPEC] * (2 * n), input_output_aliases={i: i for i in range(2 * n)},
        compiler_params=pltpu.CompilerParams(has_side_effects=_DATAFLOW),
    )(*sources, *landing, send_sems, recv_sems, *after)
    return list(outs[n:])


def _plan_everyone(x, y, c):
    me = 4 * x + 2 * y + c
    peers = [(1 - x if k & 4 else x, 1 - y if k & 2 else y, 1 - c if k & 1 else c) for k in range(1, N_DEV)]
    return [(p, 0, me) for p in peers], (0, me)


def _plan_owners(x, y, c):
    me = 4 * x + 2 * y + c
    peers = [(1 - x if k & 4 else x, 1 - y if k & 2 else y, 1 - c if k & 1 else c) for k in range(1, N_DEV)]
    return [((px, py, pc), 4 * px + 2 * py + pc, me) for px, py, pc in peers], None


def _plan_sibling(x, y, c):
    return [((x, y, 1 - c), 2 * chip + (1 - c), chip) for chip in range(N_CHIPS)], None


def _plan_chips(x, y, c):
    mine = 2 * x + y
    peers = [(1 - x, y), (x, 1 - y), (1 - x, 1 - y)]
    return [((px, py, c), 2 * px + py, mine) for px, py in peers], (mine, mine)


def _plan_own_block(x, y, c):
    me = 4 * x + 2 * y + c
    peers = [(x, y, 1 - c), (1 - x, y, c), (x, 1 - y, c), (1 - x, 1 - y, c)]
    return [(p, 0, me) for p in peers], None


def _plan_pass_on(x, y, c):
    slots = [4 * px + 2 * py + c for px, py in ((1 - x, y), (x, 1 - y), (1 - x, 1 - y))]
    return [((x, y, 1 - c), s, s) for s in slots], None


def _pair_sum(parts, received, *, name, tc):
    _, r, c = parts.shape
    core = lax.axis_index("c").astype(jnp.int32).reshape(1)

    def body(core_ref, p_ref, r_ref, o_ref, o2_ref):
        s = (p_ref[...].astype(F32) + r_ref[...].astype(F32)).astype(o_ref.dtype)
        o_ref[...] = s
        o2_ref[...] = s

    out = pl.BlockSpec((None, r, tc), lambda i, j, core_ref: (i, 0, j))
    return pl.pallas_call(
        body, name=name,
        grid_spec=pltpu.PrefetchScalarGridSpec(
            num_scalar_prefetch=1, grid=(N_CHIPS, c // tc),
            in_specs=[pl.BlockSpec((None, r, tc), lambda i, j, core_ref: (2 * i + core_ref[0], 0, j)),
                      pl.BlockSpec((None, r, tc), lambda i, j, core_ref: (i, 0, j))],
            out_specs=[out, out]),
        out_shape=[jax.ShapeDtypeStruct((N_CHIPS, r, c), BF16)] * 2,
        compiler_params=pltpu.CompilerParams(dimension_semantics=("parallel", "parallel"), vmem_limit_bytes=VMEM_LIMIT),
    )(core, parts, received)


def _sum_blocks(a, *, name, tc):
    nblk, r, c = a.shape

    def body(a_ref, o_ref):
        acc = a_ref[0].astype(F32)
        for i in range(1, nblk):
            acc = acc + a_ref[i].astype(F32)
        o_ref[...] = acc

    return pl.pallas_call(
        body, name=name, grid=(c // tc,),
        in_specs=[pl.BlockSpec((nblk, r, tc), lambda j: (0, 0, j))],
        out_specs=pl.BlockSpec((r, tc), lambda j: (0, j)),
        out_shape=jax.ShapeDtypeStruct((r, c), F32),
        compiler_params=pltpu.CompilerParams(dimension_semantics=("parallel",), vmem_limit_bytes=VMEM_LIMIT),
    )(a)


def _adamw(w, g, m, v, *, name, tr):
    r, c = w.shape

    def body(w_ref, g_ref, m_ref, v_ref, d_ref, mo_ref, vo_ref):
        gv = g_ref[...]
        mn = ADAM_B1 * m_ref[...] + (1.0 - ADAM_B1) * gv
        vn = ADAM_B2 * v_ref[...] + (1.0 - ADAM_B2) * jnp.square(gv)
        m_hat = mn / (1.0 - ADAM_B1 ** ADAM_STEP)
        v_hat = vn / (1.0 - ADAM_B2 ** ADAM_STEP)
        d_ref[...] = -ADAM_LR * (m_hat / (jnp.sqrt(v_hat) + ADAM_EPS) + ADAM_WD * w_ref[...])
        mo_ref[...] = mn
        vo_ref[...] = vn

    spec = pl.BlockSpec((tr, c), lambda i: (i, 0))
    return pl.pallas_call(
        body, name=name, grid=(r // tr,), in_specs=[spec] * 4, out_specs=[spec] * 3,
        out_shape=[jax.ShapeDtypeStruct((r, c), F32)] * 3,
        compiler_params=pltpu.CompilerParams(dimension_semantics=("parallel",), vmem_limit_bytes=VMEM_LIMIT),
    )(w, g, m, v)


def _in_proj_layout():
    z0, xbc0, dt0, gate0 = 8192, 12288, 18432, 18496
    hg = []
    for h in range(HG_HEADS):
        for part in range(4):
            hg.append(part * 2048 + h * HG_DK + np.arange(HG_DK))
    ssm = []
    for g in range(SSM_GROUPS):
        ssm.append(xbc0 + g * SSM_GW + np.arange(SSM_GW))
        ssm.append(xbc0 + SSM_DINNER + g * SSM_DSTATE + np.arange(SSM_DSTATE))
        ssm.append(xbc0 + SSM_DINNER + SSM_GROUPS * SSM_DSTATE + g * SSM_DSTATE + np.arange(SSM_DSTATE))
        ssm.append(np.concatenate([dt0 + g * SSM_HPG + np.arange(SSM_HPG), -np.ones(128 - SSM_HPG, np.int64)]))
        ssm.append(z0 + g * SSM_GW + np.arange(SSM_GW))
    gate = gate0 + np.arange(2 * D_MODEL)
    return np.concatenate(hg), np.concatenate(ssm), gate


def _conv_layout():
    idx = []
    for g in range(SSM_GROUPS):
        idx.append(np.concatenate([g * SSM_GW + np.arange(SSM_GW),
                                   SSM_DINNER + g * SSM_DSTATE + np.arange(SSM_DSTATE),
                                   SSM_DINNER + SSM_GROUPS * SSM_DSTATE + g * SSM_DSTATE + np.arange(SSM_DSTATE)]))
    return np.stack(idx)


def _up_layout():
    idx = []
    for g in range(FFN_G):
        idx.append(g * FFN_GW + np.arange(FFN_GW))
        idx.append(D_FF + g * FFN_GW + np.arange(FFN_GW))
    return np.concatenate(idx)


def _inverse(idx, n):
    inv = np.zeros(n, np.int64)
    pos = np.nonzero(idx >= 0)[0]
    inv[idx[pos]] = pos
    return inv


def _take_rows(a, idx, axis=0):
    idx = np.asarray(idx).reshape(-1)
    pieces, start = [], 0
    for i in range(1, len(idx) + 1):
        same_run = i < len(idx) and ((idx[i] == idx[i - 1] + 1 and idx[i - 1] >= 0) or (idx[i] < 0 and idx[i - 1] < 0))
        if same_run:
            continue
        n = i - start
        if idx[start] < 0:
            shape = list(a.shape)
            shape[axis] = n
            pieces.append(jnp.zeros(shape, a.dtype))
        else:
            pieces.append(lax.slice_in_dim(a, int(idx[start]), int(idx[start]) + n, axis=axis))
        start = i
    return pieces[0] if len(pieces) == 1 else jnp.concatenate(pieces, axis=axis)


def _copy_runs(sources, out_rows, runs, *, name, block):
    d, dtype = sources[0].shape[1], sources[0].dtype
    outs = []
    for o, rows in enumerate(out_rows):
        mine = sorted({i for i, _, oo, _, _ in runs if oo == o})
        ns, nblk = len(mine), rows // block
        sel = np.zeros(nblk, np.int32)
        idx = np.full((ns, nblk), -1, np.int64)
        for i, s, oo, t, n in runs:
            if oo == o:
                assert s % block == 0 and t % block == 0 and n % block == 0
                for b in range(n // block):
                    sel[t // block + b] = mine.index(i)
                    idx[mine.index(i), t // block + b] = s // block + b
        assert (idx.max(axis=0) >= 0).all()
        for i in range(ns):
            first = idx[i, np.nonzero(idx[i] >= 0)[0][0]]
            for b in range(nblk):
                if idx[i, b] < 0:
                    idx[i, b] = idx[i, b - 1] if b > 0 else first

        def body(sel_ref, idx_ref, *refs, ns=ns):
            srcs, out = refs[:ns], refs[ns]
            which = sel_ref[pl.program_id(0)]
            val = srcs[ns - 1][...]
            for i in range(ns - 2, -1, -1):
                val = jnp.where(which == i, srcs[i][...], val)
            out[...] = val

        in_specs = [pl.BlockSpec((block, d), (lambda i_, n_: lambda b, sel_ref, idx_ref: (idx_ref[i_ * n_ + b], 0))(i, nblk))
                    for i in range(ns)]
        outs.append(pl.pallas_call(
            body, name=f"{name}_{o}" if len(out_rows) > 1 else name,
            grid_spec=pltpu.PrefetchScalarGridSpec(
                num_scalar_prefetch=2, grid=(nblk,), in_specs=in_specs,
                out_specs=pl.BlockSpec((block, d), lambda b, sel_ref, idx_ref: (b, 0))),
            out_shape=jax.ShapeDtypeStruct((rows, d), dtype),
            compiler_params=pltpu.CompilerParams(dimension_semantics=("arbitrary",), vmem_limit_bytes=VMEM_LIMIT),
        )(jnp.asarray(sel), jnp.asarray(idx.reshape(-1), jnp.int32), *[sources[i] for i in mine]))
    return outs


_Z0, _XBC0, _DT0, _GATE0 = 8192, 12288, 18432, 18496
_B0, _C0 = _XBC0 + SSM_DINNER, _XBC0 + SSM_DINNER + SSM_GROUPS * SSM_DSTATE


def _in_proj_runs():
    runs = [(part * 2048 + h * HG_DK, 0, h * HG_BLK + part * HG_DK, HG_DK) for h in range(HG_HEADS) for part in range(4)]
    for g in range(SSM_GROUPS):
        base = g * SSM_BLK
        runs += [(_XBC0 + g * SSM_GW, 1, base, SSM_GW), (_B0 + g * SSM_DSTATE, 1, base + SSM_GW, SSM_DSTATE),
                 (_C0 + g * SSM_DSTATE, 1, base + SSM_GW + SSM_DSTATE, SSM_DSTATE),
                 (_Z0 + g * SSM_GW, 1, base + SSM_XBC + 128, SSM_GW)]
    return runs + [(_GATE0, 2, 0, 2 * D_MODEL)]


def _in_proj_to_kernel(in_t):
    d = in_t.shape[1]
    dt = jnp.pad(in_t[_DT0:_GATE0].reshape(SSM_GROUPS, SSM_HPG, d), ((0, 0), (0, 128 - SSM_HPG), (0, 0)))
    runs = [(0, src, sec, dst, n) for src, sec, dst, n in _in_proj_runs() if sec < 2]
    runs += [(1, g * 128, 1, g * SSM_BLK + SSM_XBC, 128) for g in range(SSM_GROUPS)]
    hg, ssm = _copy_runs([in_t, dt.reshape(SSM_GROUPS * 128, d)], [_Z0, SSM_GROUPS * SSM_BLK], runs,
                         name="in_proj_to_kernel_layout", block=128)
    return hg, ssm, in_t[_GATE0:]


def _in_proj_from_kernel(hg, ssm, gate):
    d = hg.shape[1]
    dt = ssm.reshape(SSM_GROUPS, SSM_BLK, d)[:, SSM_XBC:SSM_XBC + SSM_HPG].reshape(SSM_HEADS, d)
    runs = [(sec, dst, 0, src, n) for src, sec, dst, n in _in_proj_runs()] + [(3, 0, 0, _DT0, SSM_HEADS)]
    return _copy_runs([hg, ssm, gate, dt], [IN_TOTAL], runs, name="in_proj_to_global_layout", block=SSM_HEADS)[0]


def _up_to_kernel(up_t):
    runs = [(0, part * D_FF + g * FFN_GW, 0, (2 * g + part) * FFN_GW, FFN_GW) for g in range(FFN_G) for part in range(2)]
    return _copy_runs([up_t], [2 * D_FF], runs, name="up_to_kernel_layout", block=FFN_GW)[0]


def _up_from_kernel(up):
    runs = [(0, (2 * g + part) * FFN_GW, 0, part * D_FF + g * FFN_GW, FFN_GW) for g in range(FFN_G) for part in range(2)]
    return _copy_runs([up], [2 * D_FF], runs, name="up_to_global_layout", block=FFN_GW)[0]


_SMALL = (("mix_pre_norm", (1, 2048)), ("mix_post_norm", (1, 2048)), ("hg_lb_table", (2, 2048)), ("hg_out_norm", (1, 128)),
          ("ssm_conv_w", (4, 6144)), ("ssm_conv_b", (1, 6144)), ("ssm_dt_bias", (1, 64)), ("ssm_A_log", (1, 64)),
          ("ssm_D", (1, 64)), ("ssm_out_norm", (1, 4096)), ("ffn_pre_norm", (1, 2048)), ("ffn_post_norm", (1, 2048)),
          ("ffn_conv_w", (3, 5632)), ("ffn_conv_b", (1, 5632)), ("loss", (1, 1)))
_PACK_ROWS = 8 * (-(-sum(int(np.prod(s)) for _, s in _SMALL) // 1024))


def _pack(vals):
    flat = jnp.concatenate([vals[k].astype(F32).reshape(-1) for k, _ in _SMALL])
    return jnp.pad(flat, (0, _PACK_ROWS * 128 - flat.shape[0])).reshape(_PACK_ROWS, 128)


def _unpack(packed):
    flat, out, o = packed.reshape(-1), {}, 0
    for k, s in _SMALL:
        n = int(np.prod(s))
        out[k] = flat[o:o + n].reshape(s)
        o += n
    return out


def _local_step(x, target, w, p, late_weights=None, emit=lambda key, gw: None):
    t = x.shape[0]
    one = lambda a: a.reshape((1,) + a.shape)
    row = dict(rows=t, groups=1, consts=[], carries=[])

    (h1,), _ = _stage_fwd(_pre_step, name="pre_fwd", chunk=512, nc=1, xs=[(x, D_MODEL)], params=[one(p["mix_pre_norm"])],
                          ys=[(D_MODEL, BF16)], **row)
    proj_hg = _matmul_nt(h1, w["in_hg"], name="proj_hg")
    proj_ssm = _matmul_nt(h1, w["in_ssm"], name="proj_ssm")
    proj_gate = _matmul_nt(h1, w["in_gate"], name="proj_gate")

    hg = dict(rows=t, chunk=HG_CHUNK, nc=8, groups=HG_HEADS, xs=[(proj_hg, HG_BLK)], params=[p["hg_tab"], p["hg_nw"]],
              consts=_hg_consts(), carries=[(HG_DK, HG_DK)], gpb=8)
    (y_hg,), hg_saved = _stage_fwd(_hg_step, name="hg_fwd", ys=[(HG_DK, BF16)], **hg)

    ssd = dict(rows=t, chunk=SSM_CHUNK, nc=4, groups=SSM_GROUPS, xs=[(proj_ssm, SSM_BLK)],
               params=[p["conv_w"], p["conv_b"], p["dt_bias"], p["a_log"], p["d_skip"], p["ssm_nw"]],
               consts=_ssd_consts(), carries=[(4 * 128, SSM_DSTATE), (HALO, SSM_XBC)])
    (y_ssm,), ssd_saved = _stage_fwd(_ssd_step, name="ssd_fwd", ys=[(SSM_GW, BF16)], **ssd)

    if late_weights is not None:
        w = {**w, **late_weights([y_hg, y_ssm])}
    u_hg = _matmul_nn(y_hg, w["branch_hg"], name="branch_hg")
    u_ssm = _matmul_nn(y_ssm, w["branch_ssm"], name="branch_ssm")
    mix = dict(chunk=256, nc=1, xs=[(proj_gate, 2 * D_MODEL), (u_hg, D_MODEL), (u_ssm, D_MODEL)], params=[], **row)
    (mixed,), _ = _stage_fwd(_mix_step, name="mix_fwd", ys=[(D_MODEL, BF16)], **mix)
    v = _matmul_nn(mixed, w["out"], name="out_proj")
    post = dict(chunk=256, nc=1, xs=[(x, D_MODEL), (v, D_MODEL)],
                params=[one(p["mix_post_norm"]), one(p["ffn_pre_norm"])], **row)
    (x1, h2), _ = _stage_fwd(_post_step, name="post_fwd", ys=[(D_MODEL, F32), (D_MODEL, BF16)], **post)
    gu = _matmul_nt(h2, w["up"], name="ffn_up")
    ffn = dict(rows=t, chunk=256, nc=2, groups=FFN_G, xs=[(gu, 2 * FFN_GW)], params=[p["ffn_conv_w"], p["ffn_conv_b"]],
               consts=[], carries=[(HALO, FFN_GW)])
    (act,), ffn_saved = _stage_fwd(_ffn_step, name="ffn_fwd", ys=[(FFN_GW, BF16)], **ffn)
    d = _matmul_nn(act, w["down"], name="ffn_down")

    def head_step(carry, xv, civ, pv, cv):
        x1_, d_, tgt = xv

        def per_row_loss(a, b, nw):
            e = a + _rms(b, nw) - tgt
            return 0.5 * jnp.mean(e * e, axis=1, keepdims=True)

        lrow, vjp = jax.vjp(per_row_loss, x1_, d_, pv[0])
        dx1_, dd_, dnw = vjp(jnp.ones_like(lrow))
        loss = jnp.broadcast_to(jnp.sum(lrow, axis=0, keepdims=True), (1, 128))
        return [], [dx1_, dd_], [], [dnw, loss]

    (dy, dd), _, (g_ffn_post, loss) = _scan_call(
        head_step, name="loss_head", chunk=256, nc=1, xs=[(x1, D_MODEL), (d, D_MODEL), (target, D_MODEL)],
        params=[one(p["ffn_post_norm"])], ys=[(D_MODEL, F32), (D_MODEL, BF16)], accs=[(1, D_MODEL), (1, 128)], **row)

    gw = {}
    gw["down"] = _matmul_tn(act, dd, name="g_down")
    dact = _matmul_nt(dd, w["down"], name="d_act", dep=emit("down", gw))
    (dgu,), (g_fcw, g_fcb) = _stage_bwd(_ffn_step, name="ffn_bwd", saved=ffn_saved, dys=[(dact, FFN_GW)], dxs=[BF16], **ffn)
    gw["up"] = _matmul_tn(dgu, h2, name="g_up")
    dh2 = _matmul_nn(dgu, w["up"], name="d_h2", dep=emit("up", gw))
    (dx1, dv), (g_mix_post, g_ffn_pre) = _stage_bwd(_post_step, name="post_bwd", saved=[], dys=[(dy, D_MODEL), (dh2, D_MODEL)],
                                                    dxs=[F32, BF16], **post)
    gw["out"] = _matmul_tn(mixed, dv, name="g_out")
    dmixed = _matmul_nt(dv, w["out"], name="d_mixed")
    (dgate, du_hg, du_ssm), _ = _stage_bwd(_mix_step, name="mix_bwd", saved=[], dys=[(dmixed, D_MODEL)],
                                           dxs=[BF16, BF16, BF16], **mix)
    gw["in_gate"] = _matmul_tn(dgate, h1, name="g_in_gate")
    gw["branch_hg"] = _matmul_tn(y_hg, du_hg, name="g_branch_hg")
    gw["branch_ssm"] = _matmul_tn(y_ssm, du_ssm, name="g_branch_ssm")
    dy_hg = _matmul_nt(du_hg, w["branch_hg"], name="d_y_hg", dep=emit("branches", gw))
    dy_ssm = _matmul_nt(du_ssm, w["branch_ssm"], name="d_y_ssm")
    (dproj_ssm,), g_ssd = _stage_bwd(_ssd_step, name="ssd_bwd", saved=ssd_saved, dys=[(dy_ssm, SSM_GW)], dxs=[BF16], **ssd)
    gw["in_ssm"] = _matmul_tn(dproj_ssm, h1, name="g_in_ssm")
    (dproj_hg,), (g_tab, g_hg_nw) = _stage_bwd(_hg_step, name="hg_bwd", saved=hg_saved, dys=[(dy_hg, HG_DK)], dxs=[BF16], **hg)
    gw["in_hg"] = _matmul_tn(dproj_hg, h1, name="g_in_hg")
    dh_a = _matmul_nn(dproj_hg, w["in_hg"], name="d_h1_hg", dep=emit("in", gw))
    dh_b = _matmul_nn(dproj_ssm, w["in_ssm"], name="d_h1_ssm")
    dh_c = _matmul_nn(dgate, w["in_gate"], name="d_h1_gate")

    def pre_bwd_step(carry, xv, civ, pv, cv):
        x_, da, db, dc, dres = xv
        _, vjp = jax.vjp(_rms, x_, pv[0])
        dx_, dnw = vjp(da + db + dc)
        return [], [dx_ + dres], [], [dnw]

    (grad_x,), _, (g_mix_pre,) = _scan_call(
        pre_bwd_step, name="pre_bwd", chunk=256, nc=1,
        xs=[(x, D_MODEL), (dh_a, D_MODEL), (dh_b, D_MODEL), (dh_c, D_MODEL), (dx1, D_MODEL)],
        params=[one(p["mix_pre_norm"])], ys=[(D_MODEL, F32)], accs=[(1, D_MODEL)], **row)

    gp = dict(mix_pre_norm=g_mix_pre[0], mix_post_norm=g_mix_post[0], ffn_pre_norm=g_ffn_pre[0], ffn_post_norm=g_ffn_post[0],
              hg_tab=g_tab, hg_nw=g_hg_nw, conv_w=g_ssd[0], conv_b=g_ssd[1], dt_bias=g_ssd[2], a_log=g_ssd[3],
              d_skip=g_ssd[4], ssm_nw=g_ssd[5], ffn_conv_w=g_fcw, ffn_conv_b=g_fcb, loss=loss[0, :, :1])
    return grad_x, gw, gp


def _small_to_kernel_layout(s):
    conv_idx = _conv_layout()
    pad_heads = lambda a: jnp.pad(a.reshape(SSM_GROUPS, 1, SSM_HPG), ((0, 0), (0, 0), (0, 128 - SSM_HPG)))
    return dict(
        mix_pre_norm=s["mix_pre_norm"], mix_post_norm=s["mix_post_norm"], ffn_pre_norm=s["ffn_pre_norm"],
        ffn_post_norm=s["ffn_post_norm"],
        hg_tab=s["hg_lb_table"].reshape(2, HG_HEADS, HG_DK).transpose(1, 0, 2),
        hg_nw=jnp.broadcast_to(s["hg_out_norm"].reshape(1, 1, HG_DK), (HG_HEADS, 1, HG_DK)),
        conv_w=_take_rows(s["ssm_conv_w"], conv_idx, axis=1).reshape(SSM_CONV, SSM_GROUPS, SSM_XBC).transpose(1, 0, 2),
        conv_b=_take_rows(s["ssm_conv_b"], conv_idx, axis=1).reshape(SSM_GROUPS, 1, SSM_XBC),
        dt_bias=pad_heads(s["ssm_dt_bias"]), a_log=pad_heads(s["ssm_A_log"]),
        d_skip=jnp.repeat(s["ssm_D"].reshape(SSM_HEADS), SSM_HEADDIM).reshape(SSM_GROUPS, 1, SSM_GW),
        ssm_nw=s["ssm_out_norm"].reshape(SSM_GROUPS, 1, SSM_GW),
        ffn_conv_w=s["ffn_conv_w"].reshape(FFN_CONV, FFN_G, FFN_GW).transpose(1, 0, 2),
        ffn_conv_b=s["ffn_conv_b"].reshape(FFN_G, 1, FFN_GW),
    )


def _small_from_kernel_layout(g):
    conv_inv = _inverse(_conv_layout().reshape(-1), SSM_CONV_DIM)
    heads = lambda a: a[:, 0, :SSM_HPG].reshape(1, SSM_HEADS)
    return dict(
        mix_pre_norm=g["mix_pre_norm"], mix_post_norm=g["mix_post_norm"], ffn_pre_norm=g["ffn_pre_norm"],
        ffn_post_norm=g["ffn_post_norm"],
        hg_lb_table=g["hg_tab"].transpose(1, 0, 2).reshape(2, HG_HEADS * HG_DK),
        hg_out_norm=jnp.sum(g["hg_nw"], axis=0),
        ssm_conv_w=_take_rows(g["conv_w"].transpose(1, 0, 2).reshape(SSM_CONV, -1), conv_inv, axis=1),
        ssm_conv_b=_take_rows(g["conv_b"].reshape(1, -1), conv_inv, axis=1),
        ssm_dt_bias=heads(g["dt_bias"]), ssm_A_log=heads(g["a_log"]),
        ssm_D=jnp.sum(g["d_skip"].reshape(SSM_HEADS, SSM_HEADDIM), axis=1).reshape(1, SSM_HEADS),
        ssm_out_norm=g["ssm_nw"].reshape(1, SSM_DINNER),
        ffn_conv_w=g["ffn_conv_w"].transpose(1, 0, 2).reshape(FFN_CONV, D_FF),
        ffn_conv_b=g["ffn_conv_b"].reshape(1, D_FF),
        loss=g["loss"],
    )


def kernel(x, w_in, mix_pre_norm, mix_post_norm, hg_lb_table, hg_out_norm, ssm_conv_w, ssm_conv_b, ssm_dt_bias, ssm_A_log, ssm_D, ssm_out_norm, w_branch_hg, w_branch_ssm, w_out, ffn_pre_norm, ffn_post_norm, ffn_w_up, ffn_conv_w, ffn_conv_b, ffn_w_down, loss_target, m_w_in, m_mix_pre_norm, m_mix_post_norm, m_hg_lb_table, m_hg_out_norm, m_ssm_conv_w, m_ssm_conv_b, m_ssm_dt_bias, m_ssm_A_log, m_ssm_D, m_ssm_out_norm, m_w_branch_hg, m_w_branch_ssm, m_w_out, m_ffn_pre_norm, m_ffn_post_norm, m_ffn_w_up, m_ffn_conv_w, m_ffn_conv_b, m_ffn_w_down, v_w_in, v_mix_pre_norm, v_mix_post_norm, v_hg_lb_table, v_hg_out_norm, v_ssm_conv_w, v_ssm_conv_b, v_ssm_dt_bias, v_ssm_A_log, v_ssm_D, v_ssm_out_norm, v_w_branch_hg, v_w_branch_ssm, v_w_out, v_ffn_pre_norm, v_ffn_post_norm, v_ffn_w_up, v_ffn_conv_w, v_ffn_conv_b, v_ffn_w_down):
    names = ["w_in", "mix_pre_norm", "mix_post_norm", "hg_lb_table", "hg_out_norm", "ssm_conv_w", "ssm_conv_b", "ssm_dt_bias",
             "ssm_A_log", "ssm_D", "ssm_out_norm", "w_branch_hg", "w_branch_ssm", "w_out", "ffn_pre_norm", "ffn_post_norm",
             "ffn_w_up", "ffn_conv_w", "ffn_conv_b", "ffn_w_down"]
    ws = dict(zip(names, (w_in, mix_pre_norm, mix_post_norm, hg_lb_table, hg_out_norm, ssm_conv_w, ssm_conv_b, ssm_dt_bias,
                          ssm_A_log, ssm_D, ssm_out_norm, w_branch_hg, w_branch_ssm, w_out, ffn_pre_norm, ffn_post_norm,
                          ffn_w_up, ffn_conv_w, ffn_conv_b, ffn_w_down)))
    ms = dict(zip(names, (m_w_in, m_mix_pre_norm, m_mix_post_norm, m_hg_lb_table, m_hg_out_norm, m_ssm_conv_w, m_ssm_conv_b,
                          m_ssm_dt_bias, m_ssm_A_log, m_ssm_D, m_ssm_out_norm, m_w_branch_hg, m_w_branch_ssm, m_w_out,
                          m_ffn_pre_norm, m_ffn_post_norm, m_ffn_w_up, m_ffn_conv_w, m_ffn_conv_b, m_ffn_w_down)))
    vs = dict(zip(names, (v_w_in, v_mix_pre_norm, v_mix_post_norm, v_hg_lb_table, v_hg_out_norm, v_ssm_conv_w, v_ssm_conv_b,
                          v_ssm_dt_bias, v_ssm_A_log, v_ssm_D, v_ssm_out_norm, v_w_branch_hg, v_w_branch_ssm, v_w_out,
                          v_ffn_pre_norm, v_ffn_post_norm, v_ffn_w_up, v_ffn_conv_w, v_ffn_conv_b, v_ffn_w_down)))
    me = 4 * lax.axis_index("x") + 2 * lax.axis_index("y") + lax.axis_index("c")

    late_shards = [ffn_w_up[0].T.astype(BF16), w_branch_hg[0].astype(BF16), w_branch_ssm[0].astype(BF16),
                   w_out[0].astype(BF16), ffn_w_down[0].astype(BF16)]
    landing = [lax.dynamic_update_slice_in_dim(lax.empty((N_DEV,) + s.shape, s.dtype), s[None], me, axis=0)
               for s in late_shards]
    gathered = _all_gather([w_in[0].T.astype(BF16), ssm_conv_w[0], ffn_conv_w[0]], name="gather_in_proj")
    late = _push_start([s[None] for s in late_shards], landing, name="late_weights_start", plan=_plan_own_block,
                       after=gathered[1])
    in_hg, in_ssm, in_gate = _in_proj_to_kernel(gathered[0].reshape(IN_TOTAL, D_MODEL))
    w = dict(in_hg=in_hg, in_ssm=in_ssm, in_gate=in_gate)
    small = {k: ws[k] for k, _ in _SMALL[:-1]}
    small["mix_pre_norm"] = mix_pre_norm + late[4][0, 0]
    small["ssm_conv_w"] = gathered[1].transpose(1, 0, 2).reshape(SSM_CONV, SSM_CONV_DIM)
    small["ffn_conv_w"] = gathered[2].transpose(1, 0, 2).reshape(FFN_CONV, D_FF)
    small = {k: small[k].reshape(s) for k, s in _SMALL[:-1]}

    def late_weights(after):
        landed = _push_wait(late, after, name="late_weights_wait", plan=_plan_own_block)
        up_all, bhg, bssm, out, down = _push(landed, name="late_weights_pass_on", plan=_plan_pass_on)
        return dict(up=_up_to_kernel(up_all.reshape(2 * D_FF, D_MODEL)), branch_hg=bhg.reshape(D_MODEL, D_MODEL),
                    branch_ssm=bssm.reshape(SSM_DINNER, D_MODEL), out=out.reshape(D_MODEL, D_MODEL),
                    down=down.reshape(D_FF, D_MODEL))

    in_flight = []

    def launch_direct(key, named_parts):
        ks, parts = zip(*named_parts)
        landing = [lax.dynamic_update_slice_in_dim(lax.empty(p.shape, p.dtype), lax.dynamic_slice_in_dim(p, me, 1, axis=0),
                                                   me, axis=0) for p in parts]
        handles = _push_start(list(parts), landing, name="grads_to_owners_start_" + key, plan=_plan_owners)
        in_flight.append(("grads_to_owners_wait_" + key, ks, handles, _plan_owners))
        return handles[4]

    def launch_two_level(key, named_parts):
        ks, parts = zip(*named_parts)
        from_sibling = _push(list(parts), name="grads_to_sibling_" + key, out_slots=N_CHIPS, plan=_plan_sibling)
        sums = [_pair_sum(p, r, name="pair_sum_" + k, tc=256) for k, p, r in zip(ks, parts, from_sibling)]
        handles = _push_start([q for q, _ in sums], [z for _, z in sums], name="grads_to_chips_start_" + key, plan=_plan_chips)
        in_flight.append(("grads_to_chips_wait_" + key, ks, handles, _plan_chips))
        return handles[4]

    def emit(key, gw):
        blocks = lambda a: a.reshape(N_DEV, -1, D_MODEL)
        if key == "down":
            return launch_direct(key, [("ffn_w_down", blocks(gw["down"]))])
        if key == "up":
            return launch_direct(key, [("ffn_w_up", blocks(_up_from_kernel(gw["up"])))])
        if key == "branches":
            return launch_direct(key, [("w_branch_hg", blocks(gw["branch_hg"])), ("w_branch_ssm", blocks(gw["branch_ssm"])),
                                       ("w_out", blocks(gw["out"]))])
        return launch_two_level(key, [("w_in", blocks(_in_proj_from_kernel(gw["in_hg"], gw["in_ssm"], gw["in_gate"])))])

    grad_x, gw, gp = _local_step(x[0], loss_target[0], w, _small_to_kernel_layout(small), late_weights, emit)

    big_names = ["w_in", "ffn_w_up", "w_branch_hg", "w_branch_ssm", "w_out", "ffn_w_down"]
    grads = {}
    for wait_name, ks, handles, plan in in_flight:
        landed = _push_wait(handles, grad_x, name=wait_name, plan=plan)
        for k, r in zip(ks, landed):
            g = _sum_blocks(r, name="sum_" + k, tc=256)
            grads[k] = g.T if k in ("w_in", "ffn_w_up") else g
    small_all = _push([_pack(_small_from_kernel_layout(gp))[None]], name="small_to_everyone", out_slots=N_DEV,
                      plan=_plan_everyone)
    small_g = _unpack(_sum_blocks(small_all[0], name="sum_small", tc=128))
    loss = small_g.pop("loss").reshape(())
    for k, g in small_g.items():
        if k in ("ssm_conv_w", "ffn_conv_w"):
            n = g.shape[1] // N_DEV
            g = lax.dynamic_slice_in_dim(g, me * n, n, axis=1)
        grads[k] = g

    delta, new_m, new_v = {}, {}, {}
    for k in big_names:
        delta[k], new_m[k], new_v[k] = _adamw(ws[k][0], grads[k], ms[k][0], vs[k][0], name="adamw_" + k, tr=64)
    small_names = [k for k in names if k not in big_names]
    flat = lambda d: jnp.concatenate([d[k].astype(F32).reshape(-1) for k in small_names])
    n_small = sum(int(np.prod(ws[k].shape)) for k in small_names)
    rows = 8 * (-(-n_small // 1024))
    pack2 = lambda d: jnp.pad(flat(d), (0, rows * 128 - n_small)).reshape(rows, 128)
    v_packed = jnp.pad(flat(vs), (0, rows * 128 - n_small), constant_values=1.0).reshape(rows, 128)
    packed = _adamw(pack2(ws), pack2(grads), pack2(ms), v_packed, name="adamw_small", tr=rows)
    o = 0
    for k in small_names:
        n = int(np.prod(ws[k].shape))
        delta[k], new_m[k], new_v[k] = (a.reshape(-1)[o:o + n].reshape(ws[k].shape) for a in packed)
        o += n

    full = lambda d: [d[k].reshape(ws[k].shape) for k in names]
    return (loss, grad_x[None], *full(grads), *full(delta), *full(new_m), *full(new_v))
```

```python
import functools

import numpy as np
import jax
import jax.numpy as jnp
from jax import lax
from jax.experimental import pallas as pl
from jax.experimental.pallas import tpu as pltpu

F32, BF16 = jnp.float32, jnp.bfloat16

D_MODEL = 2048
EPS = 1e-6
HG_HEADS, HG_DK, HG_CHUNK = 16, 128, 64
HG_BLK = 4 * HG_DK
SSM_DINNER, SSM_HEADDIM, SSM_HEADS, SSM_GROUPS, SSM_DSTATE, SSM_CONV = 4096, 64, 64, 8, 128, 4
SSM_CHUNK = 128
SSM_GW = SSM_DINNER // SSM_GROUPS
SSM_HPG = SSM_HEADS // SSM_GROUPS
SSM_XBC = SSM_GW + 2 * SSM_DSTATE
SSM_BLK = SSM_XBC + 128 + SSM_GW
SSM_CONV_DIM = SSM_DINNER + 2 * SSM_GROUPS * SSM_DSTATE
D_FF, FFN_CONV = 5632, 3
FFN_GW = 512
FFN_G = D_FF // FFN_GW
IN_TOTAL = 22592
N_DEV = 8
HALO = 8
VMEM_LIMIT = 52 * 1024 * 1024
ADAM_LR, ADAM_B1, ADAM_B2, ADAM_EPS, ADAM_WD, ADAM_STEP = 0.001, 0.9, 0.999, 1e-08, 0.01, 10

_DIMS = {"nn": ((1,), (0,)), "nt": ((1,), (1,)), "tn": ((0,), (0,))}


def _mm_raw(a, b, mode):
    return lax.dot_general(a.astype(BF16), b.astype(BF16), (_DIMS[mode], ((), ())), preferred_element_type=F32)


@functools.partial(jax.custom_vjp, nondiff_argnums=(2,))
def _mm(a, b, mode):
    return _mm_raw(a, b, mode)


def _mm_fwd(a, b, mode):
    return _mm_raw(a, b, mode), (a, b)


def _mm_bwd(mode, res, dc):
    a, b = res
    if mode == "nn":
        return _mm_raw(dc, b, "nt"), _mm_raw(a, dc, "tn")
    if mode == "nt":
        return _mm_raw(dc, b, "nn"), _mm_raw(dc, a, "tn")
    return _mm_raw(b, dc, "nt"), _mm_raw(a, dc, "nn")


_mm.defvjp(_mm_fwd, _mm_bwd)


def _cmm_raw(m, x, mode):
    hi = x.astype(BF16)
    r1 = x - hi.astype(F32)
    mid = r1.astype(BF16)
    lo = (r1 - mid.astype(F32)).astype(BF16)
    dn = (_DIMS[mode], ((), ()))
    dot = lambda p: lax.dot_general(m, p, dn, preferred_element_type=F32)
    return dot(hi) + dot(mid) + dot(lo)


@jax.custom_vjp
def _cmm(m, x):
    return _cmm_raw(m, x, "nn")


def _cmm_fwd(m, x):
    return _cmm_raw(m, x, "nn"), m


def _cmm_bwd(m, dy):
    return jnp.zeros_like(m), _cmm_raw(m, dy, "tn")


_cmm.defvjp(_cmm_fwd, _cmm_bwd)


@functools.partial(jax.custom_vjp, nondiff_argnums=(1,))
def _sroll(x, s):
    return pltpu.roll(x, s, 0) if s else x


def _sroll_fwd(x, s):
    return _sroll(x, s), None


def _sroll_bwd(s, _, ct):
    return ((pltpu.roll(ct, ct.shape[0] - s, 0) if s else ct),)


_sroll.defvjp(_sroll_fwd, _sroll_bwd)


def _rms(x, w):
    return x * lax.rsqrt(jnp.mean(x * x, axis=-1, keepdims=True) + EPS) * w


def _softplus(x):
    return jnp.maximum(x, 0.0) + jnp.log(1.0 + jnp.exp(-jnp.abs(x)))


def _causal_conv(halo, x, w, b):
    k_taps = w.shape[0]
    xe = jnp.concatenate([halo, x], axis=0)
    out = b
    for k in range(k_taps):
        out = out + w[k:k + 1, :] * _sroll(xe, k_taps - 1 - k)[HALO:, :]
    return out


def _hg_consts():
    c = HG_CHUNK
    t = np.arange(c)
    blocks, pair = [], []
    for m in (32, 16, 8, 4, 2, 1):
        pos = t % (2 * m)
        late = pos >= m
        mid = t - pos + m
        j = t[None, :]
        mq = late[:, None] & (j >= mid[:, None]) & (j <= t[:, None])
        mk = (~late)[:, None] & (j > t[:, None]) & (j <= mid[:, None] - 1)
        blocks.append(mq | mk)
        parent = t // (2 * m)
        pair.append((parent[:, None] == parent[None, :]) & late[:, None] & (~late)[None, :])
    blocks.append(t[None, :] <= t[:, None])
    mall = jnp.asarray(np.concatenate(blocks, 0).astype(np.float32), BF16)
    pair = jnp.asarray(np.stack(pair, 0).astype(np.float32))
    eye = jnp.asarray(np.eye(c, dtype=np.float32))
    return [mall, pair, eye]


def _hg_step(carry, xs, params, consts):
    (st,) = carry
    blk = xs[0].astype(F32)
    tab, nw = params
    mall, pair, eye = consts
    c, dk = HG_CHUNK, HG_DK
    q_raw, f_raw, v, og = blk[:, :dk], blk[:, dk:2 * dk], blk[:, 2 * dk:3 * dk], blk[:, 3 * dk:]
    lb = jax.nn.sigmoid(tab[0:1, :] - tab[1:2, :])
    f = lb + (1.0 - lb) * jax.nn.sigmoid(f_raw)
    g = jnp.log(f)
    kk = 1.0 - f
    qh = jax.nn.silu(q_raw) * (HG_DK ** -0.5)
    yield
    sums = _cmm(mall, g)
    yield
    b = sums[6 * c:, :]
    fac = jnp.exp(sums[:6 * c, :])
    scores = eye * jnp.sum(qh * kk, axis=1, keepdims=True)
    b_last = jnp.sum(g, axis=0, keepdims=True)
    yield
    inter = _mm(qh * jnp.exp(b), st, "nt")
    st_new = st * jnp.exp(b_last) + _mm(v, kk * jnp.exp(b_last - b), "tn")
    yield
    for l in range(6):
        fl = fac[l * c:(l + 1) * c, :]
        scores = scores + pair[l] * _mm(qh * fl, kk * fl, "nt")
        if l % 2:
            yield
    o = _mm(scores, v, "nn") + inter
    yield
    y = _rms(o, nw) * jax.nn.silu(og)
    return [st_new], [y]


def _ssd_consts():
    t = np.arange(SSM_CHUNK)
    tril = (t[None, :] <= t[:, None]).astype(np.float32)
    return [jnp.asarray(tril, BF16), jnp.asarray(tril)]


def _ssd_step(carry, xs, params, consts):
    st, halo = carry
    blk = xs[0].astype(F32)
    conv_w, conv_b, dtb, alog, dskip, nw = params
    tril_b, tril = consts
    c = SSM_CHUNK
    raw, dtr, z = blk[:, :SSM_XBC], blk[:, SSM_XBC:SSM_XBC + 128], blk[:, SSM_XBC + 128:]
    act = jax.nn.silu(_causal_conv(halo, raw, conv_w, conv_b))
    xh, bm, cm = act[:, :SSM_GW], act[:, SSM_GW:SSM_GW + SSM_DSTATE], act[:, SSM_GW + SSM_DSTATE:]
    dt = _softplus(dtr + dtb)
    da = dt * (-jnp.exp(alog))
    acum = _cmm(tril_b, da)
    acum_t = acum.T
    a_last = jnp.sum(da, axis=0, keepdims=True)
    cb_causal = _mm(cm, bm, "nt") * tril
    lane = lax.broadcasted_iota(jnp.int32, (c, 128), 1)
    row = lax.broadcasted_iota(jnp.int32, (128, 128), 0)
    first = lane < SSM_HEADDIM
    ys, st_new = [], []
    for j in range(SSM_HPG // 2):
        xp = xh[:, 128 * j:128 * (j + 1)]
        sp = st[128 * j:128 * (j + 1), :]
        r0, r1 = 2 * j, 2 * j + 1
        col = lambda a, r: jnp.broadcast_to(a[:, r:r + 1], (c, 128))
        xdt = xp * jnp.where(first, col(dt, r0), col(dt, r1))
        yj = _mm(cm, sp, "nt") * jnp.exp(jnp.where(first, col(acum, r0), col(acum, r1)))
        for r, keep in ((r0, first), (r1, ~first)):
            dec = jnp.broadcast_to(acum[:, r:r + 1], (c, c)) - jnp.broadcast_to(acum_t[r:r + 1, :], (c, c))
            m = cb_causal * jnp.exp(jnp.minimum(dec, 0.0))
            yj = yj + _mm(m, jnp.where(keep, xdt, 0.0), "nn")
        al0, al1 = a_last[:, r0:r0 + 1], a_last[:, r1:r1 + 1]
        wts = jnp.exp(jnp.where(first, al0 - col(acum, r0), al1 - col(acum, r1)))
        st_new.append(jnp.where(row < SSM_HEADDIM, jnp.exp(al0), jnp.exp(al1)) * sp + _mm(xdt * wts, bm, "tn"))
        ys.append(yj)
    y = jnp.concatenate(ys, axis=1) + dskip * xh
    y = _rms(y * jax.nn.silu(z), nw)
    return [jnp.concatenate(st_new, axis=0), raw[c - HALO:, :]], [y]


def _ffn_step(carry, xs, params, consts):
    (halo,) = carry
    blk = xs[0].astype(F32)
    conv_w, conv_b = params
    gate, up = blk[:, :FFN_GW], blk[:, FFN_GW:]
    a = jax.nn.gelu(_causal_conv(halo, gate, conv_w, conv_b), approximate=True) * up
    return [gate[gate.shape[0] - HALO:, :]], [a]


def _pre_step(carry, xs, params, consts):
    return [], [_rms(xs[0], params[0])]


def _mix_step(carry, xs, params, consts):
    gates, uh, us = (a.astype(F32) for a in xs)
    return [], [jax.nn.sigmoid(gates[:, :D_MODEL]) * uh + jax.nn.sigmoid(gates[:, D_MODEL:]) * us]


def _post_step(carry, xs, params, consts):
    x, v = xs
    x1 = x + _rms(v, params[0])
    return [], [x1, _rms(x1, params[1])]


def _scan_call(step, *, name, rows, chunk, nc, groups, xs, cins=(), params=(), consts=(), carries=(), ys=(), couts=(),
               accs=(), reverse=False, gpb=1, multi=False):
    blk_rows = chunk * nc
    nb = rows // blk_rows
    n_chunks = rows // chunk
    assert nb * blk_rows == rows and groups % gpb == 0
    rb = (lambda i: nb - 1 - i) if reverse else (lambda i: i)
    n_x, n_ci, n_p, n_c = len(xs), len(cins), len(params), len(consts)
    n_y, n_co, n_a = len(ys), len(couts), len(accs)

    def chunk_spec(shape):
        zeros = (0,) * len(shape)
        return pl.BlockSpec((gpb, nc) + tuple(shape), lambda g, i: (g, rb(i)) + zeros)

    in_specs = [pl.BlockSpec((blk_rows, gpb * w), lambda g, i: (rb(i), g)) for _, w in xs]
    in_specs += [chunk_spec(a.shape[2:]) for a in cins]
    in_specs += [pl.BlockSpec((gpb,) + tuple(a.shape[1:]), lambda g, i: (g, 0, 0)) for a in params]
    in_specs += [pl.BlockSpec(a.shape, (lambda nd: lambda g, i: (0,) * nd)(a.ndim)) for a in consts]
    out_specs = [pl.BlockSpec((blk_rows, gpb * w), lambda g, i: (rb(i), g)) for w, _ in ys]
    out_specs += [chunk_spec(s) for s in couts]
    out_specs += [pl.BlockSpec((gpb, r, c), lambda g, i: (g, 0, 0)) for r, c in accs]
    out_shape = [jax.ShapeDtypeStruct((rows, groups * w), dt) for w, dt in ys]
    out_shape += [jax.ShapeDtypeStruct((groups, n_chunks) + tuple(s), F32) for s in couts]
    out_shape += [jax.ShapeDtypeStruct((groups, r, c), F32) for r, c in accs]
    x_widths = [w for _, w in xs]
    y_widths = [w for w, _ in ys]

    def body(*refs):
        x_refs = refs[:n_x]
        ci_refs = refs[n_x:n_x + n_ci]
        p_refs = refs[n_x + n_ci:n_x + n_ci + n_p]
        c_refs = refs[n_x + n_ci + n_p:n_x + n_ci + n_p + n_c]
        o = n_x + n_ci + n_p + n_c
        y_refs = refs[o:o + n_y]
        co_refs = refs[o + n_y:o + n_y + n_co]
        a_refs = refs[o + n_y + n_co:o + n_y + n_co + n_a]
        carry_refs = refs[o + n_y + n_co + n_a:]

        @pl.when(pl.program_id(1) == 0)
        def _():
            for s in carry_refs:
                s[...] = jnp.zeros(s.shape, F32)
            for a in a_refs:
                a[...] = jnp.zeros(a.shape, F32)

        cvals = [c[...] for c in c_refs]

        def one_chunk(i, _):
            c = (nc - 1 - i) if reverse else i
            r0 = c * chunk if isinstance(c, int) else pl.multiple_of(c * chunk, chunk)
            loaded = []
            for u in range(gpb):
                carry = [s[u] for s in carry_refs]
                xv = [x[pl.ds(r0, chunk), u * w:(u + 1) * w] for x, w in zip(x_refs, x_widths)]
                civ = [ci[u, c] for ci in ci_refs]
                loaded.append((carry, xv, civ, [p[u] for p in p_refs]))
            results = step(loaded, cvals) if multi else [step(*args, cvals) for args in loaded]
            for u, (new_carry, yv, cov, av) in enumerate(results):
                for s, val in zip(carry_refs, new_carry):
                    s[u] = val
                for y, w, val in zip(y_refs, y_widths, yv):
                    y[pl.ds(r0, chunk), u * w:(u + 1) * w] = val.astype(y.dtype)
                for co, val in zip(co_refs, cov):
                    co[u, c] = val
                for a, val in zip(a_refs, av):
                    a[u] += val
            return 0

        if nc == 1:
            one_chunk(0, 0)
        else:
            lax.fori_loop(0, nc, one_chunk, 0)

    outs = pl.pallas_call(
        body, name=name, grid=(groups // gpb, nb), in_specs=in_specs, out_specs=out_specs, out_shape=out_shape,
        scratch_shapes=[pltpu.VMEM((gpb,) + tuple(s), F32) for s in carries],
        compiler_params=pltpu.CompilerParams(dimension_semantics=("arbitrary", "arbitrary"),
                                             vmem_limit_bytes=VMEM_LIMIT),
    )(*[a for a, _ in xs], *cins, *params, *consts)
    return outs[:n_y], outs[n_y:n_y + n_co], outs[n_y + n_co:]


def _run_interleaved(step, arg_tuples):
    runs = [step(*args) for args in arg_tuples]
    if not hasattr(runs[0], "send"):
        return runs
    results, live = [None] * len(runs), list(range(len(runs)))
    while live:
        for u in list(live):
            try:
                next(runs[u])
            except StopIteration as done:
                results[u] = done.value
                live.remove(u)
    return results


def _stage_fwd(step, *, name, rows, chunk, nc, groups, xs, params, consts, carries, ys, gpb=1):
    def fstep(loaded, cv):
        outs = _run_interleaved(step, [(carry, xv, pv, cv) for carry, xv, _, pv in loaded])
        return [(new_carry, yv, carry, []) for (new_carry, yv), (carry, _, _, _) in zip(outs, loaded)]

    yv, saved, _ = _scan_call(fstep, name=name, rows=rows, chunk=chunk, nc=nc, groups=groups, xs=xs, params=params,
                              consts=consts, carries=carries, ys=ys, couts=carries, gpb=gpb, multi=True)
    return yv, saved


def _stage_bwd(step, *, name, rows, chunk, nc, groups, xs, saved, params, consts, carries, dys, dxs, gpb=1):
    n_x = len(xs)

    def bstep(loaded, cv):
        civs = [list(civ) for _, _, civ, _ in loaded]
        xvs = [list(xv_all[:n_x]) for _, xv_all, _, _ in loaded]
        pvs = [list(pv) for _, _, _, pv in loaded]
        cts = [(list(dcarry), [d.astype(F32) for d in xv_all[n_x:]]) for dcarry, xv_all, _, _ in loaded]

        def fwd(civs_, xvs_, pvs_):
            outs = _run_interleaved(step, [(c_, x_, p_, cv) for c_, x_, p_ in zip(civs_, xvs_, pvs_)])
            return [(list(new_carry), list(yv)) for new_carry, yv in outs]

        _, vjp = jax.vjp(fwd, civs, xvs, pvs)
        dcivs, dxvs, dpvs = vjp(cts)
        return [(dc, dx, [], dp) for dc, dx, dp in zip(dcivs, dxvs, dpvs)]

    dxv, _, dpv = _scan_call(bstep, name=name, rows=rows, chunk=chunk, nc=nc, groups=groups, xs=list(xs) + list(dys),
                             cins=saved, params=params, consts=consts, carries=carries,
                             ys=[(w, dt) for (_, w), dt in zip(xs, dxs)], accs=[a.shape[1:] for a in params],
                             reverse=True, gpb=gpb, multi=True)
    return dxv, dpv


def _mm_params(sem):
    return pltpu.CompilerParams(dimension_semantics=sem, vmem_limit_bytes=VMEM_LIMIT)


def _after(dep):
    return ([], []) if dep is None else ([dep], [pl.BlockSpec(memory_space=pl.ANY)])


def _matmul_nt(a, b, *, name, dep=None):
    m, k = a.shape
    n = b.shape[0]
    tm = min(1024, m)
    tn = 1024 if n % 1024 == 0 else 512
    deps, dep_specs = _after(dep)

    def body(a_ref, b_ref, *rest):
        rest[-1][...] = lax.dot_general(a_ref[...], b_ref[...], (_DIMS["nt"], ((), ())), preferred_element_type=F32)

    return pl.pallas_call(
        body, name=name, grid=(m // tm, n // tn),
        in_specs=[pl.BlockSpec((tm, k), lambda i, j: (i, 0)), pl.BlockSpec((tn, k), lambda i, j: (j, 0))] + dep_specs,
        out_specs=pl.BlockSpec((tm, tn), lambda i, j: (i, j)),
        out_shape=jax.ShapeDtypeStruct((m, n), F32),
        compiler_params=_mm_params(("parallel", "arbitrary")),
    )(a, b, *deps)


def _matmul_nn(a, b, *, name, dep=None):
    m, k = a.shape
    n = b.shape[1]
    deps, dep_specs = _after(dep)
    if k > 6144:
        tm, tk, steps = min(512, m), k // 4, 4

        def body_k(a_ref, b_ref, *rest):
            part = jnp.dot(a_ref[...], b_ref[...], preferred_element_type=F32)

            @pl.when(pl.program_id(1) == 0)
            def _():
                rest[-1][...] = part

            @pl.when(pl.program_id(1) != 0)
            def _():
                rest[-1][...] += part

        return pl.pallas_call(
            body_k, name=name, grid=(m // tm, steps),
            in_specs=[pl.BlockSpec((tm, tk), lambda i, j: (i, j)), pl.BlockSpec((tk, n), lambda i, j: (j, 0))] + dep_specs,
            out_specs=pl.BlockSpec((tm, n), lambda i, j: (i, 0)),
            out_shape=jax.ShapeDtypeStruct((m, n), F32),
            compiler_params=_mm_params(("parallel", "arbitrary")),
        )(a, b, *deps)
    tm, tn = (1024, 1024) if k <= 4096 else (1024, 512)
    tm = min(tm, m)

    def body(a_ref, b_ref, *rest):
        rest[-1][...] = jnp.dot(a_ref[...], b_ref[...], preferred_element_type=F32)

    return pl.pallas_call(
        body, name=name, grid=(m // tm, n // tn),
        in_specs=[pl.BlockSpec((tm, k), lambda i, j: (i, 0)), pl.BlockSpec((k, tn), lambda i, j: (0, j))] + dep_specs,
        out_specs=pl.BlockSpec((tm, tn), lambda i, j: (i, j)),
        out_shape=jax.ShapeDtypeStruct((m, n), F32),
        compiler_params=_mm_params(("parallel", "arbitrary")),
    )(a, b, *deps)


def _matmul_tn(x, y, *, name, tp=512, tq=512):
    t, p = x.shape
    q = y.shape[1]

    def body(x_ref, y_ref, o_ref):
        o_ref[...] = lax.dot_general(x_ref[...], y_ref[...], (_DIMS["tn"], ((), ())),
                                     preferred_element_type=F32).astype(o_ref.dtype)

    return pl.pallas_call(
        body, name=name, grid=(p // tp, q // tq),
        in_specs=[pl.BlockSpec((t, tp), lambda i, j: (0, i)), pl.BlockSpec((t, tq), lambda i, j: (0, j))],
        out_specs=pl.BlockSpec((tp, tq), lambda i, j: (i, j)),
        out_shape=jax.ShapeDtypeStruct((p, q), BF16),
        compiler_params=_mm_params(("parallel", "arbitrary")),
    )(x, y)


N_CHIPS = N_DEV // 2


def _all_gather(arrays, *, name):
    n = len(arrays)
    out_shape = [jax.ShapeDtypeStruct((N_DEV,) + tuple(a.shape), a.dtype) for a in arrays]

    def body(*refs):
        in_refs, out_refs = refs[:n], refs[n:2 * n]
        send_sems, recv_sems, local_sems = refs[2 * n:]
        x, y, c = lax.axis_index("x"), lax.axis_index("y"), lax.axis_index("c")
        me, sibling = (x, y, c), (x, y, 1 - c)
        chips = [(1 - x, y), (x, 1 - y), (1 - x, 1 - y)]

        def copy(a, k, block, to, src=None):
            slot = out_refs[a].at[4 * block[0] + 2 * block[1] + block[2]]
            return pltpu.make_async_remote_copy(
                src_ref=slot if src is None else src, dst_ref=slot, send_sem=send_sems.at[a, k],
                recv_sem=recv_sems.at[a, k], device_id=to, device_id_type=pl.DeviceIdType.MESH)

        mine = [pltpu.make_async_copy(in_refs[a], out_refs[a].at[4 * x + 2 * y + c], local_sems.at[a]) for a in range(n)]
        first = []
        for a in range(n):
            first.append(copy(a, 0, me, sibling, src=in_refs[a]))
            first += [copy(a, 1 + j, me, (*chip, c), src=in_refs[a]) for j, chip in enumerate(chips)]
        for cp in mine + first:
            cp.start()
        passed = []
        for j, chip in enumerate(chips):
            for a in range(n):
                copy(a, 1 + j, (*chip, c), me).wait_recv()
                passed.append(copy(a, 4 + j, (*chip, c), sibling))
                passed[-1].start()
        for a in range(n):
            copy(a, 0, sibling, me).wait_recv()
            for j, chip in enumerate(chips):
                copy(a, 4 + j, (*chip, 1 - c), me).wait_recv()
        for cp in first + passed:
            cp.wait_send()
        for cp in mine:
            cp.wait()

    any_spec = pl.BlockSpec(memory_space=pl.ANY)
    return pl.pallas_call(
        body, name=name, in_specs=[any_spec] * n, out_specs=[any_spec] * n, out_shape=out_shape,
        scratch_shapes=[pltpu.SemaphoreType.DMA((n, N_DEV - 1)), pltpu.SemaphoreType.DMA((n, N_DEV - 1)),
                        pltpu.SemaphoreType.DMA((n,))],
        compiler_params=pltpu.CompilerParams(has_side_effects=True),
    )(*arrays)


def _push(arrays, *, name, plan, out_slots=None):
    n = len(arrays)
    in_place = out_slots is None
    out_shape = [jax.ShapeDtypeStruct(((a.shape[0] if in_place else out_slots),) + tuple(a.shape[1:]), a.dtype) for a in arrays]
    n_tr = len(plan(0, 0, 0)[0])

    def body(*refs):
        in_refs, out_refs = refs[:n], refs[n:2 * n]
        send_sems, recv_sems, local_sems = refs[2 * n:]
        src_refs = out_refs if in_place else in_refs
        transfers, local = plan(lax.axis_index("x"), lax.axis_index("y"), lax.axis_index("c"))
        copies = []
        for a in range(n):
            if local is not None:
                copies.append(pltpu.make_async_copy(src_refs[a].at[local[0]], out_refs[a].at[local[1]], local_sems.at[a]))
            for k, (peer, src, dst) in enumerate(transfers):
                copies.append(pltpu.make_async_remote_copy(
                    src_ref=src_refs[a].at[src], dst_ref=out_refs[a].at[dst], send_sem=send_sems.at[a, k],
                    recv_sem=recv_sems.at[a, k], device_id=peer, device_id_type=pl.DeviceIdType.MESH))
        for cp in copies:
            cp.start()
        for cp in copies:
            cp.wait()

    any_spec = pl.BlockSpec(memory_space=pl.ANY)
    return pl.pallas_call(
        body, name=name, in_specs=[any_spec] * n, out_specs=[any_spec] * n, out_shape=out_shape,
        input_output_aliases={a: a for a in range(n)} if in_place else {},
        scratch_shapes=[pltpu.SemaphoreType.DMA((n, n_tr)), pltpu.SemaphoreType.DMA((n, n_tr)),
                        pltpu.SemaphoreType.DMA((n,))],
        compiler_params=pltpu.CompilerParams(has_side_effects=True),
    )(*arrays)


_HBM_SPEC = pl.BlockSpec(memory_space=pltpu.HBM)
_SEM_SPEC = pl.BlockSpec(memory_space=pltpu.SEMAPHORE)
_DATAFLOW = pltpu.SideEffectType.DATAFLOW_SIDE_EFFECTING


def _push_start(sources, landing, *, name, plan, after=None):
    n = len(sources)
    n_tr = len(plan(0, 0, 0)[0])
    deps, dep_specs = _after(after)

    def body(*refs):
        src_refs, land_refs = refs[:n], refs[n:2 * n]
        o = 2 * n + len(deps)
        send_sems, recv_sems, token = refs[o], refs[o + 1], refs[-1]
        transfers, _ = plan(lax.axis_index("x"), lax.axis_index("y"), lax.axis_index("c"))
        for a in range(n):
            for k, (peer, src, dst) in enumerate(transfers):
                pltpu.make_async_remote_copy(
                    src_ref=src_refs[a].at[src], dst_ref=land_refs[a].at[dst], send_sem=send_sems.at[a * n_tr + k],
                    recv_sem=recv_sems.at[a * n_tr + k], device_id=peer, device_id_type=pl.DeviceIdType.MESH).start()
        token[...] = jnp.zeros(token.shape, token.dtype)

    hbm = lambda a: pltpu.HBM(a.shape, a.dtype)
    outs = pl.pallas_call(
        body, name=name,
        out_shape=(pltpu.SemaphoreType.DMA((n * n_tr,)), pltpu.SemaphoreType.DMA((n * n_tr,)), *[hbm(a) for a in sources],
                   *[hbm(a) for a in landing], jax.ShapeDtypeStruct((8, 128), F32)),
        in_specs=[_HBM_SPEC] * (2 * n) + dep_specs,
        out_specs=(_SEM_SPEC, _SEM_SPEC, *[_HBM_SPEC] * (2 * n), pl.BlockSpec(memory_space=pltpu.VMEM)),
        input_output_aliases={i: 2 + i for i in range(2 * n)},
        compiler_params=pltpu.CompilerParams(has_side_effects=_DATAFLOW),
    )(*[pltpu.with_memory_space_constraint(a, pltpu.HBM) for a in list(sources) + list(landing)], *deps)
    return outs[0], outs[1], list(outs[2:2 + n]), list(outs[2 + n:2 + 2 * n]), outs[-1]


def _push_wait(handles, after, *, name, plan):
    send_sems, recv_sems, sources, landing, _ = handles
    n = len(sources)
    after = list(after) if isinstance(after, (list, tuple)) else [after]

    def body(*refs):
        src_refs, land_refs = refs[:n], refs[n:2 * n]
        send_sems_, recv_sems_ = refs[2 * n], refs[2 * n + 1]
        transfers, _ = plan(lax.axis_index("x"), lax.axis_index("y"), lax.axis_index("c"))
        n_tr = len(transfers)
        for a in range(n):
            for k, (peer, src, dst) in enumerate(transfers):
                cp = pltpu.make_async_remote_copy(
                    src_ref=src_refs[a].at[src], dst_ref=land_refs[a].at[dst], send_sem=send_sems_.at[a * n_tr + k],
                    recv_sem=recv_sems_.at[a * n_tr + k], device_id=peer, device_id_type=pl.DeviceIdType.MESH)
                cp.wait_send()
                cp.wait_recv()

    hbm = lambda a: pltpu.HBM(a.shape, a.dtype)
    outs = pl.pallas_call(
        body, name=name, out_shape=tuple(hbm(a) for a in list(sources) + list(landing)),
        in_specs=[_HBM_SPEC] * (2 * n) + [_SEM_SPEC, _SEM_SPEC] + [pl.BlockSpec(memory_space=pl.ANY)] * len(after),
        out_specs=[_HBM_SPEC] * (2 * n), input_output_aliases={i: i for i in range(2 * n)},
        compiler_params=pltpu.CompilerParams(has_side_effects=_DATAFLOW),
    )(*sources, *landing, send_sems, recv_sems, *after)
    return list(outs[n:])


def _plan_everyone(x, y, c):
    me = 4 * x + 2 * y + c
    peers = [(1 - x if k & 4 else x, 1 - y if k & 2 else y, 1 - c if k & 1 else c) for k in range(1, N_DEV)]
    return [(p, 0, me) for p in peers], (0, me)


def _plan_owners(x, y, c):
    me = 4 * x + 2 * y + c
    peers = [(1 - x if k & 4 else x, 1 - y if k & 2 else y, 1 - c if k & 1 else c) for k in range(1, N_DEV)]
    return [((px, py, pc), 4 * px + 2 * py + pc, me) for px, py, pc in peers], None


def _plan_sibling(x, y, c):
    return [((x, y, 1 - c), 2 * chip + (1 - c), chip) for chip in range(N_CHIPS)], None


def _plan_chips(x, y, c):
    mine = 2 * x + y
    peers = [(1 - x, y), (x, 1 - y), (1 - x, 1 - y)]
    return [((px, py, c), 2 * px + py, mine) for px, py in peers], (mine, mine)


def _plan_own_block(x, y, c):
    me = 4 * x + 2 * y + c
    peers = [(x, y, 1 - c), (1 - x, y, c), (x, 1 - y, c), (1 - x, 1 - y, c)]
    return [(p, 0, me) for p in peers], None


def _plan_pass_on(x, y, c):
    slots = [4 * px + 2 * py + c for px, py in ((1 - x, y), (x, 1 - y), (1 - x, 1 - y))]
    return [((x, y, 1 - c), s, s) for s in slots], None


def _pair_sum(parts, received, *, name, tc):
    _, r, c = parts.shape
    core = lax.axis_index("c").astype(jnp.int32).reshape(1)

    def body(core_ref, p_ref, r_ref, o_ref, o2_ref):
        s = (p_ref[...].astype(F32) + r_ref[...].astype(F32)).astype(o_ref.dtype)
        o_ref[...] = s
        o2_ref[...] = s

    out = pl.BlockSpec((None, r, tc), lambda i, j, core_ref: (i, 0, j))
    return pl.pallas_call(
        body, name=name,
        grid_spec=pltpu.PrefetchScalarGridSpec(
            num_scalar_prefetch=1, grid=(N_CHIPS, c // tc),
            in_specs=[pl.BlockSpec((None, r, tc), lambda i, j, core_ref: (2 * i + core_ref[0], 0, j)),
                      pl.BlockSpec((None, r, tc), lambda i, j, core_ref: (i, 0, j))],
            out_specs=[out, out]),
        out_shape=[jax.ShapeDtypeStruct((N_CHIPS, r, c), BF16)] * 2,
        compiler_params=pltpu.CompilerParams(dimension_semantics=("parallel", "parallel"), vmem_limit_bytes=VMEM_LIMIT),
    )(core, parts, received)


def _sum_blocks(a, *, name, tc):
    nblk, r, c = a.shape

    def body(a_ref, o_ref):
        acc = a_ref[0].astype(F32)
        for i in range(1, nblk):
            acc = acc + a_ref[i].astype(F32)
        o_ref[...] = acc

    return pl.pallas_call(
        body, name=name, grid=(c // tc,),
        in_specs=[pl.BlockSpec((nblk, r, tc), lambda j: (0, 0, j))],
        out_specs=pl.BlockSpec((r, tc), lambda j: (0, j)),
        out_shape=jax.ShapeDtypeStruct((r, c), F32),
        compiler_params=pltpu.CompilerParams(dimension_semantics=("parallel",), vmem_limit_bytes=VMEM_LIMIT),
    )(a)


def _adamw(w, g, m, v, *, name, tr):
    r, c = w.shape

    def body(w_ref, g_ref, m_ref, v_ref, d_ref, mo_ref, vo_ref):
        gv = g_ref[...]
        mn = ADAM_B1 * m_ref[...] + (1.0 - ADAM_B1) * gv
        vn = ADAM_B2 * v_ref[...] + (1.0 - ADAM_B2) * jnp.square(gv)
        m_hat = mn / (1.0 - ADAM_B1 ** ADAM_STEP)
        v_hat = vn / (1.0 - ADAM_B2 ** ADAM_STEP)
        d_ref[...] = -ADAM_LR * (m_hat / (jnp.sqrt(v_hat) + ADAM_EPS) + ADAM_WD * w_ref[...])
        mo_ref[...] = mn
        vo_ref[...] = vn

    spec = pl.BlockSpec((tr, c), lambda i: (i, 0))
    return pl.pallas_call(
        body, name=name, grid=(r // tr,), in_specs=[spec] * 4, out_specs=[spec] * 3,
        out_shape=[jax.ShapeDtypeStruct((r, c), F32)] * 3,
        compiler_params=pltpu.CompilerParams(dimension_semantics=("parallel",), vmem_limit_bytes=VMEM_LIMIT),
    )(w, g, m, v)


def _in_proj_layout():
    z0, xbc0, dt0, gate0 = 8192, 12288, 18432, 18496
    hg = []
    for h in range(HG_HEADS):
        for part in range(4):
            hg.append(part * 2048 + h * HG_DK + np.arange(HG_DK))
    ssm = []
    for g in range(SSM_GROUPS):
        ssm.append(xbc0 + g * SSM_GW + np.arange(SSM_GW))
        ssm.append(xbc0 + SSM_DINNER + g * SSM_DSTATE + np.arange(SSM_DSTATE))
        ssm.append(xbc0 + SSM_DINNER + SSM_GROUPS * SSM_DSTATE + g * SSM_DSTATE + np.arange(SSM_DSTATE))
        ssm.append(np.concatenate([dt0 + g * SSM_HPG + np.arange(SSM_HPG), -np.ones(128 - SSM_HPG, np.int64)]))
        ssm.append(z0 + g * SSM_GW + np.arange(SSM_GW))
    gate = gate0 + np.arange(2 * D_MODEL)
    return np.concatenate(hg), np.concatenate(ssm), gate


def _conv_layout():
    idx = []
    for g in range(SSM_GROUPS):
        idx.append(np.concatenate([g * SSM_GW + np.arange(SSM_GW),
                                   SSM_DINNER + g * SSM_DSTATE + np.arange(SSM_DSTATE),
                                   SSM_DINNER + SSM_GROUPS * SSM_DSTATE + g * SSM_DSTATE + np.arange(SSM_DSTATE)]))
    return np.stack(idx)


def _up_layout():
    idx = []
    for g in range(FFN_G):
        idx.append(g * FFN_GW + np.arange(FFN_GW))
        idx.append(D_FF + g * FFN_GW + np.arange(FFN_GW))
    return np.concatenate(idx)


def _inverse(idx, n):
    inv = np.zeros(n, np.int64)
    pos = np.nonzero(idx >= 0)[0]
    inv[idx[pos]] = pos
    return inv


def _take_rows(a, idx, axis=0):
    idx = np.asarray(idx).reshape(-1)
    pieces, start = [], 0
    for i in range(1, len(idx) + 1):
        same_run = i < len(idx) and ((idx[i] == idx[i - 1] + 1 and idx[i - 1] >= 0) or (idx[i] < 0 and idx[i - 1] < 0))
        if same_run:
            continue
        n = i - start
        if idx[start] < 0:
            shape = list(a.shape)
            shape[axis] = n
            pieces.append(jnp.zeros(shape, a.dtype))
        else:
            pieces.append(lax.slice_in_dim(a, int(idx[start]), int(idx[start]) + n, axis=axis))
        start = i
    return pieces[0] if len(pieces) == 1 else jnp.concatenate(pieces, axis=axis)


def _copy_runs(sources, out_rows, runs, *, name, block, total_rows=None, into=None):
    d, dtype = sources[0].shape[1], sources[0].dtype
    outs = []
    base = 0 if into is None else into[1] // block
    extra, extra_specs = ([], []) if into is None else ([into[0]], [pl.BlockSpec(memory_space=pl.ANY)])
    for o, rows in enumerate(out_rows):
        mine = sorted({i for i, _, oo, _, _ in runs if oo == o})
        ns, nblk = len(mine), rows // block
        sel = np.zeros(nblk, np.int32)
        idx = np.full((ns, nblk), -1, np.int64)
        for i, s, oo, t, n in runs:
            if oo == o:
                assert s % block == 0 and t % block == 0 and n % block == 0
                for b in range(n // block):
                    sel[t // block + b] = mine.index(i)
                    idx[mine.index(i), t // block + b] = s // block + b
        assert (idx.max(axis=0) >= 0).all()
        for i in range(ns):
            first = idx[i, np.nonzero(idx[i] >= 0)[0][0]]
            for b in range(nblk):
                if idx[i, b] < 0:
                    idx[i, b] = idx[i, b - 1] if b > 0 else first

        def body(sel_ref, idx_ref, *refs, ns=ns):
            srcs, out = refs[:ns], refs[-1]
            which = sel_ref[pl.program_id(0)]
            val = srcs[ns - 1][...]
            for i in range(ns - 2, -1, -1):
                val = jnp.where(which == i, srcs[i][...], val)
            out[...] = val

        in_specs = [pl.BlockSpec((block, d), (lambda i_, n_: lambda b, sel_ref, idx_ref: (idx_ref[i_ * n_ + b], 0))(i, nblk))
                    for i in range(ns)]
        full_rows = into[0].shape[0] if into is not None else (total_rows or rows)
        outs.append(pl.pallas_call(
            body, name=f"{name}_{o}" if len(out_rows) > 1 else name,
            grid_spec=pltpu.PrefetchScalarGridSpec(
                num_scalar_prefetch=2, grid=(nblk,), in_specs=in_specs + extra_specs,
                out_specs=pl.BlockSpec((block, d), lambda b, sel_ref, idx_ref: (base + b, 0))),
            out_shape=jax.ShapeDtypeStruct((full_rows, d), dtype),
            input_output_aliases={} if into is None else {2 + ns: 0},
            compiler_params=pltpu.CompilerParams(dimension_semantics=("arbitrary",), vmem_limit_bytes=VMEM_LIMIT),
        )(jnp.asarray(sel), jnp.asarray(idx.reshape(-1), jnp.int32), *[sources[i] for i in mine], *extra))
    return outs


_Z0, _XBC0, _DT0, _GATE0 = 8192, 12288, 18432, 18496
_B0, _C0 = _XBC0 + SSM_DINNER, _XBC0 + SSM_DINNER + SSM_GROUPS * SSM_DSTATE


def _in_proj_runs():
    runs = [(part * 2048 + h * HG_DK, 0, h * HG_BLK + part * HG_DK, HG_DK) for h in range(HG_HEADS) for part in range(4)]
    for g in range(SSM_GROUPS):
        base = g * SSM_BLK
        runs += [(_XBC0 + g * SSM_GW, 1, base, SSM_GW), (_B0 + g * SSM_DSTATE, 1, base + SSM_GW, SSM_DSTATE),
                 (_C0 + g * SSM_DSTATE, 1, base + SSM_GW + SSM_DSTATE, SSM_DSTATE),
                 (_Z0 + g * SSM_GW, 1, base + SSM_XBC + 128, SSM_GW)]
    return runs + [(_GATE0, 2, 0, 2 * D_MODEL)]


def _in_proj_to_kernel(in_t):
    d = in_t.shape[1]
    dt = jnp.pad(in_t[_DT0:_GATE0].reshape(SSM_GROUPS, SSM_HPG, d), ((0, 0), (0, 128 - SSM_HPG), (0, 0)))
    runs = [(0, src, sec, dst, n) for src, sec, dst, n in _in_proj_runs() if sec < 2]
    runs += [(1, g * 128, 1, g * SSM_BLK + SSM_XBC, 128) for g in range(SSM_GROUPS)]
    hg, ssm = _copy_runs([in_t, dt.reshape(SSM_GROUPS * 128, d)], [_Z0, SSM_GROUPS * SSM_BLK], runs,
                         name="in_proj_to_kernel_layout", block=128)
    return hg, ssm, in_t[_GATE0:]


def _in_proj_from_kernel(hg, ssm, gate):
    d = hg.shape[1]
    dt = ssm.reshape(SSM_GROUPS, SSM_BLK, d)[:, SSM_XBC:SSM_XBC + SSM_HPG].reshape(SSM_HEADS, d)
    runs = [(sec, dst, 0, src, n) for src, sec, dst, n in _in_proj_runs() if sec < 2]
    main = _copy_runs([hg, ssm], [_DT0], runs, name="in_proj_to_global_layout", block=128, total_rows=IN_TOTAL)[0]
    tail = [(0, 0, 0, 0, SSM_HEADS), (1, 0, 0, SSM_HEADS, 2 * D_MODEL)]
    return _copy_runs([dt, gate], [IN_TOTAL - _DT0], tail, name="in_proj_to_global_layout_tail", block=SSM_HEADS,
                      into=(main, _DT0))[0]


def _up_to_kernel(up_t):
    runs = [(0, part * D_FF + g * FFN_GW, 0, (2 * g + part) * FFN_GW, FFN_GW) for g in range(FFN_G) for part in range(2)]
    return _copy_runs([up_t], [2 * D_FF], runs, name="up_to_kernel_layout", block=FFN_GW)[0]


def _up_from_kernel(up):
    runs = [(0, (2 * g + part) * FFN_GW, 0, part * D_FF + g * FFN_GW, FFN_GW) for g in range(FFN_G) for part in range(2)]
    return _copy_runs([up], [2 * D_FF], runs, name="up_to_global_layout", block=FFN_GW)[0]


_SMALL = (("mix_pre_norm", (1, 2048)), ("mix_post_norm", (1, 2048)), ("hg_lb_table", (2, 2048)), ("hg_out_norm", (1, 128)),
          ("ssm_conv_w", (4, 6144)), ("ssm_conv_b", (1, 6144)), ("ssm_dt_bias", (1, 64)), ("ssm_A_log", (1, 64)),
          ("ssm_D", (1, 64)), ("ssm_out_norm", (1, 4096)), ("ffn_pre_norm", (1, 2048)), ("ffn_post_norm", (1, 2048)),
          ("ffn_conv_w", (3, 5632)), ("ffn_conv_b", (1, 5632)), ("loss", (1, 1)))
_PACK_ROWS = 8 * (-(-sum(int(np.prod(s)) for _, s in _SMALL) // 1024))


def _pack(vals):
    flat = jnp.concatenate([vals[k].astype(F32).reshape(-1) for k, _ in _SMALL])
    return jnp.pad(flat, (0, _PACK_ROWS * 128 - flat.shape[0])).reshape(_PACK_ROWS, 128)


def _unpack(packed):
    flat, out, o = packed.reshape(-1), {}, 0
    for k, s in _SMALL:
        n = int(np.prod(s))
        out[k] = flat[o:o + n].reshape(s)
        o += n
    return out


def _local_step(x, target, w, p, late_weights=None, emit=lambda key, gw: None):
    t = x.shape[0]
    one = lambda a: a.reshape((1,) + a.shape)
    row = dict(rows=t, groups=1, consts=[], carries=[])

    (h1,), _ = _stage_fwd(_pre_step, name="pre_fwd", chunk=512, nc=1, xs=[(x, D_MODEL)], params=[one(p["mix_pre_norm"])],
                          ys=[(D_MODEL, BF16)], **row)
    proj_hg = _matmul_nt(h1, w["in_hg"], name="proj_hg")
    proj_ssm = _matmul_nt(h1, w["in_ssm"], name="proj_ssm")
    proj_gate = _matmul_nt(h1, w["in_gate"], name="proj_gate")

    hg = dict(rows=t, chunk=HG_CHUNK, nc=8, groups=HG_HEADS, xs=[(proj_hg, HG_BLK)], params=[p["hg_tab"], p["hg_nw"]],
              consts=_hg_consts(), carries=[(HG_DK, HG_DK)], gpb=8)
    (y_hg,), hg_saved = _stage_fwd(_hg_step, name="hg_fwd", ys=[(HG_DK, BF16)], **hg)

    ssd = dict(rows=t, chunk=SSM_CHUNK, nc=4, groups=SSM_GROUPS, xs=[(proj_ssm, SSM_BLK)],
               params=[p["conv_w"], p["conv_b"], p["dt_bias"], p["a_log"], p["d_skip"], p["ssm_nw"]],
               consts=_ssd_consts(), carries=[(4 * 128, SSM_DSTATE), (HALO, SSM_XBC)])
    (y_ssm,), ssd_saved = _stage_fwd(_ssd_step, name="ssd_fwd", ys=[(SSM_GW, BF16)], **ssd)

    if late_weights is not None:
        w = {**w, **late_weights([y_hg, y_ssm])}
    u_hg = _matmul_nn(y_hg, w["branch_hg"], name="branch_hg")
    u_ssm = _matmul_nn(y_ssm, w["branch_ssm"], name="branch_ssm")
    mix = dict(chunk=256, nc=1, xs=[(proj_gate, 2 * D_MODEL), (u_hg, D_MODEL), (u_ssm, D_MODEL)], params=[], **row)
    (mixed,), _ = _stage_fwd(_mix_step, name="mix_fwd", ys=[(D_MODEL, BF16)], **mix)
    v = _matmul_nn(mixed, w["out"], name="out_proj")
    post = dict(chunk=256, nc=1, xs=[(x, D_MODEL), (v, D_MODEL)],
                params=[one(p["mix_post_norm"]), one(p["ffn_pre_norm"])], **row)
    (x1, h2), _ = _stage_fwd(_post_step, name="post_fwd", ys=[(D_MODEL, F32), (D_MODEL, BF16)], **post)
    gu = _matmul_nt(h2, w["up"], name="ffn_up")
    ffn = dict(rows=t, chunk=256, nc=2, groups=FFN_G, xs=[(gu, 2 * FFN_GW)], params=[p["ffn_conv_w"], p["ffn_conv_b"]],
               consts=[], carries=[(HALO, FFN_GW)])
    (act,), ffn_saved = _stage_fwd(_ffn_step, name="ffn_fwd", ys=[(FFN_GW, BF16)], **ffn)
    d = _matmul_nn(act, w["down"], name="ffn_down")

    def head_step(carry, xv, civ, pv, cv):
        x1_, d_, tgt = xv

        def per_row_loss(a, b, nw):
            e = a + _rms(b, nw) - tgt
            return 0.5 * jnp.mean(e * e, axis=1, keepdims=True)

        lrow, vjp = jax.vjp(per_row_loss, x1_, d_, pv[0])
        dx1_, dd_, dnw = vjp(jnp.ones_like(lrow))
        loss = jnp.broadcast_to(jnp.sum(lrow, axis=0, keepdims=True), (1, 128))
        return [], [dx1_, dd_], [], [dnw, loss]

    (dy, dd), _, (g_ffn_post, loss) = _scan_call(
        head_step, name="loss_head", chunk=256, nc=1, xs=[(x1, D_MODEL), (d, D_MODEL), (target, D_MODEL)],
        params=[one(p["ffn_post_norm"])], ys=[(D_MODEL, F32), (D_MODEL, BF16)], accs=[(1, D_MODEL), (1, 128)], **row)

    gw = {}
    gw["down"] = _matmul_tn(act, dd, name="g_down")
    dact = _matmul_nt(dd, w["down"], name="d_act", dep=emit("down", gw))
    (dgu,), (g_fcw, g_fcb) = _stage_bwd(_ffn_step, name="ffn_bwd", saved=ffn_saved, dys=[(dact, FFN_GW)], dxs=[BF16], **ffn)
    gw["up"] = _matmul_tn(dgu, h2, name="g_up")
    dh2 = _matmul_nn(dgu, w["up"], name="d_h2", dep=emit("up", gw))
    (dx1, dv), (g_mix_post, g_ffn_pre) = _stage_bwd(_post_step, name="post_bwd", saved=[], dys=[(dy, D_MODEL), (dh2, D_MODEL)],
                                                    dxs=[F32, BF16], **post)
    gw["out"] = _matmul_tn(mixed, dv, name="g_out")
    dmixed = _matmul_nt(dv, w["out"], name="d_mixed")
    (dgate, du_hg, du_ssm), _ = _stage_bwd(_mix_step, name="mix_bwd", saved=[], dys=[(dmixed, D_MODEL)],
                                           dxs=[BF16, BF16, BF16], **mix)
    gw["in_gate"] = _matmul_tn(dgate, h1, name="g_in_gate")
    gw["branch_hg"] = _matmul_tn(y_hg, du_hg, name="g_branch_hg")
    gw["branch_ssm"] = _matmul_tn(y_ssm, du_ssm, name="g_branch_ssm")
    dy_hg = _matmul_nt(du_hg, w["branch_hg"], name="d_y_hg", dep=emit("branches", gw))
    dy_ssm = _matmul_nt(du_ssm, w["branch_ssm"], name="d_y_ssm")
    (dproj_ssm,), g_ssd = _stage_bwd(_ssd_step, name="ssd_bwd", saved=ssd_saved, dys=[(dy_ssm, SSM_GW)], dxs=[BF16], **ssd)
    gw["in_ssm"] = _matmul_tn(dproj_ssm, h1, name="g_in_ssm")
    (dproj_hg,), (g_tab, g_hg_nw) = _stage_bwd(_hg_step, name="hg_bwd", saved=hg_saved, dys=[(dy_hg, HG_DK)], dxs=[BF16], **hg)
    gw["in_hg"] = _matmul_tn(dproj_hg, h1, name="g_in_hg")
    dh_a = _matmul_nn(dproj_hg, w["in_hg"], name="d_h1_hg", dep=emit("in", gw))
    dh_b = _matmul_nn(dproj_ssm, w["in_ssm"], name="d_h1_ssm")
    dh_c = _matmul_nn(dgate, w["in_gate"], name="d_h1_gate")

    def pre_bwd_step(carry, xv, civ, pv, cv):
        x_, da, db, dc, dres = xv
        _, vjp = jax.vjp(_rms, x_, pv[0])
        dx_, dnw = vjp(da + db + dc)
        return [], [dx_ + dres], [], [dnw]

    (grad_x,), _, (g_mix_pre,) = _scan_call(
        pre_bwd_step, name="pre_bwd", chunk=256, nc=1,
        xs=[(x, D_MODEL), (dh_a, D_MODEL), (dh_b, D_MODEL), (dh_c, D_MODEL), (dx1, D_MODEL)],
        params=[one(p["mix_pre_norm"])], ys=[(D_MODEL, F32)], accs=[(1, D_MODEL)], **row)

    gp = dict(mix_pre_norm=g_mix_pre[0], mix_post_norm=g_mix_post[0], ffn_pre_norm=g_ffn_pre[0], ffn_post_norm=g_ffn_post[0],
              hg_tab=g_tab, hg_nw=g_hg_nw, conv_w=g_ssd[0], conv_b=g_ssd[1], dt_bias=g_ssd[2], a_log=g_ssd[3],
              d_skip=g_ssd[4], ssm_nw=g_ssd[5], ffn_conv_w=g_fcw, ffn_conv_b=g_fcb, loss=loss[0, :, :1])
    return grad_x, gw, gp


def _small_to_kernel_layout(s):
    conv_idx = _conv_layout()
    pad_heads = lambda a: jnp.pad(a.reshape(SSM_GROUPS, 1, SSM_HPG), ((0, 0), (0, 0), (0, 128 - SSM_HPG)))
    return dict(
        mix_pre_norm=s["mix_pre_norm"], mix_post_norm=s["mix_post_norm"], ffn_pre_norm=s["ffn_pre_norm"],
        ffn_post_norm=s["ffn_post_norm"],
        hg_tab=s["hg_lb_table"].reshape(2, HG_HEADS, HG_DK).transpose(1, 0, 2),
        hg_nw=jnp.broadcast_to(s["hg_out_norm"].reshape(1, 1, HG_DK), (HG_HEADS, 1, HG_DK)),
        conv_w=_take_rows(s["ssm_conv_w"], conv_idx, axis=1).reshape(SSM_CONV, SSM_GROUPS, SSM_XBC).transpose(1, 0, 2),
        conv_b=_take_rows(s["ssm_conv_b"], conv_idx, axis=1).reshape(SSM_GROUPS, 1, SSM_XBC),
        dt_bias=pad_heads(s["ssm_dt_bias"]), a_log=pad_heads(s["ssm_A_log"]),
        d_skip=jnp.repeat(s["ssm_D"].reshape(SSM_HEADS), SSM_HEADDIM).reshape(SSM_GROUPS, 1, SSM_GW),
        ssm_nw=s["ssm_out_norm"].reshape(SSM_GROUPS, 1, SSM_GW),
        ffn_conv_w=s["ffn_conv_w"].reshape(FFN_CONV, FFN_G, FFN_GW).transpose(1, 0, 2),
        ffn_conv_b=s["ffn_conv_b"].reshape(FFN_G, 1, FFN_GW),
    )


def _small_from_kernel_layout(g):
    conv_inv = _inverse(_conv_layout().reshape(-1), SSM_CONV_DIM)
    heads = lambda a: a[:, 0, :SSM_HPG].reshape(1, SSM_HEADS)
    return dict(
        mix_pre_norm=g["mix_pre_norm"], mix_post_norm=g["mix_post_norm"], ffn_pre_norm=g["ffn_pre_norm"],
        ffn_post_norm=g["ffn_post_norm"],
        hg_lb_table=g["hg_tab"].transpose(1, 0, 2).reshape(2, HG_HEADS * HG_DK),
        hg_out_norm=jnp.sum(g["hg_nw"], axis=0),
        ssm_conv_w=_take_rows(g["conv_w"].transpose(1, 0, 2).reshape(SSM_CONV, -1), conv_inv, axis=1),
        ssm_conv_b=_take_rows(g["conv_b"].reshape(1, -1), conv_inv, axis=1),
        ssm_dt_bias=heads(g["dt_bias"]), ssm_A_log=heads(g["a_log"]),
        ssm_D=jnp.sum(g["d_skip"].reshape(SSM_HEADS, SSM_HEADDIM), axis=1).reshape(1, SSM_HEADS),
        ssm_out_norm=g["ssm_nw"].reshape(1, SSM_DINNER),
        ffn_conv_w=g["ffn_conv_w"].transpose(1, 0, 2).reshape(FFN_CONV, D_FF),
        ffn_conv_b=g["ffn_conv_b"].reshape(1, D_FF),
        loss=g["loss"],
    )


def kernel(x, w_in, mix_pre_norm, mix_post_norm, hg_lb_table, hg_out_norm, ssm_conv_w, ssm_conv_b, ssm_dt_bias, ssm_A_log, ssm_D, ssm_out_norm, w_branch_hg, w_branch_ssm, w_out, ffn_pre_norm, ffn_post_norm, ffn_w_up, ffn_conv_w, ffn_conv_b, ffn_w_down, loss_target, m_w_in, m_mix_pre_norm, m_mix_post_norm, m_hg_lb_table, m_hg_out_norm, m_ssm_conv_w, m_ssm_conv_b, m_ssm_dt_bias, m_ssm_A_log, m_ssm_D, m_ssm_out_norm, m_w_branch_hg, m_w_branch_ssm, m_w_out, m_ffn_pre_norm, m_ffn_post_norm, m_ffn_w_up, m_ffn_conv_w, m_ffn_conv_b, m_ffn_w_down, v_w_in, v_mix_pre_norm, v_mix_post_norm, v_hg_lb_table, v_hg_out_norm, v_ssm_conv_w, v_ssm_conv_b, v_ssm_dt_bias, v_ssm_A_log, v_ssm_D, v_ssm_out_norm, v_w_branch_hg, v_w_branch_ssm, v_w_out, v_ffn_pre_norm, v_ffn_post_norm, v_ffn_w_up, v_ffn_conv_w, v_ffn_conv_b, v_ffn_w_down):
    names = ["w_in", "mix_pre_norm", "mix_post_norm", "hg_lb_table", "hg_out_norm", "ssm_conv_w", "ssm_conv_b", "ssm_dt_bias",
             "ssm_A_log", "ssm_D", "ssm_out_norm", "w_branch_hg", "w_branch_ssm", "w_out", "ffn_pre_norm", "ffn_post_norm",
             "ffn_w_up", "ffn_conv_w", "ffn_conv_b", "ffn_w_down"]
    ws = dict(zip(names, (w_in, mix_pre_norm, mix_post_norm, hg_lb_table, hg_out_norm, ssm_conv_w, ssm_conv_b, ssm_dt_bias,
                          ssm_A_log, ssm_D, ssm_out_norm, w_branch_hg, w_branch_ssm, w_out, ffn_pre_norm, ffn_post_norm,
                          ffn_w_up, ffn_conv_w, ffn_conv_b, ffn_w_down)))
    ms = dict(zip(names, (m_w_in, m_mix_pre_norm, m_mix_post_norm, m_hg_lb_table, m_hg_out_norm, m_ssm_conv_w, m_ssm_conv_b,
                          m_ssm_dt_bias, m_ssm_A_log, m_ssm_D, m_ssm_out_norm, m_w_branch_hg, m_w_branch_ssm, m_w_out,
                          m_ffn_pre_norm, m_ffn_post_norm, m_ffn_w_up, m_ffn_conv_w, m_ffn_conv_b, m_ffn_w_down)))
    vs = dict(zip(names, (v_w_in, v_mix_pre_norm, v_mix_post_norm, v_hg_lb_table, v_hg_out_norm, v_ssm_conv_w, v_ssm_conv_b,
                          v_ssm_dt_bias, v_ssm_A_log, v_ssm_D, v_ssm_out_norm, v_w_branch_hg, v_w_branch_ssm, v_w_out,
                          v_ffn_pre_norm, v_ffn_post_norm, v_ffn_w_up, v_ffn_conv_w, v_ffn_conv_b, v_ffn_w_down)))
    me = 4 * lax.axis_index("x") + 2 * lax.axis_index("y") + lax.axis_index("c")

    late_shards = [ffn_w_up[0].T.astype(BF16), w_branch_hg[0].astype(BF16), w_branch_ssm[0].astype(BF16),
                   w_out[0].astype(BF16), ffn_w_down[0].astype(BF16)]
    landing = [lax.dynamic_update_slice_in_dim(lax.empty((N_DEV,) + s.shape, s.dtype), s[None], me, axis=0)
               for s in late_shards]
    gathered = _all_gather([w_in[0].T.astype(BF16), ssm_conv_w[0], ffn_conv_w[0]], name="gather_in_proj")
    late = _push_start([s[None] for s in late_shards], landing, name="late_weights_start", plan=_plan_own_block,
                       after=gathered[1])
    in_hg, in_ssm, in_gate = _in_proj_to_kernel(gathered[0].reshape(IN_TOTAL, D_MODEL))
    w = dict(in_hg=in_hg, in_ssm=in_ssm, in_gate=in_gate)
    small = {k: ws[k] for k, _ in _SMALL[:-1]}
    small["mix_pre_norm"] = mix_pre_norm + late[4][0, 0]
    small["ssm_conv_w"] = gathered[1].transpose(1, 0, 2).reshape(SSM_CONV, SSM_CONV_DIM)
    small["ffn_conv_w"] = gathered[2].transpose(1, 0, 2).reshape(FFN_CONV, D_FF)
    small = {k: small[k].reshape(s) for k, s in _SMALL[:-1]}

    def late_weights(after):
        landed = _push_wait(late, after, name="late_weights_wait", plan=_plan_own_block)
        up_all, bhg, bssm, out, down = _push(landed, name="late_weights_pass_on", plan=_plan_pass_on)
        return dict(up=_up_to_kernel(up_all.reshape(2 * D_FF, D_MODEL)), branch_hg=bhg.reshape(D_MODEL, D_MODEL),
                    branch_ssm=bssm.reshape(SSM_DINNER, D_MODEL), out=out.reshape(D_MODEL, D_MODEL),
                    down=down.reshape(D_FF, D_MODEL))

    in_flight = []

    def launch_direct(key, named_parts):
        ks, parts = zip(*named_parts)
        landing = [lax.dynamic_update_slice_in_dim(lax.empty(p.shape, p.dtype), lax.dynamic_slice_in_dim(p, me, 1, axis=0),
                                                   me, axis=0) for p in parts]
        handles = _push_start(list(parts), landing, name="grads_to_owners_start_" + key, plan=_plan_owners)
        in_flight.append(("grads_to_owners_wait_" + key, ks, handles, _plan_owners))
        return handles[4]

    def launch_two_level(key, named_parts):
        ks, parts = zip(*named_parts)
        from_sibling = _push(list(parts), name="grads_to_sibling_" + key, out_slots=N_CHIPS, plan=_plan_sibling)
        sums = [_pair_sum(p, r, name="pair_sum_" + k, tc=256) for k, p, r in zip(ks, parts, from_sibling)]
        handles = _push_start([q for q, _ in sums], [z for _, z in sums], name="grads_to_chips_start_" + key, plan=_plan_chips)
        in_flight.append(("grads_to_chips_wait_" + key, ks, handles, _plan_chips))
        return handles[4]

    def emit(key, gw):
        blocks = lambda a: a.reshape(N_DEV, -1, D_MODEL)
        if key == "down":
            return launch_direct(key, [("ffn_w_down", blocks(gw["down"]))])
        if key == "up":
            return launch_direct(key, [("ffn_w_up", blocks(_up_from_kernel(gw["up"])))])
        if key == "branches":
            return launch_direct(key, [("w_branch_hg", blocks(gw["branch_hg"])), ("w_branch_ssm", blocks(gw["branch_ssm"])),
                                       ("w_out", blocks(gw["out"]))])
        return launch_two_level(key, [("w_in", blocks(_in_proj_from_kernel(gw["in_hg"], gw["in_ssm"], gw["in_gate"])))])

    grad_x, gw, gp = _local_step(x[0], loss_target[0], w, _small_to_kernel_layout(small), late_weights, emit)

    big_names = ["w_in", "ffn_w_up", "w_branch_hg", "w_branch_ssm", "w_out", "ffn_w_down"]
    grads = {}
    for wait_name, ks, handles, plan in in_flight:
        landed = _push_wait(handles, grad_x, name=wait_name, plan=plan)
        for k, r in zip(ks, landed):
            g = _sum_blocks(r, name="sum_" + k, tc=256)
            grads[k] = g.T if k in ("w_in", "ffn_w_up") else g
    small_all = _push([_pack(_small_from_kernel_layout(gp))[None]], name="small_to_everyone", out_slots=N_DEV,
                      plan=_plan_everyone)
    small_g = _unpack(_sum_blocks(small_all[0], name="sum_small", tc=128))
    loss = small_g.pop("loss").reshape(())
    for k, g in small_g.items():
        if k in ("ssm_conv_w", "ffn_conv_w"):
            n = g.shape[1] // N_DEV
            g = lax.dynamic_slice_in_dim(g, me * n, n, axis=1)
        grads[k] = g

    delta, new_m, new_v = {}, {}, {}
    for k in big_names:
        delta[k], new_m[k], new_v[k] = _adamw(ws[k][0], grads[k], ms[k][0], vs[k][0], name="adamw_" + k, tr=64)
    small_names = [k for k in names if k not in big_names]
    flat = lambda d: jnp.concatenate([d[k].astype(F32).reshape(-1) for k in small_names])
    n_small = sum(int(np.prod(ws[k].shape)) for k in small_names)
    rows = 8 * (-(-n_small // 1024))
    pack2 = lambda d: jnp.pad(flat(d), (0, rows * 128 - n_small)).reshape(rows, 128)
    v_packed = jnp.pad(flat(vs), (0, rows * 128 - n_small), constant_values=1.0).reshape(rows, 128)
    packed = _adamw(pack2(ws), pack2(grads), pack2(ms), v_packed, name="adamw_small", tr=rows)
    o = 0
    for k in small_names:
        n = int(np.prod(ws[k].shape))
        delta[k], new_m[k], new_v[k] = (a.reshape(-1)[o:o + n].reshape(ws[k].shape) for a in packed)
        o += n

    full = lambda d: [d[k].reshape(ws[k].shape) for k in names]
    return (loss, grad_x[None], *full(grads), *full(delta), *full(new_m), *full(new_v))
```

```python
import functools

import numpy as np
import jax
import jax.numpy as jnp
from jax import lax
from jax.experimental import pallas as pl
from jax.experimental.pallas import tpu as pltpu

F32, BF16 = jnp.float32, jnp.bfloat16

D_MODEL = 2048
EPS = 1e-6
HG_HEADS, HG_DK, HG_CHUNK = 16, 128, 64
HG_BLK = 4 * HG_DK
SSM_DINNER, SSM_HEADDIM, SSM_HEADS, SSM_GROUPS, SSM_DSTATE, SSM_CONV = 4096, 64, 64, 8, 128, 4
SSM_CHUNK = 128
SSM_GW = SSM_DINNER // SSM_GROUPS
SSM_HPG = SSM_HEADS // SSM_GROUPS
SSM_XBC = SSM_GW + 2 * SSM_DSTATE
SSM_BLK = SSM_XBC + 128 + SSM_GW
SSM_CONV_DIM = SSM_DINNER + 2 * SSM_GROUPS * SSM_DSTATE
D_FF, FFN_CONV = 5632, 3
FFN_GW = 512
FFN_G = D_FF // FFN_GW
IN_TOTAL = 22592
N_DEV = 8
HALO = 8
VMEM_LIMIT = 52 * 1024 * 1024
ADAM_LR, ADAM_B1, ADAM_B2, ADAM_EPS, ADAM_WD, ADAM_STEP = 0.001, 0.9, 0.999, 1e-08, 0.01, 10

_DIMS = {"nn": ((1,), (0,)), "nt": ((1,), (1,)), "tn": ((0,), (0,))}


def _mm_raw(a, b, mode):
    return lax.dot_general(a.astype(BF16), b.astype(BF16), (_DIMS[mode], ((), ())), preferred_element_type=F32)


@functools.partial(jax.custom_vjp, nondiff_argnums=(2,))
def _mm(a, b, mode):
    return _mm_raw(a, b, mode)


def _mm_fwd(a, b, mode):
    return _mm_raw(a, b, mode), (a, b)


def _mm_bwd(mode, res, dc):
    a, b = res
    if mode == "nn":
        return _mm_raw(dc, b, "nt"), _mm_raw(a, dc, "tn")
    if mode == "nt":
        return _mm_raw(dc, b, "nn"), _mm_raw(dc, a, "tn")
    return _mm_raw(b, dc, "nt"), _mm_raw(a, dc, "nn")


_mm.defvjp(_mm_fwd, _mm_bwd)


def _cmm_raw(m, x, mode):
    hi = x.astype(BF16)
    r1 = x - hi.astype(F32)
    mid = r1.astype(BF16)
    lo = (r1 - mid.astype(F32)).astype(BF16)
    dn = (_DIMS[mode], ((), ()))
    dot = lambda p: lax.dot_general(m, p, dn, preferred_element_type=F32)
    return dot(hi) + dot(mid) + dot(lo)


@jax.custom_vjp
def _cmm(m, x):
    return _cmm_raw(m, x, "nn")


def _cmm_fwd(m, x):
    return _cmm_raw(m, x, "nn"), m


def _cmm_bwd(m, dy):
    return jnp.zeros_like(m), _cmm_raw(m, dy, "tn")


_cmm.defvjp(_cmm_fwd, _cmm_bwd)


@functools.partial(jax.custom_vjp, nondiff_argnums=(1,))
def _sroll(x, s):
    return pltpu.roll(x, s, 0) if s else x


def _sroll_fwd(x, s):
    return _sroll(x, s), None


def _sroll_bwd(s, _, ct):
    return ((pltpu.roll(ct, ct.shape[0] - s, 0) if s else ct),)


_sroll.defvjp(_sroll_fwd, _sroll_bwd)


def _rms(x, w):
    return x * lax.rsqrt(jnp.mean(x * x, axis=-1, keepdims=True) + EPS) * w


def _softplus(x):
    return jnp.maximum(x, 0.0) + jnp.log(1.0 + jnp.exp(-jnp.abs(x)))


def _causal_conv(halo, x, w, b):
    k_taps = w.shape[0]
    xe = jnp.concatenate([halo, x], axis=0)
    out = b
    for k in range(k_taps):
        out = out + w[k:k + 1, :] * _sroll(xe, k_taps - 1 - k)[HALO:, :]
    return out


def _hg_consts():
    c = HG_CHUNK
    t = np.arange(c)
    blocks, pair = [], []
    for m in (32, 16, 8, 4, 2, 1):
        pos = t % (2 * m)
        late = pos >= m
        mid = t - pos + m
        j = t[None, :]
        mq = late[:, None] & (j >= mid[:, None]) & (j <= t[:, None])
        mk = (~late)[:, None] & (j > t[:, None]) & (j <= mid[:, None] - 1)
        blocks.append(mq | mk)
        parent = t // (2 * m)
        pair.append((parent[:, None] == parent[None, :]) & late[:, None] & (~late)[None, :])
    blocks.append(t[None, :] <= t[:, None])
    mall = jnp.asarray(np.concatenate(blocks, 0).astype(np.float32), BF16)
    pair = jnp.asarray(np.stack(pair, 0).astype(np.float32))
    eye = jnp.asarray(np.eye(c, dtype=np.float32))
    return [mall, pair, eye]


def _hg_step(carry, xs, params, consts):
    (st,) = carry
    blk = xs[0].astype(F32)
    tab, nw = params
    mall, pair, eye = consts
    c, dk = HG_CHUNK, HG_DK
    q_raw, f_raw, v, og = blk[:, :dk], blk[:, dk:2 * dk], blk[:, 2 * dk:3 * dk], blk[:, 3 * dk:]
    lb = jax.nn.sigmoid(tab[0:1, :] - tab[1:2, :])
    f = lb + (1.0 - lb) * jax.nn.sigmoid(f_raw)
    g = jnp.log(f)
    kk = 1.0 - f
    qh = jax.nn.silu(q_raw) * (HG_DK ** -0.5)
    yield
    sums = _cmm(mall, g)
    yield
    b = sums[6 * c:, :]
    fac = jnp.exp(sums[:6 * c, :])
    scores = eye * jnp.sum(qh * kk, axis=1, keepdims=True)
    b_last = jnp.sum(g, axis=0, keepdims=True)
    yield
    inter = _mm(qh * jnp.exp(b), st, "nt")
    st_new = st * jnp.exp(b_last) + _mm(v, kk * jnp.exp(b_last - b), "tn")
    yield
    for l in range(6):
        fl = fac[l * c:(l + 1) * c, :]
        scores = scores + pair[l] * _mm(qh * fl, kk * fl, "nt")
        if l % 2:
            yield
    o = _mm(scores, v, "nn") + inter
    yield
    y = _rms(o, nw) * jax.nn.silu(og)
    return [st_new], [y]


def _ssd_consts():
    t = np.arange(SSM_CHUNK)
    tril = (t[None, :] <= t[:, None]).astype(np.float32)
    return [jnp.asarray(tril, BF16), jnp.asarray(tril)]


def _ssd_step(carry, xs, params, consts):
    st, halo = carry
    blk = xs[0].astype(F32)
    conv_w, conv_b, dtb, alog, dskip, nw = params
    tril_b, tril = consts
    c = SSM_CHUNK
    raw, dtr, z = blk[:, :SSM_XBC], blk[:, SSM_XBC:SSM_XBC + 128], blk[:, SSM_XBC + 128:]
    act = jax.nn.silu(_causal_conv(halo, raw, conv_w, conv_b))
    xh, bm, cm = act[:, :SSM_GW], act[:, SSM_GW:SSM_GW + SSM_DSTATE], act[:, SSM_GW + SSM_DSTATE:]
    dt = _softplus(dtr + dtb)
    da = dt * (-jnp.exp(alog))
    acum = _cmm(tril_b, da)
    acum_t = acum.T
    a_last = jnp.sum(da, axis=0, keepdims=True)
    cb_causal = _mm(cm, bm, "nt") * tril
    lane = lax.broadcasted_iota(jnp.int32, (c, 128), 1)
    row = lax.broadcasted_iota(jnp.int32, (128, 128), 0)
    first = lane < SSM_HEADDIM
    ys, st_new = [], []
    for j in range(SSM_HPG // 2):
        xp = xh[:, 128 * j:128 * (j + 1)]
        sp = st[128 * j:128 * (j + 1), :]
        r0, r1 = 2 * j, 2 * j + 1
        col = lambda a, r: jnp.broadcast_to(a[:, r:r + 1], (c, 128))
        xdt = xp * jnp.where(first, col(dt, r0), col(dt, r1))
        yj = _mm(cm, sp, "nt") * jnp.exp(jnp.where(first, col(acum, r0), col(acum, r1)))
        for r, keep in ((r0, first), (r1, ~first)):
            dec = jnp.broadcast_to(acum[:, r:r + 1], (c, c)) - jnp.broadcast_to(acum_t[r:r + 1, :], (c, c))
            m = cb_causal * jnp.exp(jnp.minimum(dec, 0.0))
            yj = yj + _mm(m, jnp.where(keep, xdt, 0.0), "nn")
        al0, al1 = a_last[:, r0:r0 + 1], a_last[:, r1:r1 + 1]
        wts = jnp.exp(jnp.where(first, al0 - col(acum, r0), al1 - col(acum, r1)))
        st_new.append(jnp.where(row < SSM_HEADDIM, jnp.exp(al0), jnp.exp(al1)) * sp + _mm(xdt * wts, bm, "tn"))
        ys.append(yj)
    y = jnp.concatenate(ys, axis=1) + dskip * xh
    y = _rms(y * jax.nn.silu(z), nw)
    return [jnp.concatenate(st_new, axis=0), raw[c - HALO:, :]], [y]


def _ffn_step(carry, xs, params, consts):
    (halo,) = carry
    blk = xs[0].astype(F32)
    conv_w, conv_b = params
    gate, up = blk[:, :FFN_GW], blk[:, FFN_GW:]
    a = jax.nn.gelu(_causal_conv(halo, gate, conv_w, conv_b), approximate=True) * up
    return [gate[gate.shape[0] - HALO:, :]], [a]


def _pre_step(carry, xs, params, consts):
    return [], [_rms(xs[0], params[0])]


def _mix_step(carry, xs, params, consts):
    gates, uh, us = (a.astype(F32) for a in xs)
    return [], [jax.nn.sigmoid(gates[:, :D_MODEL]) * uh + jax.nn.sigmoid(gates[:, D_MODEL:]) * us]


def _post_step(carry, xs, params, consts):
    x, v = xs
    x1 = x + _rms(v, params[0])
    return [], [x1, _rms(x1, params[1])]


def _scan_call(step, *, name, rows, chunk, nc, groups, xs, cins=(), params=(), consts=(), carries=(), ys=(), couts=(),
               accs=(), reverse=False, gpb=1, multi=False):
    blk_rows = chunk * nc
    nb = rows // blk_rows
    n_chunks = rows // chunk
    assert nb * blk_rows == rows and groups % gpb == 0
    rb = (lambda i: nb - 1 - i) if reverse else (lambda i: i)
    n_x, n_ci, n_p, n_c = len(xs), len(cins), len(params), len(consts)
    n_y, n_co, n_a = len(ys), len(couts), len(accs)

    def chunk_spec(shape):
        zeros = (0,) * len(shape)
        return pl.BlockSpec((gpb, nc) + tuple(shape), lambda g, i: (g, rb(i)) + zeros)

    in_specs = [pl.BlockSpec((blk_rows, gpb * w), lambda g, i: (rb(i), g)) for _, w in xs]
    in_specs += [chunk_spec(a.shape[2:]) for a in cins]
    in_specs += [pl.BlockSpec((gpb,) + tuple(a.shape[1:]), lambda g, i: (g, 0, 0)) for a in params]
    in_specs += [pl.BlockSpec(a.shape, (lambda nd: lambda g, i: (0,) * nd)(a.ndim)) for a in consts]
    out_specs = [pl.BlockSpec((blk_rows, gpb * w), lambda g, i: (rb(i), g)) for w, _ in ys]
    out_specs += [chunk_spec(s) for s in couts]
    out_specs += [pl.BlockSpec((gpb, r, c), lambda g, i: (g, 0, 0)) for r, c in accs]
    out_shape = [jax.ShapeDtypeStruct((rows, groups * w), dt) for w, dt in ys]
    out_shape += [jax.ShapeDtypeStruct((groups, n_chunks) + tuple(s), F32) for s in couts]
    out_shape += [jax.ShapeDtypeStruct((groups, r, c), F32) for r, c in accs]
    x_widths = [w for _, w in xs]
    y_widths = [w for w, _ in ys]

    def body(*refs):
        x_refs = refs[:n_x]
        ci_refs = refs[n_x:n_x + n_ci]
        p_refs = refs[n_x + n_ci:n_x + n_ci + n_p]
        c_refs = refs[n_x + n_ci + n_p:n_x + n_ci + n_p + n_c]
        o = n_x + n_ci + n_p + n_c
        y_refs = refs[o:o + n_y]
        co_refs = refs[o + n_y:o + n_y + n_co]
        a_refs = refs[o + n_y + n_co:o + n_y + n_co + n_a]
        carry_refs = refs[o + n_y + n_co + n_a:]

        @pl.when(pl.program_id(1) == 0)
        def _():
            for s in carry_refs:
                s[...] = jnp.zeros(s.shape, F32)
            for a in a_refs:
                a[...] = jnp.zeros(a.shape, F32)

        cvals = [c[...] for c in c_refs]

        def one_chunk(i, _):
            c = (nc - 1 - i) if reverse else i
            r0 = c * chunk if isinstance(c, int) else pl.multiple_of(c * chunk, chunk)
            loaded = []
            for u in range(gpb):
                carry = [s[u] for s in carry_refs]
                xv = [x[pl.ds(r0, chunk), u * w:(u + 1) * w] for x, w in zip(x_refs, x_widths)]
                civ = [ci[u, c] for ci in ci_refs]
                loaded.append((carry, xv, civ, [p[u] for p in p_refs]))
            results = step(loaded, cvals) if multi else [step(*args, cvals) for args in loaded]
            for u, (new_carry, yv, cov, av) in enumerate(results):
                for s, val in zip(carry_refs, new_carry):
                    s[u] = val
                for y, w, val in zip(y_refs, y_widths, yv):
                    y[pl.ds(r0, chunk), u * w:(u + 1) * w] = val.astype(y.dtype)
                for co, val in zip(co_refs, cov):
                    co[u, c] = val
                for a, val in zip(a_refs, av):
                    a[u] += val
            return 0

        if nc == 1:
            one_chunk(0, 0)
        else:
            lax.fori_loop(0, nc, one_chunk, 0)

    outs = pl.pallas_call(
        body, name=name, grid=(groups // gpb, nb), in_specs=in_specs, out_specs=out_specs, out_shape=out_shape,
        scratch_shapes=[pltpu.VMEM((gpb,) + tuple(s), F32) for s in carries],
        compiler_params=pltpu.CompilerParams(dimension_semantics=("arbitrary", "arbitrary"),
                                             vmem_limit_bytes=VMEM_LIMIT),
    )(*[a for a, _ in xs], *cins, *params, *consts)
    return outs[:n_y], outs[n_y:n_y + n_co], outs[n_y + n_co:]


def _run_interleaved(step, arg_tuples):
    runs = [step(*args) for args in arg_tuples]
    if not hasattr(runs[0], "send"):
        return runs
    results, live = [None] * len(runs), list(range(len(runs)))
    while live:
        for u in list(live):
            try:
                next(runs[u])
            except StopIteration as done:
                results[u] = done.value
                live.remove(u)
    return results


def _stage_fwd(step, *, name, rows, chunk, nc, groups, xs, params, consts, carries, ys, gpb=1):
    def fstep(loaded, cv):
        outs = _run_interleaved(step, [(carry, xv, pv, cv) for carry, xv, _, pv in loaded])
        return [(new_carry, yv, carry, []) for (new_carry, yv), (carry, _, _, _) in zip(outs, loaded)]

    yv, saved, _ = _scan_call(fstep, name=name, rows=rows, chunk=chunk, nc=nc, groups=groups, xs=xs, params=params,
                              consts=consts, carries=carries, ys=ys, couts=carries, gpb=gpb, multi=True)
    return yv, saved


def _stage_bwd(step, *, name, rows, chunk, nc, groups, xs, saved, params, consts, carries, dys, dxs, gpb=1):
    n_x = len(xs)

    def bstep(loaded, cv):
        civs = [list(civ) for _, _, civ, _ in loaded]
        xvs = [list(xv_all[:n_x]) for _, xv_all, _, _ in loaded]
        pvs = [list(pv) for _, _, _, pv in loaded]
        cts = [(list(dcarry), [d.astype(F32) for d in xv_all[n_x:]]) for dcarry, xv_all, _, _ in loaded]

        def fwd(civs_, xvs_, pvs_):
            outs = _run_interleaved(step, [(c_, x_, p_, cv) for c_, x_, p_ in zip(civs_, xvs_, pvs_)])
            return [(list(new_carry), list(yv)) for new_carry, yv in outs]

        _, vjp = jax.vjp(fwd, civs, xvs, pvs)
        dcivs, dxvs, dpvs = vjp(cts)
        return [(dc, dx, [], dp) for dc, dx, dp in zip(dcivs, dxvs, dpvs)]

    dxv, _, dpv = _scan_call(bstep, name=name, rows=rows, chunk=chunk, nc=nc, groups=groups, xs=list(xs) + list(dys),
                             cins=saved, params=params, consts=consts, carries=carries,
                             ys=[(w, dt) for (_, w), dt in zip(xs, dxs)], accs=[a.shape[1:] for a in params],
                             reverse=True, gpb=gpb, multi=True)
    return dxv, dpv


def _mm_params(sem):
    return pltpu.CompilerParams(dimension_semantics=sem, vmem_limit_bytes=VMEM_LIMIT)


def _after(dep):
    return ([], []) if dep is None else ([dep], [pl.BlockSpec(memory_space=pl.ANY)])


def _matmul_nt(a, b, *, name, dep=None):
    m, k = a.shape
    n = b.shape[0]
    tm = min(1024, m)
    tn = 1024 if n % 1024 == 0 else 512
    deps, dep_specs = _after(dep)

    def body(a_ref, b_ref, *rest):
        rest[-1][...] = lax.dot_general(a_ref[...], b_ref[...], (_DIMS["nt"], ((), ())), preferred_element_type=F32)

    return pl.pallas_call(
        body, name=name, grid=(m // tm, n // tn),
        in_specs=[pl.BlockSpec((tm, k), lambda i, j: (i, 0)), pl.BlockSpec((tn, k), lambda i, j: (j, 0))] + dep_specs,
        out_specs=pl.BlockSpec((tm, tn), lambda i, j: (i, j)),
        out_shape=jax.ShapeDtypeStruct((m, n), F32),
        compiler_params=_mm_params(("parallel", "arbitrary")),
    )(a, b, *deps)


def _matmul_nn(a, b, *, name, dep=None):
    m, k = a.shape
    n = b.shape[1]
    deps, dep_specs = _after(dep)
    if k > 6144:
        tm, tk, steps = min(512, m), k // 4, 4

        def body_k(a_ref, b_ref, *rest):
            part = jnp.dot(a_ref[...], b_ref[...], preferred_element_type=F32)

            @pl.when(pl.program_id(1) == 0)
            def _():
                rest[-1][...] = part

            @pl.when(pl.program_id(1) != 0)
            def _():
                rest[-1][...] += part

        return pl.pallas_call(
            body_k, name=name, grid=(m // tm, steps),
            in_specs=[pl.BlockSpec((tm, tk), lambda i, j: (i, j)), pl.BlockSpec((tk, n), lambda i, j: (j, 0))] + dep_specs,
            out_specs=pl.BlockSpec((tm, n), lambda i, j: (i, 0)),
            out_shape=jax.ShapeDtypeStruct((m, n), F32),
            compiler_params=_mm_params(("parallel", "arbitrary")),
        )(a, b, *deps)
    tm, tn = (1024, 1024) if k <= 4096 else (1024, 512)
    tm = min(tm, m)

    def body(a_ref, b_ref, *rest):
        rest[-1][...] = jnp.dot(a_ref[...], b_ref[...], preferred_element_type=F32)

    return pl.pallas_call(
        body, name=name, grid=(m // tm, n // tn),
        in_specs=[pl.BlockSpec((tm, k), lambda i, j: (i, 0)), pl.BlockSpec((k, tn), lambda i, j: (0, j))] + dep_specs,
        out_specs=pl.BlockSpec((tm, tn), lambda i, j: (i, j)),
        out_shape=jax.ShapeDtypeStruct((m, n), F32),
        compiler_params=_mm_params(("parallel", "arbitrary")),
    )(a, b, *deps)


def _matmul_tn(x, y, *, name, tp=512, tq=512):
    t, p = x.shape
    q = y.shape[1]

    def body(x_ref, y_ref, o_ref):
        o_ref[...] = lax.dot_general(x_ref[...], y_ref[...], (_DIMS["tn"], ((), ())),
                                     preferred_element_type=F32).astype(o_ref.dtype)

    return pl.pallas_call(
        body, name=name, grid=(p // tp, q // tq),
        in_specs=[pl.BlockSpec((t, tp), lambda i, j: (0, i)), pl.BlockSpec((t, tq), lambda i, j: (0, j))],
        out_specs=pl.BlockSpec((tp, tq), lambda i, j: (i, j)),
        out_shape=jax.ShapeDtypeStruct((p, q), BF16),
        compiler_params=_mm_params(("parallel", "arbitrary")),
    )(x, y)


N_CHIPS = N_DEV // 2


def _all_gather(arrays, *, name):
    n = len(arrays)
    out_shape = [jax.ShapeDtypeStruct((N_DEV,) + tuple(a.shape), a.dtype) for a in arrays]

    def body(*refs):
        in_refs, out_refs = refs[:n], refs[n:2 * n]
        send_sems, recv_sems, local_sems = refs[2 * n:]
        x, y, c = lax.axis_index("x"), lax.axis_index("y"), lax.axis_index("c")
        me, sibling = (x, y, c), (x, y, 1 - c)
        chips = [(1 - x, y), (x, 1 - y), (1 - x, 1 - y)]
        south = c == 0
        relay_to = (jnp.where(south, x, 1 - x), jnp.where(south, 1 - y, y), c)
        relayed = (jnp.where(south, 1 - x, x), jnp.where(south, y, 1 - y), c)

        def copy(a, k, block, to, src=None):
            slot = out_refs[a].at[4 * block[0] + 2 * block[1] + block[2]]
            return pltpu.make_async_remote_copy(
                src_ref=slot if src is None else src, dst_ref=slot, send_sem=send_sems.at[a, k],
                recv_sem=recv_sems.at[a, k], device_id=to, device_id_type=pl.DeviceIdType.MESH)

        mine = [pltpu.make_async_copy(in_refs[a], out_refs[a].at[4 * x + 2 * y + c], local_sems.at[a]) for a in range(n)]
        first = []
        for a in range(n):
            first.append(copy(a, 0, me, sibling, src=in_refs[a]))
            first += [copy(a, 1 + j, me, (*chip, c), src=in_refs[a]) for j, chip in enumerate(chips[:2])]
        for cp in mine + first:
            cp.start()
        passed = []
        for j, chip in enumerate(chips[:2]):
            for a in range(n):
                copy(a, 1 + j, (*chip, c), me).wait_recv()
                passed.append(copy(a, 4 + j, (*chip, c), sibling))
                passed[-1].start()
        for a in range(n):
            passed.append(copy(a, 3, relayed, relay_to))
            passed[-1].start()
        for a in range(n):
            copy(a, 3, (*chips[2], c), me).wait_recv()
            passed.append(copy(a, 6, (*chips[2], c), sibling))
            passed[-1].start()
        for a in range(n):
            copy(a, 0, sibling, me).wait_recv()
            for j, chip in enumerate(chips):
                copy(a, 4 + j, (*chip, 1 - c), me).wait_recv()
        for cp in first + passed:
            cp.wait_send()
        for cp in mine:
            cp.wait()

    any_spec = pl.BlockSpec(memory_space=pl.ANY)
    return pl.pallas_call(
        body, name=name, in_specs=[any_spec] * n, out_specs=[any_spec] * n, out_shape=out_shape,
        scratch_shapes=[pltpu.SemaphoreType.DMA((n, N_DEV - 1)), pltpu.SemaphoreType.DMA((n, N_DEV - 1)),
                        pltpu.SemaphoreType.DMA((n,))],
        compiler_params=pltpu.CompilerParams(has_side_effects=True),
    )(*arrays)


def _push(arrays, *, name, plan, out_slots=None):
    n = len(arrays)
    in_place = out_slots is None
    out_shape = [jax.ShapeDtypeStruct(((a.shape[0] if in_place else out_slots),) + tuple(a.shape[1:]), a.dtype) for a in arrays]
    n_tr = len(plan(0, 0, 0)[0])

    def body(*refs):
        in_refs, out_refs = refs[:n], refs[n:2 * n]
        send_sems, recv_sems, local_sems = refs[2 * n:]
        src_refs = out_refs if in_place else in_refs
        transfers, local = plan(lax.axis_index("x"), lax.axis_index("y"), lax.axis_index("c"))
        copies = []
        for a in range(n):
            if local is not None:
                copies.append(pltpu.make_async_copy(src_refs[a].at[local[0]], out_refs[a].at[local[1]], local_sems.at[a]))
            for k, (peer, src, dst) in enumerate(transfers):
                copies.append(pltpu.make_async_remote_copy(
                    src_ref=src_refs[a].at[src], dst_ref=out_refs[a].at[dst], send_sem=send_sems.at[a, k],
                    recv_sem=recv_sems.at[a, k], device_id=peer, device_id_type=pl.DeviceIdType.MESH))
        for cp in copies:
            cp.start()
        for cp in copies:
            cp.wait()

    any_spec = pl.BlockSpec(memory_space=pl.ANY)
    return pl.pallas_call(
        body, name=name, in_specs=[any_spec] * n, out_specs=[any_spec] * n, out_shape=out_shape,
        input_output_aliases={a: a for a in range(n)} if in_place else {},
        scratch_shapes=[pltpu.SemaphoreType.DMA((n, n_tr)), pltpu.SemaphoreType.DMA((n, n_tr)),
                        pltpu.SemaphoreType.DMA((n,))],
        compiler_params=pltpu.CompilerParams(has_side_effects=True),
    )(*arrays)


_HBM_SPEC = pl.BlockSpec(memory_space=pltpu.HBM)
_SEM_SPEC = pl.BlockSpec(memory_space=pltpu.SEMAPHORE)
_DATAFLOW = pltpu.SideEffectType.DATAFLOW_SIDE_EFFECTING


def _push_start(sources, landing, *, name, plan, after=None):
    n = len(sources)
    n_tr = len(plan(0, 0, 0)[0])
    deps, dep_specs = _after(after)

    def body(*refs):
        src_refs, land_refs = refs[:n], refs[n:2 * n]
        o = 2 * n + len(deps)
        send_sems, recv_sems, token = refs[o], refs[o + 1], refs[-1]
        transfers, _ = plan(lax.axis_index("x"), lax.axis_index("y"), lax.axis_index("c"))
        for a in range(n):
            for k, (peer, src, dst) in enumerate(transfers):
                pltpu.make_async_remote_copy(
                    src_ref=src_refs[a].at[src], dst_ref=land_refs[a].at[dst], send_sem=send_sems.at[a * n_tr + k],
                    recv_sem=recv_sems.at[a * n_tr + k], device_id=peer, device_id_type=pl.DeviceIdType.MESH).start()
        token[...] = jnp.zeros(token.shape, token.dtype)

    hbm = lambda a: pltpu.HBM(a.shape, a.dtype)
    outs = pl.pallas_call(
        body, name=name,
        out_shape=(pltpu.SemaphoreType.DMA((n * n_tr,)), pltpu.SemaphoreType.DMA((n * n_tr,)), *[hbm(a) for a in sources],
                   *[hbm(a) for a in landing], jax.ShapeDtypeStruct((8, 128), F32)),
        in_specs=[_HBM_SPEC] * (2 * n) + dep_specs,
        out_specs=(_SEM_SPEC, _SEM_SPEC, *[_HBM_SPEC] * (2 * n), pl.BlockSpec(memory_space=pltpu.VMEM)),
        input_output_aliases={i: 2 + i for i in range(2 * n)},
        compiler_params=pltpu.CompilerParams(has_side_effects=_DATAFLOW),
    )(*[pltpu.with_memory_space_constraint(a, pltpu.HBM) for a in list(sources) + list(landing)], *deps)
    return outs[0], outs[1], list(outs[2:2 + n]), list(outs[2 + n:2 + 2 * n]), outs[-1]


def _push_wait(handles, after, *, name, plan):
    send_sems, recv_sems, sources, landing, _ = handles
    n = len(sources)
    after = list(after) if isinstance(after, (list, tuple)) else [after]

    def body(*refs):
        src_refs, land_refs = refs[:n], refs[n:2 * n]
        send_sems_, recv_sems_ = refs[2 * n], refs[2 * n + 1]
        transfers, _ = plan(lax.axis_index("x"), lax.axis_index("y"), lax.axis_index("c"))
        n_tr = len(transfers)
        for a in range(n):
            for k, (peer, src, dst) in enumerate(transfers):
                cp = pltpu.make_async_remote_copy(
                    src_ref=src_refs[a].at[src], dst_ref=land_refs[a].at[dst], send_sem=send_sems_.at[a * n_tr + k],
                    recv_sem=recv_sems_.at[a * n_tr + k], device_id=peer, device_id_type=pl.DeviceIdType.MESH)
                cp.wait_send()
                cp.wait_recv()

    hbm = lambda a: pltpu.HBM(a.shape, a.dtype)
    outs = pl.pallas_call(
        body, name=name, out_shape=tuple(hbm(a) for a in list(sources) + list(landing)),
        in_specs=[_HBM_SPEC] * (2 * n) + [_SEM_SPEC, _SEM_SPEC] + [pl.BlockSpec(memory_space=pl.ANY)] * len(after),
        out_specs=[_HBM_SPEC] * (2 * n), input_output_aliases={i: i for i in range(2 * n)},
        compiler_params=pltpu.CompilerParams(has_side_effects=_DATAFLOW),
    )(*sources, *landing, send_sems, recv_sems, *after)
    return list(outs[n:])


def _plan_everyone(x, y, c):
    me = 4 * x + 2 * y + c
    peers = [(1 - x if k & 4 else x, 1 - y if k & 2 else y, 1 - c if k & 1 else c) for k in range(1, N_DEV)]
    return [(p, 0, me) for p in peers], (0, me)


def _plan_owners(x, y, c):
    me = 4 * x + 2 * y + c
    peers = [(1 - x if k & 4 else x, 1 - y if k & 2 else y, 1 - c if k & 1 else c) for k in range(1, N_DEV)]
    return [((px, py, pc), 4 * px + 2 * py + pc, me) for px, py, pc in peers], None


def _plan_sibling(x, y, c):
    return [((x, y, 1 - c), 2 * chip + (1 - c), chip) for chip in range(N_CHIPS)], None


def _plan_chips(x, y, c):
    mine = 2 * x + y
    peers = [(1 - x, y), (x, 1 - y), (1 - x, 1 - y)]
    return [((px, py, c), 2 * px + py, mine) for px, py in peers], (mine, mine)


def _plan_own_block(x, y, c):
    me = 4 * x + 2 * y + c
    peers = [(x, y, 1 - c), (1 - x, y, c), (x, 1 - y, c), (1 - x, 1 - y, c)]
    return [(p, 0, me) for p in peers], None


def _plan_pass_on(x, y, c):
    slots = [4 * px + 2 * py + c for px, py in ((1 - x, y), (x, 1 - y), (1 - x, 1 - y))]
    return [((x, y, 1 - c), s, s) for s in slots], None


def _pair_sum(parts, received, *, name, tc):
    _, r, c = parts.shape
    core = lax.axis_index("c").astype(jnp.int32).reshape(1)

    def body(core_ref, p_ref, r_ref, o_ref, o2_ref):
        s = (p_ref[...].astype(F32) + r_ref[...].astype(F32)).astype(o_ref.dtype)
        o_ref[...] = s
        o2_ref[...] = s

    out = pl.BlockSpec((None, r, tc), lambda i, j, core_ref: (i, 0, j))
    return pl.pallas_call(
        body, name=name,
        grid_spec=pltpu.PrefetchScalarGridSpec(
            num_scalar_prefetch=1, grid=(N_CHIPS, c // tc),
            in_specs=[pl.BlockSpec((None, r, tc), lambda i, j, core_ref: (2 * i + core_ref[0], 0, j)),
                      pl.BlockSpec((None, r, tc), lambda i, j, core_ref: (i, 0, j))],
            out_specs=[out, out]),
        out_shape=[jax.ShapeDtypeStruct((N_CHIPS, r, c), BF16)] * 2,
        compiler_params=pltpu.CompilerParams(dimension_semantics=("parallel", "parallel"), vmem_limit_bytes=VMEM_LIMIT),
    )(core, parts, received)


def _sum_blocks(a, *, name, tc):
    nblk, r, c = a.shape

    def body(a_ref, o_ref):
        acc = a_ref[0].astype(F32)
        for i in range(1, nblk):
            acc = acc + a_ref[i].astype(F32)
        o_ref[...] = acc

    return pl.pallas_call(
        body, name=name, grid=(c // tc,),
        in_specs=[pl.BlockSpec((nblk, r, tc), lambda j: (0, 0, j))],
        out_specs=pl.BlockSpec((r, tc), lambda j: (0, j)),
        out_shape=jax.ShapeDtypeStruct((r, c), F32),
        compiler_params=pltpu.CompilerParams(dimension_semantics=("parallel",), vmem_limit_bytes=VMEM_LIMIT),
    )(a)


def _adamw(w, g, m, v, *, name, tr):
    r, c = w.shape

    def body(w_ref, g_ref, m_ref, v_ref, d_ref, mo_ref, vo_ref):
        gv = g_ref[...]
        mn = ADAM_B1 * m_ref[...] + (1.0 - ADAM_B1) * gv
        vn = ADAM_B2 * v_ref[...] + (1.0 - ADAM_B2) * jnp.square(gv)
        m_hat = mn / (1.0 - ADAM_B1 ** ADAM_STEP)
        v_hat = vn / (1.0 - ADAM_B2 ** ADAM_STEP)
        d_ref[...] = -ADAM_LR * (m_hat / (jnp.sqrt(v_hat) + ADAM_EPS) + ADAM_WD * w_ref[...])
        mo_ref[...] = mn
        vo_ref[...] = vn

    spec = pl.BlockSpec((tr, c), lambda i: (i, 0))
    return pl.pallas_call(
        body, name=name, grid=(r // tr,), in_specs=[spec] * 4, out_specs=[spec] * 3,
        out_shape=[jax.ShapeDtypeStruct((r, c), F32)] * 3,
        compiler_params=pltpu.CompilerParams(dimension_semantics=("parallel",), vmem_limit_bytes=VMEM_LIMIT),
    )(w, g, m, v)


def _in_proj_layout():
    z0, xbc0, dt0, gate0 = 8192, 12288, 18432, 18496
    hg = []
    for h in range(HG_HEADS):
        for part in range(4):
            hg.append(part * 2048 + h * HG_DK + np.arange(HG_DK))
    ssm = []
    for g in range(SSM_GROUPS):
        ssm.append(xbc0 + g * SSM_GW + np.arange(SSM_GW))
        ssm.append(xbc0 + SSM_DINNER + g * SSM_DSTATE + np.arange(SSM_DSTATE))
        ssm.append(xbc0 + SSM_DINNER + SSM_GROUPS * SSM_DSTATE + g * SSM_DSTATE + np.arange(SSM_DSTATE))
        ssm.append(np.concatenate([dt0 + g * SSM_HPG + np.arange(SSM_HPG), -np.ones(128 - SSM_HPG, np.int64)]))
        ssm.append(z0 + g * SSM_GW + np.arange(SSM_GW))
    gate = gate0 + np.arange(2 * D_MODEL)
    return np.concatenate(hg), np.concatenate(ssm), gate


def _conv_layout():
    idx = []
    for g in range(SSM_GROUPS):
        idx.append(np.concatenate([g * SSM_GW + np.arange(SSM_GW),
                                   SSM_DINNER + g * SSM_DSTATE + np.arange(SSM_DSTATE),
                                   SSM_DINNER + SSM_GROUPS * SSM_DSTATE + g * SSM_DSTATE + np.arange(SSM_DSTATE)]))
    return np.stack(idx)


def _up_layout():
    idx = []
    for g in range(FFN_G):
        idx.append(g * FFN_GW + np.arange(FFN_GW))
        idx.append(D_FF + g * FFN_GW + np.arange(FFN_GW))
    return np.concatenate(idx)


def _inverse(idx, n):
    inv = np.zeros(n, np.int64)
    pos = np.nonzero(idx >= 0)[0]
    inv[idx[pos]] = pos
    return inv


def _take_rows(a, idx, axis=0):
    idx = np.asarray(idx).reshape(-1)
    pieces, start = [], 0
    for i in range(1, len(idx) + 1):
        same_run = i < len(idx) and ((idx[i] == idx[i - 1] + 1 and idx[i - 1] >= 0) or (idx[i] < 0 and idx[i - 1] < 0))
        if same_run:
            continue
        n = i - start
        if idx[start] < 0:
            shape = list(a.shape)
            shape[axis] = n
            pieces.append(jnp.zeros(shape, a.dtype))
        else:
            pieces.append(lax.slice_in_dim(a, int(idx[start]), int(idx[start]) + n, axis=axis))
        start = i
    return pieces[0] if len(pieces) == 1 else jnp.concatenate(pieces, axis=axis)


def _copy_runs(sources, out_rows, runs, *, name, block, total_rows=None, into=None):
    d, dtype = sources[0].shape[1], sources[0].dtype
    outs = []
    base = 0 if into is None else into[1] // block
    extra, extra_specs = ([], []) if into is None else ([into[0]], [pl.BlockSpec(memory_space=pl.ANY)])
    for o, rows in enumerate(out_rows):
        mine = sorted({i for i, _, oo, _, _ in runs if oo == o})
        ns, nblk = len(mine), rows // block
        sel = np.zeros(nblk, np.int32)
        idx = np.full((ns, nblk), -1, np.int64)
        for i, s, oo, t, n in runs:
            if oo == o:
                assert s % block == 0 and t % block == 0 and n % block == 0
                for b in range(n // block):
                    sel[t // block + b] = mine.index(i)
                    idx[mine.index(i), t // block + b] = s // block + b
        assert (idx.max(axis=0) >= 0).all()
        for i in range(ns):
            first = idx[i, np.nonzero(idx[i] >= 0)[0][0]]
            for b in range(nblk):
                if idx[i, b] < 0:
                    idx[i, b] = idx[i, b - 1] if b > 0 else first

        def body(sel_ref, idx_ref, *refs, ns=ns):
            srcs, out = refs[:ns], refs[-1]
            which = sel_ref[pl.program_id(0)]
            val = srcs[ns - 1][...]
            for i in range(ns - 2, -1, -1):
                val = jnp.where(which == i, srcs[i][...], val)
            out[...] = val

        in_specs = [pl.BlockSpec((block, d), (lambda i_, n_: lambda b, sel_ref, idx_ref: (idx_ref[i_ * n_ + b], 0))(i, nblk))
                    for i in range(ns)]
        full_rows = into[0].shape[0] if into is not None else (total_rows or rows)
        outs.append(pl.pallas_call(
            body, name=f"{name}_{o}" if len(out_rows) > 1 else name,
            grid_spec=pltpu.PrefetchScalarGridSpec(
                num_scalar_prefetch=2, grid=(nblk,), in_specs=in_specs + extra_specs,
                out_specs=pl.BlockSpec((block, d), lambda b, sel_ref, idx_ref: (base + b, 0))),
            out_shape=jax.ShapeDtypeStruct((full_rows, d), dtype),
            input_output_aliases={} if into is None else {2 + ns: 0},
            compiler_params=pltpu.CompilerParams(dimension_semantics=("arbitrary",), vmem_limit_bytes=VMEM_LIMIT),
        )(jnp.asarray(sel), jnp.asarray(idx.reshape(-1), jnp.int32), *[sources[i] for i in mine], *extra))
    return outs


_Z0, _XBC0, _DT0, _GATE0 = 8192, 12288, 18432, 18496
_B0, _C0 = _XBC0 + SSM_DINNER, _XBC0 + SSM_DINNER + SSM_GROUPS * SSM_DSTATE


def _in_proj_runs():
    runs = [(part * 2048 + h * HG_DK, 0, h * HG_BLK + part * HG_DK, HG_DK) for h in range(HG_HEADS) for part in range(4)]
    for g in range(SSM_GROUPS):
        base = g * SSM_BLK
        runs += [(_XBC0 + g * SSM_GW, 1, base, SSM_GW), (_B0 + g * SSM_DSTATE, 1, base + SSM_GW, SSM_DSTATE),
                 (_C0 + g * SSM_DSTATE, 1, base + SSM_GW + SSM_DSTATE, SSM_DSTATE),
                 (_Z0 + g * SSM_GW, 1, base + SSM_XBC + 128, SSM_GW)]
    return runs + [(_GATE0, 2, 0, 2 * D_MODEL)]


def _in_proj_to_kernel(in_t):
    d = in_t.shape[1]
    dt = jnp.pad(in_t[_DT0:_GATE0].reshape(SSM_GROUPS, SSM_HPG, d), ((0, 0), (0, 128 - SSM_HPG), (0, 0)))
    runs = [(0, src, sec, dst, n) for src, sec, dst, n in _in_proj_runs() if sec < 2]
    runs += [(1, g * 128, 1, g * SSM_BLK + SSM_XBC, 128) for g in range(SSM_GROUPS)]
    hg, ssm = _copy_runs([in_t, dt.reshape(SSM_GROUPS * 128, d)], [_Z0, SSM_GROUPS * SSM_BLK], runs,
                         name="in_proj_to_kernel_layout", block=128)
    return hg, ssm, in_t[_GATE0:]


def _in_proj_from_kernel(hg, ssm, gate):
    d = hg.shape[1]
    dt = ssm.reshape(SSM_GROUPS, SSM_BLK, d)[:, SSM_XBC:SSM_XBC + SSM_HPG].reshape(SSM_HEADS, d)
    runs = [(sec, dst, 0, src, n) for src, sec, dst, n in _in_proj_runs() if sec < 2]
    main = _copy_runs([hg, ssm], [_DT0], runs, name="in_proj_to_global_layout", block=128, total_rows=IN_TOTAL)[0]
    tail = [(0, 0, 0, 0, SSM_HEADS), (1, 0, 0, SSM_HEADS, 2 * D_MODEL)]
    return _copy_runs([dt, gate], [IN_TOTAL - _DT0], tail, name="in_proj_to_global_layout_tail", block=SSM_HEADS,
                      into=(main, _DT0))[0]


def _up_to_kernel(up_t):
    runs = [(0, part * D_FF + g * FFN_GW, 0, (2 * g + part) * FFN_GW, FFN_GW) for g in range(FFN_G) for part in range(2)]
    return _copy_runs([up_t], [2 * D_FF], runs, name="up_to_kernel_layout", block=FFN_GW)[0]


def _up_from_kernel(up):
    runs = [(0, (2 * g + part) * FFN_GW, 0, part * D_FF + g * FFN_GW, FFN_GW) for g in range(FFN_G) for part in range(2)]
    return _copy_runs([up], [2 * D_FF], runs, name="up_to_global_layout", block=FFN_GW)[0]


_SMALL = (("mix_pre_norm", (1, 2048)), ("mix_post_norm", (1, 2048)), ("hg_lb_table", (2, 2048)), ("hg_out_norm", (1, 128)),
          ("ssm_conv_w", (4, 6144)), ("ssm_conv_b", (1, 6144)), ("ssm_dt_bias", (1, 64)), ("ssm_A_log", (1, 64)),
          ("ssm_D", (1, 64)), ("ssm_out_norm", (1, 4096)), ("ffn_pre_norm", (1, 2048)), ("ffn_post_norm", (1, 2048)),
          ("ffn_conv_w", (3, 5632)), ("ffn_conv_b", (1, 5632)), ("loss", (1, 1)))
_PACK_ROWS = 8 * (-(-sum(int(np.prod(s)) for _, s in _SMALL) // 1024))


def _pack(vals):
    flat = jnp.concatenate([vals[k].astype(F32).reshape(-1) for k, _ in _SMALL])
    return jnp.pad(flat, (0, _PACK_ROWS * 128 - flat.shape[0])).reshape(_PACK_ROWS, 128)


def _unpack(packed):
    flat, out, o = packed.reshape(-1), {}, 0
    for k, s in _SMALL:
        n = int(np.prod(s))
        out[k] = flat[o:o + n].reshape(s)
        o += n
    return out


def _local_step(x, target, w, p, late_weights=None, emit=lambda key, gw: None):
    t = x.shape[0]
    one = lambda a: a.reshape((1,) + a.shape)
    row = dict(rows=t, groups=1, consts=[], carries=[])

    (h1,), _ = _stage_fwd(_pre_step, name="pre_fwd", chunk=512, nc=1, xs=[(x, D_MODEL)], params=[one(p["mix_pre_norm"])],
                          ys=[(D_MODEL, BF16)], **row)
    proj_hg = _matmul_nt(h1, w["in_hg"], name="proj_hg")
    proj_ssm = _matmul_nt(h1, w["in_ssm"], name="proj_ssm")
    proj_gate = _matmul_nt(h1, w["in_gate"], name="proj_gate")

    hg = dict(rows=t, chunk=HG_CHUNK, nc=8, groups=HG_HEADS, xs=[(proj_hg, HG_BLK)], params=[p["hg_tab"], p["hg_nw"]],
              consts=_hg_consts(), carries=[(HG_DK, HG_DK)], gpb=8)
    (y_hg,), hg_saved = _stage_fwd(_hg_step, name="hg_fwd", ys=[(HG_DK, BF16)], **hg)

    ssd = dict(rows=t, chunk=SSM_CHUNK, nc=4, groups=SSM_GROUPS, xs=[(proj_ssm, SSM_BLK)],
               params=[p["conv_w"], p["conv_b"], p["dt_bias"], p["a_log"], p["d_skip"], p["ssm_nw"]],
               consts=_ssd_consts(), carries=[(4 * 128, SSM_DSTATE), (HALO, SSM_XBC)])
    (y_ssm,), ssd_saved = _stage_fwd(_ssd_step, name="ssd_fwd", ys=[(SSM_GW, BF16)], **ssd)

    if late_weights is not None:
        w = {**w, **late_weights([y_hg, y_ssm])}
    u_hg = _matmul_nn(y_hg, w["branch_hg"], name="branch_hg")
    u_ssm = _matmul_nn(y_ssm, w["branch_ssm"], name="branch_ssm")
    mix = dict(chunk=256, nc=1, xs=[(proj_gate, 2 * D_MODEL), (u_hg, D_MODEL), (u_ssm, D_MODEL)], params=[], **row)
    (mixed,), _ = _stage_fwd(_mix_step, name="mix_fwd", ys=[(D_MODEL, BF16)], **mix)
    v = _matmul_nn(mixed, w["out"], name="out_proj")
    post = dict(chunk=256, nc=1, xs=[(x, D_MODEL), (v, D_MODEL)],
                params=[one(p["mix_post_norm"]), one(p["ffn_pre_norm"])], **row)
    (x1, h2), _ = _stage_fwd(_post_step, name="post_fwd", ys=[(D_MODEL, F32), (D_MODEL, BF16)], **post)
    gu = _matmul_nt(h2, w["up"], name="ffn_up")
    ffn = dict(rows=t, chunk=256, nc=2, groups=FFN_G, xs=[(gu, 2 * FFN_GW)], params=[p["ffn_conv_w"], p["ffn_conv_b"]],
               consts=[], carries=[(HALO, FFN_GW)])
    (act,), ffn_saved = _stage_fwd(_ffn_step, name="ffn_fwd", ys=[(FFN_GW, BF16)], **ffn)
    d = _matmul_nn(act, w["down"], name="ffn_down")

    def head_step(carry, xv, civ, pv, cv):
        x1_, d_, tgt = xv

        def per_row_loss(a, b, nw):
            e = a + _rms(b, nw) - tgt
            return 0.5 * jnp.mean(e * e, axis=1, keepdims=True)

        lrow, vjp = jax.vjp(per_row_loss, x1_, d_, pv[0])
        dx1_, dd_, dnw = vjp(jnp.ones_like(lrow))
        loss = jnp.broadcast_to(jnp.sum(lrow, axis=0, keepdims=True), (1, 128))
        return [], [dx1_, dd_], [], [dnw, loss]

    (dy, dd), _, (g_ffn_post, loss) = _scan_call(
        head_step, name="loss_head", chunk=256, nc=1, xs=[(x1, D_MODEL), (d, D_MODEL), (target, D_MODEL)],
        params=[one(p["ffn_post_norm"])], ys=[(D_MODEL, F32), (D_MODEL, BF16)], accs=[(1, D_MODEL), (1, 128)], **row)

    gw = {}
    gw["down"] = _matmul_tn(act, dd, name="g_down")
    dact = _matmul_nt(dd, w["down"], name="d_act", dep=emit("down", gw))
    (dgu,), (g_fcw, g_fcb) = _stage_bwd(_ffn_step, name="ffn_bwd", saved=ffn_saved, dys=[(dact, FFN_GW)], dxs=[BF16], **ffn)
    gw["up"] = _matmul_tn(dgu, h2, name="g_up")
    dh2 = _matmul_nn(dgu, w["up"], name="d_h2", dep=emit("up", gw))
    (dx1, dv), (g_mix_post, g_ffn_pre) = _stage_bwd(_post_step, name="post_bwd", saved=[], dys=[(dy, D_MODEL), (dh2, D_MODEL)],
                                                    dxs=[F32, BF16], **post)
    gw["out"] = _matmul_tn(mixed, dv, name="g_out")
    dmixed = _matmul_nt(dv, w["out"], name="d_mixed")
    (dgate, du_hg, du_ssm), _ = _stage_bwd(_mix_step, name="mix_bwd", saved=[], dys=[(dmixed, D_MODEL)],
                                           dxs=[BF16, BF16, BF16], **mix)
    gw["in_gate"] = _matmul_tn(dgate, h1, name="g_in_gate")
    gw["branch_hg"] = _matmul_tn(y_hg, du_hg, name="g_branch_hg")
    gw["branch_ssm"] = _matmul_tn(y_ssm, du_ssm, name="g_branch_ssm")
    dy_hg = _matmul_nt(du_hg, w["branch_hg"], name="d_y_hg", dep=emit("branches", gw))
    dy_ssm = _matmul_nt(du_ssm, w["branch_ssm"], name="d_y_ssm")
    (dproj_ssm,), g_ssd = _stage_bwd(_ssd_step, name="ssd_bwd", saved=ssd_saved, dys=[(dy_ssm, SSM_GW)], dxs=[BF16], **ssd)
    gw["in_ssm"] = _matmul_tn(dproj_ssm, h1, name="g_in_ssm")
    (dproj_hg,), (g_tab, g_hg_nw) = _stage_bwd(_hg_step, name="hg_bwd", saved=hg_saved, dys=[(dy_hg, HG_DK)], dxs=[BF16], **hg)
    gw["in_hg"] = _matmul_tn(dproj_hg, h1, name="g_in_hg")
    dh_a = _matmul_nn(dproj_hg, w["in_hg"], name="d_h1_hg", dep=emit("in", gw))
    dh_b = _matmul_nn(dproj_ssm, w["in_ssm"], name="d_h1_ssm")
    dh_c = _matmul_nn(dgate, w["in_gate"], name="d_h1_gate")

    def pre_bwd_step(carry, xv, civ, pv, cv):
        x_, da, db, dc, dres = xv
        _, vjp = jax.vjp(_rms, x_, pv[0])
        dx_, dnw = vjp(da + db + dc)
        return [], [dx_ + dres], [], [dnw]

    (grad_x,), _, (g_mix_pre,) = _scan_call(
        pre_bwd_step, name="pre_bwd", chunk=256, nc=1,
        xs=[(x, D_MODEL), (dh_a, D_MODEL), (dh_b, D_MODEL), (dh_c, D_MODEL), (dx1, D_MODEL)],
        params=[one(p["mix_pre_norm"])], ys=[(D_MODEL, F32)], accs=[(1, D_MODEL)], **row)

    gp = dict(mix_pre_norm=g_mix_pre[0], mix_post_norm=g_mix_post[0], ffn_pre_norm=g_ffn_pre[0], ffn_post_norm=g_ffn_post[0],
              hg_tab=g_tab, hg_nw=g_hg_nw, conv_w=g_ssd[0], conv_b=g_ssd[1], dt_bias=g_ssd[2], a_log=g_ssd[3],
              d_skip=g_ssd[4], ssm_nw=g_ssd[5], ffn_conv_w=g_fcw, ffn_conv_b=g_fcb, loss=loss[0, :, :1])
    return grad_x, gw, gp


def _small_to_kernel_layout(s):
    conv_idx = _conv_layout()
    pad_heads = lambda a: jnp.pad(a.reshape(SSM_GROUPS, 1, SSM_HPG), ((0, 0), (0, 0), (0, 128 - SSM_HPG)))
    return dict(
        mix_pre_norm=s["mix_pre_norm"], mix_post_norm=s["mix_post_norm"], ffn_pre_norm=s["ffn_pre_norm"],
        ffn_post_norm=s["ffn_post_norm"],
        hg_tab=s["hg_lb_table"].reshape(2, HG_HEADS, HG_DK).transpose(1, 0, 2),
        hg_nw=jnp.broadcast_to(s["hg_out_norm"].reshape(1, 1, HG_DK), (HG_HEADS, 1, HG_DK)),
        conv_w=_take_rows(s["ssm_conv_w"], conv_idx, axis=1).reshape(SSM_CONV, SSM_GROUPS, SSM_XBC).transpose(1, 0, 2),
        conv_b=_take_rows(s["ssm_conv_b"], conv_idx, axis=1).reshape(SSM_GROUPS, 1, SSM_XBC),
        dt_bias=pad_heads(s["ssm_dt_bias"]), a_log=pad_heads(s["ssm_A_log"]),
        d_skip=jnp.repeat(s["ssm_D"].reshape(SSM_HEADS), SSM_HEADDIM).reshape(SSM_GROUPS, 1, SSM_GW),
        ssm_nw=s["ssm_out_norm"].reshape(SSM_GROUPS, 1, SSM_GW),
        ffn_conv_w=s["ffn_conv_w"].reshape(FFN_CONV, FFN_G, FFN_GW).transpose(1, 0, 2),
        ffn_conv_b=s["ffn_conv_b"].reshape(FFN_G, 1, FFN_GW),
    )


def _small_from_kernel_layout(g):
    conv_inv = _inverse(_conv_layout().reshape(-1), SSM_CONV_DIM)
    heads = lambda a: a[:, 0, :SSM_HPG].reshape(1, SSM_HEADS)
    return dict(
        mix_pre_norm=g["mix_pre_norm"], mix_post_norm=g["mix_post_norm"], ffn_pre_norm=g["ffn_pre_norm"],
        ffn_post_norm=g["ffn_post_norm"],
        hg_lb_table=g["hg_tab"].transpose(1, 0, 2).reshape(2, HG_HEADS * HG_DK),
        hg_out_norm=jnp.sum(g["hg_nw"], axis=0),
        ssm_conv_w=_take_rows(g["conv_w"].transpose(1, 0, 2).reshape(SSM_CONV, -1), conv_inv, axis=1),
        ssm_conv_b=_take_rows(g["conv_b"].reshape(1, -1), conv_inv, axis=1),
        ssm_dt_bias=heads(g["dt_bias"]), ssm_A_log=heads(g["a_log"]),
        ssm_D=jnp.sum(g["d_skip"].reshape(SSM_HEADS, SSM_HEADDIM), axis=1).reshape(1, SSM_HEADS),
        ssm_out_norm=g["ssm_nw"].reshape(1, SSM_DINNER),
        ffn_conv_w=g["ffn_conv_w"].transpose(1, 0, 2).reshape(FFN_CONV, D_FF),
        ffn_conv_b=g["ffn_conv_b"].reshape(1, D_FF),
        loss=g["loss"],
    )


def kernel(x, w_in, mix_pre_norm, mix_post_norm, hg_lb_table, hg_out_norm, ssm_conv_w, ssm_conv_b, ssm_dt_bias, ssm_A_log, ssm_D, ssm_out_norm, w_branch_hg, w_branch_ssm, w_out, ffn_pre_norm, ffn_post_norm, ffn_w_up, ffn_conv_w, ffn_conv_b, ffn_w_down, loss_target, m_w_in, m_mix_pre_norm, m_mix_post_norm, m_hg_lb_table, m_hg_out_norm, m_ssm_conv_w, m_ssm_conv_b, m_ssm_dt_bias, m_ssm_A_log, m_ssm_D, m_ssm_out_norm, m_w_branch_hg, m_w_branch_ssm, m_w_out, m_ffn_pre_norm, m_ffn_post_norm, m_ffn_w_up, m_ffn_conv_w, m_ffn_conv_b, m_ffn_w_down, v_w_in, v_mix_pre_norm, v_mix_post_norm, v_hg_lb_table, v_hg_out_norm, v_ssm_conv_w, v_ssm_conv_b, v_ssm_dt_bias, v_ssm_A_log, v_ssm_D, v_ssm_out_norm, v_w_branch_hg, v_w_branch_ssm, v_w_out, v_ffn_pre_norm, v_ffn_post_norm, v_ffn_w_up, v_ffn_conv_w, v_ffn_conv_b, v_ffn_w_down):
    names = ["w_in", "mix_pre_norm", "mix_post_norm", "hg_lb_table", "hg_out_norm", "ssm_conv_w", "ssm_conv_b", "ssm_dt_bias",
             "ssm_A_log", "ssm_D", "ssm_out_norm", "w_branch_hg", "w_branch_ssm", "w_out", "ffn_pre_norm", "ffn_post_norm",
             "ffn_w_up", "ffn_conv_w", "ffn_conv_b", "ffn_w_down"]
    ws = dict(zip(names, (w_in, mix_pre_norm, mix_post_norm, hg_lb_table, hg_out_norm, ssm_conv_w, ssm_conv_b, ssm_dt_bias,
                          ssm_A_log, ssm_D, ssm_out_norm, w_branch_hg, w_branch_ssm, w_out, ffn_pre_norm, ffn_post_norm,
                          ffn_w_up, ffn_conv_w, ffn_conv_b, ffn_w_down)))
    ms = dict(zip(names, (m_w_in, m_mix_pre_norm, m_mix_post_norm, m_hg_lb_table, m_hg_out_norm, m_ssm_conv_w, m_ssm_conv_b,
                          m_ssm_dt_bias, m_ssm_A_log, m_ssm_D, m_ssm_out_norm, m_w_branch_hg, m_w_branch_ssm, m_w_out,
                          m_ffn_pre_norm, m_ffn_post_norm, m_ffn_w_up, m_ffn_conv_w, m_ffn_conv_b, m_ffn_w_down)))
    vs = dict(zip(names, (v_w_in, v_mix_pre_norm, v_mix_post_norm, v_hg_lb_table, v_hg_out_norm, v_ssm_conv_w, v_ssm_conv_b,
                          v_ssm_dt_bias, v_ssm_A_log, v_ssm_D, v_ssm_out_norm, v_w_branch_hg, v_w_branch_ssm, v_w_out,
                          v_ffn_pre_norm, v_ffn_post_norm, v_ffn_w_up, v_ffn_conv_w, v_ffn_conv_b, v_ffn_w_down)))
    me = 4 * lax.axis_index("x") + 2 * lax.axis_index("y") + lax.axis_index("c")

    late_shards = [ffn_w_up[0].T.astype(BF16), w_branch_hg[0].astype(BF16), w_branch_ssm[0].astype(BF16),
                   w_out[0].astype(BF16), ffn_w_down[0].astype(BF16)]
    landing = [lax.dynamic_update_slice_in_dim(lax.empty((N_DEV,) + s.shape, s.dtype), s[None], me, axis=0)
               for s in late_shards]
    gathered = _all_gather([w_in[0].T.astype(BF16), ssm_conv_w[0], ffn_conv_w[0]], name="gather_in_proj")
    late = _push_start([s[None] for s in late_shards], landing, name="late_weights_start", plan=_plan_own_block,
                       after=gathered[1])
    in_hg, in_ssm, in_gate = _in_proj_to_kernel(gathered[0].reshape(IN_TOTAL, D_MODEL))
    w = dict(in_hg=in_hg, in_ssm=in_ssm, in_gate=in_gate)
    small = {k: ws[k] for k, _ in _SMALL[:-1]}
    small["mix_pre_norm"] = mix_pre_norm + late[4][0, 0]
    small["ssm_conv_w"] = gathered[1].transpose(1, 0, 2).reshape(SSM_CONV, SSM_CONV_DIM)
    small["ffn_conv_w"] = gathered[2].transpose(1, 0, 2).reshape(FFN_CONV, D_FF)
    small = {k: small[k].reshape(s) for k, s in _SMALL[:-1]}

    def late_weights(after):
        landed = _push_wait(late, after, name="late_weights_wait", plan=_plan_own_block)
        up_all, bhg, bssm, out, down = _push(landed, name="late_weights_pass_on", plan=_plan_pass_on)
        return dict(up=_up_to_kernel(up_all.reshape(2 * D_FF, D_MODEL)), branch_hg=bhg.reshape(D_MODEL, D_MODEL),
                    branch_ssm=bssm.reshape(SSM_DINNER, D_MODEL), out=out.reshape(D_MODEL, D_MODEL),
                    down=down.reshape(D_FF, D_MODEL))

    in_flight = []

    def launch_direct(key, named_parts):
        ks, parts = zip(*named_parts)
        landing = [lax.dynamic_update_slice_in_dim(lax.empty(p.shape, p.dtype), lax.dynamic_slice_in_dim(p, me, 1, axis=0),
                                                   me, axis=0) for p in parts]
        handles = _push_start(list(parts), landing, name="grads_to_owners_start_" + key, plan=_plan_owners)
        in_flight.append(("grads_to_owners_wait_" + key, ks, handles, _plan_owners))
        return handles[4]

    def launch_two_level(key, named_parts):
        ks, parts = zip(*named_parts)
        from_sibling = _push(list(parts), name="grads_to_sibling_" + key, out_slots=N_CHIPS, plan=_plan_sibling)
        sums = [_pair_sum(p, r, name="pair_sum_" + k, tc=256) for k, p, r in zip(ks, parts, from_sibling)]
        handles = _push_start([q for q, _ in sums], [z for _, z in sums], name="grads_to_chips_start_" + key, plan=_plan_chips)
        in_flight.append(("grads_to_chips_wait_" + key, ks, handles, _plan_chips))
        return handles[4]

    def emit(key, gw):
        blocks = lambda a: a.reshape(N_DEV, -1, D_MODEL)
        if key == "down":
            return launch_direct(key, [("ffn_w_down", blocks(gw["down"]))])
        if key == "up":
            return launch_direct(key, [("ffn_w_up", blocks(_up_from_kernel(gw["up"])))])
        if key == "branches":
            return launch_direct(key, [("w_branch_hg", blocks(gw["branch_hg"])), ("w_branch_ssm", blocks(gw["branch_ssm"])),
                                       ("w_out", blocks(gw["out"]))])
        return launch_two_level(key, [("w_in", blocks(_in_proj_from_kernel(gw["in_hg"], gw["in_ssm"], gw["in_gate"])))])

    grad_x, gw, gp = _local_step(x[0], loss_target[0], w, _small_to_kernel_layout(small), late_weights, emit)

    big_names = ["w_in", "ffn_w_up", "w_branch_hg", "w_branch_ssm", "w_out", "ffn_w_down"]
    grads = {}
    for wait_name, ks, handles, plan in in_flight:
        landed = _push_wait(handles, grad_x, name=wait_name, plan=plan)
        for k, r in zip(ks, landed):
            g = _sum_blocks(r, name="sum_" + k, tc=256)
            grads[k] = g.T if k in ("w_in", "ffn_w_up") else g
    small_all = _push([_pack(_small_from_kernel_layout(gp))[None]], name="small_to_everyone", out_slots=N_DEV,
                      plan=_plan_everyone)
    small_g = _unpack(_sum_blocks(small_all[0], name="sum_small", tc=128))
    loss = small_g.pop("loss").reshape(())
    for k, g in small_g.items():
        if k in ("ssm_conv_w", "ffn_conv_w"):
            n = g.shape[1] // N_DEV
            g = lax.dynamic_slice_in_dim(g, me * n, n, axis=1)
        grads[k] = g

    delta, new_m, new_v = {}, {}, {}
    for k in big_names:
        delta[k], new_m[k], new_v[k] = _adamw(ws[k][0], grads[k], ms[k][0], vs[k][0], name="adamw_" + k, tr=64)
    small_names = [k for k in names if k not in big_names]
    flat = lambda d: jnp.concatenate([d[k].astype(F32).reshape(-1) for k in small_names])
    n_small = sum(int(np.prod(ws[k].shape)) for k in small_names)
    rows = 8 * (-(-n_small // 1024))
    pack2 = lambda d: jnp.pad(flat(d), (0, rows * 128 - n_small)).reshape(rows, 128)
    v_packed = jnp.pad(flat(vs), (0, rows * 128 - n_small), constant_values=1.0).reshape(rows, 128)
    packed = _adamw(pack2(ws), pack2(grads), pack2(ms), v_packed, name="adamw_small", tr=rows)
    o = 0
    for k in small_names:
        n = int(np.prod(ws[k].shape))
        delta[k], new_m[k], new_v[k] = (a.reshape(-1)[o:o + n].reshape(ws[k].shape) for a in packed)
        o += n

    full = lambda d: [d[k].reshape(ws[k].shape) for k in names]
    return (loss, grad_x[None], *full(grads), *full(delta), *full(new_m), *full(new_v))
```

```python
import functools

import numpy as np
import jax
import jax.numpy as jnp
from jax import lax
from jax.experimental import pallas as pl
from jax.experimental.pallas import tpu as pltpu

F32, BF16 = jnp.float32, jnp.bfloat16

D_MODEL = 2048
EPS = 1e-6
HG_HEADS, HG_DK, HG_CHUNK = 16, 128, 64
HG_BLK = 4 * HG_DK
SSM_DINNER, SSM_HEADDIM, SSM_HEADS, SSM_GROUPS, SSM_DSTATE, SSM_CONV = 4096, 64, 64, 8, 128, 4
SSM_CHUNK = 128
SSM_GW = SSM_DINNER // SSM_GROUPS
SSM_HPG = SSM_HEADS // SSM_GROUPS
SSM_XBC = SSM_GW + 2 * SSM_DSTATE
SSM_BLK = SSM_XBC + 128 + SSM_GW
SSM_CONV_DIM = SSM_DINNER + 2 * SSM_GROUPS * SSM_DSTATE
D_FF, FFN_CONV = 5632, 3
FFN_GW = 512
FFN_G = D_FF // FFN_GW
IN_TOTAL = 22592
N_DEV = 8
HALO = 8
VMEM_LIMIT = 52 * 1024 * 1024
ADAM_LR, ADAM_B1, ADAM_B2, ADAM_EPS, ADAM_WD, ADAM_STEP = 0.001, 0.9, 0.999, 1e-08, 0.01, 10

_DIMS = {"nn": ((1,), (0,)), "nt": ((1,), (1,)), "tn": ((0,), (0,))}


def _mm_raw(a, b, mode):
    return lax.dot_general(a.astype(BF16), b.astype(BF16), (_DIMS[mode], ((), ())), preferred_element_type=F32)


@functools.partial(jax.custom_vjp, nondiff_argnums=(2,))
def _mm(a, b, mode):
    return _mm_raw(a, b, mode)


def _mm_fwd(a, b, mode):
    return _mm_raw(a, b, mode), (a, b)


def _mm_bwd(mode, res, dc):
    a, b = res
    if mode == "nn":
        return _mm_raw(dc, b, "nt"), _mm_raw(a, dc, "tn")
    if mode == "nt":
        return _mm_raw(dc, b, "nn"), _mm_raw(dc, a, "tn")
    return _mm_raw(b, dc, "nt"), _mm_raw(a, dc, "nn")


_mm.defvjp(_mm_fwd, _mm_bwd)


def _cmm_raw(m, x, mode):
    hi = x.astype(BF16)
    r1 = x - hi.astype(F32)
    mid = r1.astype(BF16)
    lo = (r1 - mid.astype(F32)).astype(BF16)
    dn = (_DIMS[mode], ((), ()))
    dot = lambda p: lax.dot_general(m, p, dn, preferred_element_type=F32)
    return dot(hi) + dot(mid) + dot(lo)


@jax.custom_vjp
def _cmm(m, x):
    return _cmm_raw(m, x, "nn")


def _cmm_fwd(m, x):
    return _cmm_raw(m, x, "nn"), m


def _cmm_bwd(m, dy):
    return jnp.zeros_like(m), _cmm_raw(m, dy, "tn")


_cmm.defvjp(_cmm_fwd, _cmm_bwd)


@functools.partial(jax.custom_vjp, nondiff_argnums=(1,))
def _sroll(x, s):
    return pltpu.roll(x, s, 0) if s else x


def _sroll_fwd(x, s):
    return _sroll(x, s), None


def _sroll_bwd(s, _, ct):
    return ((pltpu.roll(ct, ct.shape[0] - s, 0) if s else ct),)


_sroll.defvjp(_sroll_fwd, _sroll_bwd)


def _rms(x, w):
    return x * lax.rsqrt(jnp.mean(x * x, axis=-1, keepdims=True) + EPS) * w


def _softplus(x):
    return jnp.maximum(x, 0.0) + jnp.log(1.0 + jnp.exp(-jnp.abs(x)))


def _causal_conv(halo, x, w, b):
    k_taps = w.shape[0]
    xe = jnp.concatenate([halo, x], axis=0)
    out = b
    for k in range(k_taps):
        out = out + w[k:k + 1, :] * _sroll(xe, k_taps - 1 - k)[HALO:, :]
    return out


def _hg_consts():
    c = HG_CHUNK
    t = np.arange(c)
    blocks, pair = [], []
    for m in (32, 16, 8, 4, 2, 1):
        pos = t % (2 * m)
        late = pos >= m
        mid = t - pos + m
        j = t[None, :]
        mq = late[:, None] & (j >= mid[:, None]) & (j <= t[:, None])
        mk = (~late)[:, None] & (j > t[:, None]) & (j <= mid[:, None] - 1)
        blocks.append(mq | mk)
        parent = t // (2 * m)
        pair.append((parent[:, None] == parent[None, :]) & late[:, None] & (~late)[None, :])
    blocks.append(t[None, :] <= t[:, None])
    mall = jnp.asarray(np.concatenate(blocks, 0).astype(np.float32), BF16)
    pair = jnp.asarray(np.stack(pair, 0).astype(np.float32))
    eye = jnp.asarray(np.eye(c, dtype=np.float32))
    return [mall, pair, eye]


def _hg_step(carry, xs, params, consts):
    (st,) = carry
    blk = xs[0].astype(F32)
    tab, nw = params
    mall, pair, eye = consts
    c, dk = HG_CHUNK, HG_DK
    q_raw, f_raw, v, og = blk[:, :dk], blk[:, dk:2 * dk], blk[:, 2 * dk:3 * dk], blk[:, 3 * dk:]
    lb = jax.nn.sigmoid(tab[0:1, :] - tab[1:2, :])
    f = lb + (1.0 - lb) * jax.nn.sigmoid(f_raw)
    g = jnp.log(f)
    kk = 1.0 - f
    qh = jax.nn.silu(q_raw) * (HG_DK ** -0.5)
    yield
    sums = _cmm(mall, g)
    yield
    b = sums[6 * c:, :]
    fac = jnp.exp(sums[:6 * c, :])
    scores = eye * jnp.sum(qh * kk, axis=1, keepdims=True)
    b_last = jnp.sum(g, axis=0, keepdims=True)
    yield
    inter = _mm(qh * jnp.exp(b), st, "nt")
    st_new = st * jnp.exp(b_last) + _mm(v, kk * jnp.exp(b_last - b), "tn")
    yield
    for l in range(6):
        fl = fac[l * c:(l + 1) * c, :]
        scores = scores + pair[l] * _mm(qh * fl, kk * fl, "nt")
        if l % 2:
            yield
    o = _mm(scores, v, "nn") + inter
    yield
    y = _rms(o, nw) * jax.nn.silu(og)
    return [st_new], [y]


def _ssd_consts():
    t = np.arange(SSM_CHUNK)
    tril = (t[None, :] <= t[:, None]).astype(np.float32)
    return [jnp.asarray(tril, BF16), jnp.asarray(tril)]


def _ssd_step(carry, xs, params, consts):
    st, halo = carry
    blk = xs[0].astype(F32)
    conv_w, conv_b, dtb, alog, dskip, nw = params
    tril_b, tril = consts
    c = SSM_CHUNK
    raw, dtr, z = blk[:, :SSM_XBC], blk[:, SSM_XBC:SSM_XBC + 128], blk[:, SSM_XBC + 128:]
    act = jax.nn.silu(_causal_conv(halo, raw, conv_w, conv_b))
    xh, bm, cm = act[:, :SSM_GW], act[:, SSM_GW:SSM_GW + SSM_DSTATE], act[:, SSM_GW + SSM_DSTATE:]
    dt = _softplus(dtr + dtb)
    da = dt * (-jnp.exp(alog))
    acum = _cmm(tril_b, da)
    acum_t = acum.T
    a_last = jnp.sum(da, axis=0, keepdims=True)
    cb_causal = _mm(cm, bm, "nt") * tril
    lane = lax.broadcasted_iota(jnp.int32, (c, 128), 1)
    row = lax.broadcasted_iota(jnp.int32, (128, 128), 0)
    first = lane < SSM_HEADDIM
    ys, st_new = [], []
    for j in range(SSM_HPG // 2):
        xp = xh[:, 128 * j:128 * (j + 1)]
        sp = st[128 * j:128 * (j + 1), :]
        r0, r1 = 2 * j, 2 * j + 1
        col = lambda a, r: jnp.broadcast_to(a[:, r:r + 1], (c, 128))
        xdt = xp * jnp.where(first, col(dt, r0), col(dt, r1))
        yj = _mm(cm, sp, "nt") * jnp.exp(jnp.where(first, col(acum, r0), col(acum, r1)))
        for r, keep in ((r0, first), (r1, ~first)):
            dec = jnp.broadcast_to(acum[:, r:r + 1], (c, c)) - jnp.broadcast_to(acum_t[r:r + 1, :], (c, c))
            m = cb_causal * jnp.exp(jnp.minimum(dec, 0.0))
            yj = yj + _mm(m, jnp.where(keep, xdt, 0.0), "nn")
        al0, al1 = a_last[:, r0:r0 + 1], a_last[:, r1:r1 + 1]
        wts = jnp.exp(jnp.where(first, al0 - col(acum, r0), al1 - col(acum, r1)))
        st_new.append(jnp.where(row < SSM_HEADDIM, jnp.exp(al0), jnp.exp(al1)) * sp + _mm(xdt * wts, bm, "tn"))
        ys.append(yj)
    y = jnp.concatenate(ys, axis=1) + dskip * xh
    y = _rms(y * jax.nn.silu(z), nw)
    return [jnp.concatenate(st_new, axis=0), raw[c - HALO:, :]], [y]


def _ffn_step(carry, xs, params, consts):
    (halo,) = carry
    blk = xs[0].astype(F32)
    conv_w, conv_b = params
    gate, up = blk[:, :FFN_GW], blk[:, FFN_GW:]
    a = jax.nn.gelu(_causal_conv(halo, gate, conv_w, conv_b), approximate=True) * up
    return [gate[gate.shape[0] - HALO:, :]], [a]


def _pre_step(carry, xs, params, consts):
    return [], [_rms(xs[0], params[0])]


def _mix_step(carry, xs, params, consts):
    gates, uh, us = (a.astype(F32) for a in xs)
    return [], [jax.nn.sigmoid(gates[:, :D_MODEL]) * uh + jax.nn.sigmoid(gates[:, D_MODEL:]) * us]


def _post_step(carry, xs, params, consts):
    x, v = xs
    x1 = x + _rms(v, params[0])
    return [], [x1, _rms(x1, params[1])]


def _scan_call(step, *, name, rows, chunk, nc, groups, xs, cins=(), params=(), consts=(), carries=(), ys=(), couts=(),
               accs=(), reverse=False, gpb=1, multi=False):
    blk_rows = chunk * nc
    nb = rows // blk_rows
    n_chunks = rows // chunk
    assert nb * blk_rows == rows and groups % gpb == 0
    rb = (lambda i: nb - 1 - i) if reverse else (lambda i: i)
    n_x, n_ci, n_p, n_c = len(xs), len(cins), len(params), len(consts)
    n_y, n_co, n_a = len(ys), len(couts), len(accs)

    def chunk_spec(shape):
        zeros = (0,) * len(shape)
        return pl.BlockSpec((gpb, nc) + tuple(shape), lambda g, i: (g, rb(i)) + zeros)

    in_specs = [pl.BlockSpec((blk_rows, gpb * w), lambda g, i: (rb(i), g)) for _, w in xs]
    in_specs += [chunk_spec(a.shape[2:]) for a in cins]
    in_specs += [pl.BlockSpec((gpb,) + tuple(a.shape[1:]), lambda g, i: (g, 0, 0)) for a in params]
    in_specs += [pl.BlockSpec(a.shape, (lambda nd: lambda g, i: (0,) * nd)(a.ndim)) for a in consts]
    out_specs = [pl.BlockSpec((blk_rows, gpb * w), lambda g, i: (rb(i), g)) for w, _ in ys]
    out_specs += [chunk_spec(s) for s in couts]
    out_specs += [pl.BlockSpec((gpb, r, c), lambda g, i: (g, 0, 0)) for r, c in accs]
    out_shape = [jax.ShapeDtypeStruct((rows, groups * w), dt) for w, dt in ys]
    out_shape += [jax.ShapeDtypeStruct((groups, n_chunks) + tuple(s), F32) for s in couts]
    out_shape += [jax.ShapeDtypeStruct((groups, r, c), F32) for r, c in accs]
    x_widths = [w for _, w in xs]
    y_widths = [w for w, _ in ys]

    def body(*refs):
        x_refs = refs[:n_x]
        ci_refs = refs[n_x:n_x + n_ci]
        p_refs = refs[n_x + n_ci:n_x + n_ci + n_p]
        c_refs = refs[n_x + n_ci + n_p:n_x + n_ci + n_p + n_c]
        o = n_x + n_ci + n_p + n_c
        y_refs = refs[o:o + n_y]
        co_refs = refs[o + n_y:o + n_y + n_co]
        a_refs = refs[o + n_y + n_co:o + n_y + n_co + n_a]
        carry_refs = refs[o + n_y + n_co + n_a:]

        @pl.when(pl.program_id(1) == 0)
        def _():
            for s in carry_refs:
                s[...] = jnp.zeros(s.shape, F32)
            for a in a_refs:
                a[...] = jnp.zeros(a.shape, F32)

        cvals = [c[...] for c in c_refs]

        def one_chunk(i, _):
            c = (nc - 1 - i) if reverse else i
            r0 = c * chunk if isinstance(c, int) else pl.multiple_of(c * chunk, chunk)
            loaded = []
            for u in range(gpb):
                carry = [s[u] for s in carry_refs]
                xv = [x[pl.ds(r0, chunk), u * w:(u + 1) * w] for x, w in zip(x_refs, x_widths)]
                civ = [ci[u, c] for ci in ci_refs]
                loaded.append((carry, xv, civ, [p[u] for p in p_refs]))
            results = step(loaded, cvals) if multi else [step(*args, cvals) for args in loaded]
            for u, (new_carry, yv, cov, av) in enumerate(results):
                for s, val in zip(carry_refs, new_carry):
                    s[u] = val
                for y, w, val in zip(y_refs, y_widths, yv):
                    y[pl.ds(r0, chunk), u * w:(u + 1) * w] = val.astype(y.dtype)
                for co, val in zip(co_refs, cov):
                    co[u, c] = val
                for a, val in zip(a_refs, av):
                    a[u] += val
            return 0

        if nc == 1:
            one_chunk(0, 0)
        else:
            lax.fori_loop(0, nc, one_chunk, 0)

    outs = pl.pallas_call(
        body, name=name, grid=(groups // gpb, nb), in_specs=in_specs, out_specs=out_specs, out_shape=out_shape,
        scratch_shapes=[pltpu.VMEM((gpb,) + tuple(s), F32) for s in carries],
        compiler_params=pltpu.CompilerParams(dimension_semantics=("arbitrary", "arbitrary"),
                                             vmem_limit_bytes=VMEM_LIMIT),
    )(*[a for a, _ in xs], *cins, *params, *consts)
    return outs[:n_y], outs[n_y:n_y + n_co], outs[n_y + n_co:]


def _run_interleaved(step, arg_tuples):
    runs = [step(*args) for args in arg_tuples]
    if not hasattr(runs[0], "send"):
        return runs
    results, live = [None] * len(runs), list(range(len(runs)))
    while live:
        for u in list(live):
            try:
                next(runs[u])
            except StopIteration as done:
                results[u] = done.value
                live.remove(u)
    return results


def _stage_fwd(step, *, name, rows, chunk, nc, groups, xs, params, consts, carries, ys, gpb=1):
    def fstep(loaded, cv):
        outs = _run_interleaved(step, [(carry, xv, pv, cv) for carry, xv, _, pv in loaded])
        return [(new_carry, yv, carry, []) for (new_carry, yv), (carry, _, _, _) in zip(outs, loaded)]

    yv, saved, _ = _scan_call(fstep, name=name, rows=rows, chunk=chunk, nc=nc, groups=groups, xs=xs, params=params,
                              consts=consts, carries=carries, ys=ys, couts=carries, gpb=gpb, multi=True)
    return yv, saved


def _stage_bwd(step, *, name, rows, chunk, nc, groups, xs, saved, params, consts, carries, dys, dxs, gpb=1):
    n_x = len(xs)

    def bstep(loaded, cv):
        civs = [list(civ) for _, _, civ, _ in loaded]
        xvs = [list(xv_all[:n_x]) for _, xv_all, _, _ in loaded]
        pvs = [list(pv) for _, _, _, pv in loaded]
        cts = [(list(dcarry), [d.astype(F32) for d in xv_all[n_x:]]) for dcarry, xv_all, _, _ in loaded]

        def fwd(civs_, xvs_, pvs_):
            outs = _run_interleaved(step, [(c_, x_, p_, cv) for c_, x_, p_ in zip(civs_, xvs_, pvs_)])
            return [(list(new_carry), list(yv)) for new_carry, yv in outs]

        _, vjp = jax.vjp(fwd, civs, xvs, pvs)
        dcivs, dxvs, dpvs = vjp(cts)
        return [(dc, dx, [], dp) for dc, dx, dp in zip(dcivs, dxvs, dpvs)]

    dxv, _, dpv = _scan_call(bstep, name=name, rows=rows, chunk=chunk, nc=nc, groups=groups, xs=list(xs) + list(dys),
                             cins=saved, params=params, consts=consts, carries=carries,
                             ys=[(w, dt) for (_, w), dt in zip(xs, dxs)], accs=[a.shape[1:] for a in params],
                             reverse=True, gpb=gpb, multi=True)
    return dxv, dpv


def _mm_params(sem):
    return pltpu.CompilerParams(dimension_semantics=sem, vmem_limit_bytes=VMEM_LIMIT)


def _after(dep):
    return ([], []) if dep is None else ([dep], [pl.BlockSpec(memory_space=pl.ANY)])


def _matmul_nt(a, b, *, name, dep=None):
    m, k = a.shape
    n = b.shape[0]
    tm = min(1024, m)
    tn = 1024 if n % 1024 == 0 else 1408 if n % 1408 == 0 else 512
    deps, dep_specs = _after(dep)

    def body(a_ref, b_ref, *rest):
        rest[-1][...] = lax.dot_general(a_ref[...], b_ref[...], (_DIMS["nt"], ((), ())), preferred_element_type=F32)

    return pl.pallas_call(
        body, name=name, grid=(m // tm, n // tn),
        in_specs=[pl.BlockSpec((tm, k), lambda i, j: (i, 0)), pl.BlockSpec((tn, k), lambda i, j: (j, 0))] + dep_specs,
        out_specs=pl.BlockSpec((tm, tn), lambda i, j: (i, j)),
        out_shape=jax.ShapeDtypeStruct((m, n), F32),
        compiler_params=_mm_params(("parallel", "arbitrary")),
    )(a, b, *deps)


def _matmul_nn(a, b, *, name, dep=None):
    m, k = a.shape
    n = b.shape[1]
    deps, dep_specs = _after(dep)
    if k > 6144:
        tm, tk, steps = min(512, m), k // 4, 4

        def body_k(a_ref, b_ref, *rest):
            part = jnp.dot(a_ref[...], b_ref[...], preferred_element_type=F32)

            @pl.when(pl.program_id(1) == 0)
            def _():
                rest[-1][...] = part

            @pl.when(pl.program_id(1) != 0)
            def _():
                rest[-1][...] += part

        return pl.pallas_call(
            body_k, name=name, grid=(m // tm, steps),
            in_specs=[pl.BlockSpec((tm, tk), lambda i, j: (i, j)), pl.BlockSpec((tk, n), lambda i, j: (j, 0))] + dep_specs,
            out_specs=pl.BlockSpec((tm, n), lambda i, j: (i, 0)),
            out_shape=jax.ShapeDtypeStruct((m, n), F32),
            compiler_params=_mm_params(("parallel", "arbitrary")),
        )(a, b, *deps)
    tm, tn = (1024, 1024) if k <= 4096 else (1024, 512)
    tm = min(tm, m)

    def body(a_ref, b_ref, *rest):
        rest[-1][...] = jnp.dot(a_ref[...], b_ref[...], preferred_element_type=F32)

    return pl.pallas_call(
        body, name=name, grid=(m // tm, n // tn),
        in_specs=[pl.BlockSpec((tm, k), lambda i, j: (i, 0)), pl.BlockSpec((k, tn), lambda i, j: (0, j))] + dep_specs,
        out_specs=pl.BlockSpec((tm, tn), lambda i, j: (i, j)),
        out_shape=jax.ShapeDtypeStruct((m, n), F32),
        compiler_params=_mm_params(("parallel", "arbitrary")),
    )(a, b, *deps)


def _matmul_tn(x, y, *, name, tp=512, tq=512):
    t, p = x.shape
    q = y.shape[1]

    def body(x_ref, y_ref, o_ref):
        o_ref[...] = lax.dot_general(x_ref[...], y_ref[...], (_DIMS["tn"], ((), ())),
                                     preferred_element_type=F32).astype(o_ref.dtype)

    return pl.pallas_call(
        body, name=name, grid=(p // tp, q // tq),
        in_specs=[pl.BlockSpec((t, tp), lambda i, j: (0, i)), pl.BlockSpec((t, tq), lambda i, j: (0, j))],
        out_specs=pl.BlockSpec((tp, tq), lambda i, j: (i, j)),
        out_shape=jax.ShapeDtypeStruct((p, q), BF16),
        compiler_params=_mm_params(("parallel", "arbitrary")),
    )(x, y)


N_CHIPS = N_DEV // 2


def _all_gather(arrays, *, name):
    n = len(arrays)
    out_shape = [jax.ShapeDtypeStruct((N_DEV,) + tuple(a.shape), a.dtype) for a in arrays]

    def body(*refs):
        in_refs, out_refs = refs[:n], refs[n:2 * n]
        send_sems, recv_sems, local_sems = refs[2 * n:]
        x, y, c = lax.axis_index("x"), lax.axis_index("y"), lax.axis_index("c")
        me, sibling = (x, y, c), (x, y, 1 - c)
        chips = [(1 - x, y), (x, 1 - y), (1 - x, 1 - y)]
        south = c == 0
        relay_to = (jnp.where(south, x, 1 - x), jnp.where(south, 1 - y, y), c)
        relayed = (jnp.where(south, 1 - x, x), jnp.where(south, y, 1 - y), c)

        def copy(a, k, block, to, src=None):
            slot = out_refs[a].at[4 * block[0] + 2 * block[1] + block[2]]
            return pltpu.make_async_remote_copy(
                src_ref=slot if src is None else src, dst_ref=slot, send_sem=send_sems.at[a, k],
                recv_sem=recv_sems.at[a, k], device_id=to, device_id_type=pl.DeviceIdType.MESH)

        mine = [pltpu.make_async_copy(in_refs[a], out_refs[a].at[4 * x + 2 * y + c], local_sems.at[a]) for a in range(n)]
        first = []
        for a in range(n):
            first.append(copy(a, 0, me, sibling, src=in_refs[a]))
            first += [copy(a, 1 + j, me, (*chip, c), src=in_refs[a]) for j, chip in enumerate(chips[:2])]
        for cp in mine + first:
            cp.start()
        passed = []
        for j, chip in enumerate(chips[:2]):
            for a in range(n):
                copy(a, 1 + j, (*chip, c), me).wait_recv()
                passed.append(copy(a, 4 + j, (*chip, c), sibling))
                passed[-1].start()
        for a in range(n):
            passed.append(copy(a, 3, relayed, relay_to))
            passed[-1].start()
        for a in range(n):
            copy(a, 3, (*chips[2], c), me).wait_recv()
            passed.append(copy(a, 6, (*chips[2], c), sibling))
            passed[-1].start()
        for a in range(n):
            copy(a, 0, sibling, me).wait_recv()
            for j, chip in enumerate(chips):
                copy(a, 4 + j, (*chip, 1 - c), me).wait_recv()
        for cp in first + passed:
            cp.wait_send()
        for cp in mine:
            cp.wait()

    any_spec = pl.BlockSpec(memory_space=pl.ANY)
    return pl.pallas_call(
        body, name=name, in_specs=[any_spec] * n, out_specs=[any_spec] * n, out_shape=out_shape,
        scratch_shapes=[pltpu.SemaphoreType.DMA((n, N_DEV - 1)), pltpu.SemaphoreType.DMA((n, N_DEV - 1)),
                        pltpu.SemaphoreType.DMA((n,))],
        compiler_params=pltpu.CompilerParams(has_side_effects=True),
    )(*arrays)


def _push(arrays, *, name, plan, out_slots=None):
    n = len(arrays)
    in_place = out_slots is None
    out_shape = [jax.ShapeDtypeStruct(((a.shape[0] if in_place else out_slots),) + tuple(a.shape[1:]), a.dtype) for a in arrays]
    n_tr = len(plan(0, 0, 0)[0])

    def body(*refs):
        in_refs, out_refs = refs[:n], refs[n:2 * n]
        send_sems, recv_sems, local_sems = refs[2 * n:]
        src_refs = out_refs if in_place else in_refs
        transfers, local = plan(lax.axis_index("x"), lax.axis_index("y"), lax.axis_index("c"))
        copies = []
        for a in range(n):
            if local is not None:
                copies.append(pltpu.make_async_copy(src_refs[a].at[local[0]], out_refs[a].at[local[1]], local_sems.at[a]))
            for k, (peer, src, dst) in enumerate(transfers):
                copies.append(pltpu.make_async_remote_copy(
                    src_ref=src_refs[a].at[src], dst_ref=out_refs[a].at[dst], send_sem=send_sems.at[a, k],
                    recv_sem=recv_sems.at[a, k], device_id=peer, device_id_type=pl.DeviceIdType.MESH))
        for cp in copies:
            cp.start()
        for cp in copies:
            cp.wait()

    any_spec = pl.BlockSpec(memory_space=pl.ANY)
    return pl.pallas_call(
        body, name=name, in_specs=[any_spec] * n, out_specs=[any_spec] * n, out_shape=out_shape,
        input_output_aliases={a: a for a in range(n)} if in_place else {},
        scratch_shapes=[pltpu.SemaphoreType.DMA((n, n_tr)), pltpu.SemaphoreType.DMA((n, n_tr)),
                        pltpu.SemaphoreType.DMA((n,))],
        compiler_params=pltpu.CompilerParams(has_side_effects=True),
    )(*arrays)


_HBM_SPEC = pl.BlockSpec(memory_space=pltpu.HBM)
_SEM_SPEC = pl.BlockSpec(memory_space=pltpu.SEMAPHORE)
_DATAFLOW = pltpu.SideEffectType.DATAFLOW_SIDE_EFFECTING


def _push_start(sources, landing, *, name, plan, after=None):
    n = len(sources)
    n_tr = len(plan(0, 0, 0)[0])
    deps, dep_specs = _after(after)

    def body(*refs):
        src_refs, land_refs = refs[:n], refs[n:2 * n]
        o = 2 * n + len(deps)
        send_sems, recv_sems, token = refs[o], refs[o + 1], refs[-1]
        transfers, _ = plan(lax.axis_index("x"), lax.axis_index("y"), lax.axis_index("c"))
        for a in range(n):
            for k, (peer, src, dst) in enumerate(transfers):
                pltpu.make_async_remote_copy(
                    src_ref=src_refs[a].at[src], dst_ref=land_refs[a].at[dst], send_sem=send_sems.at[a * n_tr + k],
                    recv_sem=recv_sems.at[a * n_tr + k], device_id=peer, device_id_type=pl.DeviceIdType.MESH).start()
        token[...] = jnp.zeros(token.shape, token.dtype)

    hbm = lambda a: pltpu.HBM(a.shape, a.dtype)
    outs = pl.pallas_call(
        body, name=name,
        out_shape=(pltpu.SemaphoreType.DMA((n * n_tr,)), pltpu.SemaphoreType.DMA((n * n_tr,)), *[hbm(a) for a in sources],
                   *[hbm(a) for a in landing], jax.ShapeDtypeStruct((8, 128), F32)),
        in_specs=[_HBM_SPEC] * (2 * n) + dep_specs,
        out_specs=(_SEM_SPEC, _SEM_SPEC, *[_HBM_SPEC] * (2 * n), pl.BlockSpec(memory_space=pltpu.VMEM)),
        input_output_aliases={i: 2 + i for i in range(2 * n)},
        compiler_params=pltpu.CompilerParams(has_side_effects=_DATAFLOW),
    )(*[pltpu.with_memory_space_constraint(a, pltpu.HBM) for a in list(sources) + list(landing)], *deps)
    return outs[0], outs[1], list(outs[2:2 + n]), list(outs[2 + n:2 + 2 * n]), outs[-1]


def _push_wait(handles, after, *, name, plan):
    send_sems, recv_sems, sources, landing, _ = handles
    n = len(sources)
    after = list(after) if isinstance(after, (list, tuple)) else [after]

    def body(*refs):
        src_refs, land_refs = refs[:n], refs[n:2 * n]
        send_sems_, recv_sems_ = refs[2 * n], refs[2 * n + 1]
        transfers, _ = plan(lax.axis_index("x"), lax.axis_index("y"), lax.axis_index("c"))
        n_tr = len(transfers)
        for a in range(n):
            for k, (peer, src, dst) in enumerate(transfers):
                cp = pltpu.make_async_remote_copy(
                    src_ref=src_refs[a].at[src], dst_ref=land_refs[a].at[dst], send_sem=send_sems_.at[a * n_tr + k],
                    recv_sem=recv_sems_.at[a * n_tr + k], device_id=peer, device_id_type=pl.DeviceIdType.MESH)
                cp.wait_send()
                cp.wait_recv()

    hbm = lambda a: pltpu.HBM(a.shape, a.dtype)
    outs = pl.pallas_call(
        body, name=name, out_shape=tuple(hbm(a) for a in list(sources) + list(landing)),
        in_specs=[_HBM_SPEC] * (2 * n) + [_SEM_SPEC, _SEM_SPEC] + [pl.BlockSpec(memory_space=pl.ANY)] * len(after),
        out_specs=[_HBM_SPEC] * (2 * n), input_output_aliases={i: i for i in range(2 * n)},
        compiler_params=pltpu.CompilerParams(has_side_effects=_DATAFLOW),
    )(*sources, *landing, send_sems, recv_sems, *after)
    return list(outs[n:])


def _plan_everyone(x, y, c):
    me = 4 * x + 2 * y + c
    peers = [(1 - x if k & 4 else x, 1 - y if k & 2 else y, 1 - c if k & 1 else c) for k in range(1, N_DEV)]
    return [(p, 0, me) for p in peers], (0, me)


def _plan_owners(x, y, c):
    me = 4 * x + 2 * y + c
    peers = [(1 - x if k & 4 else x, 1 - y if k & 2 else y, 1 - c if k & 1 else c) for k in range(1, N_DEV)]
    return [((px, py, pc), 4 * px + 2 * py + pc, me) for px, py, pc in peers], None


def _plan_sibling(x, y, c):
    return [((x, y, 1 - c), 2 * chip + (1 - c), chip) for chip in range(N_CHIPS)], None


def _plan_chips(x, y, c):
    mine = 2 * x + y
    peers = [(1 - x, y), (x, 1 - y), (1 - x, 1 - y)]
    return [((px, py, c), 2 * px + py, mine) for px, py in peers], (mine, mine)


def _plan_own_block(x, y, c):
    me = 4 * x + 2 * y + c
    peers = [(x, y, 1 - c), (1 - x, y, c), (x, 1 - y, c), (1 - x, 1 - y, c)]
    return [(p, 0, me) for p in peers], None


def _plan_pass_on(x, y, c):
    slots = [4 * px + 2 * py + c for px, py in ((1 - x, y), (x, 1 - y), (1 - x, 1 - y))]
    return [((x, y, 1 - c), s, s) for s in slots], None


def _pair_sum(parts, received, *, name, tc):
    _, r, c = parts.shape
    core = lax.axis_index("c").astype(jnp.int32).reshape(1)

    def body(core_ref, p_ref, r_ref, o_ref, o2_ref):
        s = (p_ref[...].astype(F32) + r_ref[...].astype(F32)).astype(o_ref.dtype)
        o_ref[...] = s
        o2_ref[...] = s

    out = pl.BlockSpec((None, r, tc), lambda i, j, core_ref: (i, 0, j))
    return pl.pallas_call(
        body, name=name,
        grid_spec=pltpu.PrefetchScalarGridSpec(
            num_scalar_prefetch=1, grid=(N_CHIPS, c // tc),
            in_specs=[pl.BlockSpec((None, r, tc), lambda i, j, core_ref: (2 * i + core_ref[0], 0, j)),
                      pl.BlockSpec((None, r, tc), lambda i, j, core_ref: (i, 0, j))],
            out_specs=[out, out]),
        out_shape=[jax.ShapeDtypeStruct((N_CHIPS, r, c), BF16)] * 2,
        compiler_params=pltpu.CompilerParams(dimension_semantics=("parallel", "parallel"), vmem_limit_bytes=VMEM_LIMIT),
    )(core, parts, received)


def _sum_blocks(a, *, name, tc):
    nblk, r, c = a.shape

    def body(a_ref, o_ref):
        acc = a_ref[0].astype(F32)
        for i in range(1, nblk):
            acc = acc + a_ref[i].astype(F32)
        o_ref[...] = acc

    return pl.pallas_call(
        body, name=name, grid=(c // tc,),
        in_specs=[pl.BlockSpec((nblk, r, tc), lambda j: (0, 0, j))],
        out_specs=pl.BlockSpec((r, tc), lambda j: (0, j)),
        out_shape=jax.ShapeDtypeStruct((r, c), F32),
        compiler_params=pltpu.CompilerParams(dimension_semantics=("parallel",), vmem_limit_bytes=VMEM_LIMIT),
    )(a)


def _adamw(w, g, m, v, *, name, tr):
    r, c = w.shape

    def body(w_ref, g_ref, m_ref, v_ref, d_ref, mo_ref, vo_ref):
        gv = g_ref[...]
        mn = ADAM_B1 * m_ref[...] + (1.0 - ADAM_B1) * gv
        vn = ADAM_B2 * v_ref[...] + (1.0 - ADAM_B2) * jnp.square(gv)
        m_hat = mn / (1.0 - ADAM_B1 ** ADAM_STEP)
        v_hat = vn / (1.0 - ADAM_B2 ** ADAM_STEP)
        d_ref[...] = -ADAM_LR * (m_hat / (jnp.sqrt(v_hat) + ADAM_EPS) + ADAM_WD * w_ref[...])
        mo_ref[...] = mn
        vo_ref[...] = vn

    spec = pl.BlockSpec((tr, c), lambda i: (i, 0))
    return pl.pallas_call(
        body, name=name, grid=(r // tr,), in_specs=[spec] * 4, out_specs=[spec] * 3,
        out_shape=[jax.ShapeDtypeStruct((r, c), F32)] * 3,
        compiler_params=pltpu.CompilerParams(dimension_semantics=("parallel",), vmem_limit_bytes=VMEM_LIMIT),
    )(w, g, m, v)


def _conv_layout():
    idx = []
    for g in range(SSM_GROUPS):
        idx.append(np.concatenate([g * SSM_GW + np.arange(SSM_GW),
                                   SSM_DINNER + g * SSM_DSTATE + np.arange(SSM_DSTATE),
                                   SSM_DINNER + SSM_GROUPS * SSM_DSTATE + g * SSM_DSTATE + np.arange(SSM_DSTATE)]))
    return np.stack(idx)


def _inverse(idx, n):
    inv = np.zeros(n, np.int64)
    pos = np.nonzero(idx >= 0)[0]
    inv[idx[pos]] = pos
    return inv


def _take_rows(a, idx, axis=0):
    idx = np.asarray(idx).reshape(-1)
    pieces, start = [], 0
    for i in range(1, len(idx) + 1):
        same_run = i < len(idx) and ((idx[i] == idx[i - 1] + 1 and idx[i - 1] >= 0) or (idx[i] < 0 and idx[i - 1] < 0))
        if same_run:
            continue
        n = i - start
        if idx[start] < 0:
            shape = list(a.shape)
            shape[axis] = n
            pieces.append(jnp.zeros(shape, a.dtype))
        else:
            pieces.append(lax.slice_in_dim(a, int(idx[start]), int(idx[start]) + n, axis=axis))
        start = i
    return pieces[0] if len(pieces) == 1 else jnp.concatenate(pieces, axis=axis)


def _copy_runs(sources, out_rows, runs, *, name, block, total_rows=None, into=None):
    d, dtype = sources[0].shape[1], sources[0].dtype
    outs = []
    base = 0 if into is None else into[1] // block
    extra, extra_specs = ([], []) if into is None else ([into[0]], [pl.BlockSpec(memory_space=pl.ANY)])
    for o, rows in enumerate(out_rows):
        mine = sorted({i for i, _, oo, _, _ in runs if oo == o})
        ns, nblk = len(mine), rows // block
        sel = np.zeros(nblk, np.int32)
        idx = np.full((ns, nblk), -1, np.int64)
        for i, s, oo, t, n in runs:
            if oo == o:
                assert s % block == 0 and t % block == 0 and n % block == 0
                for b in range(n // block):
                    sel[t // block + b] = mine.index(i)
                    idx[mine.index(i), t // block + b] = s // block + b
        assert (idx.max(axis=0) >= 0).all()
        for i in range(ns):
            first = idx[i, np.nonzero(idx[i] >= 0)[0][0]]
            for b in range(nblk):
                if idx[i, b] < 0:
                    idx[i, b] = idx[i, b - 1] if b > 0 else first

        def body(sel_ref, idx_ref, *refs, ns=ns):
            srcs, out = refs[:ns], refs[-1]
            which = sel_ref[pl.program_id(0)]
            val = srcs[ns - 1][...]
            for i in range(ns - 2, -1, -1):
                val = jnp.where(which == i, srcs[i][...], val)
            out[...] = val

        in_specs = [pl.BlockSpec((block, d), (lambda i_, n_: lambda b, sel_ref, idx_ref: (idx_ref[i_ * n_ + b], 0))(i, nblk))
                    for i in range(ns)]
        full_rows = into[0].shape[0] if into is not None else (total_rows or rows)
        outs.append(pl.pallas_call(
            body, name=f"{name}_{o}" if len(out_rows) > 1 else name,
            grid_spec=pltpu.PrefetchScalarGridSpec(
                num_scalar_prefetch=2, grid=(nblk,), in_specs=in_specs + extra_specs,
                out_specs=pl.BlockSpec((block, d), lambda b, sel_ref, idx_ref: (base + b, 0))),
            out_shape=jax.ShapeDtypeStruct((full_rows, d), dtype),
            input_output_aliases={} if into is None else {2 + ns: 0},
            compiler_params=pltpu.CompilerParams(dimension_semantics=("arbitrary",), vmem_limit_bytes=VMEM_LIMIT),
        )(jnp.asarray(sel), jnp.asarray(idx.reshape(-1), jnp.int32), *[sources[i] for i in mine], *extra))
    return outs


_Z0, _XBC0, _DT0, _GATE0 = 8192, 12288, 18432, 18496
_B0, _C0 = _XBC0 + SSM_DINNER, _XBC0 + SSM_DINNER + SSM_GROUPS * SSM_DSTATE


def _in_proj_runs():
    runs = [(part * 2048 + h * HG_DK, 0, h * HG_BLK + part * HG_DK, HG_DK) for h in range(HG_HEADS) for part in range(4)]
    for g in range(SSM_GROUPS):
        base = g * SSM_BLK
        runs += [(_XBC0 + g * SSM_GW, 1, base, SSM_GW), (_B0 + g * SSM_DSTATE, 1, base + SSM_GW, SSM_DSTATE),
                 (_C0 + g * SSM_DSTATE, 1, base + SSM_GW + SSM_DSTATE, SSM_DSTATE),
                 (_Z0 + g * SSM_GW, 1, base + SSM_XBC + 128, SSM_GW)]
    return runs + [(_GATE0, 2, 0, 2 * D_MODEL)]


def _in_proj_to_kernel(in_t):
    d = in_t.shape[1]
    dt = jnp.pad(in_t[_DT0:_GATE0].reshape(SSM_GROUPS, SSM_HPG, d), ((0, 0), (0, 128 - SSM_HPG), (0, 0)))
    runs = [(0, src, sec, dst, n) for src, sec, dst, n in _in_proj_runs() if sec < 2]
    runs += [(1, g * 128, 1, g * SSM_BLK + SSM_XBC, 128) for g in range(SSM_GROUPS)]
    hg, ssm = _copy_runs([in_t, dt.reshape(SSM_GROUPS * 128, d)], [_Z0, SSM_GROUPS * SSM_BLK], runs,
                         name="in_proj_to_kernel_layout", block=128)
    return hg, ssm, in_t[_GATE0:]


def _in_proj_from_kernel(hg, ssm, gate):
    d = hg.shape[1]
    dt = ssm.reshape(SSM_GROUPS, SSM_BLK, d)[:, SSM_XBC:SSM_XBC + SSM_HPG].reshape(SSM_HEADS, d)
    runs = [(sec, dst, 0, src, n) for src, sec, dst, n in _in_proj_runs() if sec < 2]
    main = _copy_runs([hg, ssm], [_DT0], runs, name="in_proj_to_global_layout", block=128, total_rows=IN_TOTAL)[0]
    tail = [(0, 0, 0, 0, SSM_HEADS), (1, 0, 0, SSM_HEADS, 2 * D_MODEL)]
    return _copy_runs([dt, gate], [IN_TOTAL - _DT0], tail, name="in_proj_to_global_layout_tail", block=SSM_HEADS,
                      into=(main, _DT0))[0]


def _up_to_kernel(up_t):
    runs = [(0, part * D_FF + g * FFN_GW, 0, (2 * g + part) * FFN_GW, FFN_GW) for g in range(FFN_G) for part in range(2)]
    return _copy_runs([up_t], [2 * D_FF], runs, name="up_to_kernel_layout", block=FFN_GW)[0]


def _up_from_kernel(up):
    runs = [(0, (2 * g + part) * FFN_GW, 0, part * D_FF + g * FFN_GW, FFN_GW) for g in range(FFN_G) for part in range(2)]
    return _copy_runs([up], [2 * D_FF], runs, name="up_to_global_layout", block=FFN_GW)[0]


_SMALL = (("mix_pre_norm", (1, 2048)), ("mix_post_norm", (1, 2048)), ("hg_lb_table", (2, 2048)), ("hg_out_norm", (1, 128)),
          ("ssm_conv_w", (4, 6144)), ("ssm_conv_b", (1, 6144)), ("ssm_dt_bias", (1, 64)), ("ssm_A_log", (1, 64)),
          ("ssm_D", (1, 64)), ("ssm_out_norm", (1, 4096)), ("ffn_pre_norm", (1, 2048)), ("ffn_post_norm", (1, 2048)),
          ("ffn_conv_w", (3, 5632)), ("ffn_conv_b", (1, 5632)), ("loss", (1, 1)))
_PACK_ROWS = 8 * (-(-sum(int(np.prod(s)) for _, s in _SMALL) // 1024))


def _pack(vals):
    flat = jnp.concatenate([vals[k].astype(F32).reshape(-1) for k, _ in _SMALL])
    return jnp.pad(flat, (0, _PACK_ROWS * 128 - flat.shape[0])).reshape(_PACK_ROWS, 128)


def _unpack(packed):
    flat, out, o = packed.reshape(-1), {}, 0
    for k, s in _SMALL:
        n = int(np.prod(s))
        out[k] = flat[o:o + n].reshape(s)
        o += n
    return out


def _local_step(x, target, w, p, late_weights=None, emit=lambda key, gw: None):
    t = x.shape[0]
    one = lambda a: a.reshape((1,) + a.shape)
    row = dict(rows=t, groups=1, consts=[], carries=[])

    (h1,), _ = _stage_fwd(_pre_step, name="pre_fwd", chunk=512, nc=1, xs=[(x, D_MODEL)], params=[one(p["mix_pre_norm"])],
                          ys=[(D_MODEL, BF16)], **row)
    proj_hg = _matmul_nt(h1, w["in_hg"], name="proj_hg")
    proj_ssm = _matmul_nt(h1, w["in_ssm"], name="proj_ssm")
    proj_gate = _matmul_nt(h1, w["in_gate"], name="proj_gate")

    hg = dict(rows=t, chunk=HG_CHUNK, nc=8, groups=HG_HEADS, xs=[(proj_hg, HG_BLK)], params=[p["hg_tab"], p["hg_nw"]],
              consts=_hg_consts(), carries=[(HG_DK, HG_DK)], gpb=8)
    (y_hg,), hg_saved = _stage_fwd(_hg_step, name="hg_fwd", ys=[(HG_DK, BF16)], **hg)

    ssd = dict(rows=t, chunk=SSM_CHUNK, nc=4, groups=SSM_GROUPS, xs=[(proj_ssm, SSM_BLK)],
               params=[p["conv_w"], p["conv_b"], p["dt_bias"], p["a_log"], p["d_skip"], p["ssm_nw"]],
               consts=_ssd_consts(), carries=[(4 * 128, SSM_DSTATE), (HALO, SSM_XBC)])
    (y_ssm,), ssd_saved = _stage_fwd(_ssd_step, name="ssd_fwd", ys=[(SSM_GW, BF16)], **ssd)

    if late_weights is not None:
        w = {**w, **late_weights([y_hg, y_ssm])}
    u_hg = _matmul_nn(y_hg, w["branch_hg"], name="branch_hg")
    u_ssm = _matmul_nn(y_ssm, w["branch_ssm"], name="branch_ssm")
    mix = dict(chunk=256, nc=1, xs=[(proj_gate, 2 * D_MODEL), (u_hg, D_MODEL), (u_ssm, D_MODEL)], params=[], **row)
    (mixed,), _ = _stage_fwd(_mix_step, name="mix_fwd", ys=[(D_MODEL, BF16)], **mix)
    v = _matmul_nn(mixed, w["out"], name="out_proj")
    post = dict(chunk=256, nc=1, xs=[(x, D_MODEL), (v, D_MODEL)],
                params=[one(p["mix_post_norm"]), one(p["ffn_pre_norm"])], **row)
    (x1, h2), _ = _stage_fwd(_post_step, name="post_fwd", ys=[(D_MODEL, F32), (D_MODEL, BF16)], **post)
    gu = _matmul_nt(h2, w["up"], name="ffn_up")
    ffn = dict(rows=t, chunk=256, nc=2, groups=FFN_G, xs=[(gu, 2 * FFN_GW)], params=[p["ffn_conv_w"], p["ffn_conv_b"]],
               consts=[], carries=[(HALO, FFN_GW)])
    (act,), ffn_saved = _stage_fwd(_ffn_step, name="ffn_fwd", ys=[(FFN_GW, BF16)], **ffn)
    d = _matmul_nn(act, w["down"], name="ffn_down")

    def head_step(carry, xv, civ, pv, cv):
        x1_, d_, tgt = xv

        def per_row_loss(a, b, nw):
            e = a + _rms(b, nw) - tgt
            return 0.5 * jnp.mean(e * e, axis=1, keepdims=True)

        lrow, vjp = jax.vjp(per_row_loss, x1_, d_, pv[0])
        dx1_, dd_, dnw = vjp(jnp.ones_like(lrow))
        loss = jnp.broadcast_to(jnp.sum(lrow, axis=0, keepdims=True), (1, 128))
        return [], [dx1_, dd_], [], [dnw, loss]

    (dy, dd), _, (g_ffn_post, loss) = _scan_call(
        head_step, name="loss_head", chunk=256, nc=1, xs=[(x1, D_MODEL), (d, D_MODEL), (target, D_MODEL)],
        params=[one(p["ffn_post_norm"])], ys=[(D_MODEL, F32), (D_MODEL, BF16)], accs=[(1, D_MODEL), (1, 128)], **row)

    gw = {}
    gw["down"] = _matmul_tn(act, dd, name="g_down")
    dact = _matmul_nt(dd, w["down"], name="d_act", dep=emit("down", gw))
    (dgu,), (g_fcw, g_fcb) = _stage_bwd(_ffn_step, name="ffn_bwd", saved=ffn_saved, dys=[(dact, FFN_GW)], dxs=[BF16], **ffn)
    gw["up"] = _matmul_tn(dgu, h2, name="g_up")
    dh2 = _matmul_nn(dgu, w["up"], name="d_h2", dep=emit("up", gw))
    (dx1, dv), (g_mix_post, g_ffn_pre) = _stage_bwd(_post_step, name="post_bwd", saved=[], dys=[(dy, D_MODEL), (dh2, D_MODEL)],
                                                    dxs=[F32, BF16], **post)
    gw["out"] = _matmul_tn(mixed, dv, name="g_out")
    dmixed = _matmul_nt(dv, w["out"], name="d_mixed")
    (dgate, du_hg, du_ssm), _ = _stage_bwd(_mix_step, name="mix_bwd", saved=[], dys=[(dmixed, D_MODEL)],
                                           dxs=[BF16, BF16, BF16], **mix)
    gw["in_gate"] = _matmul_tn(dgate, h1, name="g_in_gate")
    gw["branch_hg"] = _matmul_tn(y_hg, du_hg, name="g_branch_hg")
    gw["branch_ssm"] = _matmul_tn(y_ssm, du_ssm, name="g_branch_ssm")
    dy_hg = _matmul_nt(du_hg, w["branch_hg"], name="d_y_hg", dep=emit("branches", gw))
    dy_ssm = _matmul_nt(du_ssm, w["branch_ssm"], name="d_y_ssm")
    (dproj_ssm,), g_ssd = _stage_bwd(_ssd_step, name="ssd_bwd", saved=ssd_saved, dys=[(dy_ssm, SSM_GW)], dxs=[BF16], **ssd)
    gw["in_ssm"] = _matmul_tn(dproj_ssm, h1, name="g_in_ssm")
    (dproj_hg,), (g_tab, g_hg_nw) = _stage_bwd(_hg_step, name="hg_bwd", saved=hg_saved, dys=[(dy_hg, HG_DK)], dxs=[BF16], **hg)
    gw["in_hg"] = _matmul_tn(dproj_hg, h1, name="g_in_hg")
    dh_a = _matmul_nn(dproj_hg, w["in_hg"], name="d_h1_hg", dep=emit("in", gw))
    dh_b = _matmul_nn(dproj_ssm, w["in_ssm"], name="d_h1_ssm")
    dh_c = _matmul_nn(dgate, w["in_gate"], name="d_h1_gate")

    def pre_bwd_step(carry, xv, civ, pv, cv):
        x_, da, db, dc, dres = xv
        _, vjp = jax.vjp(_rms, x_, pv[0])
        dx_, dnw = vjp(da + db + dc)
        return [], [dx_ + dres], [], [dnw]

    (grad_x,), _, (g_mix_pre,) = _scan_call(
        pre_bwd_step, name="pre_bwd", chunk=256, nc=1,
        xs=[(x, D_MODEL), (dh_a, D_MODEL), (dh_b, D_MODEL), (dh_c, D_MODEL), (dx1, D_MODEL)],
        params=[one(p["mix_pre_norm"])], ys=[(D_MODEL, F32)], accs=[(1, D_MODEL)], **row)

    gp = dict(mix_pre_norm=g_mix_pre[0], mix_post_norm=g_mix_post[0], ffn_pre_norm=g_ffn_pre[0], ffn_post_norm=g_ffn_post[0],
              hg_tab=g_tab, hg_nw=g_hg_nw, conv_w=g_ssd[0], conv_b=g_ssd[1], dt_bias=g_ssd[2], a_log=g_ssd[3],
              d_skip=g_ssd[4], ssm_nw=g_ssd[5], ffn_conv_w=g_fcw, ffn_conv_b=g_fcb, loss=loss[0, :, :1])
    return grad_x, gw, gp


def _small_to_kernel_layout(s):
    conv_idx = _conv_layout()
    pad_heads = lambda a: jnp.pad(a.reshape(SSM_GROUPS, 1, SSM_HPG), ((0, 0), (0, 0), (0, 128 - SSM_HPG)))
    return dict(
        mix_pre_norm=s["mix_pre_norm"], mix_post_norm=s["mix_post_norm"], ffn_pre_norm=s["ffn_pre_norm"],
        ffn_post_norm=s["ffn_post_norm"],
        hg_tab=s["hg_lb_table"].reshape(2, HG_HEADS, HG_DK).transpose(1, 0, 2),
        hg_nw=jnp.broadcast_to(s["hg_out_norm"].reshape(1, 1, HG_DK), (HG_HEADS, 1, HG_DK)),
        conv_w=_take_rows(s["ssm_conv_w"], conv_idx, axis=1).reshape(SSM_CONV, SSM_GROUPS, SSM_XBC).transpose(1, 0, 2),
        conv_b=_take_rows(s["ssm_conv_b"], conv_idx, axis=1).reshape(SSM_GROUPS, 1, SSM_XBC),
        dt_bias=pad_heads(s["ssm_dt_bias"]), a_log=pad_heads(s["ssm_A_log"]),
        d_skip=jnp.repeat(s["ssm_D"].reshape(SSM_HEADS), SSM_HEADDIM).reshape(SSM_GROUPS, 1, SSM_GW),
        ssm_nw=s["ssm_out_norm"].reshape(SSM_GROUPS, 1, SSM_GW),
        ffn_conv_w=s["ffn_conv_w"].reshape(FFN_CONV, FFN_G, FFN_GW).transpose(1, 0, 2),
        ffn_conv_b=s["ffn_conv_b"].reshape(FFN_G, 1, FFN_GW),
    )


def _small_from_kernel_layout(g):
    conv_inv = _inverse(_conv_layout().reshape(-1), SSM_CONV_DIM)
    heads = lambda a: a[:, 0, :SSM_HPG].reshape(1, SSM_HEADS)
    return dict(
        mix_pre_norm=g["mix_pre_norm"], mix_post_norm=g["mix_post_norm"], ffn_pre_norm=g["ffn_pre_norm"],
        ffn_post_norm=g["ffn_post_norm"],
        hg_lb_table=g["hg_tab"].transpose(1, 0, 2).reshape(2, HG_HEADS * HG_DK),
        hg_out_norm=jnp.sum(g["hg_nw"], axis=0),
        ssm_conv_w=_take_rows(g["conv_w"].transpose(1, 0, 2).reshape(SSM_CONV, -1), conv_inv, axis=1),
        ssm_conv_b=_take_rows(g["conv_b"].reshape(1, -1), conv_inv, axis=1),
        ssm_dt_bias=heads(g["dt_bias"]), ssm_A_log=heads(g["a_log"]),
        ssm_D=jnp.sum(g["d_skip"].reshape(SSM_HEADS, SSM_HEADDIM), axis=1).reshape(1, SSM_HEADS),
        ssm_out_norm=g["ssm_nw"].reshape(1, SSM_DINNER),
        ffn_conv_w=g["ffn_conv_w"].transpose(1, 0, 2).reshape(FFN_CONV, D_FF),
        ffn_conv_b=g["ffn_conv_b"].reshape(1, D_FF),
        loss=g["loss"],
    )


def kernel(x, w_in, mix_pre_norm, mix_post_norm, hg_lb_table, hg_out_norm, ssm_conv_w, ssm_conv_b, ssm_dt_bias, ssm_A_log, ssm_D, ssm_out_norm, w_branch_hg, w_branch_ssm, w_out, ffn_pre_norm, ffn_post_norm, ffn_w_up, ffn_conv_w, ffn_conv_b, ffn_w_down, loss_target, m_w_in, m_mix_pre_norm, m_mix_post_norm, m_hg_lb_table, m_hg_out_norm, m_ssm_conv_w, m_ssm_conv_b, m_ssm_dt_bias, m_ssm_A_log, m_ssm_D, m_ssm_out_norm, m_w_branch_hg, m_w_branch_ssm, m_w_out, m_ffn_pre_norm, m_ffn_post_norm, m_ffn_w_up, m_ffn_conv_w, m_ffn_conv_b, m_ffn_w_down, v_w_in, v_mix_pre_norm, v_mix_post_norm, v_hg_lb_table, v_hg_out_norm, v_ssm_conv_w, v_ssm_conv_b, v_ssm_dt_bias, v_ssm_A_log, v_ssm_D, v_ssm_out_norm, v_w_branch_hg, v_w_branch_ssm, v_w_out, v_ffn_pre_norm, v_ffn_post_norm, v_ffn_w_up, v_ffn_conv_w, v_ffn_conv_b, v_ffn_w_down):
    names = ["w_in", "mix_pre_norm", "mix_post_norm", "hg_lb_table", "hg_out_norm", "ssm_conv_w", "ssm_conv_b", "ssm_dt_bias",
             "ssm_A_log", "ssm_D", "ssm_out_norm", "w_branch_hg", "w_branch_ssm", "w_out", "ffn_pre_norm", "ffn_post_norm",
             "ffn_w_up", "ffn_conv_w", "ffn_conv_b", "ffn_w_down"]
    ws = dict(zip(names, (w_in, mix_pre_norm, mix_post_norm, hg_lb_table, hg_out_norm, ssm_conv_w, ssm_conv_b, ssm_dt_bias,
                          ssm_A_log, ssm_D, ssm_out_norm, w_branch_hg, w_branch_ssm, w_out, ffn_pre_norm, ffn_post_norm,
                          ffn_w_up, ffn_conv_w, ffn_conv_b, ffn_w_down)))
    ms = dict(zip(names, (m_w_in, m_mix_pre_norm, m_mix_post_norm, m_hg_lb_table, m_hg_out_norm, m_ssm_conv_w, m_ssm_conv_b,
                          m_ssm_dt_bias, m_ssm_A_log, m_ssm_D, m_ssm_out_norm, m_w_branch_hg, m_w_branch_ssm, m_w_out,
                          m_ffn_pre_norm, m_ffn_post_norm, m_ffn_w_up, m_ffn_conv_w, m_ffn_conv_b, m_ffn_w_down)))
    vs = dict(zip(names, (v_w_in, v_mix_pre_norm, v_mix_post_norm, v_hg_lb_table, v_hg_out_norm, v_ssm_conv_w, v_ssm_conv_b,
                          v_ssm_dt_bias, v_ssm_A_log, v_ssm_D, v_ssm_out_norm, v_w_branch_hg, v_w_branch_ssm, v_w_out,
                          v_ffn_pre_norm, v_ffn_post_norm, v_ffn_w_up, v_ffn_conv_w, v_ffn_conv_b, v_ffn_w_down)))
    me = 4 * lax.axis_index("x") + 2 * lax.axis_index("y") + lax.axis_index("c")

    late_shards = [ffn_w_up[0].T.astype(BF16), w_branch_hg[0].astype(BF16), w_branch_ssm[0].astype(BF16),
                   w_out[0].astype(BF16), ffn_w_down[0].astype(BF16)]
    landing = [lax.dynamic_update_slice_in_dim(lax.empty((N_DEV,) + s.shape, s.dtype), s[None], me, axis=0)
               for s in late_shards]
    gathered = _all_gather([w_in[0].T.astype(BF16), ssm_conv_w[0], ffn_conv_w[0]], name="gather_in_proj")
    late = _push_start([s[None] for s in late_shards], landing, name="late_weights_start", plan=_plan_own_block,
                       after=gathered[1])
    in_hg, in_ssm, in_gate = _in_proj_to_kernel(gathered[0].reshape(IN_TOTAL, D_MODEL))
    w = dict(in_hg=in_hg, in_ssm=in_ssm, in_gate=in_gate)
    small = {k: ws[k] for k, _ in _SMALL[:-1]}
    small["mix_pre_norm"] = mix_pre_norm + late[4][0, 0]
    small["ssm_conv_w"] = gathered[1].transpose(1, 0, 2).reshape(SSM_CONV, SSM_CONV_DIM)
    small["ffn_conv_w"] = gathered[2].transpose(1, 0, 2).reshape(FFN_CONV, D_FF)
    small = {k: small[k].reshape(s) for k, s in _SMALL[:-1]}

    def late_weights(after):
        landed = _push_wait(late, after, name="late_weights_wait", plan=_plan_own_block)
        up_all, bhg, bssm, out, down = _push(landed, name="late_weights_pass_on", plan=_plan_pass_on)
        return dict(up=_up_to_kernel(up_all.reshape(2 * D_FF, D_MODEL)), branch_hg=bhg.reshape(D_MODEL, D_MODEL),
                    branch_ssm=bssm.reshape(SSM_DINNER, D_MODEL), out=out.reshape(D_MODEL, D_MODEL),
                    down=down.reshape(D_FF, D_MODEL))

    in_flight = []

    def launch_direct(key, named_parts):
        ks, parts = zip(*named_parts)
        landing = [lax.dynamic_update_slice_in_dim(lax.empty(p.shape, p.dtype), lax.dynamic_slice_in_dim(p, me, 1, axis=0),
                                                   me, axis=0) for p in parts]
        handles = _push_start(list(parts), landing, name="grads_to_owners_start_" + key, plan=_plan_owners)
        in_flight.append(("grads_to_owners_wait_" + key, ks, handles, _plan_owners))
        return handles[4]

    def launch_two_level(key, named_parts):
        ks, parts = zip(*named_parts)
        from_sibling = _push(list(parts), name="grads_to_sibling_" + key, out_slots=N_CHIPS, plan=_plan_sibling)
        sums = [_pair_sum(p, r, name="pair_sum_" + k, tc=256) for k, p, r in zip(ks, parts, from_sibling)]
        handles = _push_start([q for q, _ in sums], [z for _, z in sums], name="grads_to_chips_start_" + key, plan=_plan_chips)
        in_flight.append(("grads_to_chips_wait_" + key, ks, handles, _plan_chips))
        return handles[4]

    def emit(key, gw):
        blocks = lambda a: a.reshape(N_DEV, -1, D_MODEL)
        if key == "down":
            return launch_direct(key, [("ffn_w_down", blocks(gw["down"]))])
        if key == "up":
            return launch_direct(key, [("ffn_w_up", blocks(_up_from_kernel(gw["up"])))])
        if key == "branches":
            return launch_direct(key, [("w_branch_hg", blocks(gw["branch_hg"])), ("w_branch_ssm", blocks(gw["branch_ssm"])),
                                       ("w_out", blocks(gw["out"]))])
        return launch_two_level(key, [("w_in", blocks(_in_proj_from_kernel(gw["in_hg"], gw["in_ssm"], gw["in_gate"])))])

    grad_x, gw, gp = _local_step(x[0], loss_target[0], w, _small_to_kernel_layout(small), late_weights, emit)

    big_names = ["w_in", "ffn_w_up", "w_branch_hg", "w_branch_ssm", "w_out", "ffn_w_down"]
    grads = {}
    for wait_name, ks, handles, plan in in_flight:
        landed = _push_wait(handles, grad_x, name=wait_name, plan=plan)
        for k, r in zip(ks, landed):
            g = _sum_blocks(r, name="sum_" + k, tc=256)
            grads[k] = g.T if k in ("w_in", "ffn_w_up") else g
    small_all = _push([_pack(_small_from_kernel_layout(gp))[None]], name="small_to_everyone", out_slots=N_DEV,
                      plan=_plan_everyone)
    small_g = _unpack(_sum_blocks(small_all[0], name="sum_small", tc=128))
    loss = small_g.pop("loss").reshape(())
    for k, g in small_g.items():
        if k in ("ssm_conv_w", "ffn_conv_w"):
            n = g.shape[1] // N_DEV
            g = lax.dynamic_slice_in_dim(g, me * n, n, axis=1)
        grads[k] = g

    delta, new_m, new_v = {}, {}, {}
    for k in big_names:
        delta[k], new_m[k], new_v[k] = _adamw(ws[k][0], grads[k], ms[k][0], vs[k][0], name="adamw_" + k, tr=64)
    small_names = [k for k in names if k not in big_names]
    flat = lambda d: jnp.concatenate([d[k].astype(F32).reshape(-1) for k in small_names])
    n_small = sum(int(np.prod(ws[k].shape)) for k in small_names)
    rows = 8 * (-(-n_small // 1024))
    pack2 = lambda d: jnp.pad(flat(d), (0, rows * 128 - n_small)).reshape(rows, 128)
    v_packed = jnp.pad(flat(vs), (0, rows * 128 - n_small), constant_values=1.0).reshape(rows, 128)
    packed = _adamw(pack2(ws), pack2(grads), pack2(ms), v_packed, name="adamw_small", tr=rows)
    o = 0
    for k in small_names:
        n = int(np.prod(ws[k].shape))
        delta[k], new_m[k], new_v[k] = (a.reshape(-1)[o:o + n].reshape(ws[k].shape) for a in packed)
        o += n

    full = lambda d: [d[k].reshape(ws[k].shape) for k in names]
    return (loss, grad_x[None], *full(grads), *full(delta), *full(new_m), *full(new_v))
```

```python
import functools

import numpy as np
import jax
import jax.numpy as jnp
from jax import lax
from jax.experimental import pallas as pl
from jax.experimental.pallas import tpu as pltpu

F32, BF16 = jnp.float32, jnp.bfloat16

D_MODEL = 2048
EPS = 1e-6
HG_HEADS, HG_DK, HG_CHUNK = 16, 128, 64
HG_BLK = 4 * HG_DK
SSM_DINNER, SSM_HEADDIM, SSM_HEADS, SSM_GROUPS, SSM_DSTATE, SSM_CONV = 4096, 64, 64, 8, 128, 4
SSM_CHUNK = 128
SSM_GW = SSM_DINNER // SSM_GROUPS
SSM_HPG = SSM_HEADS // SSM_GROUPS
SSM_XBC = SSM_GW + 2 * SSM_DSTATE
SSM_BLK = SSM_XBC + 128 + SSM_GW
SSM_CONV_DIM = SSM_DINNER + 2 * SSM_GROUPS * SSM_DSTATE
D_FF, FFN_CONV = 5632, 3
FFN_GW = 512
FFN_G = D_FF // FFN_GW
IN_TOTAL = 22592
N_DEV = 8
HALO = 8
VMEM_LIMIT = 52 * 1024 * 1024
ADAM_LR, ADAM_B1, ADAM_B2, ADAM_EPS, ADAM_WD, ADAM_STEP = 0.001, 0.9, 0.999, 1e-08, 0.01, 10

_DIMS = {"nn": ((1,), (0,)), "nt": ((1,), (1,)), "tn": ((0,), (0,))}


def _mm_raw(a, b, mode):
    return lax.dot_general(a.astype(BF16), b.astype(BF16), (_DIMS[mode], ((), ())), preferred_element_type=F32)


@functools.partial(jax.custom_vjp, nondiff_argnums=(2,))
def _mm(a, b, mode):
    return _mm_raw(a, b, mode)


def _mm_fwd(a, b, mode):
    return _mm_raw(a, b, mode), (a, b)


def _mm_bwd(mode, res, dc):
    a, b = res
    if mode == "nn":
        return _mm_raw(dc, b, "nt"), _mm_raw(a, dc, "tn")
    if mode == "nt":
        return _mm_raw(dc, b, "nn"), _mm_raw(dc, a, "tn")
    return _mm_raw(b, dc, "nt"), _mm_raw(a, dc, "nn")


_mm.defvjp(_mm_fwd, _mm_bwd)


def _cmm_raw(m, x, mode):
    hi = x.astype(BF16)
    r1 = x - hi.astype(F32)
    mid = r1.astype(BF16)
    lo = (r1 - mid.astype(F32)).astype(BF16)
    dn = (_DIMS[mode], ((), ()))
    dot = lambda p: lax.dot_general(m, p, dn, preferred_element_type=F32)
    return dot(hi) + dot(mid) + dot(lo)


@jax.custom_vjp
def _cmm(m, x):
    return _cmm_raw(m, x, "nn")


def _cmm_fwd(m, x):
    return _cmm_raw(m, x, "nn"), m


def _cmm_bwd(m, dy):
    return jnp.zeros_like(m), _cmm_raw(m, dy, "tn")


_cmm.defvjp(_cmm_fwd, _cmm_bwd)


@functools.partial(jax.custom_vjp, nondiff_argnums=(1,))
def _sroll(x, s):
    return pltpu.roll(x, s, 0) if s else x


def _sroll_fwd(x, s):
    return _sroll(x, s), None


def _sroll_bwd(s, _, ct):
    return ((pltpu.roll(ct, ct.shape[0] - s, 0) if s else ct),)


_sroll.defvjp(_sroll_fwd, _sroll_bwd)


def _rms(x, w):
    return x * lax.rsqrt(jnp.mean(x * x, axis=-1, keepdims=True) + EPS) * w


def _softplus(x):
    return jnp.maximum(x, 0.0) + jnp.log(1.0 + jnp.exp(-jnp.abs(x)))


def _causal_conv(halo, x, w, b):
    k_taps = w.shape[0]
    xe = jnp.concatenate([halo, x], axis=0)
    out = b
    for k in range(k_taps):
        out = out + w[k:k + 1, :] * _sroll(xe, k_taps - 1 - k)[HALO:, :]
    return out


def _hg_consts():
    c = HG_CHUNK
    t = np.arange(c)
    blocks, pair = [], []
    for m in (32, 16, 8, 4, 2, 1):
        pos = t % (2 * m)
        late = pos >= m
        mid = t - pos + m
        j = t[None, :]
        mq = late[:, None] & (j >= mid[:, None]) & (j <= t[:, None])
        mk = (~late)[:, None] & (j > t[:, None]) & (j <= mid[:, None] - 1)
        blocks.append(mq | mk)
        parent = t // (2 * m)
        pair.append((parent[:, None] == parent[None, :]) & late[:, None] & (~late)[None, :])
    blocks.append(t[None, :] <= t[:, None])
    mall = jnp.asarray(np.concatenate(blocks, 0).astype(np.float32), BF16)
    pair = jnp.asarray(np.stack(pair, 0).astype(np.float32))
    eye = jnp.asarray(np.eye(c, dtype=np.float32))
    return [mall, pair, eye]


def _hg_step(carry, xs, params, consts):
    (st,) = carry
    blk = xs[0].astype(F32)
    tab, nw = params
    mall, pair, eye = consts
    c, dk = HG_CHUNK, HG_DK
    q_raw, f_raw, v, og = blk[:, :dk], blk[:, dk:2 * dk], blk[:, 2 * dk:3 * dk], blk[:, 3 * dk:]
    lb = jax.nn.sigmoid(tab[0:1, :] - tab[1:2, :])
    f = lb + (1.0 - lb) * jax.nn.sigmoid(f_raw)
    g = jnp.log(f)
    kk = 1.0 - f
    qh = jax.nn.silu(q_raw) * (HG_DK ** -0.5)
    yield
    sums = _cmm(mall, g)
    yield
    b = sums[6 * c:, :]
    fac = jnp.exp(sums[:6 * c, :])
    scores = eye * jnp.sum(qh * kk, axis=1, keepdims=True)
    b_last = jnp.sum(g, axis=0, keepdims=True)
    yield
    inter = _mm(qh * jnp.exp(b), st, "nt")
    st_new = st * jnp.exp(b_last) + _mm(v, kk * jnp.exp(b_last - b), "tn")
    yield
    for l in range(6):
        fl = fac[l * c:(l + 1) * c, :]
        scores = scores + pair[l] * _mm(qh * fl, kk * fl, "nt")
        if l % 2:
            yield
    o = _mm(scores, v, "nn") + inter
    yield
    y = _rms(o, nw) * jax.nn.silu(og)
    return [st_new], [y]


def _ssd_consts():
    t = np.arange(SSM_CHUNK)
    tril = (t[None, :] <= t[:, None]).astype(np.float32)
    return [jnp.asarray(tril, BF16), jnp.asarray(tril)]


def _ssd_step(carry, xs, params, consts):
    st, halo = carry
    blk = xs[0].astype(F32)
    conv_w, conv_b, dtb, alog, dskip, nw = params
    tril_b, tril = consts
    c = SSM_CHUNK
    raw, dtr, z = blk[:, :SSM_XBC], blk[:, SSM_XBC:SSM_XBC + 128], blk[:, SSM_XBC + 128:]
    act = jax.nn.silu(_causal_conv(halo, raw, conv_w, conv_b))
    xh, bm, cm = act[:, :SSM_GW], act[:, SSM_GW:SSM_GW + SSM_DSTATE], act[:, SSM_GW + SSM_DSTATE:]
    dt = _softplus(dtr + dtb)
    da = dt * (-jnp.exp(alog))
    acum = _cmm(tril_b, da)
    acum_t = acum.T
    a_last = jnp.sum(da, axis=0, keepdims=True)
    cb_causal = _mm(cm, bm, "nt") * tril
    lane = lax.broadcasted_iota(jnp.int32, (c, 128), 1)
    row = lax.broadcasted_iota(jnp.int32, (128, 128), 0)
    first = lane < SSM_HEADDIM
    ys, st_new = [], []
    for j in range(SSM_HPG // 2):
        xp = xh[:, 128 * j:128 * (j + 1)]
        sp = st[128 * j:128 * (j + 1), :]
        r0, r1 = 2 * j, 2 * j + 1
        col = lambda a, r: jnp.broadcast_to(a[:, r:r + 1], (c, 128))
        xdt = xp * jnp.where(first, col(dt, r0), col(dt, r1))
        yj = _mm(cm, sp, "nt") * jnp.exp(jnp.where(first, col(acum, r0), col(acum, r1)))
        for r, keep in ((r0, first), (r1, ~first)):
            dec = jnp.broadcast_to(acum[:, r:r + 1], (c, c)) - jnp.broadcast_to(acum_t[r:r + 1, :], (c, c))
            m = cb_causal * jnp.exp(jnp.minimum(dec, 0.0))
            yj = yj + _mm(m, jnp.where(keep, xdt, 0.0), "nn")
        al0, al1 = a_last[:, r0:r0 + 1], a_last[:, r1:r1 + 1]
        wts = jnp.exp(jnp.where(first, al0 - col(acum, r0), al1 - col(acum, r1)))
        st_new.append(jnp.where(row < SSM_HEADDIM, jnp.exp(al0), jnp.exp(al1)) * sp + _mm(xdt * wts, bm, "tn"))
        ys.append(yj)
    y = jnp.concatenate(ys, axis=1) + dskip * xh
    y = _rms(y * jax.nn.silu(z), nw)
    return [jnp.concatenate(st_new, axis=0), raw[c - HALO:, :]], [y]


def _ffn_step(carry, xs, params, consts):
    (halo,) = carry
    blk = xs[0].astype(F32)
    conv_w, conv_b = params
    gate, up = blk[:, :FFN_GW], blk[:, FFN_GW:]
    a = jax.nn.gelu(_causal_conv(halo, gate, conv_w, conv_b), approximate=True) * up
    return [gate[gate.shape[0] - HALO:, :]], [a]


def _pre_step(carry, xs, params, consts):
    return [], [_rms(xs[0], params[0])]


def _mix_step(carry, xs, params, consts):
    gates, uh, us = (a.astype(F32) for a in xs)
    return [], [jax.nn.sigmoid(gates[:, :D_MODEL]) * uh + jax.nn.sigmoid(gates[:, D_MODEL:]) * us]


def _post_step(carry, xs, params, consts):
    x, v = xs
    x1 = x + _rms(v, params[0])
    return [], [x1, _rms(x1, params[1])]


def _scan_call(step, *, name, rows, chunk, nc, groups, xs, cins=(), params=(), consts=(), carries=(), ys=(), couts=(),
               accs=(), reverse=False, gpb=1, multi=False):
    blk_rows = chunk * nc
    nb = rows // blk_rows
    n_chunks = rows // chunk
    assert nb * blk_rows == rows and groups % gpb == 0
    rb = (lambda i: nb - 1 - i) if reverse else (lambda i: i)
    n_x, n_ci, n_p, n_c = len(xs), len(cins), len(params), len(consts)
    n_y, n_co, n_a = len(ys), len(couts), len(accs)

    def chunk_spec(shape):
        zeros = (0,) * len(shape)
        return pl.BlockSpec((gpb, nc) + tuple(shape), lambda g, i: (g, rb(i)) + zeros)

    in_specs = [pl.BlockSpec((blk_rows, gpb * w), lambda g, i: (rb(i), g)) for _, w in xs]
    in_specs += [chunk_spec(a.shape[2:]) for a in cins]
    in_specs += [pl.BlockSpec((gpb,) + tuple(a.shape[1:]), lambda g, i: (g, 0, 0)) for a in params]
    in_specs += [pl.BlockSpec(a.shape, (lambda nd: lambda g, i: (0,) * nd)(a.ndim)) for a in consts]
    out_specs = [pl.BlockSpec((blk_rows, gpb * w), lambda g, i: (rb(i), g)) for w, _ in ys]
    out_specs += [chunk_spec(s) for s in couts]
    out_specs += [pl.BlockSpec((gpb, r, c), lambda g, i: (g, 0, 0)) for r, c in accs]
    out_shape = [jax.ShapeDtypeStruct((rows, groups * w), dt) for w, dt in ys]
    out_shape += [jax.ShapeDtypeStruct((groups, n_chunks) + tuple(s), F32) for s in couts]
    out_shape += [jax.ShapeDtypeStruct((groups, r, c), F32) for r, c in accs]
    x_widths = [w for _, w in xs]
    y_widths = [w for w, _ in ys]

    def body(*refs):
        x_refs = refs[:n_x]
        ci_refs = refs[n_x:n_x + n_ci]
        p_refs = refs[n_x + n_ci:n_x + n_ci + n_p]
        c_refs = refs[n_x + n_ci + n_p:n_x + n_ci + n_p + n_c]
        o = n_x + n_ci + n_p + n_c
        y_refs = refs[o:o + n_y]
        co_refs = refs[o + n_y:o + n_y + n_co]
        a_refs = refs[o + n_y + n_co:o + n_y + n_co + n_a]
        carry_refs = refs[o + n_y + n_co + n_a:]

        @pl.when(pl.program_id(1) == 0)
        def _():
            for s in carry_refs:
                s[...] = jnp.zeros(s.shape, F32)
            for a in a_refs:
                a[...] = jnp.zeros(a.shape, F32)

        cvals = [c[...] for c in c_refs]

        def one_chunk(i, _):
            c = (nc - 1 - i) if reverse else i
            r0 = c * chunk if isinstance(c, int) else pl.multiple_of(c * chunk, chunk)
            loaded = []
            for u in range(gpb):
                carry = [s[u] for s in carry_refs]
                xv = [x[pl.ds(r0, chunk), u * w:(u + 1) * w] for x, w in zip(x_refs, x_widths)]
                civ = [ci[u, c] for ci in ci_refs]
                loaded.append((carry, xv, civ, [p[u] for p in p_refs]))
            results = step(loaded, cvals) if multi else [step(*args, cvals) for args in loaded]
            for u, (new_carry, yv, cov, av) in enumerate(results):
                for s, val in zip(carry_refs, new_carry):
                    s[u] = val
                for y, w, val in zip(y_refs, y_widths, yv):
                    y[pl.ds(r0, chunk), u * w:(u + 1) * w] = val.astype(y.dtype)
                for co, val in zip(co_refs, cov):
                    co[u, c] = val
                for a, val in zip(a_refs, av):
                    a[u] += val
            return 0

        if nc == 1:
            one_chunk(0, 0)
        else:
            lax.fori_loop(0, nc, one_chunk, 0)

    outs = pl.pallas_call(
        body, name=name, grid=(groups // gpb, nb), in_specs=in_specs, out_specs=out_specs, out_shape=out_shape,
        scratch_shapes=[pltpu.VMEM((gpb,) + tuple(s), F32) for s in carries],
        compiler_params=pltpu.CompilerParams(dimension_semantics=("arbitrary", "arbitrary"),
                                             vmem_limit_bytes=VMEM_LIMIT),
    )(*[a for a, _ in xs], *cins, *params, *consts)
    return outs[:n_y], outs[n_y:n_y + n_co], outs[n_y + n_co:]


def _run_interleaved(step, arg_tuples):
    runs = [step(*args) for args in arg_tuples]
    if not hasattr(runs[0], "send"):
        return runs
    results, live = [None] * len(runs), list(range(len(runs)))
    while live:
        for u in list(live):
            try:
                next(runs[u])
            except StopIteration as done:
                results[u] = done.value
                live.remove(u)
    return results


def _stage_fwd(step, *, name, rows, chunk, nc, groups, xs, params, consts, carries, ys, gpb=1):
    def fstep(loaded, cv):
        outs = _run_interleaved(step, [(carry, xv, pv, cv) for carry, xv, _, pv in loaded])
        return [(new_carry, yv, carry, []) for (new_carry, yv), (carry, _, _, _) in zip(outs, loaded)]

    yv, saved, _ = _scan_call(fstep, name=name, rows=rows, chunk=chunk, nc=nc, groups=groups, xs=xs, params=params,
                              consts=consts, carries=carries, ys=ys, couts=carries, gpb=gpb, multi=True)
    return yv, saved


def _stage_bwd(step, *, name, rows, chunk, nc, groups, xs, saved, params, consts, carries, dys, dxs, gpb=1):
    n_x = len(xs)

    def bstep(loaded, cv):
        civs = [list(civ) for _, _, civ, _ in loaded]
        xvs = [list(xv_all[:n_x]) for _, xv_all, _, _ in loaded]
        pvs = [list(pv) for _, _, _, pv in loaded]
        cts = [(list(dcarry), [d.astype(F32) for d in xv_all[n_x:]]) for dcarry, xv_all, _, _ in loaded]

        def fwd(civs_, xvs_, pvs_):
            outs = _run_interleaved(step, [(c_, x_, p_, cv) for c_, x_, p_ in zip(civs_, xvs_, pvs_)])
            return [(list(new_carry), list(yv)) for new_carry, yv in outs]

        _, vjp = jax.vjp(fwd, civs, xvs, pvs)
        dcivs, dxvs, dpvs = vjp(cts)
        return [(dc, dx, [], dp) for dc, dx, dp in zip(dcivs, dxvs, dpvs)]

    dxv, _, dpv = _scan_call(bstep, name=name, rows=rows, chunk=chunk, nc=nc, groups=groups, xs=list(xs) + list(dys),
                             cins=saved, params=params, consts=consts, carries=carries,
                             ys=[(w, dt) for (_, w), dt in zip(xs, dxs)], accs=[a.shape[1:] for a in params],
                             reverse=True, gpb=gpb, multi=True)
    return dxv, dpv


def _mm_params(sem):
    return pltpu.CompilerParams(dimension_semantics=sem, vmem_limit_bytes=VMEM_LIMIT)


def _after(dep):
    return ([], []) if dep is None else ([dep], [pl.BlockSpec(memory_space=pl.ANY)])


def _matmul_nt(a, b, *, name, dep=None):
    m, k = a.shape
    n = b.shape[0]
    tm = min(1024, m)
    tn = 1024 if n % 1024 == 0 else 1408 if n % 1408 == 0 else 512
    deps, dep_specs = _after(dep)

    def body(a_ref, b_ref, *rest):
        rest[-1][...] = lax.dot_general(a_ref[...], b_ref[...], (_DIMS["nt"], ((), ())), preferred_element_type=F32)

    return pl.pallas_call(
        body, name=name, grid=(m // tm, n // tn),
        in_specs=[pl.BlockSpec((tm, k), lambda i, j: (i, 0)), pl.BlockSpec((tn, k), lambda i, j: (j, 0))] + dep_specs,
        out_specs=pl.BlockSpec((tm, tn), lambda i, j: (i, j)),
        out_shape=jax.ShapeDtypeStruct((m, n), F32),
        compiler_params=_mm_params(("parallel", "arbitrary")),
    )(a, b, *deps)


def _matmul_nn(a, b, *, name, dep=None):
    m, k = a.shape
    n = b.shape[1]
    deps, dep_specs = _after(dep)
    if k > 6144:
        tm, tk, steps = min(512, m), k // 4, 4

        def body_k(a_ref, b_ref, *rest):
            part = jnp.dot(a_ref[...], b_ref[...], preferred_element_type=F32)

            @pl.when(pl.program_id(1) == 0)
            def _():
                rest[-1][...] = part

            @pl.when(pl.program_id(1) != 0)
            def _():
                rest[-1][...] += part

        return pl.pallas_call(
            body_k, name=name, grid=(m // tm, steps),
            in_specs=[pl.BlockSpec((tm, tk), lambda i, j: (i, j)), pl.BlockSpec((tk, n), lambda i, j: (j, 0))] + dep_specs,
            out_specs=pl.BlockSpec((tm, n), lambda i, j: (i, 0)),
            out_shape=jax.ShapeDtypeStruct((m, n), F32),
            compiler_params=_mm_params(("parallel", "arbitrary")),
        )(a, b, *deps)
    tm, tn = (1024, 1024) if k <= 4096 else (1024, 512)
    tm = min(tm, m)

    def body(a_ref, b_ref, *rest):
        rest[-1][...] = jnp.dot(a_ref[...], b_ref[...], preferred_element_type=F32)

    return pl.pallas_call(
        body, name=name, grid=(m // tm, n // tn),
        in_specs=[pl.BlockSpec((tm, k), lambda i, j: (i, 0)), pl.BlockSpec((k, tn), lambda i, j: (0, j))] + dep_specs,
        out_specs=pl.BlockSpec((tm, tn), lambda i, j: (i, j)),
        out_shape=jax.ShapeDtypeStruct((m, n), F32),
        compiler_params=_mm_params(("parallel", "arbitrary")),
    )(a, b, *deps)


def _matmul_tn(x, y, *, name, tp=512, tq=512):
    t, p = x.shape
    q = y.shape[1]

    def body(x_ref, y_ref, o_ref):
        o_ref[...] = lax.dot_general(x_ref[...], y_ref[...], (_DIMS["tn"], ((), ())),
                                     preferred_element_type=F32).astype(o_ref.dtype)

    return pl.pallas_call(
        body, name=name, grid=(p // tp, q // tq),
        in_specs=[pl.BlockSpec((t, tp), lambda i, j: (0, i)), pl.BlockSpec((t, tq), lambda i, j: (0, j))],
        out_specs=pl.BlockSpec((tp, tq), lambda i, j: (i, j)),
        out_shape=jax.ShapeDtypeStruct((p, q), BF16),
        compiler_params=_mm_params(("parallel", "arbitrary")),
    )(x, y)


N_CHIPS = N_DEV // 2


def _all_gather(arrays, *, name):
    n = len(arrays)
    out_shape = [jax.ShapeDtypeStruct((N_DEV,) + tuple(a.shape), a.dtype) for a in arrays]

    def body(*refs):
        in_refs, out_refs = refs[:n], refs[n:2 * n]
        send_sems, recv_sems, local_sems = refs[2 * n:]
        x, y, c = lax.axis_index("x"), lax.axis_index("y"), lax.axis_index("c")
        me, sibling = (x, y, c), (x, y, 1 - c)
        chips = [(1 - x, y), (x, 1 - y), (1 - x, 1 - y)]
        south = c == 0
        relay_to = (jnp.where(south, x, 1 - x), jnp.where(south, 1 - y, y), c)
        relayed = (jnp.where(south, 1 - x, x), jnp.where(south, y, 1 - y), c)

        def copy(a, k, block, to, src=None):
            slot = out_refs[a].at[4 * block[0] + 2 * block[1] + block[2]]
            return pltpu.make_async_remote_copy(
                src_ref=slot if src is None else src, dst_ref=slot, send_sem=send_sems.at[a, k],
                recv_sem=recv_sems.at[a, k], device_id=to, device_id_type=pl.DeviceIdType.MESH)

        mine = [pltpu.make_async_copy(in_refs[a], out_refs[a].at[4 * x + 2 * y + c], local_sems.at[a]) for a in range(n)]
        first = []
        for a in range(n):
            first.append(copy(a, 0, me, sibling, src=in_refs[a]))
            first += [copy(a, 1 + j, me, (*chip, c), src=in_refs[a]) for j, chip in enumerate(chips[:2])]
        for cp in mine + first:
            cp.start()
        passed = []
        for j, chip in enumerate(chips[:2]):
            for a in range(n):
                copy(a, 1 + j, (*chip, c), me).wait_recv()
                passed.append(copy(a, 4 + j, (*chip, c), sibling))
                passed[-1].start()
        for a in range(n):
            passed.append(copy(a, 3, relayed, relay_to))
            passed[-1].start()
        for a in range(n):
            copy(a, 3, (*chips[2], c), me).wait_recv()
            passed.append(copy(a, 6, (*chips[2], c), sibling))
            passed[-1].start()
        for a in range(n):
            copy(a, 0, sibling, me).wait_recv()
            for j, chip in enumerate(chips):
                copy(a, 4 + j, (*chip, 1 - c), me).wait_recv()
        for cp in first + passed:
            cp.wait_send()
        for cp in mine:
            cp.wait()

    any_spec = pl.BlockSpec(memory_space=pl.ANY)
    return pl.pallas_call(
        body, name=name, in_specs=[any_spec] * n, out_specs=[any_spec] * n, out_shape=out_shape,
        scratch_shapes=[pltpu.SemaphoreType.DMA((n, N_DEV - 1)), pltpu.SemaphoreType.DMA((n, N_DEV - 1)),
                        pltpu.SemaphoreType.DMA((n,))],
        compiler_params=pltpu.CompilerParams(has_side_effects=True),
    )(*arrays)


def _push(arrays, *, name, plan, out_slots=None):
    n = len(arrays)
    in_place = out_slots is None
    out_shape = [jax.ShapeDtypeStruct(((a.shape[0] if in_place else out_slots),) + tuple(a.shape[1:]), a.dtype) for a in arrays]
    n_tr = len(plan(0, 0, 0)[0])

    def body(*refs):
        in_refs, out_refs = refs[:n], refs[n:2 * n]
        send_sems, recv_sems, local_sems = refs[2 * n:]
        src_refs = out_refs if in_place else in_refs
        transfers, local = plan(lax.axis_index("x"), lax.axis_index("y"), lax.axis_index("c"))
        copies = []
        for a in range(n):
            if local is not None:
                copies.append(pltpu.make_async_copy(src_refs[a].at[local[0]], out_refs[a].at[local[1]], local_sems.at[a]))
            for k, (peer, src, dst) in enumerate(transfers):
                copies.append(pltpu.make_async_remote_copy(
                    src_ref=src_refs[a].at[src], dst_ref=out_refs[a].at[dst], send_sem=send_sems.at[a, k],
                    recv_sem=recv_sems.at[a, k], device_id=peer, device_id_type=pl.DeviceIdType.MESH))
        for cp in copies:
            cp.start()
        for cp in copies:
            cp.wait()

    any_spec = pl.BlockSpec(memory_space=pl.ANY)
    return pl.pallas_call(
        body, name=name, in_specs=[any_spec] * n, out_specs=[any_spec] * n, out_shape=out_shape,
        input_output_aliases={a: a for a in range(n)} if in_place else {},
        scratch_shapes=[pltpu.SemaphoreType.DMA((n, n_tr)), pltpu.SemaphoreType.DMA((n, n_tr)),
                        pltpu.SemaphoreType.DMA((n,))],
        compiler_params=pltpu.CompilerParams(has_side_effects=True),
    )(*arrays)


_HBM_SPEC = pl.BlockSpec(memory_space=pltpu.HBM)
_SEM_SPEC = pl.BlockSpec(memory_space=pltpu.SEMAPHORE)
_DATAFLOW = pltpu.SideEffectType.DATAFLOW_SIDE_EFFECTING


def _push_start(sources, landing, *, name, plan, after=None):
    n = len(sources)
    n_tr = len(plan(0, 0, 0)[0])
    deps, dep_specs = _after(after)

    def body(*refs):
        src_refs, land_refs = refs[:n], refs[n:2 * n]
        o = 2 * n + len(deps)
        send_sems, recv_sems, token = refs[o], refs[o + 1], refs[-1]
        transfers, _ = plan(lax.axis_index("x"), lax.axis_index("y"), lax.axis_index("c"))
        for a in range(n):
            for k, (peer, src, dst) in enumerate(transfers):
                pltpu.make_async_remote_copy(
                    src_ref=src_refs[a].at[src], dst_ref=land_refs[a].at[dst], send_sem=send_sems.at[a * n_tr + k],
                    recv_sem=recv_sems.at[a * n_tr + k], device_id=peer, device_id_type=pl.DeviceIdType.MESH).start()
        token[...] = jnp.zeros(token.shape, token.dtype)

    hbm = lambda a: pltpu.HBM(a.shape, a.dtype)
    outs = pl.pallas_call(
        body, name=name,
        out_shape=(pltpu.SemaphoreType.DMA((n * n_tr,)), pltpu.SemaphoreType.DMA((n * n_tr,)), *[hbm(a) for a in sources],
                   *[hbm(a) for a in landing], jax.ShapeDtypeStruct((8, 128), F32)),
        in_specs=[_HBM_SPEC] * (2 * n) + dep_specs,
        out_specs=(_SEM_SPEC, _SEM_SPEC, *[_HBM_SPEC] * (2 * n), pl.BlockSpec(memory_space=pltpu.VMEM)),
        input_output_aliases={i: 2 + i for i in range(2 * n)},
        compiler_params=pltpu.CompilerParams(has_side_effects=_DATAFLOW),
    )(*[pltpu.with_memory_space_constraint(a, pltpu.HBM) for a in list(sources) + list(landing)], *deps)
    return outs[0], outs[1], list(outs[2:2 + n]), list(outs[2 + n:2 + 2 * n]), outs[-1]


def _push_wait(handles, after, *, name, plan):
    send_sems, recv_sems, sources, landing, _ = handles
    n = len(sources)
    after = list(after) if isinstance(after, (list, tuple)) else [after]

    def body(*refs):
        src_refs, land_refs = refs[:n], refs[n:2 * n]
        send_sems_, recv_sems_ = refs[2 * n], refs[2 * n + 1]
        transfers, _ = plan(lax.axis_index("x"), lax.axis_index("y"), lax.axis_index("c"))
        n_tr = len(transfers)
        for a in range(n):
            for k, (peer, src, dst) in enumerate(transfers):
                cp = pltpu.make_async_remote_copy(
                    src_ref=src_refs[a].at[src], dst_ref=land_refs[a].at[dst], send_sem=send_sems_.at[a * n_tr + k],
                    recv_sem=recv_sems_.at[a * n_tr + k], device_id=peer, device_id_type=pl.DeviceIdType.MESH)
                cp.wait_send()
                cp.wait_recv()

    hbm = lambda a: pltpu.HBM(a.shape, a.dtype)
    outs = pl.pallas_call(
        body, name=name, out_shape=tuple(hbm(a) for a in list(sources) + list(landing)),
        in_specs=[_HBM_SPEC] * (2 * n) + [_SEM_SPEC, _SEM_SPEC] + [pl.BlockSpec(memory_space=pl.ANY)] * len(after),
        out_specs=[_HBM_SPEC] * (2 * n), input_output_aliases={i: i for i in range(2 * n)},
        compiler_params=pltpu.CompilerParams(has_side_effects=_DATAFLOW),
    )(*sources, *landing, send_sems, recv_sems, *after)
    return list(outs[n:])


def _plan_everyone(x, y, c):
    me = 4 * x + 2 * y + c
    peers = [(1 - x if k & 4 else x, 1 - y if k & 2 else y, 1 - c if k & 1 else c) for k in range(1, N_DEV)]
    return [(p, 0, me) for p in peers], (0, me)


def _plan_owners(x, y, c):
    me = 4 * x + 2 * y + c
    peers = [(1 - x if k & 4 else x, 1 - y if k & 2 else y, 1 - c if k & 1 else c) for k in range(1, N_DEV)]
    return [((px, py, pc), 4 * px + 2 * py + pc, me) for px, py, pc in peers], None


def _plan_sibling(x, y, c):
    return [((x, y, 1 - c), 2 * chip + (1 - c), chip) for chip in range(N_CHIPS)], None


def _plan_chips(x, y, c):
    mine = 2 * x + y
    peers = [(1 - x, y), (x, 1 - y), (1 - x, 1 - y)]
    return [((px, py, c), 2 * px + py, mine) for px, py in peers], (mine, mine)


def _plan_own_block(x, y, c):
    me = 4 * x + 2 * y + c
    peers = [(x, y, 1 - c), (1 - x, y, c), (x, 1 - y, c), (1 - x, 1 - y, c)]
    return [(p, 0, me) for p in peers], None


def _plan_pass_on(x, y, c):
    slots = [4 * px + 2 * py + c for px, py in ((1 - x, y), (x, 1 - y), (1 - x, 1 - y))]
    return [((x, y, 1 - c), s, s) for s in slots], None


def _pair_sum(parts, received, *, name, tc):
    _, r, c = parts.shape
    core = lax.axis_index("c").astype(jnp.int32).reshape(1)

    def body(core_ref, p_ref, r_ref, o_ref, o2_ref):
        s = (p_ref[...].astype(F32) + r_ref[...].astype(F32)).astype(o_ref.dtype)
        o_ref[...] = s
        o2_ref[...] = s

    out = pl.BlockSpec((None, r, tc), lambda i, j, core_ref: (i, 0, j))
    return pl.pallas_call(
        body, name=name,
        grid_spec=pltpu.PrefetchScalarGridSpec(
            num_scalar_prefetch=1, grid=(N_CHIPS, c // tc),
            in_specs=[pl.BlockSpec((None, r, tc), lambda i, j, core_ref: (2 * i + core_ref[0], 0, j)),
                      pl.BlockSpec((None, r, tc), lambda i, j, core_ref: (i, 0, j))],
            out_specs=[out, out]),
        out_shape=[jax.ShapeDtypeStruct((N_CHIPS, r, c), BF16)] * 2,
        compiler_params=pltpu.CompilerParams(dimension_semantics=("parallel", "parallel"), vmem_limit_bytes=VMEM_LIMIT),
    )(core, parts, received)


def _sum_blocks(a, *, name, tc):
    nblk, r, c = a.shape

    def body(a_ref, o_ref):
        acc = a_ref[0].astype(F32)
        for i in range(1, nblk):
            acc = acc + a_ref[i].astype(F32)
        o_ref[...] = acc

    return pl.pallas_call(
        body, name=name, grid=(c // tc,),
        in_specs=[pl.BlockSpec((nblk, r, tc), lambda j: (0, 0, j))],
        out_specs=pl.BlockSpec((r, tc), lambda j: (0, j)),
        out_shape=jax.ShapeDtypeStruct((r, c), F32),
        compiler_params=pltpu.CompilerParams(dimension_semantics=("parallel",), vmem_limit_bytes=VMEM_LIMIT),
    )(a)


def _adamw(w, g, m, v, *, name, tr):
    r, c = w.shape

    def body(w_ref, g_ref, m_ref, v_ref, d_ref, mo_ref, vo_ref):
        gv = g_ref[...]
        mn = ADAM_B1 * m_ref[...] + (1.0 - ADAM_B1) * gv
        vn = ADAM_B2 * v_ref[...] + (1.0 - ADAM_B2) * jnp.square(gv)
        m_hat = mn / (1.0 - ADAM_B1 ** ADAM_STEP)
        v_hat = vn / (1.0 - ADAM_B2 ** ADAM_STEP)
        d_ref[...] = -ADAM_LR * (m_hat / (jnp.sqrt(v_hat) + ADAM_EPS) + ADAM_WD * w_ref[...])
        mo_ref[...] = mn
        vo_ref[...] = vn

    spec = pl.BlockSpec((tr, c), lambda i: (i, 0))
    return pl.pallas_call(
        body, name=name, grid=(r // tr,), in_specs=[spec] * 4, out_specs=[spec] * 3,
        out_shape=[jax.ShapeDtypeStruct((r, c), F32)] * 3,
        compiler_params=pltpu.CompilerParams(dimension_semantics=("parallel",), vmem_limit_bytes=VMEM_LIMIT),
    )(w, g, m, v)


def _conv_layout():
    idx = []
    for g in range(SSM_GROUPS):
        idx.append(np.concatenate([g * SSM_GW + np.arange(SSM_GW),
                                   SSM_DINNER + g * SSM_DSTATE + np.arange(SSM_DSTATE),
                                   SSM_DINNER + SSM_GROUPS * SSM_DSTATE + g * SSM_DSTATE + np.arange(SSM_DSTATE)]))
    return np.stack(idx)


def _inverse(idx, n):
    inv = np.zeros(n, np.int64)
    pos = np.nonzero(idx >= 0)[0]
    inv[idx[pos]] = pos
    return inv


def _take_rows(a, idx, axis=0):
    idx = np.asarray(idx).reshape(-1)
    pieces, start = [], 0
    for i in range(1, len(idx) + 1):
        same_run = i < len(idx) and ((idx[i] == idx[i - 1] + 1 and idx[i - 1] >= 0) or (idx[i] < 0 and idx[i - 1] < 0))
        if same_run:
            continue
        n = i - start
        if idx[start] < 0:
            shape = list(a.shape)
            shape[axis] = n
            pieces.append(jnp.zeros(shape, a.dtype))
        else:
            pieces.append(lax.slice_in_dim(a, int(idx[start]), int(idx[start]) + n, axis=axis))
        start = i
    return pieces[0] if len(pieces) == 1 else jnp.concatenate(pieces, axis=axis)


def _copy_runs(sources, out_rows, runs, *, name, block, total_rows=None, into=None):
    d, dtype = sources[0].shape[1], sources[0].dtype
    outs = []
    base = 0 if into is None else into[1] // block
    extra, extra_specs = ([], []) if into is None else ([into[0]], [pl.BlockSpec(memory_space=pl.ANY)])
    for o, rows in enumerate(out_rows):
        mine = sorted({i for i, _, oo, _, _ in runs if oo == o})
        ns, nblk = len(mine), rows // block
        sel = np.zeros(nblk, np.int32)
        idx = np.full((ns, nblk), -1, np.int64)
        for i, s, oo, t, n in runs:
            if oo == o:
                assert s % block == 0 and t % block == 0 and n % block == 0
                for b in range(n // block):
                    sel[t // block + b] = mine.index(i)
                    idx[mine.index(i), t // block + b] = s // block + b
        assert (idx.max(axis=0) >= 0).all()
        for i in range(ns):
            first = idx[i, np.nonzero(idx[i] >= 0)[0][0]]
            for b in range(nblk):
                if idx[i, b] < 0:
                    idx[i, b] = idx[i, b - 1] if b > 0 else first

        def body(sel_ref, idx_ref, *refs, ns=ns):
            srcs, out = refs[:ns], refs[-1]
            which = sel_ref[pl.program_id(0)]
            val = srcs[ns - 1][...]
            for i in range(ns - 2, -1, -1):
                val = jnp.where(which == i, srcs[i][...], val)
            out[...] = val

        in_specs = [pl.BlockSpec((block, d), (lambda i_, n_: lambda b, sel_ref, idx_ref: (idx_ref[i_ * n_ + b], 0))(i, nblk))
                    for i in range(ns)]
        full_rows = into[0].shape[0] if into is not None else (total_rows or rows)
        outs.append(pl.pallas_call(
            body, name=f"{name}_{o}" if len(out_rows) > 1 else name,
            grid_spec=pltpu.PrefetchScalarGridSpec(
                num_scalar_prefetch=2, grid=(nblk,), in_specs=in_specs + extra_specs,
                out_specs=pl.BlockSpec((block, d), lambda b, sel_ref, idx_ref: (base + b, 0))),
            out_shape=jax.ShapeDtypeStruct((full_rows, d), dtype),
            input_output_aliases={} if into is None else {2 + ns: 0},
            compiler_params=pltpu.CompilerParams(dimension_semantics=("arbitrary",), vmem_limit_bytes=VMEM_LIMIT),
        )(jnp.asarray(sel), jnp.asarray(idx.reshape(-1), jnp.int32), *[sources[i] for i in mine], *extra))
    return outs


_Z0, _XBC0, _DT0, _GATE0 = 8192, 12288, 18432, 18496
_B0, _C0 = _XBC0 + SSM_DINNER, _XBC0 + SSM_DINNER + SSM_GROUPS * SSM_DSTATE


def _in_proj_runs():
    runs = [(part * 2048 + h * HG_DK, 0, h * HG_BLK + part * HG_DK, HG_DK) for h in range(HG_HEADS) for part in range(4)]
    for g in range(SSM_GROUPS):
        base = g * SSM_BLK
        runs += [(_XBC0 + g * SSM_GW, 1, base, SSM_GW), (_B0 + g * SSM_DSTATE, 1, base + SSM_GW, SSM_DSTATE),
                 (_C0 + g * SSM_DSTATE, 1, base + SSM_GW + SSM_DSTATE, SSM_DSTATE),
                 (_Z0 + g * SSM_GW, 1, base + SSM_XBC + 128, SSM_GW)]
    return runs + [(_GATE0, 2, 0, 2 * D_MODEL)]


def _in_proj_to_kernel(in_t):
    d = in_t.shape[1]
    dt = jnp.pad(in_t[_DT0:_GATE0].reshape(SSM_GROUPS, SSM_HPG, d), ((0, 0), (0, 128 - SSM_HPG), (0, 0)))
    runs = [(0, src, sec, dst, n) for src, sec, dst, n in _in_proj_runs() if sec < 2]
    runs += [(1, g * 128, 1, g * SSM_BLK + SSM_XBC, 128) for g in range(SSM_GROUPS)]
    hg, ssm = _copy_runs([in_t, dt.reshape(SSM_GROUPS * 128, d)], [_Z0, SSM_GROUPS * SSM_BLK], runs,
                         name="in_proj_to_kernel_layout", block=128)
    return hg, ssm, in_t[_GATE0:]


def _in_proj_from_kernel(hg, ssm, gate):
    d = hg.shape[1]
    dt = ssm.reshape(SSM_GROUPS, SSM_BLK, d)[:, SSM_XBC:SSM_XBC + SSM_HPG].reshape(SSM_HEADS, d)
    runs = [(sec, dst, 0, src, n) for src, sec, dst, n in _in_proj_runs() if sec < 2]
    main = _copy_runs([hg, ssm], [_DT0], runs, name="in_proj_to_global_layout", block=128, total_rows=IN_TOTAL)[0]
    tail = [(0, 0, 0, 0, SSM_HEADS), (1, 0, 0, SSM_HEADS, 2 * D_MODEL)]
    return _copy_runs([dt, gate], [IN_TOTAL - _DT0], tail, name="in_proj_to_global_layout_tail", block=SSM_HEADS,
                      into=(main, _DT0))[0]


def _up_to_kernel(up_t):
    runs = [(0, part * D_FF + g * FFN_GW, 0, (2 * g + part) * FFN_GW, FFN_GW) for g in range(FFN_G) for part in range(2)]
    return _copy_runs([up_t], [2 * D_FF], runs, name="up_to_kernel_layout", block=FFN_GW)[0]


def _up_from_kernel(up):
    runs = [(0, (2 * g + part) * FFN_GW, 0, part * D_FF + g * FFN_GW, FFN_GW) for g in range(FFN_G) for part in range(2)]
    return _copy_runs([up], [2 * D_FF], runs, name="up_to_global_layout", block=FFN_GW)[0]


_SMALL = (("mix_pre_norm", (1, 2048)), ("mix_post_norm", (1, 2048)), ("hg_lb_table", (2, 2048)), ("hg_out_norm", (1, 128)),
          ("ssm_conv_w", (4, 6144)), ("ssm_conv_b", (1, 6144)), ("ssm_dt_bias", (1, 64)), ("ssm_A_log", (1, 64)),
          ("ssm_D", (1, 64)), ("ssm_out_norm", (1, 4096)), ("ffn_pre_norm", (1, 2048)), ("ffn_post_norm", (1, 2048)),
          ("ffn_conv_w", (3, 5632)), ("ffn_conv_b", (1, 5632)), ("loss", (1, 1)))
_PACK_ROWS = 8 * (-(-sum(int(np.prod(s)) for _, s in _SMALL) // 1024))


def _pack(vals):
    flat = jnp.concatenate([vals[k].astype(F32).reshape(-1) for k, _ in _SMALL])
    return jnp.pad(flat, (0, _PACK_ROWS * 128 - flat.shape[0])).reshape(_PACK_ROWS, 128)


def _unpack(packed):
    flat, out, o = packed.reshape(-1), {}, 0
    for k, s in _SMALL:
        n = int(np.prod(s))
        out[k] = flat[o:o + n].reshape(s)
        o += n
    return out


def _local_step(x, target, w, p, late_weights=None, emit=lambda key, gw: None):
    t = x.shape[0]
    one = lambda a: a.reshape((1,) + a.shape)
    row = dict(rows=t, groups=1, consts=[], carries=[])

    (h1,), _ = _stage_fwd(_pre_step, name="pre_fwd", chunk=512, nc=1, xs=[(x, D_MODEL)], params=[one(p["mix_pre_norm"])],
                          ys=[(D_MODEL, BF16)], **row)
    proj_hg = _matmul_nt(h1, w["in_hg"], name="proj_hg")
    proj_ssm = _matmul_nt(h1, w["in_ssm"], name="proj_ssm")
    proj_gate = _matmul_nt(h1, w["in_gate"], name="proj_gate")

    hg = dict(rows=t, chunk=HG_CHUNK, nc=4, groups=HG_HEADS, xs=[(proj_hg, HG_BLK)], params=[p["hg_tab"], p["hg_nw"]],
              consts=_hg_consts(), carries=[(HG_DK, HG_DK)], gpb=HG_HEADS)
    (y_hg,), hg_saved = _stage_fwd(_hg_step, name="hg_fwd", ys=[(HG_DK, BF16)], **hg)

    ssd = dict(rows=t, chunk=SSM_CHUNK, nc=4, groups=SSM_GROUPS, xs=[(proj_ssm, SSM_BLK)],
               params=[p["conv_w"], p["conv_b"], p["dt_bias"], p["a_log"], p["d_skip"], p["ssm_nw"]],
               consts=_ssd_consts(), carries=[(4 * 128, SSM_DSTATE), (HALO, SSM_XBC)])
    (y_ssm,), ssd_saved = _stage_fwd(_ssd_step, name="ssd_fwd", ys=[(SSM_GW, BF16)], **ssd)

    if late_weights is not None:
        w = {**w, **late_weights([y_hg, y_ssm])}
    u_hg = _matmul_nn(y_hg, w["branch_hg"], name="branch_hg")
    u_ssm = _matmul_nn(y_ssm, w["branch_ssm"], name="branch_ssm")
    mix = dict(chunk=256, nc=1, xs=[(proj_gate, 2 * D_MODEL), (u_hg, D_MODEL), (u_ssm, D_MODEL)], params=[], **row)
    (mixed,), _ = _stage_fwd(_mix_step, name="mix_fwd", ys=[(D_MODEL, BF16)], **mix)
    v = _matmul_nn(mixed, w["out"], name="out_proj")
    post = dict(chunk=256, nc=1, xs=[(x, D_MODEL), (v, D_MODEL)],
                params=[one(p["mix_post_norm"]), one(p["ffn_pre_norm"])], **row)
    (x1, h2), _ = _stage_fwd(_post_step, name="post_fwd", ys=[(D_MODEL, F32), (D_MODEL, BF16)], **post)
    gu = _matmul_nt(h2, w["up"], name="ffn_up")
    ffn = dict(rows=t, chunk=256, nc=2, groups=FFN_G, xs=[(gu, 2 * FFN_GW)], params=[p["ffn_conv_w"], p["ffn_conv_b"]],
               consts=[], carries=[(HALO, FFN_GW)])
    (act,), ffn_saved = _stage_fwd(_ffn_step, name="ffn_fwd", ys=[(FFN_GW, BF16)], **ffn)
    d = _matmul_nn(act, w["down"], name="ffn_down")

    def head_step(carry, xv, civ, pv, cv):
        x1_, d_, tgt = xv

        def per_row_loss(a, b, nw):
            e = a + _rms(b, nw) - tgt
            return 0.5 * jnp.mean(e * e, axis=1, keepdims=True)

        lrow, vjp = jax.vjp(per_row_loss, x1_, d_, pv[0])
        dx1_, dd_, dnw = vjp(jnp.ones_like(lrow))
        loss = jnp.broadcast_to(jnp.sum(lrow, axis=0, keepdims=True), (1, 128))
        return [], [dx1_, dd_], [], [dnw, loss]

    (dy, dd), _, (g_ffn_post, loss) = _scan_call(
        head_step, name="loss_head", chunk=256, nc=1, xs=[(x1, D_MODEL), (d, D_MODEL), (target, D_MODEL)],
        params=[one(p["ffn_post_norm"])], ys=[(D_MODEL, F32), (D_MODEL, BF16)], accs=[(1, D_MODEL), (1, 128)], **row)

    gw = {}
    gw["down"] = _matmul_tn(act, dd, name="g_down")
    dact = _matmul_nt(dd, w["down"], name="d_act", dep=emit("down", gw))
    (dgu,), (g_fcw, g_fcb) = _stage_bwd(_ffn_step, name="ffn_bwd", saved=ffn_saved, dys=[(dact, FFN_GW)], dxs=[BF16], **ffn)
    gw["up"] = _matmul_tn(dgu, h2, name="g_up")
    dh2 = _matmul_nn(dgu, w["up"], name="d_h2", dep=emit("up", gw))
    (dx1, dv), (g_mix_post, g_ffn_pre) = _stage_bwd(_post_step, name="post_bwd", saved=[], dys=[(dy, D_MODEL), (dh2, D_MODEL)],
                                                    dxs=[F32, BF16], **post)
    gw["out"] = _matmul_tn(mixed, dv, name="g_out")
    dmixed = _matmul_nt(dv, w["out"], name="d_mixed")
    (dgate, du_hg, du_ssm), _ = _stage_bwd(_mix_step, name="mix_bwd", saved=[], dys=[(dmixed, D_MODEL)],
                                           dxs=[BF16, BF16, BF16], **mix)
    gw["in_gate"] = _matmul_tn(dgate, h1, name="g_in_gate")
    gw["branch_hg"] = _matmul_tn(y_hg, du_hg, name="g_branch_hg")
    gw["branch_ssm"] = _matmul_tn(y_ssm, du_ssm, name="g_branch_ssm")
    dy_hg = _matmul_nt(du_hg, w["branch_hg"], name="d_y_hg", dep=emit("branches", gw))
    dy_ssm = _matmul_nt(du_ssm, w["branch_ssm"], name="d_y_ssm")
    (dproj_ssm,), g_ssd = _stage_bwd(_ssd_step, name="ssd_bwd", saved=ssd_saved, dys=[(dy_ssm, SSM_GW)], dxs=[BF16], **ssd)
    gw["in_ssm"] = _matmul_tn(dproj_ssm, h1, name="g_in_ssm")
    (dproj_hg,), (g_tab, g_hg_nw) = _stage_bwd(_hg_step, name="hg_bwd", saved=hg_saved, dys=[(dy_hg, HG_DK)], dxs=[BF16],
                                               **{**hg, "nc": 2})
    gw["in_hg"] = _matmul_tn(dproj_hg, h1, name="g_in_hg")
    dh_a = _matmul_nn(dproj_hg, w["in_hg"], name="d_h1_hg", dep=emit("in", gw))
    dh_b = _matmul_nn(dproj_ssm, w["in_ssm"], name="d_h1_ssm")
    dh_c = _matmul_nn(dgate, w["in_gate"], name="d_h1_gate")

    def pre_bwd_step(carry, xv, civ, pv, cv):
        x_, da, db, dc, dres = xv
        _, vjp = jax.vjp(_rms, x_, pv[0])
        dx_, dnw = vjp(da + db + dc)
        return [], [dx_ + dres], [], [dnw]

    (grad_x,), _, (g_mix_pre,) = _scan_call(
        pre_bwd_step, name="pre_bwd", chunk=256, nc=1,
        xs=[(x, D_MODEL), (dh_a, D_MODEL), (dh_b, D_MODEL), (dh_c, D_MODEL), (dx1, D_MODEL)],
        params=[one(p["mix_pre_norm"])], ys=[(D_MODEL, F32)], accs=[(1, D_MODEL)], **row)

    gp = dict(mix_pre_norm=g_mix_pre[0], mix_post_norm=g_mix_post[0], ffn_pre_norm=g_ffn_pre[0], ffn_post_norm=g_ffn_post[0],
              hg_tab=g_tab, hg_nw=g_hg_nw, conv_w=g_ssd[0], conv_b=g_ssd[1], dt_bias=g_ssd[2], a_log=g_ssd[3],
              d_skip=g_ssd[4], ssm_nw=g_ssd[5], ffn_conv_w=g_fcw, ffn_conv_b=g_fcb, loss=loss[0, :, :1])
    return grad_x, gw, gp


def _small_to_kernel_layout(s):
    conv_idx = _conv_layout()
    pad_heads = lambda a: jnp.pad(a.reshape(SSM_GROUPS, 1, SSM_HPG), ((0, 0), (0, 0), (0, 128 - SSM_HPG)))
    return dict(
        mix_pre_norm=s["mix_pre_norm"], mix_post_norm=s["mix_post_norm"], ffn_pre_norm=s["ffn_pre_norm"],
        ffn_post_norm=s["ffn_post_norm"],
        hg_tab=s["hg_lb_table"].reshape(2, HG_HEADS, HG_DK).transpose(1, 0, 2),
        hg_nw=jnp.broadcast_to(s["hg_out_norm"].reshape(1, 1, HG_DK), (HG_HEADS, 1, HG_DK)),
        conv_w=_take_rows(s["ssm_conv_w"], conv_idx, axis=1).reshape(SSM_CONV, SSM_GROUPS, SSM_XBC).transpose(1, 0, 2),
        conv_b=_take_rows(s["ssm_conv_b"], conv_idx, axis=1).reshape(SSM_GROUPS, 1, SSM_XBC),
        dt_bias=pad_heads(s["ssm_dt_bias"]), a_log=pad_heads(s["ssm_A_log"]),
        d_skip=jnp.repeat(s["ssm_D"].reshape(SSM_HEADS), SSM_HEADDIM).reshape(SSM_GROUPS, 1, SSM_GW),
        ssm_nw=s["ssm_out_norm"].reshape(SSM_GROUPS, 1, SSM_GW),
        ffn_conv_w=s["ffn_conv_w"].reshape(FFN_CONV, FFN_G, FFN_GW).transpose(1, 0, 2),
        ffn_conv_b=s["ffn_conv_b"].reshape(FFN_G, 1, FFN_GW),
    )


def _small_from_kernel_layout(g):
    conv_inv = _inverse(_conv_layout().reshape(-1), SSM_CONV_DIM)
    heads = lambda a: a[:, 0, :SSM_HPG].reshape(1, SSM_HEADS)
    return dict(
        mix_pre_norm=g["mix_pre_norm"], mix_post_norm=g["mix_post_norm"], ffn_pre_norm=g["ffn_pre_norm"],
        ffn_post_norm=g["ffn_post_norm"],
        hg_lb_table=g["hg_tab"].transpose(1, 0, 2).reshape(2, HG_HEADS * HG_DK),
        hg_out_norm=jnp.sum(g["hg_nw"], axis=0),
        ssm_conv_w=_take_rows(g["conv_w"].transpose(1, 0, 2).reshape(SSM_CONV, -1), conv_inv, axis=1),
        ssm_conv_b=_take_rows(g["conv_b"].reshape(1, -1), conv_inv, axis=1),
        ssm_dt_bias=heads(g["dt_bias"]), ssm_A_log=heads(g["a_log"]),
        ssm_D=jnp.sum(g["d_skip"].reshape(SSM_HEADS, SSM_HEADDIM), axis=1).reshape(1, SSM_HEADS),
        ssm_out_norm=g["ssm_nw"].reshape(1, SSM_DINNER),
        ffn_conv_w=g["ffn_conv_w"].transpose(1, 0, 2).reshape(FFN_CONV, D_FF),
        ffn_conv_b=g["ffn_conv_b"].reshape(1, D_FF),
        loss=g["loss"],
    )


def kernel(x, w_in, mix_pre_norm, mix_post_norm, hg_lb_table, hg_out_norm, ssm_conv_w, ssm_conv_b, ssm_dt_bias, ssm_A_log, ssm_D, ssm_out_norm, w_branch_hg, w_branch_ssm, w_out, ffn_pre_norm, ffn_post_norm, ffn_w_up, ffn_conv_w, ffn_conv_b, ffn_w_down, loss_target, m_w_in, m_mix_pre_norm, m_mix_post_norm, m_hg_lb_table, m_hg_out_norm, m_ssm_conv_w, m_ssm_conv_b, m_ssm_dt_bias, m_ssm_A_log, m_ssm_D, m_ssm_out_norm, m_w_branch_hg, m_w_branch_ssm, m_w_out, m_ffn_pre_norm, m_ffn_post_norm, m_ffn_w_up, m_ffn_conv_w, m_ffn_conv_b, m_ffn_w_down, v_w_in, v_mix_pre_norm, v_mix_post_norm, v_hg_lb_table, v_hg_out_norm, v_ssm_conv_w, v_ssm_conv_b, v_ssm_dt_bias, v_ssm_A_log, v_ssm_D, v_ssm_out_norm, v_w_branch_hg, v_w_branch_ssm, v_w_out, v_ffn_pre_norm, v_ffn_post_norm, v_ffn_w_up, v_ffn_conv_w, v_ffn_conv_b, v_ffn_w_down):
    names = ["w_in", "mix_pre_norm", "mix_post_norm", "hg_lb_table", "hg_out_norm", "ssm_conv_w", "ssm_conv_b", "ssm_dt_bias",
             "ssm_A_log", "ssm_D", "ssm_out_norm", "w_branch_hg", "w_branch_ssm", "w_out", "ffn_pre_norm", "ffn_post_norm",
             "ffn_w_up", "ffn_conv_w", "ffn_conv_b", "ffn_w_down"]
    ws = dict(zip(names, (w_in, mix_pre_norm, mix_post_norm, hg_lb_table, hg_out_norm, ssm_conv_w, ssm_conv_b, ssm_dt_bias,
                          ssm_A_log, ssm_D, ssm_out_norm, w_branch_hg, w_branch_ssm, w_out, ffn_pre_norm, ffn_post_norm,
                          ffn_w_up, ffn_conv_w, ffn_conv_b, ffn_w_down)))
    ms = dict(zip(names, (m_w_in, m_mix_pre_norm, m_mix_post_norm, m_hg_lb_table, m_hg_out_norm, m_ssm_conv_w, m_ssm_conv_b,
                          m_ssm_dt_bias, m_ssm_A_log, m_ssm_D, m_ssm_out_norm, m_w_branch_hg, m_w_branch_ssm, m_w_out,
                          m_ffn_pre_norm, m_ffn_post_norm, m_ffn_w_up, m_ffn_conv_w, m_ffn_conv_b, m_ffn_w_down)))
    vs = dict(zip(names, (v_w_in, v_mix_pre_norm, v_mix_post_norm, v_hg_lb_table, v_hg_out_norm, v_ssm_conv_w, v_ssm_conv_b,
                          v_ssm_dt_bias, v_ssm_A_log, v_ssm_D, v_ssm_out_norm, v_w_branch_hg, v_w_branch_ssm, v_w_out,
                          v_ffn_pre_norm, v_ffn_post_norm, v_ffn_w_up, v_ffn_conv_w, v_ffn_conv_b, v_ffn_w_down)))
    me = 4 * lax.axis_index("x") + 2 * lax.axis_index("y") + lax.axis_index("c")

    late_shards = [ffn_w_up[0].T.astype(BF16), w_branch_hg[0].astype(BF16), w_branch_ssm[0].astype(BF16),
                   w_out[0].astype(BF16), ffn_w_down[0].astype(BF16)]
    landing = [lax.dynamic_update_slice_in_dim(lax.empty((N_DEV,) + s.shape, s.dtype), s[None], me, axis=0)
               for s in late_shards]
    gathered = _all_gather([w_in[0].T.astype(BF16), ssm_conv_w[0], ffn_conv_w[0]], name="gather_in_proj")
    late = _push_start([s[None] for s in late_shards], landing, name="late_weights_start", plan=_plan_own_block,
                       after=gathered[1])
    in_hg, in_ssm, in_gate = _in_proj_to_kernel(gathered[0].reshape(IN_TOTAL, D_MODEL))
    w = dict(in_hg=in_hg, in_ssm=in_ssm, in_gate=in_gate)
    small = {k: ws[k] for k, _ in _SMALL[:-1]}
    small["mix_pre_norm"] = mix_pre_norm + late[4][0, 0]
    small["ssm_conv_w"] = gathered[1].transpose(1, 0, 2).reshape(SSM_CONV, SSM_CONV_DIM)
    small["ffn_conv_w"] = gathered[2].transpose(1, 0, 2).reshape(FFN_CONV, D_FF)
    small = {k: small[k].reshape(s) for k, s in _SMALL[:-1]}

    def late_weights(after):
        landed = _push_wait(late, after, name="late_weights_wait", plan=_plan_own_block)
        up_all, bhg, bssm, out, down = _push(landed, name="late_weights_pass_on", plan=_plan_pass_on)
        return dict(up=_up_to_kernel(up_all.reshape(2 * D_FF, D_MODEL)), branch_hg=bhg.reshape(D_MODEL, D_MODEL),
                    branch_ssm=bssm.reshape(SSM_DINNER, D_MODEL), out=out.reshape(D_MODEL, D_MODEL),
                    down=down.reshape(D_FF, D_MODEL))

    in_flight = []

    def launch_direct(key, named_parts):
        ks, parts = zip(*named_parts)
        landing = [lax.dynamic_update_slice_in_dim(lax.empty(p.shape, p.dtype), lax.dynamic_slice_in_dim(p, me, 1, axis=0),
                                                   me, axis=0) for p in parts]
        handles = _push_start(list(parts), landing, name="grads_to_owners_start_" + key, plan=_plan_owners)
        in_flight.append(("grads_to_owners_wait_" + key, ks, handles, _plan_owners))
        return handles[4]

    def launch_two_level(key, named_parts):
        ks, parts = zip(*named_parts)
        from_sibling = _push(list(parts), name="grads_to_sibling_" + key, out_slots=N_CHIPS, plan=_plan_sibling)
        sums = [_pair_sum(p, r, name="pair_sum_" + k, tc=256) for k, p, r in zip(ks, parts, from_sibling)]
        handles = _push_start([q for q, _ in sums], [z for _, z in sums], name="grads_to_chips_start_" + key, plan=_plan_chips)
        in_flight.append(("grads_to_chips_wait_" + key, ks, handles, _plan_chips))
        return handles[4]

    def emit(key, gw):
        blocks = lambda a: a.reshape(N_DEV, -1, D_MODEL)
        if key == "down":
            return launch_direct(key, [("ffn_w_down", blocks(gw["down"]))])
        if key == "up":
            return launch_direct(key, [("ffn_w_up", blocks(_up_from_kernel(gw["up"])))])
        if key == "branches":
            return launch_direct(key, [("w_branch_hg", blocks(gw["branch_hg"])), ("w_branch_ssm", blocks(gw["branch_ssm"])),
                                       ("w_out", blocks(gw["out"]))])
        return launch_two_level(key, [("w_in", blocks(_in_proj_from_kernel(gw["in_hg"], gw["in_ssm"], gw["in_gate"])))])

    grad_x, gw, gp = _local_step(x[0], loss_target[0], w, _small_to_kernel_layout(small), late_weights, emit)

    big_names = ["w_in", "ffn_w_up", "w_branch_hg", "w_branch_ssm", "w_out", "ffn_w_down"]
    grads = {}
    for wait_name, ks, handles, plan in in_flight:
        landed = _push_wait(handles, grad_x, name=wait_name, plan=plan)
        for k, r in zip(ks, landed):
            g = _sum_blocks(r, name="sum_" + k, tc=256)
            grads[k] = g.T if k in ("w_in", "ffn_w_up") else g
    small_all = _push([_pack(_small_from_kernel_layout(gp))[None]], name="small_to_everyone", out_slots=N_DEV,
                      plan=_plan_everyone)
    small_g = _unpack(_sum_blocks(small_all[0], name="sum_small", tc=128))
    loss = small_g.pop("loss").reshape(())
    for k, g in small_g.items():
        if k in ("ssm_conv_w", "ffn_conv_w"):
            n = g.shape[1] // N_DEV
            g = lax.dynamic_slice_in_dim(g, me * n, n, axis=1)
        grads[k] = g

    delta, new_m, new_v = {}, {}, {}
    for k in big_names:
        delta[k], new_m[k], new_v[k] = _adamw(ws[k][0], grads[k], ms[k][0], vs[k][0], name="adamw_" + k, tr=64)
    small_names = [k for k in names if k not in big_names]
    flat = lambda d: jnp.concatenate([d[k].astype(F32).reshape(-1) for k in small_names])
    n_small = sum(int(np.prod(ws[k].shape)) for k in small_names)
    rows = 8 * (-(-n_small // 1024))
    pack2 = lambda d: jnp.pad(flat(d), (0, rows * 128 - n_small)).reshape(rows, 128)
    v_packed = jnp.pad(flat(vs), (0, rows * 128 - n_small), constant_values=1.0).reshape(rows, 128)
    packed = _adamw(pack2(ws), pack2(grads), pack2(ms), v_packed, name="adamw_small", tr=rows)
    o = 0
    for k in small_names:
        n = int(np.prod(ws[k].shape))
        delta[k], new_m[k], new_v[k] = (a.reshape(-1)[o:o + n].reshape(ws[k].shape) for a in packed)
        o += n

    full = lambda d: [d[k].reshape(ws[k].shape) for k in names]
    return (loss, grad_x[None], *full(grads), *full(delta), *full(new_m), *full(new_v))
```

```python
import functools

import numpy as np
import jax
import jax.numpy as jnp
from jax import lax
from jax.experimental import pallas as pl
from jax.experimental.pallas import tpu as pltpu

F32, BF16 = jnp.float32, jnp.bfloat16

D_MODEL = 2048
EPS = 1e-6
HG_HEADS, HG_DK, HG_CHUNK = 16, 128, 64
HG_BLK = 4 * HG_DK
SSM_DINNER, SSM_HEADDIM, SSM_HEADS, SSM_GROUPS, SSM_DSTATE, SSM_CONV = 4096, 64, 64, 8, 128, 4
SSM_CHUNK = 128
SSM_GW = SSM_DINNER // SSM_GROUPS
SSM_HPG = SSM_HEADS // SSM_GROUPS
SSM_XBC = SSM_GW + 2 * SSM_DSTATE
SSM_BLK = SSM_XBC + 128 + SSM_GW
SSM_CONV_DIM = SSM_DINNER + 2 * SSM_GROUPS * SSM_DSTATE
D_FF, FFN_CONV = 5632, 3
FFN_GW = 512
FFN_G = D_FF // FFN_GW
IN_TOTAL = 22592
N_DEV = 8
HALO = 8
VMEM_LIMIT = 52 * 1024 * 1024
ADAM_LR, ADAM_B1, ADAM_B2, ADAM_EPS, ADAM_WD, ADAM_STEP = 0.001, 0.9, 0.999, 1e-08, 0.01, 10

_DIMS = {"nn": ((1,), (0,)), "nt": ((1,), (1,)), "tn": ((0,), (0,))}


def _mm_raw(a, b, mode):
    return lax.dot_general(a.astype(BF16), b.astype(BF16), (_DIMS[mode], ((), ())), preferred_element_type=F32)


@functools.partial(jax.custom_vjp, nondiff_argnums=(2,))
def _mm(a, b, mode):
    return _mm_raw(a, b, mode)


def _mm_fwd(a, b, mode):
    return _mm_raw(a, b, mode), (a, b)


def _mm_bwd(mode, res, dc):
    a, b = res
    if mode == "nn":
        return _mm_raw(dc, b, "nt"), _mm_raw(a, dc, "tn")
    if mode == "nt":
        return _mm_raw(dc, b, "nn"), _mm_raw(dc, a, "tn")
    return _mm_raw(b, dc, "nt"), _mm_raw(a, dc, "nn")


_mm.defvjp(_mm_fwd, _mm_bwd)


def _cmm_raw(m, x, mode):
    hi = x.astype(BF16)
    r1 = x - hi.astype(F32)
    mid = r1.astype(BF16)
    lo = (r1 - mid.astype(F32)).astype(BF16)
    dn = (_DIMS[mode], ((), ()))
    dot = lambda p: lax.dot_general(m, p, dn, preferred_element_type=F32)
    return dot(hi) + dot(mid) + dot(lo)


@jax.custom_vjp
def _cmm(m, x):
    return _cmm_raw(m, x, "nn")


def _cmm_fwd(m, x):
    return _cmm_raw(m, x, "nn"), m


def _cmm_bwd(m, dy):
    return jnp.zeros_like(m), _cmm_raw(m, dy, "tn")


_cmm.defvjp(_cmm_fwd, _cmm_bwd)


@functools.partial(jax.custom_vjp, nondiff_argnums=(1,))
def _sroll(x, s):
    return pltpu.roll(x, s, 0) if s else x


def _sroll_fwd(x, s):
    return _sroll(x, s), None


def _sroll_bwd(s, _, ct):
    return ((pltpu.roll(ct, ct.shape[0] - s, 0) if s else ct),)


_sroll.defvjp(_sroll_fwd, _sroll_bwd)


def _rms(x, w):
    return x * lax.rsqrt(jnp.mean(x * x, axis=-1, keepdims=True) + EPS) * w


def _softplus(x):
    return jnp.maximum(x, 0.0) + jnp.log(1.0 + jnp.exp(-jnp.abs(x)))


def _causal_conv(halo, x, w, b):
    k_taps = w.shape[0]
    xe = jnp.concatenate([halo, x], axis=0)
    out = b
    for k in range(k_taps):
        out = out + w[k:k + 1, :] * _sroll(xe, k_taps - 1 - k)[HALO:, :]
    return out


def _hg_consts():
    c = HG_CHUNK
    t = np.arange(c)
    blocks, pair = [], []
    for m in (32, 16, 8, 4, 2, 1):
        pos = t % (2 * m)
        late = pos >= m
        mid = t - pos + m
        j = t[None, :]
        mq = late[:, None] & (j >= mid[:, None]) & (j <= t[:, None])
        mk = (~late)[:, None] & (j > t[:, None]) & (j <= mid[:, None] - 1)
        blocks.append(mq | mk)
        parent = t // (2 * m)
        pair.append((parent[:, None] == parent[None, :]) & late[:, None] & (~late)[None, :])
    blocks.append(t[None, :] <= t[:, None])
    mall = jnp.asarray(np.concatenate(blocks, 0).astype(np.float32), BF16)
    pair = jnp.asarray(np.stack(pair, 0).astype(np.float32))
    eye = jnp.asarray(np.eye(c, dtype=np.float32))
    return [mall, pair, eye]


def _hg_step(carry, xs, params, consts):
    (st,) = carry
    blk = xs[0].astype(F32)
    tab, nw = params
    mall, pair, eye = consts
    c, dk = HG_CHUNK, HG_DK
    q_raw, f_raw, v, og = blk[:, :dk], blk[:, dk:2 * dk], blk[:, 2 * dk:3 * dk], blk[:, 3 * dk:]
    lb = jax.nn.sigmoid(tab[0:1, :] - tab[1:2, :])
    f = lb + (1.0 - lb) * jax.nn.sigmoid(f_raw)
    g = jnp.log(f)
    kk = 1.0 - f
    qh = jax.nn.silu(q_raw) * (HG_DK ** -0.5)
    yield
    sums = _cmm(mall, g)
    yield
    b = sums[6 * c:, :]
    fac = jnp.exp(sums[:6 * c, :])
    scores = eye * jnp.sum(qh * kk, axis=1, keepdims=True)
    b_last = jnp.sum(g, axis=0, keepdims=True)
    yield
    inter = _mm(qh * jnp.exp(b), st, "nt")
    st_new = st * jnp.exp(b_last) + _mm(v, kk * jnp.exp(b_last - b), "tn")
    yield
    for l in range(6):
        fl = fac[l * c:(l + 1) * c, :]
        scores = scores + pair[l] * _mm(qh * fl, kk * fl, "nt")
        if l % 2:
            yield
    o = _mm(scores, v, "nn") + inter
    yield
    y = _rms(o, nw) * jax.nn.silu(og)
    return [st_new], [y]


def _ssd_consts():
    t = np.arange(SSM_CHUNK)
    tril = (t[None, :] <= t[:, None]).astype(np.float32)
    return [jnp.asarray(tril, BF16), jnp.asarray(tril)]


def _ssd_step(carry, xs, params, consts):
    st, halo = carry
    blk = xs[0].astype(F32)
    conv_w, conv_b, dtb, alog, dskip, nw = params
    tril_b, tril = consts
    c = SSM_CHUNK
    raw, dtr, z = blk[:, :SSM_XBC], blk[:, SSM_XBC:SSM_XBC + 128], blk[:, SSM_XBC + 128:]
    act = jax.nn.silu(_causal_conv(halo, raw, conv_w, conv_b))
    xh, bm, cm = act[:, :SSM_GW], act[:, SSM_GW:SSM_GW + SSM_DSTATE], act[:, SSM_GW + SSM_DSTATE:]
    dt = _softplus(dtr + dtb)
    da = dt * (-jnp.exp(alog))
    acum = _cmm(tril_b, da)
    acum_t = acum.T
    a_last = jnp.sum(da, axis=0, keepdims=True)
    cb_causal = _mm(cm, bm, "nt") * tril
    lane = lax.broadcasted_iota(jnp.int32, (c, 128), 1)
    row = lax.broadcasted_iota(jnp.int32, (128, 128), 0)
    first = lane < SSM_HEADDIM
    ys, st_new = [], []
    for j in range(SSM_HPG // 2):
        xp = xh[:, 128 * j:128 * (j + 1)]
        sp = st[128 * j:128 * (j + 1), :]
        r0, r1 = 2 * j, 2 * j + 1
        col = lambda a, r: jnp.broadcast_to(a[:, r:r + 1], (c, 128))
        xdt = xp * jnp.where(first, col(dt, r0), col(dt, r1))
        yj = _mm(cm, sp, "nt") * jnp.exp(jnp.where(first, col(acum, r0), col(acum, r1)))
        for r, keep in ((r0, first), (r1, ~first)):
            dec = jnp.broadcast_to(acum[:, r:r + 1], (c, c)) - jnp.broadcast_to(acum_t[r:r + 1, :], (c, c))
            m = cb_causal * jnp.exp(jnp.minimum(dec, 0.0))
            yj = yj + _mm(m, jnp.where(keep, xdt, 0.0), "nn")
        al0, al1 = a_last[:, r0:r0 + 1], a_last[:, r1:r1 + 1]
        wts = jnp.exp(jnp.where(first, al0 - col(acum, r0), al1 - col(acum, r1)))
        st_new.append(jnp.where(row < SSM_HEADDIM, jnp.exp(al0), jnp.exp(al1)) * sp + _mm(xdt * wts, bm, "tn"))
        ys.append(yj)
        yield
    y = jnp.concatenate(ys, axis=1) + dskip * xh
    y = _rms(y * jax.nn.silu(z), nw)
    return [jnp.concatenate(st_new, axis=0), raw[c - HALO:, :]], [y]


def _ffn_step(carry, xs, params, consts):
    (halo,) = carry
    blk = xs[0].astype(F32)
    conv_w, conv_b = params
    gate, up = blk[:, :FFN_GW], blk[:, FFN_GW:]
    a = jax.nn.gelu(_causal_conv(halo, gate, conv_w, conv_b), approximate=True) * up
    return [gate[gate.shape[0] - HALO:, :]], [a]


def _pre_step(carry, xs, params, consts):
    return [], [_rms(xs[0], params[0])]


def _mix_step(carry, xs, params, consts):
    gates, uh, us = (a.astype(F32) for a in xs)
    return [], [jax.nn.sigmoid(gates[:, :D_MODEL]) * uh + jax.nn.sigmoid(gates[:, D_MODEL:]) * us]


def _post_step(carry, xs, params, consts):
    x, v = xs
    x1 = x + _rms(v, params[0])
    return [], [x1, _rms(x1, params[1])]


def _scan_call(step, *, name, rows, chunk, nc, groups, xs, cins=(), params=(), consts=(), carries=(), ys=(), couts=(),
               accs=(), reverse=False, gpb=1, multi=False):
    blk_rows = chunk * nc
    nb = rows // blk_rows
    n_chunks = rows // chunk
    assert nb * blk_rows == rows and groups % gpb == 0
    rb = (lambda i: nb - 1 - i) if reverse else (lambda i: i)
    n_x, n_ci, n_p, n_c = len(xs), len(cins), len(params), len(consts)
    n_y, n_co, n_a = len(ys), len(couts), len(accs)

    def chunk_spec(shape):
        zeros = (0,) * len(shape)
        return pl.BlockSpec((gpb, nc) + tuple(shape), lambda g, i: (g, rb(i)) + zeros)

    in_specs = [pl.BlockSpec((blk_rows, gpb * w), lambda g, i: (rb(i), g)) for _, w in xs]
    in_specs += [chunk_spec(a.shape[2:]) for a in cins]
    in_specs += [pl.BlockSpec((gpb,) + tuple(a.shape[1:]), lambda g, i: (g, 0, 0)) for a in params]
    in_specs += [pl.BlockSpec(a.shape, (lambda nd: lambda g, i: (0,) * nd)(a.ndim)) for a in consts]
    out_specs = [pl.BlockSpec((blk_rows, gpb * w), lambda g, i: (rb(i), g)) for w, _ in ys]
    out_specs += [chunk_spec(s) for s in couts]
    out_specs += [pl.BlockSpec((gpb, r, c), lambda g, i: (g, 0, 0)) for r, c in accs]
    out_shape = [jax.ShapeDtypeStruct((rows, groups * w), dt) for w, dt in ys]
    out_shape += [jax.ShapeDtypeStruct((groups, n_chunks) + tuple(s), F32) for s in couts]
    out_shape += [jax.ShapeDtypeStruct((groups, r, c), F32) for r, c in accs]
    x_widths = [w for _, w in xs]
    y_widths = [w for w, _ in ys]

    def body(*refs):
        x_refs = refs[:n_x]
        ci_refs = refs[n_x:n_x + n_ci]
        p_refs = refs[n_x + n_ci:n_x + n_ci + n_p]
        c_refs = refs[n_x + n_ci + n_p:n_x + n_ci + n_p + n_c]
        o = n_x + n_ci + n_p + n_c
        y_refs = refs[o:o + n_y]
        co_refs = refs[o + n_y:o + n_y + n_co]
        a_refs = refs[o + n_y + n_co:o + n_y + n_co + n_a]
        carry_refs = refs[o + n_y + n_co + n_a:]

        @pl.when(pl.program_id(1) == 0)
        def _():
            for s in carry_refs:
                s[...] = jnp.zeros(s.shape, F32)
            for a in a_refs:
                a[...] = jnp.zeros(a.shape, F32)

        cvals = [c[...] for c in c_refs]

        def one_chunk(i, _):
            c = (nc - 1 - i) if reverse else i
            r0 = c * chunk if isinstance(c, int) else pl.multiple_of(c * chunk, chunk)
            loaded = []
            for u in range(gpb):
                carry = [s[u] for s in carry_refs]
                xv = [x[pl.ds(r0, chunk), u * w:(u + 1) * w] for x, w in zip(x_refs, x_widths)]
                civ = [ci[u, c] for ci in ci_refs]
                loaded.append((carry, xv, civ, [p[u] for p in p_refs]))
            results = step(loaded, cvals) if multi else [step(*args, cvals) for args in loaded]
            for u, (new_carry, yv, cov, av) in enumerate(results):
                for s, val in zip(carry_refs, new_carry):
                    s[u] = val
                for y, w, val in zip(y_refs, y_widths, yv):
                    y[pl.ds(r0, chunk), u * w:(u + 1) * w] = val.astype(y.dtype)
                for co, val in zip(co_refs, cov):
                    co[u, c] = val
                for a, val in zip(a_refs, av):
                    a[u] += val
            return 0

        if nc == 1:
            one_chunk(0, 0)
        else:
            lax.fori_loop(0, nc, one_chunk, 0)

    outs = pl.pallas_call(
        body, name=name, grid=(groups // gpb, nb), in_specs=in_specs, out_specs=out_specs, out_shape=out_shape,
        scratch_shapes=[pltpu.VMEM((gpb,) + tuple(s), F32) for s in carries],
        compiler_params=pltpu.CompilerParams(dimension_semantics=("arbitrary", "arbitrary"),
                                             vmem_limit_bytes=VMEM_LIMIT),
    )(*[a for a, _ in xs], *cins, *params, *consts)
    return outs[:n_y], outs[n_y:n_y + n_co], outs[n_y + n_co:]


def _run_interleaved(step, arg_tuples):
    runs = [step(*args) for args in arg_tuples]
    if not hasattr(runs[0], "send"):
        return runs
    results, live = [None] * len(runs), list(range(len(runs)))
    while live:
        for u in list(live):
            try:
                next(runs[u])
            except StopIteration as done:
                results[u] = done.value
                live.remove(u)
    return results


def _stage_fwd(step, *, name, rows, chunk, nc, groups, xs, params, consts, carries, ys, gpb=1):
    def fstep(loaded, cv):
        outs = _run_interleaved(step, [(carry, xv, pv, cv) for carry, xv, _, pv in loaded])
        return [(new_carry, yv, carry, []) for (new_carry, yv), (carry, _, _, _) in zip(outs, loaded)]

    yv, saved, _ = _scan_call(fstep, name=name, rows=rows, chunk=chunk, nc=nc, groups=groups, xs=xs, params=params,
                              consts=consts, carries=carries, ys=ys, couts=carries, gpb=gpb, multi=True)
    return yv, saved


def _stage_bwd(step, *, name, rows, chunk, nc, groups, xs, saved, params, consts, carries, dys, dxs, gpb=1):
    n_x = len(xs)

    def bstep(loaded, cv):
        civs = [list(civ) for _, _, civ, _ in loaded]
        xvs = [list(xv_all[:n_x]) for _, xv_all, _, _ in loaded]
        pvs = [list(pv) for _, _, _, pv in loaded]
        cts = [(list(dcarry), [d.astype(F32) for d in xv_all[n_x:]]) for dcarry, xv_all, _, _ in loaded]

        def fwd(civs_, xvs_, pvs_):
            outs = _run_interleaved(step, [(c_, x_, p_, cv) for c_, x_, p_ in zip(civs_, xvs_, pvs_)])
            return [(list(new_carry), list(yv)) for new_carry, yv in outs]

        _, vjp = jax.vjp(fwd, civs, xvs, pvs)
        dcivs, dxvs, dpvs = vjp(cts)
        return [(dc, dx, [], dp) for dc, dx, dp in zip(dcivs, dxvs, dpvs)]

    dxv, _, dpv = _scan_call(bstep, name=name, rows=rows, chunk=chunk, nc=nc, groups=groups, xs=list(xs) + list(dys),
                             cins=saved, params=params, consts=consts, carries=carries,
                             ys=[(w, dt) for (_, w), dt in zip(xs, dxs)], accs=[a.shape[1:] for a in params],
                             reverse=True, gpb=gpb, multi=True)
    return dxv, dpv


def _mm_params(sem):
    return pltpu.CompilerParams(dimension_semantics=sem, vmem_limit_bytes=VMEM_LIMIT)


def _after(dep):
    return ([], []) if dep is None else ([dep], [pl.BlockSpec(memory_space=pl.ANY)])


def _matmul_nt(a, b, *, name, dep=None):
    m, k = a.shape
    n = b.shape[0]
    tm = min(1024, m)
    tn = 1024 if n % 1024 == 0 else 1408 if n % 1408 == 0 else 512
    deps, dep_specs = _after(dep)

    def body(a_ref, b_ref, *rest):
        rest[-1][...] = lax.dot_general(a_ref[...], b_ref[...], (_DIMS["nt"], ((), ())), preferred_element_type=F32)

    return pl.pallas_call(
        body, name=name, grid=(m // tm, n // tn),
        in_specs=[pl.BlockSpec((tm, k), lambda i, j: (i, 0)), pl.BlockSpec((tn, k), lambda i, j: (j, 0))] + dep_specs,
        out_specs=pl.BlockSpec((tm, tn), lambda i, j: (i, j)),
        out_shape=jax.ShapeDtypeStruct((m, n), F32),
        compiler_params=_mm_params(("parallel", "arbitrary")),
    )(a, b, *deps)


def _matmul_nn(a, b, *, name, dep=None):
    m, k = a.shape
    n = b.shape[1]
    deps, dep_specs = _after(dep)
    if k > 6144:
        tm, tk, steps = min(512, m), k // 4, 4

        def body_k(a_ref, b_ref, *rest):
            part = jnp.dot(a_ref[...], b_ref[...], preferred_element_type=F32)

            @pl.when(pl.program_id(1) == 0)
            def _():
                rest[-1][...] = part

            @pl.when(pl.program_id(1) != 0)
            def _():
                rest[-1][...] += part

        return pl.pallas_call(
            body_k, name=name, grid=(m // tm, steps),
            in_specs=[pl.BlockSpec((tm, tk), lambda i, j: (i, j)), pl.BlockSpec((tk, n), lambda i, j: (j, 0))] + dep_specs,
            out_specs=pl.BlockSpec((tm, n), lambda i, j: (i, 0)),
            out_shape=jax.ShapeDtypeStruct((m, n), F32),
            compiler_params=_mm_params(("parallel", "arbitrary")),
        )(a, b, *deps)
    tm, tn = (1024, 1024) if k <= 4096 else (1024, 512)
    tm = min(tm, m)

    def body(a_ref, b_ref, *rest):
        rest[-1][...] = jnp.dot(a_ref[...], b_ref[...], preferred_element_type=F32)

    return pl.pallas_call(
        body, name=name, grid=(m // tm, n // tn),
        in_specs=[pl.BlockSpec((tm, k), lambda i, j: (i, 0)), pl.BlockSpec((k, tn), lambda i, j: (0, j))] + dep_specs,
        out_specs=pl.BlockSpec((tm, tn), lambda i, j: (i, j)),
        out_shape=jax.ShapeDtypeStruct((m, n), F32),
        compiler_params=_mm_params(("parallel", "arbitrary")),
    )(a, b, *deps)


def _matmul_tn(x, y, *, name, tp=512, tq=512):
    t, p = x.shape
    q = y.shape[1]

    def body(x_ref, y_ref, o_ref):
        o_ref[...] = lax.dot_general(x_ref[...], y_ref[...], (_DIMS["tn"], ((), ())),
                                     preferred_element_type=F32).astype(o_ref.dtype)

    return pl.pallas_call(
        body, name=name, grid=(p // tp, q // tq),
        in_specs=[pl.BlockSpec((t, tp), lambda i, j: (0, i)), pl.BlockSpec((t, tq), lambda i, j: (0, j))],
        out_specs=pl.BlockSpec((tp, tq), lambda i, j: (i, j)),
        out_shape=jax.ShapeDtypeStruct((p, q), BF16),
        compiler_params=_mm_params(("parallel", "arbitrary")),
    )(x, y)


N_CHIPS = N_DEV // 2


def _all_gather(arrays, *, name):
    n = len(arrays)
    out_shape = [jax.ShapeDtypeStruct((N_DEV,) + tuple(a.shape), a.dtype) for a in arrays]

    def body(*refs):
        in_refs, out_refs = refs[:n], refs[n:2 * n]
        send_sems, recv_sems, local_sems = refs[2 * n:]
        x, y, c = lax.axis_index("x"), lax.axis_index("y"), lax.axis_index("c")
        me, sibling = (x, y, c), (x, y, 1 - c)
        chips = [(1 - x, y), (x, 1 - y), (1 - x, 1 - y)]
        south = c == 0
        relay_to = (jnp.where(south, x, 1 - x), jnp.where(south, 1 - y, y), c)
        relayed = (jnp.where(south, 1 - x, x), jnp.where(south, y, 1 - y), c)

        def copy(a, k, block, to, src=None):
            slot = out_refs[a].at[4 * block[0] + 2 * block[1] + block[2]]
            return pltpu.make_async_remote_copy(
                src_ref=slot if src is None else src, dst_ref=slot, send_sem=send_sems.at[a, k],
                recv_sem=recv_sems.at[a, k], device_id=to, device_id_type=pl.DeviceIdType.MESH)

        mine = [pltpu.make_async_copy(in_refs[a], out_refs[a].at[4 * x + 2 * y + c], local_sems.at[a]) for a in range(n)]
        first = []
        for a in range(n):
            first.append(copy(a, 0, me, sibling, src=in_refs[a]))
            first += [copy(a, 1 + j, me, (*chip, c), src=in_refs[a]) for j, chip in enumerate(chips[:2])]
        for cp in mine + first:
            cp.start()
        passed = []
        for j, chip in enumerate(chips[:2]):
            for a in range(n):
                copy(a, 1 + j, (*chip, c), me).wait_recv()
                passed.append(copy(a, 4 + j, (*chip, c), sibling))
                passed[-1].start()
        for a in range(n):
            passed.append(copy(a, 3, relayed, relay_to))
            passed[-1].start()
        for a in range(n):
            copy(a, 3, (*chips[2], c), me).wait_recv()
            passed.append(copy(a, 6, (*chips[2], c), sibling))
            passed[-1].start()
        for a in range(n):
            copy(a, 0, sibling, me).wait_recv()
            for j, chip in enumerate(chips):
                copy(a, 4 + j, (*chip, 1 - c), me).wait_recv()
        for cp in first + passed:
            cp.wait_send()
        for cp in mine:
            cp.wait()

    any_spec = pl.BlockSpec(memory_space=pl.ANY)
    return pl.pallas_call(
        body, name=name, in_specs=[any_spec] * n, out_specs=[any_spec] * n, out_shape=out_shape,
        scratch_shapes=[pltpu.SemaphoreType.DMA((n, N_DEV - 1)), pltpu.SemaphoreType.DMA((n, N_DEV - 1)),
                        pltpu.SemaphoreType.DMA((n,))],
        compiler_params=pltpu.CompilerParams(has_side_effects=True),
    )(*arrays)


def _push(arrays, *, name, plan, out_slots=None):
    n = len(arrays)
    in_place = out_slots is None
    out_shape = [jax.ShapeDtypeStruct(((a.shape[0] if in_place else out_slots),) + tuple(a.shape[1:]), a.dtype) for a in arrays]
    n_tr = len(plan(0, 0, 0)[0])

    def body(*refs):
        in_refs, out_refs = refs[:n], refs[n:2 * n]
        send_sems, recv_sems, local_sems = refs[2 * n:]
        src_refs = out_refs if in_place else in_refs
        transfers, local = plan(lax.axis_index("x"), lax.axis_index("y"), lax.axis_index("c"))
        copies = []
        for a in range(n):
            if local is not None:
                copies.append(pltpu.make_async_copy(src_refs[a].at[local[0]], out_refs[a].at[local[1]], local_sems.at[a]))
            for k, (peer, src, dst) in enumerate(transfers):
                copies.append(pltpu.make_async_remote_copy(
                    src_ref=src_refs[a].at[src], dst_ref=out_refs[a].at[dst], send_sem=send_sems.at[a, k],
                    recv_sem=recv_sems.at[a, k], device_id=peer, device_id_type=pl.DeviceIdType.MESH))
        for cp in copies:
            cp.start()
        for cp in copies:
            cp.wait()

    any_spec = pl.BlockSpec(memory_space=pl.ANY)
    return pl.pallas_call(
        body, name=name, in_specs=[any_spec] * n, out_specs=[any_spec] * n, out_shape=out_shape,
        input_output_aliases={a: a for a in range(n)} if in_place else {},
        scratch_shapes=[pltpu.SemaphoreType.DMA((n, n_tr)), pltpu.SemaphoreType.DMA((n, n_tr)),
                        pltpu.SemaphoreType.DMA((n,))],
        compiler_params=pltpu.CompilerParams(has_side_effects=True),
    )(*arrays)


_HBM_SPEC = pl.BlockSpec(memory_space=pltpu.HBM)
_SEM_SPEC = pl.BlockSpec(memory_space=pltpu.SEMAPHORE)
_DATAFLOW = pltpu.SideEffectType.DATAFLOW_SIDE_EFFECTING


def _push_start(sources, landing, *, name, plan, after=None):
    n = len(sources)
    n_tr = len(plan(0, 0, 0)[0])
    deps, dep_specs = _after(after)

    def body(*refs):
        src_refs, land_refs = refs[:n], refs[n:2 * n]
        o = 2 * n + len(deps)
        send_sems, recv_sems, token = refs[o], refs[o + 1], refs[-1]
        transfers, _ = plan(lax.axis_index("x"), lax.axis_index("y"), lax.axis_index("c"))
        for a in range(n):
            for k, (peer, src, dst) in enumerate(transfers):
                pltpu.make_async_remote_copy(
                    src_ref=src_refs[a].at[src], dst_ref=land_refs[a].at[dst], send_sem=send_sems.at[a * n_tr + k],
                    recv_sem=recv_sems.at[a * n_tr + k], device_id=peer, device_id_type=pl.DeviceIdType.MESH).start()
        token[...] = jnp.zeros(token.shape, token.dtype)

    hbm = lambda a: pltpu.HBM(a.shape, a.dtype)
    outs = pl.pallas_call(
        body, name=name,
        out_shape=(pltpu.SemaphoreType.DMA((n * n_tr,)), pltpu.SemaphoreType.DMA((n * n_tr,)), *[hbm(a) for a in sources],
                   *[hbm(a) for a in landing], jax.ShapeDtypeStruct((8, 128), F32)),
        in_specs=[_HBM_SPEC] * (2 * n) + dep_specs,
        out_specs=(_SEM_SPEC, _SEM_SPEC, *[_HBM_SPEC] * (2 * n), pl.BlockSpec(memory_space=pltpu.VMEM)),
        input_output_aliases={i: 2 + i for i in range(2 * n)},
        compiler_params=pltpu.CompilerParams(has_side_effects=_DATAFLOW),
    )(*[pltpu.with_memory_space_constraint(a, pltpu.HBM) for a in list(sources) + list(landing)], *deps)
    return outs[0], outs[1], list(outs[2:2 + n]), list(outs[2 + n:2 + 2 * n]), outs[-1]


def _push_wait(handles, after, *, name, plan):
    send_sems, recv_sems, sources, landing, _ = handles
    n = len(sources)
    after = list(after) if isinstance(after, (list, tuple)) else [after]

    def body(*refs):
        src_refs, land_refs = refs[:n], refs[n:2 * n]
        send_sems_, recv_sems_ = refs[2 * n], refs[2 * n + 1]
        transfers, _ = plan(lax.axis_index("x"), lax.axis_index("y"), lax.axis_index("c"))
        n_tr = len(transfers)
        for a in range(n):
            for k, (peer, src, dst) in enumerate(transfers):
                cp = pltpu.make_async_remote_copy(
                    src_ref=src_refs[a].at[src], dst_ref=land_refs[a].at[dst], send_sem=send_sems_.at[a * n_tr + k],
                    recv_sem=recv_sems_.at[a * n_tr + k], device_id=peer, device_id_type=pl.DeviceIdType.MESH)
                cp.wait_send()
                cp.wait_recv()

    hbm = lambda a: pltpu.HBM(a.shape, a.dtype)
    outs = pl.pallas_call(
        body, name=name, out_shape=tuple(hbm(a) for a in list(sources) + list(landing)),
        in_specs=[_HBM_SPEC] * (2 * n) + [_SEM_SPEC, _SEM_SPEC] + [pl.BlockSpec(memory_space=pl.ANY)] * len(after),
        out_specs=[_HBM_SPEC] * (2 * n), input_output_aliases={i: i for i in range(2 * n)},
        compiler_params=pltpu.CompilerParams(has_side_effects=_DATAFLOW),
    )(*sources, *landing, send_sems, recv_sems, *after)
    return list(outs[n:])


def _plan_everyone(x, y, c):
    me = 4 * x + 2 * y + c
    peers = [(1 - x if k & 4 else x, 1 - y if k & 2 else y, 1 - c if k & 1 else c) for k in range(1, N_DEV)]
    return [(p, 0, me) for p in peers], (0, me)


def _plan_owners(x, y, c):
    me = 4 * x + 2 * y + c
    peers = [(1 - x if k & 4 else x, 1 - y if k & 2 else y, 1 - c if k & 1 else c) for k in range(1, N_DEV)]
    return [((px, py, pc), 4 * px + 2 * py + pc, me) for px, py, pc in peers], None


def _plan_sibling(x, y, c):
    return [((x, y, 1 - c), 2 * chip + (1 - c), chip) for chip in range(N_CHIPS)], None


def _plan_chips(x, y, c):
    mine = 2 * x + y
    peers = [(1 - x, y), (x, 1 - y), (1 - x, 1 - y)]
    return [((px, py, c), 2 * px + py, mine) for px, py in peers], (mine, mine)


def _plan_own_block(x, y, c):
    me = 4 * x + 2 * y + c
    peers = [(x, y, 1 - c), (1 - x, y, c), (x, 1 - y, c), (1 - x, 1 - y, c)]
    return [(p, 0, me) for p in peers], None


def _plan_pass_on(x, y, c):
    slots = [4 * px + 2 * py + c for px, py in ((1 - x, y), (x, 1 - y), (1 - x, 1 - y))]
    return [((x, y, 1 - c), s, s) for s in slots], None


def _pair_sum(parts, received, *, name, tc):
    _, r, c = parts.shape
    core = lax.axis_index("c").astype(jnp.int32).reshape(1)

    def body(core_ref, p_ref, r_ref, o_ref, o2_ref):
        s = (p_ref[...].astype(F32) + r_ref[...].astype(F32)).astype(o_ref.dtype)
        o_ref[...] = s
        o2_ref[...] = s

    out = pl.BlockSpec((None, r, tc), lambda i, j, core_ref: (i, 0, j))
    return pl.pallas_call(
        body, name=name,
        grid_spec=pltpu.PrefetchScalarGridSpec(
            num_scalar_prefetch=1, grid=(N_CHIPS, c // tc),
            in_specs=[pl.BlockSpec((None, r, tc), lambda i, j, core_ref: (2 * i + core_ref[0], 0, j)),
                      pl.BlockSpec((None, r, tc), lambda i, j, core_ref: (i, 0, j))],
            out_specs=[out, out]),
        out_shape=[jax.ShapeDtypeStruct((N_CHIPS, r, c), BF16)] * 2,
        compiler_params=pltpu.CompilerParams(dimension_semantics=("parallel", "parallel"), vmem_limit_bytes=VMEM_LIMIT),
    )(core, parts, received)


def _sum_blocks(a, *, name, tc):
    nblk, r, c = a.shape

    def body(a_ref, o_ref):
        acc = a_ref[0].astype(F32)
        for i in range(1, nblk):
            acc = acc + a_ref[i].astype(F32)
        o_ref[...] = acc

    return pl.pallas_call(
        body, name=name, grid=(c // tc,),
        in_specs=[pl.BlockSpec((nblk, r, tc), lambda j: (0, 0, j))],
        out_specs=pl.BlockSpec((r, tc), lambda j: (0, j)),
        out_shape=jax.ShapeDtypeStruct((r, c), F32),
        compiler_params=pltpu.CompilerParams(dimension_semantics=("parallel",), vmem_limit_bytes=VMEM_LIMIT),
    )(a)


def _adamw(w, g, m, v, *, name, tr):
    r, c = w.shape

    def body(w_ref, g_ref, m_ref, v_ref, d_ref, mo_ref, vo_ref):
        gv = g_ref[...]
        mn = ADAM_B1 * m_ref[...] + (1.0 - ADAM_B1) * gv
        vn = ADAM_B2 * v_ref[...] + (1.0 - ADAM_B2) * jnp.square(gv)
        m_hat = mn / (1.0 - ADAM_B1 ** ADAM_STEP)
        v_hat = vn / (1.0 - ADAM_B2 ** ADAM_STEP)
        d_ref[...] = -ADAM_LR * (m_hat / (jnp.sqrt(v_hat) + ADAM_EPS) + ADAM_WD * w_ref[...])
        mo_ref[...] = mn
        vo_ref[...] = vn

    spec = pl.BlockSpec((tr, c), lambda i: (i, 0))
    return pl.pallas_call(
        body, name=name, grid=(r // tr,), in_specs=[spec] * 4, out_specs=[spec] * 3,
        out_shape=[jax.ShapeDtypeStruct((r, c), F32)] * 3,
        compiler_params=pltpu.CompilerParams(dimension_semantics=("parallel",), vmem_limit_bytes=VMEM_LIMIT),
    )(w, g, m, v)


def _conv_layout():
    idx = []
    for g in range(SSM_GROUPS):
        idx.append(np.concatenate([g * SSM_GW + np.arange(SSM_GW),
                                   SSM_DINNER + g * SSM_DSTATE + np.arange(SSM_DSTATE),
                                   SSM_DINNER + SSM_GROUPS * SSM_DSTATE + g * SSM_DSTATE + np.arange(SSM_DSTATE)]))
    return np.stack(idx)


def _inverse(idx, n):
    inv = np.zeros(n, np.int64)
    pos = np.nonzero(idx >= 0)[0]
    inv[idx[pos]] = pos
    return inv


def _take_rows(a, idx, axis=0):
    idx = np.asarray(idx).reshape(-1)
    pieces, start = [], 0
    for i in range(1, len(idx) + 1):
        same_run = i < len(idx) and ((idx[i] == idx[i - 1] + 1 and idx[i - 1] >= 0) or (idx[i] < 0 and idx[i - 1] < 0))
        if same_run:
            continue
        n = i - start
        if idx[start] < 0:
            shape = list(a.shape)
            shape[axis] = n
            pieces.append(jnp.zeros(shape, a.dtype))
        else:
            pieces.append(lax.slice_in_dim(a, int(idx[start]), int(idx[start]) + n, axis=axis))
        start = i
    return pieces[0] if len(pieces) == 1 else jnp.concatenate(pieces, axis=axis)


def _copy_runs(sources, out_rows, runs, *, name, block, total_rows=None, into=None):
    d, dtype = sources[0].shape[1], sources[0].dtype
    outs = []
    base = 0 if into is None else into[1] // block
    extra, extra_specs = ([], []) if into is None else ([into[0]], [pl.BlockSpec(memory_space=pl.ANY)])
    for o, rows in enumerate(out_rows):
        mine = sorted({i for i, _, oo, _, _ in runs if oo == o})
        ns, nblk = len(mine), rows // block
        sel = np.zeros(nblk, np.int32)
        idx = np.full((ns, nblk), -1, np.int64)
        for i, s, oo, t, n in runs:
            if oo == o:
                assert s % block == 0 and t % block == 0 and n % block == 0
                for b in range(n // block):
                    sel[t // block + b] = mine.index(i)
                    idx[mine.index(i), t // block + b] = s // block + b
        assert (idx.max(axis=0) >= 0).all()
        for i in range(ns):
            first = idx[i, np.nonzero(idx[i] >= 0)[0][0]]
            for b in range(nblk):
                if idx[i, b] < 0:
                    idx[i, b] = idx[i, b - 1] if b > 0 else first

        def body(sel_ref, idx_ref, *refs, ns=ns):
            srcs, out = refs[:ns], refs[-1]
            which = sel_ref[pl.program_id(0)]
            val = srcs[ns - 1][...]
            for i in range(ns - 2, -1, -1):
                val = jnp.where(which == i, srcs[i][...], val)
            out[...] = val

        in_specs = [pl.BlockSpec((block, d), (lambda i_, n_: lambda b, sel_ref, idx_ref: (idx_ref[i_ * n_ + b], 0))(i, nblk))
                    for i in range(ns)]
        full_rows = into[0].shape[0] if into is not None else (total_rows or rows)
        outs.append(pl.pallas_call(
            body, name=f"{name}_{o}" if len(out_rows) > 1 else name,
            grid_spec=pltpu.PrefetchScalarGridSpec(
                num_scalar_prefetch=2, grid=(nblk,), in_specs=in_specs + extra_specs,
                out_specs=pl.BlockSpec((block, d), lambda b, sel_ref, idx_ref: (base + b, 0))),
            out_shape=jax.ShapeDtypeStruct((full_rows, d), dtype),
            input_output_aliases={} if into is None else {2 + ns: 0},
            compiler_params=pltpu.CompilerParams(dimension_semantics=("arbitrary",), vmem_limit_bytes=VMEM_LIMIT),
        )(jnp.asarray(sel), jnp.asarray(idx.reshape(-1), jnp.int32), *[sources[i] for i in mine], *extra))
    return outs


_Z0, _XBC0, _DT0, _GATE0 = 8192, 12288, 18432, 18496
_B0, _C0 = _XBC0 + SSM_DINNER, _XBC0 + SSM_DINNER + SSM_GROUPS * SSM_DSTATE


def _in_proj_runs():
    runs = [(part * 2048 + h * HG_DK, 0, h * HG_BLK + part * HG_DK, HG_DK) for h in range(HG_HEADS) for part in range(4)]
    for g in range(SSM_GROUPS):
        base = g * SSM_BLK
        runs += [(_XBC0 + g * SSM_GW, 1, base, SSM_GW), (_B0 + g * SSM_DSTATE, 1, base + SSM_GW, SSM_DSTATE),
                 (_C0 + g * SSM_DSTATE, 1, base + SSM_GW + SSM_DSTATE, SSM_DSTATE),
                 (_Z0 + g * SSM_GW, 1, base + SSM_XBC + 128, SSM_GW)]
    return runs + [(_GATE0, 2, 0, 2 * D_MODEL)]


def _in_proj_to_kernel(in_t):
    d = in_t.shape[1]
    dt = jnp.pad(in_t[_DT0:_GATE0].reshape(SSM_GROUPS, SSM_HPG, d), ((0, 0), (0, 128 - SSM_HPG), (0, 0)))
    runs = [(0, src, sec, dst, n) for src, sec, dst, n in _in_proj_runs() if sec < 2]
    runs += [(1, g * 128, 1, g * SSM_BLK + SSM_XBC, 128) for g in range(SSM_GROUPS)]
    hg, ssm = _copy_runs([in_t, dt.reshape(SSM_GROUPS * 128, d)], [_Z0, SSM_GROUPS * SSM_BLK], runs,
                         name="in_proj_to_kernel_layout", block=128)
    return hg, ssm, in_t[_GATE0:]


def _in_proj_from_kernel(hg, ssm, gate):
    d = hg.shape[1]
    dt = ssm.reshape(SSM_GROUPS, SSM_BLK, d)[:, SSM_XBC:SSM_XBC + SSM_HPG].reshape(SSM_HEADS, d)
    runs = [(sec, dst, 0, src, n) for src, sec, dst, n in _in_proj_runs() if sec < 2]
    main = _copy_runs([hg, ssm], [_DT0], runs, name="in_proj_to_global_layout", block=128, total_rows=IN_TOTAL)[0]
    tail = [(0, 0, 0, 0, SSM_HEADS), (1, 0, 0, SSM_HEADS, 2 * D_MODEL)]
    return _copy_runs([dt, gate], [IN_TOTAL - _DT0], tail, name="in_proj_to_global_layout_tail", block=SSM_HEADS,
                      into=(main, _DT0))[0]


def _up_to_kernel(up_t):
    runs = [(0, part * D_FF + g * FFN_GW, 0, (2 * g + part) * FFN_GW, FFN_GW) for g in range(FFN_G) for part in range(2)]
    return _copy_runs([up_t], [2 * D_FF], runs, name="up_to_kernel_layout", block=FFN_GW)[0]


def _up_from_kernel(up):
    runs = [(0, (2 * g + part) * FFN_GW, 0, part * D_FF + g * FFN_GW, FFN_GW) for g in range(FFN_G) for part in range(2)]
    return _copy_runs([up], [2 * D_FF], runs, name="up_to_global_layout", block=FFN_GW)[0]


_SMALL = (("mix_pre_norm", (1, 2048)), ("mix_post_norm", (1, 2048)), ("hg_lb_table", (2, 2048)), ("hg_out_norm", (1, 128)),
          ("ssm_conv_w", (4, 6144)), ("ssm_conv_b", (1, 6144)), ("ssm_dt_bias", (1, 64)), ("ssm_A_log", (1, 64)),
          ("ssm_D", (1, 64)), ("ssm_out_norm", (1, 4096)), ("ffn_pre_norm", (1, 2048)), ("ffn_post_norm", (1, 2048)),
          ("ffn_conv_w", (3, 5632)), ("ffn_conv_b", (1, 5632)), ("loss", (1, 1)))
_PACK_ROWS = 8 * (-(-sum(int(np.prod(s)) for _, s in _SMALL) // 1024))


def _pack(vals):
    flat = jnp.concatenate([vals[k].astype(F32).reshape(-1) for k, _ in _SMALL])
    return jnp.pad(flat, (0, _PACK_ROWS * 128 - flat.shape[0])).reshape(_PACK_ROWS, 128)


def _unpack(packed):
    flat, out, o = packed.reshape(-1), {}, 0
    for k, s in _SMALL:
        n = int(np.prod(s))
        out[k] = flat[o:o + n].reshape(s)
        o += n
    return out


def _local_step(x, target, w, p, late_weights=None, emit=lambda key, gw: None):
    t = x.shape[0]
    one = lambda a: a.reshape((1,) + a.shape)
    row = dict(rows=t, groups=1, consts=[], carries=[])

    (h1,), _ = _stage_fwd(_pre_step, name="pre_fwd", chunk=512, nc=1, xs=[(x, D_MODEL)], params=[one(p["mix_pre_norm"])],
                          ys=[(D_MODEL, BF16)], **row)
    proj_hg = _matmul_nt(h1, w["in_hg"], name="proj_hg")
    proj_ssm = _matmul_nt(h1, w["in_ssm"], name="proj_ssm")
    proj_gate = _matmul_nt(h1, w["in_gate"], name="proj_gate")

    hg = dict(rows=t, chunk=HG_CHUNK, nc=4, groups=HG_HEADS, xs=[(proj_hg, HG_BLK)], params=[p["hg_tab"], p["hg_nw"]],
              consts=_hg_consts(), carries=[(HG_DK, HG_DK)], gpb=HG_HEADS)
    (y_hg,), hg_saved = _stage_fwd(_hg_step, name="hg_fwd", ys=[(HG_DK, BF16)], **hg)

    ssd = dict(rows=t, chunk=SSM_CHUNK, nc=4, groups=SSM_GROUPS, xs=[(proj_ssm, SSM_BLK)],
               params=[p["conv_w"], p["conv_b"], p["dt_bias"], p["a_log"], p["d_skip"], p["ssm_nw"]],
               consts=_ssd_consts(), carries=[(4 * 128, SSM_DSTATE), (HALO, SSM_XBC)], gpb=2)
    (y_ssm,), ssd_saved = _stage_fwd(_ssd_step, name="ssd_fwd", ys=[(SSM_GW, BF16)], **ssd)

    if late_weights is not None:
        w = {**w, **late_weights([y_hg, y_ssm])}
    u_hg = _matmul_nn(y_hg, w["branch_hg"], name="branch_hg")
    u_ssm = _matmul_nn(y_ssm, w["branch_ssm"], name="branch_ssm")
    mix = dict(chunk=256, nc=1, xs=[(proj_gate, 2 * D_MODEL), (u_hg, D_MODEL), (u_ssm, D_MODEL)], params=[], **row)
    (mixed,), _ = _stage_fwd(_mix_step, name="mix_fwd", ys=[(D_MODEL, BF16)], **mix)
    v = _matmul_nn(mixed, w["out"], name="out_proj")
    post = dict(chunk=256, nc=1, xs=[(x, D_MODEL), (v, D_MODEL)],
                params=[one(p["mix_post_norm"]), one(p["ffn_pre_norm"])], **row)
    (x1, h2), _ = _stage_fwd(_post_step, name="post_fwd", ys=[(D_MODEL, F32), (D_MODEL, BF16)], **post)
    gu = _matmul_nt(h2, w["up"], name="ffn_up")
    ffn = dict(rows=t, chunk=256, nc=2, groups=FFN_G, xs=[(gu, 2 * FFN_GW)], params=[p["ffn_conv_w"], p["ffn_conv_b"]],
               consts=[], carries=[(HALO, FFN_GW)])
    (act,), ffn_saved = _stage_fwd(_ffn_step, name="ffn_fwd", ys=[(FFN_GW, BF16)], **ffn)
    d = _matmul_nn(act, w["down"], name="ffn_down")

    def head_step(carry, xv, civ, pv, cv):
        x1_, d_, tgt = xv

        def per_row_loss(a, b, nw):
            e = a + _rms(b, nw) - tgt
            return 0.5 * jnp.mean(e * e, axis=1, keepdims=True)

        lrow, vjp = jax.vjp(per_row_loss, x1_, d_, pv[0])
        dx1_, dd_, dnw = vjp(jnp.ones_like(lrow))
        loss = jnp.broadcast_to(jnp.sum(lrow, axis=0, keepdims=True), (1, 128))
        return [], [dx1_, dd_], [], [dnw, loss]

    (dy, dd), _, (g_ffn_post, loss) = _scan_call(
        head_step, name="loss_head", chunk=256, nc=1, xs=[(x1, D_MODEL), (d, D_MODEL), (target, D_MODEL)],
        params=[one(p["ffn_post_norm"])], ys=[(D_MODEL, F32), (D_MODEL, BF16)], accs=[(1, D_MODEL), (1, 128)], **row)

    gw = {}
    gw["down"] = _matmul_tn(act, dd, name="g_down")
    dact = _matmul_nt(dd, w["down"], name="d_act", dep=emit("down", gw))
    (dgu,), (g_fcw, g_fcb) = _stage_bwd(_ffn_step, name="ffn_bwd", saved=ffn_saved, dys=[(dact, FFN_GW)], dxs=[BF16], **ffn)
    gw["up"] = _matmul_tn(dgu, h2, name="g_up")
    dh2 = _matmul_nn(dgu, w["up"], name="d_h2", dep=emit("up", gw))
    (dx1, dv), (g_mix_post, g_ffn_pre) = _stage_bwd(_post_step, name="post_bwd", saved=[], dys=[(dy, D_MODEL), (dh2, D_MODEL)],
                                                    dxs=[F32, BF16], **post)
    gw["out"] = _matmul_tn(mixed, dv, name="g_out")
    dmixed = _matmul_nt(dv, w["out"], name="d_mixed")
    (dgate, du_hg, du_ssm), _ = _stage_bwd(_mix_step, name="mix_bwd", saved=[], dys=[(dmixed, D_MODEL)],
                                           dxs=[BF16, BF16, BF16], **mix)
    gw["in_gate"] = _matmul_tn(dgate, h1, name="g_in_gate")
    gw["branch_hg"] = _matmul_tn(y_hg, du_hg, name="g_branch_hg")
    gw["branch_ssm"] = _matmul_tn(y_ssm, du_ssm, name="g_branch_ssm")
    dy_hg = _matmul_nt(du_hg, w["branch_hg"], name="d_y_hg", dep=emit("branches", gw))
    dy_ssm = _matmul_nt(du_ssm, w["branch_ssm"], name="d_y_ssm")
    (dproj_ssm,), g_ssd = _stage_bwd(_ssd_step, name="ssd_bwd", saved=ssd_saved, dys=[(dy_ssm, SSM_GW)], dxs=[BF16], **ssd)
    gw["in_ssm"] = _matmul_tn(dproj_ssm, h1, name="g_in_ssm")
    (dproj_hg,), (g_tab, g_hg_nw) = _stage_bwd(_hg_step, name="hg_bwd", saved=hg_saved, dys=[(dy_hg, HG_DK)], dxs=[BF16],
                                               **{**hg, "nc": 2})
    gw["in_hg"] = _matmul_tn(dproj_hg, h1, name="g_in_hg")
    dh_a = _matmul_nn(dproj_hg, w["in_hg"], name="d_h1_hg", dep=emit("in", gw))
    dh_b = _matmul_nn(dproj_ssm, w["in_ssm"], name="d_h1_ssm")
    dh_c = _matmul_nn(dgate, w["in_gate"], name="d_h1_gate")

    def pre_bwd_step(carry, xv, civ, pv, cv):
        x_, da, db, dc, dres = xv
        _, vjp = jax.vjp(_rms, x_, pv[0])
        dx_, dnw = vjp(da + db + dc)
        return [], [dx_ + dres], [], [dnw]

    (grad_x,), _, (g_mix_pre,) = _scan_call(
        pre_bwd_step, name="pre_bwd", chunk=256, nc=1,
        xs=[(x, D_MODEL), (dh_a, D_MODEL), (dh_b, D_MODEL), (dh_c, D_MODEL), (dx1, D_MODEL)],
        params=[one(p["mix_pre_norm"])], ys=[(D_MODEL, F32)], accs=[(1, D_MODEL)], **row)

    gp = dict(mix_pre_norm=g_mix_pre[0], mix_post_norm=g_mix_post[0], ffn_pre_norm=g_ffn_pre[0], ffn_post_norm=g_ffn_post[0],
              hg_tab=g_tab, hg_nw=g_hg_nw, conv_w=g_ssd[0], conv_b=g_ssd[1], dt_bias=g_ssd[2], a_log=g_ssd[3],
              d_skip=g_ssd[4], ssm_nw=g_ssd[5], ffn_conv_w=g_fcw, ffn_conv_b=g_fcb, loss=loss[0, :, :1])
    return grad_x, gw, gp


def _small_to_kernel_layout(s):
    conv_idx = _conv_layout()
    pad_heads = lambda a: jnp.pad(a.reshape(SSM_GROUPS, 1, SSM_HPG), ((0, 0), (0, 0), (0, 128 - SSM_HPG)))
    return dict(
        mix_pre_norm=s["mix_pre_norm"], mix_post_norm=s["mix_post_norm"], ffn_pre_norm=s["ffn_pre_norm"],
        ffn_post_norm=s["ffn_post_norm"],
        hg_tab=s["hg_lb_table"].reshape(2, HG_HEADS, HG_DK).transpose(1, 0, 2),
        hg_nw=jnp.broadcast_to(s["hg_out_norm"].reshape(1, 1, HG_DK), (HG_HEADS, 1, HG_DK)),
        conv_w=_take_rows(s["ssm_conv_w"], conv_idx, axis=1).reshape(SSM_CONV, SSM_GROUPS, SSM_XBC).transpose(1, 0, 2),
        conv_b=_take_rows(s["ssm_conv_b"], conv_idx, axis=1).reshape(SSM_GROUPS, 1, SSM_XBC),
        dt_bias=pad_heads(s["ssm_dt_bias"]), a_log=pad_heads(s["ssm_A_log"]),
        d_skip=jnp.repeat(s["ssm_D"].reshape(SSM_HEADS), SSM_HEADDIM).reshape(SSM_GROUPS, 1, SSM_GW),
        ssm_nw=s["ssm_out_norm"].reshape(SSM_GROUPS, 1, SSM_GW),
        ffn_conv_w=s["ffn_conv_w"].reshape(FFN_CONV, FFN_G, FFN_GW).transpose(1, 0, 2),
        ffn_conv_b=s["ffn_conv_b"].reshape(FFN_G, 1, FFN_GW),
    )


def _small_from_kernel_layout(g):
    conv_inv = _inverse(_conv_layout().reshape(-1), SSM_CONV_DIM)
    heads = lambda a: a[:, 0, :SSM_HPG].reshape(1, SSM_HEADS)
    return dict(
        mix_pre_norm=g["mix_pre_norm"], mix_post_norm=g["mix_post_norm"], ffn_pre_norm=g["ffn_pre_norm"],
        ffn_post_norm=g["ffn_post_norm"],
        hg_lb_table=g["hg_tab"].transpose(1, 0, 2).reshape(2, HG_HEADS * HG_DK),
        hg_out_norm=jnp.sum(g["hg_nw"], axis=0),
        ssm_conv_w=_take_rows(g["conv_w"].transpose(1, 0, 2).reshape(SSM_CONV, -1), conv_inv, axis=1),
        ssm_conv_b=_take_rows(g["conv_b"].reshape(1, -1), conv_inv, axis=1),
        ssm_dt_bias=heads(g["dt_bias"]), ssm_A_log=heads(g["a_log"]),
        ssm_D=jnp.sum(g["d_skip"].reshape(SSM_HEADS, SSM_HEADDIM), axis=1).reshape(1, SSM_HEADS),
        ssm_out_norm=g["ssm_nw"].reshape(1, SSM_DINNER),
        ffn_conv_w=g["ffn_conv_w"].transpose(1, 0, 2).reshape(FFN_CONV, D_FF),
        ffn_conv_b=g["ffn_conv_b"].reshape(1, D_FF),
        loss=g["loss"],
    )


def kernel(x, w_in, mix_pre_norm, mix_post_norm, hg_lb_table, hg_out_norm, ssm_conv_w, ssm_conv_b, ssm_dt_bias, ssm_A_log, ssm_D, ssm_out_norm, w_branch_hg, w_branch_ssm, w_out, ffn_pre_norm, ffn_post_norm, ffn_w_up, ffn_conv_w, ffn_conv_b, ffn_w_down, loss_target, m_w_in, m_mix_pre_norm, m_mix_post_norm, m_hg_lb_table, m_hg_out_norm, m_ssm_conv_w, m_ssm_conv_b, m_ssm_dt_bias, m_ssm_A_log, m_ssm_D, m_ssm_out_norm, m_w_branch_hg, m_w_branch_ssm, m_w_out, m_ffn_pre_norm, m_ffn_post_norm, m_ffn_w_up, m_ffn_conv_w, m_ffn_conv_b, m_ffn_w_down, v_w_in, v_mix_pre_norm, v_mix_post_norm, v_hg_lb_table, v_hg_out_norm, v_ssm_conv_w, v_ssm_conv_b, v_ssm_dt_bias, v_ssm_A_log, v_ssm_D, v_ssm_out_norm, v_w_branch_hg, v_w_branch_ssm, v_w_out, v_ffn_pre_norm, v_ffn_post_norm, v_ffn_w_up, v_ffn_conv_w, v_ffn_conv_b, v_ffn_w_down):
    names = ["w_in", "mix_pre_norm", "mix_post_norm", "hg_lb_table", "hg_out_norm", "ssm_conv_w", "ssm_conv_b", "ssm_dt_bias",
             "ssm_A_log", "ssm_D", "ssm_out_norm", "w_branch_hg", "w_branch_ssm", "w_out", "ffn_pre_norm", "ffn_post_norm",
             "ffn_w_up", "ffn_conv_w", "ffn_conv_b", "ffn_w_down"]
    ws = dict(zip(names, (w_in, mix_pre_norm, mix_post_norm, hg_lb_table, hg_out_norm, ssm_conv_w, ssm_conv_b, ssm_dt_bias,
                          ssm_A_log, ssm_D, ssm_out_norm, w_branch_hg, w_branch_ssm, w_out, ffn_pre_norm, ffn_post_norm,
                          ffn_w_up, ffn_conv_w, ffn_conv_b, ffn_w_down)))
    ms = dict(zip(names, (m_w_in, m_mix_pre_norm, m_mix_post_norm, m_hg_lb_table, m_hg_out_norm, m_ssm_conv_w, m_ssm_conv_b,
                          m_ssm_dt_bias, m_ssm_A_log, m_ssm_D, m_ssm_out_norm, m_w_branch_hg, m_w_branch_ssm, m_w_out,
                          m_ffn_pre_norm, m_ffn_post_norm, m_ffn_w_up, m_ffn_conv_w, m_ffn_conv_b, m_ffn_w_down)))
    vs = dict(zip(names, (v_w_in, v_mix_pre_norm, v_mix_post_norm, v_hg_lb_table, v_hg_out_norm, v_ssm_conv_w, v_ssm_conv_b,
                          v_ssm_dt_bias, v_ssm_A_log, v_ssm_D, v_ssm_out_norm, v_w_branch_hg, v_w_branch_ssm, v_w_out,
                          v_ffn_pre_norm, v_ffn_post_norm, v_ffn_w_up, v_ffn_conv_w, v_ffn_conv_b, v_ffn_w_down)))
    me = 4 * lax.axis_index("x") + 2 * lax.axis_index("y") + lax.axis_index("c")

    late_shards = [ffn_w_up[0].T.astype(BF16), w_branch_hg[0].astype(BF16), w_branch_ssm[0].astype(BF16),
                   w_out[0].astype(BF16), ffn_w_down[0].astype(BF16)]
    landing = [lax.dynamic_update_slice_in_dim(lax.empty((N_DEV,) + s.shape, s.dtype), s[None], me, axis=0)
               for s in late_shards]
    gathered = _all_gather([w_in[0].T.astype(BF16), ssm_conv_w[0], ffn_conv_w[0]], name="gather_in_proj")
    late = _push_start([s[None] for s in late_shards], landing, name="late_weights_start", plan=_plan_own_block,
                       after=gathered[1])
    in_hg, in_ssm, in_gate = _in_proj_to_kernel(gathered[0].reshape(IN_TOTAL, D_MODEL))
    w = dict(in_hg=in_hg, in_ssm=in_ssm, in_gate=in_gate)
    small = {k: ws[k] for k, _ in _SMALL[:-1]}
    small["mix_pre_norm"] = mix_pre_norm + late[4][0, 0]
    small["ssm_conv_w"] = gathered[1].transpose(1, 0, 2).reshape(SSM_CONV, SSM_CONV_DIM)
    small["ffn_conv_w"] = gathered[2].transpose(1, 0, 2).reshape(FFN_CONV, D_FF)
    small = {k: small[k].reshape(s) for k, s in _SMALL[:-1]}

    def late_weights(after):
        landed = _push_wait(late, after, name="late_weights_wait", plan=_plan_own_block)
        up_all, bhg, bssm, out, down = _push(landed, name="late_weights_pass_on", plan=_plan_pass_on)
        return dict(up=_up_to_kernel(up_all.reshape(2 * D_FF, D_MODEL)), branch_hg=bhg.reshape(D_MODEL, D_MODEL),
                    branch_ssm=bssm.reshape(SSM_DINNER, D_MODEL), out=out.reshape(D_MODEL, D_MODEL),
                    down=down.reshape(D_FF, D_MODEL))

    in_flight = []

    def launch_direct(key, named_parts):
        ks, parts = zip(*named_parts)
        landing = [lax.dynamic_update_slice_in_dim(lax.empty(p.shape, p.dtype), lax.dynamic_slice_in_dim(p, me, 1, axis=0),
                                                   me, axis=0) for p in parts]
        handles = _push_start(list(parts), landing, name="grads_to_owners_start_" + key, plan=_plan_owners)
        in_flight.append(("grads_to_owners_wait_" + key, ks, handles, _plan_owners))
        return handles[4]

    def launch_two_level(key, named_parts):
        ks, parts = zip(*named_parts)
        from_sibling = _push(list(parts), name="grads_to_sibling_" + key, out_slots=N_CHIPS, plan=_plan_sibling)
        sums = [_pair_sum(p, r, name="pair_sum_" + k, tc=256) for k, p, r in zip(ks, parts, from_sibling)]
        handles = _push_start([q for q, _ in sums], [z for _, z in sums], name="grads_to_chips_start_" + key, plan=_plan_chips)
        in_flight.append(("grads_to_chips_wait_" + key, ks, handles, _plan_chips))
        return handles[4]

    def emit(key, gw):
        blocks = lambda a: a.reshape(N_DEV, -1, D_MODEL)
        if key == "down":
            return launch_direct(key, [("ffn_w_down", blocks(gw["down"]))])
        if key == "up":
            return launch_direct(key, [("ffn_w_up", blocks(_up_from_kernel(gw["up"])))])
        if key == "branches":
            return launch_direct(key, [("w_branch_hg", blocks(gw["branch_hg"])), ("w_branch_ssm", blocks(gw["branch_ssm"])),
                                       ("w_out", blocks(gw["out"]))])
        return launch_two_level(key, [("w_in", blocks(_in_proj_from_kernel(gw["in_hg"], gw["in_ssm"], gw["in_gate"])))])

    grad_x, gw, gp = _local_step(x[0], loss_target[0], w, _small_to_kernel_layout(small), late_weights, emit)

    big_names = ["w_in", "ffn_w_up", "w_branch_hg", "w_branch_ssm", "w_out", "ffn_w_down"]
    grads = {}
    for wait_name, ks, handles, plan in in_flight:
        landed = _push_wait(handles, grad_x, name=wait_name, plan=plan)
        for k, r in zip(ks, landed):
            g = _sum_blocks(r, name="sum_" + k, tc=256)
            grads[k] = g.T if k in ("w_in", "ffn_w_up") else g
    small_all = _push([_pack(_small_from_kernel_layout(gp))[None]], name="small_to_everyone", out_slots=N_DEV,
                      plan=_plan_everyone)
    small_g = _unpack(_sum_blocks(small_all[0], name="sum_small", tc=128))
    loss = small_g.pop("loss").reshape(())
    for k, g in small_g.items():
        if k in ("ssm_conv_w", "ffn_conv_w"):
            n = g.shape[1] // N_DEV
            g = lax.dynamic_slice_in_dim(g, me * n, n, axis=1)
        grads[k] = g

    delta, new_m, new_v = {}, {}, {}
    for k in big_names:
        delta[k], new_m[k], new_v[k] = _adamw(ws[k][0], grads[k], ms[k][0], vs[k][0], name="adamw_" + k, tr=64)
    small_names = [k for k in names if k not in big_names]
    flat = lambda d: jnp.concatenate([d[k].astype(F32).reshape(-1) for k in small_names])
    n_small = sum(int(np.prod(ws[k].shape)) for k in small_names)
    rows = 8 * (-(-n_small // 1024))
    pack2 = lambda d: jnp.pad(flat(d), (0, rows * 128 - n_small)).reshape(rows, 128)
    v_packed = jnp.pad(flat(vs), (0, rows * 128 - n_small), constant_values=1.0).reshape(rows, 128)
    packed = _adamw(pack2(ws), pack2(grads), pack2(ms), v_packed, name="adamw_small", tr=rows)
    o = 0
    for k in small_names:
        n = int(np.prod(ws[k].shape))
        delta[k], new_m[k], new_v[k] = (a.reshape(-1)[o:o + n].reshape(ws[k].shape) for a in packed)
        o += n

    full = lambda d: [d[k].reshape(ws[k].shape) for k in names]
    return (loss, grad_x[None], *full(grads), *full(delta), *full(new_m), *full(new_v))
```

```python
import functools

import numpy as np
import jax
import jax.numpy as jnp
from jax import lax
from jax.experimental import pallas as pl
from jax.experimental.pallas import tpu as pltpu

F32, BF16 = jnp.float32, jnp.bfloat16

D_MODEL = 2048
EPS = 1e-6
HG_HEADS, HG_DK, HG_CHUNK = 16, 128, 64
HG_BLK = 4 * HG_DK
SSM_DINNER, SSM_HEADDIM, SSM_HEADS, SSM_GROUPS, SSM_DSTATE, SSM_CONV = 4096, 64, 64, 8, 128, 4
SSM_CHUNK = 128
SSM_GW = SSM_DINNER // SSM_GROUPS
SSM_HPG = SSM_HEADS // SSM_GROUPS
SSM_XBC = SSM_GW + 2 * SSM_DSTATE
SSM_BLK = SSM_XBC + 128 + SSM_GW
SSM_CONV_DIM = SSM_DINNER + 2 * SSM_GROUPS * SSM_DSTATE
D_FF, FFN_CONV = 5632, 3
FFN_GW = 512
FFN_G = D_FF // FFN_GW
IN_TOTAL = 22592
N_DEV = 8
HALO = 8
VMEM_LIMIT = 52 * 1024 * 1024
ADAM_LR, ADAM_B1, ADAM_B2, ADAM_EPS, ADAM_WD, ADAM_STEP = 0.001, 0.9, 0.999, 1e-08, 0.01, 10

_DIMS = {"nn": ((1,), (0,)), "nt": ((1,), (1,)), "tn": ((0,), (0,))}


def _mm_raw(a, b, mode):
    return lax.dot_general(a.astype(BF16), b.astype(BF16), (_DIMS[mode], ((), ())), preferred_element_type=F32)


@functools.partial(jax.custom_vjp, nondiff_argnums=(2,))
def _mm(a, b, mode):
    return _mm_raw(a, b, mode)


def _mm_fwd(a, b, mode):
    return _mm_raw(a, b, mode), (a, b)


def _mm_bwd(mode, res, dc):
    a, b = res
    if mode == "nn":
        return _mm_raw(dc, b, "nt"), _mm_raw(a, dc, "tn")
    if mode == "nt":
        return _mm_raw(dc, b, "nn"), _mm_raw(dc, a, "tn")
    return _mm_raw(b, dc, "nt"), _mm_raw(a, dc, "nn")


_mm.defvjp(_mm_fwd, _mm_bwd)


def _cmm_raw(m, x, mode):
    hi = x.astype(BF16)
    r1 = x - hi.astype(F32)
    mid = r1.astype(BF16)
    lo = (r1 - mid.astype(F32)).astype(BF16)
    dn = (_DIMS[mode], ((), ()))
    dot = lambda p: lax.dot_general(m, p, dn, preferred_element_type=F32)
    return dot(hi) + dot(mid) + dot(lo)


@jax.custom_vjp
def _cmm(m, x):
    return _cmm_raw(m, x, "nn")


def _cmm_fwd(m, x):
    return _cmm_raw(m, x, "nn"), m


def _cmm_bwd(m, dy):
    return jnp.zeros_like(m), _cmm_raw(m, dy, "tn")


_cmm.defvjp(_cmm_fwd, _cmm_bwd)


@functools.partial(jax.custom_vjp, nondiff_argnums=(1,))
def _sroll(x, s):
    return pltpu.roll(x, s, 0) if s else x


def _sroll_fwd(x, s):
    return _sroll(x, s), None


def _sroll_bwd(s, _, ct):
    return ((pltpu.roll(ct, ct.shape[0] - s, 0) if s else ct),)


_sroll.defvjp(_sroll_fwd, _sroll_bwd)


def _rms(x, w):
    return x * lax.rsqrt(jnp.mean(x * x, axis=-1, keepdims=True) + EPS) * w


def _softplus(x):
    return jnp.maximum(x, 0.0) + jnp.log(1.0 + jnp.exp(-jnp.abs(x)))


def _causal_conv(halo, x, w, b):
    k_taps = w.shape[0]
    xe = jnp.concatenate([halo, x], axis=0)
    out = b
    for k in range(k_taps):
        out = out + w[k:k + 1, :] * _sroll(xe, k_taps - 1 - k)[HALO:, :]
    return out


def _hg_consts():
    c = HG_CHUNK
    t = np.arange(c)
    blocks, pair = [], []
    for m in (32, 16, 8, 4, 2, 1):
        pos = t % (2 * m)
        late = pos >= m
        mid = t - pos + m
        j = t[None, :]
        mq = late[:, None] & (j >= mid[:, None]) & (j <= t[:, None])
        mk = (~late)[:, None] & (j > t[:, None]) & (j <= mid[:, None] - 1)
        blocks.append(mq | mk)
        parent = t // (2 * m)
        pair.append((parent[:, None] == parent[None, :]) & late[:, None] & (~late)[None, :])
    blocks.append(t[None, :] <= t[:, None])
    mall = jnp.asarray(np.concatenate(blocks, 0).astype(np.float32), BF16)
    pair = jnp.asarray(np.stack(pair, 0).astype(np.float32))
    eye = jnp.asarray(np.eye(c, dtype=np.float32))
    return [mall, pair, eye]


def _hg_step(carry, xs, params, consts):
    (st,) = carry
    blk = xs[0].astype(F32)
    tab, nw = params
    mall, pair, eye = consts
    c, dk = HG_CHUNK, HG_DK
    q_raw, f_raw, v, og = blk[:, :dk], blk[:, dk:2 * dk], blk[:, 2 * dk:3 * dk], blk[:, 3 * dk:]
    lb = jax.nn.sigmoid(tab[0:1, :] - tab[1:2, :])
    f = lb + (1.0 - lb) * jax.nn.sigmoid(f_raw)
    g = jnp.log(f)
    kk = 1.0 - f
    qh = jax.nn.silu(q_raw) * (HG_DK ** -0.5)
    yield
    sums = _cmm(mall, g)
    yield
    b = sums[6 * c:, :]
    fac = jnp.exp(sums[:6 * c, :])
    scores = eye * jnp.sum(qh * kk, axis=1, keepdims=True)
    b_last = jnp.sum(g, axis=0, keepdims=True)
    yield
    inter = _mm(qh * jnp.exp(b), st, "nt")
    st_new = st * jnp.exp(b_last) + _mm(v, kk * jnp.exp(b_last - b), "tn")
    yield
    for l in range(6):
        fl = fac[l * c:(l + 1) * c, :]
        scores = scores + pair[l] * _mm(qh * fl, kk * fl, "nt")
        if l % 2:
            yield
    o = _mm(scores, v, "nn") + inter
    yield
    y = _rms(o, nw) * jax.nn.silu(og)
    return [st_new], [y]


def _ssd_consts():
    t = np.arange(SSM_CHUNK)
    tril = (t[None, :] <= t[:, None]).astype(np.float32)
    return [jnp.asarray(tril, BF16), jnp.asarray(tril)]


def _ssd_step(carry, xs, params, consts):
    st, halo = carry
    blk = xs[0].astype(F32)
    conv_w, conv_b, dtb, alog, dskip, nw = params
    tril_b, tril = consts
    c = SSM_CHUNK
    raw, dtr, z = blk[:, :SSM_XBC], blk[:, SSM_XBC:SSM_XBC + 128], blk[:, SSM_XBC + 128:]
    act = jax.nn.silu(_causal_conv(halo, raw, conv_w, conv_b))
    xh, bm, cm = act[:, :SSM_GW], act[:, SSM_GW:SSM_GW + SSM_DSTATE], act[:, SSM_GW + SSM_DSTATE:]
    dt = _softplus(dtr + dtb)
    da = dt * (-jnp.exp(alog))
    acum = _cmm(tril_b, da)
    acum_t = acum.T
    a_last = jnp.sum(da, axis=0, keepdims=True)
    cb_causal = _mm(cm, bm, "nt") * tril
    lane = lax.broadcasted_iota(jnp.int32, (c, 128), 1)
    row = lax.broadcasted_iota(jnp.int32, (128, 128), 0)
    first = lane < SSM_HEADDIM
    ys, st_new = [], []
    for j in range(SSM_HPG // 2):
        xp = xh[:, 128 * j:128 * (j + 1)]
        sp = st[128 * j:128 * (j + 1), :]
        r0, r1 = 2 * j, 2 * j + 1
        col = lambda a, r: jnp.broadcast_to(a[:, r:r + 1], (c, 128))
        xdt = xp * jnp.where(first, col(dt, r0), col(dt, r1))
        yj = _mm(cm, sp, "nt") * jnp.exp(jnp.where(first, col(acum, r0), col(acum, r1)))
        for r, keep in ((r0, first), (r1, ~first)):
            dec = jnp.broadcast_to(acum[:, r:r + 1], (c, c)) - jnp.broadcast_to(acum_t[r:r + 1, :], (c, c))
            m = cb_causal * jnp.exp(jnp.minimum(dec, 0.0))
            yj = yj + _mm(m, jnp.where(keep, xdt, 0.0), "nn")
        al0, al1 = a_last[:, r0:r0 + 1], a_last[:, r1:r1 + 1]
        wts = jnp.exp(jnp.where(first, al0 - col(acum, r0), al1 - col(acum, r1)))
        st_new.append(jnp.where(row < SSM_HEADDIM, jnp.exp(al0), jnp.exp(al1)) * sp + _mm(xdt * wts, bm, "tn"))
        ys.append(yj)
        yield
    y = jnp.concatenate(ys, axis=1) + dskip * xh
    y = _rms(y * jax.nn.silu(z), nw)
    return [jnp.concatenate(st_new, axis=0), raw[c - HALO:, :]], [y]


def _ffn_step(carry, xs, params, consts):
    (halo,) = carry
    blk = xs[0].astype(F32)
    conv_w, conv_b = params
    gate, up = blk[:, :FFN_GW], blk[:, FFN_GW:]
    a = jax.nn.gelu(_causal_conv(halo, gate, conv_w, conv_b), approximate=True) * up
    return [gate[gate.shape[0] - HALO:, :]], [a]


def _pre_step(carry, xs, params, consts):
    return [], [_rms(xs[0], params[0])]


def _mix_step(carry, xs, params, consts):
    gates, uh, us = (a.astype(F32) for a in xs)
    return [], [jax.nn.sigmoid(gates[:, :D_MODEL]) * uh + jax.nn.sigmoid(gates[:, D_MODEL:]) * us]


def _post_step(carry, xs, params, consts):
    x, v = xs
    x1 = x + _rms(v, params[0])
    return [], [x1, _rms(x1, params[1])]


def _scan_call(step, *, name, rows, chunk, nc, groups, xs, cins=(), params=(), consts=(), carries=(), ys=(), couts=(),
               accs=(), reverse=False, gpb=1, multi=False):
    blk_rows = chunk * nc
    nb = rows // blk_rows
    n_chunks = rows // chunk
    assert nb * blk_rows == rows and groups % gpb == 0
    rb = (lambda i: nb - 1 - i) if reverse else (lambda i: i)
    n_x, n_ci, n_p, n_c = len(xs), len(cins), len(params), len(consts)
    n_y, n_co, n_a = len(ys), len(couts), len(accs)

    def chunk_spec(shape):
        zeros = (0,) * len(shape)
        return pl.BlockSpec((gpb, nc) + tuple(shape), lambda g, i: (g, rb(i)) + zeros)

    in_specs = [pl.BlockSpec((blk_rows, gpb * w), lambda g, i: (rb(i), g)) for _, w in xs]
    in_specs += [chunk_spec(a.shape[2:]) for a in cins]
    in_specs += [pl.BlockSpec((gpb,) + tuple(a.shape[1:]), lambda g, i: (g, 0, 0)) for a in params]
    in_specs += [pl.BlockSpec(a.shape, (lambda nd: lambda g, i: (0,) * nd)(a.ndim)) for a in consts]
    out_specs = [pl.BlockSpec((blk_rows, gpb * w), lambda g, i: (rb(i), g)) for w, _ in ys]
    out_specs += [chunk_spec(s) for s in couts]
    out_specs += [pl.BlockSpec((gpb, r, c), lambda g, i: (g, 0, 0)) for r, c in accs]
    out_shape = [jax.ShapeDtypeStruct((rows, groups * w), dt) for w, dt in ys]
    out_shape += [jax.ShapeDtypeStruct((groups, n_chunks) + tuple(s), F32) for s in couts]
    out_shape += [jax.ShapeDtypeStruct((groups, r, c), F32) for r, c in accs]
    x_widths = [w for _, w in xs]
    y_widths = [w for w, _ in ys]

    def body(*refs):
        x_refs = refs[:n_x]
        ci_refs = refs[n_x:n_x + n_ci]
        p_refs = refs[n_x + n_ci:n_x + n_ci + n_p]
        c_refs = refs[n_x + n_ci + n_p:n_x + n_ci + n_p + n_c]
        o = n_x + n_ci + n_p + n_c
        y_refs = refs[o:o + n_y]
        co_refs = refs[o + n_y:o + n_y + n_co]
        a_refs = refs[o + n_y + n_co:o + n_y + n_co + n_a]
        carry_refs = refs[o + n_y + n_co + n_a:]

        @pl.when(pl.program_id(1) == 0)
        def _():
            for s in carry_refs:
                s[...] = jnp.zeros(s.shape, F32)
            for a in a_refs:
                a[...] = jnp.zeros(a.shape, F32)

        cvals = [c[...] for c in c_refs]

        def one_chunk(i, _):
            c = (nc - 1 - i) if reverse else i
            r0 = c * chunk if isinstance(c, int) else pl.multiple_of(c * chunk, chunk)
            loaded = []
            for u in range(gpb):
                carry = [s[u] for s in carry_refs]
                xv = [x[pl.ds(r0, chunk), u * w:(u + 1) * w] for x, w in zip(x_refs, x_widths)]
                civ = [ci[u, c] for ci in ci_refs]
                loaded.append((carry, xv, civ, [p[u] for p in p_refs]))
            results = step(loaded, cvals) if multi else [step(*args, cvals) for args in loaded]
            for u, (new_carry, yv, cov, av) in enumerate(results):
                for s, val in zip(carry_refs, new_carry):
                    s[u] = val
                for y, w, val in zip(y_refs, y_widths, yv):
                    y[pl.ds(r0, chunk), u * w:(u + 1) * w] = val.astype(y.dtype)
                for co, val in zip(co_refs, cov):
                    co[u, c] = val
                for a, val in zip(a_refs, av):
                    a[u] += val
            return 0

        if nc == 1:
            one_chunk(0, 0)
        else:
            lax.fori_loop(0, nc, one_chunk, 0)

    outs = pl.pallas_call(
        body, name=name, grid=(groups // gpb, nb), in_specs=in_specs, out_specs=out_specs, out_shape=out_shape,
        scratch_shapes=[pltpu.VMEM((gpb,) + tuple(s), F32) for s in carries],
        compiler_params=pltpu.CompilerParams(dimension_semantics=("arbitrary", "arbitrary"),
                                             vmem_limit_bytes=VMEM_LIMIT),
    )(*[a for a, _ in xs], *cins, *params, *consts)
    return outs[:n_y], outs[n_y:n_y + n_co], outs[n_y + n_co:]


def _run_interleaved(step, arg_tuples):
    runs = [step(*args) for args in arg_tuples]
    if not hasattr(runs[0], "send"):
        return runs
    results, live = [None] * len(runs), list(range(len(runs)))
    while live:
        for u in list(live):
            try:
                next(runs[u])
            except StopIteration as done:
                results[u] = done.value
                live.remove(u)
    return results


def _stage_fwd(step, *, name, rows, chunk, nc, groups, xs, params, consts, carries, ys, gpb=1):
    def fstep(loaded, cv):
        outs = _run_interleaved(step, [(carry, xv, pv, cv) for carry, xv, _, pv in loaded])
        return [(new_carry, yv, carry, []) for (new_carry, yv), (carry, _, _, _) in zip(outs, loaded)]

    yv, saved, _ = _scan_call(fstep, name=name, rows=rows, chunk=chunk, nc=nc, groups=groups, xs=xs, params=params,
                              consts=consts, carries=carries, ys=ys, couts=carries, gpb=gpb, multi=True)
    return yv, saved


def _stage_bwd(step, *, name, rows, chunk, nc, groups, xs, saved, params, consts, carries, dys, dxs, gpb=1):
    n_x = len(xs)

    def bstep(loaded, cv):
        civs = [list(civ) for _, _, civ, _ in loaded]
        xvs = [list(xv_all[:n_x]) for _, xv_all, _, _ in loaded]
        pvs = [list(pv) for _, _, _, pv in loaded]
        cts = [(list(dcarry), [d.astype(F32) for d in xv_all[n_x:]]) for dcarry, xv_all, _, _ in loaded]

        def fwd(civs_, xvs_, pvs_):
            outs = _run_interleaved(step, [(c_, x_, p_, cv) for c_, x_, p_ in zip(civs_, xvs_, pvs_)])
            return [(list(new_carry), list(yv)) for new_carry, yv in outs]

        _, vjp = jax.vjp(fwd, civs, xvs, pvs)
        dcivs, dxvs, dpvs = vjp(cts)
        return [(dc, dx, [], dp) for dc, dx, dp in zip(dcivs, dxvs, dpvs)]

    dxv, _, dpv = _scan_call(bstep, name=name, rows=rows, chunk=chunk, nc=nc, groups=groups, xs=list(xs) + list(dys),
                             cins=saved, params=params, consts=consts, carries=carries,
                             ys=[(w, dt) for (_, w), dt in zip(xs, dxs)], accs=[a.shape[1:] for a in params],
                             reverse=True, gpb=gpb, multi=True)
    return dxv, dpv


def _mm_params(sem):
    return pltpu.CompilerParams(dimension_semantics=sem, vmem_limit_bytes=VMEM_LIMIT)


def _after(dep):
    return ([], []) if dep is None else ([dep], [pl.BlockSpec(memory_space=pl.ANY)])


def _matmul_nt(a, b, *, name, dep=None):
    m, k = a.shape
    n = b.shape[0]
    tm = min(1024, m)
    tn = 1024 if n % 1024 == 0 else 1408 if n % 1408 == 0 else 512
    deps, dep_specs = _after(dep)

    def body(a_ref, b_ref, *rest):
        rest[-1][...] = lax.dot_general(a_ref[...], b_ref[...], (_DIMS["nt"], ((), ())), preferred_element_type=F32)

    return pl.pallas_call(
        body, name=name, grid=(m // tm, n // tn),
        in_specs=[pl.BlockSpec((tm, k), lambda i, j: (i, 0)), pl.BlockSpec((tn, k), lambda i, j: (j, 0))] + dep_specs,
        out_specs=pl.BlockSpec((tm, tn), lambda i, j: (i, j)),
        out_shape=jax.ShapeDtypeStruct((m, n), F32),
        compiler_params=_mm_params(("parallel", "arbitrary")),
    )(a, b, *deps)


def _matmul_nn(a, b, *, name, dep=None):
    m, k = a.shape
    n = b.shape[1]
    deps, dep_specs = _after(dep)
    if k > 6144:
        tm, tk, steps = min(512, m), k // 4, 4
        total = (m // tm) * steps

        def body_k(a_ref, b_hbm, *rest):
            o_ref, ring, sems = rest[-3:]
            s = pl.program_id(0) * steps + pl.program_id(1)

            def fetch(step):
                rows = pl.ds(pl.multiple_of((step % steps) * tk, tk), tk)
                return pltpu.make_async_copy(b_hbm.at[rows, :], ring.at[step % 3], sems.at[step % 3])

            @pl.when(s == 0)
            def _():
                fetch(s).start()
                fetch(s + 1).start()

            @pl.when(s + 2 < total)
            def _():
                fetch(s + 2).start()

            fetch(s).wait()
            part = jnp.dot(a_ref[...], ring[s % 3], preferred_element_type=F32)

            @pl.when(pl.program_id(1) == 0)
            def _():
                o_ref[...] = part

            @pl.when(pl.program_id(1) != 0)
            def _():
                o_ref[...] += part

        return pl.pallas_call(
            body_k, name=name, grid=(m // tm, steps),
            in_specs=[pl.BlockSpec((tm, tk), lambda i, j: (i, j)), pl.BlockSpec(memory_space=pl.ANY)] + dep_specs,
            out_specs=pl.BlockSpec((tm, n), lambda i, j: (i, 0)),
            out_shape=jax.ShapeDtypeStruct((m, n), F32),
            scratch_shapes=[pltpu.VMEM((3, tk, n), BF16), pltpu.SemaphoreType.DMA((3,))],
            compiler_params=_mm_params(("arbitrary", "arbitrary")),
        )(a, b, *deps)
    tm, tn = (1024, 1024) if k <= 4096 else (1024, 512)
    tm = min(tm, m)

    def body(a_ref, b_ref, *rest):
        rest[-1][...] = jnp.dot(a_ref[...], b_ref[...], preferred_element_type=F32)

    return pl.pallas_call(
        body, name=name, grid=(m // tm, n // tn),
        in_specs=[pl.BlockSpec((tm, k), lambda i, j: (i, 0)), pl.BlockSpec((k, tn), lambda i, j: (0, j))] + dep_specs,
        out_specs=pl.BlockSpec((tm, tn), lambda i, j: (i, j)),
        out_shape=jax.ShapeDtypeStruct((m, n), F32),
        compiler_params=_mm_params(("parallel", "arbitrary")),
    )(a, b, *deps)


def _matmul_tn(x, y, *, name, tp=512, tq=512):
    t, p = x.shape
    q = y.shape[1]

    def body(x_ref, y_ref, o_ref):
        o_ref[...] = lax.dot_general(x_ref[...], y_ref[...], (_DIMS["tn"], ((), ())),
                                     preferred_element_type=F32).astype(o_ref.dtype)

    return pl.pallas_call(
        body, name=name, grid=(p // tp, q // tq),
        in_specs=[pl.BlockSpec((t, tp), lambda i, j: (0, i)), pl.BlockSpec((t, tq), lambda i, j: (0, j))],
        out_specs=pl.BlockSpec((tp, tq), lambda i, j: (i, j)),
        out_shape=jax.ShapeDtypeStruct((p, q), BF16),
        compiler_params=_mm_params(("parallel", "arbitrary")),
    )(x, y)


N_CHIPS = N_DEV // 2


def _all_gather(arrays, *, name):
    n = len(arrays)
    out_shape = [jax.ShapeDtypeStruct((N_DEV,) + tuple(a.shape), a.dtype) for a in arrays]

    def body(*refs):
        in_refs, out_refs = refs[:n], refs[n:2 * n]
        send_sems, recv_sems, local_sems = refs[2 * n:]
        x, y, c = lax.axis_index("x"), lax.axis_index("y"), lax.axis_index("c")
        me, sibling = (x, y, c), (x, y, 1 - c)
        chips = [(1 - x, y), (x, 1 - y), (1 - x, 1 - y)]
        south = c == 0
        relay_to = (jnp.where(south, x, 1 - x), jnp.where(south, 1 - y, y), c)
        relayed = (jnp.where(south, 1 - x, x), jnp.where(south, y, 1 - y), c)

        def copy(a, k, block, to, src=None):
            slot = out_refs[a].at[4 * block[0] + 2 * block[1] + block[2]]
            return pltpu.make_async_remote_copy(
                src_ref=slot if src is None else src, dst_ref=slot, send_sem=send_sems.at[a, k],
                recv_sem=recv_sems.at[a, k], device_id=to, device_id_type=pl.DeviceIdType.MESH)

        mine = [pltpu.make_async_copy(in_refs[a], out_refs[a].at[4 * x + 2 * y + c], local_sems.at[a]) for a in range(n)]
        first = []
        for a in range(n):
            first.append(copy(a, 0, me, sibling, src=in_refs[a]))
            first += [copy(a, 1 + j, me, (*chip, c), src=in_refs[a]) for j, chip in enumerate(chips[:2])]
        for cp in mine + first:
            cp.start()
        passed = []
        for j, chip in enumerate(chips[:2]):
            for a in range(n):
                copy(a, 1 + j, (*chip, c), me).wait_recv()
                passed.append(copy(a, 4 + j, (*chip, c), sibling))
                passed[-1].start()
        for a in range(n):
            passed.append(copy(a, 3, relayed, relay_to))
            passed[-1].start()
        for a in range(n):
            copy(a, 3, (*chips[2], c), me).wait_recv()
            passed.append(copy(a, 6, (*chips[2], c), sibling))
            passed[-1].start()
        for a in range(n):
            copy(a, 0, sibling, me).wait_recv()
            for j, chip in enumerate(chips):
                copy(a, 4 + j, (*chip, 1 - c), me).wait_recv()
        for cp in first + passed:
            cp.wait_send()
        for cp in mine:
            cp.wait()

    any_spec = pl.BlockSpec(memory_space=pl.ANY)
    return pl.pallas_call(
        body, name=name, in_specs=[any_spec] * n, out_specs=[any_spec] * n, out_shape=out_shape,
        scratch_shapes=[pltpu.SemaphoreType.DMA((n, N_DEV - 1)), pltpu.SemaphoreType.DMA((n, N_DEV - 1)),
                        pltpu.SemaphoreType.DMA((n,))],
        compiler_params=pltpu.CompilerParams(has_side_effects=True),
    )(*arrays)


def _push(arrays, *, name, plan, out_slots=None):
    n = len(arrays)
    in_place = out_slots is None
    out_shape = [jax.ShapeDtypeStruct(((a.shape[0] if in_place else out_slots),) + tuple(a.shape[1:]), a.dtype) for a in arrays]
    n_tr = len(plan(0, 0, 0)[0])

    def body(*refs):
        in_refs, out_refs = refs[:n], refs[n:2 * n]
        send_sems, recv_sems, local_sems = refs[2 * n:]
        src_refs = out_refs if in_place else in_refs
        transfers, local = plan(lax.axis_index("x"), lax.axis_index("y"), lax.axis_index("c"))
        copies = []
        for a in range(n):
            if local is not None:
                copies.append(pltpu.make_async_copy(src_refs[a].at[local[0]], out_refs[a].at[local[1]], local_sems.at[a]))
            for k, (peer, src, dst) in enumerate(transfers):
                copies.append(pltpu.make_async_remote_copy(
                    src_ref=src_refs[a].at[src], dst_ref=out_refs[a].at[dst], send_sem=send_sems.at[a, k],
                    recv_sem=recv_sems.at[a, k], device_id=peer, device_id_type=pl.DeviceIdType.MESH))
        for cp in copies:
            cp.start()
        for cp in copies:
            cp.wait()

    any_spec = pl.BlockSpec(memory_space=pl.ANY)
    return pl.pallas_call(
        body, name=name, in_specs=[any_spec] * n, out_specs=[any_spec] * n, out_shape=out_shape,
        input_output_aliases={a: a for a in range(n)} if in_place else {},
        scratch_shapes=[pltpu.SemaphoreType.DMA((n, n_tr)), pltpu.SemaphoreType.DMA((n, n_tr)),
                        pltpu.SemaphoreType.DMA((n,))],
        compiler_params=pltpu.CompilerParams(has_side_effects=True),
    )(*arrays)


_HBM_SPEC = pl.BlockSpec(memory_space=pltpu.HBM)
_SEM_SPEC = pl.BlockSpec(memory_space=pltpu.SEMAPHORE)
_DATAFLOW = pltpu.SideEffectType.DATAFLOW_SIDE_EFFECTING


def _push_start(sources, landing, *, name, plan, after=None):
    n = len(sources)
    n_tr = len(plan(0, 0, 0)[0])
    deps, dep_specs = _after(after)

    def body(*refs):
        src_refs, land_refs = refs[:n], refs[n:2 * n]
        o = 2 * n + len(deps)
        send_sems, recv_sems, token = refs[o], refs[o + 1], refs[-1]
        transfers, _ = plan(lax.axis_index("x"), lax.axis_index("y"), lax.axis_index("c"))
        for a in range(n):
            for k, (peer, src, dst) in enumerate(transfers):
                pltpu.make_async_remote_copy(
                    src_ref=src_refs[a].at[src], dst_ref=land_refs[a].at[dst], send_sem=send_sems.at[a * n_tr + k],
                    recv_sem=recv_sems.at[a * n_tr + k], device_id=peer, device_id_type=pl.DeviceIdType.MESH).start()
        token[...] = jnp.zeros(token.shape, token.dtype)

    hbm = lambda a: pltpu.HBM(a.shape, a.dtype)
    outs = pl.pallas_call(
        body, name=name,
        out_shape=(pltpu.SemaphoreType.DMA((n * n_tr,)), pltpu.SemaphoreType.DMA((n * n_tr,)), *[hbm(a) for a in sources],
                   *[hbm(a) for a in landing], jax.ShapeDtypeStruct((8, 128), F32)),
        in_specs=[_HBM_SPEC] * (2 * n) + dep_specs,
        out_specs=(_SEM_SPEC, _SEM_SPEC, *[_HBM_SPEC] * (2 * n), pl.BlockSpec(memory_space=pltpu.VMEM)),
        input_output_aliases={i: 2 + i for i in range(2 * n)},
        compiler_params=pltpu.CompilerParams(has_side_effects=_DATAFLOW),
    )(*[pltpu.with_memory_space_constraint(a, pltpu.HBM) for a in list(sources) + list(landing)], *deps)
    return outs[0], outs[1], list(outs[2:2 + n]), list(outs[2 + n:2 + 2 * n]), outs[-1]


def _push_wait(handles, after, *, name, plan):
    send_sems, recv_sems, sources, landing, _ = handles
    n = len(sources)
    after = list(after) if isinstance(after, (list, tuple)) else [after]

    def body(*refs):
        src_refs, land_refs = refs[:n], refs[n:2 * n]
        send_sems_, recv_sems_ = refs[2 * n], refs[2 * n + 1]
        transfers, _ = plan(lax.axis_index("x"), lax.axis_index("y"), lax.axis_index("c"))
        n_tr = len(transfers)
        for a in range(n):
            for k, (peer, src, dst) in enumerate(transfers):
                cp = pltpu.make_async_remote_copy(
                    src_ref=src_refs[a].at[src], dst_ref=land_refs[a].at[dst], send_sem=send_sems_.at[a * n_tr + k],
                    recv_sem=recv_sems_.at[a * n_tr + k], device_id=peer, device_id_type=pl.DeviceIdType.MESH)
                cp.wait_send()
                cp.wait_recv()

    hbm = lambda a: pltpu.HBM(a.shape, a.dtype)
    outs = pl.pallas_call(
        body, name=name, out_shape=tuple(hbm(a) for a in list(sources) + list(landing)),
        in_specs=[_HBM_SPEC] * (2 * n) + [_SEM_SPEC, _SEM_SPEC] + [pl.BlockSpec(memory_space=pl.ANY)] * len(after),
        out_specs=[_HBM_SPEC] * (2 * n), input_output_aliases={i: i for i in range(2 * n)},
        compiler_params=pltpu.CompilerParams(has_side_effects=_DATAFLOW),
    )(*sources, *landing, send_sems, recv_sems, *after)
    return list(outs[n:])


def _plan_everyone(x, y, c):
    me = 4 * x + 2 * y + c
    peers = [(1 - x if k & 4 else x, 1 - y if k & 2 else y, 1 - c if k & 1 else c) for k in range(1, N_DEV)]
    return [(p, 0, me) for p in peers], (0, me)


def _plan_owners(x, y, c):
    me = 4 * x + 2 * y + c
    peers = [(1 - x if k & 4 else x, 1 - y if k & 2 else y, 1 - c if k & 1 else c) for k in range(1, N_DEV)]
    return [((px, py, pc), 4 * px + 2 * py + pc, me) for px, py, pc in peers], None


def _plan_sibling(x, y, c):
    return [((x, y, 1 - c), 2 * chip + (1 - c), chip) for chip in range(N_CHIPS)], None


def _plan_chips(x, y, c):
    mine = 2 * x + y
    peers = [(1 - x, y), (x, 1 - y), (1 - x, 1 - y)]
    return [((px, py, c), 2 * px + py, mine) for px, py in peers], (mine, mine)


def _plan_own_block(x, y, c):
    me = 4 * x + 2 * y + c
    peers = [(x, y, 1 - c), (1 - x, y, c), (x, 1 - y, c), (1 - x, 1 - y, c)]
    return [(p, 0, me) for p in peers], None


def _plan_pass_on(x, y, c):
    slots = [4 * px + 2 * py + c for px, py in ((1 - x, y), (x, 1 - y), (1 - x, 1 - y))]
    return [((x, y, 1 - c), s, s) for s in slots], None


def _pair_sum(parts, received, *, name, tc):
    _, r, c = parts.shape
    core = lax.axis_index("c").astype(jnp.int32).reshape(1)

    def body(core_ref, p_ref, r_ref, o_ref, o2_ref):
        s = (p_ref[...].astype(F32) + r_ref[...].astype(F32)).astype(o_ref.dtype)
        o_ref[...] = s
        o2_ref[...] = s

    out = pl.BlockSpec((None, r, tc), lambda i, j, core_ref: (i, 0, j))
    return pl.pallas_call(
        body, name=name,
        grid_spec=pltpu.PrefetchScalarGridSpec(
            num_scalar_prefetch=1, grid=(N_CHIPS, c // tc),
            in_specs=[pl.BlockSpec((None, r, tc), lambda i, j, core_ref: (2 * i + core_ref[0], 0, j)),
                      pl.BlockSpec((None, r, tc), lambda i, j, core_ref: (i, 0, j))],
            out_specs=[out, out]),
        out_shape=[jax.ShapeDtypeStruct((N_CHIPS, r, c), BF16)] * 2,
        compiler_params=pltpu.CompilerParams(dimension_semantics=("parallel", "parallel"), vmem_limit_bytes=VMEM_LIMIT),
    )(core, parts, received)


def _sum_blocks(a, *, name, tc):
    nblk, r, c = a.shape

    def body(a_ref, o_ref):
        acc = a_ref[0].astype(F32)
        for i in range(1, nblk):
            acc = acc + a_ref[i].astype(F32)
        o_ref[...] = acc

    return pl.pallas_call(
        body, name=name, grid=(c // tc,),
        in_specs=[pl.BlockSpec((nblk, r, tc), lambda j: (0, 0, j))],
        out_specs=pl.BlockSpec((r, tc), lambda j: (0, j)),
        out_shape=jax.ShapeDtypeStruct((r, c), F32),
        compiler_params=pltpu.CompilerParams(dimension_semantics=("parallel",), vmem_limit_bytes=VMEM_LIMIT),
    )(a)


def _adamw(w, g, m, v, *, name, tr):
    r, c = w.shape

    def body(w_ref, g_ref, m_ref, v_ref, d_ref, mo_ref, vo_ref):
        gv = g_ref[...]
        mn = ADAM_B1 * m_ref[...] + (1.0 - ADAM_B1) * gv
        vn = ADAM_B2 * v_ref[...] + (1.0 - ADAM_B2) * jnp.square(gv)
        m_hat = mn / (1.0 - ADAM_B1 ** ADAM_STEP)
        v_hat = vn / (1.0 - ADAM_B2 ** ADAM_STEP)
        d_ref[...] = -ADAM_LR * (m_hat / (jnp.sqrt(v_hat) + ADAM_EPS) + ADAM_WD * w_ref[...])
        mo_ref[...] = mn
        vo_ref[...] = vn

    spec = pl.BlockSpec((tr, c), lambda i: (i, 0))
    return pl.pallas_call(
        body, name=name, grid=(r // tr,), in_specs=[spec] * 4, out_specs=[spec] * 3,
        out_shape=[jax.ShapeDtypeStruct((r, c), F32)] * 3,
        compiler_params=pltpu.CompilerParams(dimension_semantics=("parallel",), vmem_limit_bytes=VMEM_LIMIT),
    )(w, g, m, v)


def _conv_layout():
    idx = []
    for g in range(SSM_GROUPS):
        idx.append(np.concatenate([g * SSM_GW + np.arange(SSM_GW),
                                   SSM_DINNER + g * SSM_DSTATE + np.arange(SSM_DSTATE),
                                   SSM_DINNER + SSM_GROUPS * SSM_DSTATE + g * SSM_DSTATE + np.arange(SSM_DSTATE)]))
    return np.stack(idx)


def _inverse(idx, n):
    inv = np.zeros(n, np.int64)
    pos = np.nonzero(idx >= 0)[0]
    inv[idx[pos]] = pos
    return inv


def _take_rows(a, idx, axis=0):
    idx = np.asarray(idx).reshape(-1)
    pieces, start = [], 0
    for i in range(1, len(idx) + 1):
        same_run = i < len(idx) and ((idx[i] == idx[i - 1] + 1 and idx[i - 1] >= 0) or (idx[i] < 0 and idx[i - 1] < 0))
        if same_run:
            continue
        n = i - start
        if idx[start] < 0:
            shape = list(a.shape)
            shape[axis] = n
            pieces.append(jnp.zeros(shape, a.dtype))
        else:
            pieces.append(lax.slice_in_dim(a, int(idx[start]), int(idx[start]) + n, axis=axis))
        start = i
    return pieces[0] if len(pieces) == 1 else jnp.concatenate(pieces, axis=axis)


def _copy_runs(sources, out_rows, runs, *, name, block, total_rows=None, into=None):
    d, dtype = sources[0].shape[1], sources[0].dtype
    outs = []
    base = 0 if into is None else into[1] // block
    extra, extra_specs = ([], []) if into is None else ([into[0]], [pl.BlockSpec(memory_space=pl.ANY)])
    for o, rows in enumerate(out_rows):
        mine = sorted({i for i, _, oo, _, _ in runs if oo == o})
        ns, nblk = len(mine), rows // block
        sel = np.zeros(nblk, np.int32)
        idx = np.full((ns, nblk), -1, np.int64)
        for i, s, oo, t, n in runs:
            if oo == o:
                assert s % block == 0 and t % block == 0 and n % block == 0
                for b in range(n // block):
                    sel[t // block + b] = mine.index(i)
                    idx[mine.index(i), t // block + b] = s // block + b
        assert (idx.max(axis=0) >= 0).all()
        for i in range(ns):
            first = idx[i, np.nonzero(idx[i] >= 0)[0][0]]
            for b in range(nblk):
                if idx[i, b] < 0:
                    idx[i, b] = idx[i, b - 1] if b > 0 else first

        def body(sel_ref, idx_ref, *refs, ns=ns):
            srcs, out = refs[:ns], refs[-1]
            which = sel_ref[pl.program_id(0)]
            val = srcs[ns - 1][...]
            for i in range(ns - 2, -1, -1):
                val = jnp.where(which == i, srcs[i][...], val)
            out[...] = val

        in_specs = [pl.BlockSpec((block, d), (lambda i_, n_: lambda b, sel_ref, idx_ref: (idx_ref[i_ * n_ + b], 0))(i, nblk))
                    for i in range(ns)]
        full_rows = into[0].shape[0] if into is not None else (total_rows or rows)
        outs.append(pl.pallas_call(
            body, name=f"{name}_{o}" if len(out_rows) > 1 else name,
            grid_spec=pltpu.PrefetchScalarGridSpec(
                num_scalar_prefetch=2, grid=(nblk,), in_specs=in_specs + extra_specs,
                out_specs=pl.BlockSpec((block, d), lambda b, sel_ref, idx_ref: (base + b, 0))),
            out_shape=jax.ShapeDtypeStruct((full_rows, d), dtype),
            input_output_aliases={} if into is None else {2 + ns: 0},
            compiler_params=pltpu.CompilerParams(dimension_semantics=("arbitrary",), vmem_limit_bytes=VMEM_LIMIT),
        )(jnp.asarray(sel), jnp.asarray(idx.reshape(-1), jnp.int32), *[sources[i] for i in mine], *extra))
    return outs


_Z0, _XBC0, _DT0, _GATE0 = 8192, 12288, 18432, 18496
_B0, _C0 = _XBC0 + SSM_DINNER, _XBC0 + SSM_DINNER + SSM_GROUPS * SSM_DSTATE


def _in_proj_runs():
    runs = [(part * 2048 + h * HG_DK, 0, h * HG_BLK + part * HG_DK, HG_DK) for h in range(HG_HEADS) for part in range(4)]
    for g in range(SSM_GROUPS):
        base = g * SSM_BLK
        runs += [(_XBC0 + g * SSM_GW, 1, base, SSM_GW), (_B0 + g * SSM_DSTATE, 1, base + SSM_GW, SSM_DSTATE),
                 (_C0 + g * SSM_DSTATE, 1, base + SSM_GW + SSM_DSTATE, SSM_DSTATE),
                 (_Z0 + g * SSM_GW, 1, base + SSM_XBC + 128, SSM_GW)]
    return runs + [(_GATE0, 2, 0, 2 * D_MODEL)]


def _in_proj_to_kernel(in_t):
    d = in_t.shape[1]
    dt = jnp.pad(in_t[_DT0:_GATE0].reshape(SSM_GROUPS, SSM_HPG, d), ((0, 0), (0, 128 - SSM_HPG), (0, 0)))
    runs = [(0, src, sec, dst, n) for src, sec, dst, n in _in_proj_runs() if sec < 2]
    runs += [(1, g * 128, 1, g * SSM_BLK + SSM_XBC, 128) for g in range(SSM_GROUPS)]
    hg, ssm = _copy_runs([in_t, dt.reshape(SSM_GROUPS * 128, d)], [_Z0, SSM_GROUPS * SSM_BLK], runs,
                         name="in_proj_to_kernel_layout", block=128)
    return hg, ssm, in_t[_GATE0:]


def _in_proj_from_kernel(hg, ssm, gate):
    d = hg.shape[1]
    dt = ssm.reshape(SSM_GROUPS, SSM_BLK, d)[:, SSM_XBC:SSM_XBC + SSM_HPG].reshape(SSM_HEADS, d)
    runs = [(sec, dst, 0, src, n) for src, sec, dst, n in _in_proj_runs() if sec < 2]
    main = _copy_runs([hg, ssm], [_DT0], runs, name="in_proj_to_global_layout", block=128, total_rows=IN_TOTAL)[0]
    tail = [(0, 0, 0, 0, SSM_HEADS), (1, 0, 0, SSM_HEADS, 2 * D_MODEL)]
    return _copy_runs([dt, gate], [IN_TOTAL - _DT0], tail, name="in_proj_to_global_layout_tail", block=SSM_HEADS,
                      into=(main, _DT0))[0]


def _up_to_kernel(up_t):
    runs = [(0, part * D_FF + g * FFN_GW, 0, (2 * g + part) * FFN_GW, FFN_GW) for g in range(FFN_G) for part in range(2)]
    return _copy_runs([up_t], [2 * D_FF], runs, name="up_to_kernel_layout", block=FFN_GW)[0]


def _up_from_kernel(up):
    runs = [(0, (2 * g + part) * FFN_GW, 0, part * D_FF + g * FFN_GW, FFN_GW) for g in range(FFN_G) for part in range(2)]
    return _copy_runs([up], [2 * D_FF], runs, name="up_to_global_layout", block=FFN_GW)[0]


_SMALL = (("mix_pre_norm", (1, 2048)), ("mix_post_norm", (1, 2048)), ("hg_lb_table", (2, 2048)), ("hg_out_norm", (1, 128)),
          ("ssm_conv_w", (4, 6144)), ("ssm_conv_b", (1, 6144)), ("ssm_dt_bias", (1, 64)), ("ssm_A_log", (1, 64)),
          ("ssm_D", (1, 64)), ("ssm_out_norm", (1, 4096)), ("ffn_pre_norm", (1, 2048)), ("ffn_post_norm", (1, 2048)),
          ("ffn_conv_w", (3, 5632)), ("ffn_conv_b", (1, 5632)), ("loss", (1, 1)))
_PACK_ROWS = 8 * (-(-sum(int(np.prod(s)) for _, s in _SMALL) // 1024))


def _pack(vals):
    flat = jnp.concatenate([vals[k].astype(F32).reshape(-1) for k, _ in _SMALL])
    return jnp.pad(flat, (0, _PACK_ROWS * 128 - flat.shape[0])).reshape(_PACK_ROWS, 128)


def _unpack(packed):
    flat, out, o = packed.reshape(-1), {}, 0
    for k, s in _SMALL:
        n = int(np.prod(s))
        out[k] = flat[o:o + n].reshape(s)
        o += n
    return out


def _local_step(x, target, w, p, late_weights=None, emit=lambda key, gw: None):
    t = x.shape[0]
    one = lambda a: a.reshape((1,) + a.shape)
    row = dict(rows=t, groups=1, consts=[], carries=[])

    (h1,), _ = _stage_fwd(_pre_step, name="pre_fwd", chunk=512, nc=1, xs=[(x, D_MODEL)], params=[one(p["mix_pre_norm"])],
                          ys=[(D_MODEL, BF16)], **row)
    proj_hg = _matmul_nt(h1, w["in_hg"], name="proj_hg")
    proj_ssm = _matmul_nt(h1, w["in_ssm"], name="proj_ssm")
    proj_gate = _matmul_nt(h1, w["in_gate"], name="proj_gate")

    hg = dict(rows=t, chunk=HG_CHUNK, nc=4, groups=HG_HEADS, xs=[(proj_hg, HG_BLK)], params=[p["hg_tab"], p["hg_nw"]],
              consts=_hg_consts(), carries=[(HG_DK, HG_DK)], gpb=HG_HEADS)
    (y_hg,), hg_saved = _stage_fwd(_hg_step, name="hg_fwd", ys=[(HG_DK, BF16)], **hg)

    ssd = dict(rows=t, chunk=SSM_CHUNK, nc=4, groups=SSM_GROUPS, xs=[(proj_ssm, SSM_BLK)],
               params=[p["conv_w"], p["conv_b"], p["dt_bias"], p["a_log"], p["d_skip"], p["ssm_nw"]],
               consts=_ssd_consts(), carries=[(4 * 128, SSM_DSTATE), (HALO, SSM_XBC)], gpb=2)
    (y_ssm,), ssd_saved = _stage_fwd(_ssd_step, name="ssd_fwd", ys=[(SSM_GW, BF16)], **ssd)

    if late_weights is not None:
        w = {**w, **late_weights([y_hg, y_ssm])}
    u_hg = _matmul_nn(y_hg, w["branch_hg"], name="branch_hg")
    u_ssm = _matmul_nn(y_ssm, w["branch_ssm"], name="branch_ssm")
    mix = dict(chunk=256, nc=1, xs=[(proj_gate, 2 * D_MODEL), (u_hg, D_MODEL), (u_ssm, D_MODEL)], params=[], **row)
    (mixed,), _ = _stage_fwd(_mix_step, name="mix_fwd", ys=[(D_MODEL, BF16)], **mix)
    v = _matmul_nn(mixed, w["out"], name="out_proj")
    post = dict(chunk=256, nc=1, xs=[(x, D_MODEL), (v, D_MODEL)],
                params=[one(p["mix_post_norm"]), one(p["ffn_pre_norm"])], **row)
    (x1, h2), _ = _stage_fwd(_post_step, name="post_fwd", ys=[(D_MODEL, F32), (D_MODEL, BF16)], **post)
    gu = _matmul_nt(h2, w["up"], name="ffn_up")
    ffn = dict(rows=t, chunk=256, nc=2, groups=FFN_G, xs=[(gu, 2 * FFN_GW)], params=[p["ffn_conv_w"], p["ffn_conv_b"]],
               consts=[], carries=[(HALO, FFN_GW)])
    (act,), ffn_saved = _stage_fwd(_ffn_step, name="ffn_fwd", ys=[(FFN_GW, BF16)], **ffn)
    d = _matmul_nn(act, w["down"], name="ffn_down")

    def head_step(carry, xv, civ, pv, cv):
        x1_, d_, tgt = xv

        def per_row_loss(a, b, nw):
            e = a + _rms(b, nw) - tgt
            return 0.5 * jnp.mean(e * e, axis=1, keepdims=True)

        lrow, vjp = jax.vjp(per_row_loss, x1_, d_, pv[0])
        dx1_, dd_, dnw = vjp(jnp.ones_like(lrow))
        loss = jnp.broadcast_to(jnp.sum(lrow, axis=0, keepdims=True), (1, 128))
        return [], [dx1_, dd_], [], [dnw, loss]

    (dy, dd), _, (g_ffn_post, loss) = _scan_call(
        head_step, name="loss_head", chunk=256, nc=1, xs=[(x1, D_MODEL), (d, D_MODEL), (target, D_MODEL)],
        params=[one(p["ffn_post_norm"])], ys=[(D_MODEL, F32), (D_MODEL, BF16)], accs=[(1, D_MODEL), (1, 128)], **row)

    gw = {}
    gw["down"] = _matmul_tn(act, dd, name="g_down")
    dact = _matmul_nt(dd, w["down"], name="d_act", dep=emit("down", gw))
    (dgu,), (g_fcw, g_fcb) = _stage_bwd(_ffn_step, name="ffn_bwd", saved=ffn_saved, dys=[(dact, FFN_GW)], dxs=[BF16], **ffn)
    gw["up"] = _matmul_tn(dgu, h2, name="g_up")
    dh2 = _matmul_nn(dgu, w["up"], name="d_h2", dep=emit("up", gw))
    (dx1, dv), (g_mix_post, g_ffn_pre) = _stage_bwd(_post_step, name="post_bwd", saved=[], dys=[(dy, D_MODEL), (dh2, D_MODEL)],
                                                    dxs=[F32, BF16], **post)
    gw["out"] = _matmul_tn(mixed, dv, name="g_out")
    dmixed = _matmul_nt(dv, w["out"], name="d_mixed")
    (dgate, du_hg, du_ssm), _ = _stage_bwd(_mix_step, name="mix_bwd", saved=[], dys=[(dmixed, D_MODEL)],
                                           dxs=[BF16, BF16, BF16], **mix)
    gw["in_gate"] = _matmul_tn(dgate, h1, name="g_in_gate")
    gw["branch_hg"] = _matmul_tn(y_hg, du_hg, name="g_branch_hg")
    gw["branch_ssm"] = _matmul_tn(y_ssm, du_ssm, name="g_branch_ssm")
    dy_hg = _matmul_nt(du_hg, w["branch_hg"], name="d_y_hg", dep=emit("branches", gw))
    dy_ssm = _matmul_nt(du_ssm, w["branch_ssm"], name="d_y_ssm")
    (dproj_ssm,), g_ssd = _stage_bwd(_ssd_step, name="ssd_bwd", saved=ssd_saved, dys=[(dy_ssm, SSM_GW)], dxs=[BF16], **ssd)
    gw["in_ssm"] = _matmul_tn(dproj_ssm, h1, name="g_in_ssm")
    (dproj_hg,), (g_tab, g_hg_nw) = _stage_bwd(_hg_step, name="hg_bwd", saved=hg_saved, dys=[(dy_hg, HG_DK)], dxs=[BF16],
                                               **{**hg, "nc": 2})
    gw["in_hg"] = _matmul_tn(dproj_hg, h1, name="g_in_hg")
    dh_a = _matmul_nn(dproj_hg, w["in_hg"], name="d_h1_hg", dep=emit("in", gw))
    dh_b = _matmul_nn(dproj_ssm, w["in_ssm"], name="d_h1_ssm")
    dh_c = _matmul_nn(dgate, w["in_gate"], name="d_h1_gate")

    def pre_bwd_step(carry, xv, civ, pv, cv):
        x_, da, db, dc, dres = xv
        _, vjp = jax.vjp(_rms, x_, pv[0])
        dx_, dnw = vjp(da + db + dc)
        return [], [dx_ + dres], [], [dnw]

    (grad_x,), _, (g_mix_pre,) = _scan_call(
        pre_bwd_step, name="pre_bwd", chunk=256, nc=1,
        xs=[(x, D_MODEL), (dh_a, D_MODEL), (dh_b, D_MODEL), (dh_c, D_MODEL), (dx1, D_MODEL)],
        params=[one(p["mix_pre_norm"])], ys=[(D_MODEL, F32)], accs=[(1, D_MODEL)], **row)

    gp = dict(mix_pre_norm=g_mix_pre[0], mix_post_norm=g_mix_post[0], ffn_pre_norm=g_ffn_pre[0], ffn_post_norm=g_ffn_post[0],
              hg_tab=g_tab, hg_nw=g_hg_nw, conv_w=g_ssd[0], conv_b=g_ssd[1], dt_bias=g_ssd[2], a_log=g_ssd[3],
              d_skip=g_ssd[4], ssm_nw=g_ssd[5], ffn_conv_w=g_fcw, ffn_conv_b=g_fcb, loss=loss[0, :, :1])
    return grad_x, gw, gp


def _small_to_kernel_layout(s):
    conv_idx = _conv_layout()
    pad_heads = lambda a: jnp.pad(a.reshape(SSM_GROUPS, 1, SSM_HPG), ((0, 0), (0, 0), (0, 128 - SSM_HPG)))
    return dict(
        mix_pre_norm=s["mix_pre_norm"], mix_post_norm=s["mix_post_norm"], ffn_pre_norm=s["ffn_pre_norm"],
        ffn_post_norm=s["ffn_post_norm"],
        hg_tab=s["hg_lb_table"].reshape(2, HG_HEADS, HG_DK).transpose(1, 0, 2),
        hg_nw=jnp.broadcast_to(s["hg_out_norm"].reshape(1, 1, HG_DK), (HG_HEADS, 1, HG_DK)),
        conv_w=_take_rows(s["ssm_conv_w"], conv_idx, axis=1).reshape(SSM_CONV, SSM_GROUPS, SSM_XBC).transpose(1, 0, 2),
        conv_b=_take_rows(s["ssm_conv_b"], conv_idx, axis=1).reshape(SSM_GROUPS, 1, SSM_XBC),
        dt_bias=pad_heads(s["ssm_dt_bias"]), a_log=pad_heads(s["ssm_A_log"]),
        d_skip=jnp.repeat(s["ssm_D"].reshape(SSM_HEADS), SSM_HEADDIM).reshape(SSM_GROUPS, 1, SSM_GW),
        ssm_nw=s["ssm_out_norm"].reshape(SSM_GROUPS, 1, SSM_GW),
        ffn_conv_w=s["ffn_conv_w"].reshape(FFN_CONV, FFN_G, FFN_GW).transpose(1, 0, 2),
        ffn_conv_b=s["ffn_conv_b"].reshape(FFN_G, 1, FFN_GW),
    )


def _small_from_kernel_layout(g):
    conv_inv = _inverse(_conv_layout().reshape(-1), SSM_CONV_DIM)
    heads = lambda a: a[:, 0, :SSM_HPG].reshape(1, SSM_HEADS)
    return dict(
        mix_pre_norm=g["mix_pre_norm"], mix_post_norm=g["mix_post_norm"], ffn_pre_norm=g["ffn_pre_norm"],
        ffn_post_norm=g["ffn_post_norm"],
        hg_lb_table=g["hg_tab"].transpose(1, 0, 2).reshape(2, HG_HEADS * HG_DK),
        hg_out_norm=jnp.sum(g["hg_nw"], axis=0),
        ssm_conv_w=_take_rows(g["conv_w"].transpose(1, 0, 2).reshape(SSM_CONV, -1), conv_inv, axis=1),
        ssm_conv_b=_take_rows(g["conv_b"].reshape(1, -1), conv_inv, axis=1),
        ssm_dt_bias=heads(g["dt_bias"]), ssm_A_log=heads(g["a_log"]),
        ssm_D=jnp.sum(g["d_skip"].reshape(SSM_HEADS, SSM_HEADDIM), axis=1).reshape(1, SSM_HEADS),
        ssm_out_norm=g["ssm_nw"].reshape(1, SSM_DINNER),
        ffn_conv_w=g["ffn_conv_w"].transpose(1, 0, 2).reshape(FFN_CONV, D_FF),
        ffn_conv_b=g["ffn_conv_b"].reshape(1, D_FF),
        loss=g["loss"],
    )


def kernel(x, w_in, mix_pre_norm, mix_post_norm, hg_lb_table, hg_out_norm, ssm_conv_w, ssm_conv_b, ssm_dt_bias, ssm_A_log, ssm_D, ssm_out_norm, w_branch_hg, w_branch_ssm, w_out, ffn_pre_norm, ffn_post_norm, ffn_w_up, ffn_conv_w, ffn_conv_b, ffn_w_down, loss_target, m_w_in, m_mix_pre_norm, m_mix_post_norm, m_hg_lb_table, m_hg_out_norm, m_ssm_conv_w, m_ssm_conv_b, m_ssm_dt_bias, m_ssm_A_log, m_ssm_D, m_ssm_out_norm, m_w_branch_hg, m_w_branch_ssm, m_w_out, m_ffn_pre_norm, m_ffn_post_norm, m_ffn_w_up, m_ffn_conv_w, m_ffn_conv_b, m_ffn_w_down, v_w_in, v_mix_pre_norm, v_mix_post_norm, v_hg_lb_table, v_hg_out_norm, v_ssm_conv_w, v_ssm_conv_b, v_ssm_dt_bias, v_ssm_A_log, v_ssm_D, v_ssm_out_norm, v_w_branch_hg, v_w_branch_ssm, v_w_out, v_ffn_pre_norm, v_ffn_post_norm, v_ffn_w_up, v_ffn_conv_w, v_ffn_conv_b, v_ffn_w_down):
    names = ["w_in", "mix_pre_norm", "mix_post_norm", "hg_lb_table", "hg_out_norm", "ssm_conv_w", "ssm_conv_b", "ssm_dt_bias",
             "ssm_A_log", "ssm_D", "ssm_out_norm", "w_branch_hg", "w_branch_ssm", "w_out", "ffn_pre_norm", "ffn_post_norm",
             "ffn_w_up", "ffn_conv_w", "ffn_conv_b", "ffn_w_down"]
    ws = dict(zip(names, (w_in, mix_pre_norm, mix_post_norm, hg_lb_table, hg_out_norm, ssm_conv_w, ssm_conv_b, ssm_dt_bias,
                          ssm_A_log, ssm_D, ssm_out_norm, w_branch_hg, w_branch_ssm, w_out, ffn_pre_norm, ffn_post_norm,
                          ffn_w_up, ffn_conv_w, ffn_conv_b, ffn_w_down)))
    ms = dict(zip(names, (m_w_in, m_mix_pre_norm, m_mix_post_norm, m_hg_lb_table, m_hg_out_norm, m_ssm_conv_w, m_ssm_conv_b,
                          m_ssm_dt_bias, m_ssm_A_log, m_ssm_D, m_ssm_out_norm, m_w_branch_hg, m_w_branch_ssm, m_w_out,
                          m_ffn_pre_norm, m_ffn_post_norm, m_ffn_w_up, m_ffn_conv_w, m_ffn_conv_b, m_ffn_w_down)))
    vs = dict(zip(names, (v_w_in, v_mix_pre_norm, v_mix_post_norm, v_hg_lb_table, v_hg_out_norm, v_ssm_conv_w, v_ssm_conv_b,
                          v_ssm_dt_bias, v_ssm_A_log, v_ssm_D, v_ssm_out_norm, v_w_branch_hg, v_w_branch_ssm, v_w_out,
                          v_ffn_pre_norm, v_ffn_post_norm, v_ffn_w_up, v_ffn_conv_w, v_ffn_conv_b, v_ffn_w_down)))
    me = 4 * lax.axis_index("x") + 2 * lax.axis_index("y") + lax.axis_index("c")

    late_shards = [ffn_w_up[0].T.astype(BF16), w_branch_hg[0].astype(BF16), w_branch_ssm[0].astype(BF16),
                   w_out[0].astype(BF16), ffn_w_down[0].astype(BF16)]
    landing = [lax.dynamic_update_slice_in_dim(lax.empty((N_DEV,) + s.shape, s.dtype), s[None], me, axis=0)
               for s in late_shards]
    gathered = _all_gather([w_in[0].T.astype(BF16), ssm_conv_w[0], ffn_conv_w[0]], name="gather_in_proj")
    late = _push_start([s[None] for s in late_shards], landing, name="late_weights_start", plan=_plan_own_block,
                       after=gathered[1])
    in_hg, in_ssm, in_gate = _in_proj_to_kernel(gathered[0].reshape(IN_TOTAL, D_MODEL))
    w = dict(in_hg=in_hg, in_ssm=in_ssm, in_gate=in_gate)
    small = {k: ws[k] for k, _ in _SMALL[:-1]}
    small["mix_pre_norm"] = mix_pre_norm + late[4][0, 0]
    small["ssm_conv_w"] = gathered[1].transpose(1, 0, 2).reshape(SSM_CONV, SSM_CONV_DIM)
    small["ffn_conv_w"] = gathered[2].transpose(1, 0, 2).reshape(FFN_CONV, D_FF)
    small = {k: small[k].reshape(s) for k, s in _SMALL[:-1]}

    def late_weights(after):
        landed = _push_wait(late, after, name="late_weights_wait", plan=_plan_own_block)
        up_all, bhg, bssm, out, down = _push(landed, name="late_weights_pass_on", plan=_plan_pass_on)
        return dict(up=_up_to_kernel(up_all.reshape(2 * D_FF, D_MODEL)), branch_hg=bhg.reshape(D_MODEL, D_MODEL),
                    branch_ssm=bssm.reshape(SSM_DINNER, D_MODEL), out=out.reshape(D_MODEL, D_MODEL),
                    down=down.reshape(D_FF, D_MODEL))

    in_flight = []

    def launch_direct(key, named_parts):
        ks, parts = zip(*named_parts)
        landing = [lax.dynamic_update_slice_in_dim(lax.empty(p.shape, p.dtype), lax.dynamic_slice_in_dim(p, me, 1, axis=0),
                                                   me, axis=0) for p in parts]
        handles = _push_start(list(parts), landing, name="grads_to_owners_start_" + key, plan=_plan_owners)
        in_flight.append(("grads_to_owners_wait_" + key, ks, handles, _plan_owners))
        return handles[4]

    def launch_two_level(key, named_parts):
        ks, parts = zip(*named_parts)
        from_sibling = _push(list(parts), name="grads_to_sibling_" + key, out_slots=N_CHIPS, plan=_plan_sibling)
        sums = [_pair_sum(p, r, name="pair_sum_" + k, tc=256) for k, p, r in zip(ks, parts, from_sibling)]
        handles = _push_start([q for q, _ in sums], [z for _, z in sums], name="grads_to_chips_start_" + key, plan=_plan_chips)
        in_flight.append(("grads_to_chips_wait_" + key, ks, handles, _plan_chips))
        return handles[4]

    def emit(key, gw):
        blocks = lambda a: a.reshape(N_DEV, -1, D_MODEL)
        if key == "down":
            return launch_direct(key, [("ffn_w_down", blocks(gw["down"]))])
        if key == "up":
            return launch_direct(key, [("ffn_w_up", blocks(_up_from_kernel(gw["up"])))])
        if key == "branches":
            return launch_direct(key, [("w_branch_hg", blocks(gw["branch_hg"])), ("w_branch_ssm", blocks(gw["branch_ssm"])),
                                       ("w_out", blocks(gw["out"]))])
        return launch_two_level(key, [("w_in", blocks(_in_proj_from_kernel(gw["in_hg"], gw["in_ssm"], gw["in_gate"])))])

    grad_x, gw, gp = _local_step(x[0], loss_target[0], w, _small_to_kernel_layout(small), late_weights, emit)

    big_names = ["w_in", "ffn_w_up", "w_branch_hg", "w_branch_ssm", "w_out", "ffn_w_down"]
    grads = {}
    for wait_name, ks, handles, plan in in_flight:
        landed = _push_wait(handles, grad_x, name=wait_name, plan=plan)
        for k, r in zip(ks, landed):
            g = _sum_blocks(r, name="sum_" + k, tc=256)
            grads[k] = g.T if k in ("w_in", "ffn_w_up") else g
    small_all = _push([_pack(_small_from_kernel_layout(gp))[None]], name="small_to_everyone", out_slots=N_DEV,
                      plan=_plan_everyone)
    small_g = _unpack(_sum_blocks(small_all[0], name="sum_small", tc=128))
    loss = small_g.pop("loss").reshape(())
    for k, g in small_g.items():
        if k in ("ssm_conv_w", "ffn_conv_w"):
            n = g.shape[1] // N_DEV
            g = lax.dynamic_slice_in_dim(g, me * n, n, axis=1)
        grads[k] = g

    delta, new_m, new_v = {}, {}, {}
    for k in big_names:
        delta[k], new_m[k], new_v[k] = _adamw(ws[k][0], grads[k], ms[k][0], vs[k][0], name="adamw_" + k, tr=64)
    small_names = [k for k in names if k not in big_names]
    flat = lambda d: jnp.concatenate([d[k].astype(F32).reshape(-1) for k in small_names])
    n_small = sum(int(np.prod(ws[k].shape)) for k in small_names)
    rows = 8 * (-(-n_small // 1024))
    pack2 = lambda d: jnp.pad(flat(d), (0, rows * 128 - n_small)).reshape(rows, 128)
    v_packed = jnp.pad(flat(vs), (0, rows * 128 - n_small), constant_values=1.0).reshape(rows, 128)
    packed = _adamw(pack2(ws), pack2(grads), pack2(ms), v_packed, name="adamw_small", tr=rows)
    o = 0
    for k in small_names:
        n = int(np.prod(ws[k].shape))
        delta[k], new_m[k], new_v[k] = (a.reshape(-1)[o:o + n].reshape(ws[k].shape) for a in packed)
        o += n

    full = lambda d: [d[k].reshape(ws[k].shape) for k in names]
    return (loss, grad_x[None], *full(grads), *full(delta), *full(new_m), *full(new_v))
```
